```python
import jax, jax.numpy as jnp
from jax import lax
import numpy as np

D_MODEL = 1024
BATCH = 4
SEQ = 4096
DEPTH = 4

D_MIX = D_MODEL
HEAD_DIM = 64
G_WIDTH = D_MIX // 4
R_WIDTH = 3 * D_MIX // 8
M_WIDTH = D_MIX - G_WIDTH - R_WIDTH
G_HEADS = G_WIDTH // HEAD_DIM
R_HEADS = R_WIDTH // HEAD_DIM
M_HEADS = M_WIDTH // HEAD_DIM
CHUNK = 128
W_LORA = 64
A_LORA = 64
G_LORA = 128
CONV_K = 3
G_PROJ = 2 * G_WIDTH
R_PROJ = 3 * R_WIDTH + W_LORA + A_LORA + G_LORA
M_PROJ = 4 * M_WIDTH + 4 * M_HEADS
P_IN = G_PROJ + R_PROJ + M_PROJ
N_EXPERTS = 32
TOP_K = 4
D_FF = D_MODEL
SWIGLU_LIMIT = 7.0
SWIGLU_ALPHA = 1.702
MOE_BLOCK = 256
LN_EPS = 1e-5
RWKV_GN_EPS = 64e-5
DEEPNORM_ALPHA = (2 * DEPTH) ** 0.25
DEEPNORM_BETA = (8 * DEPTH) ** -0.25

kernel_name = 'hybrid_gmlp_rwkv7_mlstm_moe_encoder'


def _head_norm(t, eps):
    t = t.astype(jnp.float32)
    mu = jnp.mean(t, -1, keepdims=True)
    var = jnp.mean(jnp.square(t - mu), -1, keepdims=True)
    return (t - mu) * lax.rsqrt(var + eps)


def _layer_norm(x, g, b):
    return (_head_norm(x, LN_EPS) * g + b).astype(x.dtype)


def _bidir(t):
    return jnp.stack([t, jnp.flip(t, axis=1)])


def _flip_bwd(t):
    return jnp.stack([t[0], jnp.flip(t[1], axis=1)])


def _merge_dirs(t):
    return t[0] + jnp.flip(t[1], axis=1)


def _centred_dwconv(x, w, b):
    r = w.shape[0] // 2
    sn = x.shape[1]
    xp = jnp.pad(x, ((0, 0), (r, r), (0, 0)))
    y = b
    for j in range(w.shape[0]):
        y = y + xp[:, j:j + sn] * w[j]
    return y


def gmlp_mixer(p, ln_g, ln_b, ws, bs):
    bn, sn, _ = p.shape
    p = jax.nn.gelu(p.astype(jnp.float32), approximate=False)
    u, v = p[..., :G_WIDTH], p[..., G_WIDTH:]
    v = _head_norm(v, LN_EPS) * ln_g + ln_b
    vc = v.reshape(bn, sn // CHUNK, CHUNK, G_HEADS, HEAD_DIM)
    y = jnp.einsum('hpq,bcqhe->bcphe', ws, vc) + jnp.transpose(bs)[None, None, :, :, None]
    return u * y.reshape(bn, sn, G_WIDTH)


def _wkv7_step(state, inp):
    r, w, k, v, a, b = inp
    sa = jnp.einsum('phvk,phk->phv', state, a)
    state = state * w[:, :, None, :] + sa[..., None] * b[:, :, None, :] + v[..., None] * k[:, :, None, :]
    return state, jnp.einsum('phvk,phk->phv', state, r)


def rwkv7_mixer(p, mu, w0, w2, a0, a2, g2, k_k, k_a, r_k, ln_g, ln_b):
    bn, sn, _ = p.shape
    pf = p.astype(jnp.float32)
    p_prev = jnp.pad(pf, ((0, 0), (1, 0), (0, 0)))[:, :-1]
    p_next = jnp.pad(pf, ((0, 0), (0, 1), (0, 0)))[:, 1:]
    pf = pf + mu[0] * (p_prev - pf) + mu[1] * (p_next - pf)
    o3 = 3 * R_WIDTH
    r, k, v, wd, ad, gd = jnp.split(pf, [R_WIDTH, 2 * R_WIDTH, o3, o3 + W_LORA, o3 + W_LORA + A_LORA], axis=-1)
    w_log = -jax.nn.softplus(-(w0[:, None, None] + jnp.einsum('bsr,drc->dbsc', jnp.tanh(wd), w2))) - 0.5
    decay = jnp.exp(-jnp.exp(w_log))
    iclr = jax.nn.sigmoid(a0[:, None, None] + jnp.einsum('bsr,drc->dbsc', ad, a2))
    gate = jax.nn.sigmoid(gd) @ g2
    hs = lambda t: t.reshape(t.shape[:-1] + (R_HEADS, HEAD_DIM))
    kk = hs(k * k_k)
    kk = kk / jnp.maximum(jnp.sqrt(jnp.sum(kk * kk, -1, keepdims=True)), 1e-12)
    k_dir = hs(k * (1.0 + (iclr - 1.0) * k_a))
    r_h, v_h = hs(r), hs(v)
    kk2 = _bidir(kk)
    tm = lambda t: jnp.moveaxis(t, 2, 0).reshape(sn, 2 * bn, R_HEADS, HEAD_DIM)
    xs = (tm(_bidir(r_h)), tm(_flip_bwd(hs(decay))), tm(_flip_bwd(k_dir)), tm(_bidir(v_h)),
          tm(-kk2), tm(kk2 * _flip_bwd(hs(iclr))))
    state0 = jnp.zeros((2 * bn, R_HEADS, HEAD_DIM, HEAD_DIM), jnp.float32)
    _, out = lax.scan(_wkv7_step, state0, xs)
    out = _merge_dirs(jnp.moveaxis(out.reshape(sn, 2, bn, R_HEADS, HEAD_DIM), 0, 2))
    out = _head_norm(out, RWKV_GN_EPS).reshape(bn, sn, R_WIDTH) * ln_g + ln_b
    bonus = jnp.sum(jnp.sum(r_h * k_dir * r_k, -1, keepdims=True), 0) * v_h
    return (out + bonus.reshape(bn, sn, R_WIDTH)) * gate


def mlstm_mixer(p, conv_w, conv_b, gate_b, ln_g):
    bn, sn, _ = p.shape
    nc = sn // CHUNK
    pf = p.astype(jnp.float32)
    qk, v, o, gates = jnp.split(pf, [2 * M_WIDTH, 3 * M_WIDTH, 4 * M_WIDTH], axis=-1)
    qk = jax.nn.silu(_centred_dwconv(qk, conv_w, conv_b))
    hs = lambda t: t.reshape(t.shape[:-1] + (M_HEADS, HEAD_DIM))
    q = hs(qk[..., :M_WIDTH])
    k = hs(qk[..., M_WIDTH:]) * HEAD_DIM ** -0.5
    v = hs(v)
    gates = (gates + gate_b).reshape(bn, sn, 2, 2, M_HEADS)
    ig = _flip_bwd(jnp.moveaxis(gates[:, :, 0], 2, 0))
    lf = _flip_bwd(jnp.moveaxis(jax.nn.log_sigmoid(gates[:, :, 1]), 2, 0))
    ch = lambda t: t.reshape(2 * bn, nc, CHUNK, M_HEADS, HEAD_DIM).transpose(1, 0, 3, 2, 4)
    chg = lambda t: t.reshape(2 * bn, nc, CHUNK, M_HEADS).transpose(1, 0, 3, 2)
    xs = (ch(_bidir(q)), ch(_bidir(k)), ch(_bidir(v)), chg(ig), chg(lf))
    tril = jnp.tril(jnp.ones((CHUNK, CHUNK), dtype=bool))

    def chunk_step(carry, inp):
        c_st, n_st, m_st = carry
        qc, kc, vc, igc, lfc = inp
        bcum = jnp.cumsum(lfc, axis=-1)
        dmat = jnp.where(tril, bcum[..., :, None] - bcum[..., None, :] + igc[..., None, :], -jnp.inf)
        inter = bcum + m_st[..., None]
        m_t = jnp.maximum(jnp.max(dmat, -1), inter)
        sc = jnp.einsum('phtd,phsd->phts', qc, kc) * jnp.exp(dmat - m_t[..., None])
        w_inter = jnp.exp(inter - m_t)
        num = jnp.einsum('phts,phsd->phtd', sc, vc) + w_inter[..., None] * jnp.einsum('phtd,phde->phte', qc, c_st)
        den = jnp.sum(sc, -1) + w_inter * jnp.einsum('phtd,phd->pht', qc, n_st)
        h = num / jnp.maximum(jnp.abs(den), jnp.exp(-m_t))[..., None]
        b_last = bcum[..., -1]
        lw = b_last[..., None] - bcum + igc
        m_new = jnp.maximum(b_last + m_st, jnp.max(lw, -1))
        wts = jnp.exp(lw - m_new[..., None])
        dec = jnp.exp(b_last + m_st - m_new)
        c_st = dec[..., None, None] * c_st + jnp.einsum('phs,phsd,phse->phde', wts, kc, vc)
        n_st = dec[..., None] * n_st + jnp.einsum('phs,phsd->phd', wts, kc)
        return (c_st, n_st, m_new), h

    carry0 = (jnp.zeros((2 * bn, M_HEADS, HEAD_DIM, HEAD_DIM), jnp.float32),
              jnp.zeros((2 * bn, M_HEADS, HEAD_DIM), jnp.float32),
              jnp.zeros((2 * bn, M_HEADS), jnp.float32))
    _, hseq = lax.scan(chunk_step, carry0, xs)
    hseq = hseq.transpose(1, 0, 3, 2, 4).reshape(2, bn, sn, M_HEADS, HEAD_DIM)
    h = _head_norm(_merge_dirs(hseq), LN_EPS).reshape(bn, sn, M_WIDTH) * ln_g
    return jax.nn.sigmoid(o) * h


def _moe(x, router_w, router_b, w1, b1, w2, b2):
    bn, sn, d = x.shape
    t = bn * sn
    xf = x.reshape(t, d)
    logits = (xf @ router_w).astype(jnp.float32) + router_b
    top_v, top_i = lax.top_k(logits, TOP_K)
    gate = jax.nn.softmax(top_v, axis=-1)
    na = t * TOP_K
    flat_e = top_i.reshape(na)
    flat_t = jnp.arange(na, dtype=jnp.int32) // TOP_K
    flat_g = gate.reshape(na)
    order = jnp.argsort(flat_e)
    se, st, sg = flat_e[order], flat_t[order], flat_g[order]
    counts = jnp.bincount(flat_e, length=N_EXPERTS)
    start = jnp.cumsum(counts) - counts
    pcounts = (counts + MOE_BLOCK - 1) // MOE_BLOCK * MOE_BLOCK
    pend = jnp.cumsum(pcounts)
    pstart = pend - pcounts
    dest = pstart[se] + jnp.arange(na, dtype=jnp.int32) - start[se]
    nb = -(-na // MOE_BLOCK) + N_EXPERTS
    row_t = jnp.zeros((nb * MOE_BLOCK,), jnp.int32).at[dest].set(st)
    row_g = jnp.zeros((nb * MOE_BLOCK,), jnp.float32).at[dest].set(sg)
    block_e = jnp.minimum(jnp.searchsorted(pend, jnp.arange(nb, dtype=jnp.int32) * MOE_BLOCK, side='right'), N_EXPERTS - 1)
    xs = xf[row_t].reshape(nb, MOE_BLOCK, d)

    def expert_block(args):
        xb, e = args
        hdn = xb @ w1[e] + b1[e]
        g_, u_ = hdn[:, :D_FF], hdn[:, D_FF:]
        g_ = jnp.minimum(g_, SWIGLU_LIMIT)
        u_ = jnp.clip(u_, -SWIGLU_LIMIT, SWIGLU_LIMIT)
        return ((u_ + 1.0) * (g_ * jax.nn.sigmoid(g_ * SWIGLU_ALPHA))) @ w2[e] + b2[e]

    ys = lax.map(expert_block, (xs, block_e)).reshape(nb * MOE_BLOCK, d)
    y = jnp.zeros((t, d), x.dtype).at[row_t].add((ys * row_g[:, None]).astype(x.dtype))
    return y.reshape(bn, sn, d)


def setup_inputs(seed: int = 0) -> dict:
    key = jax.random.key(seed)
    ks = iter(jax.random.split(key, 48))
    L = DEPTH
    nrm = lambda shape, s: jax.random.normal(next(ks), shape, jnp.float32) * s
    gain = lambda shape: 1.0 + nrm(shape, 0.1)
    x = nrm((BATCH, SEQ, D_MODEL), 1.0)
    w_in = nrm((L, D_MODEL, P_IN), D_MODEL ** -0.5)
    gmlp_ln_g = gain((L, G_WIDTH))
    gmlp_ln_b = nrm((L, G_WIDTH), 0.02)
    gmlp_ws = nrm((L, G_HEADS, CHUNK, CHUNK), CHUNK ** -0.5)
    gmlp_bs = gain((L, G_HEADS, CHUNK))
    rwkv_mu = jax.random.uniform(next(ks), (L, 2, R_PROJ), jnp.float32, 0.0, 0.5)
    rwkv_w0 = jax.random.uniform(next(ks), (L, 2, R_WIDTH), jnp.float32, -6.0, 1.0)
    rwkv_w2 = nrm((L, 2, W_LORA, R_WIDTH), 0.5 * W_LORA ** -0.5)
    rwkv_a0 = nrm((L, 2, R_WIDTH), 0.1)
    rwkv_a2 = nrm((L, 2, A_LORA, R_WIDTH), A_LORA ** -0.5)
    rwkv_g2 = nrm((L, G_LORA, R_WIDTH), G_LORA ** -0.5)
    rwkv_k_k = 0.85 + nrm((L, R_WIDTH), 0.1)
    rwkv_k_a = gain((L, R_WIDTH))
    rwkv_r_k = nrm((L, R_HEADS, HEAD_DIM), 0.1)
    rwkv_ln_g = gain((L, R_WIDTH))
    rwkv_ln_b = nrm((L, R_WIDTH), 0.02)
    mlstm_conv_w = nrm((L, CONV_K, 2 * M_WIDTH), CONV_K ** -0.5)
    mlstm_conv_b = nrm((L, 2 * M_WIDTH), 0.02)
    ib = nrm((L, 2, M_HEADS), 0.1)
    fb = jnp.linspace(3.0, 6.0, M_HEADS, dtype=jnp.float32) + nrm((L, 2, M_HEADS), 0.1)
    mlstm_gate_b = jnp.stack([ib, fb], axis=1).reshape(L, 4 * M_HEADS)
    mlstm_ln_g = gain((L, M_WIDTH))
    w_out = nrm((L, D_MIX, D_MODEL), DEEPNORM_BETA * D_MIX ** -0.5)
    ln1_g = gain((L, D_MODEL))
    ln1_b = nrm((L, D_MODEL), 0.02)
    router_w = nrm((L, D_MODEL, N_EXPERTS), D_MODEL ** -0.5)
    router_b = nrm((L, N_EXPERTS), 0.01)
    exp_w1 = nrm((L, N_EXPERTS, D_MODEL, 2 * D_FF), D_MODEL ** -0.5)
    exp_b1 = nrm((L, N_EXPERTS, 2 * D_FF), 0.02)
    exp_w2 = nrm((L, N_EXPERTS, D_FF, D_MODEL), DEEPNORM_BETA * D_FF ** -0.5)
    exp_b2 = nrm((L, N_EXPERTS, D_MODEL), 0.02)
    ln2_g = gain((L, D_MODEL))
    ln2_b = nrm((L, D_MODEL), 0.02)
    return {'x': x, 'w_in': w_in, 'gmlp_ln_g': gmlp_ln_g, 'gmlp_ln_b': gmlp_ln_b, 'gmlp_ws': gmlp_ws,
            'gmlp_bs': gmlp_bs, 'rwkv_mu': rwkv_mu, 'rwkv_w0': rwkv_w0, 'rwkv_w2': rwkv_w2,
            'rwkv_a0': rwkv_a0, 'rwkv_a2': rwkv_a2, 'rwkv_g2': rwkv_g2, 'rwkv_k_k': rwkv_k_k,
            'rwkv_k_a': rwkv_k_a, 'rwkv_r_k': rwkv_r_k, 'rwkv_ln_g': rwkv_ln_g, 'rwkv_ln_b': rwkv_ln_b,
            'mlstm_conv_w': mlstm_conv_w, 'mlstm_conv_b': mlstm_conv_b, 'mlstm_gate_b': mlstm_gate_b,
            'mlstm_ln_g': mlstm_ln_g, 'w_out': w_out, 'ln1_g': ln1_g, 'ln1_b': ln1_b,
            'router_w': router_w, 'router_b': router_b, 'exp_w1': exp_w1, 'exp_b1': exp_b1,
            'exp_w2': exp_w2, 'exp_b2': exp_b2, 'ln2_g': ln2_g, 'ln2_b': ln2_b}


def reference(x, w_in, gmlp_ln_g, gmlp_ln_b, gmlp_ws, gmlp_bs, rwkv_mu, rwkv_w0, rwkv_w2, rwkv_a0,
              rwkv_a2, rwkv_g2, rwkv_k_k, rwkv_k_a, rwkv_r_k, rwkv_ln_g, rwkv_ln_b, mlstm_conv_w,
              mlstm_conv_b, mlstm_gate_b, mlstm_ln_g, w_out, ln1_g, ln1_b, router_w, router_b,
              exp_w1, exp_b1, exp_w2, exp_b2, ln2_g, ln2_b):
    for l in range(DEPTH):
        proj = jnp.einsum('bsd,dp->bsp', x, w_in[l])
        pg, pr, pm = jnp.split(proj, [G_PROJ, G_PROJ + R_PROJ], axis=-1)
        y_g = gmlp_mixer(pg, gmlp_ln_g[l], gmlp_ln_b[l], gmlp_ws[l], gmlp_bs[l])
        y_r = rwkv7_mixer(pr, rwkv_mu[l], rwkv_w0[l], rwkv_w2[l], rwkv_a0[l], rwkv_a2[l], rwkv_g2[l],
                          rwkv_k_k[l], rwkv_k_a[l], rwkv_r_k[l], rwkv_ln_g[l], rwkv_ln_b[l])
        y_m = mlstm_mixer(pm, mlstm_conv_w[l], mlstm_conv_b[l], mlstm_gate_b[l], mlstm_ln_g[l])
        groups = jnp.concatenate([y_g.astype(x.dtype), y_r.astype(x.dtype), y_m.astype(x.dtype)], axis=-1)
        mix = jnp.einsum('bsc,cd->bsd', groups, w_out[l])
        x = _layer_norm(DEEPNORM_ALPHA * x + mix, ln1_g[l], ln1_b[l])
        ffn = _moe(x, router_w[l], router_b[l], exp_w1[l], exp_b1[l], exp_w2[l], exp_b2[l])
        x = _layer_norm(DEEPNORM_ALPHA * x + ffn, ln2_g[l], ln2_b[l])
    return x
```

```python
import functools
import math

import jax
import jax.numpy as jnp
from jax import lax
from jax.experimental import pallas as pl
from jax.experimental.pallas import tpu as pltpu

F32 = jnp.float32
BF16 = jnp.bfloat16
HI = lax.Precision.HIGHEST

HEAD_DIM = 64
GMLP_CHUNK = 128
MLSTM_CHUNK = 128
RWKV_CHUNK = 64
W_LORA = 64
A_LORA = 64
G_LORA = 128
N_EXPERTS = 32
TOP_K = 4
MOE_BLOCK = 256
SWIGLU_LIMIT = 7.0
SWIGLU_ALPHA = 1.702
LN_EPS = 1e-5
RWKV_GN_EPS = 64e-5
LANE = 128
SUBLANE = 8
VMEM_LIMIT = 48 * 1024 * 1024
NEG_BIG = -1e30


def _cparams(n_axes):
    return pltpu.CompilerParams(dimension_semantics=("arbitrary",) * n_axes,
                                vmem_limit_bytes=VMEM_LIMIT)


def _full(shape):
    return pl.BlockSpec(shape, lambda *_: (0,) * len(shape))


def _dot(a, b, precision=None):
    return jnp.dot(a, b, preferred_element_type=F32, precision=precision)


def _dot_nt(a, b, precision=None):
    return lax.dot_general(a, b, (((1,), (1,)), ((), ())), preferred_element_type=F32, precision=precision)


def _dot_tn(a, b, precision=None):
    return lax.dot_general(a, b, (((0,), (0,)), ((), ())), preferred_element_type=F32, precision=precision)


def _sigmoid(x):
    return 1.0 / (1.0 + jnp.exp(-x))


def _softplus(x):
    return jnp.maximum(x, 0.0) + jnp.log1p(jnp.exp(-jnp.abs(x)))


def _block_diag_ones(width):
    h = jnp.arange(width) // HEAD_DIM
    return (h[:, None] == h[None, :]).astype(F32)


def _proj_body(x_ref, wg_ref, wr_ref, wm_ref, pg_ref, pr_ref, pm_ref):
    xb = x_ref[...].astype(BF16)
    pg_ref[...] = _dot(xb, wg_ref[...])
    pr_ref[...] = _dot(xb, wr_ref[...])
    pm_ref[...] = _dot(xb, wm_ref[...])


def _proj(x, wg, wr, wm, tm=256):
    t, d = x.shape
    ng, nr, nm = wg.shape[1], wr.shape[1], wm.shape[1]
    row = lambda n: pl.BlockSpec((tm, n), lambda i: (i, 0))
    return pl.pallas_call(
        _proj_body,
        grid=(t // tm,),
        in_specs=[row(d), _full((d, ng)), _full((d, nr)), _full((d, nm))],
        out_specs=[row(ng), row(nr), row(nm)],
        out_shape=[jax.ShapeDtypeStruct((t, n), F32) for n in (ng, nr, nm)],
        compiler_params=_cparams(1),
        name="in_proj",
    )(x, wg, wr, wm)


def _gmlp_body(pg_ref, lng_ref, lnb_ref, ws_ref, bst_ref, o_ref, *, gw, chunks):
    p = pg_ref[...]
    p = 0.5 * p * (1.0 + lax.erf(p * math.sqrt(0.5)))
    u, v = p[:, :gw], p[:, gw:]
    mu = jnp.mean(v, axis=-1, keepdims=True)
    vc = v - mu
    var = jnp.mean(vc * vc, axis=-1, keepdims=True)
    vn = vc * lax.rsqrt(var + LN_EPS) * lng_ref[...] + lnb_ref[...]
    n_heads = gw // HEAD_DIM
    for c in range(chunks):
        rows = slice(c * GMLP_CHUNK, (c + 1) * GMLP_CHUNK)
        ys = []
        for h in range(n_heads):
            cols = slice(h * HEAD_DIM, (h + 1) * HEAD_DIM)
            y = _dot(ws_ref[h], vn[rows, cols].astype(BF16)) + bst_ref[:, h:h + 1]
            ys.append(y)
        o_ref[rows, :] = u[rows, :] * jnp.concatenate(ys, axis=1)


def _gmlp(pg, ln_g, ln_b, ws, bs, chunks=4):
    t = pg.shape[0]
    gw = pg.shape[1] // 2
    n_heads = gw // HEAD_DIM
    tm = chunks * GMLP_CHUNK
    bst = jnp.zeros((GMLP_CHUNK, LANE), F32).at[:, :n_heads].set(bs.T)
    return pl.pallas_call(
        functools.partial(_gmlp_body, gw=gw, chunks=chunks),
        grid=(t // tm,),
        in_specs=[pl.BlockSpec((tm, 2 * gw), lambda i: (i, 0)), _full((1, gw)), _full((1, gw)),
                  _full((n_heads, GMLP_CHUNK, GMLP_CHUNK)), _full((GMLP_CHUNK, LANE))],
        out_specs=pl.BlockSpec((tm, gw), lambda i: (i, 0)),
        out_shape=jax.ShapeDtypeStruct((t, gw), F32),
        compiler_params=_cparams(1),
        name="gmlp",
    )(pg, ln_g.reshape(1, gw), ln_b.reshape(1, gw), ws.astype(BF16), bst)


def _halo_specs(tm, width, n_rows):
    per8 = tm // SUBLANE
    last = n_rows // SUBLANE - 1
    prev = pl.BlockSpec((SUBLANE, width), lambda i: (jnp.maximum(i * per8 - 1, 0), 0))
    nxt = pl.BlockSpec((SUBLANE, width), lambda i: (jnp.minimum((i + 1) * per8, last), 0))
    return prev, nxt


def _neighbours(cur, prev_blk, next_blk, tiles_per_seq):
    tm = cur.shape[0]
    j = pl.program_id(0) % tiles_per_seq
    prev_row = jnp.where(j > 0, prev_blk[SUBLANE - 1:SUBLANE, :], 0.0)
    next_row = jnp.where(j < tiles_per_seq - 1, next_blk[0:1, :], 0.0)
    ridx = lax.broadcasted_iota(jnp.int32, cur.shape, 0)
    before = jnp.where(ridx == 0, prev_row, pltpu.roll(cur, 1, 0))
    after = jnp.where(ridx == tm - 1, next_row, pltpu.roll(cur, tm - 1, 0))
    return before, after


def _rwkv_prep_body(pr_ref, prev_ref, next_ref, mu_ref, w0_ref, w2_ref, a0_ref, a2_ref, g2_ref,
                    kk_ref, ka_ref, rk_ref, bd_ref,
                    r_out, v_out, a_out, kd_out, b_out, lw_out, bonus_out, gate_out, *, rw, tiles_per_seq):
    pf = pr_ref[...]
    before, after = _neighbours(pf, prev_ref[...], next_ref[...], tiles_per_seq)
    pf = pf + mu_ref[0:1, :] * (before - pf) + mu_ref[1:2, :] * (after - pf)
    o3 = 3 * rw
    r, k, v = pf[:, :rw], pf[:, rw:2 * rw], pf[:, 2 * rw:o3]
    wd = pf[:, o3:o3 + W_LORA]
    ad = pf[:, o3 + W_LORA:o3 + W_LORA + A_LORA]
    gd = pf[:, o3 + W_LORA + A_LORA:]
    bd = bd_ref[...]
    kk = k * kk_ref[...]
    ss = _dot(kk * kk, bd, HI)
    kk = kk / jnp.maximum(jnp.sqrt(ss), 1e-12)
    twd = jnp.tanh(wd)
    ksum = jnp.zeros_like(k)
    for d in range(2):
        w_log = -_softplus(-(w0_ref[d:d + 1, :] + _dot(twd, w2_ref[d], HI))) - 0.5
        lw_out[d] = -jnp.exp(w_log)
        iclr = _sigmoid(a0_ref[d:d + 1, :] + _dot(ad, a2_ref[d], HI))
        kd = k * (1.0 + (iclr - 1.0) * ka_ref[...])
        kd_out[d] = kd
        b_out[d] = kk * iclr
        ksum = ksum + kd
    r_out[...] = r
    v_out[...] = v
    a_out[...] = -kk
    bonus_out[...] = _dot(r * ksum * rk_ref[...], bd, HI) * v
    gate_out[...] = _dot(_sigmoid(gd).astype(BF16), g2_ref[...])


def _rwkv_prep(pr, seq, mu, w0, w2, a0, a2, g2, k_k, k_a, r_k, tm=256):
    t, rproj = pr.shape
    rw = w0.shape[1]
    tiles_per_seq = seq // tm
    prev, nxt = _halo_specs(tm, rproj, t)
    row = pl.BlockSpec((tm, rw), lambda i: (i, 0))
    row2 = pl.BlockSpec((2, tm, rw), lambda i: (0, i, 0))
    one = jax.ShapeDtypeStruct((t, rw), F32)
    two = jax.ShapeDtypeStruct((2, t, rw), F32)
    return pl.pallas_call(
        functools.partial(_rwkv_prep_body, rw=rw, tiles_per_seq=tiles_per_seq),
        grid=(t // tm,),
        in_specs=[pl.BlockSpec((tm, rproj), lambda i: (i, 0)), prev, nxt,
                  _full((2, rproj)), _full((2, rw)), _full((2, W_LORA, rw)), _full((2, rw)),
                  _full((2, A_LORA, rw)), _full((G_LORA, rw)), _full((1, rw)), _full((1, rw)),
                  _full((1, rw)), _full((rw, rw))],
        out_specs=[row, row, row, row2, row2, row2, row, row],
        out_shape=[one, one, one, two, two, two, one, one],
        compiler_params=_cparams(1),
        name="rwkv_prep",
    )(pr, pr, pr, mu, w0, w2, a0, a2, g2.astype(BF16), k_k.reshape(1, rw), k_a.reshape(1, rw),
      r_k.reshape(1, rw), _block_diag_ones(rw))


def _rwkv_scan_body(r_ref, v_ref, a_ref, kd_ref, b_ref, lw_ref, o_ref, h_ref, *, n_heads):
    L = RWKV_CHUNK
    d = pl.program_id(0)
    c = pl.program_id(2)

    @pl.when(c == 0)
    def _():
        h_ref[...] = jnp.zeros_like(h_ref)

    row = lax.broadcasted_iota(jnp.int32, (L, L), 0)
    col = lax.broadcasted_iota(jnp.int32, (L, L), 1)
    fwd = d == 0
    rel = (col - row) * (1 - 2 * d)
    incl = rel <= 0
    strict = rel < 0
    eye = (row == col).astype(F32)
    lw = lw_ref[...]
    cum = _dot(incl.astype(F32), lw, HI)
    tot = jnp.where(fwd, cum[L - 1:L, :], cum[0:1, :])
    e_in = jnp.exp(cum)
    e_ex = jnp.exp(cum - lw)
    e_neg = jnp.exp(-cum)
    e_end = jnp.exp(tot - cum)
    e_tot = jnp.exp(tot)
    r, v, a, kd, b = r_ref[...], v_ref[...], a_ref[...], kd_ref[...], b_ref[...]
    at, rt, bt, kt = a * e_ex, r * e_in, b * e_neg, kd * e_neg
    kend, bend = kd * e_end, b * e_end
    outs = []
    for h in range(n_heads):
        sl = slice(h * HEAD_DIM, (h + 1) * HEAD_DIM)
        g = _dot_nt(jnp.concatenate([at[:, sl], rt[:, sl]], axis=0),
                    jnp.concatenate([bt[:, sl], kt[:, sl]], axis=0), HI)
        a_ab = jnp.where(strict, g[:L, :L], 0.0)
        a_ak = jnp.where(strict, g[:L, L:], 0.0)
        m_rb = jnp.where(incl, g[L:, :L], 0.0)
        m_rk = jnp.where(incl, g[L:, L:], 0.0)
        inv = eye + a_ab
        pw = a_ab
        for _ in range(int(math.log2(L)) - 1):
            pw = _dot(pw, pw, HI)
            inv = inv + _dot(inv, pw, HI)
        h0 = h_ref[h]
        vh = v[:, sl]
        u = _dot(inv, _dot(at[:, sl], h0, HI) + _dot(a_ak, vh, HI), HI)
        outs.append(_dot(rt[:, sl], h0, HI) + _dot(m_rb, u, HI) + _dot(m_rk, vh, HI))
        upd = _dot_tn(jnp.concatenate([kend[:, sl], bend[:, sl]], axis=0),
                      jnp.concatenate([vh, u], axis=0), HI)
        h_ref[h] = _dot(eye * e_tot[:, sl], h0, HI) + upd
    o_ref[...] = jnp.concatenate(outs, axis=1)


def _rwkv_scan(r, v, a, kd, b, lw, batch, seq):
    t, rw = r.shape
    n_heads = rw // HEAD_DIM
    L = RWKV_CHUNK
    nc = seq // L
    blk = lambda d, bi, c: bi * nc + jnp.where(d == 0, c, nc - 1 - c)
    one = pl.BlockSpec((L, rw), lambda d, bi, c: (blk(d, bi, c), 0))
    two = pl.BlockSpec((None, L, rw), lambda d, bi, c: (d, blk(d, bi, c), 0))
    return pl.pallas_call(
        functools.partial(_rwkv_scan_body, n_heads=n_heads),
        grid=(2, batch, nc),
        in_specs=[one, one, one, two, two, two],
        out_specs=two,
        out_shape=jax.ShapeDtypeStruct((2, t, rw), F32),
        scratch_shapes=[pltpu.VMEM((n_heads, HEAD_DIM, HEAD_DIM), F32)],
        compiler_params=_cparams(3),
        name="rwkv_scan",
    )(r, v, a, kd, b, lw)


def _mlstm_prep_body(qk_ref, prev_ref, next_ref, g_ref, cw_ref, cb_ref, gb_ref, q_out, k_out, gate_out,
                     *, mw, n_heads, tiles_per_seq):
    x = qk_ref[...]
    before, after = _neighbours(x, prev_ref[...], next_ref[...], tiles_per_seq)
    y = cb_ref[...] + before * cw_ref[0:1, :] + x * cw_ref[1:2, :] + after * cw_ref[2:3, :]
    y = y * _sigmoid(y)
    q_out[...] = y[:, :mw]
    k_out[...] = y[:, mw:] * (HEAD_DIM ** -0.5)
    g = g_ref[...] + gb_ref[...]
    lane = lax.broadcasted_iota(jnp.int32, g.shape, 1)
    for d in range(2):
        ig = g if d == 0 else pltpu.roll(g, LANE - n_heads, 1)
        fg = pltpu.roll(g, LANE - (1 + d) * n_heads, 1)
        lf = -_softplus(-fg)
        gate_out[d] = jnp.where(lane < n_heads, ig, jnp.where(lane < 2 * n_heads, lf, 0.0))


def _mlstm_prep(pm, seq, conv_w, conv_b, gate_b, mw, tm=256):
    t = pm.shape[0]
    n_heads = mw // HEAD_DIM
    tiles_per_seq = seq // tm
    w2 = 2 * mw
    prev, nxt = _halo_specs(tm, w2, t)
    gcol = (4 * mw) // LANE
    gb = jnp.zeros((1, LANE), F32).at[0, :4 * n_heads].set(gate_b)
    row = pl.BlockSpec((tm, mw), lambda i: (i, 0))
    return pl.pallas_call(
        functools.partial(_mlstm_prep_body, mw=mw, n_heads=n_heads, tiles_per_seq=tiles_per_seq),
        grid=(t // tm,),
        in_specs=[pl.BlockSpec((tm, w2), lambda i: (i, 0)), prev, nxt,
                  pl.BlockSpec((tm, LANE), lambda i: (i, gcol)),
                  _full((3, w2)), _full((1, w2)), _full((1, LANE))],
        out_specs=[row, row, pl.BlockSpec((2, tm, LANE), lambda i: (0, i, 0))],
        out_shape=[jax.ShapeDtypeStruct((t, mw), F32), jax.ShapeDtypeStruct((t, mw), F32),
                   jax.ShapeDtypeStruct((2, t, LANE), F32)],
        compiler_params=_cparams(1),
        name="mlstm_prep",
    )(pm, pm, pm, pm, conv_w, conv_b.reshape(1, w2), gb)


def _mlstm_scan_body(q_ref, k_ref, v_ref, g_ref, o_ref, c_ref, m_ref, *, n_heads):
    L = MLSTM_CHUNK
    d = pl.program_id(0)
    c = pl.program_id(2)

    @pl.when(c == 0)
    def _():
        c_ref[...] = jnp.zeros_like(c_ref)
        m_ref[...] = jnp.zeros_like(m_ref)

    row = lax.broadcasted_iota(jnp.int32, (L, L), 0)
    col = lax.broadcasted_iota(jnp.int32, (L, L), 1)
    fwd = d == 0
    incl = (col - row) * (1 - 2 * d) <= 0
    g = g_ref[...]
    bcum = _dot(incl.astype(F32), g, HI)
    g_t = g.T
    bcum_t = bcum.T
    lane64 = lax.broadcasted_iota(jnp.int32, (L, HEAD_DIM), 1)
    ones_col = (lane64 == 0).astype(F32)
    q, k, v = q_ref[...], k_ref[...], v_ref[...]
    outs = []
    for h in range(n_heads):
        sl = slice(h * HEAD_DIM, (h + 1) * HEAD_DIM)
        bc = bcum[:, n_heads + h:n_heads + h + 1]
        br = bcum_t[n_heads + h:n_heads + h + 1, :]
        igr = g_t[h:h + 1, :]
        igc = g[:, h:h + 1]
        m_st = m_ref[h:h + 1, 0:1]
        dm = jnp.where(incl, bc - br + igr, -jnp.inf)
        inter = bc + m_st
        m_t = jnp.maximum(jnp.max(dm, axis=1, keepdims=True), inter)
        qh = q[:, sl].astype(BF16)
        kh = k[:, sl]
        vext = jnp.concatenate([v[:, sl], ones_col], axis=1).astype(BF16)
        sc = _dot_nt(qh, kh.astype(BF16)) * jnp.exp(dm - m_t)
        w_inter = jnp.exp(inter - m_t)
        cst = c_ref[h]
        numext = _dot(sc.astype(BF16), vext) + w_inter * _dot(qh, cst.astype(BF16))
        num = numext[:, :HEAD_DIM]
        den = numext[:, HEAD_DIM:HEAD_DIM + 1]
        outs.append(num / jnp.maximum(jnp.abs(den), jnp.exp(-m_t)))
        b_last = jnp.where(fwd, bc[L - 1:L, :], bc[0:1, :])
        lwc = b_last - bc + igc
        m_new = jnp.maximum(b_last + m_st, jnp.max(lwc, axis=0, keepdims=True))
        wts = jnp.exp(lwc - m_new)
        dec = jnp.exp(b_last + m_st - m_new)
        c_ref[h] = dec * cst + _dot_tn((wts * kh).astype(BF16), vext)
        m_ref[h:h + 1, :] = jnp.broadcast_to(m_new, (1, LANE))
    o_ref[...] = jnp.concatenate(outs, axis=1)


def _mlstm_scan(q, k, pm, gates, batch, seq):
    t, mw = q.shape
    n_heads = mw // HEAD_DIM
    L = MLSTM_CHUNK
    nc = seq // L
    blk = lambda d, bi, c: bi * nc + jnp.where(d == 0, c, nc - 1 - c)
    one = pl.BlockSpec((L, mw), lambda d, bi, c: (blk(d, bi, c), 0))
    vspec = pl.BlockSpec((L, mw), lambda d, bi, c: (blk(d, bi, c), 2))
    return pl.pallas_call(
        functools.partial(_mlstm_scan_body, n_heads=n_heads),
        grid=(2, batch, nc),
        in_specs=[one, one, vspec, pl.BlockSpec((None, L, LANE), lambda d, bi, c: (d, blk(d, bi, c), 0))],
        out_specs=pl.BlockSpec((None, L, mw), lambda d, bi, c: (d, blk(d, bi, c), 0)),
        out_shape=jax.ShapeDtypeStruct((2, t, mw), F32),
        scratch_shapes=[pltpu.VMEM((n_heads, HEAD_DIM, LANE), F32), pltpu.VMEM((SUBLANE, LANE), F32)],
        compiler_params=_cparams(3),
        name="mlstm_scan",
    )(q, k, pm, gates)


def _layer_norm(x, g, b):
    mu = jnp.mean(x, axis=-1, keepdims=True)
    xc = x - mu
    var = jnp.mean(xc * xc, axis=-1, keepdims=True)
    return xc * lax.rsqrt(var + LN_EPS) * g + b


def _head_norm(x, bd_mean, eps):
    mu = _dot(x, bd_mean, HI)
    xc = x - mu
    var = _dot(xc * xc, bd_mean, HI)
    return xc * lax.rsqrt(var + eps)


def _mix_out_body(x_ref, yg_ref, ro_ref, bonus_ref, rgate_ref, rlg_ref, rlb_ref, mh_ref, og_ref, mlg_ref,
                  wg_ref, wr_ref, wm_ref, l1g_ref, l1b_ref, rw_ref, rb_ref, bdm_ref,
                  x1_out, x1b_out, topi_out, gate_out, *, alpha):
    bdm = bdm_ref[...]
    yr = _head_norm(ro_ref[0] + ro_ref[1], bdm, RWKV_GN_EPS) * rlg_ref[...] + rlb_ref[...]
    yr = (yr + bonus_ref[...]) * rgate_ref[...]
    ym = _sigmoid(og_ref[...]) * (_head_norm(mh_ref[0] + mh_ref[1], bdm, LN_EPS) * mlg_ref[...])
    mix = (_dot(yg_ref[...].astype(BF16), wg_ref[...]) + _dot(yr.astype(BF16), wr_ref[...])
           + _dot(ym.astype(BF16), wm_ref[...]))
    x1 = _layer_norm(alpha * x_ref[...] + mix, l1g_ref[...], l1b_ref[...])
    x1_out[...] = x1
    x1b_out[...] = x1.astype(BF16)
    lg = _dot(x1, rw_ref[...], HI) + rb_ref[...]
    lane = lax.broadcasted_iota(jnp.int32, lg.shape, 1)
    vals, topi = [], jnp.zeros(lg.shape, jnp.int32)
    for j in range(TOP_K):
        mx = jnp.max(lg, axis=1, keepdims=True)
        idx = jnp.min(jnp.where(lg == mx, lane, LANE), axis=1, keepdims=True)
        vals.append(mx)
        topi = jnp.where(lane == j, idx, topi)
        lg = jnp.where(lane == idx, -jnp.inf, lg)
    es = [jnp.exp(vj - vals[0]) for vj in vals]
    den = es[0] + es[1] + es[2] + es[3]
    gate = jnp.zeros(lg.shape, F32)
    for j in range(TOP_K):
        gate = jnp.where(lane == j, es[j] / den, gate)
    topi_out[...] = topi
    gate_out[...] = gate


def _mix_out(x, yg, ro, bonus, rgate, rlg, rlb, mh, pm, mlg, w_out, l1g, l1b, router_w, router_b, alpha, tm=256):
    t, dm = x.shape
    gw, rw, mw = yg.shape[1], bonus.shape[1], mh.shape[2]
    assert rw == mw
    wb = w_out.astype(BF16)
    rwp = jnp.zeros((dm, LANE), F32).at[:, :N_EXPERTS].set(router_w)
    rbp = jnp.full((1, LANE), NEG_BIG, F32).at[0, :N_EXPERTS].set(router_b)
    row = lambda n: pl.BlockSpec((tm, n), lambda i: (i, 0))
    row2 = lambda n: pl.BlockSpec((2, tm, n), lambda i: (0, i, 0))
    vec = lambda n: _full((1, n))
    return pl.pallas_call(
        functools.partial(_mix_out_body, alpha=alpha),
        grid=(t // tm,),
        in_specs=[row(dm), row(gw), row2(rw), row(rw), row(rw), vec(rw), vec(rw), row2(mw),
                  pl.BlockSpec((tm, mw), lambda i: (i, 3)),
                  vec(mw), _full((gw, dm)), _full((rw, dm)), _full((mw, dm)), vec(dm), vec(dm),
                  _full((dm, LANE)), vec(LANE), _full((rw, rw))],
        out_specs=[row(dm), row(dm), row(LANE), row(LANE)],
        out_shape=[jax.ShapeDtypeStruct((t, dm), F32), jax.ShapeDtypeStruct((t, dm), BF16),
                   jax.ShapeDtypeStruct((t, LANE), jnp.int32), jax.ShapeDtypeStruct((t, LANE), F32)],
        compiler_params=_cparams(1),
        name="mix_out",
    )(x, yg, ro, bonus, rgate, rlg.reshape(1, rw), rlb.reshape(1, rw), mh, pm, mlg.reshape(1, mw),
      wb[:gw], wb[gw:gw + rw], wb[gw + rw:], l1g.reshape(1, dm), l1b.reshape(1, dm), rwp, rbp,
      _block_diag_ones(rw) / HEAD_DIM)


def _moe_body(be_ref, nu_ref, xs_ref, w1_ref, b1_ref, w2_ref, b2_ref, o_ref, *, dff):
    i = pl.program_id(0)

    @pl.when(i < nu_ref[0])
    def _():
        hdn = _dot(xs_ref[...], w1_ref[...]) + b1_ref[...]
        g_ = jnp.minimum(hdn[:, :dff], SWIGLU_LIMIT)
        u_ = jnp.clip(hdn[:, dff:], -SWIGLU_LIMIT, SWIGLU_LIMIT)
        act = (u_ + 1.0) * (g_ * _sigmoid(g_ * SWIGLU_ALPHA))
        o_ref[...] = _dot(act.astype(BF16), w2_ref[...]) + b2_ref[...]

    @pl.when(i >= nu_ref[0])
    def _():
        o_ref[...] = jnp.zeros_like(o_ref)


def _moe_experts(xs, block_e, n_used, w1, b1, w2, b2):
    rows, dm = xs.shape
    nb = rows // MOE_BLOCK
    ne, _, dff2 = w1.shape
    dff = dff2 // 2
    grid_spec = pltpu.PrefetchScalarGridSpec(
        num_scalar_prefetch=2,
        grid=(nb,),
        in_specs=[pl.BlockSpec((MOE_BLOCK, dm), lambda i, be, nu: (i, 0)),
                  pl.BlockSpec((None, dm, dff2), lambda i, be, nu: (be[i], 0, 0)),
                  pl.BlockSpec((None, 1, dff2), lambda i, be, nu: (be[i], 0, 0)),
                  pl.BlockSpec((None, dff, dm), lambda i, be, nu: (be[i], 0, 0)),
                  pl.BlockSpec((None, 1, dm), lambda i, be, nu: (be[i], 0, 0))],
        out_specs=pl.BlockSpec((MOE_BLOCK, dm), lambda i, be, nu: (i, 0)),
    )
    return pl.pallas_call(
        functools.partial(_moe_body, dff=dff),
        grid_spec=grid_spec,
        out_shape=jax.ShapeDtypeStruct((rows, dm), F32),
        compiler_params=_cparams(1),
        name="moe_experts",
    )(block_e, n_used, xs, w1, b1.reshape(ne, 1, dff2), w2, b2.reshape(ne, 1, dm))


def _moe_plan(topi, n_tokens):
    na = n_tokens * TOP_K
    flat_e = topi[:, :TOP_K].reshape(na)
    onehot = (flat_e[:, None] == jnp.arange(N_EXPERTS, dtype=jnp.int32)[None, :]).astype(jnp.int32)
    csum = jnp.cumsum(onehot, axis=0)
    counts = csum[-1]
    rank = jnp.sum((csum - onehot) * onehot, axis=1)
    pcounts = (counts + MOE_BLOCK - 1) // MOE_BLOCK * MOE_BLOCK
    pend = jnp.cumsum(pcounts)
    pstart = pend - pcounts
    dest = (pstart[flat_e] + rank).astype(jnp.int32)
    nb = -(-na // MOE_BLOCK) + N_EXPERTS
    flat_t = jnp.arange(na, dtype=jnp.int32) // TOP_K
    row_t = jnp.zeros((nb * MOE_BLOCK,), jnp.int32).at[dest].set(flat_t)
    block_e = jnp.minimum(jnp.searchsorted(pend, jnp.arange(nb, dtype=jnp.int32) * MOE_BLOCK, side='right'),
                          N_EXPERTS - 1).astype(jnp.int32)
    n_used = (pend[-1] // MOE_BLOCK).astype(jnp.int32).reshape(1)
    return dest, row_t, block_e, n_used


def _combine_body(x1_ref, yg_ref, gate_ref, g_ref, b_ref, o_ref, *, alpha, dm):
    gate = gate_ref[...]
    ffn = gate[:, 0:1] * yg_ref[:, 0:dm]
    for j in range(1, TOP_K):
        ffn = ffn + gate[:, j:j + 1] * yg_ref[:, j * dm:(j + 1) * dm]
    o_ref[...] = _layer_norm(alpha * x1_ref[...] + ffn, g_ref[...], b_ref[...])


def _combine(x1, yg, gate, ln_g, ln_b, alpha, tm=256):
    t, dm = x1.shape
    return pl.pallas_call(
        functools.partial(_combine_body, alpha=alpha, dm=dm),
        grid=(t // tm,),
        in_specs=[pl.BlockSpec((tm, dm), lambda i: (i, 0)), pl.BlockSpec((tm, TOP_K * dm), lambda i: (i, 0)),
                  pl.BlockSpec((tm, LANE), lambda i: (i, 0)), _full((1, dm)), _full((1, dm))],
        out_specs=pl.BlockSpec((tm, dm), lambda i: (i, 0)),
        out_shape=jax.ShapeDtypeStruct((t, dm), F32),
        compiler_params=_cparams(1),
        name="combine_ln",
    )(x1, yg, gate, ln_g.reshape(1, dm), ln_b.reshape(1, dm))


def _pad_cols(w, width):
    return jnp.pad(w, ((0, 0), (0, width - w.shape[1])))


def kernel(x, w_in, gmlp_ln_g, gmlp_ln_b, gmlp_ws, gmlp_bs, rwkv_mu, rwkv_w0, rwkv_w2, rwkv_a0, rwkv_a2, rwkv_g2, rwkv_k_k, rwkv_k_a, rwkv_r_k, rwkv_ln_g, rwkv_ln_b, mlstm_conv_w, mlstm_conv_b, mlstm_gate_b, mlstm_ln_g, w_out, ln1_g, ln1_b, router_w, router_b, exp_w1, exp_b1, exp_w2, exp_b2, ln2_g, ln2_b):
    batch, seq, dm = x.shape
    depth = w_in.shape[0]
    t = batch * seq
    gw = gmlp_ln_g.shape[1]
    rw = rwkv_w0.shape[2]
    mw = mlstm_ln_g.shape[1]
    g_proj = 2 * gw
    r_proj = 3 * rw + W_LORA + A_LORA + G_LORA
    m_proj = w_in.shape[2] - g_proj - r_proj
    m_pad = -(-m_proj // LANE) * LANE
    alpha = (2 * depth) ** 0.25
    xf = x.reshape(t, dm)
    for l in range(depth):
        wl = w_in[l].astype(BF16)
        pg, pr, pm = _proj(xf, wl[:, :g_proj], wl[:, g_proj:g_proj + r_proj],
                           _pad_cols(wl[:, g_proj + r_proj:], m_pad))
        y_g = _gmlp(pg, gmlp_ln_g[l], gmlp_ln_b[l], gmlp_ws[l], gmlp_bs[l])
        r, v, a, kd, b, lw, bonus, rgate = _rwkv_prep(
            pr, seq, rwkv_mu[l], rwkv_w0[l], rwkv_w2[l], rwkv_a0[l], rwkv_a2[l], rwkv_g2[l],
            rwkv_k_k[l], rwkv_k_a[l], rwkv_r_k[l].reshape(-1))
        ro = _rwkv_scan(r, v, a, kd, b, lw, batch, seq)
        q, k, gates = _mlstm_prep(pm, seq, mlstm_conv_w[l], mlstm_conv_b[l], mlstm_gate_b[l], mw)
        mh = _mlstm_scan(q, k, pm, gates, batch, seq)
        x1, x1b, topi, gate = _mix_out(xf, y_g, ro, bonus, rgate, rwkv_ln_g[l], rwkv_ln_b[l], mh, pm,
                                       mlstm_ln_g[l], w_out[l], ln1_g[l], ln1_b[l], router_w[l], router_b[l],
                                       alpha)
        dest, row_t, block_e, n_used = _moe_plan(topi, t)
        xs = jnp.take(x1b, row_t, axis=0)
        ys = _moe_experts(xs, block_e, n_used, exp_w1[l].astype(BF16), exp_b1[l], exp_w2[l].astype(BF16),
                          exp_b2[l])
        yg = jnp.take(ys, dest, axis=0).reshape(t, TOP_K * dm)
        xf = _combine(x1, yg, gate, ln2_g[l], ln2_b[l], alpha)
    return xf.reshape(batch, seq, dm)
```

```python
import functools
import math

import jax
import jax.numpy as jnp
from jax import lax
from jax.experimental import pallas as pl
from jax.experimental.pallas import tpu as pltpu

F32 = jnp.float32
BF16 = jnp.bfloat16
HI = lax.Precision.HIGHEST

HEAD_DIM = 64
GMLP_CHUNK = 128
MLSTM_CHUNK = 128
RWKV_CHUNK = 64
W_LORA = 64
A_LORA = 64
G_LORA = 128
N_EXPERTS = 32
TOP_K = 4
MOE_BLOCK = 256
SWIGLU_LIMIT = 7.0
SWIGLU_ALPHA = 1.702
LN_EPS = 1e-5
RWKV_GN_EPS = 64e-5
LANE = 128
SUBLANE = 8
VMEM_LIMIT = 48 * 1024 * 1024
NEG_BIG = -1e30


def _cparams(n_axes):
    return pltpu.CompilerParams(dimension_semantics=("arbitrary",) * n_axes,
                                vmem_limit_bytes=VMEM_LIMIT)


def _full(shape):
    return pl.BlockSpec(shape, lambda *_: (0,) * len(shape))


def _dot(a, b, precision=None):
    return jnp.dot(a, b, preferred_element_type=F32, precision=precision)


def _dot_nt(a, b, precision=None):
    return lax.dot_general(a, b, (((1,), (1,)), ((), ())), preferred_element_type=F32, precision=precision)


def _dot_tn(a, b, precision=None):
    return lax.dot_general(a, b, (((0,), (0,)), ((), ())), preferred_element_type=F32, precision=precision)


def _sigmoid(x):
    return 1.0 / (1.0 + jnp.exp(-x))


def _softplus(x):
    return jnp.maximum(x, 0.0) + jnp.log1p(jnp.exp(-jnp.abs(x)))


def _block_diag_ones(width):
    h = jnp.arange(width) // HEAD_DIM
    return (h[:, None] == h[None, :]).astype(F32)


def _proj_body(x_ref, wg_ref, wr_ref, wm_ref, pg_ref, pr_ref, pm_ref):
    xb = x_ref[...].astype(BF16)
    pg_ref[...] = _dot(xb, wg_ref[...])
    pr_ref[...] = _dot(xb, wr_ref[...])
    pm_ref[...] = _dot(xb, wm_ref[...])


def _proj(x, wg, wr, wm, tm=256):
    t, d = x.shape
    ng, nr, nm = wg.shape[1], wr.shape[1], wm.shape[1]
    row = lambda n: pl.BlockSpec((tm, n), lambda i: (i, 0))
    return pl.pallas_call(
        _proj_body,
        grid=(t // tm,),
        in_specs=[row(d), _full((d, ng)), _full((d, nr)), _full((d, nm))],
        out_specs=[row(ng), row(nr), row(nm)],
        out_shape=[jax.ShapeDtypeStruct((t, n), F32) for n in (ng, nr, nm)],
        compiler_params=_cparams(1),
        name="in_proj",
    )(x, wg, wr, wm)


def _gmlp_body(pg_ref, lng_ref, lnb_ref, ws_ref, bst_ref, o_ref, *, gw, chunks):
    p = pg_ref[...]
    p = 0.5 * p * (1.0 + lax.erf(p * math.sqrt(0.5)))
    u, v = p[:, :gw], p[:, gw:]
    mu = jnp.mean(v, axis=-1, keepdims=True)
    vc = v - mu
    var = jnp.mean(vc * vc, axis=-1, keepdims=True)
    vn = vc * lax.rsqrt(var + LN_EPS) * lng_ref[...] + lnb_ref[...]
    n_heads = gw // HEAD_DIM
    for c in range(chunks):
        rows = slice(c * GMLP_CHUNK, (c + 1) * GMLP_CHUNK)
        ys = []
        for h in range(n_heads):
            cols = slice(h * HEAD_DIM, (h + 1) * HEAD_DIM)
            y = _dot(ws_ref[h], vn[rows, cols].astype(BF16)) + bst_ref[:, h:h + 1]
            ys.append(y)
        o_ref[rows, :] = u[rows, :] * jnp.concatenate(ys, axis=1)


def _gmlp(pg, ln_g, ln_b, ws, bs, chunks=4):
    t = pg.shape[0]
    gw = pg.shape[1] // 2
    n_heads = gw // HEAD_DIM
    tm = chunks * GMLP_CHUNK
    bst = jnp.zeros((GMLP_CHUNK, LANE), F32).at[:, :n_heads].set(bs.T)
    return pl.pallas_call(
        functools.partial(_gmlp_body, gw=gw, chunks=chunks),
        grid=(t // tm,),
        in_specs=[pl.BlockSpec((tm, 2 * gw), lambda i: (i, 0)), _full((1, gw)), _full((1, gw)),
                  _full((n_heads, GMLP_CHUNK, GMLP_CHUNK)), _full((GMLP_CHUNK, LANE))],
        out_specs=pl.BlockSpec((tm, gw), lambda i: (i, 0)),
        out_shape=jax.ShapeDtypeStruct((t, gw), F32),
        compiler_params=_cparams(1),
        name="gmlp",
    )(pg, ln_g.reshape(1, gw), ln_b.reshape(1, gw), ws.astype(BF16), bst)


def _halo_specs(tm, width, n_rows):
    per8 = tm // SUBLANE
    last = n_rows // SUBLANE - 1
    prev = pl.BlockSpec((SUBLANE, width), lambda i: (jnp.maximum(i * per8 - 1, 0), 0))
    nxt = pl.BlockSpec((SUBLANE, width), lambda i: (jnp.minimum((i + 1) * per8, last), 0))
    return prev, nxt


def _neighbours(cur, prev_blk, next_blk, tiles_per_seq):
    tm = cur.shape[0]
    j = pl.program_id(0) % tiles_per_seq
    prev_row = jnp.where(j > 0, prev_blk[SUBLANE - 1:SUBLANE, :], 0.0)
    next_row = jnp.where(j < tiles_per_seq - 1, next_blk[0:1, :], 0.0)
    ridx = lax.broadcasted_iota(jnp.int32, cur.shape, 0)
    before = jnp.where(ridx == 0, prev_row, pltpu.roll(cur, 1, 0))
    after = jnp.where(ridx == tm - 1, next_row, pltpu.roll(cur, tm - 1, 0))
    return before, after


def _rwkv_prep_body(pr_ref, prev_ref, next_ref, mu_ref, w0_ref, w2_ref, a0_ref, a2_ref, g2_ref,
                    kk_ref, ka_ref, rk_ref, bd_ref,
                    r_out, v_out, a_out, kd_out, b_out, lw_out, bonus_out, gate_out, *, rw, tiles_per_seq):
    pf = pr_ref[...]
    before, after = _neighbours(pf, prev_ref[...], next_ref[...], tiles_per_seq)
    pf = pf + mu_ref[0:1, :] * (before - pf) + mu_ref[1:2, :] * (after - pf)
    o3 = 3 * rw
    r, k, v = pf[:, :rw], pf[:, rw:2 * rw], pf[:, 2 * rw:o3]
    wd = pf[:, o3:o3 + W_LORA]
    ad = pf[:, o3 + W_LORA:o3 + W_LORA + A_LORA]
    gd = pf[:, o3 + W_LORA + A_LORA:]
    bd = bd_ref[...]
    kk = k * kk_ref[...]
    ss = _dot(kk * kk, bd, HI)
    kk = kk / jnp.maximum(jnp.sqrt(ss), 1e-12)
    twd = jnp.tanh(wd)
    ksum = jnp.zeros_like(k)
    for d in range(2):
        w_log = -_softplus(-(w0_ref[d:d + 1, :] + _dot(twd, w2_ref[d], HI))) - 0.5
        lw_out[d] = -jnp.exp(w_log)
        iclr = _sigmoid(a0_ref[d:d + 1, :] + _dot(ad, a2_ref[d], HI))
        kd = k * (1.0 + (iclr - 1.0) * ka_ref[...])
        kd_out[d] = kd
        b_out[d] = kk * iclr
        ksum = ksum + kd
    r_out[...] = r
    v_out[...] = v
    a_out[...] = -kk
    bonus_out[...] = _dot(r * ksum * rk_ref[...], bd, HI) * v
    gate_out[...] = _dot(_sigmoid(gd).astype(BF16), g2_ref[...])


def _rwkv_prep(pr, seq, mu, w0, w2, a0, a2, g2, k_k, k_a, r_k, tm=256):
    t, rproj = pr.shape
    rw = w0.shape[1]
    tiles_per_seq = seq // tm
    prev, nxt = _halo_specs(tm, rproj, t)
    row = pl.BlockSpec((tm, rw), lambda i: (i, 0))
    row2 = pl.BlockSpec((2, tm, rw), lambda i: (0, i, 0))
    one = jax.ShapeDtypeStruct((t, rw), F32)
    two = jax.ShapeDtypeStruct((2, t, rw), F32)
    return pl.pallas_call(
        functools.partial(_rwkv_prep_body, rw=rw, tiles_per_seq=tiles_per_seq),
        grid=(t // tm,),
        in_specs=[pl.BlockSpec((tm, rproj), lambda i: (i, 0)), prev, nxt,
                  _full((2, rproj)), _full((2, rw)), _full((2, W_LORA, rw)), _full((2, rw)),
                  _full((2, A_LORA, rw)), _full((G_LORA, rw)), _full((1, rw)), _full((1, rw)),
                  _full((1, rw)), _full((rw, rw))],
        out_specs=[row, row, row, row2, row2, row2, row, row],
        out_shape=[one, one, one, two, two, two, one, one],
        compiler_params=_cparams(1),
        name="rwkv_prep",
    )(pr, pr, pr, mu, w0, w2, a0, a2, g2.astype(BF16), k_k.reshape(1, rw), k_a.reshape(1, rw),
      r_k.reshape(1, rw), _block_diag_ones(rw))


def _split(x):
    hi = x.astype(BF16)
    return hi, (x - hi.astype(F32)).astype(BF16)


def _mm(a, b, mode, dot=_dot):
    if mode == "hi":
        return dot(a, b, HI)
    if mode == "b1":
        return dot(a.astype(BF16), b.astype(BF16))
    ah, al = _split(a)
    bh, bl = _split(b)
    return dot(ah, bh) + (dot(ah, bl) + dot(al, bh))


def _cumsum_rows(tri_bf16, x):
    hi = x.astype(BF16)
    r1 = x - hi.astype(F32)
    mid = r1.astype(BF16)
    lo = (r1 - mid.astype(F32)).astype(BF16)
    return _dot(tri_bf16, hi) + (_dot(tri_bf16, mid) + _dot(tri_bf16, lo))


P_G, P_INV, P_APPLY, P_STATE, P_SEQ = "b3", "b1", "b1", "b3", "b3"


def _rwkv_intra_body(r_ref, v_ref, a_ref, kd_ref, b_ref, lw_ref, rq_out, o0_out, mtx_out, hc_out,
                     *, n_heads, chunks):
    L = RWKV_CHUNK
    d = pl.program_id(0)
    row = lax.broadcasted_iota(jnp.int32, (L, L), 0)
    col = lax.broadcasted_iota(jnp.int32, (L, L), 1)
    fwd = d == 0
    rel = (col - row) * (1 - 2 * d)
    incl = rel <= 0
    strict = rel < 0
    eye = (row == col).astype(F32)
    tri = incl.astype(BF16)
    pairs = []
    for c in range(chunks):
        rows = slice(c * L, (c + 1) * L)
        lw = lw_ref[rows, :]
        cum = _cumsum_rows(tri, lw)
        tot = jnp.where(fwd, cum[L - 1:L, :], cum[0:1, :])
        e_neg = jnp.exp(-cum)
        e_end = jnp.exp(tot - cum)
        e_tot = jnp.exp(tot)
        r, v, a, kd, b = r_ref[rows, :], v_ref[rows, :], a_ref[rows, :], kd_ref[rows, :], b_ref[rows, :]
        at, rt, bt, kt = a * jnp.exp(cum - lw), r * jnp.exp(cum), b * e_neg, kd * e_neg
        kend, bend = kd * e_end, b * e_end
        for h in range(n_heads):
            sl = slice(h * HEAD_DIM, (h + 1) * HEAD_DIM)
            pairs.append(dict(at=at[:, sl], rt=rt[:, sl], bt=bt[:, sl], kt=kt[:, sl], v=v[:, sl],
                              kend=kend[:, sl], bend=bend[:, sl], e_tot=e_tot[:, sl]))
    for p in pairs:
        p["g"] = _mm(jnp.concatenate([p["at"], p["rt"]], axis=0),
                     jnp.concatenate([p["bt"], p["kt"]], axis=0), P_G, _dot_nt)
    for p in pairs:
        g = p.pop("g")
        p["pw"] = jnp.where(strict, g[:L, :L], 0.0)
        p["a_ak"] = jnp.where(strict, g[:L, L:], 0.0)
        p["m_rb"] = jnp.where(incl, g[L:, :L], 0.0)
        p["m_rk"] = jnp.where(incl, g[L:, L:], 0.0)
        p["inv"] = eye + p["pw"]
    for _ in range(int(math.log2(L)) - 1):
        for p in pairs:
            p["pw"] = _mm(p["pw"], p["pw"], P_INV)
        for p in pairs:
            p["inv"] = p["inv"] + _mm(p["inv"], p["pw"], P_INV)
    for p in pairs:
        p["akv"] = _mm(p["a_ak"], p["v"], P_APPLY)
    for p in pairs:
        p["wu"] = _mm(p["inv"], jnp.concatenate([p["at"], p["akv"]], axis=1), P_APPLY)
    for p in pairs:
        p["mwu"] = _mm(p["m_rb"], p["wu"], P_APPLY)
    for p in pairs:
        p["o0"] = p["mwu"][:, HEAD_DIM:] + _mm(p["m_rk"], p["v"], P_APPLY)
    for p in pairs:
        p["bw"] = _mm(p["bend"], p["wu"], P_STATE, _dot_tn)
    for p in pairs:
        p["hc"] = _mm(p["kend"], p["v"], P_STATE, _dot_tn) + p["bw"][:, HEAD_DIM:]
    for c in range(chunks):
        ps = pairs[c * n_heads:(c + 1) * n_heads]
        rows = slice(c * L, (c + 1) * L)
        krows = slice(c * HEAD_DIM, (c + 1) * HEAD_DIM)
        rq_out[rows, :] = jnp.concatenate([p["rt"] + p["mwu"][:, :HEAD_DIM] for p in ps], axis=1)
        o0_out[rows, :] = jnp.concatenate([p["o0"] for p in ps], axis=1)
        mtx_out[krows, :] = jnp.concatenate([eye * p["e_tot"] + p["bw"][:, :HEAD_DIM] for p in ps], axis=1)
        hc_out[krows, :] = jnp.concatenate([p["hc"] for p in ps], axis=1)


def _rwkv_intra(r, v, a, kd, b, lw, chunks=2):
    t, rw = r.shape
    n_heads = rw // HEAD_DIM
    tm = chunks * RWKV_CHUNK
    tk = chunks * HEAD_DIM
    n_tiles = t // tm
    one = pl.BlockSpec((tm, rw), lambda d, i: (i, 0))
    two = pl.BlockSpec((None, tm, rw), lambda d, i: (d, i, 0))
    twok = pl.BlockSpec((None, tk, rw), lambda d, i: (d, i, 0))
    return pl.pallas_call(
        functools.partial(_rwkv_intra_body, n_heads=n_heads, chunks=chunks),
        grid=(2, n_tiles),
        in_specs=[one, one, one, two, two, two],
        out_specs=[two, two, twok, twok],
        out_shape=[jax.ShapeDtypeStruct((2, t, rw), F32), jax.ShapeDtypeStruct((2, t, rw), F32),
                   jax.ShapeDtypeStruct((2, n_tiles * tk, rw), F32),
                   jax.ShapeDtypeStruct((2, n_tiles * tk, rw), F32)],
        compiler_params=_cparams(2),
        name="rwkv_intra",
    )(r, v, a, kd, b, lw)


def _rwkv_seq_body(rq0, o00, mtx0, hc0, rq1, o01, mtx1, hc1, out0, out1, h_ref, *, n_heads, batch):
    c = pl.program_id(0)

    @pl.when(c == 0)
    def _():
        h_ref[...] = jnp.zeros_like(h_ref)

    L = RWKV_CHUNK
    for d, (rq, o0, mtx, hc, out) in enumerate(((rq0, o00, mtx0, hc0, out0), (rq1, o01, mtx1, hc1, out1))):
        for bi in range(batch):
            rq_t, mtx_t = rq[bi], mtx[bi]
            state = h_ref[d, bi]
            outs, states = [], []
            for h in range(n_heads):
                sl = slice(h * HEAD_DIM, (h + 1) * HEAD_DIM)
                prod = _mm(jnp.concatenate([rq_t[:, sl], mtx_t[:, sl]], axis=0), state[:, sl], P_SEQ)
                outs.append(prod[:L])
                states.append(prod[L:])
            out[bi] = jnp.concatenate(outs, axis=1) + o0[bi]
            h_ref[d, bi] = jnp.concatenate(states, axis=1) + hc[bi]


def _rwkv_seq(rq, o0, mtx, hc, batch, seq):
    _, t, rw = rq.shape
    n_heads = rw // HEAD_DIM
    L = RWKV_CHUNK
    nc = seq // L
    as4 = lambda x: x.reshape(2, batch, x.shape[1] // batch, rw)
    rq, o0, mtx, hc = as4(rq), as4(o0), as4(mtx), as4(hc)
    fwd = lambda rows: pl.BlockSpec((None, batch, rows, rw), lambda c: (0, 0, c, 0))
    bwd = lambda rows: pl.BlockSpec((None, batch, rows, rw), lambda c: (1, 0, nc - 1 - c, 0))
    out0, out1 = pl.pallas_call(
        functools.partial(_rwkv_seq_body, n_heads=n_heads, batch=batch),
        grid=(nc,),
        in_specs=[fwd(L), fwd(L), fwd(HEAD_DIM), fwd(HEAD_DIM), bwd(L), bwd(L), bwd(HEAD_DIM), bwd(HEAD_DIM)],
        out_specs=[pl.BlockSpec((batch, L, rw), lambda c: (0, c, 0)),
                   pl.BlockSpec((batch, L, rw), lambda c: (0, nc - 1 - c, 0))],
        out_shape=[jax.ShapeDtypeStruct((batch, seq, rw), F32)] * 2,
        scratch_shapes=[pltpu.VMEM((2, batch, HEAD_DIM, rw), F32)],
        compiler_params=_cparams(1),
        name="rwkv_seq",
    )(rq, o0, mtx, hc, rq, o0, mtx, hc)
    return out0.reshape(t, rw), out1.reshape(t, rw)


def _rwkv_scan(r, v, a, kd, b, lw, batch, seq):
    rq, o0, mtx, hc = _rwkv_intra(r, v, a, kd, b, lw)
    return _rwkv_seq(rq, o0, mtx, hc, batch, seq)


def _mlstm_prep_body(qk_ref, prev_ref, next_ref, g_ref, cw_ref, cb_ref, gb_ref, q_out, k_out, gate_out,
                     *, mw, n_heads, tiles_per_seq):
    x = qk_ref[...]
    before, after = _neighbours(x, prev_ref[...], next_ref[...], tiles_per_seq)
    y = cb_ref[...] + before * cw_ref[0:1, :] + x * cw_ref[1:2, :] + after * cw_ref[2:3, :]
    y = y * _sigmoid(y)
    q_out[...] = y[:, :mw]
    k_out[...] = y[:, mw:] * (HEAD_DIM ** -0.5)
    g = g_ref[...] + gb_ref[...]
    lane = lax.broadcasted_iota(jnp.int32, g.shape, 1)
    for d in range(2):
        ig = g if d == 0 else pltpu.roll(g, LANE - n_heads, 1)
        fg = pltpu.roll(g, LANE - (1 + d) * n_heads, 1)
        lf = -_softplus(-fg)
        gate_out[d] = jnp.where(lane < n_heads, ig, jnp.where(lane < 2 * n_heads, lf, 0.0))


def _mlstm_prep(pm, seq, conv_w, conv_b, gate_b, mw, tm=256):
    t = pm.shape[0]
    n_heads = mw // HEAD_DIM
    tiles_per_seq = seq // tm
    w2 = 2 * mw
    prev, nxt = _halo_specs(tm, w2, t)
    gcol = (4 * mw) // LANE
    gb = jnp.zeros((1, LANE), F32).at[0, :4 * n_heads].set(gate_b)
    row = pl.BlockSpec((tm, mw), lambda i: (i, 0))
    return pl.pallas_call(
        functools.partial(_mlstm_prep_body, mw=mw, n_heads=n_heads, tiles_per_seq=tiles_per_seq),
        grid=(t // tm,),
        in_specs=[pl.BlockSpec((tm, w2), lambda i: (i, 0)), prev, nxt,
                  pl.BlockSpec((tm, LANE), lambda i: (i, gcol)),
                  _full((3, w2)), _full((1, w2)), _full((1, LANE))],
        out_specs=[row, row, pl.BlockSpec((2, tm, LANE), lambda i: (0, i, 0))],
        out_shape=[jax.ShapeDtypeStruct((t, mw), F32), jax.ShapeDtypeStruct((t, mw), F32),
                   jax.ShapeDtypeStruct((2, t, LANE), F32)],
        compiler_params=_cparams(1),
        name="mlstm_prep",
    )(pm, pm, pm, pm, conv_w, conv_b.reshape(1, w2), gb)


def _mlstm_scan_body(q_ref, k_ref, v_ref, g_ref, o_ref, c_ref, m_ref, *, n_heads):
    L = MLSTM_CHUNK
    d = pl.program_id(0)
    c = pl.program_id(2)

    @pl.when(c == 0)
    def _():
        c_ref[...] = jnp.zeros_like(c_ref)
        m_ref[...] = jnp.zeros_like(m_ref)

    row = lax.broadcasted_iota(jnp.int32, (L, L), 0)
    col = lax.broadcasted_iota(jnp.int32, (L, L), 1)
    fwd = d == 0
    incl = (col - row) * (1 - 2 * d) <= 0
    g = g_ref[...]
    bcum = _dot(incl.astype(F32), g, HI)
    g_t = g.T
    bcum_t = bcum.T
    lane64 = lax.broadcasted_iota(jnp.int32, (L, HEAD_DIM), 1)
    ones_col = (lane64 == 0).astype(F32)
    q, k, v = q_ref[...], k_ref[...], v_ref[...]
    hs = []
    for h in range(n_heads):
        sl = slice(h * HEAD_DIM, (h + 1) * HEAD_DIM)
        bc = bcum[:, n_heads + h:n_heads + h + 1]
        br = bcum_t[n_heads + h:n_heads + h + 1, :]
        igr = g_t[h:h + 1, :]
        igc = g[:, h:h + 1]
        m_st = m_ref[h:h + 1, 0:1]
        dm = jnp.where(incl, bc - br + igr, -jnp.inf)
        inter = bc + m_st
        m_t = jnp.maximum(jnp.max(dm, axis=1, keepdims=True), inter)
        b_last = jnp.where(fwd, bc[L - 1:L, :], bc[0:1, :])
        lwc = b_last - bc + igc
        m_new = jnp.maximum(b_last + m_st, jnp.max(lwc, axis=0, keepdims=True))
        hs.append(dict(qh=q[:, sl].astype(BF16), kh=k[:, sl],
                       vext=jnp.concatenate([v[:, sl], ones_col], axis=1).astype(BF16),
                       decay=jnp.exp(dm - m_t), w_inter=jnp.exp(inter - m_t), floor=jnp.exp(-m_t),
                       wts=jnp.exp(lwc - m_new), dec=jnp.exp(b_last + m_st - m_new), m_new=m_new,
                       cst=c_ref[h]))
    for p in hs:
        p["sc"] = (_dot_nt(p["qh"], p["kh"].astype(BF16)) * p["decay"]).astype(BF16)
    for p in hs:
        p["numext"] = _dot(p["sc"], p["vext"]) + p["w_inter"] * _dot(p["qh"], p["cst"].astype(BF16))
    for p in hs:
        p["upd"] = _dot_tn((p["wts"] * p["kh"]).astype(BF16), p["vext"])
    outs = []
    for h, p in enumerate(hs):
        num = p["numext"][:, :HEAD_DIM]
        den = p["numext"][:, HEAD_DIM:HEAD_DIM + 1]
        outs.append(num / jnp.maximum(jnp.abs(den), p["floor"]))
        c_ref[h] = p["dec"] * p["cst"] + p["upd"]
        m_ref[h:h + 1, :] = jnp.broadcast_to(p["m_new"], (1, LANE))
    o_ref[...] = jnp.concatenate(outs, axis=1)


def _mlstm_scan(q, k, pm, gates, batch, seq):
    t, mw = q.shape
    n_heads = mw // HEAD_DIM
    L = MLSTM_CHUNK
    nc = seq // L
    blk = lambda d, bi, c: bi * nc + jnp.where(d == 0, c, nc - 1 - c)
    one = pl.BlockSpec((L, mw), lambda d, bi, c: (blk(d, bi, c), 0))
    vspec = pl.BlockSpec((L, mw), lambda d, bi, c: (blk(d, bi, c), 2))
    return pl.pallas_call(
        functools.partial(_mlstm_scan_body, n_heads=n_heads),
        grid=(2, batch, nc),
        in_specs=[one, one, vspec, pl.BlockSpec((None, L, LANE), lambda d, bi, c: (d, blk(d, bi, c), 0))],
        out_specs=pl.BlockSpec((None, L, mw), lambda d, bi, c: (d, blk(d, bi, c), 0)),
        out_shape=jax.ShapeDtypeStruct((2, t, mw), F32),
        scratch_shapes=[pltpu.VMEM((n_heads, HEAD_DIM, LANE), F32), pltpu.VMEM((SUBLANE, LANE), F32)],
        compiler_params=_cparams(3),
        name="mlstm_scan",
    )(q, k, pm, gates)


def _layer_norm(x, g, b):
    mu = jnp.mean(x, axis=-1, keepdims=True)
    xc = x - mu
    var = jnp.mean(xc * xc, axis=-1, keepdims=True)
    return xc * lax.rsqrt(var + LN_EPS) * g + b


def _head_norm(x, bd_mean, eps):
    mu = _dot(x, bd_mean, HI)
    xc = x - mu
    var = _dot(xc * xc, bd_mean, HI)
    return xc * lax.rsqrt(var + eps)


def _mix_out_body(x_ref, yg_ref, ro0_ref, ro1_ref, bonus_ref, rgate_ref, rlg_ref, rlb_ref, mh_ref, og_ref, mlg_ref,
                  wg_ref, wr_ref, wm_ref, l1g_ref, l1b_ref, rw_ref, rb_ref, bdm_ref,
                  x1_out, x1b_out, topi_out, gate_out, *, alpha):
    bdm = bdm_ref[...]
    yr = _head_norm(ro0_ref[...] + ro1_ref[...], bdm, RWKV_GN_EPS) * rlg_ref[...] + rlb_ref[...]
    yr = (yr + bonus_ref[...]) * rgate_ref[...]
    ym = _sigmoid(og_ref[...]) * (_head_norm(mh_ref[0] + mh_ref[1], bdm, LN_EPS) * mlg_ref[...])
    mix = (_dot(yg_ref[...].astype(BF16), wg_ref[...]) + _dot(yr.astype(BF16), wr_ref[...])
           + _dot(ym.astype(BF16), wm_ref[...]))
    x1 = _layer_norm(alpha * x_ref[...] + mix, l1g_ref[...], l1b_ref[...])
    x1_out[...] = x1
    x1b_out[...] = x1.astype(BF16)
    lg = _dot(x1, rw_ref[...], HI) + rb_ref[...]
    lane = lax.broadcasted_iota(jnp.int32, lg.shape, 1)
    vals, topi = [], jnp.zeros(lg.shape, jnp.int32)
    for j in range(TOP_K):
        mx = jnp.max(lg, axis=1, keepdims=True)
        idx = jnp.min(jnp.where(lg == mx, lane, LANE), axis=1, keepdims=True)
        vals.append(mx)
        topi = jnp.where(lane == j, idx, topi)
        lg = jnp.where(lane == idx, -jnp.inf, lg)
    es = [jnp.exp(vj - vals[0]) for vj in vals]
    den = es[0] + es[1] + es[2] + es[3]
    gate = jnp.zeros(lg.shape, F32)
    for j in range(TOP_K):
        gate = jnp.where(lane == j, es[j] / den, gate)
    topi_out[...] = topi
    gate_out[...] = gate


def _mix_out(x, yg, ro, bonus, rgate, rlg, rlb, mh, pm, mlg, w_out, l1g, l1b, router_w, router_b, alpha, tm=256):
    t, dm = x.shape
    gw, rw, mw = yg.shape[1], bonus.shape[1], mh.shape[2]
    assert rw == mw
    wb = w_out.astype(BF16)
    rwp = jnp.zeros((dm, LANE), F32).at[:, :N_EXPERTS].set(router_w)
    rbp = jnp.full((1, LANE), NEG_BIG, F32).at[0, :N_EXPERTS].set(router_b)
    row = lambda n: pl.BlockSpec((tm, n), lambda i: (i, 0))
    row2 = lambda n: pl.BlockSpec((2, tm, n), lambda i: (0, i, 0))
    vec = lambda n: _full((1, n))
    return pl.pallas_call(
        functools.partial(_mix_out_body, alpha=alpha),
        grid=(t // tm,),
        in_specs=[row(dm), row(gw), row(rw), row(rw), row(rw), row(rw), vec(rw), vec(rw), row2(mw),
                  pl.BlockSpec((tm, mw), lambda i: (i, 3)),
                  vec(mw), _full((gw, dm)), _full((rw, dm)), _full((mw, dm)), vec(dm), vec(dm),
                  _full((dm, LANE)), vec(LANE), _full((rw, rw))],
        out_specs=[row(dm), row(dm), row(LANE), row(LANE)],
        out_shape=[jax.ShapeDtypeStruct((t, dm), F32), jax.ShapeDtypeStruct((t, dm), BF16),
                   jax.ShapeDtypeStruct((t, LANE), jnp.int32), jax.ShapeDtypeStruct((t, LANE), F32)],
        compiler_params=_cparams(1),
        name="mix_out",
    )(x, yg, ro[0], ro[1], bonus, rgate, rlg.reshape(1, rw), rlb.reshape(1, rw), mh, pm, mlg.reshape(1, mw),
      wb[:gw], wb[gw:gw + rw], wb[gw + rw:], l1g.reshape(1, dm), l1b.reshape(1, dm), rwp, rbp,
      _block_diag_ones(rw) / HEAD_DIM)


def _moe_body(be_ref, nu_ref, xs_ref, w1_ref, b1_ref, w2_ref, b2_ref, o_ref, *, dff):
    i = pl.program_id(0)

    @pl.when(i < nu_ref[0])
    def _():
        hdn = _dot(xs_ref[...], w1_ref[...]) + b1_ref[...]
        g_ = jnp.minimum(hdn[:, :dff], SWIGLU_LIMIT)
        u_ = jnp.clip(hdn[:, dff:], -SWIGLU_LIMIT, SWIGLU_LIMIT)
        act = (u_ + 1.0) * (g_ * _sigmoid(g_ * SWIGLU_ALPHA))
        o_ref[...] = _dot(act.astype(BF16), w2_ref[...]) + b2_ref[...]

    @pl.when(i >= nu_ref[0])
    def _():
        o_ref[...] = jnp.zeros_like(o_ref)


def _moe_experts(xs, block_e, n_used, w1, b1, w2, b2):
    rows, dm = xs.shape
    nb = rows // MOE_BLOCK
    ne, _, dff2 = w1.shape
    dff = dff2 // 2
    grid_spec = pltpu.PrefetchScalarGridSpec(
        num_scalar_prefetch=2,
        grid=(nb,),
        in_specs=[pl.BlockSpec((MOE_BLOCK, dm), lambda i, be, nu: (i, 0)),
                  pl.BlockSpec((None, dm, dff2), lambda i, be, nu: (be[i], 0, 0)),
                  pl.BlockSpec((None, 1, dff2), lambda i, be, nu: (be[i], 0, 0)),
                  pl.BlockSpec((None, dff, dm), lambda i, be, nu: (be[i], 0, 0)),
                  pl.BlockSpec((None, 1, dm), lambda i, be, nu: (be[i], 0, 0))],
        out_specs=pl.BlockSpec((MOE_BLOCK, dm), lambda i, be, nu: (i, 0)),
    )
    return pl.pallas_call(
        functools.partial(_moe_body, dff=dff),
        grid_spec=grid_spec,
        out_shape=jax.ShapeDtypeStruct((rows, dm), F32),
        compiler_params=_cparams(1),
        name="moe_experts",
    )(block_e, n_used, xs, w1, b1.reshape(ne, 1, dff2), w2, b2.reshape(ne, 1, dm))


def _moe_plan(topi, n_tokens):
    na = n_tokens * TOP_K
    flat_e = topi[:, :TOP_K].reshape(na)
    onehot = (flat_e[:, None] == jnp.arange(N_EXPERTS, dtype=jnp.int32)[None, :]).astype(jnp.int32)
    csum = jnp.cumsum(onehot, axis=0)
    counts = csum[-1]
    rank = jnp.sum((csum - onehot) * onehot, axis=1)
    pcounts = (counts + MOE_BLOCK - 1) // MOE_BLOCK * MOE_BLOCK
    pend = jnp.cumsum(pcounts)
    pstart = pend - pcounts
    dest = (pstart[flat_e] + rank).astype(jnp.int32)
    nb = -(-na // MOE_BLOCK) + N_EXPERTS
    flat_t = jnp.arange(na, dtype=jnp.int32) // TOP_K
    row_t = jnp.zeros((nb * MOE_BLOCK,), jnp.int32).at[dest].set(flat_t)
    block_e = jnp.minimum(jnp.searchsorted(pend, jnp.arange(nb, dtype=jnp.int32) * MOE_BLOCK, side='right'),
                          N_EXPERTS - 1).astype(jnp.int32)
    n_used = (pend[-1] // MOE_BLOCK).astype(jnp.int32).reshape(1)
    return dest, row_t, block_e, n_used


def _combine_body(x1_ref, yg_ref, gate_ref, g_ref, b_ref, o_ref, *, alpha, dm):
    gate = gate_ref[...]
    ffn = gate[:, 0:1] * yg_ref[:, 0:dm]
    for j in range(1, TOP_K):
        ffn = ffn + gate[:, j:j + 1] * yg_ref[:, j * dm:(j + 1) * dm]
    o_ref[...] = _layer_norm(alpha * x1_ref[...] + ffn, g_ref[...], b_ref[...])


def _combine(x1, yg, gate, ln_g, ln_b, alpha, tm=256):
    t, dm = x1.shape
    return pl.pallas_call(
        functools.partial(_combine_body, alpha=alpha, dm=dm),
        grid=(t // tm,),
        in_specs=[pl.BlockSpec((tm, dm), lambda i: (i, 0)), pl.BlockSpec((tm, TOP_K * dm), lambda i: (i, 0)),
                  pl.BlockSpec((tm, LANE), lambda i: (i, 0)), _full((1, dm)), _full((1, dm))],
        out_specs=pl.BlockSpec((tm, dm), lambda i: (i, 0)),
        out_shape=jax.ShapeDtypeStruct((t, dm), F32),
        compiler_params=_cparams(1),
        name="combine_ln",
    )(x1, yg, gate, ln_g.reshape(1, dm), ln_b.reshape(1, dm))


def _pad_cols(w, width):
    return jnp.pad(w, ((0, 0), (0, width - w.shape[1])))


def kernel(x, w_in, gmlp_ln_g, gmlp_ln_b, gmlp_ws, gmlp_bs, rwkv_mu, rwkv_w0, rwkv_w2, rwkv_a0, rwkv_a2, rwkv_g2, rwkv_k_k, rwkv_k_a, rwkv_r_k, rwkv_ln_g, rwkv_ln_b, mlstm_conv_w, mlstm_conv_b, mlstm_gate_b, mlstm_ln_g, w_out, ln1_g, ln1_b, router_w, router_b, exp_w1, exp_b1, exp_w2, exp_b2, ln2_g, ln2_b):
    batch, seq, dm = x.shape
    depth = w_in.shape[0]
    t = batch * seq
    gw = gmlp_ln_g.shape[1]
    rw = rwkv_w0.shape[2]
    mw = mlstm_ln_g.shape[1]
    g_proj = 2 * gw
    r_proj = 3 * rw + W_LORA + A_LORA + G_LORA
    m_proj = w_in.shape[2] - g_proj - r_proj
    m_pad = -(-m_proj // LANE) * LANE
    alpha = (2 * depth) ** 0.25
    xf = x.reshape(t, dm)
    for l in range(depth):
        wl = w_in[l].astype(BF16)
        pg, pr, pm = _proj(xf, wl[:, :g_proj], wl[:, g_proj:g_proj + r_proj],
                           _pad_cols(wl[:, g_proj + r_proj:], m_pad))
        y_g = _gmlp(pg, gmlp_ln_g[l], gmlp_ln_b[l], gmlp_ws[l], gmlp_bs[l])
        r, v, a, kd, b, lw, bonus, rgate = _rwkv_prep(
            pr, seq, rwkv_mu[l], rwkv_w0[l], rwkv_w2[l], rwkv_a0[l], rwkv_a2[l], rwkv_g2[l],
            rwkv_k_k[l], rwkv_k_a[l], rwkv_r_k[l].reshape(-1))
        ro = _rwkv_scan(r, v, a, kd, b, lw, batch, seq)
        q, k, gates = _mlstm_prep(pm, seq, mlstm_conv_w[l], mlstm_conv_b[l], mlstm_gate_b[l], mw)
        mh = _mlstm_scan(q, k, pm, gates, batch, seq)
        x1, x1b, topi, gate = _mix_out(xf, y_g, ro, bonus, rgate, rwkv_ln_g[l], rwkv_ln_b[l], mh, pm,
                                       mlstm_ln_g[l], w_out[l], ln1_g[l], ln1_b[l], router_w[l], router_b[l],
                                       alpha)
        dest, row_t, block_e, n_used = _moe_plan(topi, t)
        xs = jnp.take(x1b, row_t, axis=0)
        ys = _moe_experts(xs, block_e, n_used, exp_w1[l].astype(BF16), exp_b1[l], exp_w2[l].astype(BF16),
                          exp_b2[l])
        yg = jnp.take(ys, dest, axis=0).reshape(t, TOP_K * dm)
        xf = _combine(x1, yg, gate, ln2_g[l], ln2_b[l], alpha)
    return xf.reshape(batch, seq, dm)
```

```python
import functools
import math

import jax
import jax.numpy as jnp
from jax import lax
from jax.experimental import pallas as pl
from jax.experimental.pallas import tpu as pltpu
from jax.experimental.pallas import tpu_sc as plsc

F32 = jnp.float32
BF16 = jnp.bfloat16
HI = lax.Precision.HIGHEST

HEAD_DIM = 64
GMLP_CHUNK = 128
MLSTM_CHUNK = 128
RWKV_CHUNK = 64
W_LORA = 64
A_LORA = 64
G_LORA = 128
N_EXPERTS = 32
TOP_K = 4
MOE_BLOCK = 256
SWIGLU_LIMIT = 7.0
SWIGLU_ALPHA = 1.702
LN_EPS = 1e-5
RWKV_GN_EPS = 64e-5
LANE = 128
SUBLANE = 8
VMEM_LIMIT = 48 * 1024 * 1024
NEG_BIG = -1e30


def _cparams(n_axes):
    return pltpu.CompilerParams(dimension_semantics=("arbitrary",) * n_axes,
                                vmem_limit_bytes=VMEM_LIMIT)


def _full(shape):
    return pl.BlockSpec(shape, lambda *_: (0,) * len(shape))


def _dot(a, b, precision=None):
    return jnp.dot(a, b, preferred_element_type=F32, precision=precision)


def _dot_nt(a, b, precision=None):
    return lax.dot_general(a, b, (((1,), (1,)), ((), ())), preferred_element_type=F32, precision=precision)


def _dot_tn(a, b, precision=None):
    return lax.dot_general(a, b, (((0,), (0,)), ((), ())), preferred_element_type=F32, precision=precision)


def _sigmoid(x):
    return 1.0 / (1.0 + jnp.exp(-x))


def _softplus(x):
    return jnp.maximum(x, 0.0) + jnp.log1p(jnp.exp(-jnp.abs(x)))


def _block_diag_ones(width):
    h = jnp.arange(width) // HEAD_DIM
    return (h[:, None] == h[None, :]).astype(F32)


def _proj_body(x_ref, wg_ref, wr_ref, wm_ref, pg_ref, pr_ref, pm_ref):
    xb = x_ref[...].astype(BF16)
    pg_ref[...] = _dot(xb, wg_ref[...])
    pr_ref[...] = _dot(xb, wr_ref[...])
    pm_ref[...] = _dot(xb, wm_ref[...])


def _proj(x, wg, wr, wm, tm=256):
    t, d = x.shape
    ng, nr, nm = wg.shape[1], wr.shape[1], wm.shape[1]
    row = lambda n: pl.BlockSpec((tm, n), lambda i: (i, 0))
    return pl.pallas_call(
        _proj_body,
        grid=(t // tm,),
        in_specs=[row(d), _full((d, ng)), _full((d, nr)), _full((d, nm))],
        out_specs=[row(ng), row(nr), row(nm)],
        out_shape=[jax.ShapeDtypeStruct((t, n), F32) for n in (ng, nr, nm)],
        compiler_params=_cparams(1),
        name="in_proj",
    )(x, wg, wr, wm)


def _gmlp_body(pg_ref, lng_ref, lnb_ref, ws_ref, bst_ref, o_ref, *, gw, chunks):
    p = pg_ref[...]
    p = 0.5 * p * (1.0 + lax.erf(p * math.sqrt(0.5)))
    u, v = p[:, :gw], p[:, gw:]
    mu = jnp.mean(v, axis=-1, keepdims=True)
    vc = v - mu
    var = jnp.mean(vc * vc, axis=-1, keepdims=True)
    vn = vc * lax.rsqrt(var + LN_EPS) * lng_ref[...] + lnb_ref[...]
    n_heads = gw // HEAD_DIM
    for c in range(chunks):
        rows = slice(c * GMLP_CHUNK, (c + 1) * GMLP_CHUNK)
        ys = []
        for h in range(n_heads):
            cols = slice(h * HEAD_DIM, (h + 1) * HEAD_DIM)
            y = _dot(ws_ref[h], vn[rows, cols].astype(BF16)) + bst_ref[:, h:h + 1]
            ys.append(y)
        o_ref[rows, :] = u[rows, :] * jnp.concatenate(ys, axis=1)


def _gmlp(pg, ln_g, ln_b, ws, bs, chunks=4):
    t = pg.shape[0]
    gw = pg.shape[1] // 2
    n_heads = gw // HEAD_DIM
    tm = chunks * GMLP_CHUNK
    bst = jnp.zeros((GMLP_CHUNK, LANE), F32).at[:, :n_heads].set(bs.T)
    return pl.pallas_call(
        functools.partial(_gmlp_body, gw=gw, chunks=chunks),
        grid=(t // tm,),
        in_specs=[pl.BlockSpec((tm, 2 * gw), lambda i: (i, 0)), _full((1, gw)), _full((1, gw)),
                  _full((n_heads, GMLP_CHUNK, GMLP_CHUNK)), _full((GMLP_CHUNK, LANE))],
        out_specs=pl.BlockSpec((tm, gw), lambda i: (i, 0)),
        out_shape=jax.ShapeDtypeStruct((t, gw), F32),
        compiler_params=_cparams(1),
        name="gmlp",
    )(pg, ln_g.reshape(1, gw), ln_b.reshape(1, gw), ws.astype(BF16), bst)


def _halo_specs(tm, width, n_rows):
    per8 = tm // SUBLANE
    last = n_rows // SUBLANE - 1
    prev = pl.BlockSpec((SUBLANE, width), lambda i: (jnp.maximum(i * per8 - 1, 0), 0))
    nxt = pl.BlockSpec((SUBLANE, width), lambda i: (jnp.minimum((i + 1) * per8, last), 0))
    return prev, nxt


def _neighbours(cur, prev_blk, next_blk, tiles_per_seq):
    tm = cur.shape[0]
    j = pl.program_id(0) % tiles_per_seq
    prev_row = jnp.where(j > 0, prev_blk[SUBLANE - 1:SUBLANE, :], 0.0)
    next_row = jnp.where(j < tiles_per_seq - 1, next_blk[0:1, :], 0.0)
    ridx = lax.broadcasted_iota(jnp.int32, cur.shape, 0)
    before = jnp.where(ridx == 0, prev_row, pltpu.roll(cur, 1, 0))
    after = jnp.where(ridx == tm - 1, next_row, pltpu.roll(cur, tm - 1, 0))
    return before, after


def _rwkv_prep_body(pr_ref, prev_ref, next_ref, mu_ref, w0_ref, w2_ref, a0_ref, a2_ref, g2_ref,
                    kk_ref, ka_ref, rk_ref, bd_ref,
                    r_out, v_out, a_out, kd_out, b_out, lw_out, bonus_out, gate_out, *, rw, tiles_per_seq):
    pf = pr_ref[...]
    before, after = _neighbours(pf, prev_ref[...], next_ref[...], tiles_per_seq)
    pf = pf + mu_ref[0:1, :] * (before - pf) + mu_ref[1:2, :] * (after - pf)
    o3 = 3 * rw
    r, k, v = pf[:, :rw], pf[:, rw:2 * rw], pf[:, 2 * rw:o3]
    wd = pf[:, o3:o3 + W_LORA]
    ad = pf[:, o3 + W_LORA:o3 + W_LORA + A_LORA]
    gd = pf[:, o3 + W_LORA + A_LORA:]
    bd = bd_ref[...]
    kk = k * kk_ref[...]
    ss = _dot(kk * kk, bd, HI)
    kk = kk / jnp.maximum(jnp.sqrt(ss), 1e-12)
    twd = jnp.tanh(wd)
    ksum = jnp.zeros_like(k)
    for d in range(2):
        w_log = -_softplus(-(w0_ref[d:d + 1, :] + _dot(twd, w2_ref[d], HI))) - 0.5
        lw_out[d] = -jnp.exp(w_log)
        iclr = _sigmoid(a0_ref[d:d + 1, :] + _dot(ad, a2_ref[d], HI))
        kd = k * (1.0 + (iclr - 1.0) * ka_ref[...])
        kd_out[d] = kd
        b_out[d] = kk * iclr
        ksum = ksum + kd
    r_out[...] = r
    v_out[...] = v
    a_out[...] = -kk
    bonus_out[...] = _dot(r * ksum * rk_ref[...], bd, HI) * v
    gate_out[...] = _dot(_sigmoid(gd).astype(BF16), g2_ref[...])


def _rwkv_prep(pr, seq, mu, w0, w2, a0, a2, g2, k_k, k_a, r_k, tm=256):
    t, rproj = pr.shape
    rw = w0.shape[1]
    tiles_per_seq = seq // tm
    prev, nxt = _halo_specs(tm, rproj, t)
    row = pl.BlockSpec((tm, rw), lambda i: (i, 0))
    row2 = pl.BlockSpec((2, tm, rw), lambda i: (0, i, 0))
    one = jax.ShapeDtypeStruct((t, rw), F32)
    two = jax.ShapeDtypeStruct((2, t, rw), F32)
    return pl.pallas_call(
        functools.partial(_rwkv_prep_body, rw=rw, tiles_per_seq=tiles_per_seq),
        grid=(t // tm,),
        in_specs=[pl.BlockSpec((tm, rproj), lambda i: (i, 0)), prev, nxt,
                  _full((2, rproj)), _full((2, rw)), _full((2, W_LORA, rw)), _full((2, rw)),
                  _full((2, A_LORA, rw)), _full((G_LORA, rw)), _full((1, rw)), _full((1, rw)),
                  _full((1, rw)), _full((rw, rw))],
        out_specs=[row, row, row, row2, row2, row2, row, row],
        out_shape=[one, one, one, two, two, two, one, one],
        compiler_params=_cparams(1),
        name="rwkv_prep",
    )(pr, pr, pr, mu, w0, w2, a0, a2, g2.astype(BF16), k_k.reshape(1, rw), k_a.reshape(1, rw),
      r_k.reshape(1, rw), _block_diag_ones(rw))


def _split(x):
    hi = x.astype(BF16)
    return hi, (x - hi.astype(F32)).astype(BF16)


def _mm(a, b, mode, dot=_dot):
    if mode == "hi":
        return dot(a, b, HI)
    if mode == "b1":
        return dot(a.astype(BF16), b.astype(BF16))
    ah, al = _split(a)
    bh, bl = _split(b)
    return dot(ah, bh) + (dot(ah, bl) + dot(al, bh))


def _cumsum_rows(tri_bf16, x):
    hi = x.astype(BF16)
    r1 = x - hi.astype(F32)
    mid = r1.astype(BF16)
    lo = (r1 - mid.astype(F32)).astype(BF16)
    return _dot(tri_bf16, hi) + (_dot(tri_bf16, mid) + _dot(tri_bf16, lo))


P_G, P_INV, P_APPLY, P_STATE, P_SEQ = "b3", "b1", "b1", "b3", "b3"


def _rwkv_intra_body(r_ref, v_ref, a_ref, kd_ref, b_ref, lw_ref, rq_out, o0_out, mtx_out, hc_out,
                     *, n_heads, chunks):
    L = RWKV_CHUNK
    d = pl.program_id(0)
    row = lax.broadcasted_iota(jnp.int32, (L, L), 0)
    col = lax.broadcasted_iota(jnp.int32, (L, L), 1)
    fwd = d == 0
    rel = (col - row) * (1 - 2 * d)
    incl = rel <= 0
    strict = rel < 0
    eye = (row == col).astype(F32)
    tri = incl.astype(BF16)
    pairs = []
    for c in range(chunks):
        rows = slice(c * L, (c + 1) * L)
        lw = lw_ref[rows, :]
        cum = _cumsum_rows(tri, lw)
        tot = jnp.where(fwd, cum[L - 1:L, :], cum[0:1, :])
        e_neg = jnp.exp(-cum)
        e_end = jnp.exp(tot - cum)
        e_tot = jnp.exp(tot)
        r, v, a, kd, b = r_ref[rows, :], v_ref[rows, :], a_ref[rows, :], kd_ref[rows, :], b_ref[rows, :]
        at, rt, bt, kt = a * jnp.exp(cum - lw), r * jnp.exp(cum), b * e_neg, kd * e_neg
        kend, bend = kd * e_end, b * e_end
        for h in range(n_heads):
            sl = slice(h * HEAD_DIM, (h + 1) * HEAD_DIM)
            pairs.append(dict(at=at[:, sl], rt=rt[:, sl], bt=bt[:, sl], kt=kt[:, sl], v=v[:, sl],
                              kend=kend[:, sl], bend=bend[:, sl], e_tot=e_tot[:, sl]))
    for p in pairs:
        p["g"] = _mm(jnp.concatenate([p["at"], p["rt"]], axis=0),
                     jnp.concatenate([p["bt"], p["kt"]], axis=0), P_G, _dot_nt)
    for p in pairs:
        g = p.pop("g")
        p["pw"] = jnp.where(strict, g[:L, :L], 0.0)
        p["a_ak"] = jnp.where(strict, g[:L, L:], 0.0)
        p["m_rb"] = jnp.where(incl, g[L:, :L], 0.0)
        p["m_rk"] = jnp.where(incl, g[L:, L:], 0.0)
        p["inv"] = eye + p["pw"]
    for _ in range(int(math.log2(L)) - 1):
        for p in pairs:
            p["pw"] = _mm(p["pw"], p["pw"], P_INV)
        for p in pairs:
            p["inv"] = p["inv"] + _mm(p["inv"], p["pw"], P_INV)
    for p in pairs:
        p["akv"] = _mm(p["a_ak"], p["v"], P_APPLY)
    for p in pairs:
        p["wu"] = _mm(p["inv"], jnp.concatenate([p["at"], p["akv"]], axis=1), P_APPLY)
    for p in pairs:
        p["mwu"] = _mm(p["m_rb"], p["wu"], P_APPLY)
    for p in pairs:
        p["o0"] = p["mwu"][:, HEAD_DIM:] + _mm(p["m_rk"], p["v"], P_APPLY)
    for p in pairs:
        p["bw"] = _mm(p["bend"], p["wu"], P_STATE, _dot_tn)
    for p in pairs:
        p["hc"] = _mm(p["kend"], p["v"], P_STATE, _dot_tn) + p["bw"][:, HEAD_DIM:]
    for c in range(chunks):
        ps = pairs[c * n_heads:(c + 1) * n_heads]
        rows = slice(c * L, (c + 1) * L)
        krows = slice(c * HEAD_DIM, (c + 1) * HEAD_DIM)
        rq_out[rows, :] = jnp.concatenate([p["rt"] + p["mwu"][:, :HEAD_DIM] for p in ps], axis=1)
        o0_out[rows, :] = jnp.concatenate([p["o0"] for p in ps], axis=1)
        mtx_out[krows, :] = jnp.concatenate([eye * p["e_tot"] + p["bw"][:, :HEAD_DIM] for p in ps], axis=1)
        hc_out[krows, :] = jnp.concatenate([p["hc"] for p in ps], axis=1)


def _rwkv_intra(r, v, a, kd, b, lw, chunks=2):
    t, rw = r.shape
    n_heads = rw // HEAD_DIM
    tm = chunks * RWKV_CHUNK
    tk = chunks * HEAD_DIM
    n_tiles = t // tm
    one = pl.BlockSpec((tm, rw), lambda d, i: (i, 0))
    two = pl.BlockSpec((None, tm, rw), lambda d, i: (d, i, 0))
    twok = pl.BlockSpec((None, tk, rw), lambda d, i: (d, i, 0))
    return pl.pallas_call(
        functools.partial(_rwkv_intra_body, n_heads=n_heads, chunks=chunks),
        grid=(2, n_tiles),
        in_specs=[one, one, one, two, two, two],
        out_specs=[two, two, twok, twok],
        out_shape=[jax.ShapeDtypeStruct((2, t, rw), F32), jax.ShapeDtypeStruct((2, t, rw), F32),
                   jax.ShapeDtypeStruct((2, n_tiles * tk, rw), F32),
                   jax.ShapeDtypeStruct((2, n_tiles * tk, rw), F32)],
        compiler_params=_cparams(2),
        name="rwkv_intra",
    )(r, v, a, kd, b, lw)


def _rwkv_seq_body(rq0, o00, mtx0, hc0, rq1, o01, mtx1, hc1, out0, out1, h_ref, *, n_heads, batch):
    c = pl.program_id(0)

    @pl.when(c == 0)
    def _():
        h_ref[...] = jnp.zeros_like(h_ref)

    L = RWKV_CHUNK
    for d, (rq, o0, mtx, hc, out) in enumerate(((rq0, o00, mtx0, hc0, out0), (rq1, o01, mtx1, hc1, out1))):
        for bi in range(batch):
            rq_t, mtx_t = rq[bi], mtx[bi]
            state = h_ref[d, bi]
            outs, states = [], []
            for h in range(n_heads):
                sl = slice(h * HEAD_DIM, (h + 1) * HEAD_DIM)
                prod = _mm(jnp.concatenate([rq_t[:, sl], mtx_t[:, sl]], axis=0), state[:, sl], P_SEQ)
                outs.append(prod[:L])
                states.append(prod[L:])
            out[bi] = jnp.concatenate(outs, axis=1) + o0[bi]
            h_ref[d, bi] = jnp.concatenate(states, axis=1) + hc[bi]


def _rwkv_seq(rq, o0, mtx, hc, batch, seq):
    _, t, rw = rq.shape
    n_heads = rw // HEAD_DIM
    L = RWKV_CHUNK
    nc = seq // L
    as4 = lambda x: x.reshape(2, batch, x.shape[1] // batch, rw)
    rq, o0, mtx, hc = as4(rq), as4(o0), as4(mtx), as4(hc)
    fwd = lambda rows: pl.BlockSpec((None, batch, rows, rw), lambda c: (0, 0, c, 0))
    bwd = lambda rows: pl.BlockSpec((None, batch, rows, rw), lambda c: (1, 0, nc - 1 - c, 0))
    out0, out1 = pl.pallas_call(
        functools.partial(_rwkv_seq_body, n_heads=n_heads, batch=batch),
        grid=(nc,),
        in_specs=[fwd(L), fwd(L), fwd(HEAD_DIM), fwd(HEAD_DIM), bwd(L), bwd(L), bwd(HEAD_DIM), bwd(HEAD_DIM)],
        out_specs=[pl.BlockSpec((batch, L, rw), lambda c: (0, c, 0)),
                   pl.BlockSpec((batch, L, rw), lambda c: (0, nc - 1 - c, 0))],
        out_shape=[jax.ShapeDtypeStruct((batch, seq, rw), F32)] * 2,
        scratch_shapes=[pltpu.VMEM((2, batch, HEAD_DIM, rw), F32)],
        compiler_params=_cparams(1),
        name="rwkv_seq",
    )(rq, o0, mtx, hc, rq, o0, mtx, hc)
    return out0.reshape(t, rw), out1.reshape(t, rw)


def _rwkv_scan(r, v, a, kd, b, lw, batch, seq):
    rq, o0, mtx, hc = _rwkv_intra(r, v, a, kd, b, lw)
    return _rwkv_seq(rq, o0, mtx, hc, batch, seq)


def _mlstm_prep_body(qk_ref, prev_ref, next_ref, g_ref, cw_ref, cb_ref, gb_ref, q_out, k_out, gate_out,
                     *, mw, n_heads, tiles_per_seq):
    x = qk_ref[...]
    before, after = _neighbours(x, prev_ref[...], next_ref[...], tiles_per_seq)
    y = cb_ref[...] + before * cw_ref[0:1, :] + x * cw_ref[1:2, :] + after * cw_ref[2:3, :]
    y = y * _sigmoid(y)
    q_out[...] = y[:, :mw]
    k_out[...] = y[:, mw:] * (HEAD_DIM ** -0.5)
    g = g_ref[...] + gb_ref[...]
    lane = lax.broadcasted_iota(jnp.int32, g.shape, 1)
    for d in range(2):
        ig = g if d == 0 else pltpu.roll(g, LANE - n_heads, 1)
        fg = pltpu.roll(g, LANE - (1 + d) * n_heads, 1)
        lf = -_softplus(-fg)
        gate_out[d] = jnp.where(lane < n_heads, ig, jnp.where(lane < 2 * n_heads, lf, 0.0))


def _mlstm_prep(pm, seq, conv_w, conv_b, gate_b, mw, tm=256):
    t = pm.shape[0]
    n_heads = mw // HEAD_DIM
    tiles_per_seq = seq // tm
    w2 = 2 * mw
    prev, nxt = _halo_specs(tm, w2, t)
    gcol = (4 * mw) // LANE
    gb = jnp.zeros((1, LANE), F32).at[0, :4 * n_heads].set(gate_b)
    row = pl.BlockSpec((tm, mw), lambda i: (i, 0))
    return pl.pallas_call(
        functools.partial(_mlstm_prep_body, mw=mw, n_heads=n_heads, tiles_per_seq=tiles_per_seq),
        grid=(t // tm,),
        in_specs=[pl.BlockSpec((tm, w2), lambda i: (i, 0)), prev, nxt,
                  pl.BlockSpec((tm, LANE), lambda i: (i, gcol)),
                  _full((3, w2)), _full((1, w2)), _full((1, LANE))],
        out_specs=[row, row, pl.BlockSpec((2, tm, LANE), lambda i: (0, i, 0))],
        out_shape=[jax.ShapeDtypeStruct((t, mw), F32), jax.ShapeDtypeStruct((t, mw), F32),
                   jax.ShapeDtypeStruct((2, t, LANE), F32)],
        compiler_params=_cparams(1),
        name="mlstm_prep",
    )(pm, pm, pm, pm, conv_w, conv_b.reshape(1, w2), gb)


def _mlstm_scan_body(q_ref, k_ref, v_ref, g_ref, o_ref, c_ref, m_ref, *, n_heads):
    L = MLSTM_CHUNK
    d = pl.program_id(0)
    c = pl.program_id(2)

    @pl.when(c == 0)
    def _():
        c_ref[...] = jnp.zeros_like(c_ref)
        m_ref[...] = jnp.zeros_like(m_ref)

    row = lax.broadcasted_iota(jnp.int32, (L, L), 0)
    col = lax.broadcasted_iota(jnp.int32, (L, L), 1)
    fwd = d == 0
    incl = (col - row) * (1 - 2 * d) <= 0
    g = g_ref[...]
    bcum = _dot(incl.astype(F32), g, HI)
    g_t = g.T
    bcum_t = bcum.T
    lane64 = lax.broadcasted_iota(jnp.int32, (L, HEAD_DIM), 1)
    ones_col = (lane64 == 0).astype(F32)
    q, k, v = q_ref[...], k_ref[...], v_ref[...]
    hs = []
    for h in range(n_heads):
        sl = slice(h * HEAD_DIM, (h + 1) * HEAD_DIM)
        bc = bcum[:, n_heads + h:n_heads + h + 1]
        br = bcum_t[n_heads + h:n_heads + h + 1, :]
        igr = g_t[h:h + 1, :]
        igc = g[:, h:h + 1]
        m_st = m_ref[h:h + 1, 0:1]
        dm = jnp.where(incl, bc - br + igr, -jnp.inf)
        inter = bc + m_st
        m_t = jnp.maximum(jnp.max(dm, axis=1, keepdims=True), inter)
        b_last = jnp.where(fwd, bc[L - 1:L, :], bc[0:1, :])
        lwc = b_last - bc + igc
        m_new = jnp.maximum(b_last + m_st, jnp.max(lwc, axis=0, keepdims=True))
        hs.append(dict(qh=q[:, sl].astype(BF16), kh=k[:, sl],
                       vext=jnp.concatenate([v[:, sl], ones_col], axis=1).astype(BF16),
                       decay=jnp.exp(dm - m_t), w_inter=jnp.exp(inter - m_t), floor=jnp.exp(-m_t),
                       wts=jnp.exp(lwc - m_new), dec=jnp.exp(b_last + m_st - m_new), m_new=m_new,
                       cst=c_ref[h]))
    for p in hs:
        p["sc"] = (_dot_nt(p["qh"], p["kh"].astype(BF16)) * p["decay"]).astype(BF16)
    for p in hs:
        p["numext"] = _dot(p["sc"], p["vext"]) + p["w_inter"] * _dot(p["qh"], p["cst"].astype(BF16))
    for p in hs:
        p["upd"] = _dot_tn((p["wts"] * p["kh"]).astype(BF16), p["vext"])
    outs = []
    for h, p in enumerate(hs):
        num = p["numext"][:, :HEAD_DIM]
        den = p["numext"][:, HEAD_DIM:HEAD_DIM + 1]
        outs.append(num / jnp.maximum(jnp.abs(den), p["floor"]))
        c_ref[h] = p["dec"] * p["cst"] + p["upd"]
        m_ref[h:h + 1, :] = jnp.broadcast_to(p["m_new"], (1, LANE))
    o_ref[...] = jnp.concatenate(outs, axis=1)


def _mlstm_scan(q, k, pm, gates, batch, seq):
    t, mw = q.shape
    n_heads = mw // HEAD_DIM
    L = MLSTM_CHUNK
    nc = seq // L
    blk = lambda d, bi, c: bi * nc + jnp.where(d == 0, c, nc - 1 - c)
    one = pl.BlockSpec((L, mw), lambda d, bi, c: (blk(d, bi, c), 0))
    vspec = pl.BlockSpec((L, mw), lambda d, bi, c: (blk(d, bi, c), 2))
    return pl.pallas_call(
        functools.partial(_mlstm_scan_body, n_heads=n_heads),
        grid=(2, batch, nc),
        in_specs=[one, one, vspec, pl.BlockSpec((None, L, LANE), lambda d, bi, c: (d, blk(d, bi, c), 0))],
        out_specs=pl.BlockSpec((None, L, mw), lambda d, bi, c: (d, blk(d, bi, c), 0)),
        out_shape=jax.ShapeDtypeStruct((2, t, mw), F32),
        scratch_shapes=[pltpu.VMEM((n_heads, HEAD_DIM, LANE), F32), pltpu.VMEM((SUBLANE, LANE), F32)],
        compiler_params=_cparams(3),
        name="mlstm_scan",
    )(q, k, pm, gates)


def _layer_norm(x, g, b):
    mu = jnp.mean(x, axis=-1, keepdims=True)
    xc = x - mu
    var = jnp.mean(xc * xc, axis=-1, keepdims=True)
    return xc * lax.rsqrt(var + LN_EPS) * g + b


def _head_norm(x, bd_mean, eps):
    mu = _dot(x, bd_mean, HI)
    xc = x - mu
    var = _dot(xc * xc, bd_mean, HI)
    return xc * lax.rsqrt(var + eps)


def _mix_out_body(x_ref, yg_ref, ro0_ref, ro1_ref, bonus_ref, rgate_ref, rlg_ref, rlb_ref, mh_ref, og_ref, mlg_ref,
                  wg_ref, wr_ref, wm_ref, l1g_ref, l1b_ref, rw_ref, rb_ref, bdm_ref,
                  x1_out, topi_out, gate_out, *, alpha):
    bdm = bdm_ref[...]
    yr = _head_norm(ro0_ref[...] + ro1_ref[...], bdm, RWKV_GN_EPS) * rlg_ref[...] + rlb_ref[...]
    yr = (yr + bonus_ref[...]) * rgate_ref[...]
    ym = _sigmoid(og_ref[...]) * (_head_norm(mh_ref[0] + mh_ref[1], bdm, LN_EPS) * mlg_ref[...])
    mix = (_dot(yg_ref[...].astype(BF16), wg_ref[...]) + _dot(yr.astype(BF16), wr_ref[...])
           + _dot(ym.astype(BF16), wm_ref[...]))
    x1 = _layer_norm(alpha * x_ref[...] + mix, l1g_ref[...], l1b_ref[...])
    x1_out[...] = x1
    lg = _dot(x1, rw_ref[...], HI) + rb_ref[...]
    lane = lax.broadcasted_iota(jnp.int32, lg.shape, 1)
    vals, topi = [], jnp.zeros(lg.shape, jnp.int32)
    for j in range(TOP_K):
        mx = jnp.max(lg, axis=1, keepdims=True)
        idx = jnp.min(jnp.where(lg == mx, lane, LANE), axis=1, keepdims=True)
        vals.append(mx)
        topi = jnp.where(lane == j, idx, topi)
        lg = jnp.where(lane == idx, -jnp.inf, lg)
    es = [jnp.exp(vj - vals[0]) for vj in vals]
    den = es[0] + es[1] + es[2] + es[3]
    gate = jnp.zeros(lg.shape, F32)
    for j in range(TOP_K):
        gate = jnp.where(lane == j, es[j] / den, gate)
    topi_out[...] = topi
    gate_out[...] = gate


def _mix_out(x, yg, ro, bonus, rgate, rlg, rlb, mh, pm, mlg, w_out, l1g, l1b, router_w, router_b, alpha, tm=256):
    t, dm = x.shape
    gw, rw, mw = yg.shape[1], bonus.shape[1], mh.shape[2]
    assert rw == mw
    wb = w_out.astype(BF16)
    rwp = jnp.zeros((dm, LANE), F32).at[:, :N_EXPERTS].set(router_w)
    rbp = jnp.full((1, LANE), NEG_BIG, F32).at[0, :N_EXPERTS].set(router_b)
    row = lambda n: pl.BlockSpec((tm, n), lambda i: (i, 0))
    row2 = lambda n: pl.BlockSpec((2, tm, n), lambda i: (0, i, 0))
    vec = lambda n: _full((1, n))
    return pl.pallas_call(
        functools.partial(_mix_out_body, alpha=alpha),
        grid=(t // tm,),
        in_specs=[row(dm), row(gw), row(rw), row(rw), row(rw), row(rw), vec(rw), vec(rw), row2(mw),
                  pl.BlockSpec((tm, mw), lambda i: (i, 3)),
                  vec(mw), _full((gw, dm)), _full((rw, dm)), _full((mw, dm)), vec(dm), vec(dm),
                  _full((dm, LANE)), vec(LANE), _full((rw, rw))],
        out_specs=[row(dm), row(LANE), row(LANE)],
        out_shape=[jax.ShapeDtypeStruct((t, dm), F32),
                   jax.ShapeDtypeStruct((t, LANE), jnp.int32), jax.ShapeDtypeStruct((t, LANE), F32)],
        compiler_params=_cparams(1),
        name="mix_out",
    )(x, yg, ro[0], ro[1], bonus, rgate, rlg.reshape(1, rw), rlb.reshape(1, rw), mh, pm, mlg.reshape(1, mw),
      wb[:gw], wb[gw:gw + rw], wb[gw + rw:], l1g.reshape(1, dm), l1b.reshape(1, dm), rwp, rbp,
      _block_diag_ones(rw) / HEAD_DIM)


def _moe_body(be_ref, nu_ref, xs_ref, w1_ref, b1_ref, w2_ref, b2_ref, o_ref, *, dff):
    i = pl.program_id(0)

    @pl.when(i < nu_ref[0])
    def _():
        hdn = _dot(xs_ref[...].astype(BF16), w1_ref[...]) + b1_ref[...]
        g_ = jnp.minimum(hdn[:, :dff], SWIGLU_LIMIT)
        u_ = jnp.clip(hdn[:, dff:], -SWIGLU_LIMIT, SWIGLU_LIMIT)
        act = (u_ + 1.0) * (g_ * _sigmoid(g_ * SWIGLU_ALPHA))
        o_ref[...] = _dot(act.astype(BF16), w2_ref[...]) + b2_ref[...]

    @pl.when(i >= nu_ref[0])
    def _():
        o_ref[...] = jnp.zeros_like(o_ref)


def _moe_experts(xs, block_e, n_used, w1, b1, w2, b2):
    rows, dm = xs.shape
    nb = rows // MOE_BLOCK
    ne, _, dff2 = w1.shape
    dff = dff2 // 2
    grid_spec = pltpu.PrefetchScalarGridSpec(
        num_scalar_prefetch=2,
        grid=(nb,),
        in_specs=[pl.BlockSpec((MOE_BLOCK, dm), lambda i, be, nu: (i, 0)),
                  pl.BlockSpec((None, dm, dff2), lambda i, be, nu: (be[i], 0, 0)),
                  pl.BlockSpec((None, 1, dff2), lambda i, be, nu: (be[i], 0, 0)),
                  pl.BlockSpec((None, dff, dm), lambda i, be, nu: (be[i], 0, 0)),
                  pl.BlockSpec((None, 1, dm), lambda i, be, nu: (be[i], 0, 0))],
        out_specs=pl.BlockSpec((MOE_BLOCK, dm), lambda i, be, nu: (i, 0)),
    )
    return pl.pallas_call(
        functools.partial(_moe_body, dff=dff),
        grid_spec=grid_spec,
        out_shape=jax.ShapeDtypeStruct((rows, dm), F32),
        compiler_params=_cparams(1),
        name="moe_experts",
    )(block_e, n_used, xs, w1, b1.reshape(ne, 1, dff2), w2, b2.reshape(ne, 1, dm))


def _moe_plan(topi, n_tokens):
    na = n_tokens * TOP_K
    flat_e = topi[:, :TOP_K].reshape(na)
    onehot = (flat_e[:, None] == jnp.arange(N_EXPERTS, dtype=jnp.int32)[None, :]).astype(jnp.int32)
    csum = jnp.cumsum(onehot, axis=0)
    counts = csum[-1]
    rank = jnp.sum((csum - onehot) * onehot, axis=1)
    pcounts = (counts + MOE_BLOCK - 1) // MOE_BLOCK * MOE_BLOCK
    pend = jnp.cumsum(pcounts)
    pstart = pend - pcounts
    dest = (pstart[flat_e] + rank).astype(jnp.int32)
    nb = -(-na // MOE_BLOCK) + N_EXPERTS
    flat_t = jnp.arange(na, dtype=jnp.int32) // TOP_K
    row_t = jnp.zeros((nb * MOE_BLOCK,), jnp.int32).at[dest].set(flat_t)
    block_e = jnp.minimum(jnp.searchsorted(pend, jnp.arange(nb, dtype=jnp.int32) * MOE_BLOCK, side='right'),
                          N_EXPERTS - 1).astype(jnp.int32)
    n_used = (pend[-1] // MOE_BLOCK).astype(jnp.int32).reshape(1)
    return dest, row_t, block_e, n_used


def _combine_body(x1_ref, y0_ref, y1_ref, y2_ref, y3_ref, gate_ref, g_ref, b_ref, o_ref, *, alpha):
    gate = gate_ref[...]
    ffn = gate[:, 0:1] * y0_ref[...]
    for j, y_ref in enumerate((y1_ref, y2_ref, y3_ref), start=1):
        ffn = ffn + gate[:, j:j + 1] * y_ref[...]
    o_ref[...] = _layer_norm(alpha * x1_ref[...] + ffn, g_ref[...], b_ref[...])


def _combine(x1, yg, gate, ln_g, ln_b, alpha, tm=256):
    t, dm = x1.shape
    n_tiles = t // tm
    expert_rows = lambda j: pl.BlockSpec((tm, dm), lambda i: (i + j * n_tiles, 0))
    return pl.pallas_call(
        functools.partial(_combine_body, alpha=alpha),
        grid=(n_tiles,),
        in_specs=[pl.BlockSpec((tm, dm), lambda i: (i, 0))] + [expert_rows(j) for j in range(TOP_K)]
                 + [pl.BlockSpec((tm, LANE), lambda i: (i, 0)), _full((1, dm)), _full((1, dm))],
        out_specs=pl.BlockSpec((tm, dm), lambda i: (i, 0)),
        out_shape=jax.ShapeDtypeStruct((t, dm), F32),
        compiler_params=_cparams(1),
        name="combine_ln",
    )(x1, yg, yg, yg, yg, gate, ln_g.reshape(1, dm), ln_b.reshape(1, dm))


SC_CORES = 2
SC_SUBCORES = 16
SC_WORKERS = SC_CORES * SC_SUBCORES


def _sc_gather_rows(table, idx, window):
    n = idx.shape[0]
    dim = table.shape[1]
    n_steps = n // (SC_WORKERS * window)
    assert n_steps * window * SC_WORKERS == n and n_steps % 2 == 0 and window % SUBLANE == 0 and window <= LANE
    idx3 = idx.reshape(SC_WORKERS, n_steps, window)
    mesh = plsc.VectorSubcoreMesh(core_axis_name="c", subcore_axis_name="s",
                                  num_cores=SC_CORES, num_subcores=SC_SUBCORES)

    def body(table_hbm, idx_hbm, out_hbm, idx_v, rows_v, gsem, wsem):
        wid = lax.axis_index("s") * SC_CORES + lax.axis_index("c")
        pltpu.sync_copy(idx_hbm.at[wid], idx_v)

        def gather(j, buf):
            return pltpu.make_async_copy(table_hbm.at[idx_v.at[j]], rows_v.at[buf], gsem.at[buf])

        def write(j, buf):
            base = pl.multiple_of((wid * n_steps + j) * window, window)
            return pltpu.make_async_copy(rows_v.at[buf], out_hbm.at[pl.ds(base, window)], wsem.at[buf])

        gather(0, 0).start()

        @pl.loop(0, n_steps, step=2)
        def _(j0):
            for buf in range(2):
                j = j0 + buf
                gather(j, buf).wait()

                @pl.when(j >= 1)
                def _():
                    write(j - 1, 1 - buf).wait()

                @pl.when(j + 1 < n_steps)
                def _():
                    gather(j + 1, 1 - buf).start()

                write(j, buf).start()

        write(n_steps - 1, 1).wait()

    return pl.kernel(
        body, out_type=jax.ShapeDtypeStruct((n, dim), table.dtype), mesh=mesh,
        scratch_types=[pltpu.VMEM((n_steps, window), jnp.int32), pltpu.VMEM((2, window, dim), table.dtype),
                       pltpu.SemaphoreType.DMA((2,)), pltpu.SemaphoreType.DMA((2,))],
        name="sc_gather",
    )(table, idx3)


def _pad_cols(w, width):
    return jnp.pad(w, ((0, 0), (0, width - w.shape[1])))


def kernel(x, w_in, gmlp_ln_g, gmlp_ln_b, gmlp_ws, gmlp_bs, rwkv_mu, rwkv_w0, rwkv_w2, rwkv_a0, rwkv_a2, rwkv_g2, rwkv_k_k, rwkv_k_a, rwkv_r_k, rwkv_ln_g, rwkv_ln_b, mlstm_conv_w, mlstm_conv_b, mlstm_gate_b, mlstm_ln_g, w_out, ln1_g, ln1_b, router_w, router_b, exp_w1, exp_b1, exp_w2, exp_b2, ln2_g, ln2_b):
    batch, seq, dm = x.shape
    depth = w_in.shape[0]
    t = batch * seq
    gw = gmlp_ln_g.shape[1]
    rw = rwkv_w0.shape[2]
    mw = mlstm_ln_g.shape[1]
    g_proj = 2 * gw
    r_proj = 3 * rw + W_LORA + A_LORA + G_LORA
    m_proj = w_in.shape[2] - g_proj - r_proj
    m_pad = -(-m_proj // LANE) * LANE
    alpha = (2 * depth) ** 0.25
    xf = x.reshape(t, dm)
    for l in range(depth):
        wl = w_in[l].astype(BF16)
        pg, pr, pm = _proj(xf, wl[:, :g_proj], wl[:, g_proj:g_proj + r_proj],
                           _pad_cols(wl[:, g_proj + r_proj:], m_pad))
        y_g = _gmlp(pg, gmlp_ln_g[l], gmlp_ln_b[l], gmlp_ws[l], gmlp_bs[l])
        r, v, a, kd, b, lw, bonus, rgate = _rwkv_prep(
            pr, seq, rwkv_mu[l], rwkv_w0[l], rwkv_w2[l], rwkv_a0[l], rwkv_a2[l], rwkv_g2[l],
            rwkv_k_k[l], rwkv_k_a[l], rwkv_r_k[l].reshape(-1))
        ro = _rwkv_scan(r, v, a, kd, b, lw, batch, seq)
        q, k, gates = _mlstm_prep(pm, seq, mlstm_conv_w[l], mlstm_conv_b[l], mlstm_gate_b[l], mw)
        mh = _mlstm_scan(q, k, pm, gates, batch, seq)
        x1, topi, gate = _mix_out(xf, y_g, ro, bonus, rgate, rwkv_ln_g[l], rwkv_ln_b[l], mh, pm,
                                       mlstm_ln_g[l], w_out[l], ln1_g[l], ln1_b[l], router_w[l], router_b[l],
                                       alpha)
        dest, row_t, block_e, n_used = _moe_plan(topi, t)
        xs = _sc_gather_rows(x1, row_t, window=32)
        ys = _moe_experts(xs, block_e, n_used, exp_w1[l].astype(BF16), exp_b1[l], exp_w2[l].astype(BF16),
                          exp_b2[l])
        yg = _sc_gather_rows(ys, dest.reshape(t, TOP_K).T.reshape(-1), window=32)
        xf = _combine(x1, yg, gate, ln2_g[l], ln2_b[l], alpha)
    return xf.reshape(batch, seq, dm)
```

```python
import functools
import math

import jax
import jax.numpy as jnp
from jax import lax
from jax.experimental import pallas as pl
from jax.experimental.pallas import tpu as pltpu
from jax.experimental.pallas import tpu_sc as plsc

F32 = jnp.float32
BF16 = jnp.bfloat16
HI = lax.Precision.HIGHEST

HEAD_DIM = 64
GMLP_CHUNK = 128
MLSTM_CHUNK = 128
RWKV_CHUNK = 64
W_LORA = 64
A_LORA = 64
G_LORA = 128
N_EXPERTS = 32
TOP_K = 4
MOE_BLOCK = 256
SWIGLU_LIMIT = 7.0
SWIGLU_ALPHA = 1.702
LN_EPS = 1e-5
RWKV_GN_EPS = 64e-5
LANE = 128
SUBLANE = 8
VMEM_LIMIT = 48 * 1024 * 1024
NEG_BIG = -1e30


def _cparams(n_axes):
    return pltpu.CompilerParams(dimension_semantics=("arbitrary",) * n_axes,
                                vmem_limit_bytes=VMEM_LIMIT)


def _full(shape):
    return pl.BlockSpec(shape, lambda *_: (0,) * len(shape))


def _dot(a, b, precision=None):
    return jnp.dot(a, b, preferred_element_type=F32, precision=precision)


def _dot_nt(a, b, precision=None):
    return lax.dot_general(a, b, (((1,), (1,)), ((), ())), preferred_element_type=F32, precision=precision)


def _dot_tn(a, b, precision=None):
    return lax.dot_general(a, b, (((0,), (0,)), ((), ())), preferred_element_type=F32, precision=precision)


def _split(x):
    hi = x.astype(BF16)
    return hi, (x - hi.astype(F32)).astype(BF16)


def _split3(x):
    hi = x.astype(BF16)
    r1 = x - hi.astype(F32)
    mid = r1.astype(BF16)
    return hi, mid, (r1 - mid.astype(F32)).astype(BF16)


def _mm(a, b, mode, dot=_dot):
    if mode == "hi":
        return dot(a, b, HI)
    if mode == "b1":
        return dot(a.astype(BF16), b.astype(BF16))
    ah, al = _split(a)
    bh, bl = _split(b)
    return dot(ah, bh) + (dot(ah, bl) + dot(al, bh))


def _dot_exact_lhs(a_bf16, x):
    hi, mid, lo = _split3(x)
    return _dot(a_bf16, hi) + (_dot(a_bf16, mid) + _dot(a_bf16, lo))


def _dot_exact_rhs(x, b_bf16):
    hi, mid, lo = _split3(x)
    return _dot(hi, b_bf16) + (_dot(mid, b_bf16) + _dot(lo, b_bf16))


def _sigmoid(x):
    return 1.0 / (1.0 + jnp.exp(-x))


def _softplus(x):
    return jnp.maximum(x, 0.0) + jnp.log1p(jnp.exp(-jnp.abs(x)))


def _block_diag_ones(width):
    h = jnp.arange(width) // HEAD_DIM
    return (h[:, None] == h[None, :]).astype(F32)


CAST_ROWS = 128


def _cast_rows(src_ref, dst_ref):
    n_src, n_dst = src_ref.shape[1], dst_ref.shape[1]
    whole = n_src // LANE * LANE

    def step(r, carry):
        rows = pl.ds(pl.multiple_of(r * CAST_ROWS, CAST_ROWS), CAST_ROWS)
        dst_ref[rows, :whole] = src_ref[rows, :whole].astype(BF16)
        if n_dst > whole:
            tail = [src_ref[rows, whole:]] if n_src > whole else []
            tail.append(jnp.zeros((CAST_ROWS, n_dst - n_src), F32))
            dst_ref[rows, whole:] = jnp.concatenate(tail, axis=1).astype(BF16)
        return carry
    lax.fori_loop(0, src_ref.shape[0] // CAST_ROWS, step, 0)


def _proj_body(x_ref, w_ref, pg_ref, pr_ref, pm_ref, wb_ref, *, ng, nr):
    @pl.when(pl.program_id(0) == 0)
    def _():
        _cast_rows(w_ref, wb_ref)

    xb = x_ref[...].astype(BF16)
    pg_ref[...] = _dot(xb, wb_ref[:, :ng])
    pr_ref[...] = _dot(xb, wb_ref[:, ng:ng + nr])
    pm_ref[...] = _dot(xb, wb_ref[:, ng + nr:])


def _proj(x, w_in, layer, ng, nr, tm=256):
    t, d = x.shape
    p_in = w_in.shape[2]
    p_pad = -(-p_in // LANE) * LANE
    nm = p_pad - ng - nr
    row = lambda n: pl.BlockSpec((tm, n), lambda i: (i, 0))
    return pl.pallas_call(
        functools.partial(_proj_body, ng=ng, nr=nr),
        grid=(t // tm,),
        in_specs=[row(d), pl.BlockSpec((None, d, p_in), lambda i: (layer, 0, 0), pipeline_mode=pl.Buffered(1))],
        out_specs=[row(ng), row(nr), row(nm)],
        out_shape=[jax.ShapeDtypeStruct((t, n), F32) for n in (ng, nr, nm)],
        scratch_shapes=[pltpu.VMEM((d, p_pad), BF16)],
        compiler_params=_cparams(1),
        name="in_proj",
    )(x, w_in)


def _gmlp_body(pg_ref, lng_ref, lnb_ref, ws_ref, bst_ref, o_ref, *, gw, chunks):
    p = pg_ref[...]
    p = 0.5 * p * (1.0 + lax.erf(p * math.sqrt(0.5)))
    u, v = p[:, :gw], p[:, gw:]
    mu = jnp.mean(v, axis=-1, keepdims=True)
    vc = v - mu
    var = jnp.mean(vc * vc, axis=-1, keepdims=True)
    vn = vc * lax.rsqrt(var + LN_EPS) * lng_ref[...] + lnb_ref[...]
    n_heads = gw // HEAD_DIM
    for c in range(chunks):
        rows = slice(c * GMLP_CHUNK, (c + 1) * GMLP_CHUNK)
        ys = []
        for h in range(n_heads):
            cols = slice(h * HEAD_DIM, (h + 1) * HEAD_DIM)
            y = _dot(ws_ref[h], vn[rows, cols].astype(BF16)) + bst_ref[:, h:h + 1]
            ys.append(y)
        o_ref[rows, :] = u[rows, :] * jnp.concatenate(ys, axis=1)


def _gmlp(pg, ln_g, ln_b, ws, bs, chunks=4):
    t = pg.shape[0]
    gw = pg.shape[1] // 2
    n_heads = gw // HEAD_DIM
    tm = chunks * GMLP_CHUNK
    bst = jnp.zeros((GMLP_CHUNK, LANE), F32).at[:, :n_heads].set(bs.T)
    return pl.pallas_call(
        functools.partial(_gmlp_body, gw=gw, chunks=chunks),
        grid=(t // tm,),
        in_specs=[pl.BlockSpec((tm, 2 * gw), lambda i: (i, 0)), _full((1, gw)), _full((1, gw)),
                  _full((n_heads, GMLP_CHUNK, GMLP_CHUNK)), _full((GMLP_CHUNK, LANE))],
        out_specs=pl.BlockSpec((tm, gw), lambda i: (i, 0)),
        out_shape=jax.ShapeDtypeStruct((t, gw), F32),
        compiler_params=_cparams(1),
        name="gmlp",
    )(pg, ln_g.reshape(1, gw), ln_b.reshape(1, gw), ws.astype(BF16), bst)


def _halo_specs(tm, width, n_rows):
    per8 = tm // SUBLANE
    last = n_rows // SUBLANE - 1
    prev = pl.BlockSpec((SUBLANE, width), lambda i: (jnp.maximum(i * per8 - 1, 0), 0))
    nxt = pl.BlockSpec((SUBLANE, width), lambda i: (jnp.minimum((i + 1) * per8, last), 0))
    return prev, nxt


def _neighbours(cur, prev_blk, next_blk, tiles_per_seq):
    tm = cur.shape[0]
    j = pl.program_id(0) % tiles_per_seq
    prev_row = jnp.where(j > 0, prev_blk[SUBLANE - 1:SUBLANE, :], 0.0)
    next_row = jnp.where(j < tiles_per_seq - 1, next_blk[0:1, :], 0.0)
    ridx = lax.broadcasted_iota(jnp.int32, cur.shape, 0)
    before = jnp.where(ridx == 0, prev_row, pltpu.roll(cur, 1, 0))
    after = jnp.where(ridx == tm - 1, next_row, pltpu.roll(cur, tm - 1, 0))
    return before, after


def _rwkv_prep_body(pr_ref, prev_ref, next_ref, mu_ref, w0_ref, w2_ref, a0_ref, a2_ref, g2_ref,
                    kk_ref, ka_ref, rk_ref, bd_ref,
                    r_out, v_out, a_out, kd_out, b_out, lw_out, bonus_out, gate_out, *, rw, tiles_per_seq):
    pf = pr_ref[...]
    before, after = _neighbours(pf, prev_ref[...], next_ref[...], tiles_per_seq)
    pf = pf + mu_ref[0:1, :] * (before - pf) + mu_ref[1:2, :] * (after - pf)
    o3 = 3 * rw
    r, k, v = pf[:, :rw], pf[:, rw:2 * rw], pf[:, 2 * rw:o3]
    wd = pf[:, o3:o3 + W_LORA]
    ad = pf[:, o3 + W_LORA:o3 + W_LORA + A_LORA]
    gd = pf[:, o3 + W_LORA + A_LORA:]
    bd = bd_ref[...]
    kk = k * kk_ref[...]
    ss = _dot_exact_rhs(kk * kk, bd)
    kk = kk / jnp.maximum(jnp.sqrt(ss), 1e-12)
    twd = jnp.tanh(wd)
    ksum = jnp.zeros_like(k)
    for d in range(2):
        w_log = -_softplus(-(w0_ref[d:d + 1, :] + _mm(twd, w2_ref[d], "b3"))) - 0.5
        lw_out[d] = -jnp.exp(w_log)
        iclr = _sigmoid(a0_ref[d:d + 1, :] + _mm(ad, a2_ref[d], "b3"))
        kd = k * (1.0 + (iclr - 1.0) * ka_ref[...])
        kd_out[d] = kd
        b_out[d] = kk * iclr
        ksum = ksum + kd
    r_out[...] = r
    v_out[...] = v
    a_out[...] = -kk
    bonus_out[...] = _dot_exact_rhs(r * ksum * rk_ref[...], bd) * v
    gate_out[...] = _dot(_sigmoid(gd).astype(BF16), g2_ref[...])


def _rwkv_prep(pr, seq, mu, w0, w2, a0, a2, g2, k_k, k_a, r_k, tm=256):
    t, rproj = pr.shape
    rw = w0.shape[1]
    tiles_per_seq = seq // tm
    prev, nxt = _halo_specs(tm, rproj, t)
    row = pl.BlockSpec((tm, rw), lambda i: (i, 0))
    row2 = pl.BlockSpec((2, tm, rw), lambda i: (0, i, 0))
    one = jax.ShapeDtypeStruct((t, rw), F32)
    two = jax.ShapeDtypeStruct((2, t, rw), F32)
    return pl.pallas_call(
        functools.partial(_rwkv_prep_body, rw=rw, tiles_per_seq=tiles_per_seq),
        grid=(t // tm,),
        in_specs=[pl.BlockSpec((tm, rproj), lambda i: (i, 0)), prev, nxt,
                  _full((2, rproj)), _full((2, rw)), _full((2, W_LORA, rw)), _full((2, rw)),
                  _full((2, A_LORA, rw)), _full((G_LORA, rw)), _full((1, rw)), _full((1, rw)),
                  _full((1, rw)), _full((rw, rw))],
        out_specs=[row, row, row, row2, row2, row2, row, row],
        out_shape=[one, one, one, two, two, two, one, one],
        compiler_params=_cparams(1),
        name="rwkv_prep",
    )(pr, pr, pr, mu, w0, w2, a0, a2, g2.astype(BF16), k_k.reshape(1, rw), k_a.reshape(1, rw),
      r_k.reshape(1, rw), _block_diag_ones(rw).astype(BF16))


P_G, P_INV, P_APPLY, P_STATE, P_SEQ = "b3", "b1", "b1", "b3", "b3"


def _rwkv_intra_body(r_ref, v_ref, a_ref, kd_ref, b_ref, lw_ref, rq_out, o0_out, mtx_out, hc_out,
                     *, n_heads, chunks):
    L = RWKV_CHUNK
    d = pl.program_id(0)
    row = lax.broadcasted_iota(jnp.int32, (L, L), 0)
    col = lax.broadcasted_iota(jnp.int32, (L, L), 1)
    fwd = d == 0
    rel = (col - row) * (1 - 2 * d)
    incl = rel <= 0
    strict = rel < 0
    eye = (row == col).astype(F32)
    tri = incl.astype(BF16)
    pairs = []
    for c in range(chunks):
        rows = slice(c * L, (c + 1) * L)
        lw = lw_ref[rows, :]
        cum = _dot_exact_lhs(tri, lw)
        tot = jnp.where(fwd, cum[L - 1:L, :], cum[0:1, :])
        e_neg = jnp.exp(-cum)
        e_end = jnp.exp(tot - cum)
        e_tot = jnp.exp(tot)
        r, v, a, kd, b = r_ref[rows, :], v_ref[rows, :], a_ref[rows, :], kd_ref[rows, :], b_ref[rows, :]
        at, rt, bt, kt = a * jnp.exp(cum - lw), r * jnp.exp(cum), b * e_neg, kd * e_neg
        kend, bend = kd * e_end, b * e_end
        for h in range(n_heads):
            sl = slice(h * HEAD_DIM, (h + 1) * HEAD_DIM)
            pairs.append(dict(at=at[:, sl], rt=rt[:, sl], bt=bt[:, sl], kt=kt[:, sl], v=v[:, sl],
                              kend=kend[:, sl], bend=bend[:, sl], e_tot=e_tot[:, sl]))
    for p in pairs:
        p["g"] = _mm(jnp.concatenate([p["at"], p["rt"]], axis=0),
                     jnp.concatenate([p["bt"], p["kt"]], axis=0), P_G, _dot_nt)
    for p in pairs:
        g = p.pop("g")
        p["pw"] = jnp.where(strict, g[:L, :L], 0.0)
        p["a_ak"] = jnp.where(strict, g[:L, L:], 0.0)
        p["m_rb"] = jnp.where(incl, g[L:, :L], 0.0)
        p["m_rk"] = jnp.where(incl, g[L:, L:], 0.0)
        p["inv"] = eye + p["pw"]
    for _ in range(int(math.log2(L)) - 1):
        for p in pairs:
            p["pw"] = _mm(p["pw"], p["pw"], P_INV)
        for p in pairs:
            p["inv"] = p["inv"] + _mm(p["inv"], p["pw"], P_INV)
    for p in pairs:
        p["akv"] = _mm(p["a_ak"], p["v"], P_APPLY)
    for p in pairs:
        p["wu"] = _mm(p["inv"], jnp.concatenate([p["at"], p["akv"]], axis=1), P_APPLY)
    for p in pairs:
        p["mwu"] = _mm(p["m_rb"], p["wu"], P_APPLY)
    for p in pairs:
        p["o0"] = p["mwu"][:, HEAD_DIM:] + _mm(p["m_rk"], p["v"], P_APPLY)
    for p in pairs:
        p["bw"] = _mm(p["bend"], p["wu"], P_STATE, _dot_tn)
    for p in pairs:
        p["hc"] = _mm(p["kend"], p["v"], P_STATE, _dot_tn) + p["bw"][:, HEAD_DIM:]
    for c in range(chunks):
        ps = pairs[c * n_heads:(c + 1) * n_heads]
        rows = slice(c * L, (c + 1) * L)
        krows = slice(c * HEAD_DIM, (c + 1) * HEAD_DIM)
        rq_out[rows, :] = jnp.concatenate([p["rt"] + p["mwu"][:, :HEAD_DIM] for p in ps], axis=1)
        o0_out[rows, :] = jnp.concatenate([p["o0"] for p in ps], axis=1)
        mtx_out[krows, :] = jnp.concatenate([eye * p["e_tot"] + p["bw"][:, :HEAD_DIM] for p in ps], axis=1)
        hc_out[krows, :] = jnp.concatenate([p["hc"] for p in ps], axis=1)


def _rwkv_intra(r, v, a, kd, b, lw, chunks=2):
    t, rw = r.shape
    n_heads = rw // HEAD_DIM
    tm = chunks * RWKV_CHUNK
    tk = chunks * HEAD_DIM
    n_tiles = t // tm
    one = pl.BlockSpec((tm, rw), lambda d, i: (i, 0))
    two = pl.BlockSpec((None, tm, rw), lambda d, i: (d, i, 0))
    twok = pl.BlockSpec((None, tk, rw), lambda d, i: (d, i, 0))
    return pl.pallas_call(
        functools.partial(_rwkv_intra_body, n_heads=n_heads, chunks=chunks),
        grid=(2, n_tiles),
        in_specs=[one, one, one, two, two, two],
        out_specs=[two, two, twok, twok],
        out_shape=[jax.ShapeDtypeStruct((2, t, rw), F32), jax.ShapeDtypeStruct((2, t, rw), F32),
                   jax.ShapeDtypeStruct((2, n_tiles * tk, rw), F32),
                   jax.ShapeDtypeStruct((2, n_tiles * tk, rw), F32)],
        compiler_params=_cparams(2),
        name="rwkv_intra",
    )(r, v, a, kd, b, lw)


def _rwkv_seq_body(rq0, o00, mtx0, hc0, rq1, o01, mtx1, hc1, out0, out1, h_ref, *, n_heads, batch):
    c = pl.program_id(0)

    @pl.when(c == 0)
    def _():
        h_ref[...] = jnp.zeros_like(h_ref)

    L = RWKV_CHUNK
    for d, (rq, o0, mtx, hc, out) in enumerate(((rq0, o00, mtx0, hc0, out0), (rq1, o01, mtx1, hc1, out1))):
        for bi in range(batch):
            rq_t, mtx_t = rq[bi], mtx[bi]
            state = h_ref[d, bi]
            outs, states = [], []
            for h in range(n_heads):
                sl = slice(h * HEAD_DIM, (h + 1) * HEAD_DIM)
                prod = _mm(jnp.concatenate([rq_t[:, sl], mtx_t[:, sl]], axis=0), state[:, sl], P_SEQ)
                outs.append(prod[:L])
                states.append(prod[L:])
            out[bi] = jnp.concatenate(outs, axis=1) + o0[bi]
            h_ref[d, bi] = jnp.concatenate(states, axis=1) + hc[bi]


def _rwkv_seq(rq, o0, mtx, hc, batch, seq):
    _, t, rw = rq.shape
    n_heads = rw // HEAD_DIM
    L = RWKV_CHUNK
    nc = seq // L
    as4 = lambda x: x.reshape(2, batch, x.shape[1] // batch, rw)
    rq, o0, mtx, hc = as4(rq), as4(o0), as4(mtx), as4(hc)
    fwd = lambda rows: pl.BlockSpec((None, batch, rows, rw), lambda c: (0, 0, c, 0))
    bwd = lambda rows: pl.BlockSpec((None, batch, rows, rw), lambda c: (1, 0, nc - 1 - c, 0))
    out0, out1 = pl.pallas_call(
        functools.partial(_rwkv_seq_body, n_heads=n_heads, batch=batch),
        grid=(nc,),
        in_specs=[fwd(L), fwd(L), fwd(HEAD_DIM), fwd(HEAD_DIM), bwd(L), bwd(L), bwd(HEAD_DIM), bwd(HEAD_DIM)],
        out_specs=[pl.BlockSpec((batch, L, rw), lambda c: (0, c, 0)),
                   pl.BlockSpec((batch, L, rw), lambda c: (0, nc - 1 - c, 0))],
        out_shape=[jax.ShapeDtypeStruct((batch, seq, rw), F32)] * 2,
        scratch_shapes=[pltpu.VMEM((2, batch, HEAD_DIM, rw), F32)],
        compiler_params=_cparams(1),
        name="rwkv_seq",
    )(rq, o0, mtx, hc, rq, o0, mtx, hc)
    return out0.reshape(t, rw), out1.reshape(t, rw)


def _rwkv_scan(r, v, a, kd, b, lw, batch, seq):
    rq, o0, mtx, hc = _rwkv_intra(r, v, a, kd, b, lw)
    return _rwkv_seq(rq, o0, mtx, hc, batch, seq)


def _mlstm_prep_body(qk_ref, prev_ref, next_ref, g_ref, cw_ref, cb_ref, gb_ref, q_out, k_out, gate_out,
                     *, mw, n_heads, tiles_per_seq):
    x = qk_ref[...]
    before, after = _neighbours(x, prev_ref[...], next_ref[...], tiles_per_seq)
    y = cb_ref[...] + before * cw_ref[0:1, :] + x * cw_ref[1:2, :] + after * cw_ref[2:3, :]
    y = y * _sigmoid(y)
    q_out[...] = y[:, :mw]
    k_out[...] = y[:, mw:] * (HEAD_DIM ** -0.5)
    g = g_ref[...] + gb_ref[...]
    lane = lax.broadcasted_iota(jnp.int32, g.shape, 1)
    for d in range(2):
        ig = g if d == 0 else pltpu.roll(g, LANE - n_heads, 1)
        fg = pltpu.roll(g, LANE - (1 + d) * n_heads, 1)
        lf = -_softplus(-fg)
        gate_out[d] = jnp.where(lane < n_heads, ig, jnp.where(lane < 2 * n_heads, lf, 0.0))


def _mlstm_prep(pm, seq, conv_w, conv_b, gate_b, mw, tm=256):
    t = pm.shape[0]
    n_heads = mw // HEAD_DIM
    tiles_per_seq = seq // tm
    w2 = 2 * mw
    prev, nxt = _halo_specs(tm, w2, t)
    gcol = (4 * mw) // LANE
    gb = jnp.zeros((1, LANE), F32).at[0, :4 * n_heads].set(gate_b)
    row = pl.BlockSpec((tm, mw), lambda i: (i, 0))
    return pl.pallas_call(
        functools.partial(_mlstm_prep_body, mw=mw, n_heads=n_heads, tiles_per_seq=tiles_per_seq),
        grid=(t // tm,),
        in_specs=[pl.BlockSpec((tm, w2), lambda i: (i, 0)), prev, nxt,
                  pl.BlockSpec((tm, LANE), lambda i: (i, gcol)),
                  _full((3, w2)), _full((1, w2)), _full((1, LANE))],
        out_specs=[row, row, pl.BlockSpec((2, tm, LANE), lambda i: (0, i, 0))],
        out_shape=[jax.ShapeDtypeStruct((t, mw), F32), jax.ShapeDtypeStruct((t, mw), F32),
                   jax.ShapeDtypeStruct((2, t, LANE), F32)],
        compiler_params=_cparams(1),
        name="mlstm_prep",
    )(pm, pm, pm, pm, conv_w, conv_b.reshape(1, w2), gb)


def _mlstm_scan_body(q0_ref, k0_ref, v0_ref, g0_ref, q1_ref, k1_ref, v1_ref, g1_ref, o0_ref, o1_ref,
                     c_ref, m_ref, *, n_heads):
    L = MLSTM_CHUNK

    @pl.when(pl.program_id(1) == 0)
    def _():
        c_ref[...] = jnp.zeros_like(c_ref)
        m_ref[...] = jnp.zeros_like(m_ref)

    row = lax.broadcasted_iota(jnp.int32, (L, L), 0)
    col = lax.broadcasted_iota(jnp.int32, (L, L), 1)
    lane64 = lax.broadcasted_iota(jnp.int32, (L, HEAD_DIM), 1)
    ones_col = (lane64 == 0).astype(F32)
    hs = []
    for d, (q_ref, k_ref, v_ref, g_ref) in enumerate(((q0_ref, k0_ref, v0_ref, g0_ref),
                                                      (q1_ref, k1_ref, v1_ref, g1_ref))):
        incl = (col <= row) if d == 0 else (col >= row)
        last = L - 1 if d == 0 else 0
        g = g_ref[...]
        bcum = _dot_exact_lhs(incl.astype(BF16), g)
        g_t = g.T
        bcum_t = bcum.T
        q, k, v = q_ref[...], k_ref[...], v_ref[...]
        for h in range(n_heads):
            sl = slice(h * HEAD_DIM, (h + 1) * HEAD_DIM)
            bc = bcum[:, n_heads + h:n_heads + h + 1]
            br = bcum_t[n_heads + h:n_heads + h + 1, :]
            igr = g_t[h:h + 1, :]
            igc = g[:, h:h + 1]
            m_st = m_ref[d, h:h + 1, 0:1]
            dm = jnp.where(incl, bc - br + igr, -jnp.inf)
            inter = bc + m_st
            m_t = jnp.maximum(jnp.max(dm, axis=1, keepdims=True), inter)
            b_last = bc[last:last + 1, :]
            lwc = b_last - bc + igc
            m_new = jnp.maximum(b_last + m_st, jnp.max(lwc, axis=0, keepdims=True))
            hs.append(dict(qh=q[:, sl].astype(BF16), kh=k[:, sl],
                           vext=jnp.concatenate([v[:, sl], ones_col], axis=1).astype(BF16),
                           decay=jnp.exp(dm - m_t), w_inter=jnp.exp(inter - m_t), floor=jnp.exp(-m_t),
                           wts=jnp.exp(lwc - m_new), dec=jnp.exp(b_last + m_st - m_new), m_new=m_new,
                           cst=c_ref[d, h]))
    for p in hs:
        p["sc"] = (_dot_nt(p["qh"], p["kh"].astype(BF16)) * p["decay"]).astype(BF16)
    for p in hs:
        p["numext"] = _dot(p["sc"], p["vext"]) + p["w_inter"] * _dot(p["qh"], p["cst"].astype(BF16))
    for p in hs:
        p["upd"] = _dot_tn((p["wts"] * p["kh"]).astype(BF16), p["vext"])
    for d, o_ref in enumerate((o0_ref, o1_ref)):
        outs = []
        for h in range(n_heads):
            p = hs[d * n_heads + h]
            num = p["numext"][:, :HEAD_DIM]
            den = p["numext"][:, HEAD_DIM:HEAD_DIM + 1]
            outs.append(num / jnp.maximum(jnp.abs(den), p["floor"]))
            c_ref[d, h] = p["dec"] * p["cst"] + p["upd"]
            m_ref[d, h:h + 1, :] = jnp.broadcast_to(p["m_new"], (1, LANE))
        o_ref[...] = jnp.concatenate(outs, axis=1)


def _mlstm_scan(q, k, pm, gates, batch, seq):
    t, mw = q.shape
    n_heads = mw // HEAD_DIM
    L = MLSTM_CHUNK
    nc = seq // L
    fwd = lambda bi, c: bi * nc + c
    bwd = lambda bi, c: bi * nc + nc - 1 - c
    specs = []
    for d, blk in enumerate((fwd, bwd)):
        specs += [pl.BlockSpec((L, mw), lambda bi, c, blk=blk: (blk(bi, c), 0)),
                  pl.BlockSpec((L, mw), lambda bi, c, blk=blk: (blk(bi, c), 0)),
                  pl.BlockSpec((L, mw), lambda bi, c, blk=blk: (blk(bi, c), 2)),
                  pl.BlockSpec((None, L, LANE), lambda bi, c, blk=blk, d=d: (d, blk(bi, c), 0))]
    return pl.pallas_call(
        functools.partial(_mlstm_scan_body, n_heads=n_heads),
        grid=(batch, nc),
        in_specs=specs,
        out_specs=[pl.BlockSpec((L, mw), lambda bi, c: (fwd(bi, c), 0)),
                   pl.BlockSpec((L, mw), lambda bi, c: (bwd(bi, c), 0))],
        out_shape=[jax.ShapeDtypeStruct((t, mw), F32)] * 2,
        scratch_shapes=[pltpu.VMEM((2, n_heads, HEAD_DIM, LANE), F32), pltpu.VMEM((2, SUBLANE, LANE), F32)],
        compiler_params=_cparams(2),
        name="mlstm_scan",
    )(q, k, pm, gates, q, k, pm, gates)


def _layer_norm(x, g, b):
    mu = jnp.mean(x, axis=-1, keepdims=True)
    xc = x - mu
    var = jnp.mean(xc * xc, axis=-1, keepdims=True)
    return xc * lax.rsqrt(var + LN_EPS) * g + b


def _head_norm(x, bd_mean, eps):
    mu = _dot_exact_rhs(x, bd_mean)
    xc = x - mu
    var = _dot_exact_rhs(xc * xc, bd_mean)
    return xc * lax.rsqrt(var + eps)


def _mix_out_body(x_ref, yg_ref, ro0_ref, ro1_ref, bonus_ref, rgate_ref, rlg_ref, rlb_ref, mh0_ref, mh1_ref, og_ref,
                  mlg_ref, w_ref, l1g_ref, l1b_ref, rw_ref, rb_ref, bdm_ref,
                  x1_out, topi_out, gate_out, wb_ref, *, alpha, gw, rw):
    @pl.when(pl.program_id(0) == 0)
    def _():
        _cast_rows(w_ref, wb_ref)

    bdm = bdm_ref[...]
    yr = _head_norm(ro0_ref[...] + ro1_ref[...], bdm, RWKV_GN_EPS) * rlg_ref[...] + rlb_ref[...]
    yr = (yr + bonus_ref[...]) * rgate_ref[...]
    ym = _sigmoid(og_ref[...]) * (_head_norm(mh0_ref[...] + mh1_ref[...], bdm, LN_EPS) * mlg_ref[...])
    mix = (_dot(yg_ref[...].astype(BF16), wb_ref[:gw, :]) + _dot(yr.astype(BF16), wb_ref[gw:gw + rw, :])
           + _dot(ym.astype(BF16), wb_ref[gw + rw:, :]))
    x1 = _layer_norm(alpha * x_ref[...] + mix, l1g_ref[...], l1b_ref[...])
    x1_out[...] = x1
    lg = _mm(x1, rw_ref[...], "b3") + rb_ref[...]
    lane = lax.broadcasted_iota(jnp.int32, lg.shape, 1)
    vals, topi = [], jnp.zeros(lg.shape, jnp.int32)
    for j in range(TOP_K):
        mx = jnp.max(lg, axis=1, keepdims=True)
        idx = jnp.min(jnp.where(lg == mx, lane, LANE), axis=1, keepdims=True)
        vals.append(mx)
        topi = jnp.where(lane == j, idx, topi)
        lg = jnp.where(lane == idx, -jnp.inf, lg)
    es = [jnp.exp(vj - vals[0]) for vj in vals]
    den = es[0] + es[1] + es[2] + es[3]
    gate = jnp.zeros(lg.shape, F32)
    for j in range(TOP_K):
        gate = jnp.where(lane == j, es[j] / den, gate)
    topi_out[...] = topi
    gate_out[...] = gate


def _mix_out(x, yg, ro, bonus, rgate, rlg, rlb, mh, pm, mlg, w_out, layer, l1g, l1b, router_w, router_b, alpha,
             tm=256):
    t, dm = x.shape
    gw, rw, mw = yg.shape[1], bonus.shape[1], mh[0].shape[1]
    assert rw == mw
    rwp = jnp.zeros((dm, LANE), F32).at[:, :N_EXPERTS].set(router_w)
    rbp = jnp.full((1, LANE), NEG_BIG, F32).at[0, :N_EXPERTS].set(router_b)
    row = lambda n: pl.BlockSpec((tm, n), lambda i: (i, 0))
    vec = lambda n: _full((1, n))
    return pl.pallas_call(
        functools.partial(_mix_out_body, alpha=alpha, gw=gw, rw=rw),
        grid=(t // tm,),
        in_specs=[row(dm), row(gw), row(rw), row(rw), row(rw), row(rw), vec(rw), vec(rw), row(mw), row(mw),
                  pl.BlockSpec((tm, mw), lambda i: (i, 3)),
                  vec(mw),
                  pl.BlockSpec((None, dm, dm), lambda i: (layer, 0, 0), pipeline_mode=pl.Buffered(1)),
                  vec(dm), vec(dm), _full((dm, LANE)), vec(LANE), _full((rw, rw))],
        out_specs=[row(dm), row(LANE), row(LANE)],
        out_shape=[jax.ShapeDtypeStruct((t, dm), F32),
                   jax.ShapeDtypeStruct((t, LANE), jnp.int32), jax.ShapeDtypeStruct((t, LANE), F32)],
        scratch_shapes=[pltpu.VMEM((dm, dm), BF16)],
        compiler_params=_cparams(1),
        name="mix_out",
    )(x, yg, ro[0], ro[1], bonus, rgate, rlg.reshape(1, rw), rlb.reshape(1, rw), mh[0], mh[1], pm,
      mlg.reshape(1, mw), w_out, l1g.reshape(1, dm), l1b.reshape(1, dm), rwp, rbp,
      (_block_diag_ones(rw) / HEAD_DIM).astype(BF16))


def _moe_body(be_ref, nu_ref, xs_ref, w1_ref, b1_ref, w2_ref, b2_ref, o_ref, w1b_ref, w2b_ref, *, dff):
    i = pl.program_id(0)
    active = i < nu_ref[0]
    new_expert = jnp.logical_or(i == 0, be_ref[i] != be_ref[jnp.maximum(i - 1, 0)])

    @pl.when(jnp.logical_and(active, new_expert))
    def _():
        _cast_rows(w1_ref, w1b_ref)
        _cast_rows(w2_ref, w2b_ref)

    @pl.when(active)
    def _():
        hdn = _dot(xs_ref[...].astype(BF16), w1b_ref[...]) + b1_ref[...]
        g_ = jnp.minimum(hdn[:, :dff], SWIGLU_LIMIT)
        u_ = jnp.clip(hdn[:, dff:], -SWIGLU_LIMIT, SWIGLU_LIMIT)
        act = (u_ + 1.0) * (g_ * _sigmoid(g_ * SWIGLU_ALPHA))
        o_ref[...] = _dot(act.astype(BF16), w2b_ref[...]) + b2_ref[...]

    @pl.when(jnp.logical_not(active))
    def _():
        o_ref[...] = jnp.zeros_like(o_ref)


def _moe_experts(xs, block_e, n_used, w1, b1, w2, b2, layer):
    rows, dm = xs.shape
    nb = rows // MOE_BLOCK
    depth, ne, _, dff2 = w1.shape
    dff = dff2 // 2
    grid_spec = pltpu.PrefetchScalarGridSpec(
        num_scalar_prefetch=2,
        grid=(nb,),
        in_specs=[pl.BlockSpec((MOE_BLOCK, dm), lambda i, be, nu: (i, 0)),
                  pl.BlockSpec((None, None, dm, dff2), lambda i, be, nu: (layer, be[i], 0, 0)),
                  pl.BlockSpec((None, None, 1, dff2), lambda i, be, nu: (layer, be[i], 0, 0)),
                  pl.BlockSpec((None, None, dff, dm), lambda i, be, nu: (layer, be[i], 0, 0)),
                  pl.BlockSpec((None, None, 1, dm), lambda i, be, nu: (layer, be[i], 0, 0))],
        out_specs=pl.BlockSpec((MOE_BLOCK, dm), lambda i, be, nu: (i, 0)),
        scratch_shapes=[pltpu.VMEM((dm, dff2), BF16), pltpu.VMEM((dff, dm), BF16)],
    )
    return pl.pallas_call(
        functools.partial(_moe_body, dff=dff),
        grid_spec=grid_spec,
        out_shape=jax.ShapeDtypeStruct((rows, dm), F32),
        compiler_params=_cparams(1),
        name="moe_experts",
    )(block_e, n_used, xs, w1, b1.reshape(depth, ne, 1, dff2), w2, b2.reshape(depth, ne, 1, dm))


def _moe_plan(topi, n_tokens):
    na = n_tokens * TOP_K
    flat_e = topi[:, :TOP_K].reshape(na)
    onehot = (flat_e[:, None] == jnp.arange(N_EXPERTS, dtype=jnp.int32)[None, :]).astype(jnp.int32)
    csum = jnp.cumsum(onehot, axis=0)
    counts = csum[-1]
    rank = jnp.sum((csum - onehot) * onehot, axis=1)
    pcounts = (counts + MOE_BLOCK - 1) // MOE_BLOCK * MOE_BLOCK
    pend = jnp.cumsum(pcounts)
    pstart = pend - pcounts
    dest = (pstart[flat_e] + rank).astype(jnp.int32)
    nb = -(-na // MOE_BLOCK) + N_EXPERTS
    flat_t = jnp.arange(na, dtype=jnp.int32) // TOP_K
    row_t = jnp.zeros((nb * MOE_BLOCK,), jnp.int32).at[dest].set(flat_t)
    block_e = jnp.minimum(jnp.searchsorted(pend, jnp.arange(nb, dtype=jnp.int32) * MOE_BLOCK, side='right'),
                          N_EXPERTS - 1).astype(jnp.int32)
    n_used = (pend[-1] // MOE_BLOCK).astype(jnp.int32).reshape(1)
    return dest, row_t, block_e, n_used


def _combine_body(x1_ref, y0_ref, y1_ref, y2_ref, y3_ref, gate_ref, g_ref, b_ref, o_ref, *, alpha):
    gate = gate_ref[...]
    ffn = gate[:, 0:1] * y0_ref[...]
    for j, y_ref in enumerate((y1_ref, y2_ref, y3_ref), start=1):
        ffn = ffn + gate[:, j:j + 1] * y_ref[...]
    o_ref[...] = _layer_norm(alpha * x1_ref[...] + ffn, g_ref[...], b_ref[...])


def _combine(x1, yg, gate, ln_g, ln_b, alpha, tm=256):
    t, dm = x1.shape
    n_tiles = t // tm
    expert_rows = lambda j: pl.BlockSpec((tm, dm), lambda i: (i + j * n_tiles, 0))
    return pl.pallas_call(
        functools.partial(_combine_body, alpha=alpha),
        grid=(n_tiles,),
        in_specs=[pl.BlockSpec((tm, dm), lambda i: (i, 0))] + [expert_rows(j) for j in range(TOP_K)]
                 + [pl.BlockSpec((tm, LANE), lambda i: (i, 0)), _full((1, dm)), _full((1, dm))],
        out_specs=pl.BlockSpec((tm, dm), lambda i: (i, 0)),
        out_shape=jax.ShapeDtypeStruct((t, dm), F32),
        compiler_params=_cparams(1),
        name="combine_ln",
    )(x1, yg, yg, yg, yg, gate, ln_g.reshape(1, dm), ln_b.reshape(1, dm))


SC_CORES = 2
SC_SUBCORES = 16
SC_WORKERS = SC_CORES * SC_SUBCORES


def _sc_gather_rows(table, idx, window):
    n = idx.shape[0]
    dim = table.shape[1]
    n_steps = n // (SC_WORKERS * window)
    assert n_steps * window * SC_WORKERS == n and n_steps % 2 == 0 and window % SUBLANE == 0 and window <= LANE
    idx3 = idx.reshape(SC_WORKERS, n_steps, window)
    mesh = plsc.VectorSubcoreMesh(core_axis_name="c", subcore_axis_name="s",
                                  num_cores=SC_CORES, num_subcores=SC_SUBCORES)

    def body(table_hbm, idx_hbm, out_hbm, idx_v, rows_v, gsem, wsem):
        wid = lax.axis_index("s") * SC_CORES + lax.axis_index("c")
        pltpu.sync_copy(idx_hbm.at[wid], idx_v)

        def gather(j, buf):
            return pltpu.make_async_copy(table_hbm.at[idx_v.at[j]], rows_v.at[buf], gsem.at[buf])

        def write(j, buf):
            base = pl.multiple_of((wid * n_steps + j) * window, window)
            return pltpu.make_async_copy(rows_v.at[buf], out_hbm.at[pl.ds(base, window)], wsem.at[buf])

        gather(0, 0).start()

        @pl.loop(0, n_steps, step=2)
        def _(j0):
            for buf in range(2):
                j = j0 + buf
                gather(j, buf).wait()

                @pl.when(j >= 1)
                def _():
                    write(j - 1, 1 - buf).wait()

                @pl.when(j + 1 < n_steps)
                def _():
                    gather(j + 1, 1 - buf).start()

                write(j, buf).start()

        write(n_steps - 1, 1).wait()

    return pl.kernel(
        body, out_type=jax.ShapeDtypeStruct((n, dim), table.dtype), mesh=mesh,
        scratch_types=[pltpu.VMEM((n_steps, window), jnp.int32), pltpu.VMEM((2, window, dim), table.dtype),
                       pltpu.SemaphoreType.DMA((2,)), pltpu.SemaphoreType.DMA((2,))],
        name="sc_gather",
    )(table, idx3)


def _pad_cols(w, width):
    return jnp.pad(w, ((0, 0), (0, width - w.shape[1])))


def kernel(x, w_in, gmlp_ln_g, gmlp_ln_b, gmlp_ws, gmlp_bs, rwkv_mu, rwkv_w0, rwkv_w2, rwkv_a0, rwkv_a2, rwkv_g2, rwkv_k_k, rwkv_k_a, rwkv_r_k, rwkv_ln_g, rwkv_ln_b, mlstm_conv_w, mlstm_conv_b, mlstm_gate_b, mlstm_ln_g, w_out, ln1_g, ln1_b, router_w, router_b, exp_w1, exp_b1, exp_w2, exp_b2, ln2_g, ln2_b):
    batch, seq, dm = x.shape
    depth = w_in.shape[0]
    t = batch * seq
    gw = gmlp_ln_g.shape[1]
    rw = rwkv_w0.shape[2]
    mw = mlstm_ln_g.shape[1]
    g_proj = 2 * gw
    r_proj = 3 * rw + W_LORA + A_LORA + G_LORA
    alpha = (2 * depth) ** 0.25
    xf = x.reshape(t, dm)
    for l in range(depth):
        pg, pr, pm = _proj(xf, w_in, l, g_proj, r_proj)
        y_g = _gmlp(pg, gmlp_ln_g[l], gmlp_ln_b[l], gmlp_ws[l], gmlp_bs[l])
        r, v, a, kd, b, lw, bonus, rgate = _rwkv_prep(
            pr, seq, rwkv_mu[l], rwkv_w0[l], rwkv_w2[l], rwkv_a0[l], rwkv_a2[l], rwkv_g2[l],
            rwkv_k_k[l], rwkv_k_a[l], rwkv_r_k[l].reshape(-1))
        ro = _rwkv_scan(r, v, a, kd, b, lw, batch, seq)
        q, k, gates = _mlstm_prep(pm, seq, mlstm_conv_w[l], mlstm_conv_b[l], mlstm_gate_b[l], mw)
        mh = _mlstm_scan(q, k, pm, gates, batch, seq)
        x1, topi, gate = _mix_out(xf, y_g, ro, bonus, rgate, rwkv_ln_g[l], rwkv_ln_b[l], mh, pm,
                                  mlstm_ln_g[l], w_out, l, ln1_g[l], ln1_b[l], router_w[l], router_b[l], alpha)
        dest, row_t, block_e, n_used = _moe_plan(topi, t)
        xs = _sc_gather_rows(x1, row_t, window=32)
        ys = _moe_experts(xs, block_e, n_used, exp_w1, exp_b1, exp_w2, exp_b2, l)
        yg = _sc_gather_rows(ys, dest.reshape(t, TOP_K).T.reshape(-1), window=32)
        xf = _combine(x1, yg, gate, ln2_g[l], ln2_b[l], alpha)
    return xf.reshape(batch, seq, dm)
```

```python
import functools
import math

import jax
import jax.numpy as jnp
from jax import lax
from jax.experimental import pallas as pl
from jax.experimental.pallas import tpu as pltpu
from jax.experimental.pallas import tpu_sc as plsc

F32 = jnp.float32
BF16 = jnp.bfloat16
HI = lax.Precision.HIGHEST

HEAD_DIM = 64
GMLP_CHUNK = 128
MLSTM_CHUNK = 128
RWKV_CHUNK = 64
W_LORA = 64
A_LORA = 64
G_LORA = 128
N_EXPERTS = 32
TOP_K = 4
MOE_BLOCK = 256
SWIGLU_LIMIT = 7.0
SWIGLU_ALPHA = 1.702
LN_EPS = 1e-5
RWKV_GN_EPS = 64e-5
LANE = 128
SUBLANE = 8
VMEM_LIMIT = 48 * 1024 * 1024
NEG_BIG = -1e30


def _cparams(n_axes):
    return pltpu.CompilerParams(dimension_semantics=("arbitrary",) * n_axes,
                                vmem_limit_bytes=VMEM_LIMIT)


def _full(shape):
    return pl.BlockSpec(shape, lambda *_: (0,) * len(shape))


def _dot(a, b, precision=None):
    return jnp.dot(a, b, preferred_element_type=F32, precision=precision)


def _dot_nt(a, b, precision=None):
    return lax.dot_general(a, b, (((1,), (1,)), ((), ())), preferred_element_type=F32, precision=precision)


def _dot_tn(a, b, precision=None):
    return lax.dot_general(a, b, (((0,), (0,)), ((), ())), preferred_element_type=F32, precision=precision)


def _split(x):
    hi = x.astype(BF16)
    return hi, (x - hi.astype(F32)).astype(BF16)


def _split3(x):
    hi = x.astype(BF16)
    r1 = x - hi.astype(F32)
    mid = r1.astype(BF16)
    return hi, mid, (r1 - mid.astype(F32)).astype(BF16)


def _mm(a, b, mode, dot=_dot):
    if mode == "hi":
        return dot(a, b, HI)
    if mode == "b1":
        return dot(a.astype(BF16), b.astype(BF16))
    ah, al = _split(a)
    bh, bl = _split(b)
    return dot(ah, bh) + (dot(ah, bl) + dot(al, bh))


def _dot_exact_lhs(a_bf16, x):
    hi, mid, lo = _split3(x)
    return _dot(a_bf16, hi) + (_dot(a_bf16, mid) + _dot(a_bf16, lo))


def _dot_exact_rhs(x, b_bf16):
    hi, mid, lo = _split3(x)
    return _dot(hi, b_bf16) + (_dot(mid, b_bf16) + _dot(lo, b_bf16))


def _sigmoid(x):
    return 1.0 / (1.0 + jnp.exp(-x))


def _softplus(x):
    return jnp.maximum(x, 0.0) + jnp.log1p(jnp.exp(-jnp.abs(x)))


def _block_diag_ones(width):
    h = jnp.arange(width) // HEAD_DIM
    return (h[:, None] == h[None, :]).astype(F32)


CAST_ROWS = 128


def _cast_rows(src_ref, dst_ref):
    n_src, n_dst = src_ref.shape[1], dst_ref.shape[1]
    whole = n_src // LANE * LANE

    def step(r, carry):
        rows = pl.ds(pl.multiple_of(r * CAST_ROWS, CAST_ROWS), CAST_ROWS)
        dst_ref[rows, :whole] = src_ref[rows, :whole].astype(BF16)
        if n_dst > whole:
            tail = [src_ref[rows, whole:]] if n_src > whole else []
            tail.append(jnp.zeros((CAST_ROWS, n_dst - n_src), F32))
            dst_ref[rows, whole:] = jnp.concatenate(tail, axis=1).astype(BF16)
        return carry
    lax.fori_loop(0, src_ref.shape[0] // CAST_ROWS, step, 0)


def _proj_body(x_ref, w_ref, pg_ref, pr_ref, pm_ref, wb_ref, *, ng, nr):
    @pl.when(pl.program_id(0) == 0)
    def _():
        _cast_rows(w_ref, wb_ref)

    xb = x_ref[...].astype(BF16)
    pg_ref[...] = _dot(xb, wb_ref[:, :ng])
    pr_ref[...] = _dot(xb, wb_ref[:, ng:ng + nr])
    pm_ref[...] = _dot(xb, wb_ref[:, ng + nr:])


def _proj(x, w_in, layer, ng, nr, tm=256):
    t, d = x.shape
    p_in = w_in.shape[2]
    p_pad = -(-p_in // LANE) * LANE
    nm = p_pad - ng - nr
    row = lambda n: pl.BlockSpec((tm, n), lambda i: (i, 0))
    return pl.pallas_call(
        functools.partial(_proj_body, ng=ng, nr=nr),
        grid=(t // tm,),
        in_specs=[row(d), pl.BlockSpec((None, d, p_in), lambda i: (layer, 0, 0), pipeline_mode=pl.Buffered(1))],
        out_specs=[row(ng), row(nr), row(nm)],
        out_shape=[jax.ShapeDtypeStruct((t, n), F32) for n in (ng, nr, nm)],
        scratch_shapes=[pltpu.VMEM((d, p_pad), BF16)],
        compiler_params=_cparams(1),
        name="in_proj",
    )(x, w_in)


def _gmlp_body(pg_ref, lng_ref, lnb_ref, ws_ref, bst_ref, o_ref, *, gw, chunks):
    p = pg_ref[...]
    p = 0.5 * p * (1.0 + lax.erf(p * math.sqrt(0.5)))
    u, v = p[:, :gw], p[:, gw:]
    mu = jnp.mean(v, axis=-1, keepdims=True)
    vc = v - mu
    var = jnp.mean(vc * vc, axis=-1, keepdims=True)
    vn = vc * lax.rsqrt(var + LN_EPS) * lng_ref[...] + lnb_ref[...]
    n_heads = gw // HEAD_DIM
    for c in range(chunks):
        rows = slice(c * GMLP_CHUNK, (c + 1) * GMLP_CHUNK)
        ys = []
        for h in range(n_heads):
            cols = slice(h * HEAD_DIM, (h + 1) * HEAD_DIM)
            y = _dot(ws_ref[h], vn[rows, cols].astype(BF16)) + bst_ref[:, h:h + 1]
            ys.append(y)
        o_ref[rows, :] = u[rows, :] * jnp.concatenate(ys, axis=1)


def _gmlp(pg, ln_g, ln_b, ws, bs, chunks=4):
    t = pg.shape[0]
    gw = pg.shape[1] // 2
    n_heads = gw // HEAD_DIM
    tm = chunks * GMLP_CHUNK
    bst = jnp.zeros((GMLP_CHUNK, LANE), F32).at[:, :n_heads].set(bs.T)
    return pl.pallas_call(
        functools.partial(_gmlp_body, gw=gw, chunks=chunks),
        grid=(t // tm,),
        in_specs=[pl.BlockSpec((tm, 2 * gw), lambda i: (i, 0)), _full((1, gw)), _full((1, gw)),
                  _full((n_heads, GMLP_CHUNK, GMLP_CHUNK)), _full((GMLP_CHUNK, LANE))],
        out_specs=pl.BlockSpec((tm, gw), lambda i: (i, 0)),
        out_shape=jax.ShapeDtypeStruct((t, gw), F32),
        compiler_params=_cparams(1),
        name="gmlp",
    )(pg, ln_g.reshape(1, gw), ln_b.reshape(1, gw), ws.astype(BF16), bst)


def _halo_specs(tm, width, n_rows):
    per8 = tm // SUBLANE
    last = n_rows // SUBLANE - 1
    prev = pl.BlockSpec((SUBLANE, width), lambda i: (jnp.maximum(i * per8 - 1, 0), 0))
    nxt = pl.BlockSpec((SUBLANE, width), lambda i: (jnp.minimum((i + 1) * per8, last), 0))
    return prev, nxt


def _neighbours(cur, prev_blk, next_blk, tiles_per_seq):
    tm = cur.shape[0]
    j = pl.program_id(0) % tiles_per_seq
    prev_row = jnp.where(j > 0, prev_blk[SUBLANE - 1:SUBLANE, :], 0.0)
    next_row = jnp.where(j < tiles_per_seq - 1, next_blk[0:1, :], 0.0)
    ridx = lax.broadcasted_iota(jnp.int32, cur.shape, 0)
    before = jnp.where(ridx == 0, prev_row, pltpu.roll(cur, 1, 0))
    after = jnp.where(ridx == tm - 1, next_row, pltpu.roll(cur, tm - 1, 0))
    return before, after


def _rwkv_prep_body(pr_ref, prev_ref, next_ref, mu_ref, w0_ref, w2_ref, a0_ref, a2_ref, g2_ref,
                    kk_ref, ka_ref, rk_ref, bd_ref,
                    r_out, v_out, a_out, kd_out, b_out, lw_out, bonus_out, gate_out, *, rw, tiles_per_seq):
    pf = pr_ref[...]
    before, after = _neighbours(pf, prev_ref[...], next_ref[...], tiles_per_seq)
    pf = pf + mu_ref[0:1, :] * (before - pf) + mu_ref[1:2, :] * (after - pf)
    o3 = 3 * rw
    r, k, v = pf[:, :rw], pf[:, rw:2 * rw], pf[:, 2 * rw:o3]
    wd = pf[:, o3:o3 + W_LORA]
    ad = pf[:, o3 + W_LORA:o3 + W_LORA + A_LORA]
    gd = pf[:, o3 + W_LORA + A_LORA:]
    bd = bd_ref[...]
    kk = k * kk_ref[...]
    ss = _dot_exact_rhs(kk * kk, bd)
    kk = kk / jnp.maximum(jnp.sqrt(ss), 1e-12)
    twd = jnp.tanh(wd)
    ksum = jnp.zeros_like(k)
    for d in range(2):
        w_log = -_softplus(-(w0_ref[d:d + 1, :] + _mm(twd, w2_ref[d], "b3"))) - 0.5
        lw_out[d] = -jnp.exp(w_log)
        iclr = _sigmoid(a0_ref[d:d + 1, :] + _mm(ad, a2_ref[d], "b3"))
        kd = k * (1.0 + (iclr - 1.0) * ka_ref[...])
        kd_out[d] = kd
        b_out[d] = kk * iclr
        ksum = ksum + kd
    r_out[...] = r
    v_out[...] = v
    a_out[...] = -kk
    bonus_out[...] = _dot_exact_rhs(r * ksum * rk_ref[...], bd) * v
    gate_out[...] = _dot(_sigmoid(gd).astype(BF16), g2_ref[...])


def _rwkv_prep(pr, seq, mu, w0, w2, a0, a2, g2, k_k, k_a, r_k, tm=256):
    t, rproj = pr.shape
    rw = w0.shape[1]
    tiles_per_seq = seq // tm
    prev, nxt = _halo_specs(tm, rproj, t)
    row = pl.BlockSpec((tm, rw), lambda i: (i, 0))
    row2 = pl.BlockSpec((2, tm, rw), lambda i: (0, i, 0))
    one = jax.ShapeDtypeStruct((t, rw), F32)
    two = jax.ShapeDtypeStruct((2, t, rw), F32)
    return pl.pallas_call(
        functools.partial(_rwkv_prep_body, rw=rw, tiles_per_seq=tiles_per_seq),
        grid=(t // tm,),
        in_specs=[pl.BlockSpec((tm, rproj), lambda i: (i, 0)), prev, nxt,
                  _full((2, rproj)), _full((2, rw)), _full((2, W_LORA, rw)), _full((2, rw)),
                  _full((2, A_LORA, rw)), _full((G_LORA, rw)), _full((1, rw)), _full((1, rw)),
                  _full((1, rw)), _full((rw, rw))],
        out_specs=[row, row, row, row2, row2, row2, row, row],
        out_shape=[one, one, one, two, two, two, one, one],
        compiler_params=_cparams(1),
        name="rwkv_prep",
    )(pr, pr, pr, mu, w0, w2, a0, a2, g2.astype(BF16), k_k.reshape(1, rw), k_a.reshape(1, rw),
      r_k.reshape(1, rw), _block_diag_ones(rw).astype(BF16))


P_G, P_INV, P_APPLY, P_STATE, P_SEQ = "b3", "b1", "b1", "b3", "b3"


def _rwkv_intra_body(r_ref, v_ref, a_ref, kd_ref, b_ref, lw_ref, rq_out, o0_out, mtx_out, hc_out,
                     *, n_heads, chunks):
    L = RWKV_CHUNK
    d = pl.program_id(0)
    row = lax.broadcasted_iota(jnp.int32, (L, L), 0)
    col = lax.broadcasted_iota(jnp.int32, (L, L), 1)
    fwd = d == 0
    rel = (col - row) * (1 - 2 * d)
    incl = rel <= 0
    strict = rel < 0
    eye = (row == col).astype(F32)
    tri = incl.astype(BF16)
    pairs = []
    for c in range(chunks):
        rows = slice(c * L, (c + 1) * L)
        lw = lw_ref[rows, :]
        cum = _dot_exact_lhs(tri, lw)
        tot = jnp.where(fwd, cum[L - 1:L, :], cum[0:1, :])
        e_neg = jnp.exp(-cum)
        e_end = jnp.exp(tot - cum)
        e_tot = jnp.exp(tot)
        r, v, a, kd, b = r_ref[rows, :], v_ref[rows, :], a_ref[rows, :], kd_ref[rows, :], b_ref[rows, :]
        at, rt, bt, kt = a * jnp.exp(cum - lw), r * jnp.exp(cum), b * e_neg, kd * e_neg
        kend, bend = kd * e_end, b * e_end
        for h in range(n_heads):
            sl = slice(h * HEAD_DIM, (h + 1) * HEAD_DIM)
            pairs.append(dict(at=at[:, sl], rt=rt[:, sl], bt=bt[:, sl], kt=kt[:, sl], v=v[:, sl],
                              kend=kend[:, sl], bend=bend[:, sl], e_tot=e_tot[:, sl]))
    for p in pairs:
        p["g"] = _mm(jnp.concatenate([p["at"], p["rt"]], axis=0),
                     jnp.concatenate([p["bt"], p["kt"]], axis=0), P_G, _dot_nt)
    for p in pairs:
        g = p.pop("g")
        p["pw"] = jnp.where(strict, g[:L, :L], 0.0)
        p["a_ak"] = jnp.where(strict, g[:L, L:], 0.0)
        p["m_rb"] = jnp.where(incl, g[L:, :L], 0.0)
        p["m_rk"] = jnp.where(incl, g[L:, L:], 0.0)
        p["inv"] = eye + p["pw"]
    for _ in range(int(math.log2(L)) - 1):
        for p in pairs:
            p["pw"] = _mm(p["pw"], p["pw"], P_INV)
        for p in pairs:
            p["inv"] = p["inv"] + _mm(p["inv"], p["pw"], P_INV)
    for p in pairs:
        p["akv"] = _mm(p["a_ak"], p["v"], P_APPLY)
    for p in pairs:
        p["wu"] = _mm(p["inv"], jnp.concatenate([p["at"], p["akv"]], axis=1), P_APPLY)
    for p in pairs:
        p["mwu"] = _mm(p["m_rb"], p["wu"], P_APPLY)
    for p in pairs:
        p["o0"] = p["mwu"][:, HEAD_DIM:] + _mm(p["m_rk"], p["v"], P_APPLY)
    for p in pairs:
        p["bw"] = _mm(p["bend"], p["wu"], P_STATE, _dot_tn)
    for p in pairs:
        p["hc"] = _mm(p["kend"], p["v"], P_STATE, _dot_tn) + p["bw"][:, HEAD_DIM:]
    for c in range(chunks):
        ps = pairs[c * n_heads:(c + 1) * n_heads]
        rows = slice(c * L, (c + 1) * L)
        krows = slice(c * HEAD_DIM, (c + 1) * HEAD_DIM)
        rq_out[rows, :] = jnp.concatenate([p["rt"] + p["mwu"][:, :HEAD_DIM] for p in ps], axis=1)
        o0_out[rows, :] = jnp.concatenate([p["o0"] for p in ps], axis=1)
        mtx_out[krows, :] = jnp.concatenate([eye * p["e_tot"] + p["bw"][:, :HEAD_DIM] for p in ps], axis=1)
        hc_out[krows, :] = jnp.concatenate([p["hc"] for p in ps], axis=1)


def _rwkv_intra(r, v, a, kd, b, lw, chunks=2):
    t, rw = r.shape
    n_heads = rw // HEAD_DIM
    tm = chunks * RWKV_CHUNK
    tk = chunks * HEAD_DIM
    n_tiles = t // tm
    one = pl.BlockSpec((tm, rw), lambda d, i: (i, 0))
    two = pl.BlockSpec((None, tm, rw), lambda d, i: (d, i, 0))
    twok = pl.BlockSpec((None, tk, rw), lambda d, i: (d, i, 0))
    return pl.pallas_call(
        functools.partial(_rwkv_intra_body, n_heads=n_heads, chunks=chunks),
        grid=(2, n_tiles),
        in_specs=[one, one, one, two, two, two],
        out_specs=[two, two, twok, twok],
        out_shape=[jax.ShapeDtypeStruct((2, t, rw), F32), jax.ShapeDtypeStruct((2, t, rw), F32),
                   jax.ShapeDtypeStruct((2, n_tiles * tk, rw), F32),
                   jax.ShapeDtypeStruct((2, n_tiles * tk, rw), F32)],
        compiler_params=_cparams(2),
        name="rwkv_intra",
    )(r, v, a, kd, b, lw)


def _rwkv_seq_body(rq0, o00, mtx0, hc0, rq1, o01, mtx1, hc1, out0, out1, h_ref, *, n_heads, batch):
    c = pl.program_id(0)

    @pl.when(c == 0)
    def _():
        h_ref[...] = jnp.zeros_like(h_ref)

    L = RWKV_CHUNK
    for d, (rq, o0, mtx, hc, out) in enumerate(((rq0, o00, mtx0, hc0, out0), (rq1, o01, mtx1, hc1, out1))):
        for bi in range(batch):
            rq_t, mtx_t = rq[bi], mtx[bi]
            state = h_ref[d, bi]
            outs, states = [], []
            for h in range(n_heads):
                sl = slice(h * HEAD_DIM, (h + 1) * HEAD_DIM)
                prod = _mm(jnp.concatenate([rq_t[:, sl], mtx_t[:, sl]], axis=0), state[:, sl], P_SEQ)
                outs.append(prod[:L])
                states.append(prod[L:])
            out[bi] = jnp.concatenate(outs, axis=1) + o0[bi]
            h_ref[d, bi] = jnp.concatenate(states, axis=1) + hc[bi]


def _rwkv_seq(rq, o0, mtx, hc, batch, seq):
    _, t, rw = rq.shape
    n_heads = rw // HEAD_DIM
    L = RWKV_CHUNK
    nc = seq // L
    as4 = lambda x: x.reshape(2, batch, x.shape[1] // batch, rw)
    rq, o0, mtx, hc = as4(rq), as4(o0), as4(mtx), as4(hc)
    fwd = lambda rows: pl.BlockSpec((None, batch, rows, rw), lambda c: (0, 0, c, 0))
    bwd = lambda rows: pl.BlockSpec((None, batch, rows, rw), lambda c: (1, 0, nc - 1 - c, 0))
    out0, out1 = pl.pallas_call(
        functools.partial(_rwkv_seq_body, n_heads=n_heads, batch=batch),
        grid=(nc,),
        in_specs=[fwd(L), fwd(L), fwd(HEAD_DIM), fwd(HEAD_DIM), bwd(L), bwd(L), bwd(HEAD_DIM), bwd(HEAD_DIM)],
        out_specs=[pl.BlockSpec((batch, L, rw), lambda c: (0, c, 0)),
                   pl.BlockSpec((batch, L, rw), lambda c: (0, nc - 1 - c, 0))],
        out_shape=[jax.ShapeDtypeStruct((batch, seq, rw), F32)] * 2,
        scratch_shapes=[pltpu.VMEM((2, batch, HEAD_DIM, rw), F32)],
        compiler_params=_cparams(1),
        name="rwkv_seq",
    )(rq, o0, mtx, hc, rq, o0, mtx, hc)
    return out0.reshape(t, rw), out1.reshape(t, rw)


def _rwkv_scan(r, v, a, kd, b, lw, batch, seq):
    rq, o0, mtx, hc = _rwkv_intra(r, v, a, kd, b, lw)
    return _rwkv_seq(rq, o0, mtx, hc, batch, seq)


def _mlstm_prep_body(qk_ref, prev_ref, next_ref, g_ref, cw_ref, cb_ref, gb_ref, q_out, k_out, gate_out,
                     *, mw, n_heads, tiles_per_seq):
    x = qk_ref[...]
    before, after = _neighbours(x, prev_ref[...], next_ref[...], tiles_per_seq)
    y = cb_ref[...] + before * cw_ref[0:1, :] + x * cw_ref[1:2, :] + after * cw_ref[2:3, :]
    y = y * _sigmoid(y)
    q_out[...] = y[:, :mw]
    k_out[...] = y[:, mw:] * (HEAD_DIM ** -0.5)
    g = g_ref[...] + gb_ref[...]
    lane = lax.broadcasted_iota(jnp.int32, g.shape, 1)
    for d in range(2):
        ig = g if d == 0 else pltpu.roll(g, LANE - n_heads, 1)
        fg = pltpu.roll(g, LANE - (1 + d) * n_heads, 1)
        lf = -_softplus(-fg)
        gate_out[d] = jnp.where(lane < n_heads, ig, jnp.where(lane < 2 * n_heads, lf, 0.0))


def _mlstm_prep(pm, seq, conv_w, conv_b, gate_b, mw, tm=256):
    t = pm.shape[0]
    n_heads = mw // HEAD_DIM
    tiles_per_seq = seq // tm
    w2 = 2 * mw
    prev, nxt = _halo_specs(tm, w2, t)
    gcol = (4 * mw) // LANE
    gb = jnp.zeros((1, LANE), F32).at[0, :4 * n_heads].set(gate_b)
    row = pl.BlockSpec((tm, mw), lambda i: (i, 0))
    return pl.pallas_call(
        functools.partial(_mlstm_prep_body, mw=mw, n_heads=n_heads, tiles_per_seq=tiles_per_seq),
        grid=(t // tm,),
        in_specs=[pl.BlockSpec((tm, w2), lambda i: (i, 0)), prev, nxt,
                  pl.BlockSpec((tm, LANE), lambda i: (i, gcol)),
                  _full((3, w2)), _full((1, w2)), _full((1, LANE))],
        out_specs=[row, row, pl.BlockSpec((2, tm, LANE), lambda i: (0, i, 0))],
        out_shape=[jax.ShapeDtypeStruct((t, mw), F32), jax.ShapeDtypeStruct((t, mw), F32),
                   jax.ShapeDtypeStruct((2, t, LANE), F32)],
        compiler_params=_cparams(1),
        name="mlstm_prep",
    )(pm, pm, pm, pm, conv_w, conv_b.reshape(1, w2), gb)


def _mlstm_scan_body(q0_ref, k0_ref, v0_ref, g0_ref, q1_ref, k1_ref, v1_ref, g1_ref, o0_ref, o1_ref,
                     c_ref, m_ref, *, n_heads):
    L = MLSTM_CHUNK

    @pl.when(pl.program_id(1) == 0)
    def _():
        c_ref[...] = jnp.zeros_like(c_ref)
        m_ref[...] = jnp.zeros_like(m_ref)

    row = lax.broadcasted_iota(jnp.int32, (L, L), 0)
    col = lax.broadcasted_iota(jnp.int32, (L, L), 1)
    lane64 = lax.broadcasted_iota(jnp.int32, (L, HEAD_DIM), 1)
    ones_col = (lane64 == 0).astype(F32)
    hs = []
    for d, (q_ref, k_ref, v_ref, g_ref) in enumerate(((q0_ref, k0_ref, v0_ref, g0_ref),
                                                      (q1_ref, k1_ref, v1_ref, g1_ref))):
        incl = (col <= row) if d == 0 else (col >= row)
        last = L - 1 if d == 0 else 0
        g = g_ref[...]
        bcum = _dot_exact_lhs(incl.astype(BF16), g)
        g_t = g.T
        bcum_t = bcum.T
        q, k, v = q_ref[...], k_ref[...], v_ref[...]
        for h in range(n_heads):
            sl = slice(h * HEAD_DIM, (h + 1) * HEAD_DIM)
            bc = bcum[:, n_heads + h:n_heads + h + 1]
            br = bcum_t[n_heads + h:n_heads + h + 1, :]
            igr = g_t[h:h + 1, :]
            igc = g[:, h:h + 1]
            m_st = m_ref[d, h:h + 1, 0:1]
            dm = jnp.where(incl, bc - br + igr, -jnp.inf)
            inter = bc + m_st
            m_t = jnp.maximum(jnp.max(dm, axis=1, keepdims=True), inter)
            b_last = bc[last:last + 1, :]
            lwc = b_last - bc + igc
            m_new = jnp.maximum(b_last + m_st, jnp.max(lwc, axis=0, keepdims=True))
            hs.append(dict(qh=q[:, sl].astype(BF16), kh=k[:, sl],
                           vext=jnp.concatenate([v[:, sl], ones_col], axis=1).astype(BF16),
                           decay=jnp.exp(dm - m_t), w_inter=jnp.exp(inter - m_t), floor=jnp.exp(-m_t),
                           wts=jnp.exp(lwc - m_new), dec=jnp.exp(b_last + m_st - m_new), m_new=m_new,
                           cst=c_ref[d, h]))
    for p in hs:
        p["sc"] = (_dot_nt(p["qh"], p["kh"].astype(BF16)) * p["decay"]).astype(BF16)
    for p in hs:
        p["numext"] = _dot(p["sc"], p["vext"]) + p["w_inter"] * _dot(p["qh"], p["cst"].astype(BF16))
    for p in hs:
        p["upd"] = _dot_tn((p["wts"] * p["kh"]).astype(BF16), p["vext"])
    for d, o_ref in enumerate((o0_ref, o1_ref)):
        outs = []
        for h in range(n_heads):
            p = hs[d * n_heads + h]
            num = p["numext"][:, :HEAD_DIM]
            den = p["numext"][:, HEAD_DIM:HEAD_DIM + 1]
            outs.append(num / jnp.maximum(jnp.abs(den), p["floor"]))
            c_ref[d, h] = p["dec"] * p["cst"] + p["upd"]
            m_ref[d, h:h + 1, :] = jnp.broadcast_to(p["m_new"], (1, LANE))
        o_ref[...] = jnp.concatenate(outs, axis=1)


def _mlstm_scan(q, k, pm, gates, batch, seq):
    t, mw = q.shape
    n_heads = mw // HEAD_DIM
    L = MLSTM_CHUNK
    nc = seq // L
    fwd = lambda bi, c: bi * nc + c
    bwd = lambda bi, c: bi * nc + nc - 1 - c
    specs = []
    for d, blk in enumerate((fwd, bwd)):
        specs += [pl.BlockSpec((L, mw), lambda bi, c, blk=blk: (blk(bi, c), 0)),
                  pl.BlockSpec((L, mw), lambda bi, c, blk=blk: (blk(bi, c), 0)),
                  pl.BlockSpec((L, mw), lambda bi, c, blk=blk: (blk(bi, c), 2)),
                  pl.BlockSpec((None, L, LANE), lambda bi, c, blk=blk, d=d: (d, blk(bi, c), 0))]
    return pl.pallas_call(
        functools.partial(_mlstm_scan_body, n_heads=n_heads),
        grid=(batch, nc),
        in_specs=specs,
        out_specs=[pl.BlockSpec((L, mw), lambda bi, c: (fwd(bi, c), 0)),
                   pl.BlockSpec((L, mw), lambda bi, c: (bwd(bi, c), 0))],
        out_shape=[jax.ShapeDtypeStruct((t, mw), F32)] * 2,
        scratch_shapes=[pltpu.VMEM((2, n_heads, HEAD_DIM, LANE), F32), pltpu.VMEM((2, SUBLANE, LANE), F32)],
        compiler_params=_cparams(2),
        name="mlstm_scan",
    )(q, k, pm, gates, q, k, pm, gates)


def _layer_norm(x, g, b):
    mu = jnp.mean(x, axis=-1, keepdims=True)
    xc = x - mu
    var = jnp.mean(xc * xc, axis=-1, keepdims=True)
    return xc * lax.rsqrt(var + LN_EPS) * g + b


def _head_norm(x, bd_mean, eps):
    mu = _dot_exact_rhs(x, bd_mean)
    xc = x - mu
    var = _dot_exact_rhs(xc * xc, bd_mean)
    return xc * lax.rsqrt(var + eps)


def _mix_out_body(x_ref, yg_ref, ro0_ref, ro1_ref, bonus_ref, rgate_ref, rlg_ref, rlb_ref, mh0_ref, mh1_ref, og_ref,
                  mlg_ref, w_ref, l1g_ref, l1b_ref, rw_ref, rb_ref, bdm_ref,
                  x1_out, topi_out, gate_out, wb_ref, *, alpha, gw, rw):
    @pl.when(pl.program_id(0) == 0)
    def _():
        _cast_rows(w_ref, wb_ref)

    bdm = bdm_ref[...]
    yr = _head_norm(ro0_ref[...] + ro1_ref[...], bdm, RWKV_GN_EPS) * rlg_ref[...] + rlb_ref[...]
    yr = (yr + bonus_ref[...]) * rgate_ref[...]
    ym = _sigmoid(og_ref[...]) * (_head_norm(mh0_ref[...] + mh1_ref[...], bdm, LN_EPS) * mlg_ref[...])
    mix = (_dot(yg_ref[...].astype(BF16), wb_ref[:gw, :]) + _dot(yr.astype(BF16), wb_ref[gw:gw + rw, :])
           + _dot(ym.astype(BF16), wb_ref[gw + rw:, :]))
    x1 = _layer_norm(alpha * x_ref[...] + mix, l1g_ref[...], l1b_ref[...])
    x1_out[...] = x1
    lg = _mm(x1, rw_ref[...], "b3") + rb_ref[...]
    lane = lax.broadcasted_iota(jnp.int32, lg.shape, 1)
    vals, topi = [], jnp.zeros(lg.shape, jnp.int32)
    for j in range(TOP_K):
        mx = jnp.max(lg, axis=1, keepdims=True)
        idx = jnp.min(jnp.where(lg == mx, lane, LANE), axis=1, keepdims=True)
        vals.append(mx)
        topi = jnp.where(lane == j, idx, topi)
        lg = jnp.where(lane == idx, -jnp.inf, lg)
    es = [jnp.exp(vj - vals[0]) for vj in vals]
    den = es[0] + es[1] + es[2] + es[3]
    gate = jnp.zeros(lg.shape, F32)
    for j in range(TOP_K):
        gate = jnp.where(lane == j, es[j] / den, gate)
    topi_out[...] = topi.T[:SUBLANE, :]
    gate_out[...] = gate


def _mix_out(x, yg, ro, bonus, rgate, rlg, rlb, mh, pm, mlg, w_out, layer, l1g, l1b, router_w, router_b, alpha,
             tm=256):
    t, dm = x.shape
    gw, rw, mw = yg.shape[1], bonus.shape[1], mh[0].shape[1]
    assert rw == mw
    rwp = jnp.zeros((dm, LANE), F32).at[:, :N_EXPERTS].set(router_w)
    rbp = jnp.full((1, LANE), NEG_BIG, F32).at[0, :N_EXPERTS].set(router_b)
    row = lambda n: pl.BlockSpec((tm, n), lambda i: (i, 0))
    vec = lambda n: _full((1, n))
    return pl.pallas_call(
        functools.partial(_mix_out_body, alpha=alpha, gw=gw, rw=rw),
        grid=(t // tm,),
        in_specs=[row(dm), row(gw), row(rw), row(rw), row(rw), row(rw), vec(rw), vec(rw), row(mw), row(mw),
                  pl.BlockSpec((tm, mw), lambda i: (i, 3)),
                  vec(mw),
                  pl.BlockSpec((None, dm, dm), lambda i: (layer, 0, 0), pipeline_mode=pl.Buffered(1)),
                  vec(dm), vec(dm), _full((dm, LANE)), vec(LANE), _full((rw, rw))],
        out_specs=[row(dm), pl.BlockSpec((SUBLANE, tm), lambda i: (0, i)), row(LANE)],
        out_shape=[jax.ShapeDtypeStruct((t, dm), F32),
                   jax.ShapeDtypeStruct((SUBLANE, t), jnp.int32), jax.ShapeDtypeStruct((t, LANE), F32)],
        scratch_shapes=[pltpu.VMEM((dm, dm), BF16)],
        compiler_params=_cparams(1),
        name="mix_out",
    )(x, yg, ro[0], ro[1], bonus, rgate, rlg.reshape(1, rw), rlb.reshape(1, rw), mh[0], mh[1], pm,
      mlg.reshape(1, mw), w_out, l1g.reshape(1, dm), l1b.reshape(1, dm), rwp, rbp,
      (_block_diag_ones(rw) / HEAD_DIM).astype(BF16))


def _moe_body(be_ref, nu_ref, ve_ref, xs_ref, w1_ref, b1_ref, w2_ref, b2_ref, o_ref, w1b_ref, w2b_ref, *, dff):
    i = pl.program_id(0)
    active = i < nu_ref[0]
    new_expert = jnp.logical_or(i == 0, be_ref[i] != be_ref[jnp.maximum(i - 1, 0)])

    @pl.when(jnp.logical_and(active, new_expert))
    def _():
        _cast_rows(w1_ref, w1b_ref)
        _cast_rows(w2_ref, w2b_ref)

    @pl.when(active)
    def _():
        rowid = i * MOE_BLOCK + lax.broadcasted_iota(jnp.int32, (MOE_BLOCK, 1), 0)
        xs = jnp.where(rowid < ve_ref[i], xs_ref[...], 0.0)
        hdn = _dot(xs.astype(BF16), w1b_ref[...]) + b1_ref[...]
        g_ = jnp.minimum(hdn[:, :dff], SWIGLU_LIMIT)
        u_ = jnp.clip(hdn[:, dff:], -SWIGLU_LIMIT, SWIGLU_LIMIT)
        act = (u_ + 1.0) * (g_ * _sigmoid(g_ * SWIGLU_ALPHA))
        o_ref[...] = _dot(act.astype(BF16), w2b_ref[...]) + b2_ref[...]

    @pl.when(jnp.logical_not(active))
    def _():
        o_ref[...] = jnp.zeros_like(o_ref)


def _moe_experts(xs, block_e, n_used, valid_end, w1, b1, w2, b2, layer):
    rows, dm = xs.shape
    nb = rows // MOE_BLOCK
    depth, ne, _, dff2 = w1.shape
    dff = dff2 // 2
    grid_spec = pltpu.PrefetchScalarGridSpec(
        num_scalar_prefetch=3,
        grid=(nb,),
        in_specs=[pl.BlockSpec((MOE_BLOCK, dm), lambda i, be, nu, ve: (i, 0)),
                  pl.BlockSpec((None, None, dm, dff2), lambda i, be, nu, ve: (layer, be[i], 0, 0)),
                  pl.BlockSpec((None, None, 1, dff2), lambda i, be, nu, ve: (layer, be[i], 0, 0)),
                  pl.BlockSpec((None, None, dff, dm), lambda i, be, nu, ve: (layer, be[i], 0, 0)),
                  pl.BlockSpec((None, None, 1, dm), lambda i, be, nu, ve: (layer, be[i], 0, 0))],
        out_specs=pl.BlockSpec((MOE_BLOCK, dm), lambda i, be, nu, ve: (i, 0)),
        scratch_shapes=[pltpu.VMEM((dm, dff2), BF16), pltpu.VMEM((dff, dm), BF16)],
    )
    return pl.pallas_call(
        functools.partial(_moe_body, dff=dff),
        grid_spec=grid_spec,
        out_shape=jax.ShapeDtypeStruct((rows, dm), F32),
        compiler_params=_cparams(1),
        name="moe_experts",
    )(block_e, n_used, valid_end, xs, w1, b1.reshape(depth, ne, 1, dff2), w2, b2.reshape(depth, ne, 1, dm))


PLAN_TILE = 512
MOE_BLOCK_SHIFT = 8


def _moe_plan_body(e_ref, dest_ref, meta_ref, rank_ref, *, n_tokens, meta_lanes):
    tiles_per_row = n_tokens // PLAN_TILE
    n_tiles = TOP_K * tiles_per_row
    expert = lax.broadcasted_iota(jnp.int32, (N_EXPERTS, PLAN_TILE), 0)
    r_i = lax.broadcasted_iota(jnp.int32, (PLAN_TILE, PLAN_TILE), 0)
    c_i = lax.broadcasted_iota(jnp.int32, (PLAN_TILE, PLAN_TILE), 1)
    earlier = (r_i < c_i).astype(BF16)

    def tile_hits(it):
        j = it // tiles_per_row
        lanes = pl.ds(pl.multiple_of((it % tiles_per_row) * PLAN_TILE, PLAN_TILE), PLAN_TILE)
        return j, lanes, e_ref[pl.ds(j, 1), lanes] == expert

    def rank_step(it, seen):
        j, lanes, hit = tile_hits(it)
        hitf = hit.astype(F32)
        prior = _dot(hit.astype(BF16), earlier) + seen
        rank_ref[pl.ds(j, 1), lanes] = jnp.sum(hitf * prior, axis=0, keepdims=True)
        return seen + jnp.sum(hitf, axis=1, keepdims=True)

    dest_ref[...] = jnp.zeros_like(dest_ref)
    rank_ref[...] = jnp.zeros_like(rank_ref)
    counts = lax.fori_loop(0, n_tiles, rank_step, jnp.zeros((N_EXPERTS, 1), F32))
    padded = ((counts.astype(jnp.int32) + (MOE_BLOCK - 1)) >> MOE_BLOCK_SHIFT) << MOE_BLOCK_SHIFT
    er = lax.broadcasted_iota(jnp.int32, (N_EXPERTS, N_EXPERTS), 0)
    ec = lax.broadcasted_iota(jnp.int32, (N_EXPERTS, N_EXPERTS), 1)
    seg_end = _dot_exact_lhs((ec <= er).astype(BF16),
                             jnp.broadcast_to(padded.astype(F32), (N_EXPERTS, LANE)))[:, 0:1]
    seg_start = seg_end - padded.astype(F32)

    def dest_step(it, carry):
        j, lanes, hit = tile_hits(it)
        base = jnp.sum(jnp.where(hit, seg_start, 0.0), axis=0, keepdims=True)
        dest_ref[pl.ds(j, 1), lanes] = (rank_ref[pl.ds(j, 1), lanes] + base).astype(jnp.int32)
        return carry

    lax.fori_loop(0, n_tiles, dest_step, 0)
    blk_start = (lax.broadcasted_iota(jnp.int32, (N_EXPERTS, meta_lanes), 1) * MOE_BLOCK).astype(F32)
    blk_expert = jnp.minimum(jnp.sum((seg_end <= blk_start).astype(F32), axis=0, keepdims=True), N_EXPERTS - 1.0)
    mine = lax.broadcasted_iota(jnp.int32, (N_EXPERTS, meta_lanes), 0).astype(F32) == blk_expert
    valid_end = jnp.sum(jnp.where(mine, seg_start + counts, 0.0), axis=0, keepdims=True)
    n_used = jnp.broadcast_to(seg_end[N_EXPERTS - 1:N_EXPERTS, :] * (1.0 / MOE_BLOCK), (1, meta_lanes))
    mrow = lax.broadcasted_iota(jnp.int32, (SUBLANE, meta_lanes), 0)
    meta = jnp.where(mrow == 0, blk_expert, jnp.where(mrow == 1, valid_end, jnp.where(mrow == 2, n_used, 0.0)))
    meta_ref[...] = meta.astype(jnp.int32)


def _moe_plan(e_t, n_tokens, n_blocks):
    meta_lanes = -(-n_blocks // LANE) * LANE
    dest, meta = pl.pallas_call(
        functools.partial(_moe_plan_body, n_tokens=n_tokens, meta_lanes=meta_lanes),
        grid=(1,),
        in_specs=[_full((SUBLANE, n_tokens))],
        out_specs=[_full((SUBLANE, n_tokens)), _full((SUBLANE, meta_lanes))],
        out_shape=[jax.ShapeDtypeStruct((SUBLANE, n_tokens), jnp.int32),
                   jax.ShapeDtypeStruct((SUBLANE, meta_lanes), jnp.int32)],
        scratch_shapes=[pltpu.VMEM((SUBLANE, n_tokens), F32)],
        compiler_params=_cparams(1),
        name="moe_plan",
    )(e_t)
    return dest[:TOP_K], meta[0, :n_blocks], meta[1, :n_blocks], meta[2, :1]


def _combine_body(x1_ref, y0_ref, y1_ref, y2_ref, y3_ref, gate_ref, g_ref, b_ref, o_ref, *, alpha):
    gate = gate_ref[...]
    ffn = gate[:, 0:1] * y0_ref[...]
    for j, y_ref in enumerate((y1_ref, y2_ref, y3_ref), start=1):
        ffn = ffn + gate[:, j:j + 1] * y_ref[...]
    o_ref[...] = _layer_norm(alpha * x1_ref[...] + ffn, g_ref[...], b_ref[...])


def _combine(x1, yg, gate, ln_g, ln_b, alpha, tm=256):
    t, dm = x1.shape
    n_tiles = t // tm
    expert_rows = lambda j: pl.BlockSpec((tm, dm), lambda i: (i + j * n_tiles, 0))
    return pl.pallas_call(
        functools.partial(_combine_body, alpha=alpha),
        grid=(n_tiles,),
        in_specs=[pl.BlockSpec((tm, dm), lambda i: (i, 0))] + [expert_rows(j) for j in range(TOP_K)]
                 + [pl.BlockSpec((tm, LANE), lambda i: (i, 0)), _full((1, dm)), _full((1, dm))],
        out_specs=pl.BlockSpec((tm, dm), lambda i: (i, 0)),
        out_shape=jax.ShapeDtypeStruct((t, dm), F32),
        compiler_params=_cparams(1),
        name="combine_ln",
    )(x1, yg, yg, yg, yg, gate, ln_g.reshape(1, dm), ln_b.reshape(1, dm))


SC_CORES = 2
SC_SUBCORES = 16
SC_WORKERS = SC_CORES * SC_SUBCORES


def _sc_gather_rows(table, idx, window):
    n = idx.shape[0]
    dim = table.shape[1]
    n_steps = n // (SC_WORKERS * window)
    assert n_steps * window * SC_WORKERS == n and n_steps % 2 == 0 and window % SUBLANE == 0 and window <= LANE
    idx3 = idx.reshape(SC_WORKERS, n_steps, window)
    mesh = plsc.VectorSubcoreMesh(core_axis_name="c", subcore_axis_name="s",
                                  num_cores=SC_CORES, num_subcores=SC_SUBCORES)

    def body(table_hbm, idx_hbm, out_hbm, idx_v, rows_v, gsem, wsem):
        wid = lax.axis_index("s") * SC_CORES + lax.axis_index("c")
        pltpu.sync_copy(idx_hbm.at[wid], idx_v)

        def gather(j, buf):
            return pltpu.make_async_copy(table_hbm.at[idx_v.at[j]], rows_v.at[buf], gsem.at[buf])

        def write(j, buf):
            base = pl.multiple_of((wid * n_steps + j) * window, window)
            return pltpu.make_async_copy(rows_v.at[buf], out_hbm.at[pl.ds(base, window)], wsem.at[buf])

        gather(0, 0).start()

        @pl.loop(0, n_steps, step=2)
        def _(j0):
            for buf in range(2):
                j = j0 + buf
                gather(j, buf).wait()

                @pl.when(j >= 1)
                def _():
                    write(j - 1, 1 - buf).wait()

                @pl.when(j + 1 < n_steps)
                def _():
                    gather(j + 1, 1 - buf).start()

                write(j, buf).start()

        write(n_steps - 1, 1).wait()

    return pl.kernel(
        body, out_type=jax.ShapeDtypeStruct((n, dim), table.dtype), mesh=mesh,
        scratch_types=[pltpu.VMEM((n_steps, window), jnp.int32), pltpu.VMEM((2, window, dim), table.dtype),
                       pltpu.SemaphoreType.DMA((2,)), pltpu.SemaphoreType.DMA((2,))],
        name="sc_gather",
    )(table, idx3)


def _sc_scatter_rows(src, dest, n_out, window):
    t, dim = src.shape
    k = dest.shape[0]
    n_steps = t // (SC_WORKERS * window)
    assert n_steps * window * SC_WORKERS == t and n_steps % 2 == 0 and window % SUBLANE == 0 and window <= LANE
    idx3 = dest.reshape(k, SC_WORKERS, n_steps, window).transpose(1, 2, 0, 3).reshape(SC_WORKERS, n_steps * k, window)
    mesh = plsc.VectorSubcoreMesh(core_axis_name="c", subcore_axis_name="s",
                                  num_cores=SC_CORES, num_subcores=SC_SUBCORES)

    def body(src_hbm, idx_hbm, out_hbm, idx_v, rows_v, rsem, ssem):
        wid = lax.axis_index("s") * SC_CORES + lax.axis_index("c")
        pltpu.sync_copy(idx_hbm.at[wid], idx_v)

        def read(s, buf):
            base = pl.multiple_of((wid * n_steps + s) * window, window)
            return pltpu.make_async_copy(src_hbm.at[pl.ds(base, window)], rows_v.at[buf], rsem.at[buf])

        def scatter(s, j, buf):
            return pltpu.make_async_copy(rows_v.at[buf], out_hbm.at[idx_v.at[s * k + j]], ssem.at[buf])

        read(0, 0).start()

        @pl.loop(0, n_steps, step=2)
        def _(s0):
            for buf in range(2):
                s = s0 + buf
                read(s, buf).wait()

                @pl.when(s >= 1)
                def _():
                    for j in range(k):
                        scatter(s - 1, j, 1 - buf).wait()

                @pl.when(s + 1 < n_steps)
                def _():
                    read(s + 1, 1 - buf).start()

                for j in range(k):
                    scatter(s, j, buf).start()

        for j in range(k):
            scatter(n_steps - 1, j, 1).wait()

    return pl.kernel(
        body, out_type=jax.ShapeDtypeStruct((n_out, dim), src.dtype), mesh=mesh,
        scratch_types=[pltpu.VMEM((n_steps * k, window), jnp.int32), pltpu.VMEM((2, window, dim), src.dtype),
                       pltpu.SemaphoreType.DMA((2,)), pltpu.SemaphoreType.DMA((2,))],
        name="sc_scatter",
    )(src, idx3)


def _pad_cols(w, width):
    return jnp.pad(w, ((0, 0), (0, width - w.shape[1])))


def kernel(x, w_in, gmlp_ln_g, gmlp_ln_b, gmlp_ws, gmlp_bs, rwkv_mu, rwkv_w0, rwkv_w2, rwkv_a0, rwkv_a2, rwkv_g2, rwkv_k_k, rwkv_k_a, rwkv_r_k, rwkv_ln_g, rwkv_ln_b, mlstm_conv_w, mlstm_conv_b, mlstm_gate_b, mlstm_ln_g, w_out, ln1_g, ln1_b, router_w, router_b, exp_w1, exp_b1, exp_w2, exp_b2, ln2_g, ln2_b):
    batch, seq, dm = x.shape
    depth = w_in.shape[0]
    t = batch * seq
    gw = gmlp_ln_g.shape[1]
    rw = rwkv_w0.shape[2]
    mw = mlstm_ln_g.shape[1]
    g_proj = 2 * gw
    r_proj = 3 * rw + W_LORA + A_LORA + G_LORA
    alpha = (2 * depth) ** 0.25
    n_blocks = -(-t * TOP_K // MOE_BLOCK) + N_EXPERTS
    xf = x.reshape(t, dm)
    for l in range(depth):
        pg, pr, pm = _proj(xf, w_in, l, g_proj, r_proj)
        y_g = _gmlp(pg, gmlp_ln_g[l], gmlp_ln_b[l], gmlp_ws[l], gmlp_bs[l])
        r, v, a, kd, b, lw, bonus, rgate = _rwkv_prep(
            pr, seq, rwkv_mu[l], rwkv_w0[l], rwkv_w2[l], rwkv_a0[l], rwkv_a2[l], rwkv_g2[l],
            rwkv_k_k[l], rwkv_k_a[l], rwkv_r_k[l].reshape(-1))
        ro = _rwkv_scan(r, v, a, kd, b, lw, batch, seq)
        q, k, gates = _mlstm_prep(pm, seq, mlstm_conv_w[l], mlstm_conv_b[l], mlstm_gate_b[l], mw)
        mh = _mlstm_scan(q, k, pm, gates, batch, seq)
        x1, topi, gate = _mix_out(xf, y_g, ro, bonus, rgate, rwkv_ln_g[l], rwkv_ln_b[l], mh, pm,
                                  mlstm_ln_g[l], w_out, l, ln1_g[l], ln1_b[l], router_w[l], router_b[l], alpha)
        dest, block_e, valid_end, n_used = _moe_plan(topi, t, n_blocks)
        xs = _sc_scatter_rows(x1, dest, n_blocks * MOE_BLOCK, window=32)
        ys = _moe_experts(xs, block_e, n_used, valid_end, exp_w1, exp_b1, exp_w2, exp_b2, l)
        yg = _sc_gather_rows(ys, dest.reshape(-1), window=32)
        xf = _combine(x1, yg, gate, ln2_g[l], ln2_b[l], alpha)
    return xf.reshape(batch, seq, dm)
```

```python
import functools
import math

import jax
import jax.numpy as jnp
from jax import lax
from jax.experimental import pallas as pl
from jax.experimental.pallas import tpu as pltpu
from jax.experimental.pallas import tpu_sc as plsc

F32 = jnp.float32
BF16 = jnp.bfloat16
HI = lax.Precision.HIGHEST

HEAD_DIM = 64
GMLP_CHUNK = 128
MLSTM_CHUNK = 128
RWKV_CHUNK = 64
W_LORA = 64
A_LORA = 64
G_LORA = 128
N_EXPERTS = 32
TOP_K = 4
MOE_BLOCK = 256
SWIGLU_LIMIT = 7.0
SWIGLU_ALPHA = 1.702
LN_EPS = 1e-5
RWKV_GN_EPS = 64e-5
LANE = 128
SUBLANE = 8
VMEM_LIMIT = 48 * 1024 * 1024
NEG_BIG = -1e30


def _cparams(n_axes):
    return pltpu.CompilerParams(dimension_semantics=("arbitrary",) * n_axes,
                                vmem_limit_bytes=VMEM_LIMIT)


def _full(shape):
    return pl.BlockSpec(shape, lambda *_: (0,) * len(shape))


def _dot(a, b, precision=None):
    return jnp.dot(a, b, preferred_element_type=F32, precision=precision)


def _dot_nt(a, b, precision=None):
    return lax.dot_general(a, b, (((1,), (1,)), ((), ())), preferred_element_type=F32, precision=precision)


def _dot_tn(a, b, precision=None):
    return lax.dot_general(a, b, (((0,), (0,)), ((), ())), preferred_element_type=F32, precision=precision)


def _split(x):
    hi = x.astype(BF16)
    return hi, (x - hi.astype(F32)).astype(BF16)


def _split3(x):
    hi = x.astype(BF16)
    r1 = x - hi.astype(F32)
    mid = r1.astype(BF16)
    return hi, mid, (r1 - mid.astype(F32)).astype(BF16)


def _mm(a, b, mode, dot=_dot):
    if mode == "hi":
        return dot(a, b, HI)
    if mode == "b1":
        return dot(a.astype(BF16), b.astype(BF16))
    ah, al = _split(a)
    bh, bl = _split(b)
    return dot(ah, bh) + (dot(ah, bl) + dot(al, bh))


def _dot_exact_lhs(a_bf16, x):
    hi, mid, lo = _split3(x)
    return _dot(a_bf16, hi) + (_dot(a_bf16, mid) + _dot(a_bf16, lo))


def _dot_exact_rhs(x, b_bf16):
    hi, mid, lo = _split3(x)
    return _dot(hi, b_bf16) + (_dot(mid, b_bf16) + _dot(lo, b_bf16))


def _sigmoid(x):
    return 1.0 / (1.0 + jnp.exp(-x))


def _softplus(x):
    return jnp.maximum(x, 0.0) + jnp.log1p(jnp.exp(-jnp.abs(x)))


def _block_diag_ones(width):
    h = jnp.arange(width) // HEAD_DIM
    return (h[:, None] == h[None, :]).astype(F32)


CAST_ROWS = 128


def _cast_rows(src_ref, dst_ref):
    n_src, n_dst = src_ref.shape[1], dst_ref.shape[1]
    whole = n_src // LANE * LANE

    def step(r, carry):
        rows = pl.ds(pl.multiple_of(r * CAST_ROWS, CAST_ROWS), CAST_ROWS)
        dst_ref[rows, :whole] = src_ref[rows, :whole].astype(BF16)
        if n_dst > whole:
            tail = [src_ref[rows, whole:]] if n_src > whole else []
            tail.append(jnp.zeros((CAST_ROWS, n_dst - n_src), F32))
            dst_ref[rows, whole:] = jnp.concatenate(tail, axis=1).astype(BF16)
        return carry
    lax.fori_loop(0, src_ref.shape[0] // CAST_ROWS, step, 0)


def _proj_body(x_ref, w_ref, pg_ref, pr_ref, pm_ref, wb_ref, *, ng, nr):
    @pl.when(pl.program_id(0) == 0)
    def _():
        _cast_rows(w_ref, wb_ref)

    xb = x_ref[...].astype(BF16)
    pg_ref[...] = _dot(xb, wb_ref[:, :ng])
    pr_ref[...] = _dot(xb, wb_ref[:, ng:ng + nr])
    pm_ref[...] = _dot(xb, wb_ref[:, ng + nr:])


def _proj(x, w_in, layer, ng, nr, tm=256):
    t, d = x.shape
    p_in = w_in.shape[2]
    p_pad = -(-p_in // LANE) * LANE
    nm = p_pad - ng - nr
    row = lambda n: pl.BlockSpec((tm, n), lambda i: (i, 0))
    return pl.pallas_call(
        functools.partial(_proj_body, ng=ng, nr=nr),
        grid=(t // tm,),
        in_specs=[row(d), pl.BlockSpec((None, d, p_in), lambda i: (layer, 0, 0), pipeline_mode=pl.Buffered(1))],
        out_specs=[row(ng), row(nr), row(nm)],
        out_shape=[jax.ShapeDtypeStruct((t, n), F32) for n in (ng, nr, nm)],
        scratch_shapes=[pltpu.VMEM((d, p_pad), BF16)],
        compiler_params=_cparams(1),
        name="in_proj",
    )(x, w_in)


def _gmlp_body(pg_ref, lng_ref, lnb_ref, ws_ref, bst_ref, o_ref, *, gw, chunks):
    p = pg_ref[...]
    p = 0.5 * p * (1.0 + lax.erf(p * math.sqrt(0.5)))
    u, v = p[:, :gw], p[:, gw:]
    mu = jnp.mean(v, axis=-1, keepdims=True)
    vc = v - mu
    var = jnp.mean(vc * vc, axis=-1, keepdims=True)
    vn = vc * lax.rsqrt(var + LN_EPS) * lng_ref[...] + lnb_ref[...]
    n_heads = gw // HEAD_DIM
    for c in range(chunks):
        rows = slice(c * GMLP_CHUNK, (c + 1) * GMLP_CHUNK)
        ys = []
        for h in range(n_heads):
            cols = slice(h * HEAD_DIM, (h + 1) * HEAD_DIM)
            y = _dot(ws_ref[h], vn[rows, cols].astype(BF16)) + bst_ref[:, h:h + 1]
            ys.append(y)
        o_ref[rows, :] = u[rows, :] * jnp.concatenate(ys, axis=1)


def _gmlp(pg, ln_g, ln_b, ws, bs, chunks=4):
    t = pg.shape[0]
    gw = pg.shape[1] // 2
    n_heads = gw // HEAD_DIM
    tm = chunks * GMLP_CHUNK
    bst = jnp.zeros((GMLP_CHUNK, LANE), F32).at[:, :n_heads].set(bs.T)
    return pl.pallas_call(
        functools.partial(_gmlp_body, gw=gw, chunks=chunks),
        grid=(t // tm,),
        in_specs=[pl.BlockSpec((tm, 2 * gw), lambda i: (i, 0)), _full((1, gw)), _full((1, gw)),
                  _full((n_heads, GMLP_CHUNK, GMLP_CHUNK)), _full((GMLP_CHUNK, LANE))],
        out_specs=pl.BlockSpec((tm, gw), lambda i: (i, 0)),
        out_shape=jax.ShapeDtypeStruct((t, gw), F32),
        compiler_params=_cparams(1),
        name="gmlp",
    )(pg, ln_g.reshape(1, gw), ln_b.reshape(1, gw), ws.astype(BF16), bst)


def _halo_specs(tm, width, n_rows):
    per8 = tm // SUBLANE
    last = n_rows // SUBLANE - 1
    prev = pl.BlockSpec((SUBLANE, width), lambda i: (jnp.maximum(i * per8 - 1, 0), 0))
    nxt = pl.BlockSpec((SUBLANE, width), lambda i: (jnp.minimum((i + 1) * per8, last), 0))
    return prev, nxt


def _neighbours(cur, prev_blk, next_blk, tiles_per_seq):
    tm = cur.shape[0]
    j = pl.program_id(0) % tiles_per_seq
    prev_row = jnp.where(j > 0, prev_blk[SUBLANE - 1:SUBLANE, :], 0.0)
    next_row = jnp.where(j < tiles_per_seq - 1, next_blk[0:1, :], 0.0)
    ridx = lax.broadcasted_iota(jnp.int32, cur.shape, 0)
    before = jnp.where(ridx == 0, prev_row, pltpu.roll(cur, 1, 0))
    after = jnp.where(ridx == tm - 1, next_row, pltpu.roll(cur, tm - 1, 0))
    return before, after


def _rwkv_prep_body(pr_ref, prev_ref, next_ref, mu_ref, w0_ref, w2_ref, a0_ref, a2_ref, g2_ref,
                    kk_ref, ka_ref, rk_ref, bd_ref,
                    r_out, v_out, a_out, kd_out, b_out, lw_out, bonus_out, gate_out, *, rw, tiles_per_seq):
    pf = pr_ref[...]
    before, after = _neighbours(pf, prev_ref[...], next_ref[...], tiles_per_seq)
    pf = pf + mu_ref[0:1, :] * (before - pf) + mu_ref[1:2, :] * (after - pf)
    o3 = 3 * rw
    r, k, v = pf[:, :rw], pf[:, rw:2 * rw], pf[:, 2 * rw:o3]
    wd = pf[:, o3:o3 + W_LORA]
    ad = pf[:, o3 + W_LORA:o3 + W_LORA + A_LORA]
    gd = pf[:, o3 + W_LORA + A_LORA:]
    bd = bd_ref[...]
    kk = k * kk_ref[...]
    ss = _dot_exact_rhs(kk * kk, bd)
    kk = kk / jnp.maximum(jnp.sqrt(ss), 1e-12)
    twd = jnp.tanh(wd)
    ksum = jnp.zeros_like(k)
    for d in range(2):
        w_log = -_softplus(-(w0_ref[d:d + 1, :] + _mm(twd, w2_ref[d], "b3"))) - 0.5
        lw_out[d] = -jnp.exp(w_log)
        iclr = _sigmoid(a0_ref[d:d + 1, :] + _mm(ad, a2_ref[d], "b3"))
        kd = k * (1.0 + (iclr - 1.0) * ka_ref[...])
        kd_out[d] = kd
        b_out[d] = kk * iclr
        ksum = ksum + kd
    r_out[...] = r
    v_out[...] = v
    a_out[...] = -kk
    bonus_out[...] = _dot_exact_rhs(r * ksum * rk_ref[...], bd) * v
    gate_out[...] = _dot(_sigmoid(gd).astype(BF16), g2_ref[...])


def _rwkv_prep(pr, seq, mu, w0, w2, a0, a2, g2, k_k, k_a, r_k, tm=256):
    t, rproj = pr.shape
    rw = w0.shape[1]
    tiles_per_seq = seq // tm
    prev, nxt = _halo_specs(tm, rproj, t)
    row = pl.BlockSpec((tm, rw), lambda i: (i, 0))
    row2 = pl.BlockSpec((2, tm, rw), lambda i: (0, i, 0))
    one = jax.ShapeDtypeStruct((t, rw), F32)
    two = jax.ShapeDtypeStruct((2, t, rw), F32)
    return pl.pallas_call(
        functools.partial(_rwkv_prep_body, rw=rw, tiles_per_seq=tiles_per_seq),
        grid=(t // tm,),
        in_specs=[pl.BlockSpec((tm, rproj), lambda i: (i, 0)), prev, nxt,
                  _full((2, rproj)), _full((2, rw)), _full((2, W_LORA, rw)), _full((2, rw)),
                  _full((2, A_LORA, rw)), _full((G_LORA, rw)), _full((1, rw)), _full((1, rw)),
                  _full((1, rw)), _full((rw, rw))],
        out_specs=[row, row, row, row2, row2, row2, row, row],
        out_shape=[one, one, one, two, two, two, one, one],
        compiler_params=_cparams(1),
        name="rwkv_prep",
    )(pr, pr, pr, mu, w0, w2, a0, a2, g2.astype(BF16), k_k.reshape(1, rw), k_a.reshape(1, rw),
      r_k.reshape(1, rw), _block_diag_ones(rw).astype(BF16))


P_G, P_INV, P_APPLY, P_STATE, P_SEQ = "b1", "b1", "b1", "b1", "b3"


def _rwkv_intra_body(r_ref, v_ref, a_ref, kd_ref, b_ref, lw_ref, rq_out, o0_out, mtx_out, hc_out,
                     *, n_heads, chunks):
    L = RWKV_CHUNK
    d = pl.program_id(0)
    row = lax.broadcasted_iota(jnp.int32, (L, L), 0)
    col = lax.broadcasted_iota(jnp.int32, (L, L), 1)
    fwd = d == 0
    rel = (col - row) * (1 - 2 * d)
    incl = rel <= 0
    strict = rel < 0
    eye = (row == col).astype(F32)
    tri = incl.astype(BF16)
    pairs = []
    for c in range(chunks):
        rows = slice(c * L, (c + 1) * L)
        lw = lw_ref[rows, :]
        cum = _dot_exact_lhs(tri, lw)
        tot = jnp.where(fwd, cum[L - 1:L, :], cum[0:1, :])
        e_neg = jnp.exp(-cum)
        e_end = jnp.exp(tot - cum)
        e_tot = jnp.exp(tot)
        r, v, a, kd, b = r_ref[rows, :], v_ref[rows, :], a_ref[rows, :], kd_ref[rows, :], b_ref[rows, :]
        at, rt, bt, kt = a * jnp.exp(cum - lw), r * jnp.exp(cum), b * e_neg, kd * e_neg
        kend, bend = kd * e_end, b * e_end
        for h in range(n_heads):
            sl = slice(h * HEAD_DIM, (h + 1) * HEAD_DIM)
            pairs.append(dict(at=at[:, sl], rt=rt[:, sl], bt=bt[:, sl], kt=kt[:, sl], v=v[:, sl],
                              kend=kend[:, sl], bend=bend[:, sl], e_tot=e_tot[:, sl]))
    for p in pairs:
        p["g"] = _mm(jnp.concatenate([p["at"], p["rt"]], axis=0),
                     jnp.concatenate([p["bt"], p["kt"]], axis=0), P_G, _dot_nt)
    for p in pairs:
        g = p.pop("g")
        p["pw"] = jnp.where(strict, g[:L, :L], 0.0)
        p["a_ak"] = jnp.where(strict, g[:L, L:], 0.0)
        p["m_rb"] = jnp.where(incl, g[L:, :L], 0.0)
        p["m_rk"] = jnp.where(incl, g[L:, L:], 0.0)
        p["inv"] = eye + p["pw"]
    for _ in range(int(math.log2(L)) - 1):
        for p in pairs:
            p["pw"] = _mm(p["pw"], p["pw"], P_INV)
        for p in pairs:
            p["inv"] = p["inv"] + _mm(p["inv"], p["pw"], P_INV)
    for p in pairs:
        p["akv"] = _mm(p["a_ak"], p["v"], P_APPLY)
    for p in pairs:
        p["wu"] = _mm(p["inv"], jnp.concatenate([p["at"], p["akv"]], axis=1), P_APPLY)
    for p in pairs:
        p["mwu"] = _mm(p["m_rb"], p["wu"], P_APPLY)
    for p in pairs:
        p["o0"] = p["mwu"][:, HEAD_DIM:] + _mm(p["m_rk"], p["v"], P_APPLY)
    for p in pairs:
        p["bw"] = _mm(p["bend"], p["wu"], P_STATE, _dot_tn)
    for p in pairs:
        p["hc"] = _mm(p["kend"], p["v"], P_STATE, _dot_tn) + p["bw"][:, HEAD_DIM:]
    for c in range(chunks):
        ps = pairs[c * n_heads:(c + 1) * n_heads]
        rows = slice(c * L, (c + 1) * L)
        krows = slice(c * HEAD_DIM, (c + 1) * HEAD_DIM)
        rq_out[rows, :] = jnp.concatenate([p["rt"] + p["mwu"][:, :HEAD_DIM] for p in ps], axis=1)
        o0_out[rows, :] = jnp.concatenate([p["o0"] for p in ps], axis=1)
        mtx_out[krows, :] = jnp.concatenate([eye * p["e_tot"] + p["bw"][:, :HEAD_DIM] for p in ps], axis=1)
        hc_out[krows, :] = jnp.concatenate([p["hc"] for p in ps], axis=1)


def _rwkv_intra(r, v, a, kd, b, lw, chunks=4):
    t, rw = r.shape
    n_heads = rw // HEAD_DIM
    tm = chunks * RWKV_CHUNK
    tk = chunks * HEAD_DIM
    n_tiles = t // tm
    one = pl.BlockSpec((tm, rw), lambda d, i: (i, 0))
    two = pl.BlockSpec((None, tm, rw), lambda d, i: (d, i, 0))
    twok = pl.BlockSpec((None, tk, rw), lambda d, i: (d, i, 0))
    return pl.pallas_call(
        functools.partial(_rwkv_intra_body, n_heads=n_heads, chunks=chunks),
        grid=(2, n_tiles),
        in_specs=[one, one, one, two, two, two],
        out_specs=[two, two, twok, twok],
        out_shape=[jax.ShapeDtypeStruct((2, t, rw), F32), jax.ShapeDtypeStruct((2, t, rw), F32),
                   jax.ShapeDtypeStruct((2, n_tiles * tk, rw), F32),
                   jax.ShapeDtypeStruct((2, n_tiles * tk, rw), F32)],
        compiler_params=_cparams(2),
        name="rwkv_intra",
    )(r, v, a, kd, b, lw)


def _rwkv_seq_body(rq0, o00, mtx0, hc0, rq1, o01, mtx1, hc1, out0, out1, h_ref, *, n_heads, batch):
    c = pl.program_id(0)

    @pl.when(c == 0)
    def _():
        h_ref[...] = jnp.zeros_like(h_ref)

    L = RWKV_CHUNK
    for d, (rq, o0, mtx, hc, out) in enumerate(((rq0, o00, mtx0, hc0, out0), (rq1, o01, mtx1, hc1, out1))):
        for bi in range(batch):
            rq_t, mtx_t = rq[bi], mtx[bi]
            state = h_ref[d, bi]
            outs, states = [], []
            for h in range(n_heads):
                sl = slice(h * HEAD_DIM, (h + 1) * HEAD_DIM)
                prod = _mm(jnp.concatenate([rq_t[:, sl], mtx_t[:, sl]], axis=0), state[:, sl], P_SEQ)
                outs.append(prod[:L])
                states.append(prod[L:])
            out[bi] = jnp.concatenate(outs, axis=1) + o0[bi]
            h_ref[d, bi] = jnp.concatenate(states, axis=1) + hc[bi]


def _rwkv_seq(rq, o0, mtx, hc, batch, seq):
    _, t, rw = rq.shape
    n_heads = rw // HEAD_DIM
    L = RWKV_CHUNK
    nc = seq // L
    as4 = lambda x: x.reshape(2, batch, x.shape[1] // batch, rw)
    rq, o0, mtx, hc = as4(rq), as4(o0), as4(mtx), as4(hc)
    fwd = lambda rows: pl.BlockSpec((None, batch, rows, rw), lambda c: (0, 0, c, 0))
    bwd = lambda rows: pl.BlockSpec((None, batch, rows, rw), lambda c: (1, 0, nc - 1 - c, 0))
    out0, out1 = pl.pallas_call(
        functools.partial(_rwkv_seq_body, n_heads=n_heads, batch=batch),
        grid=(nc,),
        in_specs=[fwd(L), fwd(L), fwd(HEAD_DIM), fwd(HEAD_DIM), bwd(L), bwd(L), bwd(HEAD_DIM), bwd(HEAD_DIM)],
        out_specs=[pl.BlockSpec((batch, L, rw), lambda c: (0, c, 0)),
                   pl.BlockSpec((batch, L, rw), lambda c: (0, nc - 1 - c, 0))],
        out_shape=[jax.ShapeDtypeStruct((batch, seq, rw), F32)] * 2,
        scratch_shapes=[pltpu.VMEM((2, batch, HEAD_DIM, rw), F32)],
        compiler_params=_cparams(1),
        name="rwkv_seq",
    )(rq, o0, mtx, hc, rq, o0, mtx, hc)
    return out0.reshape(t, rw), out1.reshape(t, rw)


def _rwkv_scan(r, v, a, kd, b, lw, batch, seq):
    rq, o0, mtx, hc = _rwkv_intra(r, v, a, kd, b, lw)
    return _rwkv_seq(rq, o0, mtx, hc, batch, seq)


def _mlstm_prep_body(qk_ref, prev_ref, next_ref, g_ref, cw_ref, cb_ref, gb_ref, q_out, k_out, gate_out,
                     *, mw, n_heads, tiles_per_seq):
    x = qk_ref[...]
    before, after = _neighbours(x, prev_ref[...], next_ref[...], tiles_per_seq)
    y = cb_ref[...] + before * cw_ref[0:1, :] + x * cw_ref[1:2, :] + after * cw_ref[2:3, :]
    y = y * _sigmoid(y)
    q_out[...] = y[:, :mw]
    k_out[...] = y[:, mw:] * (HEAD_DIM ** -0.5)
    g = g_ref[...] + gb_ref[...]
    lane = lax.broadcasted_iota(jnp.int32, g.shape, 1)
    for d in range(2):
        ig = g if d == 0 else pltpu.roll(g, LANE - n_heads, 1)
        fg = pltpu.roll(g, LANE - (1 + d) * n_heads, 1)
        lf = -_softplus(-fg)
        gate_out[d] = jnp.where(lane < n_heads, ig, jnp.where(lane < 2 * n_heads, lf, 0.0))


def _mlstm_prep(pm, seq, conv_w, conv_b, gate_b, mw, tm=256):
    t = pm.shape[0]
    n_heads = mw // HEAD_DIM
    tiles_per_seq = seq // tm
    w2 = 2 * mw
    prev, nxt = _halo_specs(tm, w2, t)
    gcol = (4 * mw) // LANE
    gb = jnp.zeros((1, LANE), F32).at[0, :4 * n_heads].set(gate_b)
    row = pl.BlockSpec((tm, mw), lambda i: (i, 0))
    return pl.pallas_call(
        functools.partial(_mlstm_prep_body, mw=mw, n_heads=n_heads, tiles_per_seq=tiles_per_seq),
        grid=(t // tm,),
        in_specs=[pl.BlockSpec((tm, w2), lambda i: (i, 0)), prev, nxt,
                  pl.BlockSpec((tm, LANE), lambda i: (i, gcol)),
                  _full((3, w2)), _full((1, w2)), _full((1, LANE))],
        out_specs=[row, row, pl.BlockSpec((2, tm, LANE), lambda i: (0, i, 0))],
        out_shape=[jax.ShapeDtypeStruct((t, mw), F32), jax.ShapeDtypeStruct((t, mw), F32),
                   jax.ShapeDtypeStruct((2, t, LANE), F32)],
        compiler_params=_cparams(1),
        name="mlstm_prep",
    )(pm, pm, pm, pm, conv_w, conv_b.reshape(1, w2), gb)


def _mlstm_scan_body(q0_ref, k0_ref, v0_ref, g0_ref, q1_ref, k1_ref, v1_ref, g1_ref, o0_ref, o1_ref,
                     c_ref, m_ref, *, n_heads):
    L = MLSTM_CHUNK

    @pl.when(pl.program_id(1) == 0)
    def _():
        c_ref[...] = jnp.zeros_like(c_ref)
        m_ref[...] = jnp.zeros_like(m_ref)

    row = lax.broadcasted_iota(jnp.int32, (L, L), 0)
    col = lax.broadcasted_iota(jnp.int32, (L, L), 1)
    lane64 = lax.broadcasted_iota(jnp.int32, (L, HEAD_DIM), 1)
    ones_col = (lane64 == 0).astype(F32)
    hs = []
    for d, (q_ref, k_ref, v_ref, g_ref) in enumerate(((q0_ref, k0_ref, v0_ref, g0_ref),
                                                      (q1_ref, k1_ref, v1_ref, g1_ref))):
        incl = (col <= row) if d == 0 else (col >= row)
        last = L - 1 if d == 0 else 0
        g = g_ref[...]
        bcum = _dot_exact_lhs(incl.astype(BF16), g)
        g_t = g.T
        bcum_t = bcum.T
        q, k, v = q_ref[...], k_ref[...], v_ref[...]
        for h in range(n_heads):
            sl = slice(h * HEAD_DIM, (h + 1) * HEAD_DIM)
            bc = bcum[:, n_heads + h:n_heads + h + 1]
            br = bcum_t[n_heads + h:n_heads + h + 1, :]
            igr = g_t[h:h + 1, :]
            igc = g[:, h:h + 1]
            m_st = m_ref[d, h:h + 1, 0:1]
            dm = jnp.where(incl, bc - br + igr, -jnp.inf)
            inter = bc + m_st
            m_t = jnp.maximum(jnp.max(dm, axis=1, keepdims=True), inter)
            b_last = bc[last:last + 1, :]
            lwc = b_last - bc + igc
            m_new = jnp.maximum(b_last + m_st, jnp.max(lwc, axis=0, keepdims=True))
            hs.append(dict(qh=q[:, sl].astype(BF16), kh=k[:, sl],
                           vext=jnp.concatenate([v[:, sl], ones_col], axis=1).astype(BF16),
                           decay=jnp.exp(dm - m_t), w_inter=jnp.exp(inter - m_t), floor=jnp.exp(-m_t),
                           wts=jnp.exp(lwc - m_new), dec=jnp.exp(b_last + m_st - m_new), m_new=m_new,
                           cst=c_ref[d, h]))
    for p in hs:
        p["sc"] = (_dot_nt(p["qh"], p["kh"].astype(BF16)) * p["decay"]).astype(BF16)
    for p in hs:
        p["numext"] = _dot(p["sc"], p["vext"]) + p["w_inter"] * _dot(p["qh"], p["cst"].astype(BF16))
    for p in hs:
        p["upd"] = _dot_tn((p["wts"] * p["kh"]).astype(BF16), p["vext"])
    for d, o_ref in enumerate((o0_ref, o1_ref)):
        outs = []
        for h in range(n_heads):
            p = hs[d * n_heads + h]
            num = p["numext"][:, :HEAD_DIM]
            den = p["numext"][:, HEAD_DIM:HEAD_DIM + 1]
            outs.append(num / jnp.maximum(jnp.abs(den), p["floor"]))
            c_ref[d, h] = p["dec"] * p["cst"] + p["upd"]
            m_ref[d, h:h + 1, :] = jnp.broadcast_to(p["m_new"], (1, LANE))
        o_ref[...] = jnp.concatenate(outs, axis=1)


def _mlstm_scan(q, k, pm, gates, batch, seq):
    t, mw = q.shape
    n_heads = mw // HEAD_DIM
    L = MLSTM_CHUNK
    nc = seq // L
    fwd = lambda bi, c: bi * nc + c
    bwd = lambda bi, c: bi * nc + nc - 1 - c
    specs = []
    for d, blk in enumerate((fwd, bwd)):
        specs += [pl.BlockSpec((L, mw), lambda bi, c, blk=blk: (blk(bi, c), 0)),
                  pl.BlockSpec((L, mw), lambda bi, c, blk=blk: (blk(bi, c), 0)),
                  pl.BlockSpec((L, mw), lambda bi, c, blk=blk: (blk(bi, c), 2)),
                  pl.BlockSpec((None, L, LANE), lambda bi, c, blk=blk, d=d: (d, blk(bi, c), 0))]
    return pl.pallas_call(
        functools.partial(_mlstm_scan_body, n_heads=n_heads),
        grid=(batch, nc),
        in_specs=specs,
        out_specs=[pl.BlockSpec((L, mw), lambda bi, c: (fwd(bi, c), 0)),
                   pl.BlockSpec((L, mw), lambda bi, c: (bwd(bi, c), 0))],
        out_shape=[jax.ShapeDtypeStruct((t, mw), F32)] * 2,
        scratch_shapes=[pltpu.VMEM((2, n_heads, HEAD_DIM, LANE), F32), pltpu.VMEM((2, SUBLANE, LANE), F32)],
        compiler_params=_cparams(2),
        name="mlstm_scan",
    )(q, k, pm, gates, q, k, pm, gates)


def _layer_norm(x, g, b):
    mu = jnp.mean(x, axis=-1, keepdims=True)
    xc = x - mu
    var = jnp.mean(xc * xc, axis=-1, keepdims=True)
    return xc * lax.rsqrt(var + LN_EPS) * g + b


def _head_norm(x, bd_mean, eps):
    mu = _dot_exact_rhs(x, bd_mean)
    xc = x - mu
    var = _dot_exact_rhs(xc * xc, bd_mean)
    return xc * lax.rsqrt(var + eps)


def _mix_out_body(x_ref, yg_ref, ro0_ref, ro1_ref, bonus_ref, rgate_ref, rlg_ref, rlb_ref, mh0_ref, mh1_ref, og_ref,
                  mlg_ref, w_ref, l1g_ref, l1b_ref, rw_ref, rb_ref, bdm_ref,
                  x1_out, topi_out, gate_out, wb_ref, *, alpha, gw, rw):
    @pl.when(pl.program_id(0) == 0)
    def _():
        _cast_rows(w_ref, wb_ref)

    bdm = bdm_ref[...]
    yr = _head_norm(ro0_ref[...] + ro1_ref[...], bdm, RWKV_GN_EPS) * rlg_ref[...] + rlb_ref[...]
    yr = (yr + bonus_ref[...]) * rgate_ref[...]
    ym = _sigmoid(og_ref[...]) * (_head_norm(mh0_ref[...] + mh1_ref[...], bdm, LN_EPS) * mlg_ref[...])
    mix = (_dot(yg_ref[...].astype(BF16), wb_ref[:gw, :]) + _dot(yr.astype(BF16), wb_ref[gw:gw + rw, :])
           + _dot(ym.astype(BF16), wb_ref[gw + rw:, :]))
    x1 = _layer_norm(alpha * x_ref[...] + mix, l1g_ref[...], l1b_ref[...])
    x1_out[...] = x1
    lg = _mm(x1, rw_ref[...], "b3") + rb_ref[...]
    lane = lax.broadcasted_iota(jnp.int32, lg.shape, 1)
    vals, topi = [], jnp.zeros(lg.shape, jnp.int32)
    for j in range(TOP_K):
        mx = jnp.max(lg, axis=1, keepdims=True)
        idx = jnp.min(jnp.where(lg == mx, lane, LANE), axis=1, keepdims=True)
        vals.append(mx)
        topi = jnp.where(lane == j, idx, topi)
        lg = jnp.where(lane == idx, -jnp.inf, lg)
    es = [jnp.exp(vj - vals[0]) for vj in vals]
    den = es[0] + es[1] + es[2] + es[3]
    gate = jnp.zeros(lg.shape, F32)
    for j in range(TOP_K):
        gate = jnp.where(lane == j, es[j] / den, gate)
    topi_out[...] = topi.T[:SUBLANE, :]
    gate_out[...] = gate


def _mix_out(x, yg, ro, bonus, rgate, rlg, rlb, mh, pm, mlg, w_out, layer, l1g, l1b, router_w, router_b, alpha,
             tm=256):
    t, dm = x.shape
    gw, rw, mw = yg.shape[1], bonus.shape[1], mh[0].shape[1]
    assert rw == mw
    rwp = jnp.zeros((dm, LANE), F32).at[:, :N_EXPERTS].set(router_w)
    rbp = jnp.full((1, LANE), NEG_BIG, F32).at[0, :N_EXPERTS].set(router_b)
    row = lambda n: pl.BlockSpec((tm, n), lambda i: (i, 0))
    vec = lambda n: _full((1, n))
    return pl.pallas_call(
        functools.partial(_mix_out_body, alpha=alpha, gw=gw, rw=rw),
        grid=(t // tm,),
        in_specs=[row(dm), row(gw), row(rw), row(rw), row(rw), row(rw), vec(rw), vec(rw), row(mw), row(mw),
                  pl.BlockSpec((tm, mw), lambda i: (i, 3)),
                  vec(mw),
                  pl.BlockSpec((None, dm, dm), lambda i: (layer, 0, 0), pipeline_mode=pl.Buffered(1)),
                  vec(dm), vec(dm), _full((dm, LANE)), vec(LANE), _full((rw, rw))],
        out_specs=[row(dm), pl.BlockSpec((SUBLANE, tm), lambda i: (0, i)), row(LANE)],
        out_shape=[jax.ShapeDtypeStruct((t, dm), F32),
                   jax.ShapeDtypeStruct((SUBLANE, t), jnp.int32), jax.ShapeDtypeStruct((t, LANE), F32)],
        scratch_shapes=[pltpu.VMEM((dm, dm), BF16)],
        compiler_params=_cparams(1),
        name="mix_out",
    )(x, yg, ro[0], ro[1], bonus, rgate, rlg.reshape(1, rw), rlb.reshape(1, rw), mh[0], mh[1], pm,
      mlg.reshape(1, mw), w_out, l1g.reshape(1, dm), l1b.reshape(1, dm), rwp, rbp,
      (_block_diag_ones(rw) / HEAD_DIM).astype(BF16))


def _moe_body(be_ref, nu_ref, ve_ref, xs_ref, w1_ref, b1_ref, w2_ref, b2_ref, o_ref, w1b_ref, w2b_ref, *, dff):
    i = pl.program_id(0)
    active = i < nu_ref[0]
    new_expert = jnp.logical_or(i == 0, be_ref[i] != be_ref[jnp.maximum(i - 1, 0)])

    @pl.when(jnp.logical_and(active, new_expert))
    def _():
        _cast_rows(w1_ref, w1b_ref)
        _cast_rows(w2_ref, w2b_ref)

    @pl.when(active)
    def _():
        rowid = i * MOE_BLOCK + lax.broadcasted_iota(jnp.int32, (MOE_BLOCK, 1), 0)
        xs = jnp.where(rowid < ve_ref[i], xs_ref[...], 0.0)
        hdn = _dot(xs.astype(BF16), w1b_ref[...]) + b1_ref[...]
        g_ = jnp.minimum(hdn[:, :dff], SWIGLU_LIMIT)
        u_ = jnp.clip(hdn[:, dff:], -SWIGLU_LIMIT, SWIGLU_LIMIT)
        act = (u_ + 1.0) * (g_ * _sigmoid(g_ * SWIGLU_ALPHA))
        o_ref[...] = _dot(act.astype(BF16), w2b_ref[...]) + b2_ref[...]

    @pl.when(jnp.logical_not(active))
    def _():
        o_ref[...] = jnp.zeros_like(o_ref)


def _moe_experts(xs, block_e, n_used, valid_end, w1, b1, w2, b2, layer):
    rows, dm = xs.shape
    nb = rows // MOE_BLOCK
    depth, ne, _, dff2 = w1.shape
    dff = dff2 // 2
    grid_spec = pltpu.PrefetchScalarGridSpec(
        num_scalar_prefetch=3,
        grid=(nb,),
        in_specs=[pl.BlockSpec((MOE_BLOCK, dm), lambda i, be, nu, ve: (i, 0)),
                  pl.BlockSpec((None, None, dm, dff2), lambda i, be, nu, ve: (layer, be[i], 0, 0)),
                  pl.BlockSpec((None, None, 1, dff2), lambda i, be, nu, ve: (layer, be[i], 0, 0)),
                  pl.BlockSpec((None, None, dff, dm), lambda i, be, nu, ve: (layer, be[i], 0, 0)),
                  pl.BlockSpec((None, None, 1, dm), lambda i, be, nu, ve: (layer, be[i], 0, 0))],
        out_specs=pl.BlockSpec((MOE_BLOCK, dm), lambda i, be, nu, ve: (i, 0)),
        scratch_shapes=[pltpu.VMEM((dm, dff2), BF16), pltpu.VMEM((dff, dm), BF16)],
    )
    return pl.pallas_call(
        functools.partial(_moe_body, dff=dff),
        grid_spec=grid_spec,
        out_shape=jax.ShapeDtypeStruct((rows, dm), F32),
        compiler_params=_cparams(1),
        name="moe_experts",
    )(block_e, n_used, valid_end, xs, w1, b1.reshape(depth, ne, 1, dff2), w2, b2.reshape(depth, ne, 1, dm))


PLAN_TILE = 512
MOE_BLOCK_SHIFT = 8


def _moe_plan_body(e_ref, dest_ref, meta_ref, rank_ref, *, n_tokens, meta_lanes):
    tiles_per_row = n_tokens // PLAN_TILE
    n_tiles = TOP_K * tiles_per_row
    expert = lax.broadcasted_iota(jnp.int32, (N_EXPERTS, PLAN_TILE), 0)
    r_i = lax.broadcasted_iota(jnp.int32, (PLAN_TILE, PLAN_TILE), 0)
    c_i = lax.broadcasted_iota(jnp.int32, (PLAN_TILE, PLAN_TILE), 1)
    earlier = (r_i < c_i).astype(BF16)

    def tile_hits(it):
        j = it // tiles_per_row
        lanes = pl.ds(pl.multiple_of((it % tiles_per_row) * PLAN_TILE, PLAN_TILE), PLAN_TILE)
        return j, lanes, e_ref[pl.ds(j, 1), lanes] == expert

    def rank_step(it, seen):
        j, lanes, hit = tile_hits(it)
        hitf = hit.astype(F32)
        prior = _dot(hit.astype(BF16), earlier) + seen
        rank_ref[pl.ds(j, 1), lanes] = jnp.sum(hitf * prior, axis=0, keepdims=True)
        return seen + jnp.sum(hitf, axis=1, keepdims=True)

    dest_ref[...] = jnp.zeros_like(dest_ref)
    rank_ref[...] = jnp.zeros_like(rank_ref)
    counts = lax.fori_loop(0, n_tiles, rank_step, jnp.zeros((N_EXPERTS, 1), F32))
    padded = ((counts.astype(jnp.int32) + (MOE_BLOCK - 1)) >> MOE_BLOCK_SHIFT) << MOE_BLOCK_SHIFT
    er = lax.broadcasted_iota(jnp.int32, (N_EXPERTS, N_EXPERTS), 0)
    ec = lax.broadcasted_iota(jnp.int32, (N_EXPERTS, N_EXPERTS), 1)
    seg_end = _dot_exact_lhs((ec <= er).astype(BF16),
                             jnp.broadcast_to(padded.astype(F32), (N_EXPERTS, LANE)))[:, 0:1]
    seg_start = seg_end - padded.astype(F32)

    def dest_step(it, carry):
        j, lanes, hit = tile_hits(it)
        base = jnp.sum(jnp.where(hit, seg_start, 0.0), axis=0, keepdims=True)
        dest_ref[pl.ds(j, 1), lanes] = (rank_ref[pl.ds(j, 1), lanes] + base).astype(jnp.int32)
        return carry

    lax.fori_loop(0, n_tiles, dest_step, 0)
    blk_start = (lax.broadcasted_iota(jnp.int32, (N_EXPERTS, meta_lanes), 1) * MOE_BLOCK).astype(F32)
    blk_expert = jnp.minimum(jnp.sum((seg_end <= blk_start).astype(F32), axis=0, keepdims=True), N_EXPERTS - 1.0)
    mine = lax.broadcasted_iota(jnp.int32, (N_EXPERTS, meta_lanes), 0).astype(F32) == blk_expert
    valid_end = jnp.sum(jnp.where(mine, seg_start + counts, 0.0), axis=0, keepdims=True)
    n_used = jnp.broadcast_to(seg_end[N_EXPERTS - 1:N_EXPERTS, :] * (1.0 / MOE_BLOCK), (1, meta_lanes))
    mrow = lax.broadcasted_iota(jnp.int32, (SUBLANE, meta_lanes), 0)
    meta = jnp.where(mrow == 0, blk_expert, jnp.where(mrow == 1, valid_end, jnp.where(mrow == 2, n_used, 0.0)))
    meta_ref[...] = meta.astype(jnp.int32)


def _moe_plan(e_t, n_tokens, n_blocks):
    meta_lanes = -(-n_blocks // LANE) * LANE
    dest, meta = pl.pallas_call(
        functools.partial(_moe_plan_body, n_tokens=n_tokens, meta_lanes=meta_lanes),
        grid=(1,),
        in_specs=[_full((SUBLANE, n_tokens))],
        out_specs=[_full((SUBLANE, n_tokens)), _full((SUBLANE, meta_lanes))],
        out_shape=[jax.ShapeDtypeStruct((SUBLANE, n_tokens), jnp.int32),
                   jax.ShapeDtypeStruct((SUBLANE, meta_lanes), jnp.int32)],
        scratch_shapes=[pltpu.VMEM((SUBLANE, n_tokens), F32)],
        compiler_params=_cparams(1),
        name="moe_plan",
    )(e_t)
    return dest[:TOP_K], meta[0, :n_blocks], meta[1, :n_blocks], meta[2, :1]


def _combine_body(x1_ref, y0_ref, y1_ref, y2_ref, y3_ref, gate_ref, g_ref, b_ref, o_ref, *, alpha):
    gate = gate_ref[...]
    ffn = gate[:, 0:1] * y0_ref[...]
    for j, y_ref in enumerate((y1_ref, y2_ref, y3_ref), start=1):
        ffn = ffn + gate[:, j:j + 1] * y_ref[...]
    o_ref[...] = _layer_norm(alpha * x1_ref[...] + ffn, g_ref[...], b_ref[...])


def _combine(x1, yg, gate, ln_g, ln_b, alpha, tm=256):
    t, dm = x1.shape
    n_tiles = t // tm
    expert_rows = lambda j: pl.BlockSpec((tm, dm), lambda i: (i + j * n_tiles, 0))
    return pl.pallas_call(
        functools.partial(_combine_body, alpha=alpha),
        grid=(n_tiles,),
        in_specs=[pl.BlockSpec((tm, dm), lambda i: (i, 0))] + [expert_rows(j) for j in range(TOP_K)]
                 + [pl.BlockSpec((tm, LANE), lambda i: (i, 0)), _full((1, dm)), _full((1, dm))],
        out_specs=pl.BlockSpec((tm, dm), lambda i: (i, 0)),
        out_shape=jax.ShapeDtypeStruct((t, dm), F32),
        compiler_params=_cparams(1),
        name="combine_ln",
    )(x1, yg, yg, yg, yg, gate, ln_g.reshape(1, dm), ln_b.reshape(1, dm))


SC_CORES = 2
SC_SUBCORES = 16
SC_WORKERS = SC_CORES * SC_SUBCORES


def _sc_gather_rows(table, idx, window):
    n = idx.shape[0]
    dim = table.shape[1]
    n_steps = n // (SC_WORKERS * window)
    assert n_steps * window * SC_WORKERS == n and n_steps % 2 == 0 and window % SUBLANE == 0 and window <= LANE
    idx3 = idx.reshape(SC_WORKERS, n_steps, window)
    mesh = plsc.VectorSubcoreMesh(core_axis_name="c", subcore_axis_name="s",
                                  num_cores=SC_CORES, num_subcores=SC_SUBCORES)

    def body(table_hbm, idx_hbm, out_hbm, idx_v, rows_v, gsem, wsem):
        wid = lax.axis_index("s") * SC_CORES + lax.axis_index("c")
        pltpu.sync_copy(idx_hbm.at[wid], idx_v)

        def gather(j, buf):
            return pltpu.make_async_copy(table_hbm.at[idx_v.at[j]], rows_v.at[buf], gsem.at[buf])

        def write(j, buf):
            base = pl.multiple_of((wid * n_steps + j) * window, window)
            return pltpu.make_async_copy(rows_v.at[buf], out_hbm.at[pl.ds(base, window)], wsem.at[buf])

        gather(0, 0).start()

        @pl.loop(0, n_steps, step=2)
        def _(j0):
            for buf in range(2):
                j = j0 + buf
                gather(j, buf).wait()

                @pl.when(j >= 1)
                def _():
                    write(j - 1, 1 - buf).wait()

                @pl.when(j + 1 < n_steps)
                def _():
                    gather(j + 1, 1 - buf).start()

                write(j, buf).start()

        write(n_steps - 1, 1).wait()

    return pl.kernel(
        body, out_type=jax.ShapeDtypeStruct((n, dim), table.dtype), mesh=mesh,
        scratch_types=[pltpu.VMEM((n_steps, window), jnp.int32), pltpu.VMEM((2, window, dim), table.dtype),
                       pltpu.SemaphoreType.DMA((2,)), pltpu.SemaphoreType.DMA((2,))],
        name="sc_gather",
    )(table, idx3)


def _sc_scatter_rows(src, dest, n_out, window):
    t, dim = src.shape
    k = dest.shape[0]
    n_steps = t // (SC_WORKERS * window)
    assert n_steps * window * SC_WORKERS == t and n_steps % 2 == 0 and window % SUBLANE == 0 and window <= LANE
    idx3 = dest.reshape(k, SC_WORKERS, n_steps, window).transpose(1, 2, 0, 3).reshape(SC_WORKERS, n_steps * k, window)
    mesh = plsc.VectorSubcoreMesh(core_axis_name="c", subcore_axis_name="s",
                                  num_cores=SC_CORES, num_subcores=SC_SUBCORES)

    def body(src_hbm, idx_hbm, out_hbm, idx_v, rows_v, rsem, ssem):
        wid = lax.axis_index("s") * SC_CORES + lax.axis_index("c")
        pltpu.sync_copy(idx_hbm.at[wid], idx_v)

        def read(s, buf):
            base = pl.multiple_of((wid * n_steps + s) * window, window)
            return pltpu.make_async_copy(src_hbm.at[pl.ds(base, window)], rows_v.at[buf], rsem.at[buf])

        def scatter(s, j, buf):
            return pltpu.make_async_copy(rows_v.at[buf], out_hbm.at[idx_v.at[s * k + j]], ssem.at[buf])

        read(0, 0).start()

        @pl.loop(0, n_steps, step=2)
        def _(s0):
            for buf in range(2):
                s = s0 + buf
                read(s, buf).wait()

                @pl.when(s >= 1)
                def _():
                    for j in range(k):
                        scatter(s - 1, j, 1 - buf).wait()

                @pl.when(s + 1 < n_steps)
                def _():
                    read(s + 1, 1 - buf).start()

                for j in range(k):
                    scatter(s, j, buf).start()

        for j in range(k):
            scatter(n_steps - 1, j, 1).wait()

    return pl.kernel(
        body, out_type=jax.ShapeDtypeStruct((n_out, dim), src.dtype), mesh=mesh,
        scratch_types=[pltpu.VMEM((n_steps * k, window), jnp.int32), pltpu.VMEM((2, window, dim), src.dtype),
                       pltpu.SemaphoreType.DMA((2,)), pltpu.SemaphoreType.DMA((2,))],
        name="sc_scatter",
    )(src, idx3)


def _pad_cols(w, width):
    return jnp.pad(w, ((0, 0), (0, width - w.shape[1])))


def kernel(x, w_in, gmlp_ln_g, gmlp_ln_b, gmlp_ws, gmlp_bs, rwkv_mu, rwkv_w0, rwkv_w2, rwkv_a0, rwkv_a2, rwkv_g2, rwkv_k_k, rwkv_k_a, rwkv_r_k, rwkv_ln_g, rwkv_ln_b, mlstm_conv_w, mlstm_conv_b, mlstm_gate_b, mlstm_ln_g, w_out, ln1_g, ln1_b, router_w, router_b, exp_w1, exp_b1, exp_w2, exp_b2, ln2_g, ln2_b):
    batch, seq, dm = x.shape
    depth = w_in.shape[0]
    t = batch * seq
    gw = gmlp_ln_g.shape[1]
    rw = rwkv_w0.shape[2]
    mw = mlstm_ln_g.shape[1]
    g_proj = 2 * gw
    r_proj = 3 * rw + W_LORA + A_LORA + G_LORA
    alpha = (2 * depth) ** 0.25
    n_blocks = -(-t * TOP_K // MOE_BLOCK) + N_EXPERTS
    xf = x.reshape(t, dm)
    for l in range(depth):
        pg, pr, pm = _proj(xf, w_in, l, g_proj, r_proj)
        y_g = _gmlp(pg, gmlp_ln_g[l], gmlp_ln_b[l], gmlp_ws[l], gmlp_bs[l])
        r, v, a, kd, b, lw, bonus, rgate = _rwkv_prep(
            pr, seq, rwkv_mu[l], rwkv_w0[l], rwkv_w2[l], rwkv_a0[l], rwkv_a2[l], rwkv_g2[l],
            rwkv_k_k[l], rwkv_k_a[l], rwkv_r_k[l].reshape(-1))
        ro = _rwkv_scan(r, v, a, kd, b, lw, batch, seq)
        q, k, gates = _mlstm_prep(pm, seq, mlstm_conv_w[l], mlstm_conv_b[l], mlstm_gate_b[l], mw)
        mh = _mlstm_scan(q, k, pm, gates, batch, seq)
        x1, topi, gate = _mix_out(xf, y_g, ro, bonus, rgate, rwkv_ln_g[l], rwkv_ln_b[l], mh, pm,
                                  mlstm_ln_g[l], w_out, l, ln1_g[l], ln1_b[l], router_w[l], router_b[l], alpha)
        dest, block_e, valid_end, n_used = _moe_plan(topi, t, n_blocks)
        xs = _sc_scatter_rows(x1, dest, n_blocks * MOE_BLOCK, window=32)
        ys = _moe_experts(xs, block_e, n_used, valid_end, exp_w1, exp_b1, exp_w2, exp_b2, l)
        yg = _sc_gather_rows(ys, dest.reshape(-1), window=32)
        xf = _combine(x1, yg, gate, ln2_g[l], ln2_b[l], alpha)
    return xf.reshape(batch, seq, dm)
```

```python
import functools
import math

import jax
import jax.numpy as jnp
from jax import lax
from jax.experimental import pallas as pl
from jax.experimental.pallas import tpu as pltpu
from jax.experimental.pallas import tpu_sc as plsc

F32 = jnp.float32
BF16 = jnp.bfloat16
HI = lax.Precision.HIGHEST

HEAD_DIM = 64
GMLP_CHUNK = 128
MLSTM_CHUNK = 128
RWKV_CHUNK = 64
W_LORA = 64
A_LORA = 64
G_LORA = 128
N_EXPERTS = 32
TOP_K = 4
MOE_BLOCK = 256
SWIGLU_LIMIT = 7.0
SWIGLU_ALPHA = 1.702
LN_EPS = 1e-5
RWKV_GN_EPS = 64e-5
LANE = 128
SUBLANE = 8
VMEM_LIMIT = 48 * 1024 * 1024
NEG_BIG = -1e30


def _cparams(n_axes):
    return pltpu.CompilerParams(dimension_semantics=("arbitrary",) * n_axes,
                                vmem_limit_bytes=VMEM_LIMIT)


def _full(shape):
    return pl.BlockSpec(shape, lambda *_: (0,) * len(shape))


def _dot(a, b, precision=None):
    return jnp.dot(a, b, preferred_element_type=F32, precision=precision)


def _dot_nt(a, b, precision=None):
    return lax.dot_general(a, b, (((1,), (1,)), ((), ())), preferred_element_type=F32, precision=precision)


def _dot_tn(a, b, precision=None):
    return lax.dot_general(a, b, (((0,), (0,)), ((), ())), preferred_element_type=F32, precision=precision)


def _split(x):
    hi = x.astype(BF16)
    return hi, (x - hi.astype(F32)).astype(BF16)


def _split3(x):
    hi = x.astype(BF16)
    r1 = x - hi.astype(F32)
    mid = r1.astype(BF16)
    return hi, mid, (r1 - mid.astype(F32)).astype(BF16)


def _mm(a, b, mode, dot=_dot):
    if mode == "hi":
        return dot(a, b, HI)
    if mode == "b1":
        return dot(a.astype(BF16), b.astype(BF16))
    ah, al = _split(a)
    bh, bl = _split(b)
    return dot(ah, bh) + (dot(ah, bl) + dot(al, bh))


def _dot_exact_lhs(a_bf16, x):
    hi, mid, lo = _split3(x)
    return _dot(a_bf16, hi) + (_dot(a_bf16, mid) + _dot(a_bf16, lo))


def _dot_exact_rhs(x, b_bf16):
    hi, mid, lo = _split3(x)
    return _dot(hi, b_bf16) + (_dot(mid, b_bf16) + _dot(lo, b_bf16))


def _sigmoid(x):
    return 1.0 / (1.0 + jnp.exp(-x))


def _softplus(x):
    return jnp.maximum(x, 0.0) + jnp.log1p(jnp.exp(-jnp.abs(x)))


def _block_diag_ones(width):
    h = jnp.arange(width) // HEAD_DIM
    return (h[:, None] == h[None, :]).astype(F32)


CAST_ROWS = 128


def _cast_rows(src_ref, dst_ref):
    n_src, n_dst = src_ref.shape[1], dst_ref.shape[1]
    whole = n_src // LANE * LANE

    def step(r, carry):
        rows = pl.ds(pl.multiple_of(r * CAST_ROWS, CAST_ROWS), CAST_ROWS)
        dst_ref[rows, :whole] = src_ref[rows, :whole].astype(BF16)
        if n_dst > whole:
            tail = [src_ref[rows, whole:]] if n_src > whole else []
            tail.append(jnp.zeros((CAST_ROWS, n_dst - n_src), F32))
            dst_ref[rows, whole:] = jnp.concatenate(tail, axis=1).astype(BF16)
        return carry
    lax.fori_loop(0, src_ref.shape[0] // CAST_ROWS, step, 0)


def _proj_body(x_ref, w_ref, pg_ref, pr_ref, pm_ref, wb_ref, *, ng, nr):
    @pl.when(pl.program_id(0) == 0)
    def _():
        _cast_rows(w_ref, wb_ref)

    xb = x_ref[...].astype(BF16)
    pg_ref[...] = _dot(xb, wb_ref[:, :ng])
    pr_ref[...] = _dot(xb, wb_ref[:, ng:ng + nr])
    pm_ref[...] = _dot(xb, wb_ref[:, ng + nr:])


def _proj(x, w_in, layer, ng, nr, tm=256):
    t, d = x.shape
    p_in = w_in.shape[2]
    p_pad = -(-p_in // LANE) * LANE
    nm = p_pad - ng - nr
    row = lambda n: pl.BlockSpec((tm, n), lambda i: (i, 0))
    return pl.pallas_call(
        functools.partial(_proj_body, ng=ng, nr=nr),
        grid=(t // tm,),
        in_specs=[row(d), pl.BlockSpec((None, d, p_in), lambda i: (layer, 0, 0), pipeline_mode=pl.Buffered(1))],
        out_specs=[row(ng), row(nr), row(nm)],
        out_shape=[jax.ShapeDtypeStruct((t, n), F32) for n in (ng, nr, nm)],
        scratch_shapes=[pltpu.VMEM((d, p_pad), BF16)],
        compiler_params=_cparams(1),
        name="in_proj",
    )(x, w_in)


def _gmlp_body(pg_ref, lng_ref, lnb_ref, ws_ref, bst_ref, o_ref, *, gw, chunks):
    p = pg_ref[...]
    p = 0.5 * p * (1.0 + lax.erf(p * math.sqrt(0.5)))
    u, v = p[:, :gw], p[:, gw:]
    mu = jnp.mean(v, axis=-1, keepdims=True)
    vc = v - mu
    var = jnp.mean(vc * vc, axis=-1, keepdims=True)
    vn = vc * lax.rsqrt(var + LN_EPS) * lng_ref[...] + lnb_ref[...]
    n_heads = gw // HEAD_DIM
    for c in range(chunks):
        rows = slice(c * GMLP_CHUNK, (c + 1) * GMLP_CHUNK)
        ys = []
        for h in range(n_heads):
            cols = slice(h * HEAD_DIM, (h + 1) * HEAD_DIM)
            y = _dot(ws_ref[h], vn[rows, cols].astype(BF16)) + bst_ref[:, h:h + 1]
            ys.append(y)
        o_ref[rows, :] = u[rows, :] * jnp.concatenate(ys, axis=1)


def _gmlp(pg, ln_g, ln_b, ws, bs, chunks=4):
    t = pg.shape[0]
    gw = pg.shape[1] // 2
    n_heads = gw // HEAD_DIM
    tm = chunks * GMLP_CHUNK
    bst = jnp.zeros((GMLP_CHUNK, LANE), F32).at[:, :n_heads].set(bs.T)
    return pl.pallas_call(
        functools.partial(_gmlp_body, gw=gw, chunks=chunks),
        grid=(t // tm,),
        in_specs=[pl.BlockSpec((tm, 2 * gw), lambda i: (i, 0)), _full((1, gw)), _full((1, gw)),
                  _full((n_heads, GMLP_CHUNK, GMLP_CHUNK)), _full((GMLP_CHUNK, LANE))],
        out_specs=pl.BlockSpec((tm, gw), lambda i: (i, 0)),
        out_shape=jax.ShapeDtypeStruct((t, gw), F32),
        compiler_params=_cparams(1),
        name="gmlp",
    )(pg, ln_g.reshape(1, gw), ln_b.reshape(1, gw), ws.astype(BF16), bst)


def _halo_specs(tm, width, n_rows):
    per8 = tm // SUBLANE
    last = n_rows // SUBLANE - 1
    prev = pl.BlockSpec((SUBLANE, width), lambda i: (jnp.maximum(i * per8 - 1, 0), 0))
    nxt = pl.BlockSpec((SUBLANE, width), lambda i: (jnp.minimum((i + 1) * per8, last), 0))
    return prev, nxt


def _neighbours(cur, prev_blk, next_blk, tiles_per_seq):
    tm = cur.shape[0]
    j = pl.program_id(0) % tiles_per_seq
    prev_row = jnp.where(j > 0, prev_blk[SUBLANE - 1:SUBLANE, :], 0.0)
    next_row = jnp.where(j < tiles_per_seq - 1, next_blk[0:1, :], 0.0)
    ridx = lax.broadcasted_iota(jnp.int32, cur.shape, 0)
    before = jnp.where(ridx == 0, prev_row, pltpu.roll(cur, 1, 0))
    after = jnp.where(ridx == tm - 1, next_row, pltpu.roll(cur, tm - 1, 0))
    return before, after


def _rwkv_prep_body(pr_ref, prev_ref, next_ref, mu_ref, w0_ref, w2_ref, a0_ref, a2_ref, g2_ref,
                    kk_ref, ka_ref, rk_ref, bd_ref,
                    r_out, v_out, a_out, kd_out, b_out, lw_out, bonus_out, gate_out, *, rw, tiles_per_seq):
    pf = pr_ref[...]
    before, after = _neighbours(pf, prev_ref[...], next_ref[...], tiles_per_seq)
    pf = pf + mu_ref[0:1, :] * (before - pf) + mu_ref[1:2, :] * (after - pf)
    o3 = 3 * rw
    r, k, v = pf[:, :rw], pf[:, rw:2 * rw], pf[:, 2 * rw:o3]
    wd = pf[:, o3:o3 + W_LORA]
    ad = pf[:, o3 + W_LORA:o3 + W_LORA + A_LORA]
    gd = pf[:, o3 + W_LORA + A_LORA:]
    bd = bd_ref[...]
    kk = k * kk_ref[...]
    ss = _dot_exact_rhs(kk * kk, bd)
    kk = kk / jnp.maximum(jnp.sqrt(ss), 1e-12)
    twd = jnp.tanh(wd)
    ksum = jnp.zeros_like(k)
    for d in range(2):
        w_log = -_softplus(-(w0_ref[d:d + 1, :] + _mm(twd, w2_ref[d], "b3"))) - 0.5
        lw_out[d] = -jnp.exp(w_log)
        iclr = _sigmoid(a0_ref[d:d + 1, :] + _mm(ad, a2_ref[d], "b3"))
        kd = k * (1.0 + (iclr - 1.0) * ka_ref[...])
        kd_out[d] = kd
        b_out[d] = kk * iclr
        ksum = ksum + kd
    r_out[...] = r
    v_out[...] = v
    a_out[...] = -kk
    bonus_out[...] = _dot_exact_rhs(r * ksum * rk_ref[...], bd) * v
    gate_out[...] = _dot(_sigmoid(gd).astype(BF16), g2_ref[...])


def _rwkv_prep(pr, seq, mu, w0, w2, a0, a2, g2, k_k, k_a, r_k, tm=256):
    t, rproj = pr.shape
    rw = w0.shape[1]
    tiles_per_seq = seq // tm
    prev, nxt = _halo_specs(tm, rproj, t)
    row = pl.BlockSpec((tm, rw), lambda i: (i, 0))
    row2 = pl.BlockSpec((2, tm, rw), lambda i: (0, i, 0))
    one = jax.ShapeDtypeStruct((t, rw), F32)
    two = jax.ShapeDtypeStruct((2, t, rw), F32)
    return pl.pallas_call(
        functools.partial(_rwkv_prep_body, rw=rw, tiles_per_seq=tiles_per_seq),
        grid=(t // tm,),
        in_specs=[pl.BlockSpec((tm, rproj), lambda i: (i, 0)), prev, nxt,
                  _full((2, rproj)), _full((2, rw)), _full((2, W_LORA, rw)), _full((2, rw)),
                  _full((2, A_LORA, rw)), _full((G_LORA, rw)), _full((1, rw)), _full((1, rw)),
                  _full((1, rw)), _full((rw, rw))],
        out_specs=[row, row, row, row2, row2, row2, row, row],
        out_shape=[one, one, one, two, two, two, one, one],
        compiler_params=_cparams(1),
        name="rwkv_prep",
    )(pr, pr, pr, mu, w0, w2, a0, a2, g2.astype(BF16), k_k.reshape(1, rw), k_a.reshape(1, rw),
      r_k.reshape(1, rw), _block_diag_ones(rw).astype(BF16))


P_G, P_INV, P_APPLY, P_STATE, P_SEQ = "b1", "b1", "b1", "b1", "b3"


def _rwkv_intra_body(r_ref, v_ref, a_ref, kd_ref, b_ref, lw_ref, rq_out, o0_out, mtx_out, hc_out,
                     *, n_heads, chunks):
    L = RWKV_CHUNK
    d = pl.program_id(0)
    row = lax.broadcasted_iota(jnp.int32, (L, L), 0)
    col = lax.broadcasted_iota(jnp.int32, (L, L), 1)
    fwd = d == 0
    rel = (col - row) * (1 - 2 * d)
    incl = rel <= 0
    strict = rel < 0
    eye = (row == col).astype(F32)
    tri = incl.astype(BF16)
    pairs = []
    for c in range(chunks):
        rows = slice(c * L, (c + 1) * L)
        lw = lw_ref[rows, :]
        cum = _dot_exact_lhs(tri, lw)
        tot = jnp.where(fwd, cum[L - 1:L, :], cum[0:1, :])
        e_neg = jnp.exp(-cum)
        e_end = jnp.exp(tot - cum)
        e_tot = jnp.exp(tot)
        r, v, a, kd, b = r_ref[rows, :], v_ref[rows, :], a_ref[rows, :], kd_ref[rows, :], b_ref[rows, :]
        at, rt, bt, kt = a * jnp.exp(cum - lw), r * jnp.exp(cum), b * e_neg, kd * e_neg
        kend, bend = kd * e_end, b * e_end
        for h in range(n_heads):
            sl = slice(h * HEAD_DIM, (h + 1) * HEAD_DIM)
            pairs.append(dict(at=at[:, sl], rt=rt[:, sl], bt=bt[:, sl], kt=kt[:, sl], v=v[:, sl],
                              kend=kend[:, sl], bend=bend[:, sl], e_tot=e_tot[:, sl]))
    for p in pairs:
        p["g"] = _mm(jnp.concatenate([p["at"], p["rt"]], axis=0),
                     jnp.concatenate([p["bt"], p["kt"]], axis=0), P_G, _dot_nt)
    for p in pairs:
        g = p.pop("g")
        p["pw"] = jnp.where(strict, g[:L, :L], 0.0)
        p["a_ak"] = jnp.where(strict, g[:L, L:], 0.0)
        p["m_rb"] = jnp.where(incl, g[L:, :L], 0.0)
        p["m_rk"] = jnp.where(incl, g[L:, L:], 0.0)
        p["inv"] = eye + p["pw"]
    for _ in range(int(math.log2(L)) - 1):
        for p in pairs:
            p["pw"] = _mm(p["pw"], p["pw"], P_INV)
        for p in pairs:
            p["inv"] = p["inv"] + _mm(p["inv"], p["pw"], P_INV)
    for p in pairs:
        p["akv"] = _mm(p["a_ak"], p["v"], P_APPLY)
    for p in pairs:
        p["wu"] = _mm(p["inv"], jnp.concatenate([p["at"], p["akv"]], axis=1), P_APPLY)
    for p in pairs:
        p["mwu"] = _mm(p["m_rb"], p["wu"], P_APPLY)
    for p in pairs:
        p["o0"] = p["mwu"][:, HEAD_DIM:] + _mm(p["m_rk"], p["v"], P_APPLY)
    for p in pairs:
        p["bw"] = _mm(p["bend"], p["wu"], P_STATE, _dot_tn)
    for p in pairs:
        p["hc"] = _mm(p["kend"], p["v"], P_STATE, _dot_tn) + p["bw"][:, HEAD_DIM:]
    for c in range(chunks):
        ps = pairs[c * n_heads:(c + 1) * n_heads]
        rows = slice(c * L, (c + 1) * L)
        krows = slice(c * HEAD_DIM, (c + 1) * HEAD_DIM)
        rq_out[rows, :] = jnp.concatenate([p["rt"] + p["mwu"][:, :HEAD_DIM] for p in ps], axis=1)
        o0_out[rows, :] = jnp.concatenate([p["o0"] for p in ps], axis=1)
        mtx_out[krows, :] = jnp.concatenate([eye * p["e_tot"] + p["bw"][:, :HEAD_DIM] for p in ps], axis=1)
        hc_out[krows, :] = jnp.concatenate([p["hc"] for p in ps], axis=1)


def _rwkv_intra(r, v, a, kd, b, lw, chunks=4):
    t, rw = r.shape
    n_heads = rw // HEAD_DIM
    tm = chunks * RWKV_CHUNK
    tk = chunks * HEAD_DIM
    n_tiles = t // tm
    one = pl.BlockSpec((tm, rw), lambda d, i: (i, 0))
    two = pl.BlockSpec((None, tm, rw), lambda d, i: (d, i, 0))
    twok = pl.BlockSpec((None, tk, rw), lambda d, i: (d, i, 0))
    return pl.pallas_call(
        functools.partial(_rwkv_intra_body, n_heads=n_heads, chunks=chunks),
        grid=(2, n_tiles),
        in_specs=[one, one, one, two, two, two],
        out_specs=[two, two, twok, twok],
        out_shape=[jax.ShapeDtypeStruct((2, t, rw), F32), jax.ShapeDtypeStruct((2, t, rw), F32),
                   jax.ShapeDtypeStruct((2, n_tiles * tk, rw), F32),
                   jax.ShapeDtypeStruct((2, n_tiles * tk, rw), F32)],
        compiler_params=_cparams(2),
        name="rwkv_intra",
    )(r, v, a, kd, b, lw)


def _rwkv_seq_body(rq0, o00, mtx0, hc0, rq1, o01, mtx1, hc1, out0, out1, h_ref, *, n_heads, batch):
    c = pl.program_id(0)

    @pl.when(c == 0)
    def _():
        h_ref[...] = jnp.zeros_like(h_ref)

    L = RWKV_CHUNK
    for d, (rq, o0, mtx, hc, out) in enumerate(((rq0, o00, mtx0, hc0, out0), (rq1, o01, mtx1, hc1, out1))):
        for bi in range(batch):
            rq_t, mtx_t = rq[bi], mtx[bi]
            state = h_ref[d, bi]
            outs, states = [], []
            for h in range(n_heads):
                sl = slice(h * HEAD_DIM, (h + 1) * HEAD_DIM)
                prod = _mm(jnp.concatenate([rq_t[:, sl], mtx_t[:, sl]], axis=0), state[:, sl], P_SEQ)
                outs.append(prod[:L])
                states.append(prod[L:])
            out[bi] = jnp.concatenate(outs, axis=1) + o0[bi]
            h_ref[d, bi] = jnp.concatenate(states, axis=1) + hc[bi]


def _rwkv_seq(rq, o0, mtx, hc, batch, seq):
    _, t, rw = rq.shape
    n_heads = rw // HEAD_DIM
    L = RWKV_CHUNK
    nc = seq // L
    as4 = lambda x: x.reshape(2, batch, x.shape[1] // batch, rw)
    rq, o0, mtx, hc = as4(rq), as4(o0), as4(mtx), as4(hc)
    fwd = lambda rows: pl.BlockSpec((None, batch, rows, rw), lambda c: (0, 0, c, 0))
    bwd = lambda rows: pl.BlockSpec((None, batch, rows, rw), lambda c: (1, 0, nc - 1 - c, 0))
    out0, out1 = pl.pallas_call(
        functools.partial(_rwkv_seq_body, n_heads=n_heads, batch=batch),
        grid=(nc,),
        in_specs=[fwd(L), fwd(L), fwd(HEAD_DIM), fwd(HEAD_DIM), bwd(L), bwd(L), bwd(HEAD_DIM), bwd(HEAD_DIM)],
        out_specs=[pl.BlockSpec((batch, L, rw), lambda c: (0, c, 0)),
                   pl.BlockSpec((batch, L, rw), lambda c: (0, nc - 1 - c, 0))],
        out_shape=[jax.ShapeDtypeStruct((batch, seq, rw), F32)] * 2,
        scratch_shapes=[pltpu.VMEM((2, batch, HEAD_DIM, rw), F32)],
        compiler_params=_cparams(1),
        name="rwkv_seq",
    )(rq, o0, mtx, hc, rq, o0, mtx, hc)
    return out0.reshape(t, rw), out1.reshape(t, rw)


def _rwkv_scan(r, v, a, kd, b, lw, batch, seq):
    rq, o0, mtx, hc = _rwkv_intra(r, v, a, kd, b, lw)
    return _rwkv_seq(rq, o0, mtx, hc, batch, seq)


def _mlstm_prep_body(qk_ref, prev_ref, next_ref, g_ref, cw_ref, cb_ref, gb_ref, q_out, k_out, gate_out,
                     *, mw, n_heads, tiles_per_seq):
    x = qk_ref[...]
    before, after = _neighbours(x, prev_ref[...], next_ref[...], tiles_per_seq)
    y = cb_ref[...] + before * cw_ref[0:1, :] + x * cw_ref[1:2, :] + after * cw_ref[2:3, :]
    y = y * _sigmoid(y)
    q_out[...] = y[:, :mw]
    k_out[...] = y[:, mw:] * (HEAD_DIM ** -0.5)
    g = g_ref[...] + gb_ref[...]
    lane = lax.broadcasted_iota(jnp.int32, g.shape, 1)
    for d in range(2):
        ig = g if d == 0 else pltpu.roll(g, LANE - n_heads, 1)
        fg = pltpu.roll(g, LANE - (1 + d) * n_heads, 1)
        lf = -_softplus(-fg)
        gate_out[d] = jnp.where(lane < n_heads, ig, jnp.where(lane < 2 * n_heads, lf, 0.0))


def _mlstm_prep(pm, seq, conv_w, conv_b, gate_b, mw, tm=256):
    t = pm.shape[0]
    n_heads = mw // HEAD_DIM
    tiles_per_seq = seq // tm
    w2 = 2 * mw
    prev, nxt = _halo_specs(tm, w2, t)
    gcol = (4 * mw) // LANE
    gb = jnp.zeros((1, LANE), F32).at[0, :4 * n_heads].set(gate_b)
    row = pl.BlockSpec((tm, mw), lambda i: (i, 0))
    return pl.pallas_call(
        functools.partial(_mlstm_prep_body, mw=mw, n_heads=n_heads, tiles_per_seq=tiles_per_seq),
        grid=(t // tm,),
        in_specs=[pl.BlockSpec((tm, w2), lambda i: (i, 0)), prev, nxt,
                  pl.BlockSpec((tm, LANE), lambda i: (i, gcol)),
                  _full((3, w2)), _full((1, w2)), _full((1, LANE))],
        out_specs=[row, row, pl.BlockSpec((2, tm, LANE), lambda i: (0, i, 0))],
        out_shape=[jax.ShapeDtypeStruct((t, mw), F32), jax.ShapeDtypeStruct((t, mw), F32),
                   jax.ShapeDtypeStruct((2, t, LANE), F32)],
        compiler_params=_cparams(1),
        name="mlstm_prep",
    )(pm, pm, pm, pm, conv_w, conv_b.reshape(1, w2), gb)


def _mlstm_scan_body(q0_ref, k0_ref, v0_ref, g0_ref, q1_ref, k1_ref, v1_ref, g1_ref, o0_ref, o1_ref,
                     c_ref, m_ref, *, n_heads):
    L = MLSTM_CHUNK

    @pl.when(pl.program_id(1) == 0)
    def _():
        c_ref[...] = jnp.zeros_like(c_ref)
        m_ref[...] = jnp.zeros_like(m_ref)

    row = lax.broadcasted_iota(jnp.int32, (L, L), 0)
    col = lax.broadcasted_iota(jnp.int32, (L, L), 1)
    lane64 = lax.broadcasted_iota(jnp.int32, (L, HEAD_DIM), 1)
    ones_col = (lane64 == 0).astype(F32)
    hs = []
    for d, (q_ref, k_ref, v_ref, g_ref) in enumerate(((q0_ref, k0_ref, v0_ref, g0_ref),
                                                      (q1_ref, k1_ref, v1_ref, g1_ref))):
        incl = (col <= row) if d == 0 else (col >= row)
        last = L - 1 if d == 0 else 0
        g = g_ref[...]
        bcum = _dot_exact_lhs(incl.astype(BF16), g)
        g_t = g.T
        bcum_t = bcum.T
        q, k, v = q_ref[...], k_ref[...], v_ref[...]
        for h in range(n_heads):
            sl = slice(h * HEAD_DIM, (h + 1) * HEAD_DIM)
            bc = bcum[:, n_heads + h:n_heads + h + 1]
            br = bcum_t[n_heads + h:n_heads + h + 1, :]
            igr = g_t[h:h + 1, :]
            igc = g[:, h:h + 1]
            m_st = m_ref[d, h:h + 1, 0:1]
            dm = jnp.where(incl, bc - br + igr, -jnp.inf)
            inter = bc + m_st
            m_t = jnp.maximum(jnp.max(dm, axis=1, keepdims=True), inter)
            b_last = bc[last:last + 1, :]
            lwc = b_last - bc + igc
            m_new = jnp.maximum(b_last + m_st, jnp.max(lwc, axis=0, keepdims=True))
            hs.append(dict(qh=q[:, sl].astype(BF16), kh=k[:, sl],
                           vext=jnp.concatenate([v[:, sl], ones_col], axis=1).astype(BF16),
                           decay=jnp.exp(dm - m_t), w_inter=jnp.exp(inter - m_t), floor=jnp.exp(-m_t),
                           wts=jnp.exp(lwc - m_new), dec=jnp.exp(b_last + m_st - m_new), m_new=m_new,
                           cst=c_ref[d, h]))
    for p in hs:
        p["sc"] = (_dot_nt(p["qh"], p["kh"].astype(BF16)) * p["decay"]).astype(BF16)
    for p in hs:
        p["numext"] = _dot(p["sc"], p["vext"]) + p["w_inter"] * _dot(p["qh"], p["cst"].astype(BF16))
    for p in hs:
        p["upd"] = _dot_tn((p["wts"] * p["kh"]).astype(BF16), p["vext"])
    for d, o_ref in enumerate((o0_ref, o1_ref)):
        outs = []
        for h in range(n_heads):
            p = hs[d * n_heads + h]
            num = p["numext"][:, :HEAD_DIM]
            den = p["numext"][:, HEAD_DIM:HEAD_DIM + 1]
            outs.append(num / jnp.maximum(jnp.abs(den), p["floor"]))
            c_ref[d, h] = p["dec"] * p["cst"] + p["upd"]
            m_ref[d, h:h + 1, :] = jnp.broadcast_to(p["m_new"], (1, LANE))
        o_ref[...] = jnp.concatenate(outs, axis=1)


def _mlstm_scan(q, k, pm, gates, batch, seq):
    t, mw = q.shape
    n_heads = mw // HEAD_DIM
    L = MLSTM_CHUNK
    nc = seq // L
    fwd = lambda bi, c: bi * nc + c
    bwd = lambda bi, c: bi * nc + nc - 1 - c
    specs = []
    for d, blk in enumerate((fwd, bwd)):
        specs += [pl.BlockSpec((L, mw), lambda bi, c, blk=blk: (blk(bi, c), 0)),
                  pl.BlockSpec((L, mw), lambda bi, c, blk=blk: (blk(bi, c), 0)),
                  pl.BlockSpec((L, mw), lambda bi, c, blk=blk: (blk(bi, c), 2)),
                  pl.BlockSpec((None, L, LANE), lambda bi, c, blk=blk, d=d: (d, blk(bi, c), 0))]
    return pl.pallas_call(
        functools.partial(_mlstm_scan_body, n_heads=n_heads),
        grid=(batch, nc),
        in_specs=specs,
        out_specs=[pl.BlockSpec((L, mw), lambda bi, c: (fwd(bi, c), 0)),
                   pl.BlockSpec((L, mw), lambda bi, c: (bwd(bi, c), 0))],
        out_shape=[jax.ShapeDtypeStruct((t, mw), F32)] * 2,
        scratch_shapes=[pltpu.VMEM((2, n_heads, HEAD_DIM, LANE), F32), pltpu.VMEM((2, SUBLANE, LANE), F32)],
        compiler_params=_cparams(2),
        name="mlstm_scan",
    )(q, k, pm, gates, q, k, pm, gates)


def _layer_norm(x, g, b):
    mu = jnp.mean(x, axis=-1, keepdims=True)
    xc = x - mu
    var = jnp.mean(xc * xc, axis=-1, keepdims=True)
    return xc * lax.rsqrt(var + LN_EPS) * g + b


def _head_norm(x, bd_mean, eps):
    mu = _dot_exact_rhs(x, bd_mean)
    xc = x - mu
    var = _dot_exact_rhs(xc * xc, bd_mean)
    return xc * lax.rsqrt(var + eps)


def _mix_out_body(x_ref, yg_ref, ro0_ref, ro1_ref, bonus_ref, rgate_ref, rlg_ref, rlb_ref, mh0_ref, mh1_ref, og_ref,
                  mlg_ref, w_ref, l1g_ref, l1b_ref, rw_ref, rb_ref, bdm_ref,
                  x1_out, topi_out, gate_out, wb_ref, *, alpha, gw, rw):
    @pl.when(pl.program_id(0) == 0)
    def _():
        _cast_rows(w_ref, wb_ref)

    bdm = bdm_ref[...]
    yr = _head_norm(ro0_ref[...] + ro1_ref[...], bdm, RWKV_GN_EPS) * rlg_ref[...] + rlb_ref[...]
    yr = (yr + bonus_ref[...]) * rgate_ref[...]
    ym = _sigmoid(og_ref[...]) * (_head_norm(mh0_ref[...] + mh1_ref[...], bdm, LN_EPS) * mlg_ref[...])
    mix = (_dot(yg_ref[...].astype(BF16), wb_ref[:gw, :]) + _dot(yr.astype(BF16), wb_ref[gw:gw + rw, :])
           + _dot(ym.astype(BF16), wb_ref[gw + rw:, :]))
    x1 = _layer_norm(alpha * x_ref[...] + mix, l1g_ref[...], l1b_ref[...])
    x1_out[...] = x1
    lg = _mm(x1, rw_ref[...], "b3") + rb_ref[...]
    lane = lax.broadcasted_iota(jnp.int32, lg.shape, 1)
    vals, topi = [], jnp.zeros(lg.shape, jnp.int32)
    for j in range(TOP_K):
        mx = jnp.max(lg, axis=1, keepdims=True)
        idx = jnp.min(jnp.where(lg == mx, lane, LANE), axis=1, keepdims=True)
        vals.append(mx)
        topi = jnp.where(lane == j, idx, topi)
        lg = jnp.where(lane == idx, -jnp.inf, lg)
    es = [jnp.exp(vj - vals[0]) for vj in vals]
    den = es[0] + es[1] + es[2] + es[3]
    gate = jnp.zeros(lg.shape, F32)
    for j in range(TOP_K):
        gate = jnp.where(lane == j, es[j] / den, gate)
    topi_out[...] = topi.T[:SUBLANE, :]
    gate_out[...] = gate


def _mix_out(x, yg, ro, bonus, rgate, rlg, rlb, mh, pm, mlg, w_out, layer, l1g, l1b, router_w, router_b, alpha,
             tm=256):
    t, dm = x.shape
    gw, rw, mw = yg.shape[1], bonus.shape[1], mh[0].shape[1]
    assert rw == mw
    rwp = jnp.zeros((dm, LANE), F32).at[:, :N_EXPERTS].set(router_w)
    rbp = jnp.full((1, LANE), NEG_BIG, F32).at[0, :N_EXPERTS].set(router_b)
    row = lambda n: pl.BlockSpec((tm, n), lambda i: (i, 0))
    vec = lambda n: _full((1, n))
    return pl.pallas_call(
        functools.partial(_mix_out_body, alpha=alpha, gw=gw, rw=rw),
        grid=(t // tm,),
        in_specs=[row(dm), row(gw), row(rw), row(rw), row(rw), row(rw), vec(rw), vec(rw), row(mw), row(mw),
                  pl.BlockSpec((tm, mw), lambda i: (i, 3)),
                  vec(mw),
                  pl.BlockSpec((None, dm, dm), lambda i: (layer, 0, 0), pipeline_mode=pl.Buffered(1)),
                  vec(dm), vec(dm), _full((dm, LANE)), vec(LANE), _full((rw, rw))],
        out_specs=[row(dm), pl.BlockSpec((SUBLANE, tm), lambda i: (0, i)), row(LANE)],
        out_shape=[jax.ShapeDtypeStruct((t, dm), F32),
                   jax.ShapeDtypeStruct((SUBLANE, t), jnp.int32), jax.ShapeDtypeStruct((t, LANE), F32)],
        scratch_shapes=[pltpu.VMEM((dm, dm), BF16)],
        compiler_params=_cparams(1),
        name="mix_out",
    )(x, yg, ro[0], ro[1], bonus, rgate, rlg.reshape(1, rw), rlb.reshape(1, rw), mh[0], mh[1], pm,
      mlg.reshape(1, mw), w_out, l1g.reshape(1, dm), l1b.reshape(1, dm), rwp, rbp,
      (_block_diag_ones(rw) / HEAD_DIM).astype(BF16))


def _moe_body(be_ref, nu_ref, ve_ref, xs_ref, w1_ref, b1_ref, w2_ref, b2_ref, o_ref, w1b_ref, w2b_ref, *, dff):
    i = pl.program_id(0)
    active = i < nu_ref[0]
    new_expert = jnp.logical_or(i == 0, be_ref[i] != be_ref[jnp.maximum(i - 1, 0)])

    @pl.when(jnp.logical_and(active, new_expert))
    def _():
        _cast_rows(w1_ref, w1b_ref)
        _cast_rows(w2_ref, w2b_ref)

    @pl.when(active)
    def _():
        rowid = i * MOE_BLOCK + lax.broadcasted_iota(jnp.int32, (MOE_BLOCK, 1), 0)
        xs = jnp.where(rowid < ve_ref[i], xs_ref[...], 0.0)
        hdn = _dot(xs.astype(BF16), w1b_ref[...]) + b1_ref[...]
        g_ = jnp.minimum(hdn[:, :dff], SWIGLU_LIMIT)
        u_ = jnp.clip(hdn[:, dff:], -SWIGLU_LIMIT, SWIGLU_LIMIT)
        act = (u_ + 1.0) * (g_ * _sigmoid(g_ * SWIGLU_ALPHA))
        o_ref[...] = _dot(act.astype(BF16), w2b_ref[...]) + b2_ref[...]

    @pl.when(jnp.logical_not(active))
    def _():
        o_ref[...] = jnp.zeros_like(o_ref)


def _moe_experts(xs, block_e, n_used, valid_end, w1, b1, w2, b2, layer):
    rows, dm = xs.shape
    nb = rows // MOE_BLOCK
    depth, ne, _, dff2 = w1.shape
    dff = dff2 // 2
    grid_spec = pltpu.PrefetchScalarGridSpec(
        num_scalar_prefetch=3,
        grid=(nb,),
        in_specs=[pl.BlockSpec((MOE_BLOCK, dm), lambda i, be, nu, ve: (i, 0)),
                  pl.BlockSpec((None, None, dm, dff2), lambda i, be, nu, ve: (layer, be[i], 0, 0)),
                  pl.BlockSpec((None, None, 1, dff2), lambda i, be, nu, ve: (layer, be[i], 0, 0)),
                  pl.BlockSpec((None, None, dff, dm), lambda i, be, nu, ve: (layer, be[i], 0, 0)),
                  pl.BlockSpec((None, None, 1, dm), lambda i, be, nu, ve: (layer, be[i], 0, 0))],
        out_specs=pl.BlockSpec((MOE_BLOCK, dm), lambda i, be, nu, ve: (i, 0)),
        scratch_shapes=[pltpu.VMEM((dm, dff2), BF16), pltpu.VMEM((dff, dm), BF16)],
    )
    return pl.pallas_call(
        functools.partial(_moe_body, dff=dff),
        grid_spec=grid_spec,
        out_shape=jax.ShapeDtypeStruct((rows, dm), F32),
        compiler_params=_cparams(1),
        name="moe_experts",
    )(block_e, n_used, valid_end, xs, w1, b1.reshape(depth, ne, 1, dff2), w2, b2.reshape(depth, ne, 1, dm))


N_STREAMS = 2
PLAN_TILE = 512
MOE_BLOCK_SHIFT = 8


def _moe_plan_body(e_ref, dest_ref, meta_ref, rank_ref, *, n_tokens, meta_lanes):
    tiles_per_row = n_tokens // PLAN_TILE
    n_tiles = TOP_K * tiles_per_row
    expert = lax.broadcasted_iota(jnp.int32, (N_EXPERTS, PLAN_TILE), 0)
    r_i = lax.broadcasted_iota(jnp.int32, (PLAN_TILE, PLAN_TILE), 0)
    c_i = lax.broadcasted_iota(jnp.int32, (PLAN_TILE, PLAN_TILE), 1)
    earlier = (r_i < c_i).astype(BF16)

    def tile_hits(it):
        j = it // tiles_per_row
        lanes = pl.ds(pl.multiple_of((it % tiles_per_row) * PLAN_TILE, PLAN_TILE), PLAN_TILE)
        return j, lanes, e_ref[pl.ds(j, 1), lanes] == expert

    def rank_step(it, seen):
        j, lanes, hit = tile_hits(it)
        hitf = hit.astype(F32)
        prior = _dot(hit.astype(BF16), earlier) + seen
        rank_ref[pl.ds(j, 1), lanes] = jnp.sum(hitf * prior, axis=0, keepdims=True)
        return seen + jnp.sum(hitf, axis=1, keepdims=True)

    dest_ref[...] = jnp.zeros_like(dest_ref)
    rank_ref[...] = jnp.zeros_like(rank_ref)
    counts = lax.fori_loop(0, n_tiles, rank_step, jnp.zeros((N_EXPERTS, 1), F32))
    padded = ((counts.astype(jnp.int32) + (MOE_BLOCK - 1)) >> MOE_BLOCK_SHIFT) << MOE_BLOCK_SHIFT
    er = lax.broadcasted_iota(jnp.int32, (N_EXPERTS, N_EXPERTS), 0)
    ec = lax.broadcasted_iota(jnp.int32, (N_EXPERTS, N_EXPERTS), 1)
    seg_end = _dot_exact_lhs((ec <= er).astype(BF16),
                             jnp.broadcast_to(padded.astype(F32), (N_EXPERTS, LANE)))[:, 0:1]
    seg_start = seg_end - padded.astype(F32)

    def dest_step(it, carry):
        j, lanes, hit = tile_hits(it)
        base = jnp.sum(jnp.where(hit, seg_start, 0.0), axis=0, keepdims=True)
        dest_ref[pl.ds(j, 1), lanes] = (rank_ref[pl.ds(j, 1), lanes] + base).astype(jnp.int32)
        return carry

    lax.fori_loop(0, n_tiles, dest_step, 0)
    blk_start = (lax.broadcasted_iota(jnp.int32, (N_EXPERTS, meta_lanes), 1) * MOE_BLOCK).astype(F32)
    blk_expert = jnp.minimum(jnp.sum((seg_end <= blk_start).astype(F32), axis=0, keepdims=True), N_EXPERTS - 1.0)
    mine = lax.broadcasted_iota(jnp.int32, (N_EXPERTS, meta_lanes), 0).astype(F32) == blk_expert
    valid_end = jnp.sum(jnp.where(mine, seg_start + counts, 0.0), axis=0, keepdims=True)
    n_used = jnp.broadcast_to(seg_end[N_EXPERTS - 1:N_EXPERTS, :] * (1.0 / MOE_BLOCK), (1, meta_lanes))
    mrow = lax.broadcasted_iota(jnp.int32, (SUBLANE, meta_lanes), 0)
    meta = jnp.where(mrow == 0, blk_expert, jnp.where(mrow == 1, valid_end, jnp.where(mrow == 2, n_used, 0.0)))
    meta_ref[...] = meta.astype(jnp.int32)


def _moe_plan(e_t, n_tokens, n_blocks):
    meta_lanes = -(-n_blocks // LANE) * LANE
    dest, meta = pl.pallas_call(
        functools.partial(_moe_plan_body, n_tokens=n_tokens, meta_lanes=meta_lanes),
        grid=(1,),
        in_specs=[_full((SUBLANE, n_tokens))],
        out_specs=[_full((SUBLANE, n_tokens)), _full((SUBLANE, meta_lanes))],
        out_shape=[jax.ShapeDtypeStruct((SUBLANE, n_tokens), jnp.int32),
                   jax.ShapeDtypeStruct((SUBLANE, meta_lanes), jnp.int32)],
        scratch_shapes=[pltpu.VMEM((SUBLANE, n_tokens), F32)],
        compiler_params=_cparams(1),
        name="moe_plan",
    )(e_t)
    return dest[:TOP_K], meta[0, :n_blocks], meta[1, :n_blocks], meta[2, :1]


def _combine_body(x1_ref, y0_ref, y1_ref, y2_ref, y3_ref, gate_ref, g_ref, b_ref, o_ref, *, alpha):
    gate = gate_ref[...]
    ffn = gate[:, 0:1] * y0_ref[...]
    for j, y_ref in enumerate((y1_ref, y2_ref, y3_ref), start=1):
        ffn = ffn + gate[:, j:j + 1] * y_ref[...]
    o_ref[...] = _layer_norm(alpha * x1_ref[...] + ffn, g_ref[...], b_ref[...])


def _combine(x1, yg, gate, ln_g, ln_b, alpha, tm=256):
    t, dm = x1.shape
    n_tiles = t // tm
    expert_rows = lambda j: pl.BlockSpec((tm, dm), lambda i: (i + j * n_tiles, 0))
    return pl.pallas_call(
        functools.partial(_combine_body, alpha=alpha),
        grid=(n_tiles,),
        in_specs=[pl.BlockSpec((tm, dm), lambda i: (i, 0))] + [expert_rows(j) for j in range(TOP_K)]
                 + [pl.BlockSpec((tm, LANE), lambda i: (i, 0)), _full((1, dm)), _full((1, dm))],
        out_specs=pl.BlockSpec((tm, dm), lambda i: (i, 0)),
        out_shape=jax.ShapeDtypeStruct((t, dm), F32),
        compiler_params=_cparams(1),
        name="combine_ln",
    )(x1, yg, yg, yg, yg, gate, ln_g.reshape(1, dm), ln_b.reshape(1, dm))


SC_CORES = 2
SC_SUBCORES = 16
SC_WORKERS = SC_CORES * SC_SUBCORES


def _sc_gather_rows(table, idx, window):
    n = idx.shape[0]
    dim = table.shape[1]
    n_steps = n // (SC_WORKERS * window)
    assert n_steps * window * SC_WORKERS == n and n_steps % 2 == 0 and window % SUBLANE == 0 and window <= LANE
    idx3 = idx.reshape(SC_WORKERS, n_steps, window)
    mesh = plsc.VectorSubcoreMesh(core_axis_name="c", subcore_axis_name="s",
                                  num_cores=SC_CORES, num_subcores=SC_SUBCORES)

    def body(table_hbm, idx_hbm, out_hbm, idx_v, rows_v, gsem, wsem):
        wid = lax.axis_index("s") * SC_CORES + lax.axis_index("c")
        pltpu.sync_copy(idx_hbm.at[wid], idx_v)

        def gather(j, buf):
            return pltpu.make_async_copy(table_hbm.at[idx_v.at[j]], rows_v.at[buf], gsem.at[buf])

        def write(j, buf):
            base = pl.multiple_of((wid * n_steps + j) * window, window)
            return pltpu.make_async_copy(rows_v.at[buf], out_hbm.at[pl.ds(base, window)], wsem.at[buf])

        gather(0, 0).start()

        @pl.loop(0, n_steps, step=2)
        def _(j0):
            for buf in range(2):
                j = j0 + buf
                gather(j, buf).wait()

                @pl.when(j >= 1)
                def _():
                    write(j - 1, 1 - buf).wait()

                @pl.when(j + 1 < n_steps)
                def _():
                    gather(j + 1, 1 - buf).start()

                write(j, buf).start()

        write(n_steps - 1, 1).wait()

    return pl.kernel(
        body, out_type=jax.ShapeDtypeStruct((n, dim), table.dtype), mesh=mesh,
        scratch_types=[pltpu.VMEM((n_steps, window), jnp.int32), pltpu.VMEM((2, window, dim), table.dtype),
                       pltpu.SemaphoreType.DMA((2,)), pltpu.SemaphoreType.DMA((2,))],
        name="sc_gather",
    )(table, idx3)


def _sc_scatter_rows(src, dest, n_out, window):
    t, dim = src.shape
    k = dest.shape[0]
    n_steps = t // (SC_WORKERS * window)
    assert n_steps * window * SC_WORKERS == t and n_steps % 2 == 0 and window % SUBLANE == 0 and window <= LANE
    idx3 = dest.reshape(k, SC_WORKERS, n_steps, window).transpose(1, 2, 0, 3).reshape(SC_WORKERS, n_steps * k, window)
    mesh = plsc.VectorSubcoreMesh(core_axis_name="c", subcore_axis_name="s",
                                  num_cores=SC_CORES, num_subcores=SC_SUBCORES)

    def body(src_hbm, idx_hbm, out_hbm, idx_v, rows_v, rsem, ssem):
        wid = lax.axis_index("s") * SC_CORES + lax.axis_index("c")
        pltpu.sync_copy(idx_hbm.at[wid], idx_v)

        def read(s, buf):
            base = pl.multiple_of((wid * n_steps + s) * window, window)
            return pltpu.make_async_copy(src_hbm.at[pl.ds(base, window)], rows_v.at[buf], rsem.at[buf])

        def scatter(s, j, buf):
            return pltpu.make_async_copy(rows_v.at[buf], out_hbm.at[idx_v.at[s * k + j]], ssem.at[buf])

        read(0, 0).start()

        @pl.loop(0, n_steps, step=2)
        def _(s0):
            for buf in range(2):
                s = s0 + buf
                read(s, buf).wait()

                @pl.when(s >= 1)
                def _():
                    for j in range(k):
                        scatter(s - 1, j, 1 - buf).wait()

                @pl.when(s + 1 < n_steps)
                def _():
                    read(s + 1, 1 - buf).start()

                for j in range(k):
                    scatter(s, j, buf).start()

        for j in range(k):
            scatter(n_steps - 1, j, 1).wait()

    return pl.kernel(
        body, out_type=jax.ShapeDtypeStruct((n_out, dim), src.dtype), mesh=mesh,
        scratch_types=[pltpu.VMEM((n_steps * k, window), jnp.int32), pltpu.VMEM((2, window, dim), src.dtype),
                       pltpu.SemaphoreType.DMA((2,)), pltpu.SemaphoreType.DMA((2,))],
        name="sc_scatter",
    )(src, idx3)


def _pad_cols(w, width):
    return jnp.pad(w, ((0, 0), (0, width - w.shape[1])))


def kernel(x, w_in, gmlp_ln_g, gmlp_ln_b, gmlp_ws, gmlp_bs, rwkv_mu, rwkv_w0, rwkv_w2, rwkv_a0, rwkv_a2, rwkv_g2, rwkv_k_k, rwkv_k_a, rwkv_r_k, rwkv_ln_g, rwkv_ln_b, mlstm_conv_w, mlstm_conv_b, mlstm_gate_b, mlstm_ln_g, w_out, ln1_g, ln1_b, router_w, router_b, exp_w1, exp_b1, exp_w2, exp_b2, ln2_g, ln2_b):
    batch, seq, dm = x.shape
    depth = w_in.shape[0]
    sb = batch // N_STREAMS if batch % N_STREAMS == 0 else batch
    t = sb * seq
    gw = gmlp_ln_g.shape[1]
    rw = rwkv_w0.shape[2]
    mw = mlstm_ln_g.shape[1]
    g_proj = 2 * gw
    r_proj = 3 * rw + W_LORA + A_LORA + G_LORA
    alpha = (2 * depth) ** 0.25
    n_blocks = -(-t * TOP_K // MOE_BLOCK) + N_EXPERTS
    streams = [x[i * sb:(i + 1) * sb].reshape(t, dm) for i in range(batch // sb)]
    for l in range(depth):
        mixed = []
        for xf in streams:
            pg, pr, pm = _proj(xf, w_in, l, g_proj, r_proj)
            y_g = _gmlp(pg, gmlp_ln_g[l], gmlp_ln_b[l], gmlp_ws[l], gmlp_bs[l])
            r, v, a, kd, b, lw, bonus, rgate = _rwkv_prep(
                pr, seq, rwkv_mu[l], rwkv_w0[l], rwkv_w2[l], rwkv_a0[l], rwkv_a2[l], rwkv_g2[l],
                rwkv_k_k[l], rwkv_k_a[l], rwkv_r_k[l].reshape(-1))
            ro = _rwkv_scan(r, v, a, kd, b, lw, sb, seq)
            q, k, gates = _mlstm_prep(pm, seq, mlstm_conv_w[l], mlstm_conv_b[l], mlstm_gate_b[l], mw)
            mh = _mlstm_scan(q, k, pm, gates, sb, seq)
            mixed.append(_mix_out(xf, y_g, ro, bonus, rgate, rwkv_ln_g[l], rwkv_ln_b[l], mh, pm, mlstm_ln_g[l],
                                  w_out, l, ln1_g[l], ln1_b[l], router_w[l], router_b[l], alpha))
        streams = []
        for x1, topi, gate in mixed:
            dest, block_e, valid_end, n_used = _moe_plan(topi, t, n_blocks)
            xs = _sc_scatter_rows(x1, dest, n_blocks * MOE_BLOCK, window=32)
            ys = _moe_experts(xs, block_e, n_used, valid_end, exp_w1, exp_b1, exp_w2, exp_b2, l)
            yg = _sc_gather_rows(ys, dest.reshape(-1), window=32)
            streams.append(_combine(x1, yg, gate, ln2_g[l], ln2_b[l], alpha))
    return jnp.concatenate(streams, axis=0).reshape(batch, seq, dm)
```

```python
import functools
import math

import jax
import jax.numpy as jnp
from jax import lax
from jax.experimental import pallas as pl
from jax.experimental.pallas import tpu as pltpu
from jax.experimental.pallas import tpu_sc as plsc

F32 = jnp.float32
BF16 = jnp.bfloat16
HI = lax.Precision.HIGHEST

HEAD_DIM = 64
GMLP_CHUNK = 128
MLSTM_CHUNK = 128
RWKV_CHUNK = 64
W_LORA = 64
A_LORA = 64
G_LORA = 128
N_EXPERTS = 32
TOP_K = 4
MOE_BLOCK = 256
SWIGLU_LIMIT = 7.0
SWIGLU_ALPHA = 1.702
LN_EPS = 1e-5
RWKV_GN_EPS = 64e-5
LANE = 128
SUBLANE = 8
VMEM_LIMIT = 48 * 1024 * 1024
NEG_BIG = -1e30


def _cparams(n_axes):
    return pltpu.CompilerParams(dimension_semantics=("arbitrary",) * n_axes,
                                vmem_limit_bytes=VMEM_LIMIT)


def _full(shape):
    return pl.BlockSpec(shape, lambda *_: (0,) * len(shape))


def _dot(a, b, precision=None):
    return jnp.dot(a, b, preferred_element_type=F32, precision=precision)


def _dot_nt(a, b, precision=None):
    return lax.dot_general(a, b, (((1,), (1,)), ((), ())), preferred_element_type=F32, precision=precision)


def _dot_tn(a, b, precision=None):
    return lax.dot_general(a, b, (((0,), (0,)), ((), ())), preferred_element_type=F32, precision=precision)


def _split(x):
    hi = x.astype(BF16)
    return hi, (x - hi.astype(F32)).astype(BF16)


def _split3(x):
    hi = x.astype(BF16)
    r1 = x - hi.astype(F32)
    mid = r1.astype(BF16)
    return hi, mid, (r1 - mid.astype(F32)).astype(BF16)


def _mm(a, b, mode, dot=_dot):
    if mode == "hi":
        return dot(a, b, HI)
    if mode == "b1":
        return dot(a.astype(BF16), b.astype(BF16))
    ah, al = _split(a)
    bh, bl = _split(b)
    return dot(ah, bh) + (dot(ah, bl) + dot(al, bh))


def _dot_exact_lhs(a_bf16, x):
    hi, mid, lo = _split3(x)
    return _dot(a_bf16, hi) + (_dot(a_bf16, mid) + _dot(a_bf16, lo))


def _dot_exact_rhs(x, b_bf16):
    hi, mid, lo = _split3(x)
    return _dot(hi, b_bf16) + (_dot(mid, b_bf16) + _dot(lo, b_bf16))


def _sigmoid(x):
    return 1.0 / (1.0 + jnp.exp(-x))


def _softplus(x):
    return jnp.maximum(x, 0.0) + jnp.log1p(jnp.exp(-jnp.abs(x)))


def _block_diag_ones(width):
    h = jnp.arange(width) // HEAD_DIM
    return (h[:, None] == h[None, :]).astype(F32)


CAST_ROWS = 128


def _cast_rows(src_ref, dst_ref):
    n_src, n_dst = src_ref.shape[1], dst_ref.shape[1]
    whole = n_src // LANE * LANE

    def step(r, carry):
        rows = pl.ds(pl.multiple_of(r * CAST_ROWS, CAST_ROWS), CAST_ROWS)
        dst_ref[rows, :whole] = src_ref[rows, :whole].astype(BF16)
        if n_dst > whole:
            tail = [src_ref[rows, whole:]] if n_src > whole else []
            tail.append(jnp.zeros((CAST_ROWS, n_dst - n_src), F32))
            dst_ref[rows, whole:] = jnp.concatenate(tail, axis=1).astype(BF16)
        return carry
    lax.fori_loop(0, src_ref.shape[0] // CAST_ROWS, step, 0)


def _proj_body(x_ref, w_ref, pg_ref, pr_ref, pm_ref, wb_ref, *, ng, nr):
    @pl.when(pl.program_id(0) == 0)
    def _():
        _cast_rows(w_ref, wb_ref)

    xb = x_ref[...].astype(BF16)
    pg_ref[...] = _dot(xb, wb_ref[:, :ng])
    pr_ref[...] = _dot(xb, wb_ref[:, ng:ng + nr])
    pm_ref[...] = _dot(xb, wb_ref[:, ng + nr:])


def _proj(x, w_in, layer, ng, nr, tm=256):
    t, d = x.shape
    p_in = w_in.shape[2]
    p_pad = -(-p_in // LANE) * LANE
    nm = p_pad - ng - nr
    row = lambda n: pl.BlockSpec((tm, n), lambda i: (i, 0))
    return pl.pallas_call(
        functools.partial(_proj_body, ng=ng, nr=nr),
        grid=(t // tm,),
        in_specs=[row(d), pl.BlockSpec((None, d, p_in), lambda i: (layer, 0, 0), pipeline_mode=pl.Buffered(1))],
        out_specs=[row(ng), row(nr), row(nm)],
        out_shape=[jax.ShapeDtypeStruct((t, n), F32) for n in (ng, nr, nm)],
        scratch_shapes=[pltpu.VMEM((d, p_pad), BF16)],
        compiler_params=_cparams(1),
        name="in_proj",
    )(x, w_in)


def _gmlp_body(pg_ref, lng_ref, lnb_ref, ws_ref, bst_ref, o_ref, *, gw, chunks):
    p = pg_ref[...]
    p = 0.5 * p * (1.0 + lax.erf(p * math.sqrt(0.5)))
    u, v = p[:, :gw], p[:, gw:]
    mu = jnp.mean(v, axis=-1, keepdims=True)
    vc = v - mu
    var = jnp.mean(vc * vc, axis=-1, keepdims=True)
    vn = vc * lax.rsqrt(var + LN_EPS) * lng_ref[...] + lnb_ref[...]
    n_heads = gw // HEAD_DIM
    for c in range(chunks):
        rows = slice(c * GMLP_CHUNK, (c + 1) * GMLP_CHUNK)
        ys = []
        for h in range(n_heads):
            cols = slice(h * HEAD_DIM, (h + 1) * HEAD_DIM)
            y = _dot(ws_ref[h], vn[rows, cols].astype(BF16)) + bst_ref[:, h:h + 1]
            ys.append(y)
        o_ref[rows, :] = u[rows, :] * jnp.concatenate(ys, axis=1)


def _gmlp(pg, ln_g, ln_b, ws, bs, chunks=4):
    t = pg.shape[0]
    gw = pg.shape[1] // 2
    n_heads = gw // HEAD_DIM
    tm = chunks * GMLP_CHUNK
    bst = jnp.zeros((GMLP_CHUNK, LANE), F32).at[:, :n_heads].set(bs.T)
    return pl.pallas_call(
        functools.partial(_gmlp_body, gw=gw, chunks=chunks),
        grid=(t // tm,),
        in_specs=[pl.BlockSpec((tm, 2 * gw), lambda i: (i, 0)), _full((1, gw)), _full((1, gw)),
                  _full((n_heads, GMLP_CHUNK, GMLP_CHUNK)), _full((GMLP_CHUNK, LANE))],
        out_specs=pl.BlockSpec((tm, gw), lambda i: (i, 0)),
        out_shape=jax.ShapeDtypeStruct((t, gw), F32),
        compiler_params=_cparams(1),
        name="gmlp",
    )(pg, ln_g.reshape(1, gw), ln_b.reshape(1, gw), ws.astype(BF16), bst)


def _halo_specs(tm, width, n_rows):
    per8 = tm // SUBLANE
    last = n_rows // SUBLANE - 1
    prev = pl.BlockSpec((SUBLANE, width), lambda i: (jnp.maximum(i * per8 - 1, 0), 0))
    nxt = pl.BlockSpec((SUBLANE, width), lambda i: (jnp.minimum((i + 1) * per8, last), 0))
    return prev, nxt


def _neighbours(cur, prev_blk, next_blk, tiles_per_seq):
    tm = cur.shape[0]
    j = pl.program_id(0) % tiles_per_seq
    prev_row = jnp.where(j > 0, prev_blk[SUBLANE - 1:SUBLANE, :], 0.0)
    next_row = jnp.where(j < tiles_per_seq - 1, next_blk[0:1, :], 0.0)
    ridx = lax.broadcasted_iota(jnp.int32, cur.shape, 0)
    before = jnp.where(ridx == 0, prev_row, pltpu.roll(cur, 1, 0))
    after = jnp.where(ridx == tm - 1, next_row, pltpu.roll(cur, tm - 1, 0))
    return before, after


def _rwkv_prep_body(pr_ref, prev_ref, next_ref, mu_ref, w0_ref, w2_ref, a0_ref, a2_ref, g2_ref,
                    kk_ref, ka_ref, rk_ref, bd_ref,
                    r_out, v_out, a_out, kd_out, b_out, lw_out, bonus_out, gate_out, *, rw, tiles_per_seq):
    pf = pr_ref[...]
    before, after = _neighbours(pf, prev_ref[...], next_ref[...], tiles_per_seq)
    pf = pf + mu_ref[0:1, :] * (before - pf) + mu_ref[1:2, :] * (after - pf)
    o3 = 3 * rw
    r, k, v = pf[:, :rw], pf[:, rw:2 * rw], pf[:, 2 * rw:o3]
    wd = pf[:, o3:o3 + W_LORA]
    ad = pf[:, o3 + W_LORA:o3 + W_LORA + A_LORA]
    gd = pf[:, o3 + W_LORA + A_LORA:]
    bd = bd_ref[...]
    kk = k * kk_ref[...]
    ss = _dot_exact_rhs(kk * kk, bd)
    kk = kk / jnp.maximum(jnp.sqrt(ss), 1e-12)
    twd = jnp.tanh(wd)
    ksum = jnp.zeros_like(k)
    for d in range(2):
        w_log = -_softplus(-(w0_ref[d:d + 1, :] + _mm(twd, w2_ref[d], "b3"))) - 0.5
        lw_out[d] = -jnp.exp(w_log)
        iclr = _sigmoid(a0_ref[d:d + 1, :] + _mm(ad, a2_ref[d], "b3"))
        kd = k * (1.0 + (iclr - 1.0) * ka_ref[...])
        kd_out[d] = kd
        b_out[d] = kk * iclr
        ksum = ksum + kd
    r_out[...] = r
    v_out[...] = v
    a_out[...] = -kk
    bonus_out[...] = _dot_exact_rhs(r * ksum * rk_ref[...], bd) * v
    gate_out[...] = _dot(_sigmoid(gd).astype(BF16), g2_ref[...])


def _rwkv_prep(pr, seq, mu, w0, w2, a0, a2, g2, k_k, k_a, r_k, tm=256):
    t, rproj = pr.shape
    rw = w0.shape[1]
    tiles_per_seq = seq // tm
    prev, nxt = _halo_specs(tm, rproj, t)
    row = pl.BlockSpec((tm, rw), lambda i: (i, 0))
    row2 = pl.BlockSpec((2, tm, rw), lambda i: (0, i, 0))
    one = jax.ShapeDtypeStruct((t, rw), F32)
    two = jax.ShapeDtypeStruct((2, t, rw), F32)
    return pl.pallas_call(
        functools.partial(_rwkv_prep_body, rw=rw, tiles_per_seq=tiles_per_seq),
        grid=(t // tm,),
        in_specs=[pl.BlockSpec((tm, rproj), lambda i: (i, 0)), prev, nxt,
                  _full((2, rproj)), _full((2, rw)), _full((2, W_LORA, rw)), _full((2, rw)),
                  _full((2, A_LORA, rw)), _full((G_LORA, rw)), _full((1, rw)), _full((1, rw)),
                  _full((1, rw)), _full((rw, rw))],
        out_specs=[row, row, row, row2, row2, row2, row, row],
        out_shape=[one, one, one, two, two, two, one, one],
        compiler_params=_cparams(1),
        name="rwkv_prep",
    )(pr, pr, pr, mu, w0, w2, a0, a2, g2.astype(BF16), k_k.reshape(1, rw), k_a.reshape(1, rw),
      r_k.reshape(1, rw), _block_diag_ones(rw).astype(BF16))


P_G, P_INV, P_APPLY, P_STATE, P_SEQ = "b1", "b1", "b1", "b1", "b3"


def _rwkv_intra_body(r_ref, v_ref, a_ref, kd_ref, b_ref, lw_ref, rq_out, o0_out, mtx_out, hc_out,
                     *, n_heads, chunks):
    L = RWKV_CHUNK
    d = pl.program_id(0)
    row = lax.broadcasted_iota(jnp.int32, (L, L), 0)
    col = lax.broadcasted_iota(jnp.int32, (L, L), 1)
    fwd = d == 0
    rel = (col - row) * (1 - 2 * d)
    incl = rel <= 0
    strict = rel < 0
    eye = (row == col).astype(F32)
    tri = incl.astype(BF16)
    pairs = []
    for c in range(chunks):
        rows = slice(c * L, (c + 1) * L)
        lw = lw_ref[rows, :]
        cum = _dot_exact_lhs(tri, lw)
        tot = jnp.where(fwd, cum[L - 1:L, :], cum[0:1, :])
        e_neg = jnp.exp(-cum)
        e_end = jnp.exp(tot - cum)
        e_tot = jnp.exp(tot)
        r, v, a, kd, b = r_ref[rows, :], v_ref[rows, :], a_ref[rows, :], kd_ref[rows, :], b_ref[rows, :]
        at, rt, bt, kt = a * jnp.exp(cum - lw), r * jnp.exp(cum), b * e_neg, kd * e_neg
        kend, bend = kd * e_end, b * e_end
        for h in range(n_heads):
            sl = slice(h * HEAD_DIM, (h + 1) * HEAD_DIM)
            pairs.append(dict(at=at[:, sl], rt=rt[:, sl], bt=bt[:, sl], kt=kt[:, sl], v=v[:, sl],
                              kend=kend[:, sl], bend=bend[:, sl], e_tot=e_tot[:, sl]))
    for p in pairs:
        p["g"] = _mm(jnp.concatenate([p["at"], p["rt"]], axis=0),
                     jnp.concatenate([p["bt"], p["kt"]], axis=0), P_G, _dot_nt)
    for p in pairs:
        g = p.pop("g")
        p["pw"] = jnp.where(strict, g[:L, :L], 0.0)
        p["a_ak"] = jnp.where(strict, g[:L, L:], 0.0)
        p["m_rb"] = jnp.where(incl, g[L:, :L], 0.0)
        p["m_rk"] = jnp.where(incl, g[L:, L:], 0.0)
        p["inv"] = eye + p["pw"]
    for _ in range(int(math.log2(L)) - 1):
        for p in pairs:
            p["pw"] = _mm(p["pw"], p["pw"], P_INV)
        for p in pairs:
            p["inv"] = p["inv"] + _mm(p["inv"], p["pw"], P_INV)
    for p in pairs:
        p["akv"] = _mm(p["a_ak"], p["v"], P_APPLY)
    for p in pairs:
        p["wu"] = _mm(p["inv"], jnp.concatenate([p["at"], p["akv"]], axis=1), P_APPLY)
    for p in pairs:
        p["mwu"] = _mm(p["m_rb"], p["wu"], P_APPLY)
    for p in pairs:
        p["o0"] = p["mwu"][:, HEAD_DIM:] + _mm(p["m_rk"], p["v"], P_APPLY)
    for p in pairs:
        p["bw"] = _mm(p["bend"], p["wu"], P_STATE, _dot_tn)
    for p in pairs:
        p["hc"] = _mm(p["kend"], p["v"], P_STATE, _dot_tn) + p["bw"][:, HEAD_DIM:]
    for c in range(chunks):
        ps = pairs[c * n_heads:(c + 1) * n_heads]
        rows = slice(c * L, (c + 1) * L)
        krows = slice(c * HEAD_DIM, (c + 1) * HEAD_DIM)
        rq_out[rows, :] = jnp.concatenate([p["rt"] + p["mwu"][:, :HEAD_DIM] for p in ps], axis=1)
        o0_out[rows, :] = jnp.concatenate([p["o0"] for p in ps], axis=1)
        mtx_out[krows, :] = jnp.concatenate([eye * p["e_tot"] + p["bw"][:, :HEAD_DIM] for p in ps], axis=1)
        hc_out[krows, :] = jnp.concatenate([p["hc"] for p in ps], axis=1)


def _rwkv_intra(r, v, a, kd, b, lw, chunks=4):
    t, rw = r.shape
    n_heads = rw // HEAD_DIM
    tm = chunks * RWKV_CHUNK
    tk = chunks * HEAD_DIM
    n_tiles = t // tm
    one = pl.BlockSpec((tm, rw), lambda d, i: (i, 0))
    two = pl.BlockSpec((None, tm, rw), lambda d, i: (d, i, 0))
    twok = pl.BlockSpec((None, tk, rw), lambda d, i: (d, i, 0))
    return pl.pallas_call(
        functools.partial(_rwkv_intra_body, n_heads=n_heads, chunks=chunks),
        grid=(2, n_tiles),
        in_specs=[one, one, one, two, two, two],
        out_specs=[two, two, twok, twok],
        out_shape=[jax.ShapeDtypeStruct((2, t, rw), F32), jax.ShapeDtypeStruct((2, t, rw), F32),
                   jax.ShapeDtypeStruct((2, n_tiles * tk, rw), F32),
                   jax.ShapeDtypeStruct((2, n_tiles * tk, rw), F32)],
        compiler_params=_cparams(2),
        name="rwkv_intra",
    )(r, v, a, kd, b, lw)


def _rwkv_seq_body(rq0, o00, mtx0, hc0, rq1, o01, mtx1, hc1, out0, out1, h_ref, *, n_heads, batch):
    c = pl.program_id(0)

    @pl.when(c == 0)
    def _():
        h_ref[...] = jnp.zeros_like(h_ref)

    L = RWKV_CHUNK
    for d, (rq, o0, mtx, hc, out) in enumerate(((rq0, o00, mtx0, hc0, out0), (rq1, o01, mtx1, hc1, out1))):
        for bi in range(batch):
            rq_t, mtx_t = rq[bi], mtx[bi]
            state = h_ref[d, bi]
            outs, states = [], []
            for h in range(n_heads):
                sl = slice(h * HEAD_DIM, (h + 1) * HEAD_DIM)
                prod = _mm(jnp.concatenate([rq_t[:, sl], mtx_t[:, sl]], axis=0), state[:, sl], P_SEQ)
                outs.append(prod[:L])
                states.append(prod[L:])
            out[bi] = jnp.concatenate(outs, axis=1) + o0[bi]
            h_ref[d, bi] = jnp.concatenate(states, axis=1) + hc[bi]


def _rwkv_seq(rq, o0, mtx, hc, batch, seq):
    _, t, rw = rq.shape
    n_heads = rw // HEAD_DIM
    L = RWKV_CHUNK
    nc = seq // L
    as4 = lambda x: x.reshape(2, batch, x.shape[1] // batch, rw)
    rq, o0, mtx, hc = as4(rq), as4(o0), as4(mtx), as4(hc)
    fwd = lambda rows: pl.BlockSpec((None, batch, rows, rw), lambda c: (0, 0, c, 0))
    bwd = lambda rows: pl.BlockSpec((None, batch, rows, rw), lambda c: (1, 0, nc - 1 - c, 0))
    out0, out1 = pl.pallas_call(
        functools.partial(_rwkv_seq_body, n_heads=n_heads, batch=batch),
        grid=(nc,),
        in_specs=[fwd(L), fwd(L), fwd(HEAD_DIM), fwd(HEAD_DIM), bwd(L), bwd(L), bwd(HEAD_DIM), bwd(HEAD_DIM)],
        out_specs=[pl.BlockSpec((batch, L, rw), lambda c: (0, c, 0)),
                   pl.BlockSpec((batch, L, rw), lambda c: (0, nc - 1 - c, 0))],
        out_shape=[jax.ShapeDtypeStruct((batch, seq, rw), F32)] * 2,
        scratch_shapes=[pltpu.VMEM((2, batch, HEAD_DIM, rw), F32)],
        compiler_params=_cparams(1),
        name="rwkv_seq",
    )(rq, o0, mtx, hc, rq, o0, mtx, hc)
    return out0.reshape(t, rw), out1.reshape(t, rw)


def _rwkv_scan(r, v, a, kd, b, lw, batch, seq):
    rq, o0, mtx, hc = _rwkv_intra(r, v, a, kd, b, lw)
    return _rwkv_seq(rq, o0, mtx, hc, batch, seq)


def _mlstm_prep_body(qk_ref, prev_ref, next_ref, g_ref, cw_ref, cb_ref, gb_ref, q_out, k_out, gate_out,
                     *, mw, n_heads, tiles_per_seq):
    x = qk_ref[...]
    before, after = _neighbours(x, prev_ref[...], next_ref[...], tiles_per_seq)
    y = cb_ref[...] + before * cw_ref[0:1, :] + x * cw_ref[1:2, :] + after * cw_ref[2:3, :]
    y = y * _sigmoid(y)
    q_out[...] = y[:, :mw]
    k_out[...] = y[:, mw:] * (HEAD_DIM ** -0.5)
    g = g_ref[...] + gb_ref[...]
    lane = lax.broadcasted_iota(jnp.int32, g.shape, 1)
    for d in range(2):
        ig = g if d == 0 else pltpu.roll(g, LANE - n_heads, 1)
        fg = pltpu.roll(g, LANE - (1 + d) * n_heads, 1)
        lf = -_softplus(-fg)
        gate_out[d] = jnp.where(lane < n_heads, ig, jnp.where(lane < 2 * n_heads, lf, 0.0))


def _mlstm_prep(pm, seq, conv_w, conv_b, gate_b, mw, tm=256):
    t = pm.shape[0]
    n_heads = mw // HEAD_DIM
    tiles_per_seq = seq // tm
    w2 = 2 * mw
    prev, nxt = _halo_specs(tm, w2, t)
    gcol = (4 * mw) // LANE
    gb = jnp.zeros((1, LANE), F32).at[0, :4 * n_heads].set(gate_b)
    row = pl.BlockSpec((tm, mw), lambda i: (i, 0))
    return pl.pallas_call(
        functools.partial(_mlstm_prep_body, mw=mw, n_heads=n_heads, tiles_per_seq=tiles_per_seq),
        grid=(t // tm,),
        in_specs=[pl.BlockSpec((tm, w2), lambda i: (i, 0)), prev, nxt,
                  pl.BlockSpec((tm, LANE), lambda i: (i, gcol)),
                  _full((3, w2)), _full((1, w2)), _full((1, LANE))],
        out_specs=[row, row, pl.BlockSpec((2, tm, LANE), lambda i: (0, i, 0))],
        out_shape=[jax.ShapeDtypeStruct((t, mw), F32), jax.ShapeDtypeStruct((t, mw), F32),
                   jax.ShapeDtypeStruct((2, t, LANE), F32)],
        compiler_params=_cparams(1),
        name="mlstm_prep",
    )(pm, pm, pm, pm, conv_w, conv_b.reshape(1, w2), gb)


def _mlstm_scan_body(q0_ref, k0_ref, v0_ref, g0_ref, q1_ref, k1_ref, v1_ref, g1_ref, o0_ref, o1_ref,
                     c_ref, m_ref, *, n_heads):
    L = MLSTM_CHUNK
    H = n_heads

    @pl.when(pl.program_id(1) == 0)
    def _():
        c_ref[...] = jnp.zeros_like(c_ref)
        m_ref[...] = jnp.zeros_like(m_ref)

    row = lax.broadcasted_iota(jnp.int32, (L, L), 0)
    col = lax.broadcasted_iota(jnp.int32, (L, L), 1)
    trow = lax.broadcasted_iota(jnp.int32, (L, LANE), 0)
    low = lax.broadcasted_iota(jnp.int32, (L, LANE), 1) < HEAD_DIM
    xr = lax.broadcasted_iota(jnp.int32, (LANE, H * L), 0)
    xc = lax.broadcasted_iota(jnp.int32, (LANE, H * L), 1)
    spread = (xr - H == lax.shift_right_logical(xc, int(math.log2(L)))).astype(BF16)
    hs = []
    for d, (q_ref, k_ref, v_ref, g_ref) in enumerate(((q0_ref, k0_ref, v0_ref, g0_ref),
                                                      (q1_ref, k1_ref, v1_ref, g1_ref))):
        incl = (col <= row) if d == 0 else (col >= row)
        last = L - 1 if d == 0 else 0
        g = g_ref[...]
        bcum = _dot_exact_lhs(incl.astype(BF16), g)
        z = pltpu.roll(g, H, 1) - bcum
        cmax = z
        shift = 1
        while shift < L:
            if d == 0:
                moved = jnp.where(trow >= shift, pltpu.roll(cmax, shift, 0), -jnp.inf)
            else:
                moved = jnp.where(trow < L - shift, pltpu.roll(cmax, L - shift, 0), -jnp.inf)
            cmax = jnp.maximum(cmax, moved)
            shift *= 2
        m_prev = m_ref[d, 0:1, :]
        top = jnp.maximum(cmax, m_prev)
        b_last = bcum[last:last + 1, :]
        lwc = b_last + z
        m_new = jnp.maximum(b_last + m_prev, jnp.max(lwc, axis=0, keepdims=True))
        m_ref[d, 0:1, :] = m_new
        per_row = jnp.concatenate(
            [-top, m_prev - top, bcum + top, jnp.exp(lwc - m_new),
             jnp.broadcast_to(jnp.exp(b_last + m_prev - m_new), (SUBLANE, LANE))], axis=0)
        wide = _dot_exact_rhs(per_row, spread)
        z_t = z.T
        q, k, v = q_ref[...], k_ref[...], v_ref[...]
        for h in range(H):
            slab = slice(h // 2 * LANE, (h // 2 + 1) * LANE)
            cols = slice(h * L, (h + 1) * L)
            mine = low if h % 2 == 0 else jnp.logical_not(low)
            kh = jnp.where(mine, k[:, slab], 0.0)
            hs.append(dict(
                qh=jnp.where(mine, q[:, slab], 0.0).astype(BF16), kh=kh.astype(BF16),
                vext=jnp.where(mine, v[:, slab], 1.0).astype(BF16),
                decay=jnp.exp(jnp.where(incl, wide[0:L, cols] + z_t[H + h:H + h + 1, :], -jnp.inf)),
                w_inter=jnp.exp(wide[L:2 * L, cols]), floor=jnp.exp(-wide[2 * L:3 * L, cols]),
                wk=(wide[3 * L:4 * L, cols] * kh).astype(BF16), dec=wide[4 * L:4 * L + 1, cols],
                cst=c_ref[d, h]))
    for p in hs:
        p["sc"] = (_dot_nt(p["qh"], p["kh"]) * p["decay"]).astype(BF16)
    for p in hs:
        p["numext"] = _dot(p["sc"], p["vext"]) + p["w_inter"] * _dot(p["qh"], p["cst"].astype(BF16))
    for p in hs:
        p["upd"] = _dot_tn(p["wk"], p["vext"])
    for d, o_ref in enumerate((o0_ref, o1_ref)):
        res = []
        for h in range(H):
            p = hs[d * H + h]
            den = pltpu.roll(p["numext"], HEAD_DIM, 1)
            res.append(p["numext"] / jnp.maximum(jnp.abs(den), p["floor"]))
            c_ref[d, h] = p["dec"] * p["cst"] + p["upd"]
        for pair in range(H // 2):
            o_ref[:, pair * LANE:(pair + 1) * LANE] = jnp.where(low, res[2 * pair], res[2 * pair + 1])


def _mlstm_scan(q, k, pm, gates, batch, seq):
    t, mw = q.shape
    n_heads = mw // HEAD_DIM
    L = MLSTM_CHUNK
    nc = seq // L
    fwd = lambda bi, c: bi * nc + c
    bwd = lambda bi, c: bi * nc + nc - 1 - c
    specs = []
    for d, blk in enumerate((fwd, bwd)):
        specs += [pl.BlockSpec((L, mw), lambda bi, c, blk=blk: (blk(bi, c), 0)),
                  pl.BlockSpec((L, mw), lambda bi, c, blk=blk: (blk(bi, c), 0)),
                  pl.BlockSpec((L, mw), lambda bi, c, blk=blk: (blk(bi, c), 2)),
                  pl.BlockSpec((None, L, LANE), lambda bi, c, blk=blk, d=d: (d, blk(bi, c), 0))]
    return pl.pallas_call(
        functools.partial(_mlstm_scan_body, n_heads=n_heads),
        grid=(batch, nc),
        in_specs=specs,
        out_specs=[pl.BlockSpec((L, mw), lambda bi, c: (fwd(bi, c), 0)),
                   pl.BlockSpec((L, mw), lambda bi, c: (bwd(bi, c), 0))],
        out_shape=[jax.ShapeDtypeStruct((t, mw), F32)] * 2,
        scratch_shapes=[pltpu.VMEM((2, n_heads, LANE, LANE), F32), pltpu.VMEM((2, SUBLANE, LANE), F32)],
        compiler_params=_cparams(2),
        name="mlstm_scan",
    )(q, k, pm, gates, q, k, pm, gates)


def _layer_norm(x, g, b):
    mu = jnp.mean(x, axis=-1, keepdims=True)
    xc = x - mu
    var = jnp.mean(xc * xc, axis=-1, keepdims=True)
    return xc * lax.rsqrt(var + LN_EPS) * g + b


def _head_norm(x, bd_mean, eps):
    mu = _dot_exact_rhs(x, bd_mean)
    xc = x - mu
    var = _dot_exact_rhs(xc * xc, bd_mean)
    return xc * lax.rsqrt(var + eps)


def _mix_out_body(x_ref, yg_ref, ro0_ref, ro1_ref, bonus_ref, rgate_ref, rlg_ref, rlb_ref, mh0_ref, mh1_ref, og_ref,
                  mlg_ref, w_ref, l1g_ref, l1b_ref, rw_ref, rb_ref, bdm_ref,
                  x1_out, topi_out, gate_out, wb_ref, *, alpha, gw, rw):
    @pl.when(pl.program_id(0) == 0)
    def _():
        _cast_rows(w_ref, wb_ref)

    bdm = bdm_ref[...]
    yr = _head_norm(ro0_ref[...] + ro1_ref[...], bdm, RWKV_GN_EPS) * rlg_ref[...] + rlb_ref[...]
    yr = (yr + bonus_ref[...]) * rgate_ref[...]
    ym = _sigmoid(og_ref[...]) * (_head_norm(mh0_ref[...] + mh1_ref[...], bdm, LN_EPS) * mlg_ref[...])
    mix = (_dot(yg_ref[...].astype(BF16), wb_ref[:gw, :]) + _dot(yr.astype(BF16), wb_ref[gw:gw + rw, :])
           + _dot(ym.astype(BF16), wb_ref[gw + rw:, :]))
    x1 = _layer_norm(alpha * x_ref[...] + mix, l1g_ref[...], l1b_ref[...])
    x1_out[...] = x1
    lg = _mm(x1, rw_ref[...], "b3") + rb_ref[...]
    lane = lax.broadcasted_iota(jnp.int32, lg.shape, 1)
    vals, topi = [], jnp.zeros(lg.shape, jnp.int32)
    for j in range(TOP_K):
        mx = jnp.max(lg, axis=1, keepdims=True)
        idx = jnp.min(jnp.where(lg == mx, lane, LANE), axis=1, keepdims=True)
        vals.append(mx)
        topi = jnp.where(lane == j, idx, topi)
        lg = jnp.where(lane == idx, -jnp.inf, lg)
    es = [jnp.exp(vj - vals[0]) for vj in vals]
    den = es[0] + es[1] + es[2] + es[3]
    gate = jnp.zeros(lg.shape, F32)
    for j in range(TOP_K):
        gate = jnp.where(lane == j, es[j] / den, gate)
    topi_out[...] = topi.T[:SUBLANE, :]
    gate_out[...] = gate


def _mix_out(x, yg, ro, bonus, rgate, rlg, rlb, mh, pm, mlg, w_out, layer, l1g, l1b, router_w, router_b, alpha,
             tm=256):
    t, dm = x.shape
    gw, rw, mw = yg.shape[1], bonus.shape[1], mh[0].shape[1]
    assert rw == mw
    rwp = jnp.zeros((dm, LANE), F32).at[:, :N_EXPERTS].set(router_w)
    rbp = jnp.full((1, LANE), NEG_BIG, F32).at[0, :N_EXPERTS].set(router_b)
    row = lambda n: pl.BlockSpec((tm, n), lambda i: (i, 0))
    vec = lambda n: _full((1, n))
    return pl.pallas_call(
        functools.partial(_mix_out_body, alpha=alpha, gw=gw, rw=rw),
        grid=(t // tm,),
        in_specs=[row(dm), row(gw), row(rw), row(rw), row(rw), row(rw), vec(rw), vec(rw), row(mw), row(mw),
                  pl.BlockSpec((tm, mw), lambda i: (i, 3)),
                  vec(mw),
                  pl.BlockSpec((None, dm, dm), lambda i: (layer, 0, 0), pipeline_mode=pl.Buffered(1)),
                  vec(dm), vec(dm), _full((dm, LANE)), vec(LANE), _full((rw, rw))],
        out_specs=[row(dm), pl.BlockSpec((SUBLANE, tm), lambda i: (0, i)), row(LANE)],
        out_shape=[jax.ShapeDtypeStruct((t, dm), F32),
                   jax.ShapeDtypeStruct((SUBLANE, t), jnp.int32), jax.ShapeDtypeStruct((t, LANE), F32)],
        scratch_shapes=[pltpu.VMEM((dm, dm), BF16)],
        compiler_params=_cparams(1),
        name="mix_out",
    )(x, yg, ro[0], ro[1], bonus, rgate, rlg.reshape(1, rw), rlb.reshape(1, rw), mh[0], mh[1], pm,
      mlg.reshape(1, mw), w_out, l1g.reshape(1, dm), l1b.reshape(1, dm), rwp, rbp,
      (_block_diag_ones(rw) / HEAD_DIM).astype(BF16))


def _moe_body(be_ref, nu_ref, ve_ref, xs_ref, w1_ref, b1_ref, w2_ref, b2_ref, o_ref, w1b_ref, w2b_ref, *, dff):
    i = pl.program_id(0)
    active = i < nu_ref[0]
    new_expert = jnp.logical_or(i == 0, be_ref[i] != be_ref[jnp.maximum(i - 1, 0)])

    @pl.when(jnp.logical_and(active, new_expert))
    def _():
        _cast_rows(w1_ref, w1b_ref)
        _cast_rows(w2_ref, w2b_ref)

    @pl.when(active)
    def _():
        rowid = i * MOE_BLOCK + lax.broadcasted_iota(jnp.int32, (MOE_BLOCK, 1), 0)
        xs = jnp.where(rowid < ve_ref[i], xs_ref[...], 0.0)
        hdn = _dot(xs.astype(BF16), w1b_ref[...]) + b1_ref[...]
        g_ = jnp.minimum(hdn[:, :dff], SWIGLU_LIMIT)
        u_ = jnp.clip(hdn[:, dff:], -SWIGLU_LIMIT, SWIGLU_LIMIT)
        act = (u_ + 1.0) * (g_ * _sigmoid(g_ * SWIGLU_ALPHA))
        o_ref[...] = _dot(act.astype(BF16), w2b_ref[...]) + b2_ref[...]

    @pl.when(jnp.logical_not(active))
    def _():
        o_ref[...] = jnp.zeros_like(o_ref)


def _moe_experts(xs, block_e, n_used, valid_end, w1, b1, w2, b2, layer):
    rows, dm = xs.shape
    nb = rows // MOE_BLOCK
    depth, ne, _, dff2 = w1.shape
    dff = dff2 // 2
    grid_spec = pltpu.PrefetchScalarGridSpec(
        num_scalar_prefetch=3,
        grid=(nb,),
        in_specs=[pl.BlockSpec((MOE_BLOCK, dm), lambda i, be, nu, ve: (i, 0)),
                  pl.BlockSpec((None, None, dm, dff2), lambda i, be, nu, ve: (layer, be[i], 0, 0)),
                  pl.BlockSpec((None, None, 1, dff2), lambda i, be, nu, ve: (layer, be[i], 0, 0)),
                  pl.BlockSpec((None, None, dff, dm), lambda i, be, nu, ve: (layer, be[i], 0, 0)),
                  pl.BlockSpec((None, None, 1, dm), lambda i, be, nu, ve: (layer, be[i], 0, 0))],
        out_specs=pl.BlockSpec((MOE_BLOCK, dm), lambda i, be, nu, ve: (i, 0)),
        scratch_shapes=[pltpu.VMEM((dm, dff2), BF16), pltpu.VMEM((dff, dm), BF16)],
    )
    return pl.pallas_call(
        functools.partial(_moe_body, dff=dff),
        grid_spec=grid_spec,
        out_shape=jax.ShapeDtypeStruct((rows, dm), F32),
        compiler_params=_cparams(1),
        name="moe_experts",
    )(block_e, n_used, valid_end, xs, w1, b1.reshape(depth, ne, 1, dff2), w2, b2.reshape(depth, ne, 1, dm))


N_STREAMS = 1
PLAN_TILE = 512
MOE_BLOCK_SHIFT = 8


def _moe_plan_body(e_ref, dest_ref, meta_ref, rank_ref, *, n_tokens, meta_lanes):
    tiles_per_row = n_tokens // PLAN_TILE
    n_tiles = TOP_K * tiles_per_row
    expert = lax.broadcasted_iota(jnp.int32, (N_EXPERTS, PLAN_TILE), 0)
    r_i = lax.broadcasted_iota(jnp.int32, (PLAN_TILE, PLAN_TILE), 0)
    c_i = lax.broadcasted_iota(jnp.int32, (PLAN_TILE, PLAN_TILE), 1)
    earlier = (r_i < c_i).astype(BF16)

    def tile_hits(it):
        j = it // tiles_per_row
        lanes = pl.ds(pl.multiple_of((it % tiles_per_row) * PLAN_TILE, PLAN_TILE), PLAN_TILE)
        return j, lanes, e_ref[pl.ds(j, 1), lanes] == expert

    def rank_step(it, seen):
        j, lanes, hit = tile_hits(it)
        hitf = hit.astype(F32)
        prior = _dot(hit.astype(BF16), earlier) + seen
        rank_ref[pl.ds(j, 1), lanes] = jnp.sum(hitf * prior, axis=0, keepdims=True)
        return seen + jnp.sum(hitf, axis=1, keepdims=True)

    dest_ref[...] = jnp.zeros_like(dest_ref)
    rank_ref[...] = jnp.zeros_like(rank_ref)
    counts = lax.fori_loop(0, n_tiles, rank_step, jnp.zeros((N_EXPERTS, 1), F32))
    padded = ((counts.astype(jnp.int32) + (MOE_BLOCK - 1)) >> MOE_BLOCK_SHIFT) << MOE_BLOCK_SHIFT
    er = lax.broadcasted_iota(jnp.int32, (N_EXPERTS, N_EXPERTS), 0)
    ec = lax.broadcasted_iota(jnp.int32, (N_EXPERTS, N_EXPERTS), 1)
    seg_end = _dot_exact_lhs((ec <= er).astype(BF16),
                             jnp.broadcast_to(padded.astype(F32), (N_EXPERTS, LANE)))[:, 0:1]
    seg_start = seg_end - padded.astype(F32)

    def dest_step(it, carry):
        j, lanes, hit = tile_hits(it)
        base = jnp.sum(jnp.where(hit, seg_start, 0.0), axis=0, keepdims=True)
        dest_ref[pl.ds(j, 1), lanes] = (rank_ref[pl.ds(j, 1), lanes] + base).astype(jnp.int32)
        return carry

    lax.fori_loop(0, n_tiles, dest_step, 0)
    blk_start = (lax.broadcasted_iota(jnp.int32, (N_EXPERTS, meta_lanes), 1) * MOE_BLOCK).astype(F32)
    blk_expert = jnp.minimum(jnp.sum((seg_end <= blk_start).astype(F32), axis=0, keepdims=True), N_EXPERTS - 1.0)
    mine = lax.broadcasted_iota(jnp.int32, (N_EXPERTS, meta_lanes), 0).astype(F32) == blk_expert
    valid_end = jnp.sum(jnp.where(mine, seg_start + counts, 0.0), axis=0, keepdims=True)
    n_used = jnp.broadcast_to(seg_end[N_EXPERTS - 1:N_EXPERTS, :] * (1.0 / MOE_BLOCK), (1, meta_lanes))
    mrow = lax.broadcasted_iota(jnp.int32, (SUBLANE, meta_lanes), 0)
    meta = jnp.where(mrow == 0, blk_expert, jnp.where(mrow == 1, valid_end, jnp.where(mrow == 2, n_used, 0.0)))
    meta_ref[...] = meta.astype(jnp.int32)


def _moe_plan(e_t, n_tokens, n_blocks):
    meta_lanes = -(-n_blocks // LANE) * LANE
    dest, meta = pl.pallas_call(
        functools.partial(_moe_plan_body, n_tokens=n_tokens, meta_lanes=meta_lanes),
        grid=(1,),
        in_specs=[_full((SUBLANE, n_tokens))],
        out_specs=[_full((SUBLANE, n_tokens)), _full((SUBLANE, meta_lanes))],
        out_shape=[jax.ShapeDtypeStruct((SUBLANE, n_tokens), jnp.int32),
                   jax.ShapeDtypeStruct((SUBLANE, meta_lanes), jnp.int32)],
        scratch_shapes=[pltpu.VMEM((SUBLANE, n_tokens), F32)],
        compiler_params=_cparams(1),
        name="moe_plan",
    )(e_t)
    return dest[:TOP_K], meta[0, :n_blocks], meta[1, :n_blocks], meta[2, :1]


def _combine_body(x1_ref, y0_ref, y1_ref, y2_ref, y3_ref, gate_ref, g_ref, b_ref, o_ref, *, alpha):
    gate = gate_ref[...]
    ffn = gate[:, 0:1] * y0_ref[...]
    for j, y_ref in enumerate((y1_ref, y2_ref, y3_ref), start=1):
        ffn = ffn + gate[:, j:j + 1] * y_ref[...]
    o_ref[...] = _layer_norm(alpha * x1_ref[...] + ffn, g_ref[...], b_ref[...])


def _combine(x1, yg, gate, ln_g, ln_b, alpha, tm=256):
    t, dm = x1.shape
    n_tiles = t // tm
    expert_rows = lambda j: pl.BlockSpec((tm, dm), lambda i: (i + j * n_tiles, 0))
    return pl.pallas_call(
        functools.partial(_combine_body, alpha=alpha),
        grid=(n_tiles,),
        in_specs=[pl.BlockSpec((tm, dm), lambda i: (i, 0))] + [expert_rows(j) for j in range(TOP_K)]
                 + [pl.BlockSpec((tm, LANE), lambda i: (i, 0)), _full((1, dm)), _full((1, dm))],
        out_specs=pl.BlockSpec((tm, dm), lambda i: (i, 0)),
        out_shape=jax.ShapeDtypeStruct((t, dm), F32),
        compiler_params=_cparams(1),
        name="combine_ln",
    )(x1, yg, yg, yg, yg, gate, ln_g.reshape(1, dm), ln_b.reshape(1, dm))


SC_CORES = 2
SC_SUBCORES = 16
SC_WORKERS = SC_CORES * SC_SUBCORES


def _sc_gather_rows(table, idx, window):
    n = idx.shape[0]
    dim = table.shape[1]
    n_steps = n // (SC_WORKERS * window)
    assert n_steps * window * SC_WORKERS == n and n_steps % 2 == 0 and window % SUBLANE == 0 and window <= LANE
    idx3 = idx.reshape(SC_WORKERS, n_steps, window)
    mesh = plsc.VectorSubcoreMesh(core_axis_name="c", subcore_axis_name="s",
                                  num_cores=SC_CORES, num_subcores=SC_SUBCORES)

    def body(table_hbm, idx_hbm, out_hbm, idx_v, rows_v, gsem, wsem):
        wid = lax.axis_index("s") * SC_CORES + lax.axis_index("c")
        pltpu.sync_copy(idx_hbm.at[wid], idx_v)

        def gather(j, buf):
            return pltpu.make_async_copy(table_hbm.at[idx_v.at[j]], rows_v.at[buf], gsem.at[buf])

        def write(j, buf):
            base = pl.multiple_of((wid * n_steps + j) * window, window)
            return pltpu.make_async_copy(rows_v.at[buf], out_hbm.at[pl.ds(base, window)], wsem.at[buf])

        gather(0, 0).start()

        @pl.loop(0, n_steps, step=2)
        def _(j0):
            for buf in range(2):
                j = j0 + buf
                gather(j, buf).wait()

                @pl.when(j >= 1)
                def _():
                    write(j - 1, 1 - buf).wait()

                @pl.when(j + 1 < n_steps)
                def _():
                    gather(j + 1, 1 - buf).start()

                write(j, buf).start()

        write(n_steps - 1, 1).wait()

    return pl.kernel(
        body, out_type=jax.ShapeDtypeStruct((n, dim), table.dtype), mesh=mesh,
        scratch_types=[pltpu.VMEM((n_steps, window), jnp.int32), pltpu.VMEM((2, window, dim), table.dtype),
                       pltpu.SemaphoreType.DMA((2,)), pltpu.SemaphoreType.DMA((2,))],
        name="sc_gather",
    )(table, idx3)


def _sc_scatter_rows(src, dest, n_out, window):
    t, dim = src.shape
    k = dest.shape[0]
    n_steps = t // (SC_WORKERS * window)
    assert n_steps * window * SC_WORKERS == t and n_steps % 2 == 0 and window % SUBLANE == 0 and window <= LANE
    idx3 = dest.reshape(k, SC_WORKERS, n_steps, window).transpose(1, 2, 0, 3).reshape(SC_WORKERS, n_steps * k, window)
    mesh = plsc.VectorSubcoreMesh(core_axis_name="c", subcore_axis_name="s",
                                  num_cores=SC_CORES, num_subcores=SC_SUBCORES)

    def body(src_hbm, idx_hbm, out_hbm, idx_v, rows_v, rsem, ssem):
        wid = lax.axis_index("s") * SC_CORES + lax.axis_index("c")
        pltpu.sync_copy(idx_hbm.at[wid], idx_v)

        def read(s, buf):
            base = pl.multiple_of((wid * n_steps + s) * window, window)
            return pltpu.make_async_copy(src_hbm.at[pl.ds(base, window)], rows_v.at[buf], rsem.at[buf])

        def scatter(s, j, buf):
            return pltpu.make_async_copy(rows_v.at[buf], out_hbm.at[idx_v.at[s * k + j]], ssem.at[buf])

        read(0, 0).start()

        @pl.loop(0, n_steps, step=2)
        def _(s0):
            for buf in range(2):
                s = s0 + buf
                read(s, buf).wait()

                @pl.when(s >= 1)
                def _():
                    for j in range(k):
                        scatter(s - 1, j, 1 - buf).wait()

                @pl.when(s + 1 < n_steps)
                def _():
                    read(s + 1, 1 - buf).start()

                for j in range(k):
                    scatter(s, j, buf).start()

        for j in range(k):
            scatter(n_steps - 1, j, 1).wait()

    return pl.kernel(
        body, out_type=jax.ShapeDtypeStruct((n_out, dim), src.dtype), mesh=mesh,
        scratch_types=[pltpu.VMEM((n_steps * k, window), jnp.int32), pltpu.VMEM((2, window, dim), src.dtype),
                       pltpu.SemaphoreType.DMA((2,)), pltpu.SemaphoreType.DMA((2,))],
        name="sc_scatter",
    )(src, idx3)


def _pad_cols(w, width):
    return jnp.pad(w, ((0, 0), (0, width - w.shape[1])))


def kernel(x, w_in, gmlp_ln_g, gmlp_ln_b, gmlp_ws, gmlp_bs, rwkv_mu, rwkv_w0, rwkv_w2, rwkv_a0, rwkv_a2, rwkv_g2, rwkv_k_k, rwkv_k_a, rwkv_r_k, rwkv_ln_g, rwkv_ln_b, mlstm_conv_w, mlstm_conv_b, mlstm_gate_b, mlstm_ln_g, w_out, ln1_g, ln1_b, router_w, router_b, exp_w1, exp_b1, exp_w2, exp_b2, ln2_g, ln2_b):
    batch, seq, dm = x.shape
    depth = w_in.shape[0]
    sb = batch // N_STREAMS if batch % N_STREAMS == 0 else batch
    t = sb * seq
    gw = gmlp_ln_g.shape[1]
    rw = rwkv_w0.shape[2]
    mw = mlstm_ln_g.shape[1]
    g_proj = 2 * gw
    r_proj = 3 * rw + W_LORA + A_LORA + G_LORA
    alpha = (2 * depth) ** 0.25
    n_blocks = -(-t * TOP_K // MOE_BLOCK) + N_EXPERTS
    streams = [x[i * sb:(i + 1) * sb].reshape(t, dm) for i in range(batch // sb)]
    for l in range(depth):
        mixed = []
        for xf in streams:
            pg, pr, pm = _proj(xf, w_in, l, g_proj, r_proj)
            y_g = _gmlp(pg, gmlp_ln_g[l], gmlp_ln_b[l], gmlp_ws[l], gmlp_bs[l])
            r, v, a, kd, b, lw, bonus, rgate = _rwkv_prep(
                pr, seq, rwkv_mu[l], rwkv_w0[l], rwkv_w2[l], rwkv_a0[l], rwkv_a2[l], rwkv_g2[l],
                rwkv_k_k[l], rwkv_k_a[l], rwkv_r_k[l].reshape(-1))
            ro = _rwkv_scan(r, v, a, kd, b, lw, sb, seq)
            q, k, gates = _mlstm_prep(pm, seq, mlstm_conv_w[l], mlstm_conv_b[l], mlstm_gate_b[l], mw)
            mh = _mlstm_scan(q, k, pm, gates, sb, seq)
            mixed.append(_mix_out(xf, y_g, ro, bonus, rgate, rwkv_ln_g[l], rwkv_ln_b[l], mh, pm, mlstm_ln_g[l],
                                  w_out, l, ln1_g[l], ln1_b[l], router_w[l], router_b[l], alpha))
        streams = []
        for x1, topi, gate in mixed:
            dest, block_e, valid_end, n_used = _moe_plan(topi, t, n_blocks)
            xs = _sc_scatter_rows(x1, dest, n_blocks * MOE_BLOCK, window=32)
            ys = _moe_experts(xs, block_e, n_used, valid_end, exp_w1, exp_b1, exp_w2, exp_b2, l)
            yg = _sc_gather_rows(ys, dest.reshape(-1), window=32)
            streams.append(_combine(x1, yg, gate, ln2_g[l], ln2_b[l], alpha))
    return jnp.concatenate(streams, axis=0).reshape(batch, seq, dm)
```

```python
import functools
import math

import jax
import jax.numpy as jnp
from jax import lax
from jax.experimental import pallas as pl
from jax.experimental.pallas import tpu as pltpu
from jax.experimental.pallas import tpu_sc as plsc

F32 = jnp.float32
BF16 = jnp.bfloat16
HI = lax.Precision.HIGHEST

HEAD_DIM = 64
GMLP_CHUNK = 128
MLSTM_CHUNK = 128
RWKV_CHUNK = 64
W_LORA = 64
A_LORA = 64
G_LORA = 128
N_EXPERTS = 32
TOP_K = 4
MOE_BLOCK = 256
SWIGLU_LIMIT = 7.0
SWIGLU_ALPHA = 1.702
LN_EPS = 1e-5
RWKV_GN_EPS = 64e-5
LANE = 128
SUBLANE = 8
VMEM_LIMIT = 48 * 1024 * 1024
NEG_BIG = -1e30


def _cparams(n_axes):
    return pltpu.CompilerParams(dimension_semantics=("arbitrary",) * n_axes,
                                vmem_limit_bytes=VMEM_LIMIT)


def _full(shape):
    return pl.BlockSpec(shape, lambda *_: (0,) * len(shape))


def _dot(a, b, precision=None):
    return jnp.dot(a, b, preferred_element_type=F32, precision=precision)


def _dot_nt(a, b, precision=None):
    return lax.dot_general(a, b, (((1,), (1,)), ((), ())), preferred_element_type=F32, precision=precision)


def _dot_tn(a, b, precision=None):
    return lax.dot_general(a, b, (((0,), (0,)), ((), ())), preferred_element_type=F32, precision=precision)


def _split(x):
    hi = x.astype(BF16)
    return hi, (x - hi.astype(F32)).astype(BF16)


def _split3(x):
    hi = x.astype(BF16)
    r1 = x - hi.astype(F32)
    mid = r1.astype(BF16)
    return hi, mid, (r1 - mid.astype(F32)).astype(BF16)


def _mm(a, b, mode, dot=_dot):
    if mode == "hi":
        return dot(a, b, HI)
    if mode == "b1":
        return dot(a.astype(BF16), b.astype(BF16))
    ah, al = _split(a)
    bh, bl = _split(b)
    return dot(ah, bh) + (dot(ah, bl) + dot(al, bh))


def _dot_exact_lhs(a_bf16, x):
    hi, mid, lo = _split3(x)
    return _dot(a_bf16, hi) + (_dot(a_bf16, mid) + _dot(a_bf16, lo))


def _dot_exact_rhs(x, b_bf16):
    hi, mid, lo = _split3(x)
    return _dot(hi, b_bf16) + (_dot(mid, b_bf16) + _dot(lo, b_bf16))


def _pack_bf16_pairs(x):
    n = x.shape[1] // 2
    lo = pltpu.bitcast(x[:, :n].astype(BF16).astype(F32), jnp.uint32)
    hi = pltpu.bitcast(x[:, n:].astype(BF16).astype(F32), jnp.uint32)
    return hi | (lo >> 16)


def _unpack_bf16_pairs(w):
    lo = pltpu.bitcast(w << 16, F32)
    hi = pltpu.bitcast(w & jnp.uint32(0xFFFF0000), F32)
    return lo, hi


def _sigmoid(x):
    return 1.0 / (1.0 + jnp.exp(-x))


def _softplus(x):
    return jnp.maximum(x, 0.0) + jnp.log1p(jnp.exp(-jnp.abs(x)))


def _block_diag_ones(width):
    h = jnp.arange(width) // HEAD_DIM
    return (h[:, None] == h[None, :]).astype(F32)


CAST_ROWS = 128


def _cast_rows(src_ref, dst_ref):
    n_src, n_dst = src_ref.shape[1], dst_ref.shape[1]
    whole = n_src // LANE * LANE

    def step(r, carry):
        rows = pl.ds(pl.multiple_of(r * CAST_ROWS, CAST_ROWS), CAST_ROWS)
        dst_ref[rows, :whole] = src_ref[rows, :whole].astype(BF16)
        if n_dst > whole:
            tail = [src_ref[rows, whole:]] if n_src > whole else []
            tail.append(jnp.zeros((CAST_ROWS, n_dst - n_src), F32))
            dst_ref[rows, whole:] = jnp.concatenate(tail, axis=1).astype(BF16)
        return carry
    lax.fori_loop(0, src_ref.shape[0] // CAST_ROWS, step, 0)


def _proj_body(x_ref, w_ref, pg_ref, pr_ref, pm_ref, wb_ref, *, ng, nr):
    @pl.when(pl.program_id(0) == 0)
    def _():
        _cast_rows(w_ref, wb_ref)

    xb = x_ref[...].astype(BF16)
    pg_ref[...] = _dot(xb, wb_ref[:, :ng])
    pr_ref[...] = _dot(xb, wb_ref[:, ng:ng + nr])
    pm_ref[...] = _dot(xb, wb_ref[:, ng + nr:])


def _proj(x, w_in, layer, ng, nr, tm=256):
    t, d = x.shape
    p_in = w_in.shape[2]
    p_pad = -(-p_in // LANE) * LANE
    nm = p_pad - ng - nr
    row = lambda n: pl.BlockSpec((tm, n), lambda i: (i, 0))
    return pl.pallas_call(
        functools.partial(_proj_body, ng=ng, nr=nr),
        grid=(t // tm,),
        in_specs=[row(d), pl.BlockSpec((None, d, p_in), lambda i: (layer, 0, 0), pipeline_mode=pl.Buffered(1))],
        out_specs=[row(ng), row(nr), row(nm)],
        out_shape=[jax.ShapeDtypeStruct((t, n), F32) for n in (ng, nr, nm)],
        scratch_shapes=[pltpu.VMEM((d, p_pad), BF16)],
        compiler_params=_cparams(1),
        name="in_proj",
    )(x, w_in)


def _gmlp_body(pg_ref, lng_ref, lnb_ref, ws_ref, bst_ref, o_ref, *, gw, chunks):
    p = pg_ref[...]
    p = 0.5 * p * (1.0 + lax.erf(p * math.sqrt(0.5)))
    u, v = p[:, :gw], p[:, gw:]
    mu = jnp.mean(v, axis=-1, keepdims=True)
    vc = v - mu
    var = jnp.mean(vc * vc, axis=-1, keepdims=True)
    vn = vc * lax.rsqrt(var + LN_EPS) * lng_ref[...] + lnb_ref[...]
    n_heads = gw // HEAD_DIM
    for c in range(chunks):
        rows = slice(c * GMLP_CHUNK, (c + 1) * GMLP_CHUNK)
        ys = []
        for h in range(n_heads):
            cols = slice(h * HEAD_DIM, (h + 1) * HEAD_DIM)
            y = _dot(ws_ref[h], vn[rows, cols].astype(BF16)) + bst_ref[:, h:h + 1]
            ys.append(y)
        o_ref[rows, :] = u[rows, :] * jnp.concatenate(ys, axis=1)


def _gmlp(pg, ln_g, ln_b, ws, bs, chunks=4):
    t = pg.shape[0]
    gw = pg.shape[1] // 2
    n_heads = gw // HEAD_DIM
    tm = chunks * GMLP_CHUNK
    bst = jnp.zeros((GMLP_CHUNK, LANE), F32).at[:, :n_heads].set(bs.T)
    return pl.pallas_call(
        functools.partial(_gmlp_body, gw=gw, chunks=chunks),
        grid=(t // tm,),
        in_specs=[pl.BlockSpec((tm, 2 * gw), lambda i: (i, 0)), _full((1, gw)), _full((1, gw)),
                  _full((n_heads, GMLP_CHUNK, GMLP_CHUNK)), _full((GMLP_CHUNK, LANE))],
        out_specs=pl.BlockSpec((tm, gw), lambda i: (i, 0)),
        out_shape=jax.ShapeDtypeStruct((t, gw), F32),
        compiler_params=_cparams(1),
        name="gmlp",
    )(pg, ln_g.reshape(1, gw), ln_b.reshape(1, gw), ws.astype(BF16), bst)


def _halo_specs(tm, width, n_rows):
    per8 = tm // SUBLANE
    last = n_rows // SUBLANE - 1
    prev = pl.BlockSpec((SUBLANE, width), lambda i: (jnp.maximum(i * per8 - 1, 0), 0))
    nxt = pl.BlockSpec((SUBLANE, width), lambda i: (jnp.minimum((i + 1) * per8, last), 0))
    return prev, nxt


def _neighbours(cur, prev_blk, next_blk, tiles_per_seq):
    tm = cur.shape[0]
    j = pl.program_id(0) % tiles_per_seq
    prev_row = jnp.where(j > 0, prev_blk[SUBLANE - 1:SUBLANE, :], 0.0)
    next_row = jnp.where(j < tiles_per_seq - 1, next_blk[0:1, :], 0.0)
    ridx = lax.broadcasted_iota(jnp.int32, cur.shape, 0)
    before = jnp.where(ridx == 0, prev_row, pltpu.roll(cur, 1, 0))
    after = jnp.where(ridx == tm - 1, next_row, pltpu.roll(cur, tm - 1, 0))
    return before, after


def _rwkv_prep_body(pr_ref, prev_ref, next_ref, mu_ref, w0_ref, w2_ref, a0_ref, a2_ref, g2_ref,
                    kk_ref, ka_ref, rk_ref, bd_ref,
                    r_out, v_out, a_out, kd_out, b_out, lw_out, bonus_out, gate_out, *, rw, tiles_per_seq):
    pf = pr_ref[...]
    before, after = _neighbours(pf, prev_ref[...], next_ref[...], tiles_per_seq)
    pf = pf + mu_ref[0:1, :] * (before - pf) + mu_ref[1:2, :] * (after - pf)
    o3 = 3 * rw
    r, k, v = pf[:, :rw], pf[:, rw:2 * rw], pf[:, 2 * rw:o3]
    wd = pf[:, o3:o3 + W_LORA]
    ad = pf[:, o3 + W_LORA:o3 + W_LORA + A_LORA]
    gd = pf[:, o3 + W_LORA + A_LORA:]
    bd = bd_ref[...]
    kk = k * kk_ref[...]
    ss = _dot_exact_rhs(kk * kk, bd)
    kk = kk / jnp.maximum(jnp.sqrt(ss), 1e-12)
    twd = jnp.tanh(wd)
    ksum = jnp.zeros_like(k)
    for d in range(2):
        w_log = -_softplus(-(w0_ref[d:d + 1, :] + _mm(twd, w2_ref[d], "b3"))) - 0.5
        lw_out[d] = -jnp.exp(w_log)
        iclr = _sigmoid(a0_ref[d:d + 1, :] + _mm(ad, a2_ref[d], "b3"))
        kd = k * (1.0 + (iclr - 1.0) * ka_ref[...])
        kd_out[d] = kd
        b_out[d] = kk * iclr
        ksum = ksum + kd
    r_out[...] = r
    v_out[...] = v
    a_out[...] = -kk
    bonus_out[...] = _dot_exact_rhs(r * ksum * rk_ref[...], bd) * v
    gate_out[...] = _dot(_sigmoid(gd).astype(BF16), g2_ref[...])


def _rwkv_prep(pr, seq, mu, w0, w2, a0, a2, g2, k_k, k_a, r_k, tm=256):
    t, rproj = pr.shape
    rw = w0.shape[1]
    tiles_per_seq = seq // tm
    prev, nxt = _halo_specs(tm, rproj, t)
    row = pl.BlockSpec((tm, rw), lambda i: (i, 0))
    row2 = pl.BlockSpec((2, tm, rw), lambda i: (0, i, 0))
    one = jax.ShapeDtypeStruct((t, rw), F32)
    two = jax.ShapeDtypeStruct((2, t, rw), F32)
    return pl.pallas_call(
        functools.partial(_rwkv_prep_body, rw=rw, tiles_per_seq=tiles_per_seq),
        grid=(t // tm,),
        in_specs=[pl.BlockSpec((tm, rproj), lambda i: (i, 0)), prev, nxt,
                  _full((2, rproj)), _full((2, rw)), _full((2, W_LORA, rw)), _full((2, rw)),
                  _full((2, A_LORA, rw)), _full((G_LORA, rw)), _full((1, rw)), _full((1, rw)),
                  _full((1, rw)), _full((rw, rw))],
        out_specs=[row, row, row, row2, row2, row2, row, row],
        out_shape=[one, one, one, two, two, two, one, one],
        compiler_params=_cparams(1),
        name="rwkv_prep",
    )(pr, pr, pr, mu, w0, w2, a0, a2, g2.astype(BF16), k_k.reshape(1, rw), k_a.reshape(1, rw),
      r_k.reshape(1, rw), _block_diag_ones(rw).astype(BF16))


P_G, P_INV, P_APPLY, P_STATE, P_SEQ = "b1", "b1", "b1", "b1", "b3"


def _rwkv_intra_body(r_ref, v_ref, a_ref, kd_ref, b_ref, lw_ref, rq_out, o0_out, mtx_out, hc_out,
                     *, n_heads, chunks):
    L = RWKV_CHUNK
    d = pl.program_id(0)
    row = lax.broadcasted_iota(jnp.int32, (L, L), 0)
    col = lax.broadcasted_iota(jnp.int32, (L, L), 1)
    fwd = d == 0
    rel = (col - row) * (1 - 2 * d)
    incl = rel <= 0
    strict = rel < 0
    eye = (row == col).astype(F32)
    tri = incl.astype(BF16)
    pairs = []
    for c in range(chunks):
        rows = slice(c * L, (c + 1) * L)
        lw = lw_ref[rows, :]
        cum = _dot_exact_lhs(tri, lw)
        tot = jnp.where(fwd, cum[L - 1:L, :], cum[0:1, :])
        e_neg = jnp.exp(-cum)
        e_end = jnp.exp(tot - cum)
        e_tot = jnp.exp(tot)
        r, v, a, kd, b = r_ref[rows, :], v_ref[rows, :], a_ref[rows, :], kd_ref[rows, :], b_ref[rows, :]
        at, rt, bt, kt = a * jnp.exp(cum - lw), r * jnp.exp(cum), b * e_neg, kd * e_neg
        kend, bend = kd * e_end, b * e_end
        for h in range(n_heads):
            sl = slice(h * HEAD_DIM, (h + 1) * HEAD_DIM)
            pairs.append(dict(at=at[:, sl], rt=rt[:, sl], bt=bt[:, sl], kt=kt[:, sl], v=v[:, sl],
                              kend=kend[:, sl], bend=bend[:, sl], e_tot=e_tot[:, sl]))
    for p in pairs:
        p["g"] = _mm(jnp.concatenate([p["at"], p["rt"]], axis=0),
                     jnp.concatenate([p["bt"], p["kt"]], axis=0), P_G, _dot_nt)
    for p in pairs:
        g = p.pop("g")
        p["pw"] = jnp.where(strict, g[:L, :L], 0.0)
        p["a_ak"] = jnp.where(strict, g[:L, L:], 0.0)
        p["m_rb"] = jnp.where(incl, g[L:, :L], 0.0)
        p["m_rk"] = jnp.where(incl, g[L:, L:], 0.0)
        p["inv"] = eye + p["pw"]
    for _ in range(int(math.log2(L)) - 1):
        for p in pairs:
            p["pw"] = _mm(p["pw"], p["pw"], P_INV)
        for p in pairs:
            p["inv"] = p["inv"] + _mm(p["inv"], p["pw"], P_INV)
    for p in pairs:
        p["akv"] = _mm(p["a_ak"], p["v"], P_APPLY)
    for p in pairs:
        p["wu"] = _mm(p["inv"], jnp.concatenate([p["at"], p["akv"]], axis=1), P_APPLY)
    for p in pairs:
        p["mwu"] = _mm(p["m_rb"], p["wu"], P_APPLY)
    for p in pairs:
        p["o0"] = p["mwu"][:, HEAD_DIM:] + _mm(p["m_rk"], p["v"], P_APPLY)
    for p in pairs:
        p["bw"] = _mm(p["bend"], p["wu"], P_STATE, _dot_tn)
    for p in pairs:
        p["hc"] = _mm(p["kend"], p["v"], P_STATE, _dot_tn) + p["bw"][:, HEAD_DIM:]
    for c in range(chunks):
        ps = pairs[c * n_heads:(c + 1) * n_heads]
        rows = slice(c * L, (c + 1) * L)
        krows = slice(c * HEAD_DIM, (c + 1) * HEAD_DIM)
        rq_out[rows, :] = jnp.concatenate([p["rt"] + p["mwu"][:, :HEAD_DIM] for p in ps], axis=1)
        o0_out[rows, :] = jnp.concatenate([p["o0"] for p in ps], axis=1)
        mtx_out[krows, :] = jnp.concatenate([eye * p["e_tot"] + p["bw"][:, :HEAD_DIM] for p in ps], axis=1)
        hc_out[krows, :] = jnp.concatenate([p["hc"] for p in ps], axis=1)


def _rwkv_intra(r, v, a, kd, b, lw, chunks=4):
    t, rw = r.shape
    n_heads = rw // HEAD_DIM
    tm = chunks * RWKV_CHUNK
    tk = chunks * HEAD_DIM
    n_tiles = t // tm
    one = pl.BlockSpec((tm, rw), lambda d, i: (i, 0))
    two = pl.BlockSpec((None, tm, rw), lambda d, i: (d, i, 0))
    twok = pl.BlockSpec((None, tk, rw), lambda d, i: (d, i, 0))
    return pl.pallas_call(
        functools.partial(_rwkv_intra_body, n_heads=n_heads, chunks=chunks),
        grid=(2, n_tiles),
        in_specs=[one, one, one, two, two, two],
        out_specs=[two, two, twok, twok],
        out_shape=[jax.ShapeDtypeStruct((2, t, rw), F32), jax.ShapeDtypeStruct((2, t, rw), F32),
                   jax.ShapeDtypeStruct((2, n_tiles * tk, rw), F32),
                   jax.ShapeDtypeStruct((2, n_tiles * tk, rw), F32)],
        compiler_params=_cparams(2),
        name="rwkv_intra",
    )(r, v, a, kd, b, lw)


def _rwkv_seq_body(rq0, o00, mtx0, hc0, rq1, o01, mtx1, hc1, out0, out1, h_ref, *, n_heads, batch):
    c = pl.program_id(0)

    @pl.when(c == 0)
    def _():
        h_ref[...] = jnp.zeros_like(h_ref)

    L = RWKV_CHUNK
    for d, (rq, o0, mtx, hc, out) in enumerate(((rq0, o00, mtx0, hc0, out0), (rq1, o01, mtx1, hc1, out1))):
        for bi in range(batch):
            rq_t, mtx_t = rq[bi], mtx[bi]
            state = h_ref[d, bi]
            outs, states = [], []
            for h in range(n_heads):
                sl = slice(h * HEAD_DIM, (h + 1) * HEAD_DIM)
                prod = _mm(jnp.concatenate([rq_t[:, sl], mtx_t[:, sl]], axis=0), state[:, sl], P_SEQ)
                outs.append(prod[:L])
                states.append(prod[L:])
            out[bi] = jnp.concatenate(outs, axis=1) + o0[bi]
            h_ref[d, bi] = jnp.concatenate(states, axis=1) + hc[bi]


def _rwkv_seq(rq, o0, mtx, hc, batch, seq):
    _, t, rw = rq.shape
    n_heads = rw // HEAD_DIM
    L = RWKV_CHUNK
    nc = seq // L
    as4 = lambda x: x.reshape(2, batch, x.shape[1] // batch, rw)
    rq, o0, mtx, hc = as4(rq), as4(o0), as4(mtx), as4(hc)
    fwd = lambda rows: pl.BlockSpec((None, batch, rows, rw), lambda c: (0, 0, c, 0))
    bwd = lambda rows: pl.BlockSpec((None, batch, rows, rw), lambda c: (1, 0, nc - 1 - c, 0))
    out0, out1 = pl.pallas_call(
        functools.partial(_rwkv_seq_body, n_heads=n_heads, batch=batch),
        grid=(nc,),
        in_specs=[fwd(L), fwd(L), fwd(HEAD_DIM), fwd(HEAD_DIM), bwd(L), bwd(L), bwd(HEAD_DIM), bwd(HEAD_DIM)],
        out_specs=[pl.BlockSpec((batch, L, rw), lambda c: (0, c, 0)),
                   pl.BlockSpec((batch, L, rw), lambda c: (0, nc - 1 - c, 0))],
        out_shape=[jax.ShapeDtypeStruct((batch, seq, rw), F32)] * 2,
        scratch_shapes=[pltpu.VMEM((2, batch, HEAD_DIM, rw), F32)],
        compiler_params=_cparams(1),
        name="rwkv_seq",
    )(rq, o0, mtx, hc, rq, o0, mtx, hc)
    return out0.reshape(t, rw), out1.reshape(t, rw)


def _rwkv_scan(r, v, a, kd, b, lw, batch, seq):
    rq, o0, mtx, hc = _rwkv_intra(r, v, a, kd, b, lw)
    return _rwkv_seq(rq, o0, mtx, hc, batch, seq)


def _mlstm_prep_body(qk_ref, prev_ref, next_ref, g_ref, cw_ref, cb_ref, gb_ref, q_out, k_out, gate_out,
                     *, mw, n_heads, tiles_per_seq):
    x = qk_ref[...]
    before, after = _neighbours(x, prev_ref[...], next_ref[...], tiles_per_seq)
    y = cb_ref[...] + before * cw_ref[0:1, :] + x * cw_ref[1:2, :] + after * cw_ref[2:3, :]
    y = y * _sigmoid(y)
    q_out[...] = y[:, :mw]
    k_out[...] = y[:, mw:] * (HEAD_DIM ** -0.5)
    g = g_ref[...] + gb_ref[...]
    lane = lax.broadcasted_iota(jnp.int32, g.shape, 1)
    for d in range(2):
        ig = g if d == 0 else pltpu.roll(g, LANE - n_heads, 1)
        fg = pltpu.roll(g, LANE - (1 + d) * n_heads, 1)
        lf = -_softplus(-fg)
        gate_out[d] = jnp.where(lane < n_heads, ig, jnp.where(lane < 2 * n_heads, lf, 0.0))


def _mlstm_prep(pm, seq, conv_w, conv_b, gate_b, mw, tm=256):
    t = pm.shape[0]
    n_heads = mw // HEAD_DIM
    tiles_per_seq = seq // tm
    w2 = 2 * mw
    prev, nxt = _halo_specs(tm, w2, t)
    gcol = (4 * mw) // LANE
    gb = jnp.zeros((1, LANE), F32).at[0, :4 * n_heads].set(gate_b)
    row = pl.BlockSpec((tm, mw), lambda i: (i, 0))
    return pl.pallas_call(
        functools.partial(_mlstm_prep_body, mw=mw, n_heads=n_heads, tiles_per_seq=tiles_per_seq),
        grid=(t // tm,),
        in_specs=[pl.BlockSpec((tm, w2), lambda i: (i, 0)), prev, nxt,
                  pl.BlockSpec((tm, LANE), lambda i: (i, gcol)),
                  _full((3, w2)), _full((1, w2)), _full((1, LANE))],
        out_specs=[row, row, pl.BlockSpec((2, tm, LANE), lambda i: (0, i, 0))],
        out_shape=[jax.ShapeDtypeStruct((t, mw), F32), jax.ShapeDtypeStruct((t, mw), F32),
                   jax.ShapeDtypeStruct((2, t, LANE), F32)],
        compiler_params=_cparams(1),
        name="mlstm_prep",
    )(pm, pm, pm, pm, conv_w, conv_b.reshape(1, w2), gb)


def _mlstm_scan_body(q0_ref, k0_ref, v0_ref, g0_ref, q1_ref, k1_ref, v1_ref, g1_ref, o0_ref, o1_ref,
                     c_ref, m_ref, *, n_heads):
    L = MLSTM_CHUNK
    H = n_heads

    @pl.when(pl.program_id(1) == 0)
    def _():
        c_ref[...] = jnp.zeros_like(c_ref)
        m_ref[...] = jnp.zeros_like(m_ref)

    row = lax.broadcasted_iota(jnp.int32, (L, L), 0)
    col = lax.broadcasted_iota(jnp.int32, (L, L), 1)
    trow = lax.broadcasted_iota(jnp.int32, (L, LANE), 0)
    low = lax.broadcasted_iota(jnp.int32, (L, LANE), 1) < HEAD_DIM
    xr = lax.broadcasted_iota(jnp.int32, (LANE, H * L), 0)
    xc = lax.broadcasted_iota(jnp.int32, (LANE, H * L), 1)
    spread = (xr - H == lax.shift_right_logical(xc, int(math.log2(L)))).astype(BF16)
    hs = []
    for d, (q_ref, k_ref, v_ref, g_ref) in enumerate(((q0_ref, k0_ref, v0_ref, g0_ref),
                                                      (q1_ref, k1_ref, v1_ref, g1_ref))):
        incl = (col <= row) if d == 0 else (col >= row)
        last = L - 1 if d == 0 else 0
        g = g_ref[...]
        bcum = _dot_exact_lhs(incl.astype(BF16), g)
        z = pltpu.roll(g, H, 1) - bcum
        cmax = z
        shift = 1
        while shift < L:
            if d == 0:
                moved = jnp.where(trow >= shift, pltpu.roll(cmax, shift, 0), -jnp.inf)
            else:
                moved = jnp.where(trow < L - shift, pltpu.roll(cmax, L - shift, 0), -jnp.inf)
            cmax = jnp.maximum(cmax, moved)
            shift *= 2
        m_prev = m_ref[d, 0:1, :]
        top = jnp.maximum(cmax, m_prev)
        b_last = bcum[last:last + 1, :]
        lwc = b_last + z
        m_new = jnp.maximum(b_last + m_prev, jnp.max(lwc, axis=0, keepdims=True))
        m_ref[d, 0:1, :] = m_new
        per_row = jnp.concatenate(
            [-top, m_prev - top, bcum + top, jnp.exp(lwc - m_new),
             jnp.broadcast_to(jnp.exp(b_last + m_prev - m_new), (SUBLANE, LANE))], axis=0)
        wide = _dot_exact_rhs(per_row, spread)
        z_t = z.T
        q, k, v = q_ref[...], k_ref[...], v_ref[...]
        for h in range(H):
            slab = slice(h // 2 * LANE, (h // 2 + 1) * LANE)
            cols = slice(h * L, (h + 1) * L)
            mine = low if h % 2 == 0 else jnp.logical_not(low)
            kh = jnp.where(mine, k[:, slab], 0.0)
            hs.append(dict(
                qh=jnp.where(mine, q[:, slab], 0.0).astype(BF16), kh=kh.astype(BF16),
                vext=jnp.where(mine, v[:, slab], 1.0).astype(BF16),
                decay=jnp.exp(jnp.where(incl, wide[0:L, cols] + z_t[H + h:H + h + 1, :], -jnp.inf)),
                w_inter=jnp.exp(wide[L:2 * L, cols]), floor=jnp.exp(-wide[2 * L:3 * L, cols]),
                wk=(wide[3 * L:4 * L, cols] * kh).astype(BF16), dec=wide[4 * L:4 * L + 1, cols],
                cst=c_ref[d, h]))
    for p in hs:
        p["sc"] = (_dot_nt(p["qh"], p["kh"]) * p["decay"]).astype(BF16)
    for p in hs:
        p["numext"] = _dot(p["sc"], p["vext"]) + p["w_inter"] * _dot(p["qh"], p["cst"].astype(BF16))
    for p in hs:
        p["upd"] = _dot_tn(p["wk"], p["vext"])
    for d, o_ref in enumerate((o0_ref, o1_ref)):
        res = []
        for h in range(H):
            p = hs[d * H + h]
            den = pltpu.roll(p["numext"], HEAD_DIM, 1)
            res.append(p["numext"] / jnp.maximum(jnp.abs(den), p["floor"]))
            c_ref[d, h] = p["dec"] * p["cst"] + p["upd"]
        for pair in range(H // 2):
            o_ref[:, pair * LANE:(pair + 1) * LANE] = jnp.where(low, res[2 * pair], res[2 * pair + 1])


def _mlstm_scan(q, k, pm, gates, batch, seq):
    t, mw = q.shape
    n_heads = mw // HEAD_DIM
    L = MLSTM_CHUNK
    nc = seq // L
    fwd = lambda bi, c: bi * nc + c
    bwd = lambda bi, c: bi * nc + nc - 1 - c
    specs = []
    for d, blk in enumerate((fwd, bwd)):
        specs += [pl.BlockSpec((L, mw), lambda bi, c, blk=blk: (blk(bi, c), 0)),
                  pl.BlockSpec((L, mw), lambda bi, c, blk=blk: (blk(bi, c), 0)),
                  pl.BlockSpec((L, mw), lambda bi, c, blk=blk: (blk(bi, c), 2)),
                  pl.BlockSpec((None, L, LANE), lambda bi, c, blk=blk, d=d: (d, blk(bi, c), 0))]
    return pl.pallas_call(
        functools.partial(_mlstm_scan_body, n_heads=n_heads),
        grid=(batch, nc),
        in_specs=specs,
        out_specs=[pl.BlockSpec((L, mw), lambda bi, c: (fwd(bi, c), 0)),
                   pl.BlockSpec((L, mw), lambda bi, c: (bwd(bi, c), 0))],
        out_shape=[jax.ShapeDtypeStruct((t, mw), F32)] * 2,
        scratch_shapes=[pltpu.VMEM((2, n_heads, LANE, LANE), F32), pltpu.VMEM((2, SUBLANE, LANE), F32)],
        compiler_params=_cparams(2),
        name="mlstm_scan",
    )(q, k, pm, gates, q, k, pm, gates)


def _layer_norm(x, g, b):
    mu = jnp.mean(x, axis=-1, keepdims=True)
    xc = x - mu
    var = jnp.mean(xc * xc, axis=-1, keepdims=True)
    return xc * lax.rsqrt(var + LN_EPS) * g + b


def _head_norm(x, bd_mean, eps):
    mu = _dot_exact_rhs(x, bd_mean)
    xc = x - mu
    var = _dot_exact_rhs(xc * xc, bd_mean)
    return xc * lax.rsqrt(var + eps)


def _mix_out_body(x_ref, yg_ref, ro0_ref, ro1_ref, bonus_ref, rgate_ref, rlg_ref, rlb_ref, mh0_ref, mh1_ref, og_ref,
                  mlg_ref, w_ref, l1g_ref, l1b_ref, rw_ref, rb_ref, bdm_ref,
                  x1_out, x1p_out, topi_out, gate_out, wb_ref, *, alpha, gw, rw):
    @pl.when(pl.program_id(0) == 0)
    def _():
        _cast_rows(w_ref, wb_ref)

    bdm = bdm_ref[...]
    yr = _head_norm(ro0_ref[...] + ro1_ref[...], bdm, RWKV_GN_EPS) * rlg_ref[...] + rlb_ref[...]
    yr = (yr + bonus_ref[...]) * rgate_ref[...]
    ym = _sigmoid(og_ref[...]) * (_head_norm(mh0_ref[...] + mh1_ref[...], bdm, LN_EPS) * mlg_ref[...])
    mix = (_dot(yg_ref[...].astype(BF16), wb_ref[:gw, :]) + _dot(yr.astype(BF16), wb_ref[gw:gw + rw, :])
           + _dot(ym.astype(BF16), wb_ref[gw + rw:, :]))
    x1 = _layer_norm(alpha * x_ref[...] + mix, l1g_ref[...], l1b_ref[...])
    x1_out[...] = x1
    x1p_out[...] = _pack_bf16_pairs(x1)
    lg = _mm(x1, rw_ref[...], "b3") + rb_ref[...]
    lane = lax.broadcasted_iota(jnp.int32, lg.shape, 1)
    vals, topi = [], jnp.zeros(lg.shape, jnp.int32)
    for j in range(TOP_K):
        mx = jnp.max(lg, axis=1, keepdims=True)
        idx = jnp.min(jnp.where(lg == mx, lane, LANE), axis=1, keepdims=True)
        vals.append(mx)
        topi = jnp.where(lane == j, idx, topi)
        lg = jnp.where(lane == idx, -jnp.inf, lg)
    es = [jnp.exp(vj - vals[0]) for vj in vals]
    den = es[0] + es[1] + es[2] + es[3]
    gate = jnp.zeros(lg.shape, F32)
    for j in range(TOP_K):
        gate = jnp.where(lane == j, es[j] / den, gate)
    topi_out[...] = topi.T[:SUBLANE, :]
    gate_out[...] = gate


def _mix_out(x, yg, ro, bonus, rgate, rlg, rlb, mh, pm, mlg, w_out, layer, l1g, l1b, router_w, router_b, alpha,
             tm=256):
    t, dm = x.shape
    gw, rw, mw = yg.shape[1], bonus.shape[1], mh[0].shape[1]
    assert rw == mw
    rwp = jnp.zeros((dm, LANE), F32).at[:, :N_EXPERTS].set(router_w)
    rbp = jnp.full((1, LANE), NEG_BIG, F32).at[0, :N_EXPERTS].set(router_b)
    row = lambda n: pl.BlockSpec((tm, n), lambda i: (i, 0))
    vec = lambda n: _full((1, n))
    return pl.pallas_call(
        functools.partial(_mix_out_body, alpha=alpha, gw=gw, rw=rw),
        grid=(t // tm,),
        in_specs=[row(dm), row(gw), row(rw), row(rw), row(rw), row(rw), vec(rw), vec(rw), row(mw), row(mw),
                  pl.BlockSpec((tm, mw), lambda i: (i, 3)),
                  vec(mw),
                  pl.BlockSpec((None, dm, dm), lambda i: (layer, 0, 0), pipeline_mode=pl.Buffered(1)),
                  vec(dm), vec(dm), _full((dm, LANE)), vec(LANE), _full((rw, rw))],
        out_specs=[row(dm), row(dm // 2), pl.BlockSpec((SUBLANE, tm), lambda i: (0, i)), row(LANE)],
        out_shape=[jax.ShapeDtypeStruct((t, dm), F32), jax.ShapeDtypeStruct((t, dm // 2), jnp.uint32),
                   jax.ShapeDtypeStruct((SUBLANE, t), jnp.int32), jax.ShapeDtypeStruct((t, LANE), F32)],
        scratch_shapes=[pltpu.VMEM((dm, dm), BF16)],
        compiler_params=_cparams(1),
        name="mix_out",
    )(x, yg, ro[0], ro[1], bonus, rgate, rlg.reshape(1, rw), rlb.reshape(1, rw), mh[0], mh[1], pm,
      mlg.reshape(1, mw), w_out, l1g.reshape(1, dm), l1b.reshape(1, dm), rwp, rbp,
      (_block_diag_ones(rw) / HEAD_DIM).astype(BF16))


def _moe_body(be_ref, nu_ref, ve_ref, xs_ref, w1_ref, b1_ref, w2_ref, b2_ref, o_ref, w1b_ref, w2b_ref, *, dff):
    i = pl.program_id(0)
    active = i < nu_ref[0]
    new_expert = jnp.logical_or(i == 0, be_ref[i] != be_ref[jnp.maximum(i - 1, 0)])

    @pl.when(jnp.logical_and(active, new_expert))
    def _():
        _cast_rows(w1_ref, w1b_ref)
        _cast_rows(w2_ref, w2b_ref)

    @pl.when(active)
    def _():
        rowid = i * MOE_BLOCK + lax.broadcasted_iota(jnp.int32, (MOE_BLOCK, 1), 0)
        lo, hi = _unpack_bf16_pairs(jnp.where(rowid < ve_ref[i], xs_ref[...], jnp.uint32(0)))
        xs = jnp.concatenate([lo.astype(BF16), hi.astype(BF16)], axis=1)
        hdn = _dot(xs, w1b_ref[...]) + b1_ref[...]
        g_ = jnp.minimum(hdn[:, :dff], SWIGLU_LIMIT)
        u_ = jnp.clip(hdn[:, dff:], -SWIGLU_LIMIT, SWIGLU_LIMIT)
        act = (u_ + 1.0) * (g_ * _sigmoid(g_ * SWIGLU_ALPHA))
        o_ref[...] = _pack_bf16_pairs(_dot(act.astype(BF16), w2b_ref[...]) + b2_ref[...])

    @pl.when(jnp.logical_not(active))
    def _():
        o_ref[...] = jnp.zeros_like(o_ref)


def _moe_experts(xs, block_e, n_used, valid_end, w1, b1, w2, b2, layer):
    rows, half = xs.shape
    nb = rows // MOE_BLOCK
    depth, ne, dm, dff2 = w1.shape
    dff = dff2 // 2
    grid_spec = pltpu.PrefetchScalarGridSpec(
        num_scalar_prefetch=3,
        grid=(nb,),
        in_specs=[pl.BlockSpec((MOE_BLOCK, half), lambda i, be, nu, ve: (i, 0)),
                  pl.BlockSpec((None, None, dm, dff2), lambda i, be, nu, ve: (layer, be[i], 0, 0)),
                  pl.BlockSpec((None, None, 1, dff2), lambda i, be, nu, ve: (layer, be[i], 0, 0)),
                  pl.BlockSpec((None, None, dff, dm), lambda i, be, nu, ve: (layer, be[i], 0, 0)),
                  pl.BlockSpec((None, None, 1, dm), lambda i, be, nu, ve: (layer, be[i], 0, 0))],
        out_specs=pl.BlockSpec((MOE_BLOCK, half), lambda i, be, nu, ve: (i, 0)),
        scratch_shapes=[pltpu.VMEM((dm, dff2), BF16), pltpu.VMEM((dff, dm), BF16)],
    )
    return pl.pallas_call(
        functools.partial(_moe_body, dff=dff),
        grid_spec=grid_spec,
        out_shape=jax.ShapeDtypeStruct((rows, half), jnp.uint32),
        compiler_params=_cparams(1),
        name="moe_experts",
    )(block_e, n_used, valid_end, xs, w1, b1.reshape(depth, ne, 1, dff2), w2, b2.reshape(depth, ne, 1, dm))


N_STREAMS = 1
PLAN_TILE = 512
MOE_BLOCK_SHIFT = 8


def _moe_plan_body(e_ref, dest_ref, meta_ref, rank_ref, *, n_tokens, meta_lanes):
    tiles_per_row = n_tokens // PLAN_TILE
    n_tiles = TOP_K * tiles_per_row
    expert = lax.broadcasted_iota(jnp.int32, (N_EXPERTS, PLAN_TILE), 0)
    r_i = lax.broadcasted_iota(jnp.int32, (PLAN_TILE, PLAN_TILE), 0)
    c_i = lax.broadcasted_iota(jnp.int32, (PLAN_TILE, PLAN_TILE), 1)
    earlier = (r_i < c_i).astype(BF16)

    def tile_hits(it):
        j = it // tiles_per_row
        lanes = pl.ds(pl.multiple_of((it % tiles_per_row) * PLAN_TILE, PLAN_TILE), PLAN_TILE)
        return j, lanes, e_ref[pl.ds(j, 1), lanes] == expert

    def rank_step(it, seen):
        j, lanes, hit = tile_hits(it)
        hitf = hit.astype(F32)
        prior = _dot(hit.astype(BF16), earlier) + seen
        rank_ref[pl.ds(j, 1), lanes] = jnp.sum(hitf * prior, axis=0, keepdims=True)
        return seen + jnp.sum(hitf, axis=1, keepdims=True)

    dest_ref[...] = jnp.zeros_like(dest_ref)
    rank_ref[...] = jnp.zeros_like(rank_ref)
    counts = lax.fori_loop(0, n_tiles, rank_step, jnp.zeros((N_EXPERTS, 1), F32))
    padded = ((counts.astype(jnp.int32) + (MOE_BLOCK - 1)) >> MOE_BLOCK_SHIFT) << MOE_BLOCK_SHIFT
    er = lax.broadcasted_iota(jnp.int32, (N_EXPERTS, N_EXPERTS), 0)
    ec = lax.broadcasted_iota(jnp.int32, (N_EXPERTS, N_EXPERTS), 1)
    seg_end = _dot_exact_lhs((ec <= er).astype(BF16),
                             jnp.broadcast_to(padded.astype(F32), (N_EXPERTS, LANE)))[:, 0:1]
    seg_start = seg_end - padded.astype(F32)

    def dest_step(it, carry):
        j, lanes, hit = tile_hits(it)
        base = jnp.sum(jnp.where(hit, seg_start, 0.0), axis=0, keepdims=True)
        dest_ref[pl.ds(j, 1), lanes] = (rank_ref[pl.ds(j, 1), lanes] + base).astype(jnp.int32)
        return carry

    lax.fori_loop(0, n_tiles, dest_step, 0)
    blk_start = (lax.broadcasted_iota(jnp.int32, (N_EXPERTS, meta_lanes), 1) * MOE_BLOCK).astype(F32)
    blk_expert = jnp.minimum(jnp.sum((seg_end <= blk_start).astype(F32), axis=0, keepdims=True), N_EXPERTS - 1.0)
    mine = lax.broadcasted_iota(jnp.int32, (N_EXPERTS, meta_lanes), 0).astype(F32) == blk_expert
    valid_end = jnp.sum(jnp.where(mine, seg_start + counts, 0.0), axis=0, keepdims=True)
    n_used = jnp.broadcast_to(seg_end[N_EXPERTS - 1:N_EXPERTS, :] * (1.0 / MOE_BLOCK), (1, meta_lanes))
    mrow = lax.broadcasted_iota(jnp.int32, (SUBLANE, meta_lanes), 0)
    meta = jnp.where(mrow == 0, blk_expert, jnp.where(mrow == 1, valid_end, jnp.where(mrow == 2, n_used, 0.0)))
    meta_ref[...] = meta.astype(jnp.int32)


def _moe_plan(e_t, n_tokens, n_blocks):
    meta_lanes = -(-n_blocks // LANE) * LANE
    dest, meta = pl.pallas_call(
        functools.partial(_moe_plan_body, n_tokens=n_tokens, meta_lanes=meta_lanes),
        grid=(1,),
        in_specs=[_full((SUBLANE, n_tokens))],
        out_specs=[_full((SUBLANE, n_tokens)), _full((SUBLANE, meta_lanes))],
        out_shape=[jax.ShapeDtypeStruct((SUBLANE, n_tokens), jnp.int32),
                   jax.ShapeDtypeStruct((SUBLANE, meta_lanes), jnp.int32)],
        scratch_shapes=[pltpu.VMEM((SUBLANE, n_tokens), F32)],
        compiler_params=_cparams(1),
        name="moe_plan",
    )(e_t)
    return dest[:TOP_K], meta[0, :n_blocks], meta[1, :n_blocks], meta[2, :1]


def _combine_body(x1_ref, y0_ref, y1_ref, y2_ref, y3_ref, gate_ref, g_ref, b_ref, o_ref, *, alpha):
    gate = gate_ref[...]
    lo, hi = _unpack_bf16_pairs(y0_ref[...])
    lo, hi = gate[:, 0:1] * lo, gate[:, 0:1] * hi
    for j, y_ref in enumerate((y1_ref, y2_ref, y3_ref), start=1):
        lo_j, hi_j = _unpack_bf16_pairs(y_ref[...])
        lo, hi = lo + gate[:, j:j + 1] * lo_j, hi + gate[:, j:j + 1] * hi_j
    ffn = jnp.concatenate([lo, hi], axis=1)
    o_ref[...] = _layer_norm(alpha * x1_ref[...] + ffn, g_ref[...], b_ref[...])


def _combine(x1, yg, gate, ln_g, ln_b, alpha, tm=256):
    t, dm = x1.shape
    n_tiles = t // tm
    expert_rows = lambda j: pl.BlockSpec((tm, dm // 2), lambda i: (i + j * n_tiles, 0))
    return pl.pallas_call(
        functools.partial(_combine_body, alpha=alpha),
        grid=(n_tiles,),
        in_specs=[pl.BlockSpec((tm, dm), lambda i: (i, 0))] + [expert_rows(j) for j in range(TOP_K)]
                 + [pl.BlockSpec((tm, LANE), lambda i: (i, 0)), _full((1, dm)), _full((1, dm))],
        out_specs=pl.BlockSpec((tm, dm), lambda i: (i, 0)),
        out_shape=jax.ShapeDtypeStruct((t, dm), F32),
        compiler_params=_cparams(1),
        name="combine_ln",
    )(x1, yg, yg, yg, yg, gate, ln_g.reshape(1, dm), ln_b.reshape(1, dm))


SC_CORES = 2
SC_SUBCORES = 16
SC_WORKERS = SC_CORES * SC_SUBCORES


def _sc_gather_rows(table, idx, window):
    n = idx.shape[0]
    dim = table.shape[1]
    n_steps = n // (SC_WORKERS * window)
    assert n_steps * window * SC_WORKERS == n and n_steps % 2 == 0 and window % SUBLANE == 0 and window <= LANE
    idx3 = idx.reshape(SC_WORKERS, n_steps, window)
    mesh = plsc.VectorSubcoreMesh(core_axis_name="c", subcore_axis_name="s",
                                  num_cores=SC_CORES, num_subcores=SC_SUBCORES)

    def body(table_hbm, idx_hbm, out_hbm, idx_v, rows_v, gsem, wsem):
        wid = lax.axis_index("s") * SC_CORES + lax.axis_index("c")
        pltpu.sync_copy(idx_hbm.at[wid], idx_v)

        def gather(j, buf):
            return pltpu.make_async_copy(table_hbm.at[idx_v.at[j]], rows_v.at[buf], gsem.at[buf])

        def write(j, buf):
            base = pl.multiple_of((wid * n_steps + j) * window, window)
            return pltpu.make_async_copy(rows_v.at[buf], out_hbm.at[pl.ds(base, window)], wsem.at[buf])

        gather(0, 0).start()

        @pl.loop(0, n_steps, step=2)
        def _(j0):
            for buf in range(2):
                j = j0 + buf
                gather(j, buf).wait()

                @pl.when(j >= 1)
                def _():
                    write(j - 1, 1 - buf).wait()

                @pl.when(j + 1 < n_steps)
                def _():
                    gather(j + 1, 1 - buf).start()

                write(j, buf).start()

        write(n_steps - 1, 1).wait()

    return pl.kernel(
        body, out_type=jax.ShapeDtypeStruct((n, dim), table.dtype), mesh=mesh,
        scratch_types=[pltpu.VMEM((n_steps, window), jnp.int32), pltpu.VMEM((2, window, dim), table.dtype),
                       pltpu.SemaphoreType.DMA((2,)), pltpu.SemaphoreType.DMA((2,))],
        name="sc_gather",
    )(table, idx3)


def _sc_scatter_rows(src, dest, n_out, window):
    t, dim = src.shape
    k = dest.shape[0]
    n_steps = t // (SC_WORKERS * window)
    assert n_steps * window * SC_WORKERS == t and n_steps % 2 == 0 and window % SUBLANE == 0 and window <= LANE
    idx3 = dest.reshape(k, SC_WORKERS, n_steps, window).transpose(1, 2, 0, 3).reshape(SC_WORKERS, n_steps * k, window)
    mesh = plsc.VectorSubcoreMesh(core_axis_name="c", subcore_axis_name="s",
                                  num_cores=SC_CORES, num_subcores=SC_SUBCORES)

    def body(src_hbm, idx_hbm, out_hbm, idx_v, rows_v, rsem, ssem):
        wid = lax.axis_index("s") * SC_CORES + lax.axis_index("c")
        pltpu.sync_copy(idx_hbm.at[wid], idx_v)

        def read(s, buf):
            base = pl.multiple_of((wid * n_steps + s) * window, window)
            return pltpu.make_async_copy(src_hbm.at[pl.ds(base, window)], rows_v.at[buf], rsem.at[buf])

        def scatter(s, j, buf):
            return pltpu.make_async_copy(rows_v.at[buf], out_hbm.at[idx_v.at[s * k + j]], ssem.at[buf])

        read(0, 0).start()

        @pl.loop(0, n_steps, step=2)
        def _(s0):
            for buf in range(2):
                s = s0 + buf
                read(s, buf).wait()

                @pl.when(s >= 1)
                def _():
                    for j in range(k):
                        scatter(s - 1, j, 1 - buf).wait()

                @pl.when(s + 1 < n_steps)
                def _():
                    read(s + 1, 1 - buf).start()

                for j in range(k):
                    scatter(s, j, buf).start()

        for j in range(k):
            scatter(n_steps - 1, j, 1).wait()

    return pl.kernel(
        body, out_type=jax.ShapeDtypeStruct((n_out, dim), src.dtype), mesh=mesh,
        scratch_types=[pltpu.VMEM((n_steps * k, window), jnp.int32), pltpu.VMEM((2, window, dim), src.dtype),
                       pltpu.SemaphoreType.DMA((2,)), pltpu.SemaphoreType.DMA((2,))],
        name="sc_scatter",
    )(src, idx3)


def _pad_cols(w, width):
    return jnp.pad(w, ((0, 0), (0, width - w.shape[1])))


def kernel(x, w_in, gmlp_ln_g, gmlp_ln_b, gmlp_ws, gmlp_bs, rwkv_mu, rwkv_w0, rwkv_w2, rwkv_a0, rwkv_a2, rwkv_g2, rwkv_k_k, rwkv_k_a, rwkv_r_k, rwkv_ln_g, rwkv_ln_b, mlstm_conv_w, mlstm_conv_b, mlstm_gate_b, mlstm_ln_g, w_out, ln1_g, ln1_b, router_w, router_b, exp_w1, exp_b1, exp_w2, exp_b2, ln2_g, ln2_b):
    batch, seq, dm = x.shape
    depth = w_in.shape[0]
    sb = batch // N_STREAMS if batch % N_STREAMS == 0 else batch
    t = sb * seq
    gw = gmlp_ln_g.shape[1]
    rw = rwkv_w0.shape[2]
    mw = mlstm_ln_g.shape[1]
    g_proj = 2 * gw
    r_proj = 3 * rw + W_LORA + A_LORA + G_LORA
    alpha = (2 * depth) ** 0.25
    n_blocks = -(-t * TOP_K // MOE_BLOCK) + N_EXPERTS
    streams = [x[i * sb:(i + 1) * sb].reshape(t, dm) for i in range(batch // sb)]
    for l in range(depth):
        mixed = []
        for xf in streams:
            pg, pr, pm = _proj(xf, w_in, l, g_proj, r_proj)
            y_g = _gmlp(pg, gmlp_ln_g[l], gmlp_ln_b[l], gmlp_ws[l], gmlp_bs[l])
            r, v, a, kd, b, lw, bonus, rgate = _rwkv_prep(
                pr, seq, rwkv_mu[l], rwkv_w0[l], rwkv_w2[l], rwkv_a0[l], rwkv_a2[l], rwkv_g2[l],
                rwkv_k_k[l], rwkv_k_a[l], rwkv_r_k[l].reshape(-1))
            ro = _rwkv_scan(r, v, a, kd, b, lw, sb, seq)
            q, k, gates = _mlstm_prep(pm, seq, mlstm_conv_w[l], mlstm_conv_b[l], mlstm_gate_b[l], mw)
            mh = _mlstm_scan(q, k, pm, gates, sb, seq)
            mixed.append(_mix_out(xf, y_g, ro, bonus, rgate, rwkv_ln_g[l], rwkv_ln_b[l], mh, pm, mlstm_ln_g[l],
                                  w_out, l, ln1_g[l], ln1_b[l], router_w[l], router_b[l], alpha))
        streams = []
        for x1, x1p, topi, gate in mixed:
            dest, block_e, valid_end, n_used = _moe_plan(topi, t, n_blocks)
            xs = _sc_scatter_rows(x1p, dest, n_blocks * MOE_BLOCK, window=64)
            ys = _moe_experts(xs, block_e, n_used, valid_end, exp_w1, exp_b1, exp_w2, exp_b2, l)
            yg = _sc_gather_rows(ys, dest.reshape(-1), window=64)
            streams.append(_combine(x1, yg, gate, ln2_g[l], ln2_b[l], alpha))
    return jnp.concatenate(streams, axis=0).reshape(batch, seq, dm)
```

```python
import functools
import math

import jax
import jax.numpy as jnp
from jax import lax
from jax.experimental import pallas as pl
from jax.experimental.pallas import tpu as pltpu
from jax.experimental.pallas import tpu_sc as plsc

F32 = jnp.float32
BF16 = jnp.bfloat16
HI = lax.Precision.HIGHEST

HEAD_DIM = 64
GMLP_CHUNK = 128
MLSTM_CHUNK = 128
RWKV_CHUNK = 64
W_LORA = 64
A_LORA = 64
G_LORA = 128
N_EXPERTS = 32
TOP_K = 4
MOE_BLOCK = 256
SWIGLU_LIMIT = 7.0
SWIGLU_ALPHA = 1.702
LN_EPS = 1e-5
RWKV_GN_EPS = 64e-5
LANE = 128
SUBLANE = 8
VMEM_LIMIT = 48 * 1024 * 1024
NEG_BIG = -1e30


def _cparams(n_axes):
    return pltpu.CompilerParams(dimension_semantics=("arbitrary",) * n_axes,
                                vmem_limit_bytes=VMEM_LIMIT)


def _full(shape):
    return pl.BlockSpec(shape, lambda *_: (0,) * len(shape))


def _dot(a, b, precision=None):
    return jnp.dot(a, b, preferred_element_type=F32, precision=precision)


def _dot_nt(a, b, precision=None):
    return lax.dot_general(a, b, (((1,), (1,)), ((), ())), preferred_element_type=F32, precision=precision)


def _dot_tn(a, b, precision=None):
    return lax.dot_general(a, b, (((0,), (0,)), ((), ())), preferred_element_type=F32, precision=precision)


def _split(x):
    hi = x.astype(BF16)
    return hi, (x - hi.astype(F32)).astype(BF16)


def _split3(x):
    hi = x.astype(BF16)
    r1 = x - hi.astype(F32)
    mid = r1.astype(BF16)
    return hi, mid, (r1 - mid.astype(F32)).astype(BF16)


def _mm(a, b, mode, dot=_dot):
    if mode == "hi":
        return dot(a, b, HI)
    if mode == "b1":
        return dot(a.astype(BF16), b.astype(BF16))
    bh, bl = _split(b)
    if mode == "b2":
        ah = a.astype(BF16)
        return dot(ah, bh) + dot(ah, bl)
    ah, al = _split(a)
    return dot(ah, bh) + (dot(ah, bl) + dot(al, bh))


def _dot_exact_lhs(a_bf16, x):
    hi, mid, lo = _split3(x)
    return _dot(a_bf16, hi) + (_dot(a_bf16, mid) + _dot(a_bf16, lo))


def _dot_exact_rhs(x, b_bf16, terms=3):
    if terms == 2:
        hi, lo = _split(x)
        return _dot(hi, b_bf16) + _dot(lo, b_bf16)
    hi, mid, lo = _split3(x)
    return _dot(hi, b_bf16) + (_dot(mid, b_bf16) + _dot(lo, b_bf16))


def _pack_bf16_pairs(x):
    n = x.shape[1] // 2
    lo = pltpu.bitcast(x[:, :n].astype(BF16).astype(F32), jnp.uint32)
    hi = pltpu.bitcast(x[:, n:].astype(BF16).astype(F32), jnp.uint32)
    return hi | (lo >> 16)


def _unpack_bf16_pairs(w):
    lo = pltpu.bitcast(w << 16, F32)
    hi = pltpu.bitcast(w & jnp.uint32(0xFFFF0000), F32)
    return lo, hi


def _sigmoid(x):
    return 1.0 / (1.0 + jnp.exp(-x))


def _softplus(x):
    return jnp.maximum(x, 0.0) + jnp.log1p(jnp.exp(-jnp.abs(x)))


def _block_diag_ones(width):
    h = jnp.arange(width) // HEAD_DIM
    return (h[:, None] == h[None, :]).astype(F32)


CAST_ROWS = 128


def _cast_rows(src_ref, dst_ref):
    n_src, n_dst = src_ref.shape[1], dst_ref.shape[1]
    whole = n_src // LANE * LANE

    def step(r, carry):
        rows = pl.ds(pl.multiple_of(r * CAST_ROWS, CAST_ROWS), CAST_ROWS)
        dst_ref[rows, :whole] = src_ref[rows, :whole].astype(BF16)
        if n_dst > whole:
            tail = [src_ref[rows, whole:]] if n_src > whole else []
            tail.append(jnp.zeros((CAST_ROWS, n_dst - n_src), F32))
            dst_ref[rows, whole:] = jnp.concatenate(tail, axis=1).astype(BF16)
        return carry
    lax.fori_loop(0, src_ref.shape[0] // CAST_ROWS, step, 0)


def _proj_body(x_ref, w_ref, pg_ref, pr_ref, pm_ref, wb_ref, *, ng, nr):
    @pl.when(pl.program_id(0) == 0)
    def _():
        _cast_rows(w_ref, wb_ref)

    xb = x_ref[...].astype(BF16)
    pg_ref[...] = _dot(xb, wb_ref[:, :ng])
    pr_ref[...] = _dot(xb, wb_ref[:, ng:ng + nr])
    pm_ref[...] = _dot(xb, wb_ref[:, ng + nr:])


def _proj(x, w_in, layer, ng, nr, tm=256):
    t, d = x.shape
    p_in = w_in.shape[2]
    p_pad = -(-p_in // LANE) * LANE
    nm = p_pad - ng - nr
    row = lambda n: pl.BlockSpec((tm, n), lambda i: (i, 0))
    return pl.pallas_call(
        functools.partial(_proj_body, ng=ng, nr=nr),
        grid=(t // tm,),
        in_specs=[row(d), pl.BlockSpec((None, d, p_in), lambda i: (layer, 0, 0), pipeline_mode=pl.Buffered(1))],
        out_specs=[row(ng), row(nr), row(nm)],
        out_shape=[jax.ShapeDtypeStruct((t, n), F32) for n in (ng, nr, nm)],
        scratch_shapes=[pltpu.VMEM((d, p_pad), BF16)],
        compiler_params=_cparams(1),
        name="in_proj",
    )(x, w_in)


def _gmlp_body(pg_ref, lng_ref, lnb_ref, ws_ref, bst_ref, o_ref, *, gw, chunks):
    p = pg_ref[...]
    p = 0.5 * p * (1.0 + lax.erf(p * math.sqrt(0.5)))
    u, v = p[:, :gw], p[:, gw:]
    mu = jnp.mean(v, axis=-1, keepdims=True)
    vc = v - mu
    var = jnp.mean(vc * vc, axis=-1, keepdims=True)
    vn = vc * lax.rsqrt(var + LN_EPS) * lng_ref[...] + lnb_ref[...]
    n_heads = gw // HEAD_DIM
    for c in range(chunks):
        rows = slice(c * GMLP_CHUNK, (c + 1) * GMLP_CHUNK)
        ys = []
        for h in range(n_heads):
            cols = slice(h * HEAD_DIM, (h + 1) * HEAD_DIM)
            y = _dot(ws_ref[h], vn[rows, cols].astype(BF16)) + bst_ref[:, h:h + 1]
            ys.append(y)
        o_ref[rows, :] = u[rows, :] * jnp.concatenate(ys, axis=1)


def _gmlp(pg, ln_g, ln_b, ws, bs, chunks=4):
    t = pg.shape[0]
    gw = pg.shape[1] // 2
    n_heads = gw // HEAD_DIM
    tm = chunks * GMLP_CHUNK
    bst = jnp.zeros((GMLP_CHUNK, LANE), F32).at[:, :n_heads].set(bs.T)
    return pl.pallas_call(
        functools.partial(_gmlp_body, gw=gw, chunks=chunks),
        grid=(t // tm,),
        in_specs=[pl.BlockSpec((tm, 2 * gw), lambda i: (i, 0)), _full((1, gw)), _full((1, gw)),
                  _full((n_heads, GMLP_CHUNK, GMLP_CHUNK)), _full((GMLP_CHUNK, LANE))],
        out_specs=pl.BlockSpec((tm, gw), lambda i: (i, 0)),
        out_shape=jax.ShapeDtypeStruct((t, gw), F32),
        compiler_params=_cparams(1),
        name="gmlp",
    )(pg, ln_g.reshape(1, gw), ln_b.reshape(1, gw), ws.astype(BF16), bst)


def _halo_specs(tm, width, n_rows):
    per8 = tm // SUBLANE
    last = n_rows // SUBLANE - 1
    prev = pl.BlockSpec((SUBLANE, width), lambda i: (jnp.maximum(i * per8 - 1, 0), 0))
    nxt = pl.BlockSpec((SUBLANE, width), lambda i: (jnp.minimum((i + 1) * per8, last), 0))
    return prev, nxt


def _neighbours(cur, prev_blk, next_blk, tiles_per_seq):
    tm = cur.shape[0]
    j = pl.program_id(0) % tiles_per_seq
    prev_row = jnp.where(j > 0, prev_blk[SUBLANE - 1:SUBLANE, :], 0.0)
    next_row = jnp.where(j < tiles_per_seq - 1, next_blk[0:1, :], 0.0)
    ridx = lax.broadcasted_iota(jnp.int32, cur.shape, 0)
    before = jnp.where(ridx == 0, prev_row, pltpu.roll(cur, 1, 0))
    after = jnp.where(ridx == tm - 1, next_row, pltpu.roll(cur, tm - 1, 0))
    return before, after


def _rwkv_prep_body(pr_ref, prev_ref, next_ref, mu_ref, w0_ref, w2_ref, a0_ref, a2_ref, g2_ref,
                    kk_ref, ka_ref, rk_ref, bd_ref,
                    r_out, v_out, a_out, kd_out, b_out, lw_out, bonus_out, gate_out, *, rw, tiles_per_seq):
    pf = pr_ref[...]
    before, after = _neighbours(pf, prev_ref[...], next_ref[...], tiles_per_seq)
    pf = pf + mu_ref[0:1, :] * (before - pf) + mu_ref[1:2, :] * (after - pf)
    o3 = 3 * rw
    r, k, v = pf[:, :rw], pf[:, rw:2 * rw], pf[:, 2 * rw:o3]
    wd = pf[:, o3:o3 + W_LORA]
    ad = pf[:, o3 + W_LORA:o3 + W_LORA + A_LORA]
    gd = pf[:, o3 + W_LORA + A_LORA:]
    bd = bd_ref[...]
    kk = k * kk_ref[...]
    ss = _dot_exact_rhs(kk * kk, bd, terms=2)
    kk = kk / jnp.maximum(jnp.sqrt(ss), 1e-12)
    twd = jnp.tanh(wd)
    ksum = jnp.zeros_like(k)
    for d in range(2):
        w_log = -_softplus(-(w0_ref[d:d + 1, :] + _mm(twd, w2_ref[d], "b3"))) - 0.5
        lw_out[d] = -jnp.exp(w_log)
        iclr = _sigmoid(a0_ref[d:d + 1, :] + _mm(ad, a2_ref[d], "b3"))
        kd = k * (1.0 + (iclr - 1.0) * ka_ref[...])
        kd_out[d] = kd
        b_out[d] = kk * iclr
        ksum = ksum + kd
    r_out[...] = r
    v_out[...] = v
    a_out[...] = -kk
    bonus_out[...] = _dot_exact_rhs(r * ksum * rk_ref[...], bd, terms=2) * v
    gate_out[...] = _dot(_sigmoid(gd).astype(BF16), g2_ref[...])


def _rwkv_prep(pr, seq, mu, w0, w2, a0, a2, g2, k_k, k_a, r_k, tm=256):
    t, rproj = pr.shape
    rw = w0.shape[1]
    tiles_per_seq = seq // tm
    prev, nxt = _halo_specs(tm, rproj, t)
    row = pl.BlockSpec((tm, rw), lambda i: (i, 0))
    row2 = pl.BlockSpec((2, tm, rw), lambda i: (0, i, 0))
    one = jax.ShapeDtypeStruct((t, rw), F32)
    two = jax.ShapeDtypeStruct((2, t, rw), F32)
    return pl.pallas_call(
        functools.partial(_rwkv_prep_body, rw=rw, tiles_per_seq=tiles_per_seq),
        grid=(t // tm,),
        in_specs=[pl.BlockSpec((tm, rproj), lambda i: (i, 0)), prev, nxt,
                  _full((2, rproj)), _full((2, rw)), _full((2, W_LORA, rw)), _full((2, rw)),
                  _full((2, A_LORA, rw)), _full((G_LORA, rw)), _full((1, rw)), _full((1, rw)),
                  _full((1, rw)), _full((rw, rw))],
        out_specs=[row, row, row, row2, row2, row2, row, row],
        out_shape=[one, one, one, two, two, two, one, one],
        compiler_params=_cparams(1),
        name="rwkv_prep",
    )(pr, pr, pr, mu, w0, w2, a0, a2, g2.astype(BF16), k_k.reshape(1, rw), k_a.reshape(1, rw),
      r_k.reshape(1, rw), _block_diag_ones(rw).astype(BF16))


P_G, P_INV, P_APPLY, P_STATE, P_SEQ = "b1", "b1", "b1", "b1", "b2"


def _rwkv_intra_body(r_ref, v_ref, a_ref, kd_ref, b_ref, lw_ref, rq_out, o0_out, mtx_out, hc_out,
                     *, n_heads, chunks):
    L = RWKV_CHUNK
    d = pl.program_id(0)
    row = lax.broadcasted_iota(jnp.int32, (L, L), 0)
    col = lax.broadcasted_iota(jnp.int32, (L, L), 1)
    fwd = d == 0
    rel = (col - row) * (1 - 2 * d)
    incl = rel <= 0
    strict = rel < 0
    eye = (row == col).astype(F32)
    tri = incl.astype(BF16)
    pairs = []
    for c in range(chunks):
        rows = slice(c * L, (c + 1) * L)
        lw = lw_ref[rows, :]
        cum = _dot_exact_lhs(tri, lw)
        tot = jnp.where(fwd, cum[L - 1:L, :], cum[0:1, :])
        e_neg = jnp.exp(-cum)
        e_end = jnp.exp(tot - cum)
        e_tot = jnp.exp(tot)
        r, v, a, kd, b = r_ref[rows, :], v_ref[rows, :], a_ref[rows, :], kd_ref[rows, :], b_ref[rows, :]
        at, rt, bt, kt = a * jnp.exp(cum - lw), r * jnp.exp(cum), b * e_neg, kd * e_neg
        kend, bend = kd * e_end, b * e_end
        for h in range(n_heads):
            sl = slice(h * HEAD_DIM, (h + 1) * HEAD_DIM)
            pairs.append(dict(at=at[:, sl], rt=rt[:, sl], bt=bt[:, sl], kt=kt[:, sl], v=v[:, sl],
                              kend=kend[:, sl], bend=bend[:, sl], e_tot=e_tot[:, sl]))
    for p in pairs:
        p["g"] = _mm(jnp.concatenate([p["at"], p["rt"]], axis=0),
                     jnp.concatenate([p["bt"], p["kt"]], axis=0), P_G, _dot_nt)
    row2 = lax.broadcasted_iota(jnp.int32, (L, 2 * L), 0)
    col2 = lax.broadcasted_iota(jnp.int32, (L, 2 * L), 1) & (L - 1)
    rel2 = (col2 - row2) * (1 - 2 * d)
    incl2 = rel2 <= 0
    strict2 = rel2 < 0
    zeros = jnp.zeros((L, HEAD_DIM), F32)
    for p in pairs:
        g = p.pop("g")
        a_both = jnp.where(strict2, g[:L, :], 0.0)
        p["m_both"] = jnp.where(incl2, g[L:, :], 0.0)
        p["pw"] = a_both[:, :L]
        p["a_ak"] = a_both[:, L:]
        p["inv"] = eye + p["pw"]
    for _ in range(int(math.log2(L)) - 1):
        for p in pairs:
            p["pw"] = _mm(p["pw"], p["pw"], P_INV)
        for p in pairs:
            p["inv"] = p["inv"] + _mm(p["inv"], p["pw"], P_INV)
    for p in pairs:
        p["akv"] = _mm(p["a_ak"], p["v"], P_APPLY)
    for p in pairs:
        wu = _mm(p["inv"], jnp.concatenate([p["at"], p["akv"]], axis=1), P_APPLY)
        p["rhs"] = jnp.concatenate([wu, jnp.concatenate([zeros, p["v"]], axis=1)], axis=0)
    for p in pairs:
        p["rq_o0"] = _mm(p["m_both"], p["rhs"], P_APPLY)
    for p in pairs:
        p["m_hc"] = _mm(jnp.concatenate([p["bend"], p["kend"]], axis=0), p["rhs"], P_STATE, _dot_tn)
    for c in range(chunks):
        ps = pairs[c * n_heads:(c + 1) * n_heads]
        rows = slice(c * L, (c + 1) * L)
        krows = slice(c * HEAD_DIM, (c + 1) * HEAD_DIM)
        rq_out[rows, :] = jnp.concatenate([p["rt"] + p["rq_o0"][:, :HEAD_DIM] for p in ps], axis=1)
        o0_out[rows, :] = jnp.concatenate([p["rq_o0"][:, HEAD_DIM:] for p in ps], axis=1)
        mtx_out[krows, :] = jnp.concatenate([eye * p["e_tot"] + p["m_hc"][:, :HEAD_DIM] for p in ps], axis=1)
        hc_out[krows, :] = jnp.concatenate([p["m_hc"][:, HEAD_DIM:] for p in ps], axis=1)


def _rwkv_intra(r, v, a, kd, b, lw, chunks=4):
    t, rw = r.shape
    n_heads = rw // HEAD_DIM
    tm = chunks * RWKV_CHUNK
    tk = chunks * HEAD_DIM
    n_tiles = t // tm
    one = pl.BlockSpec((tm, rw), lambda d, i: (i, 0))
    two = pl.BlockSpec((None, tm, rw), lambda d, i: (d, i, 0))
    twok = pl.BlockSpec((None, tk, rw), lambda d, i: (d, i, 0))
    return pl.pallas_call(
        functools.partial(_rwkv_intra_body, n_heads=n_heads, chunks=chunks),
        grid=(2, n_tiles),
        in_specs=[one, one, one, two, two, two],
        out_specs=[two, two, twok, twok],
        out_shape=[jax.ShapeDtypeStruct((2, t, rw), F32), jax.ShapeDtypeStruct((2, t, rw), F32),
                   jax.ShapeDtypeStruct((2, n_tiles * tk, rw), F32),
                   jax.ShapeDtypeStruct((2, n_tiles * tk, rw), F32)],
        compiler_params=_cparams(2),
        name="rwkv_intra",
    )(r, v, a, kd, b, lw)


def _rwkv_seq_body(rq0, o00, mtx0, hc0, rq1, o01, mtx1, hc1, out0, out1, h_ref, *, n_heads, batch):
    c = pl.program_id(0)

    @pl.when(c == 0)
    def _():
        h_ref[...] = jnp.zeros_like(h_ref)

    L = RWKV_CHUNK
    for d, (rq, o0, mtx, hc, out) in enumerate(((rq0, o00, mtx0, hc0, out0), (rq1, o01, mtx1, hc1, out1))):
        for bi in range(batch):
            rq_t, mtx_t = rq[bi], mtx[bi]
            state = h_ref[d, bi]
            outs, states = [], []
            for h in range(n_heads):
                sl = slice(h * HEAD_DIM, (h + 1) * HEAD_DIM)
                prod = _mm(jnp.concatenate([rq_t[:, sl], mtx_t[:, sl]], axis=0), state[:, sl], P_SEQ)
                outs.append(prod[:L])
                states.append(prod[L:])
            out[bi] = jnp.concatenate(outs, axis=1) + o0[bi]
            h_ref[d, bi] = jnp.concatenate(states, axis=1) + hc[bi]


def _rwkv_seq(rq, o0, mtx, hc, batch, seq):
    _, t, rw = rq.shape
    n_heads = rw // HEAD_DIM
    L = RWKV_CHUNK
    nc = seq // L
    as4 = lambda x: x.reshape(2, batch, x.shape[1] // batch, rw)
    rq, o0, mtx, hc = as4(rq), as4(o0), as4(mtx), as4(hc)
    fwd = lambda rows: pl.BlockSpec((None, batch, rows, rw), lambda c: (0, 0, c, 0))
    bwd = lambda rows: pl.BlockSpec((None, batch, rows, rw), lambda c: (1, 0, nc - 1 - c, 0))
    out0, out1 = pl.pallas_call(
        functools.partial(_rwkv_seq_body, n_heads=n_heads, batch=batch),
        grid=(nc,),
        in_specs=[fwd(L), fwd(L), fwd(HEAD_DIM), fwd(HEAD_DIM), bwd(L), bwd(L), bwd(HEAD_DIM), bwd(HEAD_DIM)],
        out_specs=[pl.BlockSpec((batch, L, rw), lambda c: (0, c, 0)),
                   pl.BlockSpec((batch, L, rw), lambda c: (0, nc - 1 - c, 0))],
        out_shape=[jax.ShapeDtypeStruct((batch, seq, rw), F32)] * 2,
        scratch_shapes=[pltpu.VMEM((2, batch, HEAD_DIM, rw), F32)],
        compiler_params=_cparams(1),
        name="rwkv_seq",
    )(rq, o0, mtx, hc, rq, o0, mtx, hc)
    return out0.reshape(t, rw), out1.reshape(t, rw)


def _rwkv_scan(r, v, a, kd, b, lw, batch, seq):
    rq, o0, mtx, hc = _rwkv_intra(r, v, a, kd, b, lw)
    return _rwkv_seq(rq, o0, mtx, hc, batch, seq)


def _mlstm_prep_body(qk_ref, prev_ref, next_ref, g_ref, cw_ref, cb_ref, gb_ref, q_out, k_out, gate_out,
                     *, mw, n_heads, tiles_per_seq):
    x = qk_ref[...]
    before, after = _neighbours(x, prev_ref[...], next_ref[...], tiles_per_seq)
    y = cb_ref[...] + before * cw_ref[0:1, :] + x * cw_ref[1:2, :] + after * cw_ref[2:3, :]
    y = y * _sigmoid(y)
    q_out[...] = y[:, :mw]
    k_out[...] = y[:, mw:] * (HEAD_DIM ** -0.5)
    g = g_ref[...] + gb_ref[...]
    lane = lax.broadcasted_iota(jnp.int32, g.shape, 1)
    for d in range(2):
        ig = g if d == 0 else pltpu.roll(g, LANE - n_heads, 1)
        fg = pltpu.roll(g, LANE - (1 + d) * n_heads, 1)
        lf = -_softplus(-fg)
        gate_out[d] = jnp.where(lane < n_heads, ig, jnp.where(lane < 2 * n_heads, lf, 0.0))


def _mlstm_prep(pm, seq, conv_w, conv_b, gate_b, mw, tm=256):
    t = pm.shape[0]
    n_heads = mw // HEAD_DIM
    tiles_per_seq = seq // tm
    w2 = 2 * mw
    prev, nxt = _halo_specs(tm, w2, t)
    gcol = (4 * mw) // LANE
    gb = jnp.zeros((1, LANE), F32).at[0, :4 * n_heads].set(gate_b)
    row = pl.BlockSpec((tm, mw), lambda i: (i, 0))
    return pl.pallas_call(
        functools.partial(_mlstm_prep_body, mw=mw, n_heads=n_heads, tiles_per_seq=tiles_per_seq),
        grid=(t // tm,),
        in_specs=[pl.BlockSpec((tm, w2), lambda i: (i, 0)), prev, nxt,
                  pl.BlockSpec((tm, LANE), lambda i: (i, gcol)),
                  _full((3, w2)), _full((1, w2)), _full((1, LANE))],
        out_specs=[row, row, pl.BlockSpec((2, tm, LANE), lambda i: (0, i, 0))],
        out_shape=[jax.ShapeDtypeStruct((t, mw), F32), jax.ShapeDtypeStruct((t, mw), F32),
                   jax.ShapeDtypeStruct((2, t, LANE), F32)],
        compiler_params=_cparams(1),
        name="mlstm_prep",
    )(pm, pm, pm, pm, conv_w, conv_b.reshape(1, w2), gb)


def _mlstm_scan_body(q0_ref, k0_ref, v0_ref, g0_ref, q1_ref, k1_ref, v1_ref, g1_ref, o0_ref, o1_ref,
                     c_ref, m_ref, *, n_heads):
    L = MLSTM_CHUNK
    H = n_heads

    @pl.when(pl.program_id(1) == 0)
    def _():
        c_ref[...] = jnp.zeros_like(c_ref)
        m_ref[...] = jnp.zeros_like(m_ref)

    row = lax.broadcasted_iota(jnp.int32, (L, L), 0)
    col = lax.broadcasted_iota(jnp.int32, (L, L), 1)
    trow = lax.broadcasted_iota(jnp.int32, (L, LANE), 0)
    low = lax.broadcasted_iota(jnp.int32, (L, LANE), 1) < HEAD_DIM
    xr = lax.broadcasted_iota(jnp.int32, (LANE, H * L), 0)
    xc = lax.broadcasted_iota(jnp.int32, (LANE, H * L), 1)
    spread = (xr - H == lax.shift_right_logical(xc, int(math.log2(L)))).astype(BF16)
    hs = []
    for d, (q_ref, k_ref, v_ref, g_ref) in enumerate(((q0_ref, k0_ref, v0_ref, g0_ref),
                                                      (q1_ref, k1_ref, v1_ref, g1_ref))):
        incl = (col <= row) if d == 0 else (col >= row)
        last = L - 1 if d == 0 else 0
        g = g_ref[...]
        bcum = _dot_exact_lhs(incl.astype(BF16), g)
        z = pltpu.roll(g, H, 1) - bcum
        cmax = z
        shift = 1
        while shift < L:
            if d == 0:
                moved = jnp.where(trow >= shift, pltpu.roll(cmax, shift, 0), -jnp.inf)
            else:
                moved = jnp.where(trow < L - shift, pltpu.roll(cmax, L - shift, 0), -jnp.inf)
            cmax = jnp.maximum(cmax, moved)
            shift *= 2
        m_prev = m_ref[d, 0:1, :]
        top = jnp.maximum(cmax, m_prev)
        b_last = bcum[last:last + 1, :]
        lwc = b_last + z
        m_new = jnp.maximum(b_last + m_prev, jnp.max(lwc, axis=0, keepdims=True))
        m_ref[d, 0:1, :] = m_new
        per_row = jnp.concatenate(
            [-top, m_prev - top, bcum + top, jnp.exp(lwc - m_new),
             jnp.broadcast_to(jnp.exp(b_last + m_prev - m_new), (SUBLANE, LANE))], axis=0)
        wide = _dot_exact_rhs(per_row, spread)
        z_t = z.T
        q, k, v = q_ref[...], k_ref[...], v_ref[...]
        for h in range(H):
            slab = slice(h // 2 * LANE, (h // 2 + 1) * LANE)
            cols = slice(h * L, (h + 1) * L)
            mine = low if h % 2 == 0 else jnp.logical_not(low)
            kh = jnp.where(mine, k[:, slab], 0.0)
            hs.append(dict(
                qh=jnp.where(mine, q[:, slab], 0.0).astype(BF16), kh=kh.astype(BF16),
                vext=jnp.where(mine, v[:, slab], 1.0).astype(BF16),
                decay=jnp.exp(jnp.where(incl, wide[0:L, cols] + z_t[H + h:H + h + 1, :], -jnp.inf)),
                w_inter=jnp.exp(wide[L:2 * L, cols]), floor=jnp.exp(-wide[2 * L:3 * L, cols]),
                wk=(wide[3 * L:4 * L, cols] * kh).astype(BF16), dec=wide[4 * L:4 * L + 1, cols],
                cst=c_ref[d, h]))
    for p in hs:
        p["sc"] = (_dot_nt(p["qh"], p["kh"]) * p["decay"]).astype(BF16)
    for p in hs:
        p["numext"] = _dot(p["sc"], p["vext"]) + p["w_inter"] * _dot(p["qh"], p["cst"].astype(BF16))
    for p in hs:
        p["upd"] = _dot_tn(p["wk"], p["vext"])
    for d, o_ref in enumerate((o0_ref, o1_ref)):
        res = []
        for h in range(H):
            p = hs[d * H + h]
            den = pltpu.roll(p["numext"], HEAD_DIM, 1)
            res.append(p["numext"] / jnp.maximum(jnp.abs(den), p["floor"]))
            c_ref[d, h] = p["dec"] * p["cst"] + p["upd"]
        for pair in range(H // 2):
            o_ref[:, pair * LANE:(pair + 1) * LANE] = jnp.where(low, res[2 * pair], res[2 * pair + 1])


def _mlstm_scan(q, k, pm, gates, batch, seq):
    t, mw = q.shape
    n_heads = mw // HEAD_DIM
    L = MLSTM_CHUNK
    nc = seq // L
    fwd = lambda bi, c: bi * nc + c
    bwd = lambda bi, c: bi * nc + nc - 1 - c
    specs = []
    for d, blk in enumerate((fwd, bwd)):
        specs += [pl.BlockSpec((L, mw), lambda bi, c, blk=blk: (blk(bi, c), 0)),
                  pl.BlockSpec((L, mw), lambda bi, c, blk=blk: (blk(bi, c), 0)),
                  pl.BlockSpec((L, mw), lambda bi, c, blk=blk: (blk(bi, c), 2)),
                  pl.BlockSpec((None, L, LANE), lambda bi, c, blk=blk, d=d: (d, blk(bi, c), 0))]
    return pl.pallas_call(
        functools.partial(_mlstm_scan_body, n_heads=n_heads),
        grid=(batch, nc),
        in_specs=specs,
        out_specs=[pl.BlockSpec((L, mw), lambda bi, c: (fwd(bi, c), 0)),
                   pl.BlockSpec((L, mw), lambda bi, c: (bwd(bi, c), 0))],
        out_shape=[jax.ShapeDtypeStruct((t, mw), F32)] * 2,
        scratch_shapes=[pltpu.VMEM((2, n_heads, LANE, LANE), F32), pltpu.VMEM((2, SUBLANE, LANE), F32)],
        compiler_params=_cparams(2),
        name="mlstm_scan",
    )(q, k, pm, gates, q, k, pm, gates)


def _layer_norm(x, g, b):
    mu = jnp.mean(x, axis=-1, keepdims=True)
    xc = x - mu
    var = jnp.mean(xc * xc, axis=-1, keepdims=True)
    return xc * lax.rsqrt(var + LN_EPS) * g + b


def _head_norm(x, bd_mean, eps):
    mu = _dot_exact_rhs(x, bd_mean, terms=2)
    xc = x - mu
    var = _dot_exact_rhs(xc * xc, bd_mean, terms=2)
    return xc * lax.rsqrt(var + eps)


def _mix_out_body(x_ref, yg_ref, ro0_ref, ro1_ref, bonus_ref, rgate_ref, rlg_ref, rlb_ref, mh0_ref, mh1_ref, og_ref,
                  mlg_ref, w_ref, l1g_ref, l1b_ref, rw_ref, rb_ref, bdm_ref,
                  x1_out, x1p_out, topi_out, gate_out, wb_ref, *, alpha, gw, rw):
    @pl.when(pl.program_id(0) == 0)
    def _():
        _cast_rows(w_ref, wb_ref)

    bdm = bdm_ref[...]
    yr = _head_norm(ro0_ref[...] + ro1_ref[...], bdm, RWKV_GN_EPS) * rlg_ref[...] + rlb_ref[...]
    yr = (yr + bonus_ref[...]) * rgate_ref[...]
    ym = _sigmoid(og_ref[...]) * (_head_norm(mh0_ref[...] + mh1_ref[...], bdm, LN_EPS) * mlg_ref[...])
    mix = (_dot(yg_ref[...].astype(BF16), wb_ref[:gw, :]) + _dot(yr.astype(BF16), wb_ref[gw:gw + rw, :])
           + _dot(ym.astype(BF16), wb_ref[gw + rw:, :]))
    x1 = _layer_norm(alpha * x_ref[...] + mix, l1g_ref[...], l1b_ref[...])
    x1_out[...] = x1
    x1p_out[...] = _pack_bf16_pairs(x1)
    lg = _mm(x1, rw_ref[...], "b3") + rb_ref[...]
    lane = lax.broadcasted_iota(jnp.int32, lg.shape, 1)
    vals, topi = [], jnp.zeros(lg.shape, jnp.int32)
    for j in range(TOP_K):
        mx = jnp.max(lg, axis=1, keepdims=True)
        idx = jnp.min(jnp.where(lg == mx, lane, LANE), axis=1, keepdims=True)
        vals.append(mx)
        topi = jnp.where(lane == j, idx, topi)
        lg = jnp.where(lane == idx, -jnp.inf, lg)
    es = [jnp.exp(vj - vals[0]) for vj in vals]
    den = es[0] + es[1] + es[2] + es[3]
    gate = jnp.zeros(lg.shape, F32)
    for j in range(TOP_K):
        gate = jnp.where(lane == j, es[j] / den, gate)
    topi_out[...] = topi.T[:SUBLANE, :]
    gate_out[...] = gate


def _mix_out(x, yg, ro, bonus, rgate, rlg, rlb, mh, pm, mlg, w_out, layer, l1g, l1b, router_w, router_b, alpha,
             tm=256):
    t, dm = x.shape
    gw, rw, mw = yg.shape[1], bonus.shape[1], mh[0].shape[1]
    assert rw == mw
    rwp = jnp.zeros((dm, LANE), F32).at[:, :N_EXPERTS].set(router_w)
    rbp = jnp.full((1, LANE), NEG_BIG, F32).at[0, :N_EXPERTS].set(router_b)
    row = lambda n: pl.BlockSpec((tm, n), lambda i: (i, 0))
    vec = lambda n: _full((1, n))
    return pl.pallas_call(
        functools.partial(_mix_out_body, alpha=alpha, gw=gw, rw=rw),
        grid=(t // tm,),
        in_specs=[row(dm), row(gw), row(rw), row(rw), row(rw), row(rw), vec(rw), vec(rw), row(mw), row(mw),
                  pl.BlockSpec((tm, mw), lambda i: (i, 3)),
                  vec(mw),
                  pl.BlockSpec((None, dm, dm), lambda i: (layer, 0, 0), pipeline_mode=pl.Buffered(1)),
                  vec(dm), vec(dm), _full((dm, LANE)), vec(LANE), _full((rw, rw))],
        out_specs=[row(dm), row(dm // 2), pl.BlockSpec((SUBLANE, tm), lambda i: (0, i)), row(LANE)],
        out_shape=[jax.ShapeDtypeStruct((t, dm), F32), jax.ShapeDtypeStruct((t, dm // 2), jnp.uint32),
                   jax.ShapeDtypeStruct((SUBLANE, t), jnp.int32), jax.ShapeDtypeStruct((t, LANE), F32)],
        scratch_shapes=[pltpu.VMEM((dm, dm), BF16)],
        compiler_params=_cparams(1),
        name="mix_out",
    )(x, yg, ro[0], ro[1], bonus, rgate, rlg.reshape(1, rw), rlb.reshape(1, rw), mh[0], mh[1], pm,
      mlg.reshape(1, mw), w_out, l1g.reshape(1, dm), l1b.reshape(1, dm), rwp, rbp,
      (_block_diag_ones(rw) / HEAD_DIM).astype(BF16))


def _moe_body(be_ref, nu_ref, ve_ref, xs_ref, w1_ref, b1_ref, w2_ref, b2_ref, o_ref, w1b_ref, w2b_ref, *, dff):
    i = pl.program_id(0)
    active = i < nu_ref[0]
    new_expert = jnp.logical_or(i == 0, be_ref[i] != be_ref[jnp.maximum(i - 1, 0)])

    @pl.when(jnp.logical_and(active, new_expert))
    def _():
        _cast_rows(w1_ref, w1b_ref)
        _cast_rows(w2_ref, w2b_ref)

    @pl.when(active)
    def _():
        rowid = i * MOE_BLOCK + lax.broadcasted_iota(jnp.int32, (MOE_BLOCK, 1), 0)
        lo, hi = _unpack_bf16_pairs(jnp.where(rowid < ve_ref[i], xs_ref[...], jnp.uint32(0)))
        xs = jnp.concatenate([lo.astype(BF16), hi.astype(BF16)], axis=1)
        hdn = _dot(xs, w1b_ref[...]) + b1_ref[...]
        g_ = jnp.minimum(hdn[:, :dff], SWIGLU_LIMIT)
        u_ = jnp.clip(hdn[:, dff:], -SWIGLU_LIMIT, SWIGLU_LIMIT)
        act = (u_ + 1.0) * (g_ * _sigmoid(g_ * SWIGLU_ALPHA))
        o_ref[...] = _pack_bf16_pairs(_dot(act.astype(BF16), w2b_ref[...]) + b2_ref[...])

    @pl.when(jnp.logical_not(active))
    def _():
        o_ref[...] = jnp.zeros_like(o_ref)


def _moe_experts(xs, block_e, n_used, valid_end, w1, b1, w2, b2, layer):
    rows, half = xs.shape
    nb = rows // MOE_BLOCK
    depth, ne, dm, dff2 = w1.shape
    dff = dff2 // 2
    grid_spec = pltpu.PrefetchScalarGridSpec(
        num_scalar_prefetch=3,
        grid=(nb,),
        in_specs=[pl.BlockSpec((MOE_BLOCK, half), lambda i, be, nu, ve: (i, 0)),
                  pl.BlockSpec((None, None, dm, dff2), lambda i, be, nu, ve: (layer, be[i], 0, 0)),
                  pl.BlockSpec((None, None, 1, dff2), lambda i, be, nu, ve: (layer, be[i], 0, 0)),
                  pl.BlockSpec((None, None, dff, dm), lambda i, be, nu, ve: (layer, be[i], 0, 0)),
                  pl.BlockSpec((None, None, 1, dm), lambda i, be, nu, ve: (layer, be[i], 0, 0))],
        out_specs=pl.BlockSpec((MOE_BLOCK, half), lambda i, be, nu, ve: (i, 0)),
        scratch_shapes=[pltpu.VMEM((dm, dff2), BF16), pltpu.VMEM((dff, dm), BF16)],
    )
    return pl.pallas_call(
        functools.partial(_moe_body, dff=dff),
        grid_spec=grid_spec,
        out_shape=jax.ShapeDtypeStruct((rows, half), jnp.uint32),
        compiler_params=_cparams(1),
        name="moe_experts",
    )(block_e, n_used, valid_end, xs, w1, b1.reshape(depth, ne, 1, dff2), w2, b2.reshape(depth, ne, 1, dm))


N_STREAMS = 1
PLAN_TILE = 512
MOE_BLOCK_SHIFT = 8


def _moe_plan_body(e_ref, dest_ref, meta_ref, rank_ref, *, n_tokens, meta_lanes):
    tiles_per_row = n_tokens // PLAN_TILE
    n_tiles = TOP_K * tiles_per_row
    expert = lax.broadcasted_iota(jnp.int32, (N_EXPERTS, PLAN_TILE), 0)
    r_i = lax.broadcasted_iota(jnp.int32, (PLAN_TILE, PLAN_TILE), 0)
    c_i = lax.broadcasted_iota(jnp.int32, (PLAN_TILE, PLAN_TILE), 1)
    earlier = (r_i < c_i).astype(BF16)

    def tile_hits(it):
        j = it // tiles_per_row
        lanes = pl.ds(pl.multiple_of((it % tiles_per_row) * PLAN_TILE, PLAN_TILE), PLAN_TILE)
        return j, lanes, e_ref[pl.ds(j, 1), lanes] == expert

    def rank_step(it, seen):
        j, lanes, hit = tile_hits(it)
        hitf = hit.astype(F32)
        prior = _dot(hit.astype(BF16), earlier) + seen
        rank_ref[pl.ds(j, 1), lanes] = jnp.sum(hitf * prior, axis=0, keepdims=True)
        return seen + jnp.sum(hitf, axis=1, keepdims=True)

    dest_ref[...] = jnp.zeros_like(dest_ref)
    rank_ref[...] = jnp.zeros_like(rank_ref)
    counts = lax.fori_loop(0, n_tiles, rank_step, jnp.zeros((N_EXPERTS, 1), F32))
    padded = ((counts.astype(jnp.int32) + (MOE_BLOCK - 1)) >> MOE_BLOCK_SHIFT) << MOE_BLOCK_SHIFT
    er = lax.broadcasted_iota(jnp.int32, (N_EXPERTS, N_EXPERTS), 0)
    ec = lax.broadcasted_iota(jnp.int32, (N_EXPERTS, N_EXPERTS), 1)
    seg_end = _dot_exact_lhs((ec <= er).astype(BF16),
                             jnp.broadcast_to(padded.astype(F32), (N_EXPERTS, LANE)))[:, 0:1]
    seg_start = seg_end - padded.astype(F32)

    def dest_step(it, carry):
        j, lanes, hit = tile_hits(it)
        base = jnp.sum(jnp.where(hit, seg_start, 0.0), axis=0, keepdims=True)
        dest_ref[pl.ds(j, 1), lanes] = (rank_ref[pl.ds(j, 1), lanes] + base).astype(jnp.int32)
        return carry

    lax.fori_loop(0, n_tiles, dest_step, 0)
    blk_start = (lax.broadcasted_iota(jnp.int32, (N_EXPERTS, meta_lanes), 1) * MOE_BLOCK).astype(F32)
    blk_expert = jnp.minimum(jnp.sum((seg_end <= blk_start).astype(F32), axis=0, keepdims=True), N_EXPERTS - 1.0)
    mine = lax.broadcasted_iota(jnp.int32, (N_EXPERTS, meta_lanes), 0).astype(F32) == blk_expert
    valid_end = jnp.sum(jnp.where(mine, seg_start + counts, 0.0), axis=0, keepdims=True)
    n_used = jnp.broadcast_to(seg_end[N_EXPERTS - 1:N_EXPERTS, :] * (1.0 / MOE_BLOCK), (1, meta_lanes))
    mrow = lax.broadcasted_iota(jnp.int32, (SUBLANE, meta_lanes), 0)
    meta = jnp.where(mrow == 0, blk_expert, jnp.where(mrow == 1, valid_end, jnp.where(mrow == 2, n_used, 0.0)))
    meta_ref[...] = meta.astype(jnp.int32)


def _moe_plan(e_t, n_tokens, n_blocks):
    meta_lanes = -(-n_blocks // LANE) * LANE
    dest, meta = pl.pallas_call(
        functools.partial(_moe_plan_body, n_tokens=n_tokens, meta_lanes=meta_lanes),
        grid=(1,),
        in_specs=[_full((SUBLANE, n_tokens))],
        out_specs=[_full((SUBLANE, n_tokens)), _full((SUBLANE, meta_lanes))],
        out_shape=[jax.ShapeDtypeStruct((SUBLANE, n_tokens), jnp.int32),
                   jax.ShapeDtypeStruct((SUBLANE, meta_lanes), jnp.int32)],
        scratch_shapes=[pltpu.VMEM((SUBLANE, n_tokens), F32)],
        compiler_params=_cparams(1),
        name="moe_plan",
    )(e_t)
    return dest[:TOP_K], meta[0, :n_blocks], meta[1, :n_blocks], meta[2, :1]


def _combine_body(x1_ref, y0_ref, y1_ref, y2_ref, y3_ref, gate_ref, g_ref, b_ref, o_ref, *, alpha):
    gate = gate_ref[...]
    lo, hi = _unpack_bf16_pairs(y0_ref[...])
    lo, hi = gate[:, 0:1] * lo, gate[:, 0:1] * hi
    for j, y_ref in enumerate((y1_ref, y2_ref, y3_ref), start=1):
        lo_j, hi_j = _unpack_bf16_pairs(y_ref[...])
        lo, hi = lo + gate[:, j:j + 1] * lo_j, hi + gate[:, j:j + 1] * hi_j
    ffn = jnp.concatenate([lo, hi], axis=1)
    o_ref[...] = _layer_norm(alpha * x1_ref[...] + ffn, g_ref[...], b_ref[...])


def _combine(x1, yg, gate, ln_g, ln_b, alpha, tm=256):
    t, dm = x1.shape
    n_tiles = t // tm
    expert_rows = lambda j: pl.BlockSpec((tm, dm // 2), lambda i: (i + j * n_tiles, 0))
    return pl.pallas_call(
        functools.partial(_combine_body, alpha=alpha),
        grid=(n_tiles,),
        in_specs=[pl.BlockSpec((tm, dm), lambda i: (i, 0))] + [expert_rows(j) for j in range(TOP_K)]
                 + [pl.BlockSpec((tm, LANE), lambda i: (i, 0)), _full((1, dm)), _full((1, dm))],
        out_specs=pl.BlockSpec((tm, dm), lambda i: (i, 0)),
        out_shape=jax.ShapeDtypeStruct((t, dm), F32),
        compiler_params=_cparams(1),
        name="combine_ln",
    )(x1, yg, yg, yg, yg, gate, ln_g.reshape(1, dm), ln_b.reshape(1, dm))


SC_CORES = 2
SC_SUBCORES = 16
SC_WORKERS = SC_CORES * SC_SUBCORES


def _sc_gather_rows(table, idx, window):
    n = idx.shape[0]
    dim = table.shape[1]
    n_steps = n // (SC_WORKERS * window)
    assert n_steps * window * SC_WORKERS == n and n_steps % 2 == 0 and window % SUBLANE == 0 and window <= LANE
    idx3 = idx.reshape(SC_WORKERS, n_steps, window)
    mesh = plsc.VectorSubcoreMesh(core_axis_name="c", subcore_axis_name="s",
                                  num_cores=SC_CORES, num_subcores=SC_SUBCORES)

    def body(table_hbm, idx_hbm, out_hbm, idx_v, rows_v, gsem, wsem):
        wid = lax.axis_index("s") * SC_CORES + lax.axis_index("c")
        pltpu.sync_copy(idx_hbm.at[wid], idx_v)

        def gather(j, buf):
            return pltpu.make_async_copy(table_hbm.at[idx_v.at[j]], rows_v.at[buf], gsem.at[buf])

        def write(j, buf):
            base = pl.multiple_of((wid * n_steps + j) * window, window)
            return pltpu.make_async_copy(rows_v.at[buf], out_hbm.at[pl.ds(base, window)], wsem.at[buf])

        gather(0, 0).start()

        @pl.loop(0, n_steps, step=2)
        def _(j0):
            for buf in range(2):
                j = j0 + buf
                gather(j, buf).wait()

                @pl.when(j >= 1)
                def _():
                    write(j - 1, 1 - buf).wait()

                @pl.when(j + 1 < n_steps)
                def _():
                    gather(j + 1, 1 - buf).start()

                write(j, buf).start()

        write(n_steps - 1, 1).wait()

    return pl.kernel(
        body, out_type=jax.ShapeDtypeStruct((n, dim), table.dtype), mesh=mesh,
        scratch_types=[pltpu.VMEM((n_steps, window), jnp.int32), pltpu.VMEM((2, window, dim), table.dtype),
                       pltpu.SemaphoreType.DMA((2,)), pltpu.SemaphoreType.DMA((2,))],
        name="sc_gather",
    )(table, idx3)


def _sc_scatter_rows(src, dest, n_out, window):
    t, dim = src.shape
    k = dest.shape[0]
    n_steps = t // (SC_WORKERS * window)
    assert n_steps * window * SC_WORKERS == t and n_steps % 2 == 0 and window % SUBLANE == 0 and window <= LANE
    idx3 = dest.reshape(k, SC_WORKERS, n_steps, window).transpose(1, 2, 0, 3).reshape(SC_WORKERS, n_steps * k, window)
    mesh = plsc.VectorSubcoreMesh(core_axis_name="c", subcore_axis_name="s",
                                  num_cores=SC_CORES, num_subcores=SC_SUBCORES)

    def body(src_hbm, idx_hbm, out_hbm, idx_v, rows_v, rsem, ssem):
        wid = lax.axis_index("s") * SC_CORES + lax.axis_index("c")
        pltpu.sync_copy(idx_hbm.at[wid], idx_v)

        def read(s, buf):
            base = pl.multiple_of((wid * n_steps + s) * window, window)
            return pltpu.make_async_copy(src_hbm.at[pl.ds(base, window)], rows_v.at[buf], rsem.at[buf])

        def scatter(s, j, buf):
            return pltpu.make_async_copy(rows_v.at[buf], out_hbm.at[idx_v.at[s * k + j]], ssem.at[buf])

        read(0, 0).start()

        @pl.loop(0, n_steps, step=2)
        def _(s0):
            for buf in range(2):
                s = s0 + buf
                read(s, buf).wait()

                @pl.when(s >= 1)
                def _():
                    for j in range(k):
                        scatter(s - 1, j, 1 - buf).wait()

                @pl.when(s + 1 < n_steps)
                def _():
                    read(s + 1, 1 - buf).start()

                for j in range(k):
                    scatter(s, j, buf).start()

        for j in range(k):
            scatter(n_steps - 1, j, 1).wait()

    return pl.kernel(
        body, out_type=jax.ShapeDtypeStruct((n_out, dim), src.dtype), mesh=mesh,
        scratch_types=[pltpu.VMEM((n_steps * k, window), jnp.int32), pltpu.VMEM((2, window, dim), src.dtype),
                       pltpu.SemaphoreType.DMA((2,)), pltpu.SemaphoreType.DMA((2,))],
        name="sc_scatter",
    )(src, idx3)


def _pad_cols(w, width):
    return jnp.pad(w, ((0, 0), (0, width - w.shape[1])))


def kernel(x, w_in, gmlp_ln_g, gmlp_ln_b, gmlp_ws, gmlp_bs, rwkv_mu, rwkv_w0, rwkv_w2, rwkv_a0, rwkv_a2, rwkv_g2, rwkv_k_k, rwkv_k_a, rwkv_r_k, rwkv_ln_g, rwkv_ln_b, mlstm_conv_w, mlstm_conv_b, mlstm_gate_b, mlstm_ln_g, w_out, ln1_g, ln1_b, router_w, router_b, exp_w1, exp_b1, exp_w2, exp_b2, ln2_g, ln2_b):
    batch, seq, dm = x.shape
    depth = w_in.shape[0]
    sb = batch // N_STREAMS if batch % N_STREAMS == 0 else batch
    t = sb * seq
    gw = gmlp_ln_g.shape[1]
    rw = rwkv_w0.shape[2]
    mw = mlstm_ln_g.shape[1]
    g_proj = 2 * gw
    r_proj = 3 * rw + W_LORA + A_LORA + G_LORA
    alpha = (2 * depth) ** 0.25
    n_blocks = -(-t * TOP_K // MOE_BLOCK) + N_EXPERTS
    streams = [x[i * sb:(i + 1) * sb].reshape(t, dm) for i in range(batch // sb)]
    for l in range(depth):
        mixed = []
        for xf in streams:
            pg, pr, pm = _proj(xf, w_in, l, g_proj, r_proj)
            y_g = _gmlp(pg, gmlp_ln_g[l], gmlp_ln_b[l], gmlp_ws[l], gmlp_bs[l])
            r, v, a, kd, b, lw, bonus, rgate = _rwkv_prep(
                pr, seq, rwkv_mu[l], rwkv_w0[l], rwkv_w2[l], rwkv_a0[l], rwkv_a2[l], rwkv_g2[l],
                rwkv_k_k[l], rwkv_k_a[l], rwkv_r_k[l].reshape(-1))
            ro = _rwkv_scan(r, v, a, kd, b, lw, sb, seq)
            q, k, gates = _mlstm_prep(pm, seq, mlstm_conv_w[l], mlstm_conv_b[l], mlstm_gate_b[l], mw)
            mh = _mlstm_scan(q, k, pm, gates, sb, seq)
            mixed.append(_mix_out(xf, y_g, ro, bonus, rgate, rwkv_ln_g[l], rwkv_ln_b[l], mh, pm, mlstm_ln_g[l],
                                  w_out, l, ln1_g[l], ln1_b[l], router_w[l], router_b[l], alpha))
        streams = []
        for x1, x1p, topi, gate in mixed:
            dest, block_e, valid_end, n_used = _moe_plan(topi, t, n_blocks)
            xs = _sc_scatter_rows(x1p, dest, n_blocks * MOE_BLOCK, window=64)
            ys = _moe_experts(xs, block_e, n_used, valid_end, exp_w1, exp_b1, exp_w2, exp_b2, l)
            yg = _sc_gather_rows(ys, dest.reshape(-1), window=64)
            streams.append(_combine(x1, yg, gate, ln2_g[l], ln2_b[l], alpha))
    return jnp.concatenate(streams, axis=0).reshape(batch, seq, dm)
```

```python
import functools
import math

import jax
import jax.numpy as jnp
from jax import lax
from jax.experimental import pallas as pl
from jax.experimental.pallas import tpu as pltpu
from jax.experimental.pallas import tpu_sc as plsc

F32 = jnp.float32
BF16 = jnp.bfloat16
HI = lax.Precision.HIGHEST

HEAD_DIM = 64
GMLP_CHUNK = 128
MLSTM_CHUNK = 128
RWKV_CHUNK = 64
W_LORA = 64
A_LORA = 64
G_LORA = 128
N_EXPERTS = 32
TOP_K = 4
MOE_BLOCK = 256
SWIGLU_LIMIT = 7.0
SWIGLU_ALPHA = 1.702
LN_EPS = 1e-5
RWKV_GN_EPS = 64e-5
LANE = 128
SUBLANE = 8
VMEM_LIMIT = 48 * 1024 * 1024
NEG_BIG = -1e30


def _cparams(n_axes):
    return pltpu.CompilerParams(dimension_semantics=("arbitrary",) * n_axes,
                                vmem_limit_bytes=VMEM_LIMIT)


def _full(shape):
    return pl.BlockSpec(shape, lambda *_: (0,) * len(shape))


def _dot(a, b, precision=None):
    return jnp.dot(a, b, preferred_element_type=F32, precision=precision)


def _dot_nt(a, b, precision=None):
    return lax.dot_general(a, b, (((1,), (1,)), ((), ())), preferred_element_type=F32, precision=precision)


def _dot_tn(a, b, precision=None):
    return lax.dot_general(a, b, (((0,), (0,)), ((), ())), preferred_element_type=F32, precision=precision)


def _split(x):
    hi = x.astype(BF16)
    return hi, (x - hi.astype(F32)).astype(BF16)


def _split3(x):
    hi = x.astype(BF16)
    r1 = x - hi.astype(F32)
    mid = r1.astype(BF16)
    return hi, mid, (r1 - mid.astype(F32)).astype(BF16)


def _mm(a, b, mode, dot=_dot):
    if mode == "hi":
        return dot(a, b, HI)
    if mode == "b1":
        return dot(a.astype(BF16), b.astype(BF16))
    bh, bl = _split(b)
    if mode == "b2":
        ah = a.astype(BF16)
        return dot(ah, bh) + dot(ah, bl)
    ah, al = _split(a)
    return dot(ah, bh) + (dot(ah, bl) + dot(al, bh))


def _dot_exact_lhs(a_bf16, x):
    hi, mid, lo = _split3(x)
    return _dot(a_bf16, hi) + (_dot(a_bf16, mid) + _dot(a_bf16, lo))


def _dot_exact_rhs(x, b_bf16, terms=3):
    if terms == 2:
        hi, lo = _split(x)
        return _dot(hi, b_bf16) + _dot(lo, b_bf16)
    hi, mid, lo = _split3(x)
    return _dot(hi, b_bf16) + (_dot(mid, b_bf16) + _dot(lo, b_bf16))


def _pack_bf16_pairs(x):
    n = x.shape[1] // 2
    lo = pltpu.bitcast(x[:, :n].astype(BF16).astype(F32), jnp.uint32)
    hi = pltpu.bitcast(x[:, n:].astype(BF16).astype(F32), jnp.uint32)
    return hi | (lo >> 16)


def _unpack_bf16_pairs(w):
    lo = pltpu.bitcast(w << 16, F32)
    hi = pltpu.bitcast(w & jnp.uint32(0xFFFF0000), F32)
    return lo, hi


def _sigmoid(x):
    return 1.0 / (1.0 + jnp.exp(-x))


def _softplus(x):
    return jnp.maximum(x, 0.0) + jnp.log1p(jnp.exp(-jnp.abs(x)))


def _block_diag_ones(width):
    h = jnp.arange(width) // HEAD_DIM
    return (h[:, None] == h[None, :]).astype(F32)


CAST_ROWS = 128


def _cast_rows(src_ref, dst_ref):
    n_src, n_dst = src_ref.shape[1], dst_ref.shape[1]
    whole = n_src // LANE * LANE

    def step(r, carry):
        rows = pl.ds(pl.multiple_of(r * CAST_ROWS, CAST_ROWS), CAST_ROWS)
        dst_ref[rows, :whole] = src_ref[rows, :whole].astype(BF16)
        if n_dst > whole:
            tail = [src_ref[rows, whole:]] if n_src > whole else []
            tail.append(jnp.zeros((CAST_ROWS, n_dst - n_src), F32))
            dst_ref[rows, whole:] = jnp.concatenate(tail, axis=1).astype(BF16)
        return carry
    lax.fori_loop(0, src_ref.shape[0] // CAST_ROWS, step, 0)


def _proj_body(x_ref, w_ref, pg_ref, pr_ref, pm_ref, wb_ref, *, ng, nr):
    @pl.when(pl.program_id(0) == 0)
    def _():
        _cast_rows(w_ref, wb_ref)

    xb = x_ref[...].astype(BF16)
    pg_ref[...] = _dot(xb, wb_ref[:, :ng])
    pr_ref[...] = _dot(xb, wb_ref[:, ng:ng + nr])
    pm_ref[...] = _dot(xb, wb_ref[:, ng + nr:])


def _proj(x, w_in, layer, ng, nr, tm=256):
    t, d = x.shape
    p_in = w_in.shape[2]
    p_pad = -(-p_in // LANE) * LANE
    nm = p_pad - ng - nr
    row = lambda n: pl.BlockSpec((tm, n), lambda i: (i, 0))
    return pl.pallas_call(
        functools.partial(_proj_body, ng=ng, nr=nr),
        grid=(t // tm,),
        in_specs=[row(d), pl.BlockSpec((None, d, p_in), lambda i: (layer, 0, 0), pipeline_mode=pl.Buffered(1))],
        out_specs=[row(ng), row(nr), row(nm)],
        out_shape=[jax.ShapeDtypeStruct((t, n), F32) for n in (ng, nr, nm)],
        scratch_shapes=[pltpu.VMEM((d, p_pad), BF16)],
        compiler_params=_cparams(1),
        name="in_proj",
    )(x, w_in)


def _gmlp_body(pg_ref, lng_ref, lnb_ref, ws_ref, bst_ref, o_ref, *, gw, chunks):
    p = pg_ref[...]
    p = 0.5 * p * (1.0 + lax.erf(p * math.sqrt(0.5)))
    u, v = p[:, :gw], p[:, gw:]
    mu = jnp.mean(v, axis=-1, keepdims=True)
    vc = v - mu
    var = jnp.mean(vc * vc, axis=-1, keepdims=True)
    vn = vc * lax.rsqrt(var + LN_EPS) * lng_ref[...] + lnb_ref[...]
    n_heads = gw // HEAD_DIM
    for c in range(chunks):
        rows = slice(c * GMLP_CHUNK, (c + 1) * GMLP_CHUNK)
        ys = []
        for h in range(n_heads):
            cols = slice(h * HEAD_DIM, (h + 1) * HEAD_DIM)
            y = _dot(ws_ref[h], vn[rows, cols].astype(BF16)) + bst_ref[:, h:h + 1]
            ys.append(y)
        o_ref[rows, :] = u[rows, :] * jnp.concatenate(ys, axis=1)


def _gmlp(pg, ln_g, ln_b, ws, bs, chunks=4):
    t = pg.shape[0]
    gw = pg.shape[1] // 2
    n_heads = gw // HEAD_DIM
    tm = chunks * GMLP_CHUNK
    bst = jnp.zeros((GMLP_CHUNK, LANE), F32).at[:, :n_heads].set(bs.T)
    return pl.pallas_call(
        functools.partial(_gmlp_body, gw=gw, chunks=chunks),
        grid=(t // tm,),
        in_specs=[pl.BlockSpec((tm, 2 * gw), lambda i: (i, 0)), _full((1, gw)), _full((1, gw)),
                  _full((n_heads, GMLP_CHUNK, GMLP_CHUNK)), _full((GMLP_CHUNK, LANE))],
        out_specs=pl.BlockSpec((tm, gw), lambda i: (i, 0)),
        out_shape=jax.ShapeDtypeStruct((t, gw), F32),
        compiler_params=_cparams(1),
        name="gmlp",
    )(pg, ln_g.reshape(1, gw), ln_b.reshape(1, gw), ws.astype(BF16), bst)


def _halo_specs(tm, width, n_rows):
    per8 = tm // SUBLANE
    last = n_rows // SUBLANE - 1
    prev = pl.BlockSpec((SUBLANE, width), lambda i: (jnp.maximum(i * per8 - 1, 0), 0))
    nxt = pl.BlockSpec((SUBLANE, width), lambda i: (jnp.minimum((i + 1) * per8, last), 0))
    return prev, nxt


def _neighbours(cur, prev_blk, next_blk, tiles_per_seq):
    tm = cur.shape[0]
    j = pl.program_id(0) % tiles_per_seq
    prev_row = jnp.where(j > 0, prev_blk[SUBLANE - 1:SUBLANE, :], 0.0)
    next_row = jnp.where(j < tiles_per_seq - 1, next_blk[0:1, :], 0.0)
    ridx = lax.broadcasted_iota(jnp.int32, cur.shape, 0)
    before = jnp.where(ridx == 0, prev_row, pltpu.roll(cur, 1, 0))
    after = jnp.where(ridx == tm - 1, next_row, pltpu.roll(cur, tm - 1, 0))
    return before, after


def _rwkv_prep_body(pr_ref, prev_ref, next_ref, mu_ref, w0_ref, w2_ref, a0_ref, a2_ref, g2_ref,
                    kk_ref, ka_ref, rk_ref, bd_ref,
                    r_out, v_out, a_out, kd_out, b_out, lw_out, bonus_out, gate_out, *, rw, tiles_per_seq):
    pf = pr_ref[...]
    before, after = _neighbours(pf, prev_ref[...], next_ref[...], tiles_per_seq)
    pf = pf + mu_ref[0:1, :] * (before - pf) + mu_ref[1:2, :] * (after - pf)
    o3 = 3 * rw
    r, k, v = pf[:, :rw], pf[:, rw:2 * rw], pf[:, 2 * rw:o3]
    wd = pf[:, o3:o3 + W_LORA]
    ad = pf[:, o3 + W_LORA:o3 + W_LORA + A_LORA]
    gd = pf[:, o3 + W_LORA + A_LORA:]
    bd = bd_ref[...]
    kk = k * kk_ref[...]
    ss = _dot_exact_rhs(kk * kk, bd, terms=2)
    kk = kk / jnp.maximum(jnp.sqrt(ss), 1e-12)
    twd = jnp.tanh(wd)
    ksum = jnp.zeros_like(k)
    for d in range(2):
        w_log = -_softplus(-(w0_ref[d:d + 1, :] + _mm(twd, w2_ref[d], "b3"))) - 0.5
        lw_out[d] = -jnp.exp(w_log)
        iclr = _sigmoid(a0_ref[d:d + 1, :] + _mm(ad, a2_ref[d], "b3"))
        kd = k * (1.0 + (iclr - 1.0) * ka_ref[...])
        kd_out[d] = kd
        b_out[d] = kk * iclr
        ksum = ksum + kd
    r_out[...] = r
    v_out[...] = v
    a_out[...] = -kk
    bonus_out[...] = _dot_exact_rhs(r * ksum * rk_ref[...], bd, terms=2) * v
    gate_out[...] = _dot(_sigmoid(gd).astype(BF16), g2_ref[...])


def _rwkv_prep(pr, seq, mu, w0, w2, a0, a2, g2, k_k, k_a, r_k, tm=256):
    t, rproj = pr.shape
    rw = w0.shape[1]
    tiles_per_seq = seq // tm
    prev, nxt = _halo_specs(tm, rproj, t)
    row = pl.BlockSpec((tm, rw), lambda i: (i, 0))
    row2 = pl.BlockSpec((2, tm, rw), lambda i: (0, i, 0))
    one = jax.ShapeDtypeStruct((t, rw), F32)
    two = jax.ShapeDtypeStruct((2, t, rw), F32)
    return pl.pallas_call(
        functools.partial(_rwkv_prep_body, rw=rw, tiles_per_seq=tiles_per_seq),
        grid=(t // tm,),
        in_specs=[pl.BlockSpec((tm, rproj), lambda i: (i, 0)), prev, nxt,
                  _full((2, rproj)), _full((2, rw)), _full((2, W_LORA, rw)), _full((2, rw)),
                  _full((2, A_LORA, rw)), _full((G_LORA, rw)), _full((1, rw)), _full((1, rw)),
                  _full((1, rw)), _full((rw, rw))],
        out_specs=[row, row, row, row2, row2, row2, row, row],
        out_shape=[one, one, one, two, two, two, one, one],
        compiler_params=_cparams(1),
        name="rwkv_prep",
    )(pr, pr, pr, mu, w0, w2, a0, a2, g2.astype(BF16), k_k.reshape(1, rw), k_a.reshape(1, rw),
      r_k.reshape(1, rw), _block_diag_ones(rw).astype(BF16))


P_G, P_INV, P_APPLY, P_STATE, P_SEQ = "b1", "b1", "b1", "b1", "b2"


def _rwkv_intra_body(r_ref, v_ref, a_ref, kd_ref, b_ref, lw_ref, rq_out, o0_out, mtx_out, hc_out,
                     *, n_heads, chunks):
    L = RWKV_CHUNK
    d = pl.program_id(0)
    row = lax.broadcasted_iota(jnp.int32, (L, L), 0)
    col = lax.broadcasted_iota(jnp.int32, (L, L), 1)
    fwd = d == 0
    rel = (col - row) * (1 - 2 * d)
    incl = rel <= 0
    strict = rel < 0
    eye = (row == col).astype(F32)
    tri = incl.astype(BF16)
    pairs = []
    for c in range(chunks):
        rows = slice(c * L, (c + 1) * L)
        lw = lw_ref[rows, :]
        cum = _dot_exact_lhs(tri, lw)
        tot = jnp.where(fwd, cum[L - 1:L, :], cum[0:1, :])
        e_neg = jnp.exp(-cum)
        e_end = jnp.exp(tot - cum)
        e_tot = jnp.exp(tot)
        r, v, a, kd, b = r_ref[rows, :], v_ref[rows, :], a_ref[rows, :], kd_ref[rows, :], b_ref[rows, :]
        at, rt, bt, kt = a * jnp.exp(cum - lw), r * jnp.exp(cum), b * e_neg, kd * e_neg
        kend, bend = kd * e_end, b * e_end
        for h in range(n_heads):
            sl = slice(h * HEAD_DIM, (h + 1) * HEAD_DIM)
            pairs.append(dict(at=at[:, sl], rt=rt[:, sl], bt=bt[:, sl], kt=kt[:, sl], v=v[:, sl],
                              kend=kend[:, sl], bend=bend[:, sl], e_tot=e_tot[:, sl]))
    for p in pairs:
        p["g"] = _mm(jnp.concatenate([p["at"], p["rt"]], axis=0),
                     jnp.concatenate([p["bt"], p["kt"]], axis=0), P_G, _dot_nt)
    row2 = lax.broadcasted_iota(jnp.int32, (L, 2 * L), 0)
    col2 = lax.broadcasted_iota(jnp.int32, (L, 2 * L), 1) & (L - 1)
    rel2 = (col2 - row2) * (1 - 2 * d)
    incl2 = rel2 <= 0
    strict2 = rel2 < 0
    zeros = jnp.zeros((L, HEAD_DIM), F32)
    for p in pairs:
        g = p.pop("g")
        a_both = jnp.where(strict2, g[:L, :], 0.0)
        p["m_both"] = jnp.where(incl2, g[L:, :], 0.0)
        p["pw"] = a_both[:, :L]
        p["a_ak"] = a_both[:, L:]
        p["inv"] = eye + p["pw"]
    for _ in range(int(math.log2(L)) - 1):
        for p in pairs:
            p["pw"] = _mm(p["pw"], p["pw"], P_INV)
        for p in pairs:
            p["inv"] = p["inv"] + _mm(p["inv"], p["pw"], P_INV)
    for p in pairs:
        p["akv"] = _mm(p["a_ak"], p["v"], P_APPLY)
    for p in pairs:
        wu = _mm(p["inv"], jnp.concatenate([p["at"], p["akv"]], axis=1), P_APPLY)
        p["rhs"] = jnp.concatenate([wu, jnp.concatenate([zeros, p["v"]], axis=1)], axis=0)
    for p in pairs:
        p["rq_o0"] = _mm(p["m_both"], p["rhs"], P_APPLY)
    for p in pairs:
        p["m_hc"] = _mm(jnp.concatenate([p["bend"], p["kend"]], axis=0), p["rhs"], P_STATE, _dot_tn)
    for c in range(chunks):
        ps = pairs[c * n_heads:(c + 1) * n_heads]
        rows = slice(c * L, (c + 1) * L)
        krows = slice(c * HEAD_DIM, (c + 1) * HEAD_DIM)
        rq_out[rows, :] = jnp.concatenate([p["rt"] + p["rq_o0"][:, :HEAD_DIM] for p in ps], axis=1)
        o0_out[rows, :] = jnp.concatenate([p["rq_o0"][:, HEAD_DIM:] for p in ps], axis=1)
        mtx_out[krows, :] = jnp.concatenate([eye * p["e_tot"] + p["m_hc"][:, :HEAD_DIM] for p in ps], axis=1)
        hc_out[krows, :] = jnp.concatenate([p["m_hc"][:, HEAD_DIM:] for p in ps], axis=1)


def _rwkv_intra(r, v, a, kd, b, lw, chunks=4):
    t, rw = r.shape
    n_heads = rw // HEAD_DIM
    tm = chunks * RWKV_CHUNK
    tk = chunks * HEAD_DIM
    n_tiles = t // tm
    one = pl.BlockSpec((tm, rw), lambda d, i: (i, 0))
    two = pl.BlockSpec((None, tm, rw), lambda d, i: (d, i, 0))
    twok = pl.BlockSpec((None, tk, rw), lambda d, i: (d, i, 0))
    return pl.pallas_call(
        functools.partial(_rwkv_intra_body, n_heads=n_heads, chunks=chunks),
        grid=(2, n_tiles),
        in_specs=[one, one, one, two, two, two],
        out_specs=[two, two, twok, twok],
        out_shape=[jax.ShapeDtypeStruct((2, t, rw), F32), jax.ShapeDtypeStruct((2, t, rw), F32),
                   jax.ShapeDtypeStruct((2, n_tiles * tk, rw), F32),
                   jax.ShapeDtypeStruct((2, n_tiles * tk, rw), F32)],
        compiler_params=_cparams(2),
        name="rwkv_intra",
    )(r, v, a, kd, b, lw)


def _rwkv_seq_body(rq0, o00, mtx0, hc0, rq1, o01, mtx1, hc1, out0, out1, h_ref, *, n_heads, batch):
    c = pl.program_id(0)

    @pl.when(c == 0)
    def _():
        h_ref[...] = jnp.zeros_like(h_ref)

    L = RWKV_CHUNK
    for d, (rq, o0, mtx, hc, out) in enumerate(((rq0, o00, mtx0, hc0, out0), (rq1, o01, mtx1, hc1, out1))):
        for bi in range(batch):
            rq_t, mtx_t = rq[bi], mtx[bi]
            state = h_ref[d, bi]
            outs, states = [], []
            for h in range(n_heads):
                sl = slice(h * HEAD_DIM, (h + 1) * HEAD_DIM)
                prod = _mm(jnp.concatenate([rq_t[:, sl], mtx_t[:, sl]], axis=0), state[:, sl], P_SEQ)
                outs.append(prod[:L])
                states.append(prod[L:])
            out[bi] = jnp.concatenate(outs, axis=1) + o0[bi]
            h_ref[d, bi] = jnp.concatenate(states, axis=1) + hc[bi]


def _rwkv_seq(rq, o0, mtx, hc, batch, seq):
    _, t, rw = rq.shape
    n_heads = rw // HEAD_DIM
    L = RWKV_CHUNK
    nc = seq // L
    as4 = lambda x: x.reshape(2, batch, x.shape[1] // batch, rw)
    rq, o0, mtx, hc = as4(rq), as4(o0), as4(mtx), as4(hc)
    fwd = lambda rows: pl.BlockSpec((None, batch, rows, rw), lambda c: (0, 0, c, 0))
    bwd = lambda rows: pl.BlockSpec((None, batch, rows, rw), lambda c: (1, 0, nc - 1 - c, 0))
    out0, out1 = pl.pallas_call(
        functools.partial(_rwkv_seq_body, n_heads=n_heads, batch=batch),
        grid=(nc,),
        in_specs=[fwd(L), fwd(L), fwd(HEAD_DIM), fwd(HEAD_DIM), bwd(L), bwd(L), bwd(HEAD_DIM), bwd(HEAD_DIM)],
        out_specs=[pl.BlockSpec((batch, L, rw), lambda c: (0, c, 0)),
                   pl.BlockSpec((batch, L, rw), lambda c: (0, nc - 1 - c, 0))],
        out_shape=[jax.ShapeDtypeStruct((batch, seq, rw), F32)] * 2,
        scratch_shapes=[pltpu.VMEM((2, batch, HEAD_DIM, rw), F32)],
        compiler_params=_cparams(1),
        name="rwkv_seq",
    )(rq, o0, mtx, hc, rq, o0, mtx, hc)
    return out0.reshape(t, rw), out1.reshape(t, rw)


def _rwkv_scan(r, v, a, kd, b, lw, batch, seq):
    rq, o0, mtx, hc = _rwkv_intra(r, v, a, kd, b, lw)
    return _rwkv_seq(rq, o0, mtx, hc, batch, seq)


def _mlstm_prep_body(qk_ref, prev_ref, next_ref, g_ref, cw_ref, cb_ref, gb_ref, q_out, k_out, gate_out,
                     *, mw, n_heads, tiles_per_seq):
    x = qk_ref[...]
    before, after = _neighbours(x, prev_ref[...], next_ref[...], tiles_per_seq)
    y = cb_ref[...] + before * cw_ref[0:1, :] + x * cw_ref[1:2, :] + after * cw_ref[2:3, :]
    y = y * _sigmoid(y)
    q_out[...] = y[:, :mw]
    k_out[...] = y[:, mw:] * (HEAD_DIM ** -0.5)
    g = g_ref[...] + gb_ref[...]
    lane = lax.broadcasted_iota(jnp.int32, g.shape, 1)
    for d in range(2):
        ig = g if d == 0 else pltpu.roll(g, LANE - n_heads, 1)
        fg = pltpu.roll(g, LANE - (1 + d) * n_heads, 1)
        lf = -_softplus(-fg)
        gate_out[d] = jnp.where(lane < n_heads, ig, jnp.where(lane < 2 * n_heads, lf, 0.0))


def _mlstm_prep(pm, seq, conv_w, conv_b, gate_b, mw, tm=256):
    t = pm.shape[0]
    n_heads = mw // HEAD_DIM
    tiles_per_seq = seq // tm
    w2 = 2 * mw
    prev, nxt = _halo_specs(tm, w2, t)
    gcol = (4 * mw) // LANE
    gb = jnp.zeros((1, LANE), F32).at[0, :4 * n_heads].set(gate_b)
    row = pl.BlockSpec((tm, mw), lambda i: (i, 0))
    return pl.pallas_call(
        functools.partial(_mlstm_prep_body, mw=mw, n_heads=n_heads, tiles_per_seq=tiles_per_seq),
        grid=(t // tm,),
        in_specs=[pl.BlockSpec((tm, w2), lambda i: (i, 0)), prev, nxt,
                  pl.BlockSpec((tm, LANE), lambda i: (i, gcol)),
                  _full((3, w2)), _full((1, w2)), _full((1, LANE))],
        out_specs=[row, row, pl.BlockSpec((2, tm, LANE), lambda i: (0, i, 0))],
        out_shape=[jax.ShapeDtypeStruct((t, mw), F32), jax.ShapeDtypeStruct((t, mw), F32),
                   jax.ShapeDtypeStruct((2, t, LANE), F32)],
        compiler_params=_cparams(1),
        name="mlstm_prep",
    )(pm, pm, pm, pm, conv_w, conv_b.reshape(1, w2), gb)


def _mlstm_scan_body(q0_ref, k0_ref, v0_ref, g0_ref, q1_ref, k1_ref, v1_ref, g1_ref, o0_ref, o1_ref,
                     c_ref, m_ref, *, n_heads):
    L = MLSTM_CHUNK
    H = n_heads

    @pl.when(pl.program_id(1) == 0)
    def _():
        c_ref[...] = jnp.zeros_like(c_ref)
        m_ref[...] = jnp.zeros_like(m_ref)

    row = lax.broadcasted_iota(jnp.int32, (L, L), 0)
    col = lax.broadcasted_iota(jnp.int32, (L, L), 1)
    trow = lax.broadcasted_iota(jnp.int32, (L, LANE), 0)
    low = lax.broadcasted_iota(jnp.int32, (L, LANE), 1) < HEAD_DIM
    xr = lax.broadcasted_iota(jnp.int32, (LANE, H * L), 0)
    xc = lax.broadcasted_iota(jnp.int32, (LANE, H * L), 1)
    spread = (xr - H == lax.shift_right_logical(xc, int(math.log2(L)))).astype(BF16)
    hs = []
    for d, (q_ref, k_ref, v_ref, g_ref) in enumerate(((q0_ref, k0_ref, v0_ref, g0_ref),
                                                      (q1_ref, k1_ref, v1_ref, g1_ref))):
        incl = (col <= row) if d == 0 else (col >= row)
        last = L - 1 if d == 0 else 0
        g = g_ref[...]
        bcum = _dot_exact_lhs(incl.astype(BF16), g)
        z = pltpu.roll(g, H, 1) - bcum
        cmax = z
        shift = 1
        while shift < L:
            if d == 0:
                moved = jnp.where(trow >= shift, pltpu.roll(cmax, shift, 0), -jnp.inf)
            else:
                moved = jnp.where(trow < L - shift, pltpu.roll(cmax, L - shift, 0), -jnp.inf)
            cmax = jnp.maximum(cmax, moved)
            shift *= 2
        m_prev = m_ref[d, 0:1, :]
        top = jnp.maximum(cmax, m_prev)
        b_last = bcum[last:last + 1, :]
        lwc = b_last + z
        m_new = jnp.maximum(b_last + m_prev, jnp.max(lwc, axis=0, keepdims=True))
        m_ref[d, 0:1, :] = m_new
        per_row = jnp.concatenate(
            [-top, m_prev - top, bcum + top, jnp.exp(lwc - m_new),
             jnp.broadcast_to(jnp.exp(b_last + m_prev - m_new), (SUBLANE, LANE))], axis=0)
        wide = _dot_exact_rhs(per_row, spread)
        z_t = z.T
        q, k, v = q_ref[...], k_ref[...], v_ref[...]
        for h in range(H):
            slab = slice(h // 2 * LANE, (h // 2 + 1) * LANE)
            cols = slice(h * L, (h + 1) * L)
            mine = low if h % 2 == 0 else jnp.logical_not(low)
            kh = jnp.where(mine, k[:, slab], 0.0)
            hs.append(dict(
                qh=jnp.where(mine, q[:, slab], 0.0).astype(BF16), kh=kh.astype(BF16),
                vext=jnp.where(mine, v[:, slab], 1.0).astype(BF16),
                decay=jnp.exp(jnp.where(incl, wide[0:L, cols] + z_t[H + h:H + h + 1, :], -jnp.inf)),
                w_inter=jnp.exp(wide[L:2 * L, cols]), floor=jnp.exp(-wide[2 * L:3 * L, cols]),
                wk=(wide[3 * L:4 * L, cols] * kh).astype(BF16), dec=wide[4 * L:4 * L + 1, cols],
                cst=c_ref[d, h]))
    for p in hs:
        p["sc"] = (_dot_nt(p["qh"], p["kh"]) * p["decay"]).astype(BF16)
    for p in hs:
        p["numext"] = _dot(p["sc"], p["vext"]) + p["w_inter"] * _dot(p["qh"], p["cst"].astype(BF16))
    for p in hs:
        p["upd"] = _dot_tn(p["wk"], p["vext"])
    for d, o_ref in enumerate((o0_ref, o1_ref)):
        res = []
        for h in range(H):
            p = hs[d * H + h]
            den = pltpu.roll(p["numext"], HEAD_DIM, 1)
            res.append(p["numext"] / jnp.maximum(jnp.abs(den), p["floor"]))
            c_ref[d, h] = p["dec"] * p["cst"] + p["upd"]
        for pair in range(H // 2):
            o_ref[:, pair * LANE:(pair + 1) * LANE] = jnp.where(low, res[2 * pair], res[2 * pair + 1])


def _mlstm_scan(q, k, pm, gates, batch, seq):
    t, mw = q.shape
    n_heads = mw // HEAD_DIM
    L = MLSTM_CHUNK
    nc = seq // L
    fwd = lambda bi, c: bi * nc + c
    bwd = lambda bi, c: bi * nc + nc - 1 - c
    specs = []
    for d, blk in enumerate((fwd, bwd)):
        specs += [pl.BlockSpec((L, mw), lambda bi, c, blk=blk: (blk(bi, c), 0)),
                  pl.BlockSpec((L, mw), lambda bi, c, blk=blk: (blk(bi, c), 0)),
                  pl.BlockSpec((L, mw), lambda bi, c, blk=blk: (blk(bi, c), 2)),
                  pl.BlockSpec((None, L, LANE), lambda bi, c, blk=blk, d=d: (d, blk(bi, c), 0))]
    return pl.pallas_call(
        functools.partial(_mlstm_scan_body, n_heads=n_heads),
        grid=(batch, nc),
        in_specs=specs,
        out_specs=[pl.BlockSpec((L, mw), lambda bi, c: (fwd(bi, c), 0)),
                   pl.BlockSpec((L, mw), lambda bi, c: (bwd(bi, c), 0))],
        out_shape=[jax.ShapeDtypeStruct((t, mw), F32)] * 2,
        scratch_shapes=[pltpu.VMEM((2, n_heads, LANE, LANE), F32), pltpu.VMEM((2, SUBLANE, LANE), F32)],
        compiler_params=_cparams(2),
        name="mlstm_scan",
    )(q, k, pm, gates, q, k, pm, gates)


def _layer_norm(x, g, b):
    mu = jnp.mean(x, axis=-1, keepdims=True)
    xc = x - mu
    var = jnp.mean(xc * xc, axis=-1, keepdims=True)
    return xc * lax.rsqrt(var + LN_EPS) * g + b


def _head_norm(x, bd_mean, eps):
    mu = _dot_exact_rhs(x, bd_mean, terms=2)
    xc = x - mu
    var = _dot_exact_rhs(xc * xc, bd_mean, terms=2)
    return xc * lax.rsqrt(var + eps)


def _mix_out_body(x_ref, yg_ref, ro0_ref, ro1_ref, bonus_ref, rgate_ref, rlg_ref, rlb_ref, mh0_ref, mh1_ref, og_ref,
                  mlg_ref, w_ref, l1g_ref, l1b_ref, rw_ref, rb_ref, bdm_ref,
                  x1_out, x1p_out, topi_out, gate_out, wb_ref, *, alpha, gw, rw):
    @pl.when(pl.program_id(0) == 0)
    def _():
        _cast_rows(w_ref, wb_ref)

    bdm = bdm_ref[...]
    yr = _head_norm(ro0_ref[...] + ro1_ref[...], bdm, RWKV_GN_EPS) * rlg_ref[...] + rlb_ref[...]
    yr = (yr + bonus_ref[...]) * rgate_ref[...]
    ym = _sigmoid(og_ref[...]) * (_head_norm(mh0_ref[...] + mh1_ref[...], bdm, LN_EPS) * mlg_ref[...])
    mix = (_dot(yg_ref[...].astype(BF16), wb_ref[:gw, :]) + _dot(yr.astype(BF16), wb_ref[gw:gw + rw, :])
           + _dot(ym.astype(BF16), wb_ref[gw + rw:, :]))
    x1 = _layer_norm(alpha * x_ref[...] + mix, l1g_ref[...], l1b_ref[...])
    x1_out[...] = x1
    x1p_out[...] = _pack_bf16_pairs(x1)
    lg = _mm(x1, rw_ref[...], "b3") + rb_ref[...]
    lane = lax.broadcasted_iota(jnp.int32, lg.shape, 1)
    vals, topi = [], jnp.zeros(lg.shape, jnp.int32)
    for j in range(TOP_K):
        mx = jnp.max(lg, axis=1, keepdims=True)
        idx = jnp.min(jnp.where(lg == mx, lane, LANE), axis=1, keepdims=True)
        vals.append(mx)
        topi = jnp.where(lane == j, idx, topi)
        lg = jnp.where(lane == idx, -jnp.inf, lg)
    es = [jnp.exp(vj - vals[0]) for vj in vals]
    den = es[0] + es[1] + es[2] + es[3]
    gate = jnp.zeros(lg.shape, F32)
    for j in range(TOP_K):
        gate = jnp.where(lane == j, es[j] / den, gate)
    topi_out[...] = topi.T[:SUBLANE, :]
    gate_out[...] = gate


def _mix_out(x, yg, ro, bonus, rgate, rlg, rlb, mh, pm, mlg, w_out, layer, l1g, l1b, router_w, router_b, alpha,
             tm=256):
    t, dm = x.shape
    gw, rw, mw = yg.shape[1], bonus.shape[1], mh[0].shape[1]
    assert rw == mw
    rwp = jnp.zeros((dm, LANE), F32).at[:, :N_EXPERTS].set(router_w)
    rbp = jnp.full((1, LANE), NEG_BIG, F32).at[0, :N_EXPERTS].set(router_b)
    row = lambda n: pl.BlockSpec((tm, n), lambda i: (i, 0))
    vec = lambda n: _full((1, n))
    return pl.pallas_call(
        functools.partial(_mix_out_body, alpha=alpha, gw=gw, rw=rw),
        grid=(t // tm,),
        in_specs=[row(dm), row(gw), row(rw), row(rw), row(rw), row(rw), vec(rw), vec(rw), row(mw), row(mw),
                  pl.BlockSpec((tm, mw), lambda i: (i, 3)),
                  vec(mw),
                  pl.BlockSpec((None, dm, dm), lambda i: (layer, 0, 0), pipeline_mode=pl.Buffered(1)),
                  vec(dm), vec(dm), _full((dm, LANE)), vec(LANE), _full((rw, rw))],
        out_specs=[row(dm), row(dm // 2), pl.BlockSpec((SUBLANE, tm), lambda i: (0, i)), row(LANE)],
        out_shape=[jax.ShapeDtypeStruct((t, dm), F32), jax.ShapeDtypeStruct((t, dm // 2), jnp.uint32),
                   jax.ShapeDtypeStruct((SUBLANE, t), jnp.int32), jax.ShapeDtypeStruct((t, LANE), F32)],
        scratch_shapes=[pltpu.VMEM((dm, dm), BF16)],
        compiler_params=_cparams(1),
        name="mix_out",
    )(x, yg, ro[0], ro[1], bonus, rgate, rlg.reshape(1, rw), rlb.reshape(1, rw), mh[0], mh[1], pm,
      mlg.reshape(1, mw), w_out, l1g.reshape(1, dm), l1b.reshape(1, dm), rwp, rbp,
      (_block_diag_ones(rw) / HEAD_DIM).astype(BF16))


def _moe_body(be_ref, nu_ref, ve_ref, xs_ref, w1_ref, b1_ref, w2_ref, b2_ref, o_ref, *, dff):
    i = pl.program_id(0)
    active = i < nu_ref[0]

    @pl.when(active)
    def _():
        rowid = i * MOE_BLOCK + lax.broadcasted_iota(jnp.int32, (MOE_BLOCK, 1), 0)
        lo, hi = _unpack_bf16_pairs(jnp.where(rowid < ve_ref[i], xs_ref[...], jnp.uint32(0)))
        xs = jnp.concatenate([lo.astype(BF16), hi.astype(BF16)], axis=1)
        hdn = _dot(xs, w1_ref[...].astype(BF16)) + b1_ref[...]
        g_ = jnp.minimum(hdn[:, :dff], SWIGLU_LIMIT)
        u_ = jnp.clip(hdn[:, dff:], -SWIGLU_LIMIT, SWIGLU_LIMIT)
        act = (u_ + 1.0) * (g_ * _sigmoid(g_ * SWIGLU_ALPHA))
        o_ref[...] = _pack_bf16_pairs(_dot(act.astype(BF16), w2_ref[...].astype(BF16)) + b2_ref[...])

    @pl.when(jnp.logical_not(active))
    def _():
        o_ref[...] = jnp.zeros_like(o_ref)


def _moe_experts(xs, block_e, n_used, valid_end, w1, b1, w2, b2, layer):
    rows, half = xs.shape
    nb = rows // MOE_BLOCK
    depth, ne, dm, dff2 = w1.shape
    dff = dff2 // 2
    grid_spec = pltpu.PrefetchScalarGridSpec(
        num_scalar_prefetch=3,
        grid=(nb,),
        in_specs=[pl.BlockSpec((MOE_BLOCK, half), lambda i, be, nu, ve: (i, 0)),
                  pl.BlockSpec((None, None, dm, dff2), lambda i, be, nu, ve: (layer, be[i], 0, 0)),
                  pl.BlockSpec((None, None, 1, dff2), lambda i, be, nu, ve: (layer, be[i], 0, 0)),
                  pl.BlockSpec((None, None, dff, dm), lambda i, be, nu, ve: (layer, be[i], 0, 0)),
                  pl.BlockSpec((None, None, 1, dm), lambda i, be, nu, ve: (layer, be[i], 0, 0))],
        out_specs=pl.BlockSpec((MOE_BLOCK, half), lambda i, be, nu, ve: (i, 0)),
    )
    return pl.pallas_call(
        functools.partial(_moe_body, dff=dff),
        grid_spec=grid_spec,
        out_shape=jax.ShapeDtypeStruct((rows, half), jnp.uint32),
        compiler_params=_cparams(1),
        name="moe_experts",
    )(block_e, n_used, valid_end, xs, w1, b1.reshape(depth, ne, 1, dff2), w2, b2.reshape(depth, ne, 1, dm))


N_STREAMS = 1
PLAN_TILE = 512
MOE_BLOCK_SHIFT = 8


def _moe_plan_body(e_ref, dest_ref, meta_ref, rank_ref, *, n_tokens, meta_lanes):
    tiles_per_row = n_tokens // PLAN_TILE
    n_tiles = TOP_K * tiles_per_row
    expert = lax.broadcasted_iota(jnp.int32, (N_EXPERTS, PLAN_TILE), 0)
    r_i = lax.broadcasted_iota(jnp.int32, (PLAN_TILE, PLAN_TILE), 0)
    c_i = lax.broadcasted_iota(jnp.int32, (PLAN_TILE, PLAN_TILE), 1)
    earlier = (r_i < c_i).astype(BF16)

    def tile_hits(it):
        j = it // tiles_per_row
        lanes = pl.ds(pl.multiple_of((it % tiles_per_row) * PLAN_TILE, PLAN_TILE), PLAN_TILE)
        return j, lanes, e_ref[pl.ds(j, 1), lanes] == expert

    def rank_step(it, seen):
        j, lanes, hit = tile_hits(it)
        hitf = hit.astype(F32)
        prior = _dot(hit.astype(BF16), earlier) + seen
        rank_ref[pl.ds(j, 1), lanes] = jnp.sum(hitf * prior, axis=0, keepdims=True)
        return seen + jnp.sum(hitf, axis=1, keepdims=True)

    dest_ref[...] = jnp.zeros_like(dest_ref)
    rank_ref[...] = jnp.zeros_like(rank_ref)
    counts = lax.fori_loop(0, n_tiles, rank_step, jnp.zeros((N_EXPERTS, 1), F32))
    padded = ((counts.astype(jnp.int32) + (MOE_BLOCK - 1)) >> MOE_BLOCK_SHIFT) << MOE_BLOCK_SHIFT
    er = lax.broadcasted_iota(jnp.int32, (N_EXPERTS, N_EXPERTS), 0)
    ec = lax.broadcasted_iota(jnp.int32, (N_EXPERTS, N_EXPERTS), 1)
    seg_end = _dot_exact_lhs((ec <= er).astype(BF16),
                             jnp.broadcast_to(padded.astype(F32), (N_EXPERTS, LANE)))[:, 0:1]
    seg_start = seg_end - padded.astype(F32)

    def dest_step(it, carry):
        j, lanes, hit = tile_hits(it)
        base = jnp.sum(jnp.where(hit, seg_start, 0.0), axis=0, keepdims=True)
        dest_ref[pl.ds(j, 1), lanes] = (rank_ref[pl.ds(j, 1), lanes] + base).astype(jnp.int32)
        return carry

    lax.fori_loop(0, n_tiles, dest_step, 0)
    blk_start = (lax.broadcasted_iota(jnp.int32, (N_EXPERTS, meta_lanes), 1) * MOE_BLOCK).astype(F32)
    blk_expert = jnp.minimum(jnp.sum((seg_end <= blk_start).astype(F32), axis=0, keepdims=True), N_EXPERTS - 1.0)
    mine = lax.broadcasted_iota(jnp.int32, (N_EXPERTS, meta_lanes), 0).astype(F32) == blk_expert
    valid_end = jnp.sum(jnp.where(mine, seg_start + counts, 0.0), axis=0, keepdims=True)
    n_used = jnp.broadcast_to(seg_end[N_EXPERTS - 1:N_EXPERTS, :] * (1.0 / MOE_BLOCK), (1, meta_lanes))
    mrow = lax.broadcasted_iota(jnp.int32, (SUBLANE, meta_lanes), 0)
    meta = jnp.where(mrow == 0, blk_expert, jnp.where(mrow == 1, valid_end, jnp.where(mrow == 2, n_used, 0.0)))
    meta_ref[...] = meta.astype(jnp.int32)


def _moe_plan(e_t, n_tokens, n_blocks):
    meta_lanes = -(-n_blocks // LANE) * LANE
    dest, meta = pl.pallas_call(
        functools.partial(_moe_plan_body, n_tokens=n_tokens, meta_lanes=meta_lanes),
        grid=(1,),
        in_specs=[_full((SUBLANE, n_tokens))],
        out_specs=[_full((SUBLANE, n_tokens)), _full((SUBLANE, meta_lanes))],
        out_shape=[jax.ShapeDtypeStruct((SUBLANE, n_tokens), jnp.int32),
                   jax.ShapeDtypeStruct((SUBLANE, meta_lanes), jnp.int32)],
        scratch_shapes=[pltpu.VMEM((SUBLANE, n_tokens), F32)],
        compiler_params=_cparams(1),
        name="moe_plan",
    )(e_t)
    return dest[:TOP_K], meta[0, :n_blocks], meta[1, :n_blocks], meta[2, :1]


def _combine_body(x1_ref, y0_ref, y1_ref, y2_ref, y3_ref, gate_ref, g_ref, b_ref, o_ref, *, alpha):
    gate = gate_ref[...]
    lo, hi = _unpack_bf16_pairs(y0_ref[...])
    lo, hi = gate[:, 0:1] * lo, gate[:, 0:1] * hi
    for j, y_ref in enumerate((y1_ref, y2_ref, y3_ref), start=1):
        lo_j, hi_j = _unpack_bf16_pairs(y_ref[...])
        lo, hi = lo + gate[:, j:j + 1] * lo_j, hi + gate[:, j:j + 1] * hi_j
    ffn = jnp.concatenate([lo, hi], axis=1)
    o_ref[...] = _layer_norm(alpha * x1_ref[...] + ffn, g_ref[...], b_ref[...])


def _combine(x1, yg, gate, ln_g, ln_b, alpha, tm=256):
    t, dm = x1.shape
    n_tiles = t // tm
    expert_rows = lambda j: pl.BlockSpec((tm, dm // 2), lambda i: (i + j * n_tiles, 0))
    return pl.pallas_call(
        functools.partial(_combine_body, alpha=alpha),
        grid=(n_tiles,),
        in_specs=[pl.BlockSpec((tm, dm), lambda i: (i, 0))] + [expert_rows(j) for j in range(TOP_K)]
                 + [pl.BlockSpec((tm, LANE), lambda i: (i, 0)), _full((1, dm)), _full((1, dm))],
        out_specs=pl.BlockSpec((tm, dm), lambda i: (i, 0)),
        out_shape=jax.ShapeDtypeStruct((t, dm), F32),
        compiler_params=_cparams(1),
        name="combine_ln",
    )(x1, yg, yg, yg, yg, gate, ln_g.reshape(1, dm), ln_b.reshape(1, dm))


SC_CORES = 2
SC_SUBCORES = 16
SC_WORKERS = SC_CORES * SC_SUBCORES


def _sc_gather_rows(table, idx, window):
    n = idx.shape[0]
    dim = table.shape[1]
    n_steps = n // (SC_WORKERS * window)
    assert n_steps * window * SC_WORKERS == n and n_steps % 2 == 0 and window % SUBLANE == 0 and window <= LANE
    idx3 = idx.reshape(SC_WORKERS, n_steps, window)
    mesh = plsc.VectorSubcoreMesh(core_axis_name="c", subcore_axis_name="s",
                                  num_cores=SC_CORES, num_subcores=SC_SUBCORES)

    def body(table_hbm, idx_hbm, out_hbm, idx_v, rows_v, gsem, wsem):
        wid = lax.axis_index("s") * SC_CORES + lax.axis_index("c")
        pltpu.sync_copy(idx_hbm.at[wid], idx_v)

        def gather(j, buf):
            return pltpu.make_async_copy(table_hbm.at[idx_v.at[j]], rows_v.at[buf], gsem.at[buf])

        def write(j, buf):
            base = pl.multiple_of((wid * n_steps + j) * window, window)
            return pltpu.make_async_copy(rows_v.at[buf], out_hbm.at[pl.ds(base, window)], wsem.at[buf])

        gather(0, 0).start()

        @pl.loop(0, n_steps, step=2)
        def _(j0):
            for buf in range(2):
                j = j0 + buf
                gather(j, buf).wait()

                @pl.when(j >= 1)
                def _():
                    write(j - 1, 1 - buf).wait()

                @pl.when(j + 1 < n_steps)
                def _():
                    gather(j + 1, 1 - buf).start()

                write(j, buf).start()

        write(n_steps - 1, 1).wait()

    return pl.kernel(
        body, out_type=jax.ShapeDtypeStruct((n, dim), table.dtype), mesh=mesh,
        scratch_types=[pltpu.VMEM((n_steps, window), jnp.int32), pltpu.VMEM((2, window, dim), table.dtype),
                       pltpu.SemaphoreType.DMA((2,)), pltpu.SemaphoreType.DMA((2,))],
        name="sc_gather",
    )(table, idx3)


def _sc_scatter_rows(src, dest, n_out, window):
    t, dim = src.shape
    k = dest.shape[0]
    n_steps = t // (SC_WORKERS * window)
    assert n_steps * window * SC_WORKERS == t and n_steps % 2 == 0 and window % SUBLANE == 0 and window <= LANE
    idx3 = dest.reshape(k, SC_WORKERS, n_steps, window).transpose(1, 2, 0, 3).reshape(SC_WORKERS, n_steps * k, window)
    mesh = plsc.VectorSubcoreMesh(core_axis_name="c", subcore_axis_name="s",
                                  num_cores=SC_CORES, num_subcores=SC_SUBCORES)

    def body(src_hbm, idx_hbm, out_hbm, idx_v, rows_v, rsem, ssem):
        wid = lax.axis_index("s") * SC_CORES + lax.axis_index("c")
        pltpu.sync_copy(idx_hbm.at[wid], idx_v)

        def read(s, buf):
            base = pl.multiple_of((wid * n_steps + s) * window, window)
            return pltpu.make_async_copy(src_hbm.at[pl.ds(base, window)], rows_v.at[buf], rsem.at[buf])

        def scatter(s, j, buf):
            return pltpu.make_async_copy(rows_v.at[buf], out_hbm.at[idx_v.at[s * k + j]], ssem.at[buf])

        read(0, 0).start()

        @pl.loop(0, n_steps, step=2)
        def _(s0):
            for buf in range(2):
                s = s0 + buf
                read(s, buf).wait()

                @pl.when(s >= 1)
                def _():
                    for j in range(k):
                        scatter(s - 1, j, 1 - buf).wait()

                @pl.when(s + 1 < n_steps)
                def _():
                    read(s + 1, 1 - buf).start()

                for j in range(k):
                    scatter(s, j, buf).start()

        for j in range(k):
            scatter(n_steps - 1, j, 1).wait()

    return pl.kernel(
        body, out_type=jax.ShapeDtypeStruct((n_out, dim), src.dtype), mesh=mesh,
        scratch_types=[pltpu.VMEM((n_steps * k, window), jnp.int32), pltpu.VMEM((2, window, dim), src.dtype),
                       pltpu.SemaphoreType.DMA((2,)), pltpu.SemaphoreType.DMA((2,))],
        name="sc_scatter",
    )(src, idx3)


def _pad_cols(w, width):
    return jnp.pad(w, ((0, 0), (0, width - w.shape[1])))


def kernel(x, w_in, gmlp_ln_g, gmlp_ln_b, gmlp_ws, gmlp_bs, rwkv_mu, rwkv_w0, rwkv_w2, rwkv_a0, rwkv_a2, rwkv_g2, rwkv_k_k, rwkv_k_a, rwkv_r_k, rwkv_ln_g, rwkv_ln_b, mlstm_conv_w, mlstm_conv_b, mlstm_gate_b, mlstm_ln_g, w_out, ln1_g, ln1_b, router_w, router_b, exp_w1, exp_b1, exp_w2, exp_b2, ln2_g, ln2_b):
    batch, seq, dm = x.shape
    depth = w_in.shape[0]
    sb = batch // N_STREAMS if batch % N_STREAMS == 0 else batch
    t = sb * seq
    gw = gmlp_ln_g.shape[1]
    rw = rwkv_w0.shape[2]
    mw = mlstm_ln_g.shape[1]
    g_proj = 2 * gw
    r_proj = 3 * rw + W_LORA + A_LORA + G_LORA
    alpha = (2 * depth) ** 0.25
    n_blocks = -(-t * TOP_K // MOE_BLOCK) + N_EXPERTS
    streams = [x[i * sb:(i + 1) * sb].reshape(t, dm) for i in range(batch // sb)]
    for l in range(depth):
        mixed = []
        for xf in streams:
            pg, pr, pm = _proj(xf, w_in, l, g_proj, r_proj)
            y_g = _gmlp(pg, gmlp_ln_g[l], gmlp_ln_b[l], gmlp_ws[l], gmlp_bs[l])
            r, v, a, kd, b, lw, bonus, rgate = _rwkv_prep(
                pr, seq, rwkv_mu[l], rwkv_w0[l], rwkv_w2[l], rwkv_a0[l], rwkv_a2[l], rwkv_g2[l],
                rwkv_k_k[l], rwkv_k_a[l], rwkv_r_k[l].reshape(-1))
            ro = _rwkv_scan(r, v, a, kd, b, lw, sb, seq)
            q, k, gates = _mlstm_prep(pm, seq, mlstm_conv_w[l], mlstm_conv_b[l], mlstm_gate_b[l], mw)
            mh = _mlstm_scan(q, k, pm, gates, sb, seq)
            mixed.append(_mix_out(xf, y_g, ro, bonus, rgate, rwkv_ln_g[l], rwkv_ln_b[l], mh, pm, mlstm_ln_g[l],
                                  w_out, l, ln1_g[l], ln1_b[l], router_w[l], router_b[l], alpha))
        streams = []
        for x1, x1p, topi, gate in mixed:
            dest, block_e, valid_end, n_used = _moe_plan(topi, t, n_blocks)
            xs = _sc_scatter_rows(x1p, dest, n_blocks * MOE_BLOCK, window=64)
            ys = _moe_experts(xs, block_e, n_used, valid_end, exp_w1, exp_b1, exp_w2, exp_b2, l)
            yg = _sc_gather_rows(ys, dest.reshape(-1), window=64)
            streams.append(_combine(x1, yg, gate, ln2_g[l], ln2_b[l], alpha))
    return jnp.concatenate(streams, axis=0).reshape(batch, seq, dm)
```

```python
import functools
import math

import jax
import jax.numpy as jnp
from jax import lax
from jax.experimental import pallas as pl
from jax.experimental.pallas import tpu as pltpu
from jax.experimental.pallas import tpu_sc as plsc

F32 = jnp.float32
BF16 = jnp.bfloat16
HI = lax.Precision.HIGHEST

HEAD_DIM = 64
GMLP_CHUNK = 128
MLSTM_CHUNK = 128
RWKV_CHUNK = 64
W_LORA = 64
A_LORA = 64
G_LORA = 128
N_EXPERTS = 32
TOP_K = 4
MOE_BLOCK = 512
SWIGLU_LIMIT = 7.0
SWIGLU_ALPHA = 1.702
LN_EPS = 1e-5
RWKV_GN_EPS = 64e-5
LANE = 128
SUBLANE = 8
VMEM_LIMIT = 48 * 1024 * 1024
NEG_BIG = -1e30


def _cparams(n_axes):
    return pltpu.CompilerParams(dimension_semantics=("arbitrary",) * n_axes,
                                vmem_limit_bytes=VMEM_LIMIT)


def _full(shape):
    return pl.BlockSpec(shape, lambda *_: (0,) * len(shape))


def _dot(a, b, precision=None):
    return jnp.dot(a, b, preferred_element_type=F32, precision=precision)


def _dot_nt(a, b, precision=None):
    return lax.dot_general(a, b, (((1,), (1,)), ((), ())), preferred_element_type=F32, precision=precision)


def _dot_tn(a, b, precision=None):
    return lax.dot_general(a, b, (((0,), (0,)), ((), ())), preferred_element_type=F32, precision=precision)


def _split(x):
    hi = x.astype(BF16)
    return hi, (x - hi.astype(F32)).astype(BF16)


def _split3(x):
    hi = x.astype(BF16)
    r1 = x - hi.astype(F32)
    mid = r1.astype(BF16)
    return hi, mid, (r1 - mid.astype(F32)).astype(BF16)


def _mm(a, b, mode, dot=_dot):
    if mode == "hi":
        return dot(a, b, HI)
    if mode == "b1":
        return dot(a.astype(BF16), b.astype(BF16))
    bh, bl = _split(b)
    if mode == "b2":
        ah = a.astype(BF16)
        return dot(ah, bh) + dot(ah, bl)
    ah, al = _split(a)
    return dot(ah, bh) + (dot(ah, bl) + dot(al, bh))


def _dot_exact_lhs(a_bf16, x):
    hi, mid, lo = _split3(x)
    return _dot(a_bf16, hi) + (_dot(a_bf16, mid) + _dot(a_bf16, lo))


def _dot_exact_rhs(x, b_bf16, terms=3):
    if terms == 2:
        hi, lo = _split(x)
        return _dot(hi, b_bf16) + _dot(lo, b_bf16)
    hi, mid, lo = _split3(x)
    return _dot(hi, b_bf16) + (_dot(mid, b_bf16) + _dot(lo, b_bf16))


def _pack_bf16_pairs(x):
    n = x.shape[1] // 2
    lo = pltpu.bitcast(x[:, :n].astype(BF16).astype(F32), jnp.uint32)
    hi = pltpu.bitcast(x[:, n:].astype(BF16).astype(F32), jnp.uint32)
    return hi | (lo >> 16)


def _unpack_bf16_pairs(w):
    lo = pltpu.bitcast(w << 16, F32)
    hi = pltpu.bitcast(w & jnp.uint32(0xFFFF0000), F32)
    return lo, hi


def _sigmoid(x):
    return 1.0 / (1.0 + jnp.exp(-x))


def _softplus(x):
    return jnp.maximum(x, 0.0) + jnp.log1p(jnp.exp(-jnp.abs(x)))


def _block_diag_ones(width):
    h = jnp.arange(width) // HEAD_DIM
    return (h[:, None] == h[None, :]).astype(F32)


CAST_ROWS = 128


def _cast_rows(src_ref, dst_ref):
    n_src, n_dst = src_ref.shape[1], dst_ref.shape[1]
    whole = n_src // LANE * LANE

    def step(r, carry):
        rows = pl.ds(pl.multiple_of(r * CAST_ROWS, CAST_ROWS), CAST_ROWS)
        dst_ref[rows, :whole] = src_ref[rows, :whole].astype(BF16)
        if n_dst > whole:
            tail = [src_ref[rows, whole:]] if n_src > whole else []
            tail.append(jnp.zeros((CAST_ROWS, n_dst - n_src), F32))
            dst_ref[rows, whole:] = jnp.concatenate(tail, axis=1).astype(BF16)
        return carry
    lax.fori_loop(0, src_ref.shape[0] // CAST_ROWS, step, 0)


def _proj_body(x_ref, w_ref, pg_ref, pr_ref, pm_ref, wb_ref, *, ng, nr):
    @pl.when(pl.program_id(0) == 0)
    def _():
        _cast_rows(w_ref, wb_ref)

    xb = x_ref[...].astype(BF16)
    pg_ref[...] = _dot(xb, wb_ref[:, :ng])
    pr_ref[...] = _dot(xb, wb_ref[:, ng:ng + nr])
    pm_ref[...] = _dot(xb, wb_ref[:, ng + nr:])


def _proj(x, w_in, layer, ng, nr, tm=256):
    t, d = x.shape
    p_in = w_in.shape[2]
    p_pad = -(-p_in // LANE) * LANE
    nm = p_pad - ng - nr
    row = lambda n: pl.BlockSpec((tm, n), lambda i: (i, 0))
    return pl.pallas_call(
        functools.partial(_proj_body, ng=ng, nr=nr),
        grid=(t // tm,),
        in_specs=[row(d), pl.BlockSpec((None, d, p_in), lambda i: (layer, 0, 0), pipeline_mode=pl.Buffered(1))],
        out_specs=[row(ng), row(nr), row(nm)],
        out_shape=[jax.ShapeDtypeStruct((t, n), F32) for n in (ng, nr, nm)],
        scratch_shapes=[pltpu.VMEM((d, p_pad), BF16)],
        compiler_params=_cparams(1),
        name="in_proj",
    )(x, w_in)


def _gmlp_body(pg_ref, lng_ref, lnb_ref, ws_ref, bst_ref, o_ref, *, gw, chunks):
    p = pg_ref[...]
    p = 0.5 * p * (1.0 + lax.erf(p * math.sqrt(0.5)))
    u, v = p[:, :gw], p[:, gw:]
    mu = jnp.mean(v, axis=-1, keepdims=True)
    vc = v - mu
    var = jnp.mean(vc * vc, axis=-1, keepdims=True)
    vn = vc * lax.rsqrt(var + LN_EPS) * lng_ref[...] + lnb_ref[...]
    n_heads = gw // HEAD_DIM
    for c in range(chunks):
        rows = slice(c * GMLP_CHUNK, (c + 1) * GMLP_CHUNK)
        ys = []
        for h in range(n_heads):
            cols = slice(h * HEAD_DIM, (h + 1) * HEAD_DIM)
            y = _dot(ws_ref[h], vn[rows, cols].astype(BF16)) + bst_ref[:, h:h + 1]
            ys.append(y)
        o_ref[rows, :] = u[rows, :] * jnp.concatenate(ys, axis=1)


def _gmlp(pg, ln_g, ln_b, ws, bs, chunks=4):
    t = pg.shape[0]
    gw = pg.shape[1] // 2
    n_heads = gw // HEAD_DIM
    tm = chunks * GMLP_CHUNK
    bst = jnp.zeros((GMLP_CHUNK, LANE), F32).at[:, :n_heads].set(bs.T)
    return pl.pallas_call(
        functools.partial(_gmlp_body, gw=gw, chunks=chunks),
        grid=(t // tm,),
        in_specs=[pl.BlockSpec((tm, 2 * gw), lambda i: (i, 0)), _full((1, gw)), _full((1, gw)),
                  _full((n_heads, GMLP_CHUNK, GMLP_CHUNK)), _full((GMLP_CHUNK, LANE))],
        out_specs=pl.BlockSpec((tm, gw), lambda i: (i, 0)),
        out_shape=jax.ShapeDtypeStruct((t, gw), F32),
        compiler_params=_cparams(1),
        name="gmlp",
    )(pg, ln_g.reshape(1, gw), ln_b.reshape(1, gw), ws.astype(BF16), bst)


def _halo_specs(tm, width, n_rows):
    per8 = tm // SUBLANE
    last = n_rows // SUBLANE - 1
    prev = pl.BlockSpec((SUBLANE, width), lambda i: (jnp.maximum(i * per8 - 1, 0), 0))
    nxt = pl.BlockSpec((SUBLANE, width), lambda i: (jnp.minimum((i + 1) * per8, last), 0))
    return prev, nxt


def _neighbours(cur, prev_blk, next_blk, tiles_per_seq):
    tm = cur.shape[0]
    j = pl.program_id(0) % tiles_per_seq
    prev_row = jnp.where(j > 0, prev_blk[SUBLANE - 1:SUBLANE, :], 0.0)
    next_row = jnp.where(j < tiles_per_seq - 1, next_blk[0:1, :], 0.0)
    ridx = lax.broadcasted_iota(jnp.int32, cur.shape, 0)
    before = jnp.where(ridx == 0, prev_row, pltpu.roll(cur, 1, 0))
    after = jnp.where(ridx == tm - 1, next_row, pltpu.roll(cur, tm - 1, 0))
    return before, after


def _rwkv_prep_body(pr_ref, prev_ref, next_ref, mu_ref, w0_ref, w2_ref, a0_ref, a2_ref, g2_ref,
                    kk_ref, ka_ref, rk_ref, bd_ref,
                    r_out, v_out, a_out, kd_out, b_out, lw_out, bonus_out, gate_out, *, rw, tiles_per_seq):
    pf = pr_ref[...]
    before, after = _neighbours(pf, prev_ref[...], next_ref[...], tiles_per_seq)
    pf = pf + mu_ref[0:1, :] * (before - pf) + mu_ref[1:2, :] * (after - pf)
    o3 = 3 * rw
    r, k, v = pf[:, :rw], pf[:, rw:2 * rw], pf[:, 2 * rw:o3]
    wd = pf[:, o3:o3 + W_LORA]
    ad = pf[:, o3 + W_LORA:o3 + W_LORA + A_LORA]
    gd = pf[:, o3 + W_LORA + A_LORA:]
    bd = bd_ref[...]
    kk = k * kk_ref[...]
    ss = _dot_exact_rhs(kk * kk, bd, terms=2)
    kk = kk / jnp.maximum(jnp.sqrt(ss), 1e-12)
    twd = jnp.tanh(wd)
    ksum = jnp.zeros_like(k)
    for d in range(2):
        w_log = -_softplus(-(w0_ref[d:d + 1, :] + _mm(twd, w2_ref[d], "b3"))) - 0.5
        lw_out[d] = -jnp.exp(w_log)
        iclr = _sigmoid(a0_ref[d:d + 1, :] + _mm(ad, a2_ref[d], "b3"))
        kd = k * (1.0 + (iclr - 1.0) * ka_ref[...])
        kd_out[d] = kd
        b_out[d] = kk * iclr
        ksum = ksum + kd
    r_out[...] = r
    v_out[...] = v
    a_out[...] = -kk
    bonus_out[...] = _dot_exact_rhs(r * ksum * rk_ref[...], bd, terms=2) * v
    gate_out[...] = _dot(_sigmoid(gd).astype(BF16), g2_ref[...])


def _rwkv_prep(pr, seq, mu, w0, w2, a0, a2, g2, k_k, k_a, r_k, tm=256):
    t, rproj = pr.shape
    rw = w0.shape[1]
    tiles_per_seq = seq // tm
    prev, nxt = _halo_specs(tm, rproj, t)
    row = pl.BlockSpec((tm, rw), lambda i: (i, 0))
    row2 = pl.BlockSpec((2, tm, rw), lambda i: (0, i, 0))
    one = jax.ShapeDtypeStruct((t, rw), F32)
    two = jax.ShapeDtypeStruct((2, t, rw), F32)
    return pl.pallas_call(
        functools.partial(_rwkv_prep_body, rw=rw, tiles_per_seq=tiles_per_seq),
        grid=(t // tm,),
        in_specs=[pl.BlockSpec((tm, rproj), lambda i: (i, 0)), prev, nxt,
                  _full((2, rproj)), _full((2, rw)), _full((2, W_LORA, rw)), _full((2, rw)),
                  _full((2, A_LORA, rw)), _full((G_LORA, rw)), _full((1, rw)), _full((1, rw)),
                  _full((1, rw)), _full((rw, rw))],
        out_specs=[row, row, row, row2, row2, row2, row, row],
        out_shape=[one, one, one, two, two, two, one, one],
        compiler_params=_cparams(1),
        name="rwkv_prep",
    )(pr, pr, pr, mu, w0, w2, a0, a2, g2.astype(BF16), k_k.reshape(1, rw), k_a.reshape(1, rw),
      r_k.reshape(1, rw), _block_diag_ones(rw).astype(BF16))


P_G, P_INV, P_APPLY, P_STATE, P_SEQ = "b1", "b1", "b1", "b1", "b2"


def _rwkv_intra_body(r_ref, v_ref, a_ref, kd_ref, b_ref, lw_ref, rq_out, o0_out, mtx_out, hc_out,
                     *, n_heads, chunks):
    L = RWKV_CHUNK
    d = pl.program_id(0)
    row = lax.broadcasted_iota(jnp.int32, (L, L), 0)
    col = lax.broadcasted_iota(jnp.int32, (L, L), 1)
    fwd = d == 0
    rel = (col - row) * (1 - 2 * d)
    incl = rel <= 0
    strict = rel < 0
    eye = (row == col).astype(F32)
    tri = incl.astype(BF16)
    pairs = []
    for c in range(chunks):
        rows = slice(c * L, (c + 1) * L)
        lw = lw_ref[rows, :]
        cum = _dot_exact_lhs(tri, lw)
        tot = jnp.where(fwd, cum[L - 1:L, :], cum[0:1, :])
        e_neg = jnp.exp(-cum)
        e_end = jnp.exp(tot - cum)
        e_tot = jnp.exp(tot)
        r, v, a, kd, b = r_ref[rows, :], v_ref[rows, :], a_ref[rows, :], kd_ref[rows, :], b_ref[rows, :]
        at, rt, bt, kt = a * jnp.exp(cum - lw), r * jnp.exp(cum), b * e_neg, kd * e_neg
        kend, bend = kd * e_end, b * e_end
        for h in range(n_heads):
            sl = slice(h * HEAD_DIM, (h + 1) * HEAD_DIM)
            pairs.append(dict(at=at[:, sl], rt=rt[:, sl], bt=bt[:, sl], kt=kt[:, sl], v=v[:, sl],
                              kend=kend[:, sl], bend=bend[:, sl], e_tot=e_tot[:, sl]))
    for p in pairs:
        p["g"] = _mm(jnp.concatenate([p["at"], p["rt"]], axis=0),
                     jnp.concatenate([p["bt"], p["kt"]], axis=0), P_G, _dot_nt)
    row2 = lax.broadcasted_iota(jnp.int32, (L, 2 * L), 0)
    col2 = lax.broadcasted_iota(jnp.int32, (L, 2 * L), 1) & (L - 1)
    rel2 = (col2 - row2) * (1 - 2 * d)
    incl2 = rel2 <= 0
    strict2 = rel2 < 0
    zeros = jnp.zeros((L, HEAD_DIM), F32)
    for p in pairs:
        g = p.pop("g")
        a_both = jnp.where(strict2, g[:L, :], 0.0)
        p["m_both"] = jnp.where(incl2, g[L:, :], 0.0)
        p["pw"] = a_both[:, :L]
        p["a_ak"] = a_both[:, L:]
        p["inv"] = eye + p["pw"]
    for _ in range(int(math.log2(L)) - 1):
        for p in pairs:
            p["pw"] = _mm(p["pw"], p["pw"], P_INV)
        for p in pairs:
            p["inv"] = p["inv"] + _mm(p["inv"], p["pw"], P_INV)
    for p in pairs:
        p["akv"] = _mm(p["a_ak"], p["v"], P_APPLY)
    for p in pairs:
        wu = _mm(p["inv"], jnp.concatenate([p["at"], p["akv"]], axis=1), P_APPLY)
        p["rhs"] = jnp.concatenate([wu, jnp.concatenate([zeros, p["v"]], axis=1)], axis=0)
    for p in pairs:
        p["rq_o0"] = _mm(p["m_both"], p["rhs"], P_APPLY)
    for p in pairs:
        p["m_hc"] = _mm(jnp.concatenate([p["bend"], p["kend"]], axis=0), p["rhs"], P_STATE, _dot_tn)
    for c in range(chunks):
        ps = pairs[c * n_heads:(c + 1) * n_heads]
        rows = slice(c * L, (c + 1) * L)
        krows = slice(c * HEAD_DIM, (c + 1) * HEAD_DIM)
        rq_out[rows, :] = jnp.concatenate([p["rt"] + p["rq_o0"][:, :HEAD_DIM] for p in ps], axis=1)
        o0_out[rows, :] = jnp.concatenate([p["rq_o0"][:, HEAD_DIM:] for p in ps], axis=1)
        mtx_out[krows, :] = jnp.concatenate([eye * p["e_tot"] + p["m_hc"][:, :HEAD_DIM] for p in ps], axis=1)
        hc_out[krows, :] = jnp.concatenate([p["m_hc"][:, HEAD_DIM:] for p in ps], axis=1)


def _rwkv_intra(r, v, a, kd, b, lw, chunks=4):
    t, rw = r.shape
    n_heads = rw // HEAD_DIM
    tm = chunks * RWKV_CHUNK
    tk = chunks * HEAD_DIM
    n_tiles = t // tm
    one = pl.BlockSpec((tm, rw), lambda d, i: (i, 0))
    two = pl.BlockSpec((None, tm, rw), lambda d, i: (d, i, 0))
    twok = pl.BlockSpec((None, tk, rw), lambda d, i: (d, i, 0))
    return pl.pallas_call(
        functools.partial(_rwkv_intra_body, n_heads=n_heads, chunks=chunks),
        grid=(2, n_tiles),
        in_specs=[one, one, one, two, two, two],
        out_specs=[two, two, twok, twok],
        out_shape=[jax.ShapeDtypeStruct((2, t, rw), F32), jax.ShapeDtypeStruct((2, t, rw), F32),
                   jax.ShapeDtypeStruct((2, n_tiles * tk, rw), F32),
                   jax.ShapeDtypeStruct((2, n_tiles * tk, rw), F32)],
        compiler_params=_cparams(2),
        name="rwkv_intra",
    )(r, v, a, kd, b, lw)


def _rwkv_seq_body(rq0, o00, mtx0, hc0, rq1, o01, mtx1, hc1, out0, out1, h_ref, *, n_heads, batch):
    c = pl.program_id(0)

    @pl.when(c == 0)
    def _():
        h_ref[...] = jnp.zeros_like(h_ref)

    L = RWKV_CHUNK
    for d, (rq, o0, mtx, hc, out) in enumerate(((rq0, o00, mtx0, hc0, out0), (rq1, o01, mtx1, hc1, out1))):
        for bi in range(batch):
            rq_t, mtx_t = rq[bi], mtx[bi]
            state = h_ref[d, bi]
            outs, states = [], []
            for h in range(n_heads):
                sl = slice(h * HEAD_DIM, (h + 1) * HEAD_DIM)
                prod = _mm(jnp.concatenate([rq_t[:, sl], mtx_t[:, sl]], axis=0), state[:, sl], P_SEQ)
                outs.append(prod[:L])
                states.append(prod[L:])
            out[bi] = jnp.concatenate(outs, axis=1) + o0[bi]
            h_ref[d, bi] = jnp.concatenate(states, axis=1) + hc[bi]


def _rwkv_seq(rq, o0, mtx, hc, batch, seq):
    _, t, rw = rq.shape
    n_heads = rw // HEAD_DIM
    L = RWKV_CHUNK
    nc = seq // L
    as4 = lambda x: x.reshape(2, batch, x.shape[1] // batch, rw)
    rq, o0, mtx, hc = as4(rq), as4(o0), as4(mtx), as4(hc)
    fwd = lambda rows: pl.BlockSpec((None, batch, rows, rw), lambda c: (0, 0, c, 0))
    bwd = lambda rows: pl.BlockSpec((None, batch, rows, rw), lambda c: (1, 0, nc - 1 - c, 0))
    out0, out1 = pl.pallas_call(
        functools.partial(_rwkv_seq_body, n_heads=n_heads, batch=batch),
        grid=(nc,),
        in_specs=[fwd(L), fwd(L), fwd(HEAD_DIM), fwd(HEAD_DIM), bwd(L), bwd(L), bwd(HEAD_DIM), bwd(HEAD_DIM)],
        out_specs=[pl.BlockSpec((batch, L, rw), lambda c: (0, c, 0)),
                   pl.BlockSpec((batch, L, rw), lambda c: (0, nc - 1 - c, 0))],
        out_shape=[jax.ShapeDtypeStruct((batch, seq, rw), F32)] * 2,
        scratch_shapes=[pltpu.VMEM((2, batch, HEAD_DIM, rw), F32)],
        compiler_params=_cparams(1),
        name="rwkv_seq",
    )(rq, o0, mtx, hc, rq, o0, mtx, hc)
    return out0.reshape(t, rw), out1.reshape(t, rw)


def _rwkv_scan(r, v, a, kd, b, lw, batch, seq):
    rq, o0, mtx, hc = _rwkv_intra(r, v, a, kd, b, lw)
    return _rwkv_seq(rq, o0, mtx, hc, batch, seq)


def _mlstm_prep_body(qk_ref, prev_ref, next_ref, g_ref, cw_ref, cb_ref, gb_ref, q_out, k_out, gate_out,
                     *, mw, n_heads, tiles_per_seq):
    x = qk_ref[...]
    before, after = _neighbours(x, prev_ref[...], next_ref[...], tiles_per_seq)
    y = cb_ref[...] + before * cw_ref[0:1, :] + x * cw_ref[1:2, :] + after * cw_ref[2:3, :]
    y = y * _sigmoid(y)
    q_out[...] = y[:, :mw]
    k_out[...] = y[:, mw:] * (HEAD_DIM ** -0.5)
    g = g_ref[...] + gb_ref[...]
    lane = lax.broadcasted_iota(jnp.int32, g.shape, 1)
    for d in range(2):
        ig = g if d == 0 else pltpu.roll(g, LANE - n_heads, 1)
        fg = pltpu.roll(g, LANE - (1 + d) * n_heads, 1)
        lf = -_softplus(-fg)
        gate_out[d] = jnp.where(lane < n_heads, ig, jnp.where(lane < 2 * n_heads, lf, 0.0))


def _mlstm_prep(pm, seq, conv_w, conv_b, gate_b, mw, tm=256):
    t = pm.shape[0]
    n_heads = mw // HEAD_DIM
    tiles_per_seq = seq // tm
    w2 = 2 * mw
    prev, nxt = _halo_specs(tm, w2, t)
    gcol = (4 * mw) // LANE
    gb = jnp.zeros((1, LANE), F32).at[0, :4 * n_heads].set(gate_b)
    row = pl.BlockSpec((tm, mw), lambda i: (i, 0))
    return pl.pallas_call(
        functools.partial(_mlstm_prep_body, mw=mw, n_heads=n_heads, tiles_per_seq=tiles_per_seq),
        grid=(t // tm,),
        in_specs=[pl.BlockSpec((tm, w2), lambda i: (i, 0)), prev, nxt,
                  pl.BlockSpec((tm, LANE), lambda i: (i, gcol)),
                  _full((3, w2)), _full((1, w2)), _full((1, LANE))],
        out_specs=[row, row, pl.BlockSpec((2, tm, LANE), lambda i: (0, i, 0))],
        out_shape=[jax.ShapeDtypeStruct((t, mw), F32), jax.ShapeDtypeStruct((t, mw), F32),
                   jax.ShapeDtypeStruct((2, t, LANE), F32)],
        compiler_params=_cparams(1),
        name="mlstm_prep",
    )(pm, pm, pm, pm, conv_w, conv_b.reshape(1, w2), gb)


def _mlstm_scan_body(q0_ref, k0_ref, v0_ref, g0_ref, q1_ref, k1_ref, v1_ref, g1_ref, o0_ref, o1_ref,
                     c_ref, m_ref, *, n_heads):
    L = MLSTM_CHUNK
    H = n_heads

    @pl.when(pl.program_id(1) == 0)
    def _():
        c_ref[...] = jnp.zeros_like(c_ref)
        m_ref[...] = jnp.zeros_like(m_ref)

    row = lax.broadcasted_iota(jnp.int32, (L, L), 0)
    col = lax.broadcasted_iota(jnp.int32, (L, L), 1)
    trow = lax.broadcasted_iota(jnp.int32, (L, LANE), 0)
    low = lax.broadcasted_iota(jnp.int32, (L, LANE), 1) < HEAD_DIM
    xr = lax.broadcasted_iota(jnp.int32, (LANE, H * L), 0)
    xc = lax.broadcasted_iota(jnp.int32, (LANE, H * L), 1)
    spread = (xr - H == lax.shift_right_logical(xc, int(math.log2(L)))).astype(BF16)
    hs = []
    for d, (q_ref, k_ref, v_ref, g_ref) in enumerate(((q0_ref, k0_ref, v0_ref, g0_ref),
                                                      (q1_ref, k1_ref, v1_ref, g1_ref))):
        incl = (col <= row) if d == 0 else (col >= row)
        last = L - 1 if d == 0 else 0
        g = g_ref[...]
        bcum = _dot_exact_lhs(incl.astype(BF16), g)
        z = pltpu.roll(g, H, 1) - bcum
        cmax = z
        shift = 1
        while shift < L:
            if d == 0:
                moved = jnp.where(trow >= shift, pltpu.roll(cmax, shift, 0), -jnp.inf)
            else:
                moved = jnp.where(trow < L - shift, pltpu.roll(cmax, L - shift, 0), -jnp.inf)
            cmax = jnp.maximum(cmax, moved)
            shift *= 2
        m_prev = m_ref[d, 0:1, :]
        top = jnp.maximum(cmax, m_prev)
        b_last = bcum[last:last + 1, :]
        lwc = b_last + z
        m_new = jnp.maximum(b_last + m_prev, jnp.max(lwc, axis=0, keepdims=True))
        m_ref[d, 0:1, :] = m_new
        per_row = jnp.concatenate(
            [-top, m_prev - top, bcum + top, jnp.exp(lwc - m_new),
             jnp.broadcast_to(jnp.exp(b_last + m_prev - m_new), (SUBLANE, LANE))], axis=0)
        wide = _dot_exact_rhs(per_row, spread)
        z_t = z.T
        q, k, v = q_ref[...], k_ref[...], v_ref[...]
        for h in range(H):
            slab = slice(h // 2 * LANE, (h // 2 + 1) * LANE)
            cols = slice(h * L, (h + 1) * L)
            mine = low if h % 2 == 0 else jnp.logical_not(low)
            kh = jnp.where(mine, k[:, slab], 0.0)
            hs.append(dict(
                qh=jnp.where(mine, q[:, slab], 0.0).astype(BF16), kh=kh.astype(BF16),
                vext=jnp.where(mine, v[:, slab], 1.0).astype(BF16),
                decay=jnp.exp(jnp.where(incl, wide[0:L, cols] + z_t[H + h:H + h + 1, :], -jnp.inf)),
                w_inter=jnp.exp(wide[L:2 * L, cols]), floor=jnp.exp(-wide[2 * L:3 * L, cols]),
                wk=(wide[3 * L:4 * L, cols] * kh).astype(BF16), dec=wide[4 * L:4 * L + 1, cols],
                cst=c_ref[d, h]))
    for p in hs:
        p["sc"] = (_dot_nt(p["qh"], p["kh"]) * p["decay"]).astype(BF16)
    for p in hs:
        p["numext"] = _dot(p["sc"], p["vext"]) + p["w_inter"] * _dot(p["qh"], p["cst"].astype(BF16))
    for p in hs:
        p["upd"] = _dot_tn(p["wk"], p["vext"])
    for d, o_ref in enumerate((o0_ref, o1_ref)):
        res = []
        for h in range(H):
            p = hs[d * H + h]
            den = pltpu.roll(p["numext"], HEAD_DIM, 1)
            res.append(p["numext"] / jnp.maximum(jnp.abs(den), p["floor"]))
            c_ref[d, h] = p["dec"] * p["cst"] + p["upd"]
        for pair in range(H // 2):
            o_ref[:, pair * LANE:(pair + 1) * LANE] = jnp.where(low, res[2 * pair], res[2 * pair + 1])


def _mlstm_scan(q, k, pm, gates, batch, seq):
    t, mw = q.shape
    n_heads = mw // HEAD_DIM
    L = MLSTM_CHUNK
    nc = seq // L
    fwd = lambda bi, c: bi * nc + c
    bwd = lambda bi, c: bi * nc + nc - 1 - c
    specs = []
    for d, blk in enumerate((fwd, bwd)):
        specs += [pl.BlockSpec((L, mw), lambda bi, c, blk=blk: (blk(bi, c), 0)),
                  pl.BlockSpec((L, mw), lambda bi, c, blk=blk: (blk(bi, c), 0)),
                  pl.BlockSpec((L, mw), lambda bi, c, blk=blk: (blk(bi, c), 2)),
                  pl.BlockSpec((None, L, LANE), lambda bi, c, blk=blk, d=d: (d, blk(bi, c), 0))]
    return pl.pallas_call(
        functools.partial(_mlstm_scan_body, n_heads=n_heads),
        grid=(batch, nc),
        in_specs=specs,
        out_specs=[pl.BlockSpec((L, mw), lambda bi, c: (fwd(bi, c), 0)),
                   pl.BlockSpec((L, mw), lambda bi, c: (bwd(bi, c), 0))],
        out_shape=[jax.ShapeDtypeStruct((t, mw), F32)] * 2,
        scratch_shapes=[pltpu.VMEM((2, n_heads, LANE, LANE), F32), pltpu.VMEM((2, SUBLANE, LANE), F32)],
        compiler_params=_cparams(2),
        name="mlstm_scan",
    )(q, k, pm, gates, q, k, pm, gates)


def _layer_norm(x, g, b):
    mu = jnp.mean(x, axis=-1, keepdims=True)
    xc = x - mu
    var = jnp.mean(xc * xc, axis=-1, keepdims=True)
    return xc * lax.rsqrt(var + LN_EPS) * g + b


def _head_norm(x, bd_mean, eps):
    mu = _dot_exact_rhs(x, bd_mean, terms=2)
    xc = x - mu
    var = _dot_exact_rhs(xc * xc, bd_mean, terms=2)
    return xc * lax.rsqrt(var + eps)


def _mix_out_body(x_ref, yg_ref, ro0_ref, ro1_ref, bonus_ref, rgate_ref, rlg_ref, rlb_ref, mh0_ref, mh1_ref, og_ref,
                  mlg_ref, w_ref, l1g_ref, l1b_ref, rw_ref, rb_ref, bdm_ref,
                  x1_out, x1p_out, topi_out, gate_out, wb_ref, *, alpha, gw, rw):
    @pl.when(pl.program_id(0) == 0)
    def _():
        _cast_rows(w_ref, wb_ref)

    bdm = bdm_ref[...]
    yr = _head_norm(ro0_ref[...] + ro1_ref[...], bdm, RWKV_GN_EPS) * rlg_ref[...] + rlb_ref[...]
    yr = (yr + bonus_ref[...]) * rgate_ref[...]
    ym = _sigmoid(og_ref[...]) * (_head_norm(mh0_ref[...] + mh1_ref[...], bdm, LN_EPS) * mlg_ref[...])
    mix = (_dot(yg_ref[...].astype(BF16), wb_ref[:gw, :]) + _dot(yr.astype(BF16), wb_ref[gw:gw + rw, :])
           + _dot(ym.astype(BF16), wb_ref[gw + rw:, :]))
    x1 = _layer_norm(alpha * x_ref[...] + mix, l1g_ref[...], l1b_ref[...])
    x1_out[...] = x1
    x1p_out[...] = _pack_bf16_pairs(x1)
    lg = _mm(x1, rw_ref[...], "b3") + rb_ref[...]
    lane = lax.broadcasted_iota(jnp.int32, lg.shape, 1)
    vals, topi = [], jnp.zeros(lg.shape, jnp.int32)
    for j in range(TOP_K):
        mx = jnp.max(lg, axis=1, keepdims=True)
        idx = jnp.min(jnp.where(lg == mx, lane, LANE), axis=1, keepdims=True)
        vals.append(mx)
        topi = jnp.where(lane == j, idx, topi)
        lg = jnp.where(lane == idx, -jnp.inf, lg)
    es = [jnp.exp(vj - vals[0]) for vj in vals]
    den = es[0] + es[1] + es[2] + es[3]
    gate = jnp.zeros(lg.shape, F32)
    for j in range(TOP_K):
        gate = jnp.where(lane == j, es[j] / den, gate)
    topi_out[...] = topi.T[:SUBLANE, :]
    gate_out[...] = gate


def _mix_out(x, yg, ro, bonus, rgate, rlg, rlb, mh, pm, mlg, w_out, layer, l1g, l1b, router_w, router_b, alpha,
             tm=256):
    t, dm = x.shape
    gw, rw, mw = yg.shape[1], bonus.shape[1], mh[0].shape[1]
    assert rw == mw
    rwp = jnp.zeros((dm, LANE), F32).at[:, :N_EXPERTS].set(router_w)
    rbp = jnp.full((1, LANE), NEG_BIG, F32).at[0, :N_EXPERTS].set(router_b)
    row = lambda n: pl.BlockSpec((tm, n), lambda i: (i, 0))
    vec = lambda n: _full((1, n))
    return pl.pallas_call(
        functools.partial(_mix_out_body, alpha=alpha, gw=gw, rw=rw),
        grid=(t // tm,),
        in_specs=[row(dm), row(gw), row(rw), row(rw), row(rw), row(rw), vec(rw), vec(rw), row(mw), row(mw),
                  pl.BlockSpec((tm, mw), lambda i: (i, 3)),
                  vec(mw),
                  pl.BlockSpec((None, dm, dm), lambda i: (layer, 0, 0), pipeline_mode=pl.Buffered(1)),
                  vec(dm), vec(dm), _full((dm, LANE)), vec(LANE), _full((rw, rw))],
        out_specs=[row(dm), row(dm // 2), pl.BlockSpec((SUBLANE, tm), lambda i: (0, i)), row(LANE)],
        out_shape=[jax.ShapeDtypeStruct((t, dm), F32), jax.ShapeDtypeStruct((t, dm // 2), jnp.uint32),
                   jax.ShapeDtypeStruct((SUBLANE, t), jnp.int32), jax.ShapeDtypeStruct((t, LANE), F32)],
        scratch_shapes=[pltpu.VMEM((dm, dm), BF16)],
        compiler_params=_cparams(1),
        name="mix_out",
    )(x, yg, ro[0], ro[1], bonus, rgate, rlg.reshape(1, rw), rlb.reshape(1, rw), mh[0], mh[1], pm,
      mlg.reshape(1, mw), w_out, l1g.reshape(1, dm), l1b.reshape(1, dm), rwp, rbp,
      (_block_diag_ones(rw) / HEAD_DIM).astype(BF16))


def _moe_body(be_ref, nu_ref, ve_ref, xs_ref, w1_ref, b1_ref, w2_ref, b2_ref, o_ref, *, dff):
    i = pl.program_id(0)
    active = i < nu_ref[0]

    @pl.when(active)
    def _():
        rowid = i * MOE_BLOCK + lax.broadcasted_iota(jnp.int32, (MOE_BLOCK, 1), 0)
        lo, hi = _unpack_bf16_pairs(jnp.where(rowid < ve_ref[i], xs_ref[...], jnp.uint32(0)))
        xs = jnp.concatenate([lo.astype(BF16), hi.astype(BF16)], axis=1)
        hdn = _dot(xs, w1_ref[...].astype(BF16)) + b1_ref[...]
        g_ = jnp.minimum(hdn[:, :dff], SWIGLU_LIMIT)
        u_ = jnp.clip(hdn[:, dff:], -SWIGLU_LIMIT, SWIGLU_LIMIT)
        act = (u_ + 1.0) * (g_ * _sigmoid(g_ * SWIGLU_ALPHA))
        o_ref[...] = _pack_bf16_pairs(_dot(act.astype(BF16), w2_ref[...].astype(BF16)) + b2_ref[...])

    @pl.when(jnp.logical_not(active))
    def _():
        o_ref[...] = jnp.zeros_like(o_ref)


def _moe_experts(xs, block_e, n_used, valid_end, w1, b1, w2, b2, layer):
    rows, half = xs.shape
    nb = rows // MOE_BLOCK
    depth, ne, dm, dff2 = w1.shape
    dff = dff2 // 2
    grid_spec = pltpu.PrefetchScalarGridSpec(
        num_scalar_prefetch=3,
        grid=(nb,),
        in_specs=[pl.BlockSpec((MOE_BLOCK, half), lambda i, be, nu, ve: (i, 0)),
                  pl.BlockSpec((None, None, dm, dff2), lambda i, be, nu, ve: (layer, be[i], 0, 0)),
                  pl.BlockSpec((None, None, 1, dff2), lambda i, be, nu, ve: (layer, be[i], 0, 0)),
                  pl.BlockSpec((None, None, dff, dm), lambda i, be, nu, ve: (layer, be[i], 0, 0)),
                  pl.BlockSpec((None, None, 1, dm), lambda i, be, nu, ve: (layer, be[i], 0, 0))],
        out_specs=pl.BlockSpec((MOE_BLOCK, half), lambda i, be, nu, ve: (i, 0)),
    )
    return pl.pallas_call(
        functools.partial(_moe_body, dff=dff),
        grid_spec=grid_spec,
        out_shape=jax.ShapeDtypeStruct((rows, half), jnp.uint32),
        compiler_params=_cparams(1),
        name="moe_experts",
    )(block_e, n_used, valid_end, xs, w1, b1.reshape(depth, ne, 1, dff2), w2, b2.reshape(depth, ne, 1, dm))


N_STREAMS = 1
PLAN_TILE = 512
MOE_BLOCK_SHIFT = MOE_BLOCK.bit_length() - 1
assert 1 << MOE_BLOCK_SHIFT == MOE_BLOCK


def _moe_plan_body(e_ref, dest_ref, meta_ref, rank_ref, *, n_tokens, meta_lanes):
    tiles_per_row = n_tokens // PLAN_TILE
    n_tiles = TOP_K * tiles_per_row
    expert = lax.broadcasted_iota(jnp.int32, (N_EXPERTS, PLAN_TILE), 0)
    r_i = lax.broadcasted_iota(jnp.int32, (PLAN_TILE, PLAN_TILE), 0)
    c_i = lax.broadcasted_iota(jnp.int32, (PLAN_TILE, PLAN_TILE), 1)
    earlier = (r_i < c_i).astype(BF16)

    def tile_hits(it):
        j = it // tiles_per_row
        lanes = pl.ds(pl.multiple_of((it % tiles_per_row) * PLAN_TILE, PLAN_TILE), PLAN_TILE)
        return j, lanes, e_ref[pl.ds(j, 1), lanes] == expert

    def rank_step(it, seen):
        j, lanes, hit = tile_hits(it)
        hitf = hit.astype(F32)
        prior = _dot(hit.astype(BF16), earlier) + seen
        rank_ref[pl.ds(j, 1), lanes] = jnp.sum(hitf * prior, axis=0, keepdims=True)
        return seen + jnp.sum(hitf, axis=1, keepdims=True)

    dest_ref[...] = jnp.zeros_like(dest_ref)
    rank_ref[...] = jnp.zeros_like(rank_ref)
    counts = lax.fori_loop(0, n_tiles, rank_step, jnp.zeros((N_EXPERTS, 1), F32))
    padded = ((counts.astype(jnp.int32) + (MOE_BLOCK - 1)) >> MOE_BLOCK_SHIFT) << MOE_BLOCK_SHIFT
    er = lax.broadcasted_iota(jnp.int32, (N_EXPERTS, N_EXPERTS), 0)
    ec = lax.broadcasted_iota(jnp.int32, (N_EXPERTS, N_EXPERTS), 1)
    seg_end = _dot_exact_lhs((ec <= er).astype(BF16),
                             jnp.broadcast_to(padded.astype(F32), (N_EXPERTS, LANE)))[:, 0:1]
    seg_start = seg_end - padded.astype(F32)

    def dest_step(it, carry):
        j, lanes, hit = tile_hits(it)
        base = jnp.sum(jnp.where(hit, seg_start, 0.0), axis=0, keepdims=True)
        dest_ref[pl.ds(j, 1), lanes] = (rank_ref[pl.ds(j, 1), lanes] + base).astype(jnp.int32)
        return carry

    lax.fori_loop(0, n_tiles, dest_step, 0)
    blk_start = (lax.broadcasted_iota(jnp.int32, (N_EXPERTS, meta_lanes), 1) * MOE_BLOCK).astype(F32)
    blk_expert = jnp.minimum(jnp.sum((seg_end <= blk_start).astype(F32), axis=0, keepdims=True), N_EXPERTS - 1.0)
    mine = lax.broadcasted_iota(jnp.int32, (N_EXPERTS, meta_lanes), 0).astype(F32) == blk_expert
    valid_end = jnp.sum(jnp.where(mine, seg_start + counts, 0.0), axis=0, keepdims=True)
    n_used = jnp.broadcast_to(seg_end[N_EXPERTS - 1:N_EXPERTS, :] * (1.0 / MOE_BLOCK), (1, meta_lanes))
    mrow = lax.broadcasted_iota(jnp.int32, (SUBLANE, meta_lanes), 0)
    meta = jnp.where(mrow == 0, blk_expert, jnp.where(mrow == 1, valid_end, jnp.where(mrow == 2, n_used, 0.0)))
    meta_ref[...] = meta.astype(jnp.int32)


def _moe_plan(e_t, n_tokens, n_blocks):
    meta_lanes = -(-n_blocks // LANE) * LANE
    dest, meta = pl.pallas_call(
        functools.partial(_moe_plan_body, n_tokens=n_tokens, meta_lanes=meta_lanes),
        grid=(1,),
        in_specs=[_full((SUBLANE, n_tokens))],
        out_specs=[_full((SUBLANE, n_tokens)), _full((SUBLANE, meta_lanes))],
        out_shape=[jax.ShapeDtypeStruct((SUBLANE, n_tokens), jnp.int32),
                   jax.ShapeDtypeStruct((SUBLANE, meta_lanes), jnp.int32)],
        scratch_shapes=[pltpu.VMEM((SUBLANE, n_tokens), F32)],
        compiler_params=_cparams(1),
        name="moe_plan",
    )(e_t)
    return dest[:TOP_K], meta[0, :n_blocks], meta[1, :n_blocks], meta[2, :1]


def _combine_body(x1_ref, y0_ref, y1_ref, y2_ref, y3_ref, gate_ref, g_ref, b_ref, o_ref, *, alpha):
    gate = gate_ref[...]
    lo, hi = _unpack_bf16_pairs(y0_ref[...])
    lo, hi = gate[:, 0:1] * lo, gate[:, 0:1] * hi
    for j, y_ref in enumerate((y1_ref, y2_ref, y3_ref), start=1):
        lo_j, hi_j = _unpack_bf16_pairs(y_ref[...])
        lo, hi = lo + gate[:, j:j + 1] * lo_j, hi + gate[:, j:j + 1] * hi_j
    ffn = jnp.concatenate([lo, hi], axis=1)
    o_ref[...] = _layer_norm(alpha * x1_ref[...] + ffn, g_ref[...], b_ref[...])


def _combine(x1, yg, gate, ln_g, ln_b, alpha, tm=256):
    t, dm = x1.shape
    n_tiles = t // tm
    expert_rows = lambda j: pl.BlockSpec((tm, dm // 2), lambda i: (i + j * n_tiles, 0))
    return pl.pallas_call(
        functools.partial(_combine_body, alpha=alpha),
        grid=(n_tiles,),
        in_specs=[pl.BlockSpec((tm, dm), lambda i: (i, 0))] + [expert_rows(j) for j in range(TOP_K)]
                 + [pl.BlockSpec((tm, LANE), lambda i: (i, 0)), _full((1, dm)), _full((1, dm))],
        out_specs=pl.BlockSpec((tm, dm), lambda i: (i, 0)),
        out_shape=jax.ShapeDtypeStruct((t, dm), F32),
        compiler_params=_cparams(1),
        name="combine_ln",
    )(x1, yg, yg, yg, yg, gate, ln_g.reshape(1, dm), ln_b.reshape(1, dm))


SC_CORES = 2
SC_SUBCORES = 16
SC_WORKERS = SC_CORES * SC_SUBCORES


def _sc_gather_rows(table, idx, window):
    n = idx.shape[0]
    dim = table.shape[1]
    n_steps = n // (SC_WORKERS * window)
    assert n_steps * window * SC_WORKERS == n and n_steps % 2 == 0 and window % SUBLANE == 0 and window <= LANE
    idx3 = idx.reshape(SC_WORKERS, n_steps, window)
    mesh = plsc.VectorSubcoreMesh(core_axis_name="c", subcore_axis_name="s",
                                  num_cores=SC_CORES, num_subcores=SC_SUBCORES)

    def body(table_hbm, idx_hbm, out_hbm, idx_v, rows_v, gsem, wsem):
        wid = lax.axis_index("s") * SC_CORES + lax.axis_index("c")
        pltpu.sync_copy(idx_hbm.at[wid], idx_v)

        def gather(j, buf):
            return pltpu.make_async_copy(table_hbm.at[idx_v.at[j]], rows_v.at[buf], gsem.at[buf])

        def write(j, buf):
            base = pl.multiple_of((wid * n_steps + j) * window, window)
            return pltpu.make_async_copy(rows_v.at[buf], out_hbm.at[pl.ds(base, window)], wsem.at[buf])

        gather(0, 0).start()

        @pl.loop(0, n_steps, step=2)
        def _(j0):
            for buf in range(2):
                j = j0 + buf
                gather(j, buf).wait()

                @pl.when(j >= 1)
                def _():
                    write(j - 1, 1 - buf).wait()

                @pl.when(j + 1 < n_steps)
                def _():
                    gather(j + 1, 1 - buf).start()

                write(j, buf).start()

        write(n_steps - 1, 1).wait()

    return pl.kernel(
        body, out_type=jax.ShapeDtypeStruct((n, dim), table.dtype), mesh=mesh,
        scratch_types=[pltpu.VMEM((n_steps, window), jnp.int32), pltpu.VMEM((2, window, dim), table.dtype),
                       pltpu.SemaphoreType.DMA((2,)), pltpu.SemaphoreType.DMA((2,))],
        name="sc_gather",
    )(table, idx3)


def _sc_scatter_rows(src, dest, n_out, window):
    t, dim = src.shape
    k = dest.shape[0]
    n_steps = t // (SC_WORKERS * window)
    assert n_steps * window * SC_WORKERS == t and n_steps % 2 == 0 and window % SUBLANE == 0 and window <= LANE
    idx3 = dest.reshape(k, SC_WORKERS, n_steps, window).transpose(1, 2, 0, 3).reshape(SC_WORKERS, n_steps * k, window)
    mesh = plsc.VectorSubcoreMesh(core_axis_name="c", subcore_axis_name="s",
                                  num_cores=SC_CORES, num_subcores=SC_SUBCORES)

    def body(src_hbm, idx_hbm, out_hbm, idx_v, rows_v, rsem, ssem):
        wid = lax.axis_index("s") * SC_CORES + lax.axis_index("c")
        pltpu.sync_copy(idx_hbm.at[wid], idx_v)

        def read(s, buf):
            base = pl.multiple_of((wid * n_steps + s) * window, window)
            return pltpu.make_async_copy(src_hbm.at[pl.ds(base, window)], rows_v.at[buf], rsem.at[buf])

        def scatter(s, j, buf):
            return pltpu.make_async_copy(rows_v.at[buf], out_hbm.at[idx_v.at[s * k + j]], ssem.at[buf])

        read(0, 0).start()

        @pl.loop(0, n_steps, step=2)
        def _(s0):
            for buf in range(2):
                s = s0 + buf
                read(s, buf).wait()

                @pl.when(s >= 1)
                def _():
                    for j in range(k):
                        scatter(s - 1, j, 1 - buf).wait()

                @pl.when(s + 1 < n_steps)
                def _():
                    read(s + 1, 1 - buf).start()

                for j in range(k):
                    scatter(s, j, buf).start()

        for j in range(k):
            scatter(n_steps - 1, j, 1).wait()

    return pl.kernel(
        body, out_type=jax.ShapeDtypeStruct((n_out, dim), src.dtype), mesh=mesh,
        scratch_types=[pltpu.VMEM((n_steps * k, window), jnp.int32), pltpu.VMEM((2, window, dim), src.dtype),
                       pltpu.SemaphoreType.DMA((2,)), pltpu.SemaphoreType.DMA((2,))],
        name="sc_scatter",
    )(src, idx3)


def _pad_cols(w, width):
    return jnp.pad(w, ((0, 0), (0, width - w.shape[1])))


def kernel(x, w_in, gmlp_ln_g, gmlp_ln_b, gmlp_ws, gmlp_bs, rwkv_mu, rwkv_w0, rwkv_w2, rwkv_a0, rwkv_a2, rwkv_g2, rwkv_k_k, rwkv_k_a, rwkv_r_k, rwkv_ln_g, rwkv_ln_b, mlstm_conv_w, mlstm_conv_b, mlstm_gate_b, mlstm_ln_g, w_out, ln1_g, ln1_b, router_w, router_b, exp_w1, exp_b1, exp_w2, exp_b2, ln2_g, ln2_b):
    batch, seq, dm = x.shape
    depth = w_in.shape[0]
    sb = batch // N_STREAMS if batch % N_STREAMS == 0 else batch
    t = sb * seq
    gw = gmlp_ln_g.shape[1]
    rw = rwkv_w0.shape[2]
    mw = mlstm_ln_g.shape[1]
    g_proj = 2 * gw
    r_proj = 3 * rw + W_LORA + A_LORA + G_LORA
    alpha = (2 * depth) ** 0.25
    n_blocks = -(-t * TOP_K // MOE_BLOCK) + N_EXPERTS
    streams = [x[i * sb:(i + 1) * sb].reshape(t, dm) for i in range(batch // sb)]
    for l in range(depth):
        mixed = []
        for xf in streams:
            pg, pr, pm = _proj(xf, w_in, l, g_proj, r_proj)
            y_g = _gmlp(pg, gmlp_ln_g[l], gmlp_ln_b[l], gmlp_ws[l], gmlp_bs[l])
            r, v, a, kd, b, lw, bonus, rgate = _rwkv_prep(
                pr, seq, rwkv_mu[l], rwkv_w0[l], rwkv_w2[l], rwkv_a0[l], rwkv_a2[l], rwkv_g2[l],
                rwkv_k_k[l], rwkv_k_a[l], rwkv_r_k[l].reshape(-1))
            ro = _rwkv_scan(r, v, a, kd, b, lw, sb, seq)
            q, k, gates = _mlstm_prep(pm, seq, mlstm_conv_w[l], mlstm_conv_b[l], mlstm_gate_b[l], mw)
            mh = _mlstm_scan(q, k, pm, gates, sb, seq)
            mixed.append(_mix_out(xf, y_g, ro, bonus, rgate, rwkv_ln_g[l], rwkv_ln_b[l], mh, pm, mlstm_ln_g[l],
                                  w_out, l, ln1_g[l], ln1_b[l], router_w[l], router_b[l], alpha))
        streams = []
        for x1, x1p, topi, gate in mixed:
            dest, block_e, valid_end, n_used = _moe_plan(topi, t, n_blocks)
            xs = _sc_scatter_rows(x1p, dest, n_blocks * MOE_BLOCK, window=64)
            ys = _moe_experts(xs, block_e, n_used, valid_end, exp_w1, exp_b1, exp_w2, exp_b2, l)
            yg = _sc_gather_rows(ys, dest.reshape(-1), window=64)
            streams.append(_combine(x1, yg, gate, ln2_g[l], ln2_b[l], alpha))
    return jnp.concatenate(streams, axis=0).reshape(batch, seq, dm)
```

```python
import functools
import math

import jax
import jax.numpy as jnp
from jax import lax
from jax.experimental import pallas as pl
from jax.experimental.pallas import tpu as pltpu
from jax.experimental.pallas import tpu_sc as plsc

F32 = jnp.float32
BF16 = jnp.bfloat16
HI = lax.Precision.HIGHEST

HEAD_DIM = 64
GMLP_CHUNK = 128
MLSTM_CHUNK = 128
RWKV_CHUNK = 64
W_LORA = 64
A_LORA = 64
G_LORA = 128
N_EXPERTS = 32
TOP_K = 4
MOE_BLOCK = 512
SWIGLU_LIMIT = 7.0
SWIGLU_ALPHA = 1.702
LN_EPS = 1e-5
RWKV_GN_EPS = 64e-5
LANE = 128
SUBLANE = 8
VMEM_LIMIT = 48 * 1024 * 1024
NEG_BIG = -1e30


def _cparams(n_axes):
    return pltpu.CompilerParams(dimension_semantics=("arbitrary",) * n_axes,
                                vmem_limit_bytes=VMEM_LIMIT)


def _full(shape):
    return pl.BlockSpec(shape, lambda *_: (0,) * len(shape))


def _dot(a, b, precision=None):
    return jnp.dot(a, b, preferred_element_type=F32, precision=precision)


def _dot_nt(a, b, precision=None):
    return lax.dot_general(a, b, (((1,), (1,)), ((), ())), preferred_element_type=F32, precision=precision)


def _dot_tn(a, b, precision=None):
    return lax.dot_general(a, b, (((0,), (0,)), ((), ())), preferred_element_type=F32, precision=precision)


def _split(x):
    hi = x.astype(BF16)
    return hi, (x - hi.astype(F32)).astype(BF16)


def _split3(x):
    hi = x.astype(BF16)
    r1 = x - hi.astype(F32)
    mid = r1.astype(BF16)
    return hi, mid, (r1 - mid.astype(F32)).astype(BF16)


def _mm(a, b, mode, dot=_dot):
    if mode == "hi":
        return dot(a, b, HI)
    if mode == "b1":
        return dot(a.astype(BF16), b.astype(BF16))
    bh, bl = _split(b)
    if mode == "b2":
        ah = a.astype(BF16)
        return dot(ah, bh) + dot(ah, bl)
    ah, al = _split(a)
    return dot(ah, bh) + (dot(ah, bl) + dot(al, bh))


def _dot_exact_lhs(a_bf16, x):
    hi, mid, lo = _split3(x)
    return _dot(a_bf16, hi) + (_dot(a_bf16, mid) + _dot(a_bf16, lo))


def _dot_exact_rhs(x, b_bf16, terms=3):
    if terms == 2:
        hi, lo = _split(x)
        return _dot(hi, b_bf16) + _dot(lo, b_bf16)
    hi, mid, lo = _split3(x)
    return _dot(hi, b_bf16) + (_dot(mid, b_bf16) + _dot(lo, b_bf16))


def _pack_bf16_pairs(x):
    n = x.shape[1] // 2
    lo = pltpu.bitcast(x[:, :n].astype(BF16).astype(F32), jnp.uint32)
    hi = pltpu.bitcast(x[:, n:].astype(BF16).astype(F32), jnp.uint32)
    return hi | (lo >> 16)


def _unpack_bf16_pairs(w):
    lo = pltpu.bitcast(w << 16, F32)
    hi = pltpu.bitcast(w & jnp.uint32(0xFFFF0000), F32)
    return lo, hi


def _sigmoid(x):
    return 1.0 / (1.0 + jnp.exp(-x))


def _softplus(x):
    return jnp.maximum(x, 0.0) + jnp.log1p(jnp.exp(-jnp.abs(x)))


def _block_diag_ones(width):
    h = jnp.arange(width) // HEAD_DIM
    return (h[:, None] == h[None, :]).astype(F32)


CAST_ROWS = 128


def _cast_rows(src_ref, dst_ref):
    n_src, n_dst = src_ref.shape[1], dst_ref.shape[1]
    whole = n_src // LANE * LANE

    def step(r, carry):
        rows = pl.ds(pl.multiple_of(r * CAST_ROWS, CAST_ROWS), CAST_ROWS)
        dst_ref[rows, :whole] = src_ref[rows, :whole].astype(BF16)
        if n_dst > whole:
            tail = [src_ref[rows, whole:]] if n_src > whole else []
            tail.append(jnp.zeros((CAST_ROWS, n_dst - n_src), F32))
            dst_ref[rows, whole:] = jnp.concatenate(tail, axis=1).astype(BF16)
        return carry
    lax.fori_loop(0, src_ref.shape[0] // CAST_ROWS, step, 0)


def _proj_body(x_ref, w_ref, pg_ref, pr_ref, pm_ref, wb_ref, *, ng, nr):
    @pl.when(pl.program_id(0) == 0)
    def _():
        _cast_rows(w_ref, wb_ref)

    xb = x_ref[...].astype(BF16)
    pg_ref[...] = _dot(xb, wb_ref[:, :ng])
    pr_ref[...] = _dot(xb, wb_ref[:, ng:ng + nr])
    pm_ref[...] = _dot(xb, wb_ref[:, ng + nr:])


def _proj(x, w_in, layer, ng, nr, tm=512):
    t, d = x.shape
    p_in = w_in.shape[2]
    p_pad = -(-p_in // LANE) * LANE
    nm = p_pad - ng - nr
    row = lambda n: pl.BlockSpec((tm, n), lambda i: (i, 0))
    return pl.pallas_call(
        functools.partial(_proj_body, ng=ng, nr=nr),
        grid=(t // tm,),
        in_specs=[row(d), pl.BlockSpec((None, d, p_in), lambda i: (layer, 0, 0), pipeline_mode=pl.Buffered(1))],
        out_specs=[row(ng), row(nr), row(nm)],
        out_shape=[jax.ShapeDtypeStruct((t, n), F32) for n in (ng, nr, nm)],
        scratch_shapes=[pltpu.VMEM((d, p_pad), BF16)],
        compiler_params=_cparams(1),
        name="in_proj",
    )(x, w_in)


def _gmlp_body(pg_ref, lng_ref, lnb_ref, ws_ref, bst_ref, o_ref, *, gw, chunks):
    p = pg_ref[...]
    p = 0.5 * p * (1.0 + lax.erf(p * math.sqrt(0.5)))
    u, v = p[:, :gw], p[:, gw:]
    mu = jnp.mean(v, axis=-1, keepdims=True)
    vc = v - mu
    var = jnp.mean(vc * vc, axis=-1, keepdims=True)
    vn = vc * lax.rsqrt(var + LN_EPS) * lng_ref[...] + lnb_ref[...]
    n_heads = gw // HEAD_DIM
    for c in range(chunks):
        rows = slice(c * GMLP_CHUNK, (c + 1) * GMLP_CHUNK)
        ys = []
        for h in range(n_heads):
            cols = slice(h * HEAD_DIM, (h + 1) * HEAD_DIM)
            y = _dot(ws_ref[h], vn[rows, cols].astype(BF16)) + bst_ref[:, h:h + 1]
            ys.append(y)
        o_ref[rows, :] = u[rows, :] * jnp.concatenate(ys, axis=1)


def _gmlp(pg, ln_g, ln_b, ws, bs, chunks=4):
    t = pg.shape[0]
    gw = pg.shape[1] // 2
    n_heads = gw // HEAD_DIM
    tm = chunks * GMLP_CHUNK
    bst = jnp.zeros((GMLP_CHUNK, LANE), F32).at[:, :n_heads].set(bs.T)
    return pl.pallas_call(
        functools.partial(_gmlp_body, gw=gw, chunks=chunks),
        grid=(t // tm,),
        in_specs=[pl.BlockSpec((tm, 2 * gw), lambda i: (i, 0)), _full((1, gw)), _full((1, gw)),
                  _full((n_heads, GMLP_CHUNK, GMLP_CHUNK)), _full((GMLP_CHUNK, LANE))],
        out_specs=pl.BlockSpec((tm, gw), lambda i: (i, 0)),
        out_shape=jax.ShapeDtypeStruct((t, gw), F32),
        compiler_params=_cparams(1),
        name="gmlp",
    )(pg, ln_g.reshape(1, gw), ln_b.reshape(1, gw), ws.astype(BF16), bst)


def _halo_specs(tm, width, n_rows):
    per8 = tm // SUBLANE
    last = n_rows // SUBLANE - 1
    prev = pl.BlockSpec((SUBLANE, width), lambda i: (jnp.maximum(i * per8 - 1, 0), 0))
    nxt = pl.BlockSpec((SUBLANE, width), lambda i: (jnp.minimum((i + 1) * per8, last), 0))
    return prev, nxt


def _neighbours(cur, prev_blk, next_blk, tiles_per_seq):
    tm = cur.shape[0]
    j = pl.program_id(0) % tiles_per_seq
    prev_row = jnp.where(j > 0, prev_blk[SUBLANE - 1:SUBLANE, :], 0.0)
    next_row = jnp.where(j < tiles_per_seq - 1, next_blk[0:1, :], 0.0)
    ridx = lax.broadcasted_iota(jnp.int32, cur.shape, 0)
    before = jnp.where(ridx == 0, prev_row, pltpu.roll(cur, 1, 0))
    after = jnp.where(ridx == tm - 1, next_row, pltpu.roll(cur, tm - 1, 0))
    return before, after


def _rwkv_prep_body(pr_ref, prev_ref, next_ref, mu_ref, w0_ref, w2_ref, a0_ref, a2_ref, g2_ref,
                    kk_ref, ka_ref, rk_ref, bd_ref,
                    r_out, v_out, a_out, kd_out, b_out, lw_out, bonus_out, gate_out, *, rw, tiles_per_seq):
    pf = pr_ref[...]
    before, after = _neighbours(pf, prev_ref[...], next_ref[...], tiles_per_seq)
    pf = pf + mu_ref[0:1, :] * (before - pf) + mu_ref[1:2, :] * (after - pf)
    o3 = 3 * rw
    r, k, v = pf[:, :rw], pf[:, rw:2 * rw], pf[:, 2 * rw:o3]
    wd = pf[:, o3:o3 + W_LORA]
    ad = pf[:, o3 + W_LORA:o3 + W_LORA + A_LORA]
    gd = pf[:, o3 + W_LORA + A_LORA:]
    bd = bd_ref[...]
    kk = k * kk_ref[...]
    ss = _dot_exact_rhs(kk * kk, bd, terms=2)
    kk = kk / jnp.maximum(jnp.sqrt(ss), 1e-12)
    twd = jnp.tanh(wd)
    ksum = jnp.zeros_like(k)
    for d in range(2):
        w_log = -_softplus(-(w0_ref[d:d + 1, :] + _mm(twd, w2_ref[d], "b3"))) - 0.5
        lw_out[d] = -jnp.exp(w_log)
        iclr = _sigmoid(a0_ref[d:d + 1, :] + _mm(ad, a2_ref[d], "b3"))
        kd = k * (1.0 + (iclr - 1.0) * ka_ref[...])
        kd_out[d] = kd
        b_out[d] = kk * iclr
        ksum = ksum + kd
    r_out[...] = r
    v_out[...] = v
    a_out[...] = -kk
    bonus_out[...] = _dot_exact_rhs(r * ksum * rk_ref[...], bd, terms=2) * v
    gate_out[...] = _dot(_sigmoid(gd).astype(BF16), g2_ref[...])


def _rwkv_prep(pr, seq, mu, w0, w2, a0, a2, g2, k_k, k_a, r_k, tm=256):
    t, rproj = pr.shape
    rw = w0.shape[1]
    tiles_per_seq = seq // tm
    prev, nxt = _halo_specs(tm, rproj, t)
    row = pl.BlockSpec((tm, rw), lambda i: (i, 0))
    row2 = pl.BlockSpec((2, tm, rw), lambda i: (0, i, 0))
    one = jax.ShapeDtypeStruct((t, rw), F32)
    two = jax.ShapeDtypeStruct((2, t, rw), F32)
    return pl.pallas_call(
        functools.partial(_rwkv_prep_body, rw=rw, tiles_per_seq=tiles_per_seq),
        grid=(t // tm,),
        in_specs=[pl.BlockSpec((tm, rproj), lambda i: (i, 0)), prev, nxt,
                  _full((2, rproj)), _full((2, rw)), _full((2, W_LORA, rw)), _full((2, rw)),
                  _full((2, A_LORA, rw)), _full((G_LORA, rw)), _full((1, rw)), _full((1, rw)),
                  _full((1, rw)), _full((rw, rw))],
        out_specs=[row, row, row, row2, row2, row2, row, row],
        out_shape=[one, one, one, two, two, two, one, one],
        compiler_params=_cparams(1),
        name="rwkv_prep",
    )(pr, pr, pr, mu, w0, w2, a0, a2, g2.astype(BF16), k_k.reshape(1, rw), k_a.reshape(1, rw),
      r_k.reshape(1, rw), _block_diag_ones(rw).astype(BF16))


P_G, P_INV, P_APPLY, P_STATE, P_SEQ = "b1", "b1", "b1", "b1", "b2"


def _rwkv_intra_body(r_ref, v_ref, a_ref, kd_ref, b_ref, lw_ref, rq_out, o0_out, mtx_out, hc_out,
                     *, n_heads, chunks):
    L = RWKV_CHUNK
    d = pl.program_id(0)
    row = lax.broadcasted_iota(jnp.int32, (L, L), 0)
    col = lax.broadcasted_iota(jnp.int32, (L, L), 1)
    fwd = d == 0
    rel = (col - row) * (1 - 2 * d)
    incl = rel <= 0
    strict = rel < 0
    eye = (row == col).astype(F32)
    tri = incl.astype(BF16)
    pairs = []
    for c in range(chunks):
        rows = slice(c * L, (c + 1) * L)
        lw = lw_ref[rows, :]
        cum = _dot_exact_lhs(tri, lw)
        tot = jnp.where(fwd, cum[L - 1:L, :], cum[0:1, :])
        e_neg = jnp.exp(-cum)
        e_end = jnp.exp(tot - cum)
        e_tot = jnp.exp(tot)
        r, v, a, kd, b = r_ref[rows, :], v_ref[rows, :], a_ref[rows, :], kd_ref[rows, :], b_ref[rows, :]
        at, rt, bt, kt = a * jnp.exp(cum - lw), r * jnp.exp(cum), b * e_neg, kd * e_neg
        kend, bend = kd * e_end, b * e_end
        for h in range(n_heads):
            sl = slice(h * HEAD_DIM, (h + 1) * HEAD_DIM)
            pairs.append(dict(at=at[:, sl], rt=rt[:, sl], bt=bt[:, sl], kt=kt[:, sl], v=v[:, sl],
                              kend=kend[:, sl], bend=bend[:, sl], e_tot=e_tot[:, sl]))
    for p in pairs:
        p["g"] = _mm(jnp.concatenate([p["at"], p["rt"]], axis=0),
                     jnp.concatenate([p["bt"], p["kt"]], axis=0), P_G, _dot_nt)
    row2 = lax.broadcasted_iota(jnp.int32, (L, 2 * L), 0)
    col2 = lax.broadcasted_iota(jnp.int32, (L, 2 * L), 1) & (L - 1)
    rel2 = (col2 - row2) * (1 - 2 * d)
    incl2 = rel2 <= 0
    strict2 = rel2 < 0
    zeros = jnp.zeros((L, HEAD_DIM), F32)
    for p in pairs:
        g = p.pop("g")
        a_both = jnp.where(strict2, g[:L, :], 0.0)
        p["m_both"] = jnp.where(incl2, g[L:, :], 0.0)
        p["pw"] = a_both[:, :L]
        p["a_ak"] = a_both[:, L:]
        p["inv"] = eye + p["pw"]
    for _ in range(int(math.log2(L)) - 1):
        for p in pairs:
            p["pw"] = _mm(p["pw"], p["pw"], P_INV)
        for p in pairs:
            p["inv"] = p["inv"] + _mm(p["inv"], p["pw"], P_INV)
    for p in pairs:
        p["akv"] = _mm(p["a_ak"], p["v"], P_APPLY)
    for p in pairs:
        wu = _mm(p["inv"], jnp.concatenate([p["at"], p["akv"]], axis=1), P_APPLY)
        p["rhs"] = jnp.concatenate([wu, jnp.concatenate([zeros, p["v"]], axis=1)], axis=0)
    for p in pairs:
        p["rq_o0"] = _mm(p["m_both"], p["rhs"], P_APPLY)
    for p in pairs:
        p["m_hc"] = _mm(jnp.concatenate([p["bend"], p["kend"]], axis=0), p["rhs"], P_STATE, _dot_tn)
    for c in range(chunks):
        ps = pairs[c * n_heads:(c + 1) * n_heads]
        rows = slice(c * L, (c + 1) * L)
        krows = slice(c * HEAD_DIM, (c + 1) * HEAD_DIM)
        rq_out[rows, :] = jnp.concatenate([p["rt"] + p["rq_o0"][:, :HEAD_DIM] for p in ps], axis=1)
        o0_out[rows, :] = jnp.concatenate([p["rq_o0"][:, HEAD_DIM:] for p in ps], axis=1)
        mtx_out[krows, :] = jnp.concatenate([eye * p["e_tot"] + p["m_hc"][:, :HEAD_DIM] for p in ps], axis=1)
        hc_out[krows, :] = jnp.concatenate([p["m_hc"][:, HEAD_DIM:] for p in ps], axis=1)


def _rwkv_intra(r, v, a, kd, b, lw, chunks=4):
    t, rw = r.shape
    n_heads = rw // HEAD_DIM
    tm = chunks * RWKV_CHUNK
    tk = chunks * HEAD_DIM
    n_tiles = t // tm
    one = pl.BlockSpec((tm, rw), lambda d, i: (i, 0))
    two = pl.BlockSpec((None, tm, rw), lambda d, i: (d, i, 0))
    twok = pl.BlockSpec((None, tk, rw), lambda d, i: (d, i, 0))
    return pl.pallas_call(
        functools.partial(_rwkv_intra_body, n_heads=n_heads, chunks=chunks),
        grid=(2, n_tiles),
        in_specs=[one, one, one, two, two, two],
        out_specs=[two, two, twok, twok],
        out_shape=[jax.ShapeDtypeStruct((2, t, rw), F32), jax.ShapeDtypeStruct((2, t, rw), F32),
                   jax.ShapeDtypeStruct((2, n_tiles * tk, rw), F32),
                   jax.ShapeDtypeStruct((2, n_tiles * tk, rw), F32)],
        compiler_params=_cparams(2),
        name="rwkv_intra",
    )(r, v, a, kd, b, lw)


def _rwkv_seq_body(rq0, o00, mtx0, hc0, rq1, o01, mtx1, hc1, out0, out1, h_ref, *, n_heads, batch):
    c = pl.program_id(0)

    @pl.when(c == 0)
    def _():
        h_ref[...] = jnp.zeros_like(h_ref)

    L = RWKV_CHUNK
    for d, (rq, o0, mtx, hc, out) in enumerate(((rq0, o00, mtx0, hc0, out0), (rq1, o01, mtx1, hc1, out1))):
        for bi in range(batch):
            rq_t, mtx_t = rq[bi], mtx[bi]
            state = h_ref[d, bi]
            outs, states = [], []
            for h in range(n_heads):
                sl = slice(h * HEAD_DIM, (h + 1) * HEAD_DIM)
                prod = _mm(jnp.concatenate([rq_t[:, sl], mtx_t[:, sl]], axis=0), state[:, sl], P_SEQ)
                outs.append(prod[:L])
                states.append(prod[L:])
            out[bi] = jnp.concatenate(outs, axis=1) + o0[bi]
            h_ref[d, bi] = jnp.concatenate(states, axis=1) + hc[bi]


def _rwkv_seq(rq, o0, mtx, hc, batch, seq):
    _, t, rw = rq.shape
    n_heads = rw // HEAD_DIM
    L = RWKV_CHUNK
    nc = seq // L
    as4 = lambda x: x.reshape(2, batch, x.shape[1] // batch, rw)
    rq, o0, mtx, hc = as4(rq), as4(o0), as4(mtx), as4(hc)
    fwd = lambda rows: pl.BlockSpec((None, batch, rows, rw), lambda c: (0, 0, c, 0))
    bwd = lambda rows: pl.BlockSpec((None, batch, rows, rw), lambda c: (1, 0, nc - 1 - c, 0))
    out0, out1 = pl.pallas_call(
        functools.partial(_rwkv_seq_body, n_heads=n_heads, batch=batch),
        grid=(nc,),
        in_specs=[fwd(L), fwd(L), fwd(HEAD_DIM), fwd(HEAD_DIM), bwd(L), bwd(L), bwd(HEAD_DIM), bwd(HEAD_DIM)],
        out_specs=[pl.BlockSpec((batch, L, rw), lambda c: (0, c, 0)),
                   pl.BlockSpec((batch, L, rw), lambda c: (0, nc - 1 - c, 0))],
        out_shape=[jax.ShapeDtypeStruct((batch, seq, rw), F32)] * 2,
        scratch_shapes=[pltpu.VMEM((2, batch, HEAD_DIM, rw), F32)],
        compiler_params=_cparams(1),
        name="rwkv_seq",
    )(rq, o0, mtx, hc, rq, o0, mtx, hc)
    return out0.reshape(t, rw), out1.reshape(t, rw)


def _rwkv_scan(r, v, a, kd, b, lw, batch, seq):
    rq, o0, mtx, hc = _rwkv_intra(r, v, a, kd, b, lw)
    return _rwkv_seq(rq, o0, mtx, hc, batch, seq)


def _mlstm_prep_body(qk_ref, prev_ref, next_ref, g_ref, cw_ref, cb_ref, gb_ref, q_out, k_out, gate_out,
                     *, mw, n_heads, tiles_per_seq):
    x = qk_ref[...]
    before, after = _neighbours(x, prev_ref[...], next_ref[...], tiles_per_seq)
    y = cb_ref[...] + before * cw_ref[0:1, :] + x * cw_ref[1:2, :] + after * cw_ref[2:3, :]
    y = y * _sigmoid(y)
    q_out[...] = y[:, :mw]
    k_out[...] = y[:, mw:] * (HEAD_DIM ** -0.5)
    g = g_ref[...] + gb_ref[...]
    lane = lax.broadcasted_iota(jnp.int32, g.shape, 1)
    for d in range(2):
        ig = g if d == 0 else pltpu.roll(g, LANE - n_heads, 1)
        fg = pltpu.roll(g, LANE - (1 + d) * n_heads, 1)
        lf = -_softplus(-fg)
        gate_out[d] = jnp.where(lane < n_heads, ig, jnp.where(lane < 2 * n_heads, lf, 0.0))


def _mlstm_prep(pm, seq, conv_w, conv_b, gate_b, mw, tm=256):
    t = pm.shape[0]
    n_heads = mw // HEAD_DIM
    tiles_per_seq = seq // tm
    w2 = 2 * mw
    prev, nxt = _halo_specs(tm, w2, t)
    gcol = (4 * mw) // LANE
    gb = jnp.zeros((1, LANE), F32).at[0, :4 * n_heads].set(gate_b)
    row = pl.BlockSpec((tm, mw), lambda i: (i, 0))
    return pl.pallas_call(
        functools.partial(_mlstm_prep_body, mw=mw, n_heads=n_heads, tiles_per_seq=tiles_per_seq),
        grid=(t // tm,),
        in_specs=[pl.BlockSpec((tm, w2), lambda i: (i, 0)), prev, nxt,
                  pl.BlockSpec((tm, LANE), lambda i: (i, gcol)),
                  _full((3, w2)), _full((1, w2)), _full((1, LANE))],
        out_specs=[row, row, pl.BlockSpec((2, tm, LANE), lambda i: (0, i, 0))],
        out_shape=[jax.ShapeDtypeStruct((t, mw), F32), jax.ShapeDtypeStruct((t, mw), F32),
                   jax.ShapeDtypeStruct((2, t, LANE), F32)],
        compiler_params=_cparams(1),
        name="mlstm_prep",
    )(pm, pm, pm, pm, conv_w, conv_b.reshape(1, w2), gb)


def _mlstm_scan_body(q0_ref, k0_ref, v0_ref, g0_ref, q1_ref, k1_ref, v1_ref, g1_ref, o0_ref, o1_ref,
                     c_ref, m_ref, *, n_heads):
    L = MLSTM_CHUNK
    H = n_heads

    @pl.when(pl.program_id(1) == 0)
    def _():
        c_ref[...] = jnp.zeros_like(c_ref)
        m_ref[...] = jnp.zeros_like(m_ref)

    row = lax.broadcasted_iota(jnp.int32, (L, L), 0)
    col = lax.broadcasted_iota(jnp.int32, (L, L), 1)
    trow = lax.broadcasted_iota(jnp.int32, (L, LANE), 0)
    low = lax.broadcasted_iota(jnp.int32, (L, LANE), 1) < HEAD_DIM
    xr = lax.broadcasted_iota(jnp.int32, (LANE, H * L), 0)
    xc = lax.broadcasted_iota(jnp.int32, (LANE, H * L), 1)
    spread = (xr - H == lax.shift_right_logical(xc, int(math.log2(L)))).astype(BF16)
    hs = []
    for d, (q_ref, k_ref, v_ref, g_ref) in enumerate(((q0_ref, k0_ref, v0_ref, g0_ref),
                                                      (q1_ref, k1_ref, v1_ref, g1_ref))):
        incl = (col <= row) if d == 0 else (col >= row)
        last = L - 1 if d == 0 else 0
        g = g_ref[...]
        bcum = _dot_exact_lhs(incl.astype(BF16), g)
        z = pltpu.roll(g, H, 1) - bcum
        cmax = z
        shift = 1
        while shift < L:
            if d == 0:
                moved = jnp.where(trow >= shift, pltpu.roll(cmax, shift, 0), -jnp.inf)
            else:
                moved = jnp.where(trow < L - shift, pltpu.roll(cmax, L - shift, 0), -jnp.inf)
            cmax = jnp.maximum(cmax, moved)
            shift *= 2
        m_prev = m_ref[d, 0:1, :]
        top = jnp.maximum(cmax, m_prev)
        b_last = bcum[last:last + 1, :]
        lwc = b_last + z
        m_new = jnp.maximum(b_last + m_prev, jnp.max(lwc, axis=0, keepdims=True))
        m_ref[d, 0:1, :] = m_new
        alpha_w = _dot_exact_rhs(-top, spread)
        floor_w = jnp.exp(-_dot_exact_rhs(bcum + top, spread, terms=2))
        wts_w = _dot(jnp.exp(lwc - m_new).astype(BF16), spread)
        rows_w = _dot_exact_rhs(jnp.concatenate(
            [jnp.broadcast_to(m_prev, (SUBLANE, LANE)),
             jnp.broadcast_to(jnp.exp(b_last + m_prev - m_new), (SUBLANE, LANE))], axis=0), spread)
        z_t = z.T
        q, k, v = q_ref[...], k_ref[...], v_ref[...]
        for h in range(H):
            slab = slice(h // 2 * LANE, (h // 2 + 1) * LANE)
            cols = slice(h * L, (h + 1) * L)
            mine = low if h % 2 == 0 else jnp.logical_not(low)
            kh = jnp.where(mine, k[:, slab], 0.0)
            hs.append(dict(
                qh=jnp.where(mine, q[:, slab], 0.0).astype(BF16), kh=kh.astype(BF16),
                vext=jnp.where(mine, v[:, slab], 1.0).astype(BF16),
                decay=jnp.exp(jnp.where(incl, alpha_w[:, cols] + z_t[H + h:H + h + 1, :], -jnp.inf)),
                w_inter=jnp.exp(alpha_w[:, cols] + rows_w[0:1, cols]), floor=floor_w[:, cols],
                wk=(wts_w[:, cols] * kh).astype(BF16), dec=rows_w[SUBLANE:SUBLANE + 1, cols],
                cst=c_ref[d, h]))
    for p in hs:
        p["sc"] = (_dot_nt(p["qh"], p["kh"]) * p["decay"]).astype(BF16)
    for p in hs:
        p["numext"] = _dot(p["sc"], p["vext"]) + p["w_inter"] * _dot(p["qh"], p["cst"].astype(BF16))
    for p in hs:
        p["upd"] = _dot_tn(p["wk"], p["vext"])
    for d, o_ref in enumerate((o0_ref, o1_ref)):
        res = []
        for h in range(H):
            p = hs[d * H + h]
            den = pltpu.roll(p["numext"], HEAD_DIM, 1)
            res.append(p["numext"] / jnp.maximum(jnp.abs(den), p["floor"]))
            c_ref[d, h] = p["dec"] * p["cst"] + p["upd"]
        for pair in range(H // 2):
            o_ref[:, pair * LANE:(pair + 1) * LANE] = jnp.where(low, res[2 * pair], res[2 * pair + 1])


def _mlstm_scan(q, k, pm, gates, batch, seq):
    t, mw = q.shape
    n_heads = mw // HEAD_DIM
    L = MLSTM_CHUNK
    nc = seq // L
    fwd = lambda bi, c: bi * nc + c
    bwd = lambda bi, c: bi * nc + nc - 1 - c
    specs = []
    for d, blk in enumerate((fwd, bwd)):
        specs += [pl.BlockSpec((L, mw), lambda bi, c, blk=blk: (blk(bi, c), 0)),
                  pl.BlockSpec((L, mw), lambda bi, c, blk=blk: (blk(bi, c), 0)),
                  pl.BlockSpec((L, mw), lambda bi, c, blk=blk: (blk(bi, c), 2)),
                  pl.BlockSpec((None, L, LANE), lambda bi, c, blk=blk, d=d: (d, blk(bi, c), 0))]
    return pl.pallas_call(
        functools.partial(_mlstm_scan_body, n_heads=n_heads),
        grid=(batch, nc),
        in_specs=specs,
        out_specs=[pl.BlockSpec((L, mw), lambda bi, c: (fwd(bi, c), 0)),
                   pl.BlockSpec((L, mw), lambda bi, c: (bwd(bi, c), 0))],
        out_shape=[jax.ShapeDtypeStruct((t, mw), F32)] * 2,
        scratch_shapes=[pltpu.VMEM((2, n_heads, LANE, LANE), F32), pltpu.VMEM((2, SUBLANE, LANE), F32)],
        compiler_params=_cparams(2),
        name="mlstm_scan",
    )(q, k, pm, gates, q, k, pm, gates)


def _layer_norm(x, g, b):
    mu = jnp.mean(x, axis=-1, keepdims=True)
    xc = x - mu
    var = jnp.mean(xc * xc, axis=-1, keepdims=True)
    return xc * lax.rsqrt(var + LN_EPS) * g + b


def _head_norm(x, bd_mean, eps):
    mu = _dot_exact_rhs(x, bd_mean, terms=2)
    xc = x - mu
    var = _dot_exact_rhs(xc * xc, bd_mean, terms=2)
    return xc * lax.rsqrt(var + eps)


def _mix_out_body(x_ref, yg_ref, ro0_ref, ro1_ref, bonus_ref, rgate_ref, rlg_ref, rlb_ref, mh0_ref, mh1_ref, og_ref,
                  mlg_ref, w_ref, l1g_ref, l1b_ref, rw_ref, rb_ref, bdm_ref,
                  x1_out, x1p_out, topi_out, gate_out, wb_ref, *, alpha, gw, rw):
    @pl.when(pl.program_id(0) == 0)
    def _():
        _cast_rows(w_ref, wb_ref)

    bdm = bdm_ref[...]
    yr = _head_norm(ro0_ref[...] + ro1_ref[...], bdm, RWKV_GN_EPS) * rlg_ref[...] + rlb_ref[...]
    yr = (yr + bonus_ref[...]) * rgate_ref[...]
    ym = _sigmoid(og_ref[...]) * (_head_norm(mh0_ref[...] + mh1_ref[...], bdm, LN_EPS) * mlg_ref[...])
    mix = (_dot(yg_ref[...].astype(BF16), wb_ref[:gw, :]) + _dot(yr.astype(BF16), wb_ref[gw:gw + rw, :])
           + _dot(ym.astype(BF16), wb_ref[gw + rw:, :]))
    x1 = _layer_norm(alpha * x_ref[...] + mix, l1g_ref[...], l1b_ref[...])
    x1_out[...] = x1
    x1p_out[...] = _pack_bf16_pairs(x1)
    lg = _mm(x1, rw_ref[...], "b3") + rb_ref[...]
    lane = lax.broadcasted_iota(jnp.int32, lg.shape, 1)
    vals, topi = [], jnp.zeros(lg.shape, jnp.int32)
    for j in range(TOP_K):
        mx = jnp.max(lg, axis=1, keepdims=True)
        idx = jnp.min(jnp.where(lg == mx, lane, LANE), axis=1, keepdims=True)
        vals.append(mx)
        topi = jnp.where(lane == j, idx, topi)
        lg = jnp.where(lane == idx, -jnp.inf, lg)
    es = [jnp.exp(vj - vals[0]) for vj in vals]
    den = es[0] + es[1] + es[2] + es[3]
    gate = jnp.zeros(lg.shape, F32)
    for j in range(TOP_K):
        gate = jnp.where(lane == j, es[j] / den, gate)
    topi_out[...] = topi.T[:SUBLANE, :]
    gate_out[...] = gate


def _mix_out(x, yg, ro, bonus, rgate, rlg, rlb, mh, pm, mlg, w_out, layer, l1g, l1b, router_w, router_b, alpha,
             tm=256):
    t, dm = x.shape
    gw, rw, mw = yg.shape[1], bonus.shape[1], mh[0].shape[1]
    assert rw == mw
    rwp = jnp.zeros((dm, LANE), F32).at[:, :N_EXPERTS].set(router_w)
    rbp = jnp.full((1, LANE), NEG_BIG, F32).at[0, :N_EXPERTS].set(router_b)
    row = lambda n: pl.BlockSpec((tm, n), lambda i: (i, 0))
    vec = lambda n: _full((1, n))
    return pl.pallas_call(
        functools.partial(_mix_out_body, alpha=alpha, gw=gw, rw=rw),
        grid=(t // tm,),
        in_specs=[row(dm), row(gw), row(rw), row(rw), row(rw), row(rw), vec(rw), vec(rw), row(mw), row(mw),
                  pl.BlockSpec((tm, mw), lambda i: (i, 3)),
                  vec(mw),
                  pl.BlockSpec((None, dm, dm), lambda i: (layer, 0, 0), pipeline_mode=pl.Buffered(1)),
                  vec(dm), vec(dm), _full((dm, LANE)), vec(LANE), _full((rw, rw))],
        out_specs=[row(dm), row(dm // 2), pl.BlockSpec((SUBLANE, tm), lambda i: (0, i)), row(LANE)],
        out_shape=[jax.ShapeDtypeStruct((t, dm), F32), jax.ShapeDtypeStruct((t, dm // 2), jnp.uint32),
                   jax.ShapeDtypeStruct((SUBLANE, t), jnp.int32), jax.ShapeDtypeStruct((t, LANE), F32)],
        scratch_shapes=[pltpu.VMEM((dm, dm), BF16)],
        compiler_params=_cparams(1),
        name="mix_out",
    )(x, yg, ro[0], ro[1], bonus, rgate, rlg.reshape(1, rw), rlb.reshape(1, rw), mh[0], mh[1], pm,
      mlg.reshape(1, mw), w_out, l1g.reshape(1, dm), l1b.reshape(1, dm), rwp, rbp,
      (_block_diag_ones(rw) / HEAD_DIM).astype(BF16))


def _moe_body(be_ref, nu_ref, ve_ref, xs_ref, w1_ref, b1_ref, w2_ref, b2_ref, o_ref, *, dff):
    i = pl.program_id(0)
    active = i < nu_ref[0]

    @pl.when(active)
    def _():
        rowid = i * MOE_BLOCK + lax.broadcasted_iota(jnp.int32, (MOE_BLOCK, 1), 0)
        lo, hi = _unpack_bf16_pairs(jnp.where(rowid < ve_ref[i], xs_ref[...], jnp.uint32(0)))
        xs = jnp.concatenate([lo.astype(BF16), hi.astype(BF16)], axis=1)
        hdn = _dot(xs, w1_ref[...].astype(BF16)) + b1_ref[...]
        g_ = jnp.minimum(hdn[:, :dff], SWIGLU_LIMIT)
        u_ = jnp.clip(hdn[:, dff:], -SWIGLU_LIMIT, SWIGLU_LIMIT)
        act = (u_ + 1.0) * (g_ * _sigmoid(g_ * SWIGLU_ALPHA))
        o_ref[...] = _pack_bf16_pairs(_dot(act.astype(BF16), w2_ref[...].astype(BF16)) + b2_ref[...])

    @pl.when(jnp.logical_not(active))
    def _():
        o_ref[...] = jnp.zeros_like(o_ref)


def _moe_experts(xs, block_e, n_used, valid_end, w1, b1, w2, b2, layer):
    rows, half = xs.shape
    nb = rows // MOE_BLOCK
    depth, ne, dm, dff2 = w1.shape
    dff = dff2 // 2
    grid_spec = pltpu.PrefetchScalarGridSpec(
        num_scalar_prefetch=3,
        grid=(nb,),
        in_specs=[pl.BlockSpec((MOE_BLOCK, half), lambda i, be, nu, ve: (i, 0)),
                  pl.BlockSpec((None, None, dm, dff2), lambda i, be, nu, ve: (layer, be[i], 0, 0)),
                  pl.BlockSpec((None, None, 1, dff2), lambda i, be, nu, ve: (layer, be[i], 0, 0)),
                  pl.BlockSpec((None, None, dff, dm), lambda i, be, nu, ve: (layer, be[i], 0, 0)),
                  pl.BlockSpec((None, None, 1, dm), lambda i, be, nu, ve: (layer, be[i], 0, 0))],
        out_specs=pl.BlockSpec((MOE_BLOCK, half), lambda i, be, nu, ve: (i, 0)),
    )
    return pl.pallas_call(
        functools.partial(_moe_body, dff=dff),
        grid_spec=grid_spec,
        out_shape=jax.ShapeDtypeStruct((rows, half), jnp.uint32),
        compiler_params=_cparams(1),
        name="moe_experts",
    )(block_e, n_used, valid_end, xs, w1, b1.reshape(depth, ne, 1, dff2), w2, b2.reshape(depth, ne, 1, dm))


N_STREAMS = 1
PLAN_TILE = 512
MOE_BLOCK_SHIFT = MOE_BLOCK.bit_length() - 1
assert 1 << MOE_BLOCK_SHIFT == MOE_BLOCK


def _moe_plan_body(e_ref, dest_ref, meta_ref, rank_ref, *, n_tokens, meta_lanes):
    tiles_per_row = n_tokens // PLAN_TILE
    n_tiles = TOP_K * tiles_per_row
    expert = lax.broadcasted_iota(jnp.int32, (N_EXPERTS, PLAN_TILE), 0)
    r_i = lax.broadcasted_iota(jnp.int32, (PLAN_TILE, PLAN_TILE), 0)
    c_i = lax.broadcasted_iota(jnp.int32, (PLAN_TILE, PLAN_TILE), 1)
    earlier = (r_i < c_i).astype(BF16)

    def tile_hits(it):
        j = it // tiles_per_row
        lanes = pl.ds(pl.multiple_of((it % tiles_per_row) * PLAN_TILE, PLAN_TILE), PLAN_TILE)
        return j, lanes, e_ref[pl.ds(j, 1), lanes] == expert

    def rank_step(it, seen):
        j, lanes, hit = tile_hits(it)
        hitf = hit.astype(F32)
        prior = _dot(hit.astype(BF16), earlier) + seen
        rank_ref[pl.ds(j, 1), lanes] = jnp.sum(hitf * prior, axis=0, keepdims=True)
        return seen + jnp.sum(hitf, axis=1, keepdims=True)

    dest_ref[...] = jnp.zeros_like(dest_ref)
    rank_ref[...] = jnp.zeros_like(rank_ref)
    counts = lax.fori_loop(0, n_tiles, rank_step, jnp.zeros((N_EXPERTS, 1), F32))
    padded = ((counts.astype(jnp.int32) + (MOE_BLOCK - 1)) >> MOE_BLOCK_SHIFT) << MOE_BLOCK_SHIFT
    er = lax.broadcasted_iota(jnp.int32, (N_EXPERTS, N_EXPERTS), 0)
    ec = lax.broadcasted_iota(jnp.int32, (N_EXPERTS, N_EXPERTS), 1)
    seg_end = _dot_exact_lhs((ec <= er).astype(BF16),
                             jnp.broadcast_to(padded.astype(F32), (N_EXPERTS, LANE)))[:, 0:1]
    seg_start = seg_end - padded.astype(F32)

    def dest_step(it, carry):
        j, lanes, hit = tile_hits(it)
        base = jnp.sum(jnp.where(hit, seg_start, 0.0), axis=0, keepdims=True)
        dest_ref[pl.ds(j, 1), lanes] = (rank_ref[pl.ds(j, 1), lanes] + base).astype(jnp.int32)
        return carry

    lax.fori_loop(0, n_tiles, dest_step, 0)
    blk_start = (lax.broadcasted_iota(jnp.int32, (N_EXPERTS, meta_lanes), 1) * MOE_BLOCK).astype(F32)
    blk_expert = jnp.minimum(jnp.sum((seg_end <= blk_start).astype(F32), axis=0, keepdims=True), N_EXPERTS - 1.0)
    mine = lax.broadcasted_iota(jnp.int32, (N_EXPERTS, meta_lanes), 0).astype(F32) == blk_expert
    valid_end = jnp.sum(jnp.where(mine, seg_start + counts, 0.0), axis=0, keepdims=True)
    n_used = jnp.broadcast_to(seg_end[N_EXPERTS - 1:N_EXPERTS, :] * (1.0 / MOE_BLOCK), (1, meta_lanes))
    mrow = lax.broadcasted_iota(jnp.int32, (SUBLANE, meta_lanes), 0)
    meta = jnp.where(mrow == 0, blk_expert, jnp.where(mrow == 1, valid_end, jnp.where(mrow == 2, n_used, 0.0)))
    meta_ref[...] = meta.astype(jnp.int32)


def _moe_plan(e_t, n_tokens, n_blocks):
    meta_lanes = -(-n_blocks // LANE) * LANE
    dest, meta = pl.pallas_call(
        functools.partial(_moe_plan_body, n_tokens=n_tokens, meta_lanes=meta_lanes),
        grid=(1,),
        in_specs=[_full((SUBLANE, n_tokens))],
        out_specs=[_full((SUBLANE, n_tokens)), _full((SUBLANE, meta_lanes))],
        out_shape=[jax.ShapeDtypeStruct((SUBLANE, n_tokens), jnp.int32),
                   jax.ShapeDtypeStruct((SUBLANE, meta_lanes), jnp.int32)],
        scratch_shapes=[pltpu.VMEM((SUBLANE, n_tokens), F32)],
        compiler_params=_cparams(1),
        name="moe_plan",
    )(e_t)
    return dest[:TOP_K], meta[0, :n_blocks], meta[1, :n_blocks], meta[2, :1]


def _combine_body(x1_ref, y0_ref, y1_ref, y2_ref, y3_ref, gate_ref, g_ref, b_ref, o_ref, *, alpha):
    gate = gate_ref[...]
    lo, hi = _unpack_bf16_pairs(y0_ref[...])
    lo, hi = gate[:, 0:1] * lo, gate[:, 0:1] * hi
    for j, y_ref in enumerate((y1_ref, y2_ref, y3_ref), start=1):
        lo_j, hi_j = _unpack_bf16_pairs(y_ref[...])
        lo, hi = lo + gate[:, j:j + 1] * lo_j, hi + gate[:, j:j + 1] * hi_j
    ffn = jnp.concatenate([lo, hi], axis=1)
    o_ref[...] = _layer_norm(alpha * x1_ref[...] + ffn, g_ref[...], b_ref[...])


def _combine(x1, yg, gate, ln_g, ln_b, alpha, tm=256):
    t, dm = x1.shape
    n_tiles = t // tm
    expert_rows = lambda j: pl.BlockSpec((tm, dm // 2), lambda i: (i + j * n_tiles, 0))
    return pl.pallas_call(
        functools.partial(_combine_body, alpha=alpha),
        grid=(n_tiles,),
        in_specs=[pl.BlockSpec((tm, dm), lambda i: (i, 0))] + [expert_rows(j) for j in range(TOP_K)]
                 + [pl.BlockSpec((tm, LANE), lambda i: (i, 0)), _full((1, dm)), _full((1, dm))],
        out_specs=pl.BlockSpec((tm, dm), lambda i: (i, 0)),
        out_shape=jax.ShapeDtypeStruct((t, dm), F32),
        compiler_params=_cparams(1),
        name="combine_ln",
    )(x1, yg, yg, yg, yg, gate, ln_g.reshape(1, dm), ln_b.reshape(1, dm))


SC_CORES = 2
SC_SUBCORES = 16
SC_WORKERS = SC_CORES * SC_SUBCORES


def _sc_gather_rows(table, idx, window):
    n = idx.shape[0]
    dim = table.shape[1]
    n_steps = n // (SC_WORKERS * window)
    assert n_steps * window * SC_WORKERS == n and n_steps % 2 == 0 and window % SUBLANE == 0 and window <= LANE
    idx3 = idx.reshape(SC_WORKERS, n_steps, window)
    mesh = plsc.VectorSubcoreMesh(core_axis_name="c", subcore_axis_name="s",
                                  num_cores=SC_CORES, num_subcores=SC_SUBCORES)

    def body(table_hbm, idx_hbm, out_hbm, idx_v, rows_v, gsem, wsem):
        wid = lax.axis_index("s") * SC_CORES + lax.axis_index("c")
        pltpu.sync_copy(idx_hbm.at[wid], idx_v)

        def gather(j, buf):
            return pltpu.make_async_copy(table_hbm.at[idx_v.at[j]], rows_v.at[buf], gsem.at[buf])

        def write(j, buf):
            base = pl.multiple_of((wid * n_steps + j) * window, window)
            return pltpu.make_async_copy(rows_v.at[buf], out_hbm.at[pl.ds(base, window)], wsem.at[buf])

        gather(0, 0).start()

        @pl.loop(0, n_steps, step=2)
        def _(j0):
            for buf in range(2):
                j = j0 + buf
                gather(j, buf).wait()

                @pl.when(j >= 1)
                def _():
                    write(j - 1, 1 - buf).wait()

                @pl.when(j + 1 < n_steps)
                def _():
                    gather(j + 1, 1 - buf).start()

                write(j, buf).start()

        write(n_steps - 1, 1).wait()

    return pl.kernel(
        body, out_type=jax.ShapeDtypeStruct((n, dim), table.dtype), mesh=mesh,
        scratch_types=[pltpu.VMEM((n_steps, window), jnp.int32), pltpu.VMEM((2, window, dim), table.dtype),
                       pltpu.SemaphoreType.DMA((2,)), pltpu.SemaphoreType.DMA((2,))],
        name="sc_gather",
    )(table, idx3)


def _sc_scatter_rows(src, dest, n_out, window):
    t, dim = src.shape
    k = dest.shape[0]
    n_steps = t // (SC_WORKERS * window)
    assert n_steps * window * SC_WORKERS == t and n_steps % 2 == 0 and window % SUBLANE == 0 and window <= LANE
    idx3 = dest.reshape(k, SC_WORKERS, n_steps, window).transpose(1, 2, 0, 3).reshape(SC_WORKERS, n_steps * k, window)
    mesh = plsc.VectorSubcoreMesh(core_axis_name="c", subcore_axis_name="s",
                                  num_cores=SC_CORES, num_subcores=SC_SUBCORES)

    def body(src_hbm, idx_hbm, out_hbm, idx_v, rows_v, rsem, ssem):
        wid = lax.axis_index("s") * SC_CORES + lax.axis_index("c")
        pltpu.sync_copy(idx_hbm.at[wid], idx_v)

        def read(s, buf):
            base = pl.multiple_of((wid * n_steps + s) * window, window)
            return pltpu.make_async_copy(src_hbm.at[pl.ds(base, window)], rows_v.at[buf], rsem.at[buf])

        def scatter(s, j, buf):
            return pltpu.make_async_copy(rows_v.at[buf], out_hbm.at[idx_v.at[s * k + j]], ssem.at[buf])

        read(0, 0).start()

        @pl.loop(0, n_steps, step=2)
        def _(s0):
            for buf in range(2):
                s = s0 + buf
                read(s, buf).wait()

                @pl.when(s >= 1)
                def _():
                    for j in range(k):
                        scatter(s - 1, j, 1 - buf).wait()

                @pl.when(s + 1 < n_steps)
                def _():
                    read(s + 1, 1 - buf).start()

                for j in range(k):
                    scatter(s, j, buf).start()

        for j in range(k):
            scatter(n_steps - 1, j, 1).wait()

    return pl.kernel(
        body, out_type=jax.ShapeDtypeStruct((n_out, dim), src.dtype), mesh=mesh,
        scratch_types=[pltpu.VMEM((n_steps * k, window), jnp.int32), pltpu.VMEM((2, window, dim), src.dtype),
                       pltpu.SemaphoreType.DMA((2,)), pltpu.SemaphoreType.DMA((2,))],
        name="sc_scatter",
    )(src, idx3)


def _pad_cols(w, width):
    return jnp.pad(w, ((0, 0), (0, width - w.shape[1])))


def kernel(x, w_in, gmlp_ln_g, gmlp_ln_b, gmlp_ws, gmlp_bs, rwkv_mu, rwkv_w0, rwkv_w2, rwkv_a0, rwkv_a2, rwkv_g2, rwkv_k_k, rwkv_k_a, rwkv_r_k, rwkv_ln_g, rwkv_ln_b, mlstm_conv_w, mlstm_conv_b, mlstm_gate_b, mlstm_ln_g, w_out, ln1_g, ln1_b, router_w, router_b, exp_w1, exp_b1, exp_w2, exp_b2, ln2_g, ln2_b):
    batch, seq, dm = x.shape
    depth = w_in.shape[0]
    sb = batch // N_STREAMS if batch % N_STREAMS == 0 else batch
    t = sb * seq
    gw = gmlp_ln_g.shape[1]
    rw = rwkv_w0.shape[2]
    mw = mlstm_ln_g.shape[1]
    g_proj = 2 * gw
    r_proj = 3 * rw + W_LORA + A_LORA + G_LORA
    alpha = (2 * depth) ** 0.25
    n_blocks = -(-t * TOP_K // MOE_BLOCK) + N_EXPERTS
    streams = [x[i * sb:(i + 1) * sb].reshape(t, dm) for i in range(batch // sb)]
    for l in range(depth):
        mixed = []
        for xf in streams:
            pg, pr, pm = _proj(xf, w_in, l, g_proj, r_proj)
            y_g = _gmlp(pg, gmlp_ln_g[l], gmlp_ln_b[l], gmlp_ws[l], gmlp_bs[l])
            r, v, a, kd, b, lw, bonus, rgate = _rwkv_prep(
                pr, seq, rwkv_mu[l], rwkv_w0[l], rwkv_w2[l], rwkv_a0[l], rwkv_a2[l], rwkv_g2[l],
                rwkv_k_k[l], rwkv_k_a[l], rwkv_r_k[l].reshape(-1))
            ro = _rwkv_scan(r, v, a, kd, b, lw, sb, seq)
            q, k, gates = _mlstm_prep(pm, seq, mlstm_conv_w[l], mlstm_conv_b[l], mlstm_gate_b[l], mw)
            mh = _mlstm_scan(q, k, pm, gates, sb, seq)
            mixed.append(_mix_out(xf, y_g, ro, bonus, rgate, rwkv_ln_g[l], rwkv_ln_b[l], mh, pm, mlstm_ln_g[l],
                                  w_out, l, ln1_g[l], ln1_b[l], router_w[l], router_b[l], alpha))
        streams = []
        for x1, x1p, topi, gate in mixed:
            dest, block_e, valid_end, n_used = _moe_plan(topi, t, n_blocks)
            xs = _sc_scatter_rows(x1p, dest, n_blocks * MOE_BLOCK, window=64)
            ys = _moe_experts(xs, block_e, n_used, valid_end, exp_w1, exp_b1, exp_w2, exp_b2, l)
            yg = _sc_gather_rows(ys, dest.reshape(-1), window=64)
            streams.append(_combine(x1, yg, gate, ln2_g[l], ln2_b[l], alpha))
    return jnp.concatenate(streams, axis=0).reshape(batch, seq, dm)
```

```python
import functools
import math

import jax
import jax.numpy as jnp
from jax import lax
from jax.experimental import pallas as pl
from jax.experimental.pallas import tpu as pltpu
from jax.experimental.pallas import tpu_sc as plsc

F32 = jnp.float32
BF16 = jnp.bfloat16
HI = lax.Precision.HIGHEST

HEAD_DIM = 64
GMLP_CHUNK = 128
MLSTM_CHUNK = 128
RWKV_CHUNK = 64
W_LORA = 64
A_LORA = 64
G_LORA = 128
N_EXPERTS = 32
TOP_K = 4
MOE_BLOCK = 512
SWIGLU_LIMIT = 7.0
SWIGLU_ALPHA = 1.702
LN_EPS = 1e-5
RWKV_GN_EPS = 64e-5
LANE = 128
SUBLANE = 8
VMEM_LIMIT = 48 * 1024 * 1024
NEG_BIG = -1e30


def _cparams(n_axes):
    return pltpu.CompilerParams(dimension_semantics=("arbitrary",) * n_axes,
                                vmem_limit_bytes=VMEM_LIMIT)


def _full(shape):
    return pl.BlockSpec(shape, lambda *_: (0,) * len(shape))


def _dot(a, b, precision=None):
    return jnp.dot(a, b, preferred_element_type=F32, precision=precision)


def _dot_nt(a, b, precision=None):
    return lax.dot_general(a, b, (((1,), (1,)), ((), ())), preferred_element_type=F32, precision=precision)


def _dot_tn(a, b, precision=None):
    return lax.dot_general(a, b, (((0,), (0,)), ((), ())), preferred_element_type=F32, precision=precision)


def _split(x):
    hi = x.astype(BF16)
    return hi, (x - hi.astype(F32)).astype(BF16)


def _split3(x):
    hi = x.astype(BF16)
    r1 = x - hi.astype(F32)
    mid = r1.astype(BF16)
    return hi, mid, (r1 - mid.astype(F32)).astype(BF16)


def _mm(a, b, mode, dot=_dot):
    if mode == "hi":
        return dot(a, b, HI)
    if mode == "b1":
        return dot(a.astype(BF16), b.astype(BF16))
    bh, bl = _split(b)
    if mode == "b2":
        ah = a.astype(BF16)
        return dot(ah, bh) + dot(ah, bl)
    ah, al = _split(a)
    return dot(ah, bh) + (dot(ah, bl) + dot(al, bh))


def _dot_exact_lhs(a_bf16, x):
    hi, mid, lo = _split3(x)
    return _dot(a_bf16, hi) + (_dot(a_bf16, mid) + _dot(a_bf16, lo))


def _dot_exact_rhs(x, b_bf16, terms=3):
    if terms == 2:
        hi, lo = _split(x)
        return _dot(hi, b_bf16) + _dot(lo, b_bf16)
    hi, mid, lo = _split3(x)
    return _dot(hi, b_bf16) + (_dot(mid, b_bf16) + _dot(lo, b_bf16))


def _pack_bf16_pairs(x):
    n = x.shape[1] // 2
    lo = pltpu.bitcast(x[:, :n].astype(BF16).astype(F32), jnp.uint32)
    hi = pltpu.bitcast(x[:, n:].astype(BF16).astype(F32), jnp.uint32)
    return hi | (lo >> 16)


def _unpack_bf16_pairs(w):
    lo = pltpu.bitcast(w << 16, F32)
    hi = pltpu.bitcast(w & jnp.uint32(0xFFFF0000), F32)
    return lo, hi


def _sigmoid(x):
    return 1.0 / (1.0 + jnp.exp(-x))


def _softplus(x):
    return jnp.maximum(x, 0.0) + jnp.log1p(jnp.exp(-jnp.abs(x)))


def _block_diag_ones(width):
    h = jnp.arange(width) // HEAD_DIM
    return (h[:, None] == h[None, :]).astype(F32)


CAST_ROWS = 128


def _cast_rows(src_ref, dst_ref):
    n_src, n_dst = src_ref.shape[1], dst_ref.shape[1]
    whole = n_src // LANE * LANE

    def step(r, carry):
        rows = pl.ds(pl.multiple_of(r * CAST_ROWS, CAST_ROWS), CAST_ROWS)
        dst_ref[rows, :whole] = src_ref[rows, :whole].astype(BF16)
        if n_dst > whole:
            tail = [src_ref[rows, whole:]] if n_src > whole else []
            tail.append(jnp.zeros((CAST_ROWS, n_dst - n_src), F32))
            dst_ref[rows, whole:] = jnp.concatenate(tail, axis=1).astype(BF16)
        return carry
    lax.fori_loop(0, src_ref.shape[0] // CAST_ROWS, step, 0)


def _gmlp_gate(p, lng_ref, lnb_ref, ws_ref, bst_ref, o_ref):
    gw = p.shape[1] // 2
    p = 0.5 * p * (1.0 + lax.erf(p * math.sqrt(0.5)))
    u, v = p[:, :gw], p[:, gw:]
    mu = jnp.mean(v, axis=-1, keepdims=True)
    vc = v - mu
    var = jnp.mean(vc * vc, axis=-1, keepdims=True)
    vn = vc * lax.rsqrt(var + LN_EPS) * lng_ref[...] + lnb_ref[...]
    for c in range(p.shape[0] // GMLP_CHUNK):
        rows = slice(c * GMLP_CHUNK, (c + 1) * GMLP_CHUNK)
        ys = []
        for h in range(gw // HEAD_DIM):
            cols = slice(h * HEAD_DIM, (h + 1) * HEAD_DIM)
            ys.append(_dot(ws_ref[h], vn[rows, cols].astype(BF16)) + bst_ref[:, h:h + 1])
        o_ref[rows, :] = u[rows, :] * jnp.concatenate(ys, axis=1)


def _proj_body(x_ref, w_ref, lng_ref, lnb_ref, ws_ref, bst_ref, yg_ref, pr_ref, pm_ref, wb_ref, *, ng, nr):
    @pl.when(pl.program_id(0) == 0)
    def _():
        _cast_rows(w_ref, wb_ref)

    xb = x_ref[...].astype(BF16)
    _gmlp_gate(_dot(xb, wb_ref[:, :ng]), lng_ref, lnb_ref, ws_ref, bst_ref, yg_ref)
    pr_ref[...] = _dot(xb, wb_ref[:, ng:ng + nr])
    pm_ref[...] = _dot(xb, wb_ref[:, ng + nr:])


def _proj(x, w_in, layer, ng, nr, ln_g, ln_b, ws, bs, tm=512):
    t, d = x.shape
    p_in = w_in.shape[2]
    p_pad = -(-p_in // LANE) * LANE
    nm = p_pad - ng - nr
    gw = ng // 2
    n_heads = gw // HEAD_DIM
    bst = jnp.zeros((GMLP_CHUNK, LANE), F32).at[:, :n_heads].set(bs.T)
    row = lambda n: pl.BlockSpec((tm, n), lambda i: (i, 0))
    return pl.pallas_call(
        functools.partial(_proj_body, ng=ng, nr=nr),
        grid=(t // tm,),
        in_specs=[row(d), pl.BlockSpec((None, d, p_in), lambda i: (layer, 0, 0), pipeline_mode=pl.Buffered(1)),
                  _full((1, gw)), _full((1, gw)), _full((n_heads, GMLP_CHUNK, GMLP_CHUNK)),
                  _full((GMLP_CHUNK, LANE))],
        out_specs=[row(gw), row(nr), row(nm)],
        out_shape=[jax.ShapeDtypeStruct((t, n), F32) for n in (gw, nr, nm)],
        scratch_shapes=[pltpu.VMEM((d, p_pad), BF16)],
        compiler_params=_cparams(1),
        name="in_proj",
    )(x, w_in, ln_g.reshape(1, gw), ln_b.reshape(1, gw), ws.astype(BF16), bst)


def _halo_specs(tm, width, n_rows):
    per8 = tm // SUBLANE
    last = n_rows // SUBLANE - 1
    prev = pl.BlockSpec((SUBLANE, width), lambda i: (jnp.maximum(i * per8 - 1, 0), 0))
    nxt = pl.BlockSpec((SUBLANE, width), lambda i: (jnp.minimum((i + 1) * per8, last), 0))
    return prev, nxt


def _neighbours(cur, prev_blk, next_blk, tiles_per_seq):
    tm = cur.shape[0]
    j = pl.program_id(0) % tiles_per_seq
    prev_row = jnp.where(j > 0, prev_blk[SUBLANE - 1:SUBLANE, :], 0.0)
    next_row = jnp.where(j < tiles_per_seq - 1, next_blk[0:1, :], 0.0)
    ridx = lax.broadcasted_iota(jnp.int32, cur.shape, 0)
    before = jnp.where(ridx == 0, prev_row, pltpu.roll(cur, 1, 0))
    after = jnp.where(ridx == tm - 1, next_row, pltpu.roll(cur, tm - 1, 0))
    return before, after


def _rwkv_prep_body(pr_ref, prev_ref, next_ref, mu_ref, w0_ref, w2_ref, a0_ref, a2_ref, g2_ref,
                    kk_ref, ka_ref, rk_ref, bd_ref,
                    r_out, v_out, a_out, kd_out, b_out, lw_out, bonus_out, gate_out, *, rw, tiles_per_seq):
    pf = pr_ref[...]
    before, after = _neighbours(pf, prev_ref[...], next_ref[...], tiles_per_seq)
    pf = pf + mu_ref[0:1, :] * (before - pf) + mu_ref[1:2, :] * (after - pf)
    o3 = 3 * rw
    r, k, v = pf[:, :rw], pf[:, rw:2 * rw], pf[:, 2 * rw:o3]
    wd = pf[:, o3:o3 + W_LORA]
    ad = pf[:, o3 + W_LORA:o3 + W_LORA + A_LORA]
    gd = pf[:, o3 + W_LORA + A_LORA:]
    bd = bd_ref[...]
    kk = k * kk_ref[...]
    ss = _dot_exact_rhs(kk * kk, bd, terms=2)
    kk = kk / jnp.maximum(jnp.sqrt(ss), 1e-12)
    twd = jnp.tanh(wd)
    ksum = jnp.zeros_like(k)
    for d in range(2):
        w_log = -_softplus(-(w0_ref[d:d + 1, :] + _mm(twd, w2_ref[d], "b3"))) - 0.5
        lw_out[d] = -jnp.exp(w_log)
        iclr = _sigmoid(a0_ref[d:d + 1, :] + _mm(ad, a2_ref[d], "b3"))
        kd = k * (1.0 + (iclr - 1.0) * ka_ref[...])
        kd_out[d] = kd
        b_out[d] = kk * iclr
        ksum = ksum + kd
    r_out[...] = r
    v_out[...] = v
    a_out[...] = -kk
    bonus_out[...] = _dot_exact_rhs(r * ksum * rk_ref[...], bd, terms=2) * v
    gate_out[...] = _dot(_sigmoid(gd).astype(BF16), g2_ref[...])


def _rwkv_prep(pr, seq, mu, w0, w2, a0, a2, g2, k_k, k_a, r_k, tm=256):
    t, rproj = pr.shape
    rw = w0.shape[1]
    tiles_per_seq = seq // tm
    prev, nxt = _halo_specs(tm, rproj, t)
    row = pl.BlockSpec((tm, rw), lambda i: (i, 0))
    row2 = pl.BlockSpec((2, tm, rw), lambda i: (0, i, 0))
    one = jax.ShapeDtypeStruct((t, rw), F32)
    two = jax.ShapeDtypeStruct((2, t, rw), F32)
    return pl.pallas_call(
        functools.partial(_rwkv_prep_body, rw=rw, tiles_per_seq=tiles_per_seq),
        grid=(t // tm,),
        in_specs=[pl.BlockSpec((tm, rproj), lambda i: (i, 0)), prev, nxt,
                  _full((2, rproj)), _full((2, rw)), _full((2, W_LORA, rw)), _full((2, rw)),
                  _full((2, A_LORA, rw)), _full((G_LORA, rw)), _full((1, rw)), _full((1, rw)),
                  _full((1, rw)), _full((rw, rw))],
        out_specs=[row, row, row, row2, row2, row2, row, row],
        out_shape=[one, one, one, two, two, two, one, one],
        compiler_params=_cparams(1),
        name="rwkv_prep",
    )(pr, pr, pr, mu, w0, w2, a0, a2, g2.astype(BF16), k_k.reshape(1, rw), k_a.reshape(1, rw),
      r_k.reshape(1, rw), _block_diag_ones(rw).astype(BF16))


P_G, P_INV, P_APPLY, P_STATE, P_SEQ = "b1", "b1", "b1", "b1", "b2"


def _rwkv_intra_body(r_ref, v_ref, a_ref, kd_ref, b_ref, lw_ref, rq_out, o0_out, mtx_out, hc_out,
                     *, n_heads, chunks):
    L = RWKV_CHUNK
    d = pl.program_id(0)
    row = lax.broadcasted_iota(jnp.int32, (L, L), 0)
    col = lax.broadcasted_iota(jnp.int32, (L, L), 1)
    fwd = d == 0
    rel = (col - row) * (1 - 2 * d)
    incl = rel <= 0
    strict = rel < 0
    eye = (row == col).astype(F32)
    tri = incl.astype(BF16)
    pairs = []
    for c in range(chunks):
        rows = slice(c * L, (c + 1) * L)
        lw = lw_ref[rows, :]
        cum = _dot_exact_lhs(tri, lw)
        tot = jnp.where(fwd, cum[L - 1:L, :], cum[0:1, :])
        e_neg = jnp.exp(-cum)
        e_end = jnp.exp(tot - cum)
        e_tot = jnp.exp(tot)
        r, v, a, kd, b = r_ref[rows, :], v_ref[rows, :], a_ref[rows, :], kd_ref[rows, :], b_ref[rows, :]
        at, rt, bt, kt = a * jnp.exp(cum - lw), r * jnp.exp(cum), b * e_neg, kd * e_neg
        kend, bend = kd * e_end, b * e_end
        for h in range(n_heads):
            sl = slice(h * HEAD_DIM, (h + 1) * HEAD_DIM)
            pairs.append(dict(at=at[:, sl], rt=rt[:, sl], bt=bt[:, sl], kt=kt[:, sl], v=v[:, sl],
                              kend=kend[:, sl], bend=bend[:, sl], e_tot=e_tot[:, sl]))
    for p in pairs:
        p["g"] = _mm(jnp.concatenate([p["at"], p["rt"]], axis=0),
                     jnp.concatenate([p["bt"], p["kt"]], axis=0), P_G, _dot_nt)
    row2 = lax.broadcasted_iota(jnp.int32, (L, 2 * L), 0)
    col2 = lax.broadcasted_iota(jnp.int32, (L, 2 * L), 1) & (L - 1)
    rel2 = (col2 - row2) * (1 - 2 * d)
    incl2 = rel2 <= 0
    strict2 = rel2 < 0
    zeros = jnp.zeros((L, HEAD_DIM), F32)
    for p in pairs:
        g = p.pop("g")
        a_both = jnp.where(strict2, g[:L, :], 0.0)
        p["m_both"] = jnp.where(incl2, g[L:, :], 0.0)
        p["pw"] = a_both[:, :L]
        p["a_ak"] = a_both[:, L:]
        p["inv"] = eye + p["pw"]
    for _ in range(int(math.log2(L)) - 1):
        for p in pairs:
            p["pw"] = _mm(p["pw"], p["pw"], P_INV)
        for p in pairs:
            p["inv"] = p["inv"] + _mm(p["inv"], p["pw"], P_INV)
    for p in pairs:
        p["akv"] = _mm(p["a_ak"], p["v"], P_APPLY)
    for p in pairs:
        wu = _mm(p["inv"], jnp.concatenate([p["at"], p["akv"]], axis=1), P_APPLY)
        p["rhs"] = jnp.concatenate([wu, jnp.concatenate([zeros, p["v"]], axis=1)], axis=0)
    for p in pairs:
        p["rq_o0"] = _mm(p["m_both"], p["rhs"], P_APPLY)
    for p in pairs:
        p["m_hc"] = _mm(jnp.concatenate([p["bend"], p["kend"]], axis=0), p["rhs"], P_STATE, _dot_tn)
    for c in range(chunks):
        ps = pairs[c * n_heads:(c + 1) * n_heads]
        rows = slice(c * L, (c + 1) * L)
        krows = slice(c * HEAD_DIM, (c + 1) * HEAD_DIM)
        rq_out[rows, :] = jnp.concatenate([p["rt"] + p["rq_o0"][:, :HEAD_DIM] for p in ps], axis=1)
        o0_out[rows, :] = jnp.concatenate([p["rq_o0"][:, HEAD_DIM:] for p in ps], axis=1)
        mtx_out[krows, :] = jnp.concatenate([eye * p["e_tot"] + p["m_hc"][:, :HEAD_DIM] for p in ps], axis=1)
        hc_out[krows, :] = jnp.concatenate([p["m_hc"][:, HEAD_DIM:] for p in ps], axis=1)


def _rwkv_intra(r, v, a, kd, b, lw, chunks=4):
    t, rw = r.shape
    n_heads = rw // HEAD_DIM
    tm = chunks * RWKV_CHUNK
    tk = chunks * HEAD_DIM
    n_tiles = t // tm
    one = pl.BlockSpec((tm, rw), lambda d, i: (i, 0))
    two = pl.BlockSpec((None, tm, rw), lambda d, i: (d, i, 0))
    twok = pl.BlockSpec((None, tk, rw), lambda d, i: (d, i, 0))
    return pl.pallas_call(
        functools.partial(_rwkv_intra_body, n_heads=n_heads, chunks=chunks),
        grid=(2, n_tiles),
        in_specs=[one, one, one, two, two, two],
        out_specs=[two, two, twok, twok],
        out_shape=[jax.ShapeDtypeStruct((2, t, rw), F32), jax.ShapeDtypeStruct((2, t, rw), F32),
                   jax.ShapeDtypeStruct((2, n_tiles * tk, rw), F32),
                   jax.ShapeDtypeStruct((2, n_tiles * tk, rw), F32)],
        compiler_params=_cparams(2),
        name="rwkv_intra",
    )(r, v, a, kd, b, lw)


def _rwkv_seq_body(rq0, o00, mtx0, hc0, rq1, o01, mtx1, hc1, out0, out1, h_ref, *, n_heads, batch):
    c = pl.program_id(0)

    @pl.when(c == 0)
    def _():
        h_ref[...] = jnp.zeros_like(h_ref)

    L = RWKV_CHUNK
    for d, (rq, o0, mtx, hc, out) in enumerate(((rq0, o00, mtx0, hc0, out0), (rq1, o01, mtx1, hc1, out1))):
        for bi in range(batch):
            rq_t, mtx_t = rq[bi], mtx[bi]
            state = h_ref[d, bi]
            outs, states = [], []
            for h in range(n_heads):
                sl = slice(h * HEAD_DIM, (h + 1) * HEAD_DIM)
                prod = _mm(jnp.concatenate([rq_t[:, sl], mtx_t[:, sl]], axis=0), state[:, sl], P_SEQ)
                outs.append(prod[:L])
                states.append(prod[L:])
            out[bi] = jnp.concatenate(outs, axis=1) + o0[bi]
            h_ref[d, bi] = jnp.concatenate(states, axis=1) + hc[bi]


def _rwkv_seq(rq, o0, mtx, hc, batch, seq):
    _, t, rw = rq.shape
    n_heads = rw // HEAD_DIM
    L = RWKV_CHUNK
    nc = seq // L
    as4 = lambda x: x.reshape(2, batch, x.shape[1] // batch, rw)
    rq, o0, mtx, hc = as4(rq), as4(o0), as4(mtx), as4(hc)
    fwd = lambda rows: pl.BlockSpec((None, batch, rows, rw), lambda c: (0, 0, c, 0))
    bwd = lambda rows: pl.BlockSpec((None, batch, rows, rw), lambda c: (1, 0, nc - 1 - c, 0))
    out0, out1 = pl.pallas_call(
        functools.partial(_rwkv_seq_body, n_heads=n_heads, batch=batch),
        grid=(nc,),
        in_specs=[fwd(L), fwd(L), fwd(HEAD_DIM), fwd(HEAD_DIM), bwd(L), bwd(L), bwd(HEAD_DIM), bwd(HEAD_DIM)],
        out_specs=[pl.BlockSpec((batch, L, rw), lambda c: (0, c, 0)),
                   pl.BlockSpec((batch, L, rw), lambda c: (0, nc - 1 - c, 0))],
        out_shape=[jax.ShapeDtypeStruct((batch, seq, rw), F32)] * 2,
        scratch_shapes=[pltpu.VMEM((2, batch, HEAD_DIM, rw), F32)],
        compiler_params=_cparams(1),
        name="rwkv_seq",
    )(rq, o0, mtx, hc, rq, o0, mtx, hc)
    return out0.reshape(t, rw), out1.reshape(t, rw)


def _rwkv_scan(r, v, a, kd, b, lw, batch, seq):
    rq, o0, mtx, hc = _rwkv_intra(r, v, a, kd, b, lw)
    return _rwkv_seq(rq, o0, mtx, hc, batch, seq)


def _mlstm_prep_body(qk_ref, prev_ref, next_ref, g_ref, cw_ref, cb_ref, gb_ref, q_out, k_out, gate_out,
                     *, mw, n_heads, tiles_per_seq):
    x = qk_ref[...]
    before, after = _neighbours(x, prev_ref[...], next_ref[...], tiles_per_seq)
    y = cb_ref[...] + before * cw_ref[0:1, :] + x * cw_ref[1:2, :] + after * cw_ref[2:3, :]
    y = y * _sigmoid(y)
    q_out[...] = y[:, :mw]
    k_out[...] = y[:, mw:] * (HEAD_DIM ** -0.5)
    g = g_ref[...] + gb_ref[...]
    lane = lax.broadcasted_iota(jnp.int32, g.shape, 1)
    for d in range(2):
        ig = g if d == 0 else pltpu.roll(g, LANE - n_heads, 1)
        fg = pltpu.roll(g, LANE - (1 + d) * n_heads, 1)
        lf = -_softplus(-fg)
        gate_out[d] = jnp.where(lane < n_heads, ig, jnp.where(lane < 2 * n_heads, lf, 0.0))


def _mlstm_prep(pm, seq, conv_w, conv_b, gate_b, mw, tm=256):
    t = pm.shape[0]
    n_heads = mw // HEAD_DIM
    tiles_per_seq = seq // tm
    w2 = 2 * mw
    prev, nxt = _halo_specs(tm, w2, t)
    gcol = (4 * mw) // LANE
    gb = jnp.zeros((1, LANE), F32).at[0, :4 * n_heads].set(gate_b)
    row = pl.BlockSpec((tm, mw), lambda i: (i, 0))
    return pl.pallas_call(
        functools.partial(_mlstm_prep_body, mw=mw, n_heads=n_heads, tiles_per_seq=tiles_per_seq),
        grid=(t // tm,),
        in_specs=[pl.BlockSpec((tm, w2), lambda i: (i, 0)), prev, nxt,
                  pl.BlockSpec((tm, LANE), lambda i: (i, gcol)),
                  _full((3, w2)), _full((1, w2)), _full((1, LANE))],
        out_specs=[row, row, pl.BlockSpec((2, tm, LANE), lambda i: (0, i, 0))],
        out_shape=[jax.ShapeDtypeStruct((t, mw), F32), jax.ShapeDtypeStruct((t, mw), F32),
                   jax.ShapeDtypeStruct((2, t, LANE), F32)],
        compiler_params=_cparams(1),
        name="mlstm_prep",
    )(pm, pm, pm, pm, conv_w, conv_b.reshape(1, w2), gb)


def _mlstm_scan_body(q0_ref, k0_ref, v0_ref, g0_ref, q1_ref, k1_ref, v1_ref, g1_ref, o0_ref, o1_ref,
                     c_ref, m_ref, *, n_heads):
    L = MLSTM_CHUNK
    H = n_heads

    @pl.when(pl.program_id(1) == 0)
    def _():
        c_ref[...] = jnp.zeros_like(c_ref)
        m_ref[...] = jnp.zeros_like(m_ref)

    row = lax.broadcasted_iota(jnp.int32, (L, L), 0)
    col = lax.broadcasted_iota(jnp.int32, (L, L), 1)
    trow = lax.broadcasted_iota(jnp.int32, (L, LANE), 0)
    low = lax.broadcasted_iota(jnp.int32, (L, LANE), 1) < HEAD_DIM
    xr = lax.broadcasted_iota(jnp.int32, (LANE, H * L), 0)
    xc = lax.broadcasted_iota(jnp.int32, (LANE, H * L), 1)
    spread = (xr - H == lax.shift_right_logical(xc, int(math.log2(L)))).astype(BF16)
    hs = []
    for d, (q_ref, k_ref, v_ref, g_ref) in enumerate(((q0_ref, k0_ref, v0_ref, g0_ref),
                                                      (q1_ref, k1_ref, v1_ref, g1_ref))):
        incl = (col <= row) if d == 0 else (col >= row)
        last = L - 1 if d == 0 else 0
        g = g_ref[...]
        bcum = _dot_exact_lhs(incl.astype(BF16), g)
        z = pltpu.roll(g, H, 1) - bcum
        cmax = z
        shift = 1
        while shift < L:
            if d == 0:
                moved = jnp.where(trow >= shift, pltpu.roll(cmax, shift, 0), -jnp.inf)
            else:
                moved = jnp.where(trow < L - shift, pltpu.roll(cmax, L - shift, 0), -jnp.inf)
            cmax = jnp.maximum(cmax, moved)
            shift *= 2
        m_prev = m_ref[d, 0:1, :]
        top = jnp.maximum(cmax, m_prev)
        b_last = bcum[last:last + 1, :]
        lwc = b_last + z
        m_new = jnp.maximum(b_last + m_prev, jnp.max(lwc, axis=0, keepdims=True))
        m_ref[d, 0:1, :] = m_new
        alpha_w = _dot_exact_rhs(-top, spread)
        floor_w = jnp.exp(-_dot_exact_rhs(bcum + top, spread, terms=2))
        wts_w = _dot(jnp.exp(lwc - m_new).astype(BF16), spread)
        rows_w = _dot_exact_rhs(jnp.concatenate(
            [jnp.broadcast_to(m_prev, (SUBLANE, LANE)),
             jnp.broadcast_to(jnp.exp(b_last + m_prev - m_new), (SUBLANE, LANE))], axis=0), spread)
        z_t = z.T
        q, k, v = q_ref[...], k_ref[...], v_ref[...]
        for h in range(H):
            slab = slice(h // 2 * LANE, (h // 2 + 1) * LANE)
            cols = slice(h * L, (h + 1) * L)
            mine = low if h % 2 == 0 else jnp.logical_not(low)
            kh = jnp.where(mine, k[:, slab], 0.0)
            hs.append(dict(
                qh=jnp.where(mine, q[:, slab], 0.0).astype(BF16), kh=kh.astype(BF16),
                vext=jnp.where(mine, v[:, slab], 1.0).astype(BF16),
                decay=jnp.exp(jnp.where(incl, alpha_w[:, cols] + z_t[H + h:H + h + 1, :], -jnp.inf)),
                w_inter=jnp.exp(alpha_w[:, cols] + rows_w[0:1, cols]), floor=floor_w[:, cols],
                wk=(wts_w[:, cols] * kh).astype(BF16), dec=rows_w[SUBLANE:SUBLANE + 1, cols],
                cst=c_ref[d, h]))
    for p in hs:
        p["sc"] = (_dot_nt(p["qh"], p["kh"]) * p["decay"]).astype(BF16)
    for p in hs:
        p["numext"] = _dot(p["sc"], p["vext"]) + p["w_inter"] * _dot(p["qh"], p["cst"].astype(BF16))
    for p in hs:
        p["upd"] = _dot_tn(p["wk"], p["vext"])
    for d, o_ref in enumerate((o0_ref, o1_ref)):
        res = []
        for h in range(H):
            p = hs[d * H + h]
            den = pltpu.roll(p["numext"], HEAD_DIM, 1)
            res.append(p["numext"] / jnp.maximum(jnp.abs(den), p["floor"]))
            c_ref[d, h] = p["dec"] * p["cst"] + p["upd"]
        for pair in range(H // 2):
            o_ref[:, pair * LANE:(pair + 1) * LANE] = jnp.where(low, res[2 * pair], res[2 * pair + 1])


def _mlstm_scan(q, k, pm, gates, batch, seq):
    t, mw = q.shape
    n_heads = mw // HEAD_DIM
    L = MLSTM_CHUNK
    nc = seq // L
    fwd = lambda bi, c: bi * nc + c
    bwd = lambda bi, c: bi * nc + nc - 1 - c
    specs = []
    for d, blk in enumerate((fwd, bwd)):
        specs += [pl.BlockSpec((L, mw), lambda bi, c, blk=blk: (blk(bi, c), 0)),
                  pl.BlockSpec((L, mw), lambda bi, c, blk=blk: (blk(bi, c), 0)),
                  pl.BlockSpec((L, mw), lambda bi, c, blk=blk: (blk(bi, c), 2)),
                  pl.BlockSpec((None, L, LANE), lambda bi, c, blk=blk, d=d: (d, blk(bi, c), 0))]
    return pl.pallas_call(
        functools.partial(_mlstm_scan_body, n_heads=n_heads),
        grid=(batch, nc),
        in_specs=specs,
        out_specs=[pl.BlockSpec((L, mw), lambda bi, c: (fwd(bi, c), 0)),
                   pl.BlockSpec((L, mw), lambda bi, c: (bwd(bi, c), 0))],
        out_shape=[jax.ShapeDtypeStruct((t, mw), F32)] * 2,
        scratch_shapes=[pltpu.VMEM((2, n_heads, LANE, LANE), F32), pltpu.VMEM((2, SUBLANE, LANE), F32)],
        compiler_params=_cparams(2),
        name="mlstm_scan",
    )(q, k, pm, gates, q, k, pm, gates)


def _layer_norm(x, g, b):
    mu = jnp.mean(x, axis=-1, keepdims=True)
    xc = x - mu
    var = jnp.mean(xc * xc, axis=-1, keepdims=True)
    return xc * lax.rsqrt(var + LN_EPS) * g + b


def _head_norm(x, bd_mean, eps):
    mu = _dot_exact_rhs(x, bd_mean, terms=2)
    xc = x - mu
    var = _dot_exact_rhs(xc * xc, bd_mean, terms=2)
    return xc * lax.rsqrt(var + eps)


def _mix_out_body(x_ref, yg_ref, ro0_ref, ro1_ref, bonus_ref, rgate_ref, rlg_ref, rlb_ref, mh0_ref, mh1_ref, og_ref,
                  mlg_ref, w_ref, l1g_ref, l1b_ref, rw_ref, rb_ref, bdm_ref,
                  x1_out, x1p_out, topi_out, gate_out, wb_ref, *, alpha, gw, rw):
    @pl.when(pl.program_id(0) == 0)
    def _():
        _cast_rows(w_ref, wb_ref)

    bdm = bdm_ref[...]
    yr = _head_norm(ro0_ref[...] + ro1_ref[...], bdm, RWKV_GN_EPS) * rlg_ref[...] + rlb_ref[...]
    yr = (yr + bonus_ref[...]) * rgate_ref[...]
    ym = _sigmoid(og_ref[...]) * (_head_norm(mh0_ref[...] + mh1_ref[...], bdm, LN_EPS) * mlg_ref[...])
    mix = (_dot(yg_ref[...].astype(BF16), wb_ref[:gw, :]) + _dot(yr.astype(BF16), wb_ref[gw:gw + rw, :])
           + _dot(ym.astype(BF16), wb_ref[gw + rw:, :]))
    x1 = _layer_norm(alpha * x_ref[...] + mix, l1g_ref[...], l1b_ref[...])
    x1_out[...] = x1
    x1p_out[...] = _pack_bf16_pairs(x1)
    lg = _mm(x1, rw_ref[...], "b3") + rb_ref[...]
    lane = lax.broadcasted_iota(jnp.int32, lg.shape, 1)
    vals, topi = [], jnp.zeros(lg.shape, jnp.int32)
    for j in range(TOP_K):
        mx = jnp.max(lg, axis=1, keepdims=True)
        idx = jnp.min(jnp.where(lg == mx, lane, LANE), axis=1, keepdims=True)
        vals.append(mx)
        topi = jnp.where(lane == j, idx, topi)
        lg = jnp.where(lane == idx, -jnp.inf, lg)
    es = [jnp.exp(vj - vals[0]) for vj in vals]
    den = es[0] + es[1] + es[2] + es[3]
    gate = jnp.zeros(lg.shape, F32)
    for j in range(TOP_K):
        gate = jnp.where(lane == j, es[j] / den, gate)
    topi_out[...] = topi.T[:SUBLANE, :]
    gate_out[...] = gate


def _mix_out(x, yg, ro, bonus, rgate, rlg, rlb, mh, pm, mlg, w_out, layer, l1g, l1b, router_w, router_b, alpha,
             tm=256):
    t, dm = x.shape
    gw, rw, mw = yg.shape[1], bonus.shape[1], mh[0].shape[1]
    assert rw == mw
    rwp = jnp.zeros((dm, LANE), F32).at[:, :N_EXPERTS].set(router_w)
    rbp = jnp.full((1, LANE), NEG_BIG, F32).at[0, :N_EXPERTS].set(router_b)
    row = lambda n: pl.BlockSpec((tm, n), lambda i: (i, 0))
    vec = lambda n: _full((1, n))
    return pl.pallas_call(
        functools.partial(_mix_out_body, alpha=alpha, gw=gw, rw=rw),
        grid=(t // tm,),
        in_specs=[row(dm), row(gw), row(rw), row(rw), row(rw), row(rw), vec(rw), vec(rw), row(mw), row(mw),
                  pl.BlockSpec((tm, mw), lambda i: (i, 3)),
                  vec(mw),
                  pl.BlockSpec((None, dm, dm), lambda i: (layer, 0, 0), pipeline_mode=pl.Buffered(1)),
                  vec(dm), vec(dm), _full((dm, LANE)), vec(LANE), _full((rw, rw))],
        out_specs=[row(dm), row(dm // 2), pl.BlockSpec((SUBLANE, tm), lambda i: (0, i)), row(LANE)],
        out_shape=[jax.ShapeDtypeStruct((t, dm), F32), jax.ShapeDtypeStruct((t, dm // 2), jnp.uint32),
                   jax.ShapeDtypeStruct((SUBLANE, t), jnp.int32), jax.ShapeDtypeStruct((t, LANE), F32)],
        scratch_shapes=[pltpu.VMEM((dm, dm), BF16)],
        compiler_params=_cparams(1),
        name="mix_out",
    )(x, yg, ro[0], ro[1], bonus, rgate, rlg.reshape(1, rw), rlb.reshape(1, rw), mh[0], mh[1], pm,
      mlg.reshape(1, mw), w_out, l1g.reshape(1, dm), l1b.reshape(1, dm), rwp, rbp,
      (_block_diag_ones(rw) / HEAD_DIM).astype(BF16))


def _moe_body(be_ref, nu_ref, ve_ref, xs_ref, w1_ref, b1_ref, w2_ref, b2_ref, o_ref, *, dff):
    i = pl.program_id(0)
    active = i < nu_ref[0]

    @pl.when(active)
    def _():
        rowid = i * MOE_BLOCK + lax.broadcasted_iota(jnp.int32, (MOE_BLOCK, 1), 0)
        lo, hi = _unpack_bf16_pairs(jnp.where(rowid < ve_ref[i], xs_ref[...], jnp.uint32(0)))
        xs = jnp.concatenate([lo.astype(BF16), hi.astype(BF16)], axis=1)
        hdn = _dot(xs, w1_ref[...].astype(BF16)) + b1_ref[...]
        g_ = jnp.minimum(hdn[:, :dff], SWIGLU_LIMIT)
        u_ = jnp.clip(hdn[:, dff:], -SWIGLU_LIMIT, SWIGLU_LIMIT)
        act = (u_ + 1.0) * (g_ * _sigmoid(g_ * SWIGLU_ALPHA))
        o_ref[...] = _pack_bf16_pairs(_dot(act.astype(BF16), w2_ref[...].astype(BF16)) + b2_ref[...])

    @pl.when(jnp.logical_not(active))
    def _():
        o_ref[...] = jnp.zeros_like(o_ref)


def _moe_experts(xs, block_e, n_used, valid_end, w1, b1, w2, b2, layer):
    rows, half = xs.shape
    nb = rows // MOE_BLOCK
    depth, ne, dm, dff2 = w1.shape
    dff = dff2 // 2
    grid_spec = pltpu.PrefetchScalarGridSpec(
        num_scalar_prefetch=3,
        grid=(nb,),
        in_specs=[pl.BlockSpec((MOE_BLOCK, half), lambda i, be, nu, ve: (i, 0)),
                  pl.BlockSpec((None, None, dm, dff2), lambda i, be, nu, ve: (layer, be[i], 0, 0)),
                  pl.BlockSpec((None, None, 1, dff2), lambda i, be, nu, ve: (layer, be[i], 0, 0)),
                  pl.BlockSpec((None, None, dff, dm), lambda i, be, nu, ve: (layer, be[i], 0, 0)),
                  pl.BlockSpec((None, None, 1, dm), lambda i, be, nu, ve: (layer, be[i], 0, 0))],
        out_specs=pl.BlockSpec((MOE_BLOCK, half), lambda i, be, nu, ve: (i, 0)),
    )
    return pl.pallas_call(
        functools.partial(_moe_body, dff=dff),
        grid_spec=grid_spec,
        out_shape=jax.ShapeDtypeStruct((rows, half), jnp.uint32),
        compiler_params=_cparams(1),
        name="moe_experts",
    )(block_e, n_used, valid_end, xs, w1, b1.reshape(depth, ne, 1, dff2), w2, b2.reshape(depth, ne, 1, dm))


N_STREAMS = 1
PLAN_TILE = 512
MOE_BLOCK_SHIFT = MOE_BLOCK.bit_length() - 1
assert 1 << MOE_BLOCK_SHIFT == MOE_BLOCK


def _moe_plan_body(e_ref, dest_ref, meta_ref, rank_ref, *, n_tokens, meta_lanes):
    tiles_per_row = n_tokens // PLAN_TILE
    n_tiles = TOP_K * tiles_per_row
    expert = lax.broadcasted_iota(jnp.int32, (N_EXPERTS, PLAN_TILE), 0)
    r_i = lax.broadcasted_iota(jnp.int32, (PLAN_TILE, PLAN_TILE), 0)
    c_i = lax.broadcasted_iota(jnp.int32, (PLAN_TILE, PLAN_TILE), 1)
    earlier = (r_i < c_i).astype(BF16)

    def tile_hits(it):
        j = it // tiles_per_row
        lanes = pl.ds(pl.multiple_of((it % tiles_per_row) * PLAN_TILE, PLAN_TILE), PLAN_TILE)
        return j, lanes, e_ref[pl.ds(j, 1), lanes] == expert

    def rank_step(it, seen):
        j, lanes, hit = tile_hits(it)
        hitf = hit.astype(F32)
        prior = _dot(hit.astype(BF16), earlier) + seen
        rank_ref[pl.ds(j, 1), lanes] = jnp.sum(hitf * prior, axis=0, keepdims=True)
        return seen + jnp.sum(hitf, axis=1, keepdims=True)

    dest_ref[...] = jnp.zeros_like(dest_ref)
    rank_ref[...] = jnp.zeros_like(rank_ref)
    counts = lax.fori_loop(0, n_tiles, rank_step, jnp.zeros((N_EXPERTS, 1), F32))
    padded = ((counts.astype(jnp.int32) + (MOE_BLOCK - 1)) >> MOE_BLOCK_SHIFT) << MOE_BLOCK_SHIFT
    er = lax.broadcasted_iota(jnp.int32, (N_EXPERTS, N_EXPERTS), 0)
    ec = lax.broadcasted_iota(jnp.int32, (N_EXPERTS, N_EXPERTS), 1)
    seg_end = _dot_exact_lhs((ec <= er).astype(BF16),
                             jnp.broadcast_to(padded.astype(F32), (N_EXPERTS, LANE)))[:, 0:1]
    seg_start = seg_end - padded.astype(F32)

    def dest_step(it, carry):
        j, lanes, hit = tile_hits(it)
        base = jnp.sum(jnp.where(hit, seg_start, 0.0), axis=0, keepdims=True)
        dest_ref[pl.ds(j, 1), lanes] = (rank_ref[pl.ds(j, 1), lanes] + base).astype(jnp.int32)
        return carry

    lax.fori_loop(0, n_tiles, dest_step, 0)
    blk_start = (lax.broadcasted_iota(jnp.int32, (N_EXPERTS, meta_lanes), 1) * MOE_BLOCK).astype(F32)
    blk_expert = jnp.minimum(jnp.sum((seg_end <= blk_start).astype(F32), axis=0, keepdims=True), N_EXPERTS - 1.0)
    mine = lax.broadcasted_iota(jnp.int32, (N_EXPERTS, meta_lanes), 0).astype(F32) == blk_expert
    valid_end = jnp.sum(jnp.where(mine, seg_start + counts, 0.0), axis=0, keepdims=True)
    n_used = jnp.broadcast_to(seg_end[N_EXPERTS - 1:N_EXPERTS, :] * (1.0 / MOE_BLOCK), (1, meta_lanes))
    mrow = lax.broadcasted_iota(jnp.int32, (SUBLANE, meta_lanes), 0)
    meta = jnp.where(mrow == 0, blk_expert, jnp.where(mrow == 1, valid_end, jnp.where(mrow == 2, n_used, 0.0)))
    meta_ref[...] = meta.astype(jnp.int32)


def _moe_plan(e_t, n_tokens, n_blocks):
    meta_lanes = -(-n_blocks // LANE) * LANE
    dest, meta = pl.pallas_call(
        functools.partial(_moe_plan_body, n_tokens=n_tokens, meta_lanes=meta_lanes),
        grid=(1,),
        in_specs=[_full((SUBLANE, n_tokens))],
        out_specs=[_full((SUBLANE, n_tokens)), _full((SUBLANE, meta_lanes))],
        out_shape=[jax.ShapeDtypeStruct((SUBLANE, n_tokens), jnp.int32),
                   jax.ShapeDtypeStruct((SUBLANE, meta_lanes), jnp.int32)],
        scratch_shapes=[pltpu.VMEM((SUBLANE, n_tokens), F32)],
        compiler_params=_cparams(1),
        name="moe_plan",
    )(e_t)
    return dest[:TOP_K], meta[0, :n_blocks], meta[1, :n_blocks], meta[2, :1]


def _combine_body(x1_ref, y0_ref, y1_ref, y2_ref, y3_ref, gate_ref, g_ref, b_ref, o_ref, *, alpha):
    gate = gate_ref[...]
    lo, hi = _unpack_bf16_pairs(y0_ref[...])
    lo, hi = gate[:, 0:1] * lo, gate[:, 0:1] * hi
    for j, y_ref in enumerate((y1_ref, y2_ref, y3_ref), start=1):
        lo_j, hi_j = _unpack_bf16_pairs(y_ref[...])
        lo, hi = lo + gate[:, j:j + 1] * lo_j, hi + gate[:, j:j + 1] * hi_j
    ffn = jnp.concatenate([lo, hi], axis=1)
    o_ref[...] = _layer_norm(alpha * x1_ref[...] + ffn, g_ref[...], b_ref[...])


def _combine(x1, yg, gate, ln_g, ln_b, alpha, tm=256):
    t, dm = x1.shape
    n_tiles = t // tm
    expert_rows = lambda j: pl.BlockSpec((tm, dm // 2), lambda i: (i + j * n_tiles, 0))
    return pl.pallas_call(
        functools.partial(_combine_body, alpha=alpha),
        grid=(n_tiles,),
        in_specs=[pl.BlockSpec((tm, dm), lambda i: (i, 0))] + [expert_rows(j) for j in range(TOP_K)]
                 + [pl.BlockSpec((tm, LANE), lambda i: (i, 0)), _full((1, dm)), _full((1, dm))],
        out_specs=pl.BlockSpec((tm, dm), lambda i: (i, 0)),
        out_shape=jax.ShapeDtypeStruct((t, dm), F32),
        compiler_params=_cparams(1),
        name="combine_ln",
    )(x1, yg, yg, yg, yg, gate, ln_g.reshape(1, dm), ln_b.reshape(1, dm))


SC_CORES = 2
SC_SUBCORES = 16
SC_WORKERS = SC_CORES * SC_SUBCORES


def _sc_gather_rows(table, idx, window):
    n = idx.shape[0]
    dim = table.shape[1]
    n_steps = n // (SC_WORKERS * window)
    assert n_steps * window * SC_WORKERS == n and n_steps % 2 == 0 and window % SUBLANE == 0 and window <= LANE
    idx3 = idx.reshape(SC_WORKERS, n_steps, window)
    mesh = plsc.VectorSubcoreMesh(core_axis_name="c", subcore_axis_name="s",
                                  num_cores=SC_CORES, num_subcores=SC_SUBCORES)

    def body(table_hbm, idx_hbm, out_hbm, idx_v, rows_v, gsem, wsem):
        wid = lax.axis_index("s") * SC_CORES + lax.axis_index("c")
        pltpu.sync_copy(idx_hbm.at[wid], idx_v)

        def gather(j, buf):
            return pltpu.make_async_copy(table_hbm.at[idx_v.at[j]], rows_v.at[buf], gsem.at[buf])

        def write(j, buf):
            base = pl.multiple_of((wid * n_steps + j) * window, window)
            return pltpu.make_async_copy(rows_v.at[buf], out_hbm.at[pl.ds(base, window)], wsem.at[buf])

        gather(0, 0).start()

        @pl.loop(0, n_steps, step=2)
        def _(j0):
            for buf in range(2):
                j = j0 + buf
                gather(j, buf).wait()

                @pl.when(j >= 1)
                def _():
                    write(j - 1, 1 - buf).wait()

                @pl.when(j + 1 < n_steps)
                def _():
                    gather(j + 1, 1 - buf).start()

                write(j, buf).start()

        write(n_steps - 1, 1).wait()

    return pl.kernel(
        body, out_type=jax.ShapeDtypeStruct((n, dim), table.dtype), mesh=mesh,
        scratch_types=[pltpu.VMEM((n_steps, window), jnp.int32), pltpu.VMEM((2, window, dim), table.dtype),
                       pltpu.SemaphoreType.DMA((2,)), pltpu.SemaphoreType.DMA((2,))],
        name="sc_gather",
    )(table, idx3)


def _sc_scatter_rows(src, dest, n_out, window):
    t, dim = src.shape
    k = dest.shape[0]
    n_steps = t // (SC_WORKERS * window)
    assert n_steps * window * SC_WORKERS == t and n_steps % 2 == 0 and window % SUBLANE == 0 and window <= LANE
    idx3 = dest.reshape(k, SC_WORKERS, n_steps, window).transpose(1, 2, 0, 3).reshape(SC_WORKERS, n_steps * k, window)
    mesh = plsc.VectorSubcoreMesh(core_axis_name="c", subcore_axis_name="s",
                                  num_cores=SC_CORES, num_subcores=SC_SUBCORES)

    def body(src_hbm, idx_hbm, out_hbm, idx_v, rows_v, rsem, ssem):
        wid = lax.axis_index("s") * SC_CORES + lax.axis_index("c")
        pltpu.sync_copy(idx_hbm.at[wid], idx_v)

        def read(s, buf):
            base = pl.multiple_of((wid * n_steps + s) * window, window)
            return pltpu.make_async_copy(src_hbm.at[pl.ds(base, window)], rows_v.at[buf], rsem.at[buf])

        def scatter(s, j, buf):
            return pltpu.make_async_copy(rows_v.at[buf], out_hbm.at[idx_v.at[s * k + j]], ssem.at[buf])

        read(0, 0).start()

        @pl.loop(0, n_steps, step=2)
        def _(s0):
            for buf in range(2):
                s = s0 + buf
                read(s, buf).wait()

                @pl.when(s >= 1)
                def _():
                    for j in range(k):
                        scatter(s - 1, j, 1 - buf).wait()

                @pl.when(s + 1 < n_steps)
                def _():
                    read(s + 1, 1 - buf).start()

                for j in range(k):
                    scatter(s, j, buf).start()

        for j in range(k):
            scatter(n_steps - 1, j, 1).wait()

    return pl.kernel(
        body, out_type=jax.ShapeDtypeStruct((n_out, dim), src.dtype), mesh=mesh,
        scratch_types=[pltpu.VMEM((n_steps * k, window), jnp.int32), pltpu.VMEM((2, window, dim), src.dtype),
                       pltpu.SemaphoreType.DMA((2,)), pltpu.SemaphoreType.DMA((2,))],
        name="sc_scatter",
    )(src, idx3)


def _pad_cols(w, width):
    return jnp.pad(w, ((0, 0), (0, width - w.shape[1])))


def kernel(x, w_in, gmlp_ln_g, gmlp_ln_b, gmlp_ws, gmlp_bs, rwkv_mu, rwkv_w0, rwkv_w2, rwkv_a0, rwkv_a2, rwkv_g2, rwkv_k_k, rwkv_k_a, rwkv_r_k, rwkv_ln_g, rwkv_ln_b, mlstm_conv_w, mlstm_conv_b, mlstm_gate_b, mlstm_ln_g, w_out, ln1_g, ln1_b, router_w, router_b, exp_w1, exp_b1, exp_w2, exp_b2, ln2_g, ln2_b):
    batch, seq, dm = x.shape
    depth = w_in.shape[0]
    sb = batch // N_STREAMS if batch % N_STREAMS == 0 else batch
    t = sb * seq
    gw = gmlp_ln_g.shape[1]
    rw = rwkv_w0.shape[2]
    mw = mlstm_ln_g.shape[1]
    g_proj = 2 * gw
    r_proj = 3 * rw + W_LORA + A_LORA + G_LORA
    alpha = (2 * depth) ** 0.25
    n_blocks = -(-t * TOP_K // MOE_BLOCK) + N_EXPERTS
    streams = [x[i * sb:(i + 1) * sb].reshape(t, dm) for i in range(batch // sb)]
    for l in range(depth):
        mixed = []
        for xf in streams:
            y_g, pr, pm = _proj(xf, w_in, l, g_proj, r_proj, gmlp_ln_g[l], gmlp_ln_b[l], gmlp_ws[l], gmlp_bs[l])
            r, v, a, kd, b, lw, bonus, rgate = _rwkv_prep(
                pr, seq, rwkv_mu[l], rwkv_w0[l], rwkv_w2[l], rwkv_a0[l], rwkv_a2[l], rwkv_g2[l],
                rwkv_k_k[l], rwkv_k_a[l], rwkv_r_k[l].reshape(-1))
            ro = _rwkv_scan(r, v, a, kd, b, lw, sb, seq)
            q, k, gates = _mlstm_prep(pm, seq, mlstm_conv_w[l], mlstm_conv_b[l], mlstm_gate_b[l], mw)
            mh = _mlstm_scan(q, k, pm, gates, sb, seq)
            mixed.append(_mix_out(xf, y_g, ro, bonus, rgate, rwkv_ln_g[l], rwkv_ln_b[l], mh, pm, mlstm_ln_g[l],
                                  w_out, l, ln1_g[l], ln1_b[l], router_w[l], router_b[l], alpha))
        streams = []
        for x1, x1p, topi, gate in mixed:
            dest, block_e, valid_end, n_used = _moe_plan(topi, t, n_blocks)
            xs = _sc_scatter_rows(x1p, dest, n_blocks * MOE_BLOCK, window=64)
            ys = _moe_experts(xs, block_e, n_used, valid_end, exp_w1, exp_b1, exp_w2, exp_b2, l)
            yg = _sc_gather_rows(ys, dest.reshape(-1), window=64)
            streams.append(_combine(x1, yg, gate, ln2_g[l], ln2_b[l], alpha))
    return jnp.concatenate(streams, axis=0).reshape(batch, seq, dm)
```

```python
import functools
import math

import jax
import jax.numpy as jnp
from jax import lax
from jax.experimental import pallas as pl
from jax.experimental.pallas import tpu as pltpu
from jax.experimental.pallas import tpu_sc as plsc

F32 = jnp.float32
BF16 = jnp.bfloat16
HI = lax.Precision.HIGHEST

HEAD_DIM = 64
GMLP_CHUNK = 128
MLSTM_CHUNK = 128
RWKV_CHUNK = 64
W_LORA = 64
A_LORA = 64
G_LORA = 128
N_EXPERTS = 32
TOP_K = 4
MOE_BLOCK = 512
SWIGLU_LIMIT = 7.0
SWIGLU_ALPHA = 1.702
LN_EPS = 1e-5
RWKV_GN_EPS = 64e-5
LANE = 128
SUBLANE = 8
VMEM_LIMIT = 48 * 1024 * 1024
NEG_BIG = -1e30


def _cparams(n_axes):
    return pltpu.CompilerParams(dimension_semantics=("arbitrary",) * n_axes,
                                vmem_limit_bytes=VMEM_LIMIT)


def _full(shape):
    return pl.BlockSpec(shape, lambda *_: (0,) * len(shape))


def _dot(a, b, precision=None):
    return jnp.dot(a, b, preferred_element_type=F32, precision=precision)


def _dot_nt(a, b, precision=None):
    return lax.dot_general(a, b, (((1,), (1,)), ((), ())), preferred_element_type=F32, precision=precision)


def _dot_tn(a, b, precision=None):
    return lax.dot_general(a, b, (((0,), (0,)), ((), ())), preferred_element_type=F32, precision=precision)


def _split(x):
    hi = x.astype(BF16)
    return hi, (x - hi.astype(F32)).astype(BF16)


def _split3(x):
    hi = x.astype(BF16)
    r1 = x - hi.astype(F32)
    mid = r1.astype(BF16)
    return hi, mid, (r1 - mid.astype(F32)).astype(BF16)


def _mm(a, b, mode, dot=_dot):
    if mode == "hi":
        return dot(a, b, HI)
    if mode == "b1":
        return dot(a.astype(BF16), b.astype(BF16))
    bh, bl = _split(b)
    if mode == "b2":
        ah = a.astype(BF16)
        return dot(ah, bh) + dot(ah, bl)
    ah, al = _split(a)
    return dot(ah, bh) + (dot(ah, bl) + dot(al, bh))


def _dot_exact_lhs(a_bf16, x):
    hi, mid, lo = _split3(x)
    return _dot(a_bf16, hi) + (_dot(a_bf16, mid) + _dot(a_bf16, lo))


def _dot_exact_rhs(x, b_bf16, terms=3):
    if terms == 2:
        hi, lo = _split(x)
        return _dot(hi, b_bf16) + _dot(lo, b_bf16)
    hi, mid, lo = _split3(x)
    return _dot(hi, b_bf16) + (_dot(mid, b_bf16) + _dot(lo, b_bf16))


def _pack_bf16_pairs(x):
    n = x.shape[1] // 2
    lo = pltpu.bitcast(x[:, :n].astype(BF16).astype(F32), jnp.uint32)
    hi = pltpu.bitcast(x[:, n:].astype(BF16).astype(F32), jnp.uint32)
    return hi | (lo >> 16)


def _unpack_bf16_pairs(w):
    lo = pltpu.bitcast(w << 16, F32)
    hi = pltpu.bitcast(w & jnp.uint32(0xFFFF0000), F32)
    return lo, hi


def _sigmoid(x):
    return 1.0 / (1.0 + jnp.exp(-x))


def _softplus(x):
    return jnp.maximum(x, 0.0) + jnp.log1p(jnp.exp(-jnp.abs(x)))


def _block_diag_ones(width):
    h = jnp.arange(width) // HEAD_DIM
    return (h[:, None] == h[None, :]).astype(F32)


CAST_ROWS = 128


def _cast_rows(src_ref, dst_ref):
    n_src, n_dst = src_ref.shape[1], dst_ref.shape[1]
    whole = n_src // LANE * LANE

    def step(r, carry):
        rows = pl.ds(pl.multiple_of(r * CAST_ROWS, CAST_ROWS), CAST_ROWS)
        dst_ref[rows, :whole] = src_ref[rows, :whole].astype(BF16)
        if n_dst > whole:
            tail = [src_ref[rows, whole:]] if n_src > whole else []
            tail.append(jnp.zeros((CAST_ROWS, n_dst - n_src), F32))
            dst_ref[rows, whole:] = jnp.concatenate(tail, axis=1).astype(BF16)
        return carry
    lax.fori_loop(0, src_ref.shape[0] // CAST_ROWS, step, 0)


def _gmlp_gate(p, lng_ref, lnb_ref, ws_ref, bst_ref, o_ref):
    gw = p.shape[1] // 2
    p = 0.5 * p * (1.0 + lax.erf(p * math.sqrt(0.5)))
    u, v = p[:, :gw], p[:, gw:]
    mu = jnp.mean(v, axis=-1, keepdims=True)
    vc = v - mu
    var = jnp.mean(vc * vc, axis=-1, keepdims=True)
    vn = vc * lax.rsqrt(var + LN_EPS) * lng_ref[...] + lnb_ref[...]
    for c in range(p.shape[0] // GMLP_CHUNK):
        rows = slice(c * GMLP_CHUNK, (c + 1) * GMLP_CHUNK)
        ys = []
        for h in range(gw // HEAD_DIM):
            cols = slice(h * HEAD_DIM, (h + 1) * HEAD_DIM)
            ys.append(_dot(ws_ref[h], vn[rows, cols].astype(BF16)) + bst_ref[:, h:h + 1])
        o_ref[rows, :] = u[rows, :] * jnp.concatenate(ys, axis=1)


def _proj_body(x_ref, w_ref, lng_ref, lnb_ref, ws_ref, bst_ref, yg_ref, pr_ref, pm_ref, wb_ref, *, ng, nr):
    @pl.when(pl.program_id(0) == 0)
    def _():
        _cast_rows(w_ref, wb_ref)

    xb = x_ref[...].astype(BF16)
    _gmlp_gate(_dot(xb, wb_ref[:, :ng]), lng_ref, lnb_ref, ws_ref, bst_ref, yg_ref)
    pr_ref[...] = _dot(xb, wb_ref[:, ng:ng + nr])
    pm_ref[...] = _dot(xb, wb_ref[:, ng + nr:])


def _proj(x, w_in, layer, ng, nr, ln_g, ln_b, ws, bs, tm=512):
    t, d = x.shape
    p_in = w_in.shape[2]
    p_pad = -(-p_in // LANE) * LANE
    nm = p_pad - ng - nr
    gw = ng // 2
    n_heads = gw // HEAD_DIM
    bst = jnp.zeros((GMLP_CHUNK, LANE), F32).at[:, :n_heads].set(bs.T)
    row = lambda n: pl.BlockSpec((tm, n), lambda i: (i, 0))
    return pl.pallas_call(
        functools.partial(_proj_body, ng=ng, nr=nr),
        grid=(t // tm,),
        in_specs=[row(d), pl.BlockSpec((None, d, p_in), lambda i: (layer, 0, 0), pipeline_mode=pl.Buffered(1)),
                  _full((1, gw)), _full((1, gw)), _full((n_heads, GMLP_CHUNK, GMLP_CHUNK)),
                  _full((GMLP_CHUNK, LANE))],
        out_specs=[row(gw), row(nr), row(nm)],
        out_shape=[jax.ShapeDtypeStruct((t, n), F32) for n in (gw, nr, nm)],
        scratch_shapes=[pltpu.VMEM((d, p_pad), BF16)],
        compiler_params=_cparams(1),
        name="in_proj",
    )(x, w_in, ln_g.reshape(1, gw), ln_b.reshape(1, gw), ws.astype(BF16), bst)


def _halo_specs(tm, width, n_rows):
    per8 = tm // SUBLANE
    last = n_rows // SUBLANE - 1
    prev = pl.BlockSpec((SUBLANE, width), lambda i: (jnp.maximum(i * per8 - 1, 0), 0))
    nxt = pl.BlockSpec((SUBLANE, width), lambda i: (jnp.minimum((i + 1) * per8, last), 0))
    return prev, nxt


def _neighbours(cur, prev_blk, next_blk, tiles_per_seq):
    tm = cur.shape[0]
    j = pl.program_id(0) % tiles_per_seq
    prev_row = jnp.where(j > 0, prev_blk[SUBLANE - 1:SUBLANE, :], 0.0)
    next_row = jnp.where(j < tiles_per_seq - 1, next_blk[0:1, :], 0.0)
    ridx = lax.broadcasted_iota(jnp.int32, cur.shape, 0)
    before = jnp.where(ridx == 0, prev_row, pltpu.roll(cur, 1, 0))
    after = jnp.where(ridx == tm - 1, next_row, pltpu.roll(cur, tm - 1, 0))
    return before, after


def _rwkv_prep_body(pr_ref, prev_ref, next_ref, mu_ref, w0_ref, w2_ref, a0_ref, a2_ref, g2_ref,
                    kk_ref, ka_ref, rk_ref, bd_ref,
                    r_out, v_out, a_out, kd_out, b_out, lw_out, bonus_out, gate_out, *, rw, tiles_per_seq):
    pf = pr_ref[...]
    before, after = _neighbours(pf, prev_ref[...], next_ref[...], tiles_per_seq)
    pf = pf + mu_ref[0:1, :] * (before - pf) + mu_ref[1:2, :] * (after - pf)
    o3 = 3 * rw
    r, k, v = pf[:, :rw], pf[:, rw:2 * rw], pf[:, 2 * rw:o3]
    wd = pf[:, o3:o3 + W_LORA]
    ad = pf[:, o3 + W_LORA:o3 + W_LORA + A_LORA]
    gd = pf[:, o3 + W_LORA + A_LORA:]
    bd = bd_ref[...]
    kk = k * kk_ref[...]
    ss = _dot_exact_rhs(kk * kk, bd, terms=2)
    kk = kk / jnp.maximum(jnp.sqrt(ss), 1e-12)
    twd = jnp.tanh(wd)
    ksum = jnp.zeros_like(k)
    for d in range(2):
        w_log = -_softplus(-(w0_ref[d:d + 1, :] + _mm(twd, w2_ref[d], "b3"))) - 0.5
        lw_out[d] = -jnp.exp(w_log)
        iclr = _sigmoid(a0_ref[d:d + 1, :] + _mm(ad, a2_ref[d], "b3"))
        kd = k * (1.0 + (iclr - 1.0) * ka_ref[...])
        kd_out[d] = kd
        b_out[d] = kk * iclr
        ksum = ksum + kd
    r_out[...] = r
    v_out[...] = v
    a_out[...] = -kk
    bonus_out[...] = _dot_exact_rhs(r * ksum * rk_ref[...], bd, terms=2) * v
    gate_out[...] = _dot(_sigmoid(gd).astype(BF16), g2_ref[...])


def _rwkv_prep(pr, seq, mu, w0, w2, a0, a2, g2, k_k, k_a, r_k, tm=256):
    t, rproj = pr.shape
    rw = w0.shape[1]
    tiles_per_seq = seq // tm
    prev, nxt = _halo_specs(tm, rproj, t)
    row = pl.BlockSpec((tm, rw), lambda i: (i, 0))
    row2 = pl.BlockSpec((2, tm, rw), lambda i: (0, i, 0))
    one = jax.ShapeDtypeStruct((t, rw), F32)
    two = jax.ShapeDtypeStruct((2, t, rw), F32)
    return pl.pallas_call(
        functools.partial(_rwkv_prep_body, rw=rw, tiles_per_seq=tiles_per_seq),
        grid=(t // tm,),
        in_specs=[pl.BlockSpec((tm, rproj), lambda i: (i, 0)), prev, nxt,
                  _full((2, rproj)), _full((2, rw)), _full((2, W_LORA, rw)), _full((2, rw)),
                  _full((2, A_LORA, rw)), _full((G_LORA, rw)), _full((1, rw)), _full((1, rw)),
                  _full((1, rw)), _full((rw, rw))],
        out_specs=[row, row, row, row2, row2, row2, row, row],
        out_shape=[one, one, one, two, two, two, one, one],
        compiler_params=_cparams(1),
        name="rwkv_prep",
    )(pr, pr, pr, mu, w0, w2, a0, a2, g2.astype(BF16), k_k.reshape(1, rw), k_a.reshape(1, rw),
      r_k.reshape(1, rw), _block_diag_ones(rw).astype(BF16))


P_G, P_INV, P_APPLY, P_STATE, P_SEQ = "b1", "b1", "b1", "b1", "b2"


def _rwkv_intra_body(r_ref, v_ref, a_ref, kd_ref, b_ref, lw_ref, rq_out, o0_out, mtx_out, hc_out,
                     *, n_heads, chunks):
    L = RWKV_CHUNK
    d = pl.program_id(0)
    row = lax.broadcasted_iota(jnp.int32, (L, L), 0)
    col = lax.broadcasted_iota(jnp.int32, (L, L), 1)
    fwd = d == 0
    rel = (col - row) * (1 - 2 * d)
    incl = rel <= 0
    strict = rel < 0
    eye = (row == col).astype(F32)
    tri = incl.astype(BF16)
    pairs = []
    for c in range(chunks):
        rows = slice(c * L, (c + 1) * L)
        lw = lw_ref[rows, :]
        cum = _dot_exact_lhs(tri, lw)
        tot = jnp.where(fwd, cum[L - 1:L, :], cum[0:1, :])
        e_neg = jnp.exp(-cum)
        e_end = jnp.exp(tot - cum)
        e_tot = jnp.exp(tot)
        r, v, a, kd, b = r_ref[rows, :], v_ref[rows, :], a_ref[rows, :], kd_ref[rows, :], b_ref[rows, :]
        at, rt, bt, kt = a * jnp.exp(cum - lw), r * jnp.exp(cum), b * e_neg, kd * e_neg
        kend, bend = kd * e_end, b * e_end
        for h in range(n_heads):
            sl = slice(h * HEAD_DIM, (h + 1) * HEAD_DIM)
            pairs.append(dict(at=at[:, sl], rt=rt[:, sl], bt=bt[:, sl], kt=kt[:, sl], v=v[:, sl],
                              kend=kend[:, sl], bend=bend[:, sl], e_tot=e_tot[:, sl]))
    for p in pairs:
        p["g"] = _mm(jnp.concatenate([p["at"], p["rt"]], axis=0),
                     jnp.concatenate([p["bt"], p["kt"]], axis=0), P_G, _dot_nt)
    row2 = lax.broadcasted_iota(jnp.int32, (L, 2 * L), 0)
    col2 = lax.broadcasted_iota(jnp.int32, (L, 2 * L), 1) & (L - 1)
    rel2 = (col2 - row2) * (1 - 2 * d)
    incl2 = rel2 <= 0
    strict2 = rel2 < 0
    zeros = jnp.zeros((L, HEAD_DIM), F32)
    for p in pairs:
        g = p.pop("g")
        a_both = jnp.where(strict2, g[:L, :], 0.0)
        p["m_both"] = jnp.where(incl2, g[L:, :], 0.0)
        p["pw"] = a_both[:, :L]
        p["a_ak"] = a_both[:, L:]
        p["inv"] = eye + p["pw"]
    for _ in range(int(math.log2(L)) - 1):
        for p in pairs:
            p["pw"] = _mm(p["pw"], p["pw"], P_INV)
        for p in pairs:
            p["inv"] = p["inv"] + _mm(p["inv"], p["pw"], P_INV)
    for p in pairs:
        p["akv"] = _mm(p["a_ak"], p["v"], P_APPLY)
    for p in pairs:
        wu = _mm(p["inv"], jnp.concatenate([p["at"], p["akv"]], axis=1), P_APPLY)
        p["rhs"] = jnp.concatenate([wu, jnp.concatenate([zeros, p["v"]], axis=1)], axis=0)
    for p in pairs:
        p["rq_o0"] = _mm(p["m_both"], p["rhs"], P_APPLY)
    for p in pairs:
        p["m_hc"] = _mm(jnp.concatenate([p["bend"], p["kend"]], axis=0), p["rhs"], P_STATE, _dot_tn)
    for c in range(chunks):
        ps = pairs[c * n_heads:(c + 1) * n_heads]
        rows = slice(c * L, (c + 1) * L)
        krows = slice(c * HEAD_DIM, (c + 1) * HEAD_DIM)
        rq_out[rows, :] = jnp.concatenate([p["rt"] + p["rq_o0"][:, :HEAD_DIM] for p in ps], axis=1)
        o0_out[rows, :] = jnp.concatenate([p["rq_o0"][:, HEAD_DIM:] for p in ps], axis=1)
        mtx_out[krows, :] = jnp.concatenate([eye * p["e_tot"] + p["m_hc"][:, :HEAD_DIM] for p in ps], axis=1)
        hc_out[krows, :] = jnp.concatenate([p["m_hc"][:, HEAD_DIM:] for p in ps], axis=1)


def _rwkv_intra(r, v, a, kd, b, lw, chunks=4):
    t, rw = r.shape
    n_heads = rw // HEAD_DIM
    tm = chunks * RWKV_CHUNK
    tk = chunks * HEAD_DIM
    n_tiles = t // tm
    one = pl.BlockSpec((tm, rw), lambda d, i: (i, 0))
    two = pl.BlockSpec((None, tm, rw), lambda d, i: (d, i, 0))
    twok = pl.BlockSpec((None, tk, rw), lambda d, i: (d, i, 0))
    return pl.pallas_call(
        functools.partial(_rwkv_intra_body, n_heads=n_heads, chunks=chunks),
        grid=(2, n_tiles),
        in_specs=[one, one, one, two, two, two],
        out_specs=[two, two, twok, twok],
        out_shape=[jax.ShapeDtypeStruct((2, t, rw), F32), jax.ShapeDtypeStruct((2, t, rw), F32),
                   jax.ShapeDtypeStruct((2, n_tiles * tk, rw), F32),
                   jax.ShapeDtypeStruct((2, n_tiles * tk, rw), F32)],
        compiler_params=_cparams(2),
        name="rwkv_intra",
    )(r, v, a, kd, b, lw)


def _rwkv_seq_body(rq0, o00, mtx0, hc0, rq1, o01, mtx1, hc1, out0, out1, h_ref, *, n_heads, batch):
    c = pl.program_id(0)

    @pl.when(c == 0)
    def _():
        h_ref[...] = jnp.zeros_like(h_ref)

    L = RWKV_CHUNK
    for d, (rq, o0, mtx, hc, out) in enumerate(((rq0, o00, mtx0, hc0, out0), (rq1, o01, mtx1, hc1, out1))):
        for bi in range(batch):
            rq_t, mtx_t = rq[bi], mtx[bi]
            state = h_ref[d, bi]
            outs, states = [], []
            for h in range(n_heads):
                sl = slice(h * HEAD_DIM, (h + 1) * HEAD_DIM)
                prod = _mm(jnp.concatenate([rq_t[:, sl], mtx_t[:, sl]], axis=0), state[:, sl], P_SEQ)
                outs.append(prod[:L])
                states.append(prod[L:])
            out[bi] = jnp.concatenate(outs, axis=1) + o0[bi]
            h_ref[d, bi] = jnp.concatenate(states, axis=1) + hc[bi]


def _rwkv_seq(rq, o0, mtx, hc, batch, seq):
    _, t, rw = rq.shape
    n_heads = rw // HEAD_DIM
    L = RWKV_CHUNK
    nc = seq // L
    as4 = lambda x: x.reshape(2, batch, x.shape[1] // batch, rw)
    rq, o0, mtx, hc = as4(rq), as4(o0), as4(mtx), as4(hc)
    fwd = lambda rows: pl.BlockSpec((None, batch, rows, rw), lambda c: (0, 0, c, 0))
    bwd = lambda rows: pl.BlockSpec((None, batch, rows, rw), lambda c: (1, 0, nc - 1 - c, 0))
    out0, out1 = pl.pallas_call(
        functools.partial(_rwkv_seq_body, n_heads=n_heads, batch=batch),
        grid=(nc,),
        in_specs=[fwd(L), fwd(L), fwd(HEAD_DIM), fwd(HEAD_DIM), bwd(L), bwd(L), bwd(HEAD_DIM), bwd(HEAD_DIM)],
        out_specs=[pl.BlockSpec((batch, L, rw), lambda c: (0, c, 0)),
                   pl.BlockSpec((batch, L, rw), lambda c: (0, nc - 1 - c, 0))],
        out_shape=[jax.ShapeDtypeStruct((batch, seq, rw), F32)] * 2,
        scratch_shapes=[pltpu.VMEM((2, batch, HEAD_DIM, rw), F32)],
        compiler_params=_cparams(1),
        name="rwkv_seq",
    )(rq, o0, mtx, hc, rq, o0, mtx, hc)
    return out0.reshape(t, rw), out1.reshape(t, rw)


def _rwkv_scan(r, v, a, kd, b, lw, batch, seq):
    rq, o0, mtx, hc = _rwkv_intra(r, v, a, kd, b, lw)
    return _rwkv_seq(rq, o0, mtx, hc, batch, seq)


def _mlstm_prep_body(qk_ref, prev_ref, next_ref, g_ref, cw_ref, cb_ref, gb_ref, q_out, k_out, gate_out,
                     *, mw, n_heads, tiles_per_seq):
    x = qk_ref[...]
    before, after = _neighbours(x, prev_ref[...], next_ref[...], tiles_per_seq)
    y = cb_ref[...] + before * cw_ref[0:1, :] + x * cw_ref[1:2, :] + after * cw_ref[2:3, :]
    y = y * _sigmoid(y)
    q_out[...] = y[:, :mw]
    k_out[...] = y[:, mw:] * (HEAD_DIM ** -0.5)
    g = g_ref[...] + gb_ref[...]
    lane = lax.broadcasted_iota(jnp.int32, g.shape, 1)
    for d in range(2):
        ig = g if d == 0 else pltpu.roll(g, LANE - n_heads, 1)
        fg = pltpu.roll(g, LANE - (1 + d) * n_heads, 1)
        lf = -_softplus(-fg)
        gate_out[d] = jnp.where(lane < n_heads, ig, jnp.where(lane < 2 * n_heads, lf, 0.0))


def _mlstm_prep(pm, seq, conv_w, conv_b, gate_b, mw, tm=256):
    t = pm.shape[0]
    n_heads = mw // HEAD_DIM
    tiles_per_seq = seq // tm
    w2 = 2 * mw
    prev, nxt = _halo_specs(tm, w2, t)
    gcol = (4 * mw) // LANE
    gb = jnp.zeros((1, LANE), F32).at[0, :4 * n_heads].set(gate_b)
    row = pl.BlockSpec((tm, mw), lambda i: (i, 0))
    return pl.pallas_call(
        functools.partial(_mlstm_prep_body, mw=mw, n_heads=n_heads, tiles_per_seq=tiles_per_seq),
        grid=(t // tm,),
        in_specs=[pl.BlockSpec((tm, w2), lambda i: (i, 0)), prev, nxt,
                  pl.BlockSpec((tm, LANE), lambda i: (i, gcol)),
                  _full((3, w2)), _full((1, w2)), _full((1, LANE))],
        out_specs=[row, row, pl.BlockSpec((2, tm, LANE), lambda i: (0, i, 0))],
        out_shape=[jax.ShapeDtypeStruct((t, mw), F32), jax.ShapeDtypeStruct((t, mw), F32),
                   jax.ShapeDtypeStruct((2, t, LANE), F32)],
        compiler_params=_cparams(1),
        name="mlstm_prep",
    )(pm, pm, pm, pm, conv_w, conv_b.reshape(1, w2), gb)


def _mlstm_scan_body(q0_ref, k0_ref, v0_ref, g0_ref, q1_ref, k1_ref, v1_ref, g1_ref, o0_ref, o1_ref,
                     c_ref, m_ref, *, n_heads):
    L = MLSTM_CHUNK
    H = n_heads

    @pl.when(pl.program_id(1) == 0)
    def _():
        c_ref[...] = jnp.zeros_like(c_ref)
        m_ref[...] = jnp.zeros_like(m_ref)

    row = lax.broadcasted_iota(jnp.int32, (L, L), 0)
    col = lax.broadcasted_iota(jnp.int32, (L, L), 1)
    trow = lax.broadcasted_iota(jnp.int32, (L, LANE), 0)
    low = lax.broadcasted_iota(jnp.int32, (L, LANE), 1) < HEAD_DIM
    xr = lax.broadcasted_iota(jnp.int32, (LANE, H * L), 0)
    xc = lax.broadcasted_iota(jnp.int32, (LANE, H * L), 1)
    spread = (xr - H == lax.shift_right_logical(xc, int(math.log2(L)))).astype(BF16)
    hs = []
    for d, (q_ref, k_ref, v_ref, g_ref) in enumerate(((q0_ref, k0_ref, v0_ref, g0_ref),
                                                      (q1_ref, k1_ref, v1_ref, g1_ref))):
        incl = (col <= row) if d == 0 else (col >= row)
        last = L - 1 if d == 0 else 0
        g = g_ref[...]
        bcum = _dot_exact_lhs(incl.astype(BF16), g)
        z = pltpu.roll(g, H, 1) - bcum
        cmax = z
        shift = 1
        while shift < L:
            if d == 0:
                moved = jnp.where(trow >= shift, pltpu.roll(cmax, shift, 0), -jnp.inf)
            else:
                moved = jnp.where(trow < L - shift, pltpu.roll(cmax, L - shift, 0), -jnp.inf)
            cmax = jnp.maximum(cmax, moved)
            shift *= 2
        m_prev = m_ref[d, 0:1, :]
        top = jnp.maximum(cmax, m_prev)
        b_last = bcum[last:last + 1, :]
        lwc = b_last + z
        m_new = jnp.maximum(b_last + m_prev, jnp.max(lwc, axis=0, keepdims=True))
        m_ref[d, 0:1, :] = m_new
        alpha_w = _dot_exact_rhs(-top, spread)
        floor_w = jnp.exp(-_dot_exact_rhs(bcum + top, spread, terms=2))
        wts_w = _dot(jnp.exp(lwc - m_new).astype(BF16), spread)
        rows_w = _dot_exact_rhs(jnp.concatenate(
            [jnp.broadcast_to(m_prev, (SUBLANE, LANE)),
             jnp.broadcast_to(jnp.exp(b_last + m_prev - m_new), (SUBLANE, LANE))], axis=0), spread)
        z_t = z.T
        q, k, v = q_ref[...], k_ref[...], v_ref[...]
        for h in range(H):
            slab = slice(h // 2 * LANE, (h // 2 + 1) * LANE)
            cols = slice(h * L, (h + 1) * L)
            mine = low if h % 2 == 0 else jnp.logical_not(low)
            kh = jnp.where(mine, k[:, slab], 0.0)
            hs.append(dict(
                qh=jnp.where(mine, q[:, slab], 0.0).astype(BF16), kh=kh.astype(BF16),
                vext=jnp.where(mine, v[:, slab], 1.0).astype(BF16),
                decay=jnp.exp(jnp.where(incl, alpha_w[:, cols] + z_t[H + h:H + h + 1, :], -jnp.inf)),
                w_inter=jnp.exp(alpha_w[:, cols] + rows_w[0:1, cols]), floor=floor_w[:, cols],
                wk=(wts_w[:, cols] * kh).astype(BF16), dec=rows_w[SUBLANE:SUBLANE + 1, cols],
                cst=c_ref[d, h]))
    for p in hs:
        p["sc"] = (_dot_nt(p["qh"], p["kh"]) * p["decay"]).astype(BF16)
    for p in hs:
        p["numext"] = _dot(p["sc"], p["vext"]) + p["w_inter"] * _dot(p["qh"], p["cst"].astype(BF16))
    for p in hs:
        p["upd"] = _dot_tn(p["wk"], p["vext"])
    for d, o_ref in enumerate((o0_ref, o1_ref)):
        res = []
        for h in range(H):
            p = hs[d * H + h]
            den = pltpu.roll(p["numext"], HEAD_DIM, 1)
            res.append(p["numext"] / jnp.maximum(jnp.abs(den), p["floor"]))
            c_ref[d, h] = p["dec"] * p["cst"] + p["upd"]
        for pair in range(H // 2):
            o_ref[:, pair * LANE:(pair + 1) * LANE] = jnp.where(low, res[2 * pair], res[2 * pair + 1])


def _mlstm_scan(q, k, pm, gates, batch, seq):
    t, mw = q.shape
    n_heads = mw // HEAD_DIM
    L = MLSTM_CHUNK
    nc = seq // L
    fwd = lambda bi, c: bi * nc + c
    bwd = lambda bi, c: bi * nc + nc - 1 - c
    specs = []
    for d, blk in enumerate((fwd, bwd)):
        specs += [pl.BlockSpec((L, mw), lambda bi, c, blk=blk: (blk(bi, c), 0)),
                  pl.BlockSpec((L, mw), lambda bi, c, blk=blk: (blk(bi, c), 0)),
                  pl.BlockSpec((L, mw), lambda bi, c, blk=blk: (blk(bi, c), 2)),
                  pl.BlockSpec((None, L, LANE), lambda bi, c, blk=blk, d=d: (d, blk(bi, c), 0))]
    return pl.pallas_call(
        functools.partial(_mlstm_scan_body, n_heads=n_heads),
        grid=(batch, nc),
        in_specs=specs,
        out_specs=[pl.BlockSpec((L, mw), lambda bi, c: (fwd(bi, c), 0)),
                   pl.BlockSpec((L, mw), lambda bi, c: (bwd(bi, c), 0))],
        out_shape=[jax.ShapeDtypeStruct((t, mw), F32)] * 2,
        scratch_shapes=[pltpu.VMEM((2, n_heads, LANE, LANE), F32), pltpu.VMEM((2, SUBLANE, LANE), F32)],
        compiler_params=_cparams(2),
        name="mlstm_scan",
    )(q, k, pm, gates, q, k, pm, gates)


def _layer_norm(x, g, b):
    mu = jnp.mean(x, axis=-1, keepdims=True)
    xc = x - mu
    var = jnp.mean(xc * xc, axis=-1, keepdims=True)
    return xc * lax.rsqrt(var + LN_EPS) * g + b


def _head_norm(x, bd_mean, eps):
    mu = _dot_exact_rhs(x, bd_mean, terms=2)
    xc = x - mu
    var = _dot_exact_rhs(xc * xc, bd_mean, terms=2)
    return xc * lax.rsqrt(var + eps)


def _mix_out_body(x_ref, yg_ref, ro0_ref, ro1_ref, bonus_ref, rgate_ref, rlg_ref, rlb_ref, mh0_ref, mh1_ref, og_ref,
                  mlg_ref, w_ref, l1g_ref, l1b_ref, rw_ref, rb_ref, bdm_ref,
                  x1_out, x1p_out, topi_out, gate_out, wb_ref, *, alpha, gw, rw):
    @pl.when(pl.program_id(0) == 0)
    def _():
        _cast_rows(w_ref, wb_ref)

    bdm = bdm_ref[...]
    yr = _head_norm(ro0_ref[...] + ro1_ref[...], bdm, RWKV_GN_EPS) * rlg_ref[...] + rlb_ref[...]
    yr = (yr + bonus_ref[...]) * rgate_ref[...]
    ym = _sigmoid(og_ref[...]) * (_head_norm(mh0_ref[...] + mh1_ref[...], bdm, LN_EPS) * mlg_ref[...])
    mix = (_dot(yg_ref[...].astype(BF16), wb_ref[:gw, :]) + _dot(yr.astype(BF16), wb_ref[gw:gw + rw, :])
           + _dot(ym.astype(BF16), wb_ref[gw + rw:, :]))
    x1 = _layer_norm(alpha * x_ref[...] + mix, l1g_ref[...], l1b_ref[...])
    x1_out[...] = x1
    x1p_out[...] = _pack_bf16_pairs(x1)
    lg = _mm(x1, rw_ref[...], "b3") + rb_ref[...]
    lane = lax.broadcasted_iota(jnp.int32, lg.shape, 1)
    vals, topi = [], jnp.zeros(lg.shape, jnp.int32)
    for j in range(TOP_K):
        mx = jnp.max(lg, axis=1, keepdims=True)
        idx = jnp.min(jnp.where(lg == mx, lane, LANE), axis=1, keepdims=True)
        vals.append(mx)
        topi = jnp.where(lane == j, idx, topi)
        lg = jnp.where(lane == idx, -jnp.inf, lg)
    es = [jnp.exp(vj - vals[0]) for vj in vals]
    den = es[0] + es[1] + es[2] + es[3]
    gate = jnp.zeros(lg.shape, F32)
    for j in range(TOP_K):
        gate = jnp.where(lane == j, es[j] / den, gate)
    topi_out[...] = topi.T[:SUBLANE, :]
    gate_out[...] = gate


def _mix_out(x, yg, ro, bonus, rgate, rlg, rlb, mh, pm, mlg, w_out, layer, l1g, l1b, router_w, router_b, alpha,
             tm=256):
    t, dm = x.shape
    gw, rw, mw = yg.shape[1], bonus.shape[1], mh[0].shape[1]
    assert rw == mw
    rwp = jnp.zeros((dm, LANE), F32).at[:, :N_EXPERTS].set(router_w)
    rbp = jnp.full((1, LANE), NEG_BIG, F32).at[0, :N_EXPERTS].set(router_b)
    row = lambda n: pl.BlockSpec((tm, n), lambda i: (i, 0))
    vec = lambda n: _full((1, n))
    return pl.pallas_call(
        functools.partial(_mix_out_body, alpha=alpha, gw=gw, rw=rw),
        grid=(t // tm,),
        in_specs=[row(dm), row(gw), row(rw), row(rw), row(rw), row(rw), vec(rw), vec(rw), row(mw), row(mw),
                  pl.BlockSpec((tm, mw), lambda i: (i, 3)),
                  vec(mw),
                  pl.BlockSpec((None, dm, dm), lambda i: (layer, 0, 0), pipeline_mode=pl.Buffered(1)),
                  vec(dm), vec(dm), _full((dm, LANE)), vec(LANE), _full((rw, rw))],
        out_specs=[row(dm), row(dm // 2), pl.BlockSpec((SUBLANE, tm), lambda i: (0, i)), row(LANE)],
        out_shape=[jax.ShapeDtypeStruct((t, dm), F32), jax.ShapeDtypeStruct((t, dm // 2), jnp.uint32),
                   jax.ShapeDtypeStruct((SUBLANE, t), jnp.int32), jax.ShapeDtypeStruct((t, LANE), F32)],
        scratch_shapes=[pltpu.VMEM((dm, dm), BF16)],
        compiler_params=_cparams(1),
        name="mix_out",
    )(x, yg, ro[0], ro[1], bonus, rgate, rlg.reshape(1, rw), rlb.reshape(1, rw), mh[0], mh[1], pm,
      mlg.reshape(1, mw), w_out, l1g.reshape(1, dm), l1b.reshape(1, dm), rwp, rbp,
      (_block_diag_ones(rw) / HEAD_DIM).astype(BF16))


def _moe_body(be_ref, nu_ref, ve_ref, xs_ref, w1_ref, b1_ref, w2_ref, b2_ref, o_ref, *, dff):
    i = pl.program_id(0)
    active = i < nu_ref[0]

    @pl.when(active)
    def _():
        rowid = i * MOE_BLOCK + lax.broadcasted_iota(jnp.int32, (MOE_BLOCK, 1), 0)
        lo, hi = _unpack_bf16_pairs(jnp.where(rowid < ve_ref[i], xs_ref[...], jnp.uint32(0)))
        xs = jnp.concatenate([lo.astype(BF16), hi.astype(BF16)], axis=1)
        hdn = _dot(xs, w1_ref[...].astype(BF16)) + b1_ref[...]
        g_ = jnp.minimum(hdn[:, :dff], SWIGLU_LIMIT)
        u_ = jnp.clip(hdn[:, dff:], -SWIGLU_LIMIT, SWIGLU_LIMIT)
        act = (u_ + 1.0) * (g_ * _sigmoid(g_ * SWIGLU_ALPHA))
        o_ref[...] = _pack_bf16_pairs(_dot(act.astype(BF16), w2_ref[...].astype(BF16)) + b2_ref[...])

    @pl.when(jnp.logical_not(active))
    def _():
        o_ref[...] = jnp.zeros_like(o_ref)


def _moe_experts(xs, block_e, n_used, valid_end, w1, b1, w2, b2, layer):
    rows, half = xs.shape
    nb = rows // MOE_BLOCK
    depth, ne, dm, dff2 = w1.shape
    dff = dff2 // 2
    grid_spec = pltpu.PrefetchScalarGridSpec(
        num_scalar_prefetch=3,
        grid=(nb,),
        in_specs=[pl.BlockSpec((MOE_BLOCK, half), lambda i, be, nu, ve: (i, 0)),
                  pl.BlockSpec((None, None, dm, dff2), lambda i, be, nu, ve: (layer, be[i], 0, 0)),
                  pl.BlockSpec((None, None, 1, dff2), lambda i, be, nu, ve: (layer, be[i], 0, 0)),
                  pl.BlockSpec((None, None, dff, dm), lambda i, be, nu, ve: (layer, be[i], 0, 0)),
                  pl.BlockSpec((None, None, 1, dm), lambda i, be, nu, ve: (layer, be[i], 0, 0))],
        out_specs=pl.BlockSpec((MOE_BLOCK, half), lambda i, be, nu, ve: (i, 0)),
    )
    return pl.pallas_call(
        functools.partial(_moe_body, dff=dff),
        grid_spec=grid_spec,
        out_shape=jax.ShapeDtypeStruct((rows, half), jnp.uint32),
        compiler_params=_cparams(1),
        name="moe_experts",
    )(block_e, n_used, valid_end, xs, w1, b1.reshape(depth, ne, 1, dff2), w2, b2.reshape(depth, ne, 1, dm))


COMBINE_PARTS = 2
N_STREAMS = 1
PLAN_TILE = 512
MOE_BLOCK_SHIFT = MOE_BLOCK.bit_length() - 1
assert 1 << MOE_BLOCK_SHIFT == MOE_BLOCK


def _moe_plan_body(e_ref, dest_ref, meta_ref, rank_ref, *, n_tokens, meta_lanes):
    tiles_per_row = n_tokens // PLAN_TILE
    n_tiles = TOP_K * tiles_per_row
    expert = lax.broadcasted_iota(jnp.int32, (N_EXPERTS, PLAN_TILE), 0)
    r_i = lax.broadcasted_iota(jnp.int32, (PLAN_TILE, PLAN_TILE), 0)
    c_i = lax.broadcasted_iota(jnp.int32, (PLAN_TILE, PLAN_TILE), 1)
    earlier = (r_i < c_i).astype(BF16)

    def tile_hits(it):
        j = it // tiles_per_row
        lanes = pl.ds(pl.multiple_of((it % tiles_per_row) * PLAN_TILE, PLAN_TILE), PLAN_TILE)
        return j, lanes, e_ref[pl.ds(j, 1), lanes] == expert

    def rank_step(it, seen):
        j, lanes, hit = tile_hits(it)
        hitf = hit.astype(F32)
        prior = _dot(hit.astype(BF16), earlier) + seen
        rank_ref[pl.ds(j, 1), lanes] = jnp.sum(hitf * prior, axis=0, keepdims=True)
        return seen + jnp.sum(hitf, axis=1, keepdims=True)

    dest_ref[...] = jnp.zeros_like(dest_ref)
    rank_ref[...] = jnp.zeros_like(rank_ref)
    counts = lax.fori_loop(0, n_tiles, rank_step, jnp.zeros((N_EXPERTS, 1), F32))
    padded = ((counts.astype(jnp.int32) + (MOE_BLOCK - 1)) >> MOE_BLOCK_SHIFT) << MOE_BLOCK_SHIFT
    er = lax.broadcasted_iota(jnp.int32, (N_EXPERTS, N_EXPERTS), 0)
    ec = lax.broadcasted_iota(jnp.int32, (N_EXPERTS, N_EXPERTS), 1)
    seg_end = _dot_exact_lhs((ec <= er).astype(BF16),
                             jnp.broadcast_to(padded.astype(F32), (N_EXPERTS, LANE)))[:, 0:1]
    seg_start = seg_end - padded.astype(F32)

    def dest_step(it, carry):
        j, lanes, hit = tile_hits(it)
        base = jnp.sum(jnp.where(hit, seg_start, 0.0), axis=0, keepdims=True)
        dest_ref[pl.ds(j, 1), lanes] = (rank_ref[pl.ds(j, 1), lanes] + base).astype(jnp.int32)
        return carry

    lax.fori_loop(0, n_tiles, dest_step, 0)
    blk_start = (lax.broadcasted_iota(jnp.int32, (N_EXPERTS, meta_lanes), 1) * MOE_BLOCK).astype(F32)
    blk_expert = jnp.minimum(jnp.sum((seg_end <= blk_start).astype(F32), axis=0, keepdims=True), N_EXPERTS - 1.0)
    mine = lax.broadcasted_iota(jnp.int32, (N_EXPERTS, meta_lanes), 0).astype(F32) == blk_expert
    valid_end = jnp.sum(jnp.where(mine, seg_start + counts, 0.0), axis=0, keepdims=True)
    n_used = jnp.broadcast_to(seg_end[N_EXPERTS - 1:N_EXPERTS, :] * (1.0 / MOE_BLOCK), (1, meta_lanes))
    mrow = lax.broadcasted_iota(jnp.int32, (SUBLANE, meta_lanes), 0)
    meta = jnp.where(mrow == 0, blk_expert, jnp.where(mrow == 1, valid_end, jnp.where(mrow == 2, n_used, 0.0)))
    meta_ref[...] = meta.astype(jnp.int32)


def _moe_plan(e_t, n_tokens, n_blocks):
    meta_lanes = -(-n_blocks // LANE) * LANE
    dest, meta = pl.pallas_call(
        functools.partial(_moe_plan_body, n_tokens=n_tokens, meta_lanes=meta_lanes),
        grid=(1,),
        in_specs=[_full((SUBLANE, n_tokens))],
        out_specs=[_full((SUBLANE, n_tokens)), _full((SUBLANE, meta_lanes))],
        out_shape=[jax.ShapeDtypeStruct((SUBLANE, n_tokens), jnp.int32),
                   jax.ShapeDtypeStruct((SUBLANE, meta_lanes), jnp.int32)],
        scratch_shapes=[pltpu.VMEM((SUBLANE, n_tokens), F32)],
        compiler_params=_cparams(1),
        name="moe_plan",
    )(e_t)
    return dest[:TOP_K], meta[0, :n_blocks], meta[1, :n_blocks], meta[2, :1]


def _combine_body(x1_ref, y0_ref, y1_ref, y2_ref, y3_ref, gate_ref, g_ref, b_ref, *rest, alpha):
    o_ref = rest[-1]
    gate = gate_ref[...]
    lo, hi = _unpack_bf16_pairs(y0_ref[...])
    lo, hi = gate[:, 0:1] * lo, gate[:, 0:1] * hi
    for j, y_ref in enumerate((y1_ref, y2_ref, y3_ref), start=1):
        lo_j, hi_j = _unpack_bf16_pairs(y_ref[...])
        lo, hi = lo + gate[:, j:j + 1] * lo_j, hi + gate[:, j:j + 1] * hi_j
    ffn = jnp.concatenate([lo, hi], axis=1)
    o_ref[...] = _layer_norm(alpha * x1_ref[...] + ffn, g_ref[...], b_ref[...])


def _combine(x1, yg, gate, ln_g, ln_b, alpha, part, n_parts, earlier=None, tm=256):
    t, dm = x1.shape
    n_tiles = t // tm // n_parts
    first = part * n_tiles
    tile = lambda n: pl.BlockSpec((tm, n), lambda i: (first + i, 0))
    expert_rows = lambda j: pl.BlockSpec((tm, dm // 2), lambda i: (i + j * n_tiles, 0))
    in_specs = ([tile(dm)] + [expert_rows(j) for j in range(TOP_K)] + [tile(LANE), _full((1, dm)), _full((1, dm))])
    args = [x1, yg, yg, yg, yg, gate, ln_g.reshape(1, dm), ln_b.reshape(1, dm)]
    aliases = {}
    if earlier is not None:
        in_specs.append(pl.BlockSpec(memory_space=pl.ANY))
        args.append(earlier)
        aliases = {len(args) - 1: 0}
    return pl.pallas_call(
        functools.partial(_combine_body, alpha=alpha),
        grid=(n_tiles,),
        in_specs=in_specs,
        out_specs=tile(dm),
        out_shape=jax.ShapeDtypeStruct((t, dm), F32),
        input_output_aliases=aliases,
        compiler_params=_cparams(1),
        name="combine_ln",
    )(*args)


SC_CORES = 2
SC_SUBCORES = 16
SC_WORKERS = SC_CORES * SC_SUBCORES


def _sc_gather_rows(table, idx, window):
    n = idx.shape[0]
    dim = table.shape[1]
    n_steps = n // (SC_WORKERS * window)
    assert n_steps * window * SC_WORKERS == n and n_steps % 2 == 0 and window % SUBLANE == 0 and window <= LANE
    idx3 = idx.reshape(SC_WORKERS, n_steps, window)
    mesh = plsc.VectorSubcoreMesh(core_axis_name="c", subcore_axis_name="s",
                                  num_cores=SC_CORES, num_subcores=SC_SUBCORES)

    def body(table_hbm, idx_hbm, out_hbm, idx_v, rows_v, gsem, wsem):
        wid = lax.axis_index("s") * SC_CORES + lax.axis_index("c")
        pltpu.sync_copy(idx_hbm.at[wid], idx_v)

        def gather(j, buf):
            return pltpu.make_async_copy(table_hbm.at[idx_v.at[j]], rows_v.at[buf], gsem.at[buf])

        def write(j, buf):
            base = pl.multiple_of((wid * n_steps + j) * window, window)
            return pltpu.make_async_copy(rows_v.at[buf], out_hbm.at[pl.ds(base, window)], wsem.at[buf])

        gather(0, 0).start()

        @pl.loop(0, n_steps, step=2)
        def _(j0):
            for buf in range(2):
                j = j0 + buf
                gather(j, buf).wait()

                @pl.when(j >= 1)
                def _():
                    write(j - 1, 1 - buf).wait()

                @pl.when(j + 1 < n_steps)
                def _():
                    gather(j + 1, 1 - buf).start()

                write(j, buf).start()

        write(n_steps - 1, 1).wait()

    return pl.kernel(
        body, out_type=jax.ShapeDtypeStruct((n, dim), table.dtype), mesh=mesh,
        scratch_types=[pltpu.VMEM((n_steps, window), jnp.int32), pltpu.VMEM((2, window, dim), table.dtype),
                       pltpu.SemaphoreType.DMA((2,)), pltpu.SemaphoreType.DMA((2,))],
        name="sc_gather",
    )(table, idx3)


def _sc_scatter_rows(src, dest, n_out, window):
    t, dim = src.shape
    k = dest.shape[0]
    n_steps = t // (SC_WORKERS * window)
    assert n_steps * window * SC_WORKERS == t and n_steps % 2 == 0 and window % SUBLANE == 0 and window <= LANE
    idx3 = dest.reshape(k, SC_WORKERS, n_steps, window).transpose(1, 2, 0, 3).reshape(SC_WORKERS, n_steps * k, window)
    mesh = plsc.VectorSubcoreMesh(core_axis_name="c", subcore_axis_name="s",
                                  num_cores=SC_CORES, num_subcores=SC_SUBCORES)

    def body(src_hbm, idx_hbm, out_hbm, idx_v, rows_v, rsem, ssem):
        wid = lax.axis_index("s") * SC_CORES + lax.axis_index("c")
        pltpu.sync_copy(idx_hbm.at[wid], idx_v)

        def read(s, buf):
            base = pl.multiple_of((wid * n_steps + s) * window, window)
            return pltpu.make_async_copy(src_hbm.at[pl.ds(base, window)], rows_v.at[buf], rsem.at[buf])

        def scatter(s, j, buf):
            return pltpu.make_async_copy(rows_v.at[buf], out_hbm.at[idx_v.at[s * k + j]], ssem.at[buf])

        read(0, 0).start()

        @pl.loop(0, n_steps, step=2)
        def _(s0):
            for buf in range(2):
                s = s0 + buf
                read(s, buf).wait()

                @pl.when(s >= 1)
                def _():
                    for j in range(k):
                        scatter(s - 1, j, 1 - buf).wait()

                @pl.when(s + 1 < n_steps)
                def _():
                    read(s + 1, 1 - buf).start()

                for j in range(k):
                    scatter(s, j, buf).start()

        for j in range(k):
            scatter(n_steps - 1, j, 1).wait()

    return pl.kernel(
        body, out_type=jax.ShapeDtypeStruct((n_out, dim), src.dtype), mesh=mesh,
        scratch_types=[pltpu.VMEM((n_steps * k, window), jnp.int32), pltpu.VMEM((2, window, dim), src.dtype),
                       pltpu.SemaphoreType.DMA((2,)), pltpu.SemaphoreType.DMA((2,))],
        name="sc_scatter",
    )(src, idx3)


def _pad_cols(w, width):
    return jnp.pad(w, ((0, 0), (0, width - w.shape[1])))


def kernel(x, w_in, gmlp_ln_g, gmlp_ln_b, gmlp_ws, gmlp_bs, rwkv_mu, rwkv_w0, rwkv_w2, rwkv_a0, rwkv_a2, rwkv_g2, rwkv_k_k, rwkv_k_a, rwkv_r_k, rwkv_ln_g, rwkv_ln_b, mlstm_conv_w, mlstm_conv_b, mlstm_gate_b, mlstm_ln_g, w_out, ln1_g, ln1_b, router_w, router_b, exp_w1, exp_b1, exp_w2, exp_b2, ln2_g, ln2_b):
    batch, seq, dm = x.shape
    depth = w_in.shape[0]
    sb = batch // N_STREAMS if batch % N_STREAMS == 0 else batch
    t = sb * seq
    gw = gmlp_ln_g.shape[1]
    rw = rwkv_w0.shape[2]
    mw = mlstm_ln_g.shape[1]
    g_proj = 2 * gw
    r_proj = 3 * rw + W_LORA + A_LORA + G_LORA
    alpha = (2 * depth) ** 0.25
    n_blocks = -(-t * TOP_K // MOE_BLOCK) + N_EXPERTS
    streams = [x[i * sb:(i + 1) * sb].reshape(t, dm) for i in range(batch // sb)]
    for l in range(depth):
        mixed = []
        for xf in streams:
            y_g, pr, pm = _proj(xf, w_in, l, g_proj, r_proj, gmlp_ln_g[l], gmlp_ln_b[l], gmlp_ws[l], gmlp_bs[l])
            r, v, a, kd, b, lw, bonus, rgate = _rwkv_prep(
                pr, seq, rwkv_mu[l], rwkv_w0[l], rwkv_w2[l], rwkv_a0[l], rwkv_a2[l], rwkv_g2[l],
                rwkv_k_k[l], rwkv_k_a[l], rwkv_r_k[l].reshape(-1))
            ro = _rwkv_scan(r, v, a, kd, b, lw, sb, seq)
            q, k, gates = _mlstm_prep(pm, seq, mlstm_conv_w[l], mlstm_conv_b[l], mlstm_gate_b[l], mw)
            mh = _mlstm_scan(q, k, pm, gates, sb, seq)
            mixed.append(_mix_out(xf, y_g, ro, bonus, rgate, rwkv_ln_g[l], rwkv_ln_b[l], mh, pm, mlstm_ln_g[l],
                                  w_out, l, ln1_g[l], ln1_b[l], router_w[l], router_b[l], alpha))
        streams = []
        for x1, x1p, topi, gate in mixed:
            dest, block_e, valid_end, n_used = _moe_plan(topi, t, n_blocks)
            xs = _sc_scatter_rows(x1p, dest, n_blocks * MOE_BLOCK, window=64)
            ys = _moe_experts(xs, block_e, n_used, valid_end, exp_w1, exp_b1, exp_w2, exp_b2, l)
            part_len = t // COMBINE_PARTS
            x2 = None
            for part in range(COMBINE_PARTS):
                idx = dest[:, part * part_len:(part + 1) * part_len].reshape(-1)
                yg = _sc_gather_rows(ys, idx, window=64)
                x2 = _combine(x1, yg, gate, ln2_g[l], ln2_b[l], alpha, part, COMBINE_PARTS, earlier=x2)
            streams.append(x2)
    return jnp.concatenate(streams, axis=0).reshape(batch, seq, dm)
```

```python
import functools
import math

import jax
import jax.numpy as jnp
from jax import lax
from jax.experimental import pallas as pl
from jax.experimental.pallas import tpu as pltpu
from jax.experimental.pallas import tpu_sc as plsc

F32 = jnp.float32
BF16 = jnp.bfloat16
HI = lax.Precision.HIGHEST

HEAD_DIM = 64
GMLP_CHUNK = 128
MLSTM_CHUNK = 128
RWKV_CHUNK = 64
W_LORA = 64
A_LORA = 64
G_LORA = 128
N_EXPERTS = 32
TOP_K = 4
MOE_BLOCK = 512
SWIGLU_LIMIT = 7.0
SWIGLU_ALPHA = 1.702
LN_EPS = 1e-5
RWKV_GN_EPS = 64e-5
LANE = 128
SUBLANE = 8
VMEM_LIMIT = 48 * 1024 * 1024
NEG_BIG = -1e30


def _cparams(n_axes):
    return pltpu.CompilerParams(dimension_semantics=("arbitrary",) * n_axes,
                                vmem_limit_bytes=VMEM_LIMIT)


def _full(shape):
    return pl.BlockSpec(shape, lambda *_: (0,) * len(shape))


def _dot(a, b, precision=None):
    return jnp.dot(a, b, preferred_element_type=F32, precision=precision)


def _dot_nt(a, b, precision=None):
    return lax.dot_general(a, b, (((1,), (1,)), ((), ())), preferred_element_type=F32, precision=precision)


def _dot_tn(a, b, precision=None):
    return lax.dot_general(a, b, (((0,), (0,)), ((), ())), preferred_element_type=F32, precision=precision)


def _split(x):
    hi = x.astype(BF16)
    return hi, (x - hi.astype(F32)).astype(BF16)


def _split3(x):
    hi = x.astype(BF16)
    r1 = x - hi.astype(F32)
    mid = r1.astype(BF16)
    return hi, mid, (r1 - mid.astype(F32)).astype(BF16)


def _mm(a, b, mode, dot=_dot):
    if mode == "hi":
        return dot(a, b, HI)
    if mode == "b1":
        return dot(a.astype(BF16), b.astype(BF16))
    bh, bl = _split(b)
    if mode == "b2":
        ah = a.astype(BF16)
        return dot(ah, bh) + dot(ah, bl)
    ah, al = _split(a)
    return dot(ah, bh) + (dot(ah, bl) + dot(al, bh))


def _dot_exact_lhs(a_bf16, x):
    hi, mid, lo = _split3(x)
    return _dot(a_bf16, hi) + (_dot(a_bf16, mid) + _dot(a_bf16, lo))


def _dot_exact_rhs(x, b_bf16, terms=3):
    if terms == 2:
        hi, lo = _split(x)
        return _dot(hi, b_bf16) + _dot(lo, b_bf16)
    hi, mid, lo = _split3(x)
    return _dot(hi, b_bf16) + (_dot(mid, b_bf16) + _dot(lo, b_bf16))


def _pack_bf16_pairs(x):
    n = x.shape[1] // 2
    lo = pltpu.bitcast(x[:, :n].astype(BF16).astype(F32), jnp.uint32)
    hi = pltpu.bitcast(x[:, n:].astype(BF16).astype(F32), jnp.uint32)
    return hi | (lo >> 16)


def _unpack_bf16_pairs(w):
    lo = pltpu.bitcast(w << 16, F32)
    hi = pltpu.bitcast(w & jnp.uint32(0xFFFF0000), F32)
    return lo, hi


def _sigmoid(x):
    return 1.0 / (1.0 + jnp.exp(-x))


def _softplus(x):
    return jnp.maximum(x, 0.0) + jnp.log1p(jnp.exp(-jnp.abs(x)))


def _block_diag_ones(width):
    h = jnp.arange(width) // HEAD_DIM
    return (h[:, None] == h[None, :]).astype(F32)


CAST_ROWS = 128


def _cast_rows(src_ref, dst_ref):
    n_src, n_dst = src_ref.shape[1], dst_ref.shape[1]
    whole = n_src // LANE * LANE

    def step(r, carry):
        rows = pl.ds(pl.multiple_of(r * CAST_ROWS, CAST_ROWS), CAST_ROWS)
        dst_ref[rows, :whole] = src_ref[rows, :whole].astype(BF16)
        if n_dst > whole:
            tail = [src_ref[rows, whole:]] if n_src > whole else []
            tail.append(jnp.zeros((CAST_ROWS, n_dst - n_src), F32))
            dst_ref[rows, whole:] = jnp.concatenate(tail, axis=1).astype(BF16)
        return carry
    lax.fori_loop(0, src_ref.shape[0] // CAST_ROWS, step, 0)


def _gmlp_gate(p, lng_ref, lnb_ref, ws_ref, bst_ref, o_ref):
    gw = p.shape[1] // 2
    p = 0.5 * p * (1.0 + lax.erf(p * math.sqrt(0.5)))
    u, v = p[:, :gw], p[:, gw:]
    mu = jnp.mean(v, axis=-1, keepdims=True)
    vc = v - mu
    var = jnp.mean(vc * vc, axis=-1, keepdims=True)
    vn = vc * lax.rsqrt(var + LN_EPS) * lng_ref[...] + lnb_ref[...]
    for c in range(p.shape[0] // GMLP_CHUNK):
        rows = slice(c * GMLP_CHUNK, (c + 1) * GMLP_CHUNK)
        ys = []
        for h in range(gw // HEAD_DIM):
            cols = slice(h * HEAD_DIM, (h + 1) * HEAD_DIM)
            ys.append(_dot(ws_ref[h], vn[rows, cols].astype(BF16)) + bst_ref[:, h:h + 1])
        o_ref[rows, :] = u[rows, :] * jnp.concatenate(ys, axis=1)


def _proj_body(x_ref, w_ref, lng_ref, lnb_ref, ws_ref, bst_ref, yg_ref, pr_ref, pm_ref, wb_ref, *, ng, nr):
    @pl.when(pl.program_id(0) == 0)
    def _():
        _cast_rows(w_ref, wb_ref)

    xb = x_ref[...].astype(BF16)
    _gmlp_gate(_dot(xb, wb_ref[:, :ng]), lng_ref, lnb_ref, ws_ref, bst_ref, yg_ref)
    pr_ref[...] = _dot(xb, wb_ref[:, ng:ng + nr])
    pm_ref[...] = _dot(xb, wb_ref[:, ng + nr:])


def _proj(x, w_in, layer, ng, nr, ln_g, ln_b, ws, bs, tm=512):
    t, d = x.shape
    p_in = w_in.shape[2]
    p_pad = -(-p_in // LANE) * LANE
    nm = p_pad - ng - nr
    gw = ng // 2
    n_heads = gw // HEAD_DIM
    bst = jnp.zeros((GMLP_CHUNK, LANE), F32).at[:, :n_heads].set(bs.T)
    row = lambda n: pl.BlockSpec((tm, n), lambda i: (i, 0))
    return pl.pallas_call(
        functools.partial(_proj_body, ng=ng, nr=nr),
        grid=(t // tm,),
        in_specs=[row(d), pl.BlockSpec((None, d, p_in), lambda i: (layer, 0, 0), pipeline_mode=pl.Buffered(1)),
                  _full((1, gw)), _full((1, gw)), _full((n_heads, GMLP_CHUNK, GMLP_CHUNK)),
                  _full((GMLP_CHUNK, LANE))],
        out_specs=[row(gw), row(nr), row(nm)],
        out_shape=[jax.ShapeDtypeStruct((t, n), F32) for n in (gw, nr, nm)],
        scratch_shapes=[pltpu.VMEM((d, p_pad), BF16)],
        compiler_params=_cparams(1),
        name="in_proj",
    )(x, w_in, ln_g.reshape(1, gw), ln_b.reshape(1, gw), ws.astype(BF16), bst)


def _halo_specs(tm, width, n_rows):
    per8 = tm // SUBLANE
    last = n_rows // SUBLANE - 1
    prev = pl.BlockSpec((SUBLANE, width), lambda i: (jnp.maximum(i * per8 - 1, 0), 0))
    nxt = pl.BlockSpec((SUBLANE, width), lambda i: (jnp.minimum((i + 1) * per8, last), 0))
    return prev, nxt


def _neighbours(cur, prev_blk, next_blk, tiles_per_seq):
    tm = cur.shape[0]
    j = pl.program_id(0) % tiles_per_seq
    prev_row = jnp.where(j > 0, prev_blk[SUBLANE - 1:SUBLANE, :], 0.0)
    next_row = jnp.where(j < tiles_per_seq - 1, next_blk[0:1, :], 0.0)
    ridx = lax.broadcasted_iota(jnp.int32, cur.shape, 0)
    before = jnp.where(ridx == 0, prev_row, pltpu.roll(cur, 1, 0))
    after = jnp.where(ridx == tm - 1, next_row, pltpu.roll(cur, tm - 1, 0))
    return before, after


def _rwkv_prep_body(pr_ref, prev_ref, next_ref, mu_ref, w0_ref, w2_ref, a0_ref, a2_ref, g2_ref,
                    kk_ref, ka_ref, rk_ref, bd_ref,
                    r_out, v_out, a_out, kd_out, b_out, lw_out, bonus_out, gate_out, *, rw, tiles_per_seq):
    pf = pr_ref[...]
    before, after = _neighbours(pf, prev_ref[...], next_ref[...], tiles_per_seq)
    pf = pf + mu_ref[0:1, :] * (before - pf) + mu_ref[1:2, :] * (after - pf)
    o3 = 3 * rw
    r, k, v = pf[:, :rw], pf[:, rw:2 * rw], pf[:, 2 * rw:o3]
    wd = pf[:, o3:o3 + W_LORA]
    ad = pf[:, o3 + W_LORA:o3 + W_LORA + A_LORA]
    gd = pf[:, o3 + W_LORA + A_LORA:]
    bd = bd_ref[...]
    kk = k * kk_ref[...]
    ss = _dot_exact_rhs(kk * kk, bd, terms=2)
    kk = kk / jnp.maximum(jnp.sqrt(ss), 1e-12)
    twd = jnp.tanh(wd)
    ksum = jnp.zeros_like(k)
    for d in range(2):
        w_log = -_softplus(-(w0_ref[d:d + 1, :] + _mm(twd, w2_ref[d], "b3"))) - 0.5
        lw_out[d] = -jnp.exp(w_log)
        iclr = _sigmoid(a0_ref[d:d + 1, :] + _mm(ad, a2_ref[d], "b3"))
        kd = k * (1.0 + (iclr - 1.0) * ka_ref[...])
        kd_out[d] = kd
        b_out[d] = kk * iclr
        ksum = ksum + kd
    r_out[...] = r
    v_out[...] = v
    a_out[...] = -kk
    bonus_out[...] = _dot_exact_rhs(r * ksum * rk_ref[...], bd, terms=2) * v
    gate_out[...] = _dot(_sigmoid(gd).astype(BF16), g2_ref[...])


def _rwkv_prep(pr, seq, mu, w0, w2, a0, a2, g2, k_k, k_a, r_k, tm=256):
    t, rproj = pr.shape
    rw = w0.shape[1]
    tiles_per_seq = seq // tm
    prev, nxt = _halo_specs(tm, rproj, t)
    row = pl.BlockSpec((tm, rw), lambda i: (i, 0))
    row2 = pl.BlockSpec((2, tm, rw), lambda i: (0, i, 0))
    one = jax.ShapeDtypeStruct((t, rw), F32)
    two = jax.ShapeDtypeStruct((2, t, rw), F32)
    return pl.pallas_call(
        functools.partial(_rwkv_prep_body, rw=rw, tiles_per_seq=tiles_per_seq),
        grid=(t // tm,),
        in_specs=[pl.BlockSpec((tm, rproj), lambda i: (i, 0)), prev, nxt,
                  _full((2, rproj)), _full((2, rw)), _full((2, W_LORA, rw)), _full((2, rw)),
                  _full((2, A_LORA, rw)), _full((G_LORA, rw)), _full((1, rw)), _full((1, rw)),
                  _full((1, rw)), _full((rw, rw))],
        out_specs=[row, row, row, row2, row2, row2, row, row],
        out_shape=[one, one, one, two, two, two, one, one],
        compiler_params=_cparams(1),
        name="rwkv_prep",
    )(pr, pr, pr, mu, w0, w2, a0, a2, g2.astype(BF16), k_k.reshape(1, rw), k_a.reshape(1, rw),
      r_k.reshape(1, rw), _block_diag_ones(rw).astype(BF16))


P_G, P_INV, P_APPLY, P_STATE, P_SEQ = "b1", "b1", "b1", "b1", "b2"


def _rwkv_intra_body(r_ref, v_ref, a_ref, kd_ref, b_ref, lw_ref, rq_out, o0_out, mtx_out, hc_out,
                     *, n_heads, chunks):
    L = RWKV_CHUNK
    d = pl.program_id(0)
    row = lax.broadcasted_iota(jnp.int32, (L, L), 0)
    col = lax.broadcasted_iota(jnp.int32, (L, L), 1)
    fwd = d == 0
    rel = (col - row) * (1 - 2 * d)
    incl = rel <= 0
    strict = rel < 0
    eye = (row == col).astype(F32)
    tri = incl.astype(BF16)
    pairs = []
    for c in range(chunks):
        rows = slice(c * L, (c + 1) * L)
        lw = lw_ref[rows, :]
        cum = _dot_exact_lhs(tri, lw)
        tot = jnp.where(fwd, cum[L - 1:L, :], cum[0:1, :])
        e_neg = jnp.exp(-cum)
        e_end = jnp.exp(tot - cum)
        e_tot = jnp.exp(tot)
        r, v, a, kd, b = r_ref[rows, :], v_ref[rows, :], a_ref[rows, :], kd_ref[rows, :], b_ref[rows, :]
        at, rt, bt, kt = a * jnp.exp(cum - lw), r * jnp.exp(cum), b * e_neg, kd * e_neg
        kend, bend = kd * e_end, b * e_end
        for h in range(n_heads):
            sl = slice(h * HEAD_DIM, (h + 1) * HEAD_DIM)
            pairs.append(dict(at=at[:, sl], rt=rt[:, sl], bt=bt[:, sl], kt=kt[:, sl], v=v[:, sl],
                              kend=kend[:, sl], bend=bend[:, sl], e_tot=e_tot[:, sl]))
    for p in pairs:
        p["g"] = _mm(jnp.concatenate([p["at"], p["rt"]], axis=0),
                     jnp.concatenate([p["bt"], p["kt"]], axis=0), P_G, _dot_nt)
    row2 = lax.broadcasted_iota(jnp.int32, (L, 2 * L), 0)
    col2 = lax.broadcasted_iota(jnp.int32, (L, 2 * L), 1) & (L - 1)
    rel2 = (col2 - row2) * (1 - 2 * d)
    incl2 = rel2 <= 0
    strict2 = rel2 < 0
    zeros = jnp.zeros((L, HEAD_DIM), F32)
    for p in pairs:
        g = p.pop("g")
        a_both = jnp.where(strict2, g[:L, :], 0.0)
        p["m_both"] = jnp.where(incl2, g[L:, :], 0.0)
        p["pw"] = a_both[:, :L]
        p["a_ak"] = a_both[:, L:]
        p["inv"] = eye + p["pw"]
    for _ in range(int(math.log2(L)) - 1):
        for p in pairs:
            p["pw"] = _mm(p["pw"], p["pw"], P_INV)
        for p in pairs:
            p["inv"] = p["inv"] + _mm(p["inv"], p["pw"], P_INV)
    for p in pairs:
        p["akv"] = _mm(p["a_ak"], p["v"], P_APPLY)
    for p in pairs:
        wu = _mm(p["inv"], jnp.concatenate([p["at"], p["akv"]], axis=1), P_APPLY)
        p["rhs"] = jnp.concatenate([wu, jnp.concatenate([zeros, p["v"]], axis=1)], axis=0)
    for p in pairs:
        p["rq_o0"] = _mm(p["m_both"], p["rhs"], P_APPLY)
    for p in pairs:
        p["m_hc"] = _mm(jnp.concatenate([p["bend"], p["kend"]], axis=0), p["rhs"], P_STATE, _dot_tn)
    for c in range(chunks):
        ps = pairs[c * n_heads:(c + 1) * n_heads]
        rows = slice(c * L, (c + 1) * L)
        krows = slice(c * HEAD_DIM, (c + 1) * HEAD_DIM)
        rq_out[rows, :] = jnp.concatenate([p["rt"] + p["rq_o0"][:, :HEAD_DIM] for p in ps], axis=1)
        o0_out[rows, :] = jnp.concatenate([p["rq_o0"][:, HEAD_DIM:] for p in ps], axis=1)
        mtx_out[krows, :] = jnp.concatenate([eye * p["e_tot"] + p["m_hc"][:, :HEAD_DIM] for p in ps], axis=1)
        hc_out[krows, :] = jnp.concatenate([p["m_hc"][:, HEAD_DIM:] for p in ps], axis=1)


def _rwkv_intra(r, v, a, kd, b, lw, chunks=8):
    t, rw = r.shape
    n_heads = rw // HEAD_DIM
    tm = chunks * RWKV_CHUNK
    tk = chunks * HEAD_DIM
    n_tiles = t // tm
    one = pl.BlockSpec((tm, rw), lambda d, i: (i, 0))
    two = pl.BlockSpec((None, tm, rw), lambda d, i: (d, i, 0))
    twok = pl.BlockSpec((None, tk, rw), lambda d, i: (d, i, 0))
    return pl.pallas_call(
        functools.partial(_rwkv_intra_body, n_heads=n_heads, chunks=chunks),
        grid=(2, n_tiles),
        in_specs=[one, one, one, two, two, two],
        out_specs=[two, two, twok, twok],
        out_shape=[jax.ShapeDtypeStruct((2, t, rw), F32), jax.ShapeDtypeStruct((2, t, rw), F32),
                   jax.ShapeDtypeStruct((2, n_tiles * tk, rw), F32),
                   jax.ShapeDtypeStruct((2, n_tiles * tk, rw), F32)],
        compiler_params=_cparams(2),
        name="rwkv_intra",
    )(r, v, a, kd, b, lw)


def _rwkv_seq_body(rq0, o00, mtx0, hc0, rq1, o01, mtx1, hc1, out0, out1, h_ref, *, n_heads, batch):
    c = pl.program_id(0)

    @pl.when(c == 0)
    def _():
        h_ref[...] = jnp.zeros_like(h_ref)

    L = RWKV_CHUNK
    for d, (rq, o0, mtx, hc, out) in enumerate(((rq0, o00, mtx0, hc0, out0), (rq1, o01, mtx1, hc1, out1))):
        for bi in range(batch):
            rq_t, mtx_t = rq[bi], mtx[bi]
            state = h_ref[d, bi]
            outs, states = [], []
            for h in range(n_heads):
                sl = slice(h * HEAD_DIM, (h + 1) * HEAD_DIM)
                prod = _mm(jnp.concatenate([rq_t[:, sl], mtx_t[:, sl]], axis=0), state[:, sl], P_SEQ)
                outs.append(prod[:L])
                states.append(prod[L:])
            out[bi] = jnp.concatenate(outs, axis=1) + o0[bi]
            h_ref[d, bi] = jnp.concatenate(states, axis=1) + hc[bi]


def _rwkv_seq(rq, o0, mtx, hc, batch, seq):
    _, t, rw = rq.shape
    n_heads = rw // HEAD_DIM
    L = RWKV_CHUNK
    nc = seq // L
    as4 = lambda x: x.reshape(2, batch, x.shape[1] // batch, rw)
    rq, o0, mtx, hc = as4(rq), as4(o0), as4(mtx), as4(hc)
    fwd = lambda rows: pl.BlockSpec((None, batch, rows, rw), lambda c: (0, 0, c, 0))
    bwd = lambda rows: pl.BlockSpec((None, batch, rows, rw), lambda c: (1, 0, nc - 1 - c, 0))
    out0, out1 = pl.pallas_call(
        functools.partial(_rwkv_seq_body, n_heads=n_heads, batch=batch),
        grid=(nc,),
        in_specs=[fwd(L), fwd(L), fwd(HEAD_DIM), fwd(HEAD_DIM), bwd(L), bwd(L), bwd(HEAD_DIM), bwd(HEAD_DIM)],
        out_specs=[pl.BlockSpec((batch, L, rw), lambda c: (0, c, 0)),
                   pl.BlockSpec((batch, L, rw), lambda c: (0, nc - 1 - c, 0))],
        out_shape=[jax.ShapeDtypeStruct((batch, seq, rw), F32)] * 2,
        scratch_shapes=[pltpu.VMEM((2, batch, HEAD_DIM, rw), F32)],
        compiler_params=_cparams(1),
        name="rwkv_seq",
    )(rq, o0, mtx, hc, rq, o0, mtx, hc)
    return out0.reshape(t, rw), out1.reshape(t, rw)


def _rwkv_scan(r, v, a, kd, b, lw, batch, seq):
    rq, o0, mtx, hc = _rwkv_intra(r, v, a, kd, b, lw)
    return _rwkv_seq(rq, o0, mtx, hc, batch, seq)


def _mlstm_prep_body(qk_ref, prev_ref, next_ref, g_ref, cw_ref, cb_ref, gb_ref, q_out, k_out, gate_out,
                     *, mw, n_heads, tiles_per_seq):
    x = qk_ref[...]
    before, after = _neighbours(x, prev_ref[...], next_ref[...], tiles_per_seq)
    y = cb_ref[...] + before * cw_ref[0:1, :] + x * cw_ref[1:2, :] + after * cw_ref[2:3, :]
    y = y * _sigmoid(y)
    q_out[...] = y[:, :mw]
    k_out[...] = y[:, mw:] * (HEAD_DIM ** -0.5)
    g = g_ref[...] + gb_ref[...]
    lane = lax.broadcasted_iota(jnp.int32, g.shape, 1)
    for d in range(2):
        ig = g if d == 0 else pltpu.roll(g, LANE - n_heads, 1)
        fg = pltpu.roll(g, LANE - (1 + d) * n_heads, 1)
        lf = -_softplus(-fg)
        gate_out[d] = jnp.where(lane < n_heads, ig, jnp.where(lane < 2 * n_heads, lf, 0.0))


def _mlstm_prep(pm, seq, conv_w, conv_b, gate_b, mw, tm=256):
    t = pm.shape[0]
    n_heads = mw // HEAD_DIM
    tiles_per_seq = seq // tm
    w2 = 2 * mw
    prev, nxt = _halo_specs(tm, w2, t)
    gcol = (4 * mw) // LANE
    gb = jnp.zeros((1, LANE), F32).at[0, :4 * n_heads].set(gate_b)
    row = pl.BlockSpec((tm, mw), lambda i: (i, 0))
    return pl.pallas_call(
        functools.partial(_mlstm_prep_body, mw=mw, n_heads=n_heads, tiles_per_seq=tiles_per_seq),
        grid=(t // tm,),
        in_specs=[pl.BlockSpec((tm, w2), lambda i: (i, 0)), prev, nxt,
                  pl.BlockSpec((tm, LANE), lambda i: (i, gcol)),
                  _full((3, w2)), _full((1, w2)), _full((1, LANE))],
        out_specs=[row, row, pl.BlockSpec((2, tm, LANE), lambda i: (0, i, 0))],
        out_shape=[jax.ShapeDtypeStruct((t, mw), F32), jax.ShapeDtypeStruct((t, mw), F32),
                   jax.ShapeDtypeStruct((2, t, LANE), F32)],
        compiler_params=_cparams(1),
        name="mlstm_prep",
    )(pm, pm, pm, pm, conv_w, conv_b.reshape(1, w2), gb)


def _mlstm_scan_body(q0_ref, k0_ref, v0_ref, g0_ref, q1_ref, k1_ref, v1_ref, g1_ref, o0_ref, o1_ref,
                     c_ref, m_ref, *, n_heads, group):
    L = MLSTM_CHUNK
    H = n_heads

    @pl.when(pl.program_id(1) == 0)
    def _():
        c_ref[...] = jnp.zeros_like(c_ref)
        m_ref[...] = jnp.zeros_like(m_ref)

    row = lax.broadcasted_iota(jnp.int32, (L, L), 0)
    col = lax.broadcasted_iota(jnp.int32, (L, L), 1)
    trow = lax.broadcasted_iota(jnp.int32, (L, LANE), 0)
    low = lax.broadcasted_iota(jnp.int32, (L, LANE), 1) < HEAD_DIM
    xr = lax.broadcasted_iota(jnp.int32, (LANE, H * L), 0)
    xc = lax.broadcasted_iota(jnp.int32, (LANE, H * L), 1)
    spread = (xr - H == lax.shift_right_logical(xc, int(math.log2(L)))).astype(BF16)
    hs = []
    dirs = ((q0_ref, k0_ref, v0_ref, g0_ref), (q1_ref, k1_ref, v1_ref, g1_ref))
    for bi, d in [(bi, d) for bi in range(group) for d in range(2)]:
        q_ref, k_ref, v_ref, g_ref = dirs[d]
        incl = (col <= row) if d == 0 else (col >= row)
        last = L - 1 if d == 0 else 0
        g = g_ref[bi]
        bcum = _dot_exact_lhs(incl.astype(BF16), g)
        z = pltpu.roll(g, H, 1) - bcum
        cmax = z
        shift = 1
        while shift < L:
            if d == 0:
                moved = jnp.where(trow >= shift, pltpu.roll(cmax, shift, 0), -jnp.inf)
            else:
                moved = jnp.where(trow < L - shift, pltpu.roll(cmax, L - shift, 0), -jnp.inf)
            cmax = jnp.maximum(cmax, moved)
            shift *= 2
        m_prev = m_ref[bi, d, 0:1, :]
        top = jnp.maximum(cmax, m_prev)
        b_last = bcum[last:last + 1, :]
        lwc = b_last + z
        m_new = jnp.maximum(b_last + m_prev, jnp.max(lwc, axis=0, keepdims=True))
        m_ref[bi, d, 0:1, :] = m_new
        alpha_w = _dot_exact_rhs(-top, spread)
        floor_w = jnp.exp(-_dot_exact_rhs(bcum + top, spread, terms=2))
        wts_w = _dot(jnp.exp(lwc - m_new).astype(BF16), spread)
        rows_w = _dot_exact_rhs(jnp.concatenate(
            [jnp.broadcast_to(m_prev, (SUBLANE, LANE)),
             jnp.broadcast_to(jnp.exp(b_last + m_prev - m_new), (SUBLANE, LANE))], axis=0), spread)
        z_t = z.T
        q, k, v = q_ref[bi], k_ref[bi], v_ref[bi]
        for h in range(H):
            slab = slice(h // 2 * LANE, (h // 2 + 1) * LANE)
            cols = slice(h * L, (h + 1) * L)
            mine = low if h % 2 == 0 else jnp.logical_not(low)
            kh = jnp.where(mine, k[:, slab], 0.0)
            hs.append(dict(
                qh=jnp.where(mine, q[:, slab], 0.0).astype(BF16), kh=kh.astype(BF16),
                vext=jnp.where(mine, v[:, slab], 1.0).astype(BF16),
                decay=jnp.exp(jnp.where(incl, alpha_w[:, cols] + z_t[H + h:H + h + 1, :], -jnp.inf)),
                w_inter=jnp.exp(alpha_w[:, cols] + rows_w[0:1, cols]), floor=floor_w[:, cols],
                wk=(wts_w[:, cols] * kh).astype(BF16), dec=rows_w[SUBLANE:SUBLANE + 1, cols],
                cst=c_ref[bi, d, h]))
    for p in hs:
        p["sc"] = (_dot_nt(p["qh"], p["kh"]) * p["decay"]).astype(BF16)
    for p in hs:
        p["numext"] = _dot(p["sc"], p["vext"]) + p["w_inter"] * _dot(p["qh"], p["cst"].astype(BF16))
    for p in hs:
        p["upd"] = _dot_tn(p["wk"], p["vext"])
    for bi, d in [(bi, d) for bi in range(group) for d in range(2)]:
        o_ref = (o0_ref, o1_ref)[d]
        res = []
        for h in range(H):
            p = hs[(bi * 2 + d) * H + h]
            den = pltpu.roll(p["numext"], HEAD_DIM, 1)
            res.append(p["numext"] / jnp.maximum(jnp.abs(den), p["floor"]))
            c_ref[bi, d, h] = p["dec"] * p["cst"] + p["upd"]
        for pair in range(H // 2):
            o_ref[bi, :, pair * LANE:(pair + 1) * LANE] = jnp.where(low, res[2 * pair], res[2 * pair + 1])


def _mlstm_scan(q, k, pm, gates, batch, seq):
    t, mw = q.shape
    n_heads = mw // HEAD_DIM
    L = MLSTM_CHUNK
    nc = seq // L
    group = 2 if batch % 2 == 0 else 1
    q3, k3, pm3 = (x.reshape(batch, seq, x.shape[1]) for x in (q, k, pm))
    g4 = gates.reshape(2, batch, seq, LANE)
    specs = []
    for d, blk in enumerate((lambda c: c, lambda c: nc - 1 - c)):
        specs += [pl.BlockSpec((group, L, mw), lambda gi, c, blk=blk: (gi, blk(c), 0)),
                  pl.BlockSpec((group, L, mw), lambda gi, c, blk=blk: (gi, blk(c), 0)),
                  pl.BlockSpec((group, L, mw), lambda gi, c, blk=blk: (gi, blk(c), 2)),
                  pl.BlockSpec((None, group, L, LANE), lambda gi, c, blk=blk, d=d: (d, gi, blk(c), 0))]
    out0, out1 = pl.pallas_call(
        functools.partial(_mlstm_scan_body, n_heads=n_heads, group=group),
        grid=(batch // group, nc),
        in_specs=specs,
        out_specs=[pl.BlockSpec((group, L, mw), lambda gi, c: (gi, c, 0)),
                   pl.BlockSpec((group, L, mw), lambda gi, c: (gi, nc - 1 - c, 0))],
        out_shape=[jax.ShapeDtypeStruct((batch, seq, mw), F32)] * 2,
        scratch_shapes=[pltpu.VMEM((group, 2, n_heads, LANE, LANE), F32),
                        pltpu.VMEM((group, 2, SUBLANE, LANE), F32)],
        compiler_params=_cparams(2),
        name="mlstm_scan",
    )(q3, k3, pm3, g4, q3, k3, pm3, g4)
    return out0.reshape(t, mw), out1.reshape(t, mw)


def _layer_norm(x, g, b):
    mu = jnp.mean(x, axis=-1, keepdims=True)
    xc = x - mu
    var = jnp.mean(xc * xc, axis=-1, keepdims=True)
    return xc * lax.rsqrt(var + LN_EPS) * g + b


def _head_norm(x, bd_mean, eps):
    mu = _dot_exact_rhs(x, bd_mean, terms=2)
    xc = x - mu
    var = _dot_exact_rhs(xc * xc, bd_mean, terms=2)
    return xc * lax.rsqrt(var + eps)


def _mix_out_body(x_ref, yg_ref, ro0_ref, ro1_ref, bonus_ref, rgate_ref, rlg_ref, rlb_ref, mh0_ref, mh1_ref, og_ref,
                  mlg_ref, w_ref, l1g_ref, l1b_ref, rw_ref, rb_ref, bdm_ref,
                  x1_out, x1p_out, topi_out, gate_out, wb_ref, *, alpha, gw, rw):
    @pl.when(pl.program_id(0) == 0)
    def _():
        _cast_rows(w_ref, wb_ref)

    bdm = bdm_ref[...]
    yr = _head_norm(ro0_ref[...] + ro1_ref[...], bdm, RWKV_GN_EPS) * rlg_ref[...] + rlb_ref[...]
    yr = (yr + bonus_ref[...]) * rgate_ref[...]
    ym = _sigmoid(og_ref[...]) * (_head_norm(mh0_ref[...] + mh1_ref[...], bdm, LN_EPS) * mlg_ref[...])
    mix = (_dot(yg_ref[...].astype(BF16), wb_ref[:gw, :]) + _dot(yr.astype(BF16), wb_ref[gw:gw + rw, :])
           + _dot(ym.astype(BF16), wb_ref[gw + rw:, :]))
    x1 = _layer_norm(alpha * x_ref[...] + mix, l1g_ref[...], l1b_ref[...])
    x1_out[...] = x1
    x1p_out[...] = _pack_bf16_pairs(x1)
    lg = _mm(x1, rw_ref[...], "b3") + rb_ref[...]
    lane = lax.broadcasted_iota(jnp.int32, lg.shape, 1)
    vals, topi = [], jnp.zeros(lg.shape, jnp.int32)
    for j in range(TOP_K):
        mx = jnp.max(lg, axis=1, keepdims=True)
        idx = jnp.min(jnp.where(lg == mx, lane, LANE), axis=1, keepdims=True)
        vals.append(mx)
        topi = jnp.where(lane == j, idx, topi)
        lg = jnp.where(lane == idx, -jnp.inf, lg)
    es = [jnp.exp(vj - vals[0]) for vj in vals]
    den = es[0] + es[1] + es[2] + es[3]
    gate = jnp.zeros(lg.shape, F32)
    for j in range(TOP_K):
        gate = jnp.where(lane == j, es[j] / den, gate)
    topi_out[...] = topi.T[:SUBLANE, :]
    gate_out[...] = gate


def _mix_out(x, yg, ro, bonus, rgate, rlg, rlb, mh, pm, mlg, w_out, layer, l1g, l1b, router_w, router_b, alpha,
             tm=256):
    t, dm = x.shape
    gw, rw, mw = yg.shape[1], bonus.shape[1], mh[0].shape[1]
    assert rw == mw
    rwp = jnp.zeros((dm, LANE), F32).at[:, :N_EXPERTS].set(router_w)
    rbp = jnp.full((1, LANE), NEG_BIG, F32).at[0, :N_EXPERTS].set(router_b)
    row = lambda n: pl.BlockSpec((tm, n), lambda i: (i, 0))
    vec = lambda n: _full((1, n))
    return pl.pallas_call(
        functools.partial(_mix_out_body, alpha=alpha, gw=gw, rw=rw),
        grid=(t // tm,),
        in_specs=[row(dm), row(gw), row(rw), row(rw), row(rw), row(rw), vec(rw), vec(rw), row(mw), row(mw),
                  pl.BlockSpec((tm, mw), lambda i: (i, 3)),
                  vec(mw),
                  pl.BlockSpec((None, dm, dm), lambda i: (layer, 0, 0), pipeline_mode=pl.Buffered(1)),
                  vec(dm), vec(dm), _full((dm, LANE)), vec(LANE), _full((rw, rw))],
        out_specs=[row(dm), row(dm // 2), pl.BlockSpec((SUBLANE, tm), lambda i: (0, i)), row(LANE)],
        out_shape=[jax.ShapeDtypeStruct((t, dm), F32), jax.ShapeDtypeStruct((t, dm // 2), jnp.uint32),
                   jax.ShapeDtypeStruct((SUBLANE, t), jnp.int32), jax.ShapeDtypeStruct((t, LANE), F32)],
        scratch_shapes=[pltpu.VMEM((dm, dm), BF16)],
        compiler_params=_cparams(1),
        name="mix_out",
    )(x, yg, ro[0], ro[1], bonus, rgate, rlg.reshape(1, rw), rlb.reshape(1, rw), mh[0], mh[1], pm,
      mlg.reshape(1, mw), w_out, l1g.reshape(1, dm), l1b.reshape(1, dm), rwp, rbp,
      (_block_diag_ones(rw) / HEAD_DIM).astype(BF16))


def _moe_body(be_ref, nu_ref, ve_ref, xs_ref, w1_ref, b1_ref, w2_ref, b2_ref, o_ref, *, dff):
    i = pl.program_id(0)
    active = i < nu_ref[0]

    @pl.when(active)
    def _():
        rowid = i * MOE_BLOCK + lax.broadcasted_iota(jnp.int32, (MOE_BLOCK, 1), 0)
        lo, hi = _unpack_bf16_pairs(jnp.where(rowid < ve_ref[i], xs_ref[...], jnp.uint32(0)))
        xs = jnp.concatenate([lo.astype(BF16), hi.astype(BF16)], axis=1)
        hdn = _dot(xs, w1_ref[...].astype(BF16)) + b1_ref[...]
        g_ = jnp.minimum(hdn[:, :dff], SWIGLU_LIMIT)
        u_ = jnp.clip(hdn[:, dff:], -SWIGLU_LIMIT, SWIGLU_LIMIT)
        act = (u_ + 1.0) * (g_ * _sigmoid(g_ * SWIGLU_ALPHA))
        o_ref[...] = _pack_bf16_pairs(_dot(act.astype(BF16), w2_ref[...].astype(BF16)) + b2_ref[...])

    @pl.when(jnp.logical_not(active))
    def _():
        o_ref[...] = jnp.zeros_like(o_ref)


def _moe_experts(xs, block_e, n_used, valid_end, w1, b1, w2, b2, layer):
    rows, half = xs.shape
    nb = rows // MOE_BLOCK
    depth, ne, dm, dff2 = w1.shape
    dff = dff2 // 2
    grid_spec = pltpu.PrefetchScalarGridSpec(
        num_scalar_prefetch=3,
        grid=(nb,),
        in_specs=[pl.BlockSpec((MOE_BLOCK, half), lambda i, be, nu, ve: (i, 0)),
                  pl.BlockSpec((None, None, dm, dff2), lambda i, be, nu, ve: (layer, be[i], 0, 0)),
                  pl.BlockSpec((None, None, 1, dff2), lambda i, be, nu, ve: (layer, be[i], 0, 0)),
                  pl.BlockSpec((None, None, dff, dm), lambda i, be, nu, ve: (layer, be[i], 0, 0)),
                  pl.BlockSpec((None, None, 1, dm), lambda i, be, nu, ve: (layer, be[i], 0, 0))],
        out_specs=pl.BlockSpec((MOE_BLOCK, half), lambda i, be, nu, ve: (i, 0)),
    )
    return pl.pallas_call(
        functools.partial(_moe_body, dff=dff),
        grid_spec=grid_spec,
        out_shape=jax.ShapeDtypeStruct((rows, half), jnp.uint32),
        compiler_params=_cparams(1),
        name="moe_experts",
    )(block_e, n_used, valid_end, xs, w1, b1.reshape(depth, ne, 1, dff2), w2, b2.reshape(depth, ne, 1, dm))


N_STREAMS = 1
PLAN_TILE = 512
MOE_BLOCK_SHIFT = MOE_BLOCK.bit_length() - 1
assert 1 << MOE_BLOCK_SHIFT == MOE_BLOCK


def _moe_plan_body(e_ref, dest_ref, meta_ref, rank_ref, *, n_tokens, meta_lanes):
    tiles_per_row = n_tokens // PLAN_TILE
    n_tiles = TOP_K * tiles_per_row
    expert = lax.broadcasted_iota(jnp.int32, (N_EXPERTS, PLAN_TILE), 0)
    r_i = lax.broadcasted_iota(jnp.int32, (PLAN_TILE, PLAN_TILE), 0)
    c_i = lax.broadcasted_iota(jnp.int32, (PLAN_TILE, PLAN_TILE), 1)
    earlier = (r_i < c_i).astype(BF16)

    def tile_hits(it):
        j = it // tiles_per_row
        lanes = pl.ds(pl.multiple_of((it % tiles_per_row) * PLAN_TILE, PLAN_TILE), PLAN_TILE)
        return j, lanes, e_ref[pl.ds(j, 1), lanes] == expert

    def rank_step(it, seen):
        j, lanes, hit = tile_hits(it)
        hitf = hit.astype(F32)
        prior = _dot(hit.astype(BF16), earlier) + seen
        rank_ref[pl.ds(j, 1), lanes] = jnp.sum(hitf * prior, axis=0, keepdims=True)
        return seen + jnp.sum(hitf, axis=1, keepdims=True)

    dest_ref[...] = jnp.zeros_like(dest_ref)
    rank_ref[...] = jnp.zeros_like(rank_ref)
    counts = lax.fori_loop(0, n_tiles, rank_step, jnp.zeros((N_EXPERTS, 1), F32))
    padded = ((counts.astype(jnp.int32) + (MOE_BLOCK - 1)) >> MOE_BLOCK_SHIFT) << MOE_BLOCK_SHIFT
    er = lax.broadcasted_iota(jnp.int32, (N_EXPERTS, N_EXPERTS), 0)
    ec = lax.broadcasted_iota(jnp.int32, (N_EXPERTS, N_EXPERTS), 1)
    seg_end = _dot_exact_lhs((ec <= er).astype(BF16),
                             jnp.broadcast_to(padded.astype(F32), (N_EXPERTS, LANE)))[:, 0:1]
    seg_start = seg_end - padded.astype(F32)

    def dest_step(it, carry):
        j, lanes, hit = tile_hits(it)
        base = jnp.sum(jnp.where(hit, seg_start, 0.0), axis=0, keepdims=True)
        dest_ref[pl.ds(j, 1), lanes] = (rank_ref[pl.ds(j, 1), lanes] + base).astype(jnp.int32)
        return carry

    lax.fori_loop(0, n_tiles, dest_step, 0)
    blk_start = (lax.broadcasted_iota(jnp.int32, (N_EXPERTS, meta_lanes), 1) * MOE_BLOCK).astype(F32)
    blk_expert = jnp.minimum(jnp.sum((seg_end <= blk_start).astype(F32), axis=0, keepdims=True), N_EXPERTS - 1.0)
    mine = lax.broadcasted_iota(jnp.int32, (N_EXPERTS, meta_lanes), 0).astype(F32) == blk_expert
    valid_end = jnp.sum(jnp.where(mine, seg_start + counts, 0.0), axis=0, keepdims=True)
    n_used = jnp.broadcast_to(seg_end[N_EXPERTS - 1:N_EXPERTS, :] * (1.0 / MOE_BLOCK), (1, meta_lanes))
    mrow = lax.broadcasted_iota(jnp.int32, (SUBLANE, meta_lanes), 0)
    meta = jnp.where(mrow == 0, blk_expert, jnp.where(mrow == 1, valid_end, jnp.where(mrow == 2, n_used, 0.0)))
    meta_ref[...] = meta.astype(jnp.int32)


def _moe_plan(e_t, n_tokens, n_blocks):
    meta_lanes = -(-n_blocks // LANE) * LANE
    dest, meta = pl.pallas_call(
        functools.partial(_moe_plan_body, n_tokens=n_tokens, meta_lanes=meta_lanes),
        grid=(1,),
        in_specs=[_full((SUBLANE, n_tokens))],
        out_specs=[_full((SUBLANE, n_tokens)), _full((SUBLANE, meta_lanes))],
        out_shape=[jax.ShapeDtypeStruct((SUBLANE, n_tokens), jnp.int32),
                   jax.ShapeDtypeStruct((SUBLANE, meta_lanes), jnp.int32)],
        scratch_shapes=[pltpu.VMEM((SUBLANE, n_tokens), F32)],
        compiler_params=_cparams(1),
        name="moe_plan",
    )(e_t)
    return dest[:TOP_K], meta[0, :n_blocks], meta[1, :n_blocks], meta[2, :1]


def _combine_body(x1_ref, y0_ref, y1_ref, y2_ref, y3_ref, gate_ref, g_ref, b_ref, o_ref, *, alpha):
    gate = gate_ref[...]
    lo, hi = _unpack_bf16_pairs(y0_ref[...])
    lo, hi = gate[:, 0:1] * lo, gate[:, 0:1] * hi
    for j, y_ref in enumerate((y1_ref, y2_ref, y3_ref), start=1):
        lo_j, hi_j = _unpack_bf16_pairs(y_ref[...])
        lo, hi = lo + gate[:, j:j + 1] * lo_j, hi + gate[:, j:j + 1] * hi_j
    ffn = jnp.concatenate([lo, hi], axis=1)
    o_ref[...] = _layer_norm(alpha * x1_ref[...] + ffn, g_ref[...], b_ref[...])


def _combine(x1, yg, gate, ln_g, ln_b, alpha, tm=256):
    t, dm = x1.shape
    n_tiles = t // tm
    expert_rows = lambda j: pl.BlockSpec((tm, dm // 2), lambda i: (i + j * n_tiles, 0))
    return pl.pallas_call(
        functools.partial(_combine_body, alpha=alpha),
        grid=(n_tiles,),
        in_specs=[pl.BlockSpec((tm, dm), lambda i: (i, 0))] + [expert_rows(j) for j in range(TOP_K)]
                 + [pl.BlockSpec((tm, LANE), lambda i: (i, 0)), _full((1, dm)), _full((1, dm))],
        out_specs=pl.BlockSpec((tm, dm), lambda i: (i, 0)),
        out_shape=jax.ShapeDtypeStruct((t, dm), F32),
        compiler_params=_cparams(1),
        name="combine_ln",
    )(x1, yg, yg, yg, yg, gate, ln_g.reshape(1, dm), ln_b.reshape(1, dm))


SC_CORES = 2
SC_SUBCORES = 16
SC_WORKERS = SC_CORES * SC_SUBCORES


def _sc_gather_rows(table, idx, window):
    n = idx.shape[0]
    dim = table.shape[1]
    n_steps = n // (SC_WORKERS * window)
    assert n_steps * window * SC_WORKERS == n and n_steps % 2 == 0 and window % SUBLANE == 0 and window <= LANE
    idx3 = idx.reshape(SC_WORKERS, n_steps, window)
    mesh = plsc.VectorSubcoreMesh(core_axis_name="c", subcore_axis_name="s",
                                  num_cores=SC_CORES, num_subcores=SC_SUBCORES)

    def body(table_hbm, idx_hbm, out_hbm, idx_v, rows_v, gsem, wsem):
        wid = lax.axis_index("s") * SC_CORES + lax.axis_index("c")
        pltpu.sync_copy(idx_hbm.at[wid], idx_v)

        def gather(j, buf):
            return pltpu.make_async_copy(table_hbm.at[idx_v.at[j]], rows_v.at[buf], gsem.at[buf])

        def write(j, buf):
            base = pl.multiple_of((wid * n_steps + j) * window, window)
            return pltpu.make_async_copy(rows_v.at[buf], out_hbm.at[pl.ds(base, window)], wsem.at[buf])

        gather(0, 0).start()

        @pl.loop(0, n_steps, step=2)
        def _(j0):
            for buf in range(2):
                j = j0 + buf
                gather(j, buf).wait()

                @pl.when(j >= 1)
                def _():
                    write(j - 1, 1 - buf).wait()

                @pl.when(j + 1 < n_steps)
                def _():
                    gather(j + 1, 1 - buf).start()

                write(j, buf).start()

        write(n_steps - 1, 1).wait()

    return pl.kernel(
        body, out_type=jax.ShapeDtypeStruct((n, dim), table.dtype), mesh=mesh,
        scratch_types=[pltpu.VMEM((n_steps, window), jnp.int32), pltpu.VMEM((2, window, dim), table.dtype),
                       pltpu.SemaphoreType.DMA((2,)), pltpu.SemaphoreType.DMA((2,))],
        name="sc_gather",
    )(table, idx3)


def _sc_scatter_rows(src, dest, n_out, window):
    t, dim = src.shape
    k = dest.shape[0]
    n_steps = t // (SC_WORKERS * window)
    assert n_steps * window * SC_WORKERS == t and n_steps % 2 == 0 and window % SUBLANE == 0 and window <= LANE
    idx3 = dest.reshape(k, SC_WORKERS, n_steps, window).transpose(1, 2, 0, 3).reshape(SC_WORKERS, n_steps * k, window)
    mesh = plsc.VectorSubcoreMesh(core_axis_name="c", subcore_axis_name="s",
                                  num_cores=SC_CORES, num_subcores=SC_SUBCORES)

    def body(src_hbm, idx_hbm, out_hbm, idx_v, rows_v, rsem, ssem):
        wid = lax.axis_index("s") * SC_CORES + lax.axis_index("c")
        pltpu.sync_copy(idx_hbm.at[wid], idx_v)

        def read(s, buf):
            base = pl.multiple_of((wid * n_steps + s) * window, window)
            return pltpu.make_async_copy(src_hbm.at[pl.ds(base, window)], rows_v.at[buf], rsem.at[buf])

        def scatter(s, j, buf):
            return pltpu.make_async_copy(rows_v.at[buf], out_hbm.at[idx_v.at[s * k + j]], ssem.at[buf])

        read(0, 0).start()

        @pl.loop(0, n_steps, step=2)
        def _(s0):
            for buf in range(2):
                s = s0 + buf
                read(s, buf).wait()

                @pl.when(s >= 1)
                def _():
                    for j in range(k):
                        scatter(s - 1, j, 1 - buf).wait()

                @pl.when(s + 1 < n_steps)
                def _():
                    read(s + 1, 1 - buf).start()

                for j in range(k):
                    scatter(s, j, buf).start()

        for j in range(k):
            scatter(n_steps - 1, j, 1).wait()

    return pl.kernel(
        body, out_type=jax.ShapeDtypeStruct((n_out, dim), src.dtype), mesh=mesh,
        scratch_types=[pltpu.VMEM((n_steps * k, window), jnp.int32), pltpu.VMEM((2, window, dim), src.dtype),
                       pltpu.SemaphoreType.DMA((2,)), pltpu.SemaphoreType.DMA((2,))],
        name="sc_scatter",
    )(src, idx3)


def _pad_cols(w, width):
    return jnp.pad(w, ((0, 0), (0, width - w.shape[1])))


def kernel(x, w_in, gmlp_ln_g, gmlp_ln_b, gmlp_ws, gmlp_bs, rwkv_mu, rwkv_w0, rwkv_w2, rwkv_a0, rwkv_a2, rwkv_g2, rwkv_k_k, rwkv_k_a, rwkv_r_k, rwkv_ln_g, rwkv_ln_b, mlstm_conv_w, mlstm_conv_b, mlstm_gate_b, mlstm_ln_g, w_out, ln1_g, ln1_b, router_w, router_b, exp_w1, exp_b1, exp_w2, exp_b2, ln2_g, ln2_b):
    batch, seq, dm = x.shape
    depth = w_in.shape[0]
    sb = batch // N_STREAMS if batch % N_STREAMS == 0 else batch
    t = sb * seq
    gw = gmlp_ln_g.shape[1]
    rw = rwkv_w0.shape[2]
    mw = mlstm_ln_g.shape[1]
    g_proj = 2 * gw
    r_proj = 3 * rw + W_LORA + A_LORA + G_LORA
    alpha = (2 * depth) ** 0.25
    n_blocks = -(-t * TOP_K // MOE_BLOCK) + N_EXPERTS
    streams = [x[i * sb:(i + 1) * sb].reshape(t, dm) for i in range(batch // sb)]
    for l in range(depth):
        mixed = []
        for xf in streams:
            y_g, pr, pm = _proj(xf, w_in, l, g_proj, r_proj, gmlp_ln_g[l], gmlp_ln_b[l], gmlp_ws[l], gmlp_bs[l])
            r, v, a, kd, b, lw, bonus, rgate = _rwkv_prep(
                pr, seq, rwkv_mu[l], rwkv_w0[l], rwkv_w2[l], rwkv_a0[l], rwkv_a2[l], rwkv_g2[l],
                rwkv_k_k[l], rwkv_k_a[l], rwkv_r_k[l].reshape(-1))
            ro = _rwkv_scan(r, v, a, kd, b, lw, sb, seq)
            q, k, gates = _mlstm_prep(pm, seq, mlstm_conv_w[l], mlstm_conv_b[l], mlstm_gate_b[l], mw)
            mh = _mlstm_scan(q, k, pm, gates, sb, seq)
            mixed.append(_mix_out(xf, y_g, ro, bonus, rgate, rwkv_ln_g[l], rwkv_ln_b[l], mh, pm, mlstm_ln_g[l],
                                  w_out, l, ln1_g[l], ln1_b[l], router_w[l], router_b[l], alpha))
        streams = []
        for x1, x1p, topi, gate in mixed:
            dest, block_e, valid_end, n_used = _moe_plan(topi, t, n_blocks)
            xs = _sc_scatter_rows(x1p, dest, n_blocks * MOE_BLOCK, window=64)
            ys = _moe_experts(xs, block_e, n_used, valid_end, exp_w1, exp_b1, exp_w2, exp_b2, l)
            yg = _sc_gather_rows(ys, dest.reshape(-1), window=64)
            streams.append(_combine(x1, yg, gate, ln2_g[l], ln2_b[l], alpha))
    return jnp.concatenate(streams, axis=0).reshape(batch, seq, dm)
```

```python
import functools
import math

import jax
import jax.numpy as jnp
from jax import lax
from jax.experimental import pallas as pl
from jax.experimental.pallas import tpu as pltpu
from jax.experimental.pallas import tpu_sc as plsc

F32 = jnp.float32
BF16 = jnp.bfloat16
HI = lax.Precision.HIGHEST

HEAD_DIM = 64
GMLP_CHUNK = 128
MLSTM_CHUNK = 128
RWKV_CHUNK = 64
W_LORA = 64
A_LORA = 64
G_LORA = 128
N_EXPERTS = 32
TOP_K = 4
MOE_BLOCK = 512
SWIGLU_LIMIT = 7.0
SWIGLU_ALPHA = 1.702
LN_EPS = 1e-5
RWKV_GN_EPS = 64e-5
LANE = 128
SUBLANE = 8
VMEM_LIMIT = 48 * 1024 * 1024
NEG_BIG = -1e30


def _cparams(n_axes):
    return pltpu.CompilerParams(dimension_semantics=("arbitrary",) * n_axes,
                                vmem_limit_bytes=VMEM_LIMIT)


def _full(shape):
    return pl.BlockSpec(shape, lambda *_: (0,) * len(shape))


def _dot(a, b, precision=None):
    return jnp.dot(a, b, preferred_element_type=F32, precision=precision)


def _dot_nt(a, b, precision=None):
    return lax.dot_general(a, b, (((1,), (1,)), ((), ())), preferred_element_type=F32, precision=precision)


def _dot_tn(a, b, precision=None):
    return lax.dot_general(a, b, (((0,), (0,)), ((), ())), preferred_element_type=F32, precision=precision)


def _split(x):
    hi = x.astype(BF16)
    return hi, (x - hi.astype(F32)).astype(BF16)


def _split3(x):
    hi = x.astype(BF16)
    r1 = x - hi.astype(F32)
    mid = r1.astype(BF16)
    return hi, mid, (r1 - mid.astype(F32)).astype(BF16)


def _mm(a, b, mode, dot=_dot):
    if mode == "hi":
        return dot(a, b, HI)
    if mode == "b1":
        return dot(a.astype(BF16), b.astype(BF16))
    bh, bl = _split(b)
    if mode == "b2":
        ah = a.astype(BF16)
        return dot(ah, bh) + dot(ah, bl)
    ah, al = _split(a)
    return dot(ah, bh) + (dot(ah, bl) + dot(al, bh))


def _dot_exact_lhs(a_bf16, x):
    hi, mid, lo = _split3(x)
    return _dot(a_bf16, hi) + (_dot(a_bf16, mid) + _dot(a_bf16, lo))


def _dot_exact_rhs(x, b_bf16, terms=3):
    if terms == 2:
        hi, lo = _split(x)
        return _dot(hi, b_bf16) + _dot(lo, b_bf16)
    hi, mid, lo = _split3(x)
    return _dot(hi, b_bf16) + (_dot(mid, b_bf16) + _dot(lo, b_bf16))


def _pack_bf16_pairs(x):
    n = x.shape[1] // 2
    lo = pltpu.bitcast(x[:, :n].astype(BF16).astype(F32), jnp.uint32)
    hi = pltpu.bitcast(x[:, n:].astype(BF16).astype(F32), jnp.uint32)
    return hi | (lo >> 16)


def _unpack_bf16_pairs(w):
    lo = pltpu.bitcast(w << 16, F32)
    hi = pltpu.bitcast(w & jnp.uint32(0xFFFF0000), F32)
    return lo, hi


def _sigmoid(x):
    return 1.0 / (1.0 + jnp.exp(-x))


def _softplus(x):
    return jnp.maximum(x, 0.0) + jnp.log1p(jnp.exp(-jnp.abs(x)))


def _block_diag_ones(width):
    h = jnp.arange(width) // HEAD_DIM
    return (h[:, None] == h[None, :]).astype(F32)


CAST_ROWS = 128


def _cast_rows(src_ref, dst_ref):
    n_src, n_dst = src_ref.shape[1], dst_ref.shape[1]
    whole = n_src // LANE * LANE

    def step(r, carry):
        rows = pl.ds(pl.multiple_of(r * CAST_ROWS, CAST_ROWS), CAST_ROWS)
        dst_ref[rows, :whole] = src_ref[rows, :whole].astype(BF16)
        if n_dst > whole:
            tail = [src_ref[rows, whole:]] if n_src > whole else []
            tail.append(jnp.zeros((CAST_ROWS, n_dst - n_src), F32))
            dst_ref[rows, whole:] = jnp.concatenate(tail, axis=1).astype(BF16)
        return carry
    lax.fori_loop(0, src_ref.shape[0] // CAST_ROWS, step, 0)


def _gmlp_gate(p, lng_ref, lnb_ref, ws_ref, bst_ref, o_ref):
    gw = p.shape[1] // 2
    p = 0.5 * p * (1.0 + lax.erf(p * math.sqrt(0.5)))
    u, v = p[:, :gw], p[:, gw:]
    mu = jnp.mean(v, axis=-1, keepdims=True)
    vc = v - mu
    var = jnp.mean(vc * vc, axis=-1, keepdims=True)
    vn = vc * lax.rsqrt(var + LN_EPS) * lng_ref[...] + lnb_ref[...]
    for c in range(p.shape[0] // GMLP_CHUNK):
        rows = slice(c * GMLP_CHUNK, (c + 1) * GMLP_CHUNK)
        ys = []
        for h in range(gw // HEAD_DIM):
            cols = slice(h * HEAD_DIM, (h + 1) * HEAD_DIM)
            ys.append(_dot(ws_ref[h], vn[rows, cols].astype(BF16)) + bst_ref[:, h:h + 1])
        o_ref[rows, :] = u[rows, :] * jnp.concatenate(ys, axis=1)


def _proj_body(x_ref, w_ref, lng_ref, lnb_ref, ws_ref, bst_ref, yg_ref, pr_ref, pm_ref, wb_ref, *, ng, nr):
    @pl.when(pl.program_id(0) == 0)
    def _():
        _cast_rows(w_ref, wb_ref)

    xb = x_ref[...].astype(BF16)
    _gmlp_gate(_dot(xb, wb_ref[:, :ng]), lng_ref, lnb_ref, ws_ref, bst_ref, yg_ref)
    pr_ref[...] = _dot(xb, wb_ref[:, ng:ng + nr])
    pm_ref[...] = _dot(xb, wb_ref[:, ng + nr:])


def _proj(x, w_in, layer, ng, nr, ln_g, ln_b, ws, bs, tm=512):
    t, d = x.shape
    p_in = w_in.shape[2]
    p_pad = -(-p_in // LANE) * LANE
    nm = p_pad - ng - nr
    gw = ng // 2
    n_heads = gw // HEAD_DIM
    bst = jnp.zeros((GMLP_CHUNK, LANE), F32).at[:, :n_heads].set(bs.T)
    row = lambda n: pl.BlockSpec((tm, n), lambda i: (i, 0))
    return pl.pallas_call(
        functools.partial(_proj_body, ng=ng, nr=nr),
        grid=(t // tm,),
        in_specs=[row(d), pl.BlockSpec((None, d, p_in), lambda i: (layer, 0, 0), pipeline_mode=pl.Buffered(1)),
                  _full((1, gw)), _full((1, gw)), _full((n_heads, GMLP_CHUNK, GMLP_CHUNK)),
                  _full((GMLP_CHUNK, LANE))],
        out_specs=[row(gw), row(nr), row(nm)],
        out_shape=[jax.ShapeDtypeStruct((t, n), F32) for n in (gw, nr, nm)],
        scratch_shapes=[pltpu.VMEM((d, p_pad), BF16)],
        compiler_params=_cparams(1),
        name="in_proj",
    )(x, w_in, ln_g.reshape(1, gw), ln_b.reshape(1, gw), ws.astype(BF16), bst)


def _halo_specs(tm, width, n_rows):
    per8 = tm // SUBLANE
    last = n_rows // SUBLANE - 1
    prev = pl.BlockSpec((SUBLANE, width), lambda i: (jnp.maximum(i * per8 - 1, 0), 0))
    nxt = pl.BlockSpec((SUBLANE, width), lambda i: (jnp.minimum((i + 1) * per8, last), 0))
    return prev, nxt


def _neighbours(cur, prev_blk, next_blk, tiles_per_seq):
    tm = cur.shape[0]
    j = pl.program_id(0) % tiles_per_seq
    prev_row = jnp.where(j > 0, prev_blk[SUBLANE - 1:SUBLANE, :], 0.0)
    next_row = jnp.where(j < tiles_per_seq - 1, next_blk[0:1, :], 0.0)
    ridx = lax.broadcasted_iota(jnp.int32, cur.shape, 0)
    before = jnp.where(ridx == 0, prev_row, pltpu.roll(cur, 1, 0))
    after = jnp.where(ridx == tm - 1, next_row, pltpu.roll(cur, tm - 1, 0))
    return before, after


def _rwkv_prep_body(pr_ref, prev_ref, next_ref, mu_ref, w0_ref, w2_ref, a0_ref, a2_ref, g2_ref,
                    kk_ref, ka_ref, rk_ref, bd_ref,
                    r_out, v_out, a_out, kd_out, b_out, lw_out, bonus_out, gate_out, *, rw, tiles_per_seq):
    pf = pr_ref[...]
    before, after = _neighbours(pf, prev_ref[...], next_ref[...], tiles_per_seq)
    pf = pf + mu_ref[0:1, :] * (before - pf) + mu_ref[1:2, :] * (after - pf)
    o3 = 3 * rw
    r, k, v = pf[:, :rw], pf[:, rw:2 * rw], pf[:, 2 * rw:o3]
    wd = pf[:, o3:o3 + W_LORA]
    ad = pf[:, o3 + W_LORA:o3 + W_LORA + A_LORA]
    gd = pf[:, o3 + W_LORA + A_LORA:]
    bd = bd_ref[...]
    kk = k * kk_ref[...]
    ss = _dot_exact_rhs(kk * kk, bd, terms=2)
    kk = kk / jnp.maximum(jnp.sqrt(ss), 1e-12)
    twd = jnp.tanh(wd)
    ksum = jnp.zeros_like(k)
    for d in range(2):
        w_log = -_softplus(-(w0_ref[d:d + 1, :] + _mm(twd, w2_ref[d], "b3"))) - 0.5
        lw_out[d] = -jnp.exp(w_log)
        iclr = _sigmoid(a0_ref[d:d + 1, :] + _mm(ad, a2_ref[d], "b3"))
        kd = k * (1.0 + (iclr - 1.0) * ka_ref[...])
        kd_out[d] = kd
        b_out[d] = kk * iclr
        ksum = ksum + kd
    r_out[...] = r
    v_out[...] = v
    a_out[...] = -kk
    bonus_out[...] = _dot_exact_rhs(r * ksum * rk_ref[...], bd, terms=2) * v
    gate_out[...] = _dot(_sigmoid(gd).astype(BF16), g2_ref[...])


def _rwkv_prep(pr, seq, mu, w0, w2, a0, a2, g2, k_k, k_a, r_k, tm=512):
    t, rproj = pr.shape
    rw = w0.shape[1]
    tiles_per_seq = seq // tm
    prev, nxt = _halo_specs(tm, rproj, t)
    row = pl.BlockSpec((tm, rw), lambda i: (i, 0))
    row2 = pl.BlockSpec((2, tm, rw), lambda i: (0, i, 0))
    one = jax.ShapeDtypeStruct((t, rw), F32)
    two = jax.ShapeDtypeStruct((2, t, rw), F32)
    return pl.pallas_call(
        functools.partial(_rwkv_prep_body, rw=rw, tiles_per_seq=tiles_per_seq),
        grid=(t // tm,),
        in_specs=[pl.BlockSpec((tm, rproj), lambda i: (i, 0)), prev, nxt,
                  _full((2, rproj)), _full((2, rw)), _full((2, W_LORA, rw)), _full((2, rw)),
                  _full((2, A_LORA, rw)), _full((G_LORA, rw)), _full((1, rw)), _full((1, rw)),
                  _full((1, rw)), _full((rw, rw))],
        out_specs=[row, row, row, row2, row2, row2, row, row],
        out_shape=[one, one, one, two, two, two, one, one],
        compiler_params=_cparams(1),
        name="rwkv_prep",
    )(pr, pr, pr, mu, w0, w2, a0, a2, g2.astype(BF16), k_k.reshape(1, rw), k_a.reshape(1, rw),
      r_k.reshape(1, rw), _block_diag_ones(rw).astype(BF16))


P_G, P_INV, P_APPLY, P_STATE, P_SEQ = "b1", "b1", "b1", "b1", "b2"


def _rwkv_intra_body(r_ref, v_ref, a_ref, kd_ref, b_ref, lw_ref, rq_out, o0_out, mtx_out, hc_out,
                     *, n_heads, chunks):
    L = RWKV_CHUNK
    d = pl.program_id(0)
    row = lax.broadcasted_iota(jnp.int32, (L, L), 0)
    col = lax.broadcasted_iota(jnp.int32, (L, L), 1)
    fwd = d == 0
    rel = (col - row) * (1 - 2 * d)
    incl = rel <= 0
    strict = rel < 0
    eye = (row == col).astype(F32)
    tri = incl.astype(BF16)
    pairs = []
    for c in range(chunks):
        rows = slice(c * L, (c + 1) * L)
        lw = lw_ref[rows, :]
        cum = _dot_exact_lhs(tri, lw)
        tot = jnp.where(fwd, cum[L - 1:L, :], cum[0:1, :])
        e_neg = jnp.exp(-cum)
        e_end = jnp.exp(tot - cum)
        e_tot = jnp.exp(tot)
        r, v, a, kd, b = r_ref[rows, :], v_ref[rows, :], a_ref[rows, :], kd_ref[rows, :], b_ref[rows, :]
        at, rt, bt, kt = a * jnp.exp(cum - lw), r * jnp.exp(cum), b * e_neg, kd * e_neg
        kend, bend = kd * e_end, b * e_end
        for h in range(n_heads):
            sl = slice(h * HEAD_DIM, (h + 1) * HEAD_DIM)
            pairs.append(dict(at=at[:, sl], rt=rt[:, sl], bt=bt[:, sl], kt=kt[:, sl], v=v[:, sl],
                              kend=kend[:, sl], bend=bend[:, sl], e_tot=e_tot[:, sl]))
    for p in pairs:
        p["g"] = _mm(jnp.concatenate([p["at"], p["rt"]], axis=0),
                     jnp.concatenate([p["bt"], p["kt"]], axis=0), P_G, _dot_nt)
    row2 = lax.broadcasted_iota(jnp.int32, (L, 2 * L), 0)
    col2 = lax.broadcasted_iota(jnp.int32, (L, 2 * L), 1) & (L - 1)
    rel2 = (col2 - row2) * (1 - 2 * d)
    incl2 = rel2 <= 0
    strict2 = rel2 < 0
    zeros = jnp.zeros((L, HEAD_DIM), F32)
    for p in pairs:
        g = p.pop("g")
        a_both = jnp.where(strict2, g[:L, :], 0.0)
        p["m_both"] = jnp.where(incl2, g[L:, :], 0.0)
        p["pw"] = a_both[:, :L]
        p["a_ak"] = a_both[:, L:]
        p["inv"] = eye + p["pw"]
    for _ in range(int(math.log2(L)) - 1):
        for p in pairs:
            p["pw"] = _mm(p["pw"], p["pw"], P_INV)
        for p in pairs:
            p["inv"] = p["inv"] + _mm(p["inv"], p["pw"], P_INV)
    for p in pairs:
        p["akv"] = _mm(p["a_ak"], p["v"], P_APPLY)
    for p in pairs:
        wu = _mm(p["inv"], jnp.concatenate([p["at"], p["akv"]], axis=1), P_APPLY)
        p["rhs"] = jnp.concatenate([wu, jnp.concatenate([zeros, p["v"]], axis=1)], axis=0)
    for p in pairs:
        p["rq_o0"] = _mm(p["m_both"], p["rhs"], P_APPLY)
    for p in pairs:
        p["m_hc"] = _mm(jnp.concatenate([p["bend"], p["kend"]], axis=0), p["rhs"], P_STATE, _dot_tn)
    for c in range(chunks):
        ps = pairs[c * n_heads:(c + 1) * n_heads]
        rows = slice(c * L, (c + 1) * L)
        krows = slice(c * HEAD_DIM, (c + 1) * HEAD_DIM)
        rq_out[rows, :] = jnp.concatenate([p["rt"] + p["rq_o0"][:, :HEAD_DIM] for p in ps], axis=1)
        o0_out[rows, :] = jnp.concatenate([p["rq_o0"][:, HEAD_DIM:] for p in ps], axis=1)
        mtx_out[krows, :] = jnp.concatenate([eye * p["e_tot"] + p["m_hc"][:, :HEAD_DIM] for p in ps], axis=1)
        hc_out[krows, :] = jnp.concatenate([p["m_hc"][:, HEAD_DIM:] for p in ps], axis=1)


def _rwkv_intra(r, v, a, kd, b, lw, chunks=8):
    t, rw = r.shape
    n_heads = rw // HEAD_DIM
    tm = chunks * RWKV_CHUNK
    tk = chunks * HEAD_DIM
    n_tiles = t // tm
    one = pl.BlockSpec((tm, rw), lambda d, i: (i, 0))
    two = pl.BlockSpec((None, tm, rw), lambda d, i: (d, i, 0))
    twok = pl.BlockSpec((None, tk, rw), lambda d, i: (d, i, 0))
    return pl.pallas_call(
        functools.partial(_rwkv_intra_body, n_heads=n_heads, chunks=chunks),
        grid=(2, n_tiles),
        in_specs=[one, one, one, two, two, two],
        out_specs=[two, two, twok, twok],
        out_shape=[jax.ShapeDtypeStruct((2, t, rw), F32), jax.ShapeDtypeStruct((2, t, rw), F32),
                   jax.ShapeDtypeStruct((2, n_tiles * tk, rw), F32),
                   jax.ShapeDtypeStruct((2, n_tiles * tk, rw), F32)],
        compiler_params=_cparams(2),
        name="rwkv_intra",
    )(r, v, a, kd, b, lw)


def _rwkv_seq_body(rq0, o00, mtx0, hc0, rq1, o01, mtx1, hc1, out0, out1, h_ref, *, n_heads, batch):
    c = pl.program_id(0)

    @pl.when(c == 0)
    def _():
        h_ref[...] = jnp.zeros_like(h_ref)

    L = RWKV_CHUNK
    for d, (rq, o0, mtx, hc, out) in enumerate(((rq0, o00, mtx0, hc0, out0), (rq1, o01, mtx1, hc1, out1))):
        for bi in range(batch):
            rq_t, mtx_t = rq[bi], mtx[bi]
            state = h_ref[d, bi]
            outs, states = [], []
            for h in range(n_heads):
                sl = slice(h * HEAD_DIM, (h + 1) * HEAD_DIM)
                prod = _mm(jnp.concatenate([rq_t[:, sl], mtx_t[:, sl]], axis=0), state[:, sl], P_SEQ)
                outs.append(prod[:L])
                states.append(prod[L:])
            out[bi] = jnp.concatenate(outs, axis=1) + o0[bi]
            h_ref[d, bi] = jnp.concatenate(states, axis=1) + hc[bi]


def _rwkv_seq(rq, o0, mtx, hc, batch, seq):
    _, t, rw = rq.shape
    n_heads = rw // HEAD_DIM
    L = RWKV_CHUNK
    nc = seq // L
    as4 = lambda x: x.reshape(2, batch, x.shape[1] // batch, rw)
    rq, o0, mtx, hc = as4(rq), as4(o0), as4(mtx), as4(hc)
    fwd = lambda rows: pl.BlockSpec((None, batch, rows, rw), lambda c: (0, 0, c, 0))
    bwd = lambda rows: pl.BlockSpec((None, batch, rows, rw), lambda c: (1, 0, nc - 1 - c, 0))
    out0, out1 = pl.pallas_call(
        functools.partial(_rwkv_seq_body, n_heads=n_heads, batch=batch),
        grid=(nc,),
        in_specs=[fwd(L), fwd(L), fwd(HEAD_DIM), fwd(HEAD_DIM), bwd(L), bwd(L), bwd(HEAD_DIM), bwd(HEAD_DIM)],
        out_specs=[pl.BlockSpec((batch, L, rw), lambda c: (0, c, 0)),
                   pl.BlockSpec((batch, L, rw), lambda c: (0, nc - 1 - c, 0))],
        out_shape=[jax.ShapeDtypeStruct((batch, seq, rw), F32)] * 2,
        scratch_shapes=[pltpu.VMEM((2, batch, HEAD_DIM, rw), F32)],
        compiler_params=_cparams(1),
        name="rwkv_seq",
    )(rq, o0, mtx, hc, rq, o0, mtx, hc)
    return out0.reshape(t, rw), out1.reshape(t, rw)


def _rwkv_scan(r, v, a, kd, b, lw, batch, seq):
    rq, o0, mtx, hc = _rwkv_intra(r, v, a, kd, b, lw)
    return _rwkv_seq(rq, o0, mtx, hc, batch, seq)


def _mlstm_prep_body(qk_ref, prev_ref, next_ref, g_ref, cw_ref, cb_ref, gb_ref, q_out, k_out, gate_out,
                     *, mw, n_heads, tiles_per_seq):
    x = qk_ref[...]
    before, after = _neighbours(x, prev_ref[...], next_ref[...], tiles_per_seq)
    y = cb_ref[...] + before * cw_ref[0:1, :] + x * cw_ref[1:2, :] + after * cw_ref[2:3, :]
    y = y * _sigmoid(y)
    q_out[...] = y[:, :mw]
    k_out[...] = y[:, mw:] * (HEAD_DIM ** -0.5)
    g = g_ref[...] + gb_ref[...]
    lane = lax.broadcasted_iota(jnp.int32, g.shape, 1)
    for d in range(2):
        ig = g if d == 0 else pltpu.roll(g, LANE - n_heads, 1)
        fg = pltpu.roll(g, LANE - (1 + d) * n_heads, 1)
        lf = -_softplus(-fg)
        gate_out[d] = jnp.where(lane < n_heads, ig, jnp.where(lane < 2 * n_heads, lf, 0.0))


def _mlstm_prep(pm, seq, conv_w, conv_b, gate_b, mw, tm=512):
    t = pm.shape[0]
    n_heads = mw // HEAD_DIM
    tiles_per_seq = seq // tm
    w2 = 2 * mw
    prev, nxt = _halo_specs(tm, w2, t)
    gcol = (4 * mw) // LANE
    gb = jnp.zeros((1, LANE), F32).at[0, :4 * n_heads].set(gate_b)
    row = pl.BlockSpec((tm, mw), lambda i: (i, 0))
    return pl.pallas_call(
        functools.partial(_mlstm_prep_body, mw=mw, n_heads=n_heads, tiles_per_seq=tiles_per_seq),
        grid=(t // tm,),
        in_specs=[pl.BlockSpec((tm, w2), lambda i: (i, 0)), prev, nxt,
                  pl.BlockSpec((tm, LANE), lambda i: (i, gcol)),
                  _full((3, w2)), _full((1, w2)), _full((1, LANE))],
        out_specs=[row, row, pl.BlockSpec((2, tm, LANE), lambda i: (0, i, 0))],
        out_shape=[jax.ShapeDtypeStruct((t, mw), F32), jax.ShapeDtypeStruct((t, mw), F32),
                   jax.ShapeDtypeStruct((2, t, LANE), F32)],
        compiler_params=_cparams(1),
        name="mlstm_prep",
    )(pm, pm, pm, pm, conv_w, conv_b.reshape(1, w2), gb)


def _mlstm_scan_body(q0_ref, k0_ref, v0_ref, g0_ref, q1_ref, k1_ref, v1_ref, g1_ref, o0_ref, o1_ref,
                     c_ref, m_ref, *, n_heads, group):
    L = MLSTM_CHUNK
    H = n_heads

    @pl.when(pl.program_id(1) == 0)
    def _():
        c_ref[...] = jnp.zeros_like(c_ref)
        m_ref[...] = jnp.zeros_like(m_ref)

    row = lax.broadcasted_iota(jnp.int32, (L, L), 0)
    col = lax.broadcasted_iota(jnp.int32, (L, L), 1)
    trow = lax.broadcasted_iota(jnp.int32, (L, LANE), 0)
    low = lax.broadcasted_iota(jnp.int32, (L, LANE), 1) < HEAD_DIM
    xr = lax.broadcasted_iota(jnp.int32, (LANE, H * L), 0)
    xc = lax.broadcasted_iota(jnp.int32, (LANE, H * L), 1)
    spread = (xr - H == lax.shift_right_logical(xc, int(math.log2(L)))).astype(BF16)
    hs = []
    dirs = ((q0_ref, k0_ref, v0_ref, g0_ref), (q1_ref, k1_ref, v1_ref, g1_ref))
    for bi, d in [(bi, d) for bi in range(group) for d in range(2)]:
        q_ref, k_ref, v_ref, g_ref = dirs[d]
        incl = (col <= row) if d == 0 else (col >= row)
        last = L - 1 if d == 0 else 0
        g = g_ref[bi]
        bcum = _dot_exact_lhs(incl.astype(BF16), g)
        z = pltpu.roll(g, H, 1) - bcum
        cmax = z
        shift = 1
        while shift < L:
            if d == 0:
                moved = jnp.where(trow >= shift, pltpu.roll(cmax, shift, 0), -jnp.inf)
            else:
                moved = jnp.where(trow < L - shift, pltpu.roll(cmax, L - shift, 0), -jnp.inf)
            cmax = jnp.maximum(cmax, moved)
            shift *= 2
        m_prev = m_ref[bi, d, 0:1, :]
        top = jnp.maximum(cmax, m_prev)
        b_last = bcum[last:last + 1, :]
        lwc = b_last + z
        m_new = jnp.maximum(b_last + m_prev, jnp.max(lwc, axis=0, keepdims=True))
        m_ref[bi, d, 0:1, :] = m_new
        alpha_w = _dot_exact_rhs(-top, spread)
        floor_w = jnp.exp(-_dot_exact_rhs(bcum + top, spread, terms=2))
        wts_w = _dot(jnp.exp(lwc - m_new).astype(BF16), spread)
        rows_w = _dot_exact_rhs(jnp.concatenate(
            [jnp.broadcast_to(m_prev, (SUBLANE, LANE)),
             jnp.broadcast_to(jnp.exp(b_last + m_prev - m_new), (SUBLANE, LANE))], axis=0), spread)
        z_t = z.T
        q, k, v = q_ref[bi], k_ref[bi], v_ref[bi]
        for h in range(H):
            slab = slice(h // 2 * LANE, (h // 2 + 1) * LANE)
            cols = slice(h * L, (h + 1) * L)
            mine = low if h % 2 == 0 else jnp.logical_not(low)
            kh = jnp.where(mine, k[:, slab], 0.0)
            hs.append(dict(
                qh=jnp.where(mine, q[:, slab], 0.0).astype(BF16), kh=kh.astype(BF16),
                vext=jnp.where(mine, v[:, slab], 1.0).astype(BF16),
                decay=jnp.exp(jnp.where(incl, alpha_w[:, cols] + z_t[H + h:H + h + 1, :], -jnp.inf)),
                w_inter=jnp.exp(alpha_w[:, cols] + rows_w[0:1, cols]), floor=floor_w[:, cols],
                wk=(wts_w[:, cols] * kh).astype(BF16), dec=rows_w[SUBLANE:SUBLANE + 1, cols],
                cst=c_ref[bi, d, h]))
    for p in hs:
        p["sc"] = (_dot_nt(p["qh"], p["kh"]) * p["decay"]).astype(BF16)
    for p in hs:
        p["numext"] = _dot(p["sc"], p["vext"]) + p["w_inter"] * _dot(p["qh"], p["cst"].astype(BF16))
    for p in hs:
        p["upd"] = _dot_tn(p["wk"], p["vext"])
    for bi, d in [(bi, d) for bi in range(group) for d in range(2)]:
        o_ref = (o0_ref, o1_ref)[d]
        res = []
        for h in range(H):
            p = hs[(bi * 2 + d) * H + h]
            den = pltpu.roll(p["numext"], HEAD_DIM, 1)
            res.append(p["numext"] / jnp.maximum(jnp.abs(den), p["floor"]))
            c_ref[bi, d, h] = p["dec"] * p["cst"] + p["upd"]
        for pair in range(H // 2):
            o_ref[bi, :, pair * LANE:(pair + 1) * LANE] = jnp.where(low, res[2 * pair], res[2 * pair + 1])


def _mlstm_scan(q, k, pm, gates, batch, seq):
    t, mw = q.shape
    n_heads = mw // HEAD_DIM
    L = MLSTM_CHUNK
    nc = seq // L
    group = math.gcd(batch, 4)
    q3, k3, pm3 = (x.reshape(batch, seq, x.shape[1]) for x in (q, k, pm))
    g4 = gates.reshape(2, batch, seq, LANE)
    specs = []
    for d, blk in enumerate((lambda c: c, lambda c: nc - 1 - c)):
        specs += [pl.BlockSpec((group, L, mw), lambda gi, c, blk=blk: (gi, blk(c), 0)),
                  pl.BlockSpec((group, L, mw), lambda gi, c, blk=blk: (gi, blk(c), 0)),
                  pl.BlockSpec((group, L, mw), lambda gi, c, blk=blk: (gi, blk(c), 2)),
                  pl.BlockSpec((None, group, L, LANE), lambda gi, c, blk=blk, d=d: (d, gi, blk(c), 0))]
    out0, out1 = pl.pallas_call(
        functools.partial(_mlstm_scan_body, n_heads=n_heads, group=group),
        grid=(batch // group, nc),
        in_specs=specs,
        out_specs=[pl.BlockSpec((group, L, mw), lambda gi, c: (gi, c, 0)),
                   pl.BlockSpec((group, L, mw), lambda gi, c: (gi, nc - 1 - c, 0))],
        out_shape=[jax.ShapeDtypeStruct((batch, seq, mw), F32)] * 2,
        scratch_shapes=[pltpu.VMEM((group, 2, n_heads, LANE, LANE), F32),
                        pltpu.VMEM((group, 2, SUBLANE, LANE), F32)],
        compiler_params=_cparams(2),
        name="mlstm_scan",
    )(q3, k3, pm3, g4, q3, k3, pm3, g4)
    return out0.reshape(t, mw), out1.reshape(t, mw)


def _layer_norm(x, g, b):
    mu = jnp.mean(x, axis=-1, keepdims=True)
    xc = x - mu
    var = jnp.mean(xc * xc, axis=-1, keepdims=True)
    return xc * lax.rsqrt(var + LN_EPS) * g + b


def _head_norm(x, bd_mean, eps):
    mu = _dot_exact_rhs(x, bd_mean, terms=2)
    xc = x - mu
    var = _dot_exact_rhs(xc * xc, bd_mean, terms=2)
    return xc * lax.rsqrt(var + eps)


def _mix_out_body(x_ref, yg_ref, ro0_ref, ro1_ref, bonus_ref, rgate_ref, rlg_ref, rlb_ref, mh0_ref, mh1_ref, og_ref,
                  mlg_ref, w_ref, l1g_ref, l1b_ref, rw_ref, rb_ref, bdm_ref,
                  x1_out, x1p_out, topi_out, gate_out, wb_ref, *, alpha, gw, rw):
    @pl.when(pl.program_id(0) == 0)
    def _():
        _cast_rows(w_ref, wb_ref)

    bdm = bdm_ref[...]
    yr = _head_norm(ro0_ref[...] + ro1_ref[...], bdm, RWKV_GN_EPS) * rlg_ref[...] + rlb_ref[...]
    yr = (yr + bonus_ref[...]) * rgate_ref[...]
    ym = _sigmoid(og_ref[...]) * (_head_norm(mh0_ref[...] + mh1_ref[...], bdm, LN_EPS) * mlg_ref[...])
    mix = (_dot(yg_ref[...].astype(BF16), wb_ref[:gw, :]) + _dot(yr.astype(BF16), wb_ref[gw:gw + rw, :])
           + _dot(ym.astype(BF16), wb_ref[gw + rw:, :]))
    x1 = _layer_norm(alpha * x_ref[...] + mix, l1g_ref[...], l1b_ref[...])
    x1_out[...] = x1
    x1p_out[...] = _pack_bf16_pairs(x1)
    lg = _mm(x1, rw_ref[...], "b3") + rb_ref[...]
    lane = lax.broadcasted_iota(jnp.int32, lg.shape, 1)
    vals, topi = [], jnp.zeros(lg.shape, jnp.int32)
    for j in range(TOP_K):
        mx = jnp.max(lg, axis=1, keepdims=True)
        idx = jnp.min(jnp.where(lg == mx, lane, LANE), axis=1, keepdims=True)
        vals.append(mx)
        topi = jnp.where(lane == j, idx, topi)
        lg = jnp.where(lane == idx, -jnp.inf, lg)
    es = [jnp.exp(vj - vals[0]) for vj in vals]
    den = es[0] + es[1] + es[2] + es[3]
    gate = jnp.zeros(lg.shape, F32)
    for j in range(TOP_K):
        gate = jnp.where(lane == j, es[j] / den, gate)
    topi_out[...] = topi.T[:SUBLANE, :]
    gate_out[...] = gate


def _mix_out(x, yg, ro, bonus, rgate, rlg, rlb, mh, pm, mlg, w_out, layer, l1g, l1b, router_w, router_b, alpha,
             tm=512):
    t, dm = x.shape
    gw, rw, mw = yg.shape[1], bonus.shape[1], mh[0].shape[1]
    assert rw == mw
    rwp = jnp.zeros((dm, LANE), F32).at[:, :N_EXPERTS].set(router_w)
    rbp = jnp.full((1, LANE), NEG_BIG, F32).at[0, :N_EXPERTS].set(router_b)
    row = lambda n: pl.BlockSpec((tm, n), lambda i: (i, 0))
    vec = lambda n: _full((1, n))
    return pl.pallas_call(
        functools.partial(_mix_out_body, alpha=alpha, gw=gw, rw=rw),
        grid=(t // tm,),
        in_specs=[row(dm), row(gw), row(rw), row(rw), row(rw), row(rw), vec(rw), vec(rw), row(mw), row(mw),
                  pl.BlockSpec((tm, mw), lambda i: (i, 3)),
                  vec(mw),
                  pl.BlockSpec((None, dm, dm), lambda i: (layer, 0, 0), pipeline_mode=pl.Buffered(1)),
                  vec(dm), vec(dm), _full((dm, LANE)), vec(LANE), _full((rw, rw))],
        out_specs=[row(dm), row(dm // 2), pl.BlockSpec((SUBLANE, tm), lambda i: (0, i)), row(LANE)],
        out_shape=[jax.ShapeDtypeStruct((t, dm), F32), jax.ShapeDtypeStruct((t, dm // 2), jnp.uint32),
                   jax.ShapeDtypeStruct((SUBLANE, t), jnp.int32), jax.ShapeDtypeStruct((t, LANE), F32)],
        scratch_shapes=[pltpu.VMEM((dm, dm), BF16)],
        compiler_params=_cparams(1),
        name="mix_out",
    )(x, yg, ro[0], ro[1], bonus, rgate, rlg.reshape(1, rw), rlb.reshape(1, rw), mh[0], mh[1], pm,
      mlg.reshape(1, mw), w_out, l1g.reshape(1, dm), l1b.reshape(1, dm), rwp, rbp,
      (_block_diag_ones(rw) / HEAD_DIM).astype(BF16))


def _moe_body(be_ref, nu_ref, ve_ref, xs_ref, w1_ref, b1_ref, w2_ref, b2_ref, o_ref, *, dff):
    i = pl.program_id(0)
    active = i < nu_ref[0]

    @pl.when(active)
    def _():
        rowid = i * MOE_BLOCK + lax.broadcasted_iota(jnp.int32, (MOE_BLOCK, 1), 0)
        lo, hi = _unpack_bf16_pairs(jnp.where(rowid < ve_ref[i], xs_ref[...], jnp.uint32(0)))
        xs = jnp.concatenate([lo.astype(BF16), hi.astype(BF16)], axis=1)
        hdn = _dot(xs, w1_ref[...].astype(BF16)) + b1_ref[...]
        g_ = jnp.minimum(hdn[:, :dff], SWIGLU_LIMIT)
        u_ = jnp.clip(hdn[:, dff:], -SWIGLU_LIMIT, SWIGLU_LIMIT)
        act = (u_ + 1.0) * (g_ * _sigmoid(g_ * SWIGLU_ALPHA))
        o_ref[...] = _pack_bf16_pairs(_dot(act.astype(BF16), w2_ref[...].astype(BF16)) + b2_ref[...])

    @pl.when(jnp.logical_not(active))
    def _():
        o_ref[...] = jnp.zeros_like(o_ref)


def _moe_experts(xs, block_e, n_used, valid_end, w1, b1, w2, b2, layer):
    rows, half = xs.shape
    nb = rows // MOE_BLOCK
    depth, ne, dm, dff2 = w1.shape
    dff = dff2 // 2
    grid_spec = pltpu.PrefetchScalarGridSpec(
        num_scalar_prefetch=3,
        grid=(nb,),
        in_specs=[pl.BlockSpec((MOE_BLOCK, half), lambda i, be, nu, ve: (i, 0)),
                  pl.BlockSpec((None, None, dm, dff2), lambda i, be, nu, ve: (layer, be[i], 0, 0)),
                  pl.BlockSpec((None, None, 1, dff2), lambda i, be, nu, ve: (layer, be[i], 0, 0)),
                  pl.BlockSpec((None, None, dff, dm), lambda i, be, nu, ve: (layer, be[i], 0, 0)),
                  pl.BlockSpec((None, None, 1, dm), lambda i, be, nu, ve: (layer, be[i], 0, 0))],
        out_specs=pl.BlockSpec((MOE_BLOCK, half), lambda i, be, nu, ve: (i, 0)),
    )
    return pl.pallas_call(
        functools.partial(_moe_body, dff=dff),
        grid_spec=grid_spec,
        out_shape=jax.ShapeDtypeStruct((rows, half), jnp.uint32),
        compiler_params=_cparams(1),
        name="moe_experts",
    )(block_e, n_used, valid_end, xs, w1, b1.reshape(depth, ne, 1, dff2), w2, b2.reshape(depth, ne, 1, dm))


N_STREAMS = 1
PLAN_TILE = 512
MOE_BLOCK_SHIFT = MOE_BLOCK.bit_length() - 1
assert 1 << MOE_BLOCK_SHIFT == MOE_BLOCK


def _moe_plan_body(e_ref, dest_ref, meta_ref, rank_ref, *, n_tokens, meta_lanes):
    tiles_per_row = n_tokens // PLAN_TILE
    n_tiles = TOP_K * tiles_per_row
    expert = lax.broadcasted_iota(jnp.int32, (N_EXPERTS, PLAN_TILE), 0)
    r_i = lax.broadcasted_iota(jnp.int32, (PLAN_TILE, PLAN_TILE), 0)
    c_i = lax.broadcasted_iota(jnp.int32, (PLAN_TILE, PLAN_TILE), 1)
    earlier = (r_i < c_i).astype(BF16)

    def tile_hits(it):
        j = it // tiles_per_row
        lanes = pl.ds(pl.multiple_of((it % tiles_per_row) * PLAN_TILE, PLAN_TILE), PLAN_TILE)
        return j, lanes, e_ref[pl.ds(j, 1), lanes] == expert

    def rank_step(it, seen):
        j, lanes, hit = tile_hits(it)
        hitf = hit.astype(F32)
        prior = _dot(hit.astype(BF16), earlier) + seen
        rank_ref[pl.ds(j, 1), lanes] = jnp.sum(hitf * prior, axis=0, keepdims=True)
        return seen + jnp.sum(hitf, axis=1, keepdims=True)

    dest_ref[...] = jnp.zeros_like(dest_ref)
    rank_ref[...] = jnp.zeros_like(rank_ref)
    counts = lax.fori_loop(0, n_tiles, rank_step, jnp.zeros((N_EXPERTS, 1), F32))
    padded = ((counts.astype(jnp.int32) + (MOE_BLOCK - 1)) >> MOE_BLOCK_SHIFT) << MOE_BLOCK_SHIFT
    er = lax.broadcasted_iota(jnp.int32, (N_EXPERTS, N_EXPERTS), 0)
    ec = lax.broadcasted_iota(jnp.int32, (N_EXPERTS, N_EXPERTS), 1)
    seg_end = _dot_exact_lhs((ec <= er).astype(BF16),
                             jnp.broadcast_to(padded.astype(F32), (N_EXPERTS, LANE)))[:, 0:1]
    seg_start = seg_end - padded.astype(F32)

    def dest_step(it, carry):
        j, lanes, hit = tile_hits(it)
        base = jnp.sum(jnp.where(hit, seg_start, 0.0), axis=0, keepdims=True)
        dest_ref[pl.ds(j, 1), lanes] = (rank_ref[pl.ds(j, 1), lanes] + base).astype(jnp.int32)
        return carry

    lax.fori_loop(0, n_tiles, dest_step, 0)
    blk_start = (lax.broadcasted_iota(jnp.int32, (N_EXPERTS, meta_lanes), 1) * MOE_BLOCK).astype(F32)
    blk_expert = jnp.minimum(jnp.sum((seg_end <= blk_start).astype(F32), axis=0, keepdims=True), N_EXPERTS - 1.0)
    mine = lax.broadcasted_iota(jnp.int32, (N_EXPERTS, meta_lanes), 0).astype(F32) == blk_expert
    valid_end = jnp.sum(jnp.where(mine, seg_start + counts, 0.0), axis=0, keepdims=True)
    n_used = jnp.broadcast_to(seg_end[N_EXPERTS - 1:N_EXPERTS, :] * (1.0 / MOE_BLOCK), (1, meta_lanes))
    mrow = lax.broadcasted_iota(jnp.int32, (SUBLANE, meta_lanes), 0)
    meta = jnp.where(mrow == 0, blk_expert, jnp.where(mrow == 1, valid_end, jnp.where(mrow == 2, n_used, 0.0)))
    meta_ref[...] = meta.astype(jnp.int32)


def _moe_plan(e_t, n_tokens, n_blocks):
    meta_lanes = -(-n_blocks // LANE) * LANE
    dest, meta = pl.pallas_call(
        functools.partial(_moe_plan_body, n_tokens=n_tokens, meta_lanes=meta_lanes),
        grid=(1,),
        in_specs=[_full((SUBLANE, n_tokens))],
        out_specs=[_full((SUBLANE, n_tokens)), _full((SUBLANE, meta_lanes))],
        out_shape=[jax.ShapeDtypeStruct((SUBLANE, n_tokens), jnp.int32),
                   jax.ShapeDtypeStruct((SUBLANE, meta_lanes), jnp.int32)],
        scratch_shapes=[pltpu.VMEM((SUBLANE, n_tokens), F32)],
        compiler_params=_cparams(1),
        name="moe_plan",
    )(e_t)
    return dest[:TOP_K], meta[0, :n_blocks], meta[1, :n_blocks], meta[2, :1]


def _combine_body(x1_ref, y0_ref, y1_ref, y2_ref, y3_ref, gate_ref, g_ref, b_ref, o_ref, *, alpha):
    gate = gate_ref[...]
    lo, hi = _unpack_bf16_pairs(y0_ref[...])
    lo, hi = gate[:, 0:1] * lo, gate[:, 0:1] * hi
    for j, y_ref in enumerate((y1_ref, y2_ref, y3_ref), start=1):
        lo_j, hi_j = _unpack_bf16_pairs(y_ref[...])
        lo, hi = lo + gate[:, j:j + 1] * lo_j, hi + gate[:, j:j + 1] * hi_j
    ffn = jnp.concatenate([lo, hi], axis=1)
    o_ref[...] = _layer_norm(alpha * x1_ref[...] + ffn, g_ref[...], b_ref[...])


def _combine(x1, yg, gate, ln_g, ln_b, alpha, tm=512):
    t, dm = x1.shape
    n_tiles = t // tm
    expert_rows = lambda j: pl.BlockSpec((tm, dm // 2), lambda i: (i + j * n_tiles, 0))
    return pl.pallas_call(
        functools.partial(_combine_body, alpha=alpha),
        grid=(n_tiles,),
        in_specs=[pl.BlockSpec((tm, dm), lambda i: (i, 0))] + [expert_rows(j) for j in range(TOP_K)]
                 + [pl.BlockSpec((tm, LANE), lambda i: (i, 0)), _full((1, dm)), _full((1, dm))],
        out_specs=pl.BlockSpec((tm, dm), lambda i: (i, 0)),
        out_shape=jax.ShapeDtypeStruct((t, dm), F32),
        compiler_params=_cparams(1),
        name="combine_ln",
    )(x1, yg, yg, yg, yg, gate, ln_g.reshape(1, dm), ln_b.reshape(1, dm))


SC_CORES = 2
SC_SUBCORES = 16
SC_WORKERS = SC_CORES * SC_SUBCORES


def _sc_gather_rows(table, idx, window):
    n = idx.shape[0]
    dim = table.shape[1]
    n_steps = n // (SC_WORKERS * window)
    assert n_steps * window * SC_WORKERS == n and n_steps % 2 == 0 and window % SUBLANE == 0 and window <= LANE
    idx3 = idx.reshape(SC_WORKERS, n_steps, window)
    mesh = plsc.VectorSubcoreMesh(core_axis_name="c", subcore_axis_name="s",
                                  num_cores=SC_CORES, num_subcores=SC_SUBCORES)

    def body(table_hbm, idx_hbm, out_hbm, idx_v, rows_v, gsem, wsem):
        wid = lax.axis_index("s") * SC_CORES + lax.axis_index("c")
        pltpu.sync_copy(idx_hbm.at[wid], idx_v)

        def gather(j, buf):
            return pltpu.make_async_copy(table_hbm.at[idx_v.at[j]], rows_v.at[buf], gsem.at[buf])

        def write(j, buf):
            base = pl.multiple_of((wid * n_steps + j) * window, window)
            return pltpu.make_async_copy(rows_v.at[buf], out_hbm.at[pl.ds(base, window)], wsem.at[buf])

        gather(0, 0).start()

        @pl.loop(0, n_steps, step=2)
        def _(j0):
            for buf in range(2):
                j = j0 + buf
                gather(j, buf).wait()

                @pl.when(j >= 1)
                def _():
                    write(j - 1, 1 - buf).wait()

                @pl.when(j + 1 < n_steps)
                def _():
                    gather(j + 1, 1 - buf).start()

                write(j, buf).start()

        write(n_steps - 1, 1).wait()

    return pl.kernel(
        body, out_type=jax.ShapeDtypeStruct((n, dim), table.dtype), mesh=mesh,
        scratch_types=[pltpu.VMEM((n_steps, window), jnp.int32), pltpu.VMEM((2, window, dim), table.dtype),
                       pltpu.SemaphoreType.DMA((2,)), pltpu.SemaphoreType.DMA((2,))],
        name="sc_gather",
    )(table, idx3)


def _sc_scatter_rows(src, dest, n_out, window):
    t, dim = src.shape
    k = dest.shape[0]
    n_steps = t // (SC_WORKERS * window)
    assert n_steps * window * SC_WORKERS == t and n_steps % 2 == 0 and window % SUBLANE == 0 and window <= LANE
    idx3 = dest.reshape(k, SC_WORKERS, n_steps, window).transpose(1, 2, 0, 3).reshape(SC_WORKERS, n_steps * k, window)
    mesh = plsc.VectorSubcoreMesh(core_axis_name="c", subcore_axis_name="s",
                                  num_cores=SC_CORES, num_subcores=SC_SUBCORES)

    def body(src_hbm, idx_hbm, out_hbm, idx_v, rows_v, rsem, ssem):
        wid = lax.axis_index("s") * SC_CORES + lax.axis_index("c")
        pltpu.sync_copy(idx_hbm.at[wid], idx_v)

        def read(s, buf):
            base = pl.multiple_of((wid * n_steps + s) * window, window)
            return pltpu.make_async_copy(src_hbm.at[pl.ds(base, window)], rows_v.at[buf], rsem.at[buf])

        def scatter(s, j, buf):
            return pltpu.make_async_copy(rows_v.at[buf], out_hbm.at[idx_v.at[s * k + j]], ssem.at[buf])

        read(0, 0).start()

        @pl.loop(0, n_steps, step=2)
        def _(s0):
            for buf in range(2):
                s = s0 + buf
                read(s, buf).wait()

                @pl.when(s >= 1)
                def _():
                    for j in range(k):
                        scatter(s - 1, j, 1 - buf).wait()

                @pl.when(s + 1 < n_steps)
                def _():
                    read(s + 1, 1 - buf).start()

                for j in range(k):
                    scatter(s, j, buf).start()

        for j in range(k):
            scatter(n_steps - 1, j, 1).wait()

    return pl.kernel(
        body, out_type=jax.ShapeDtypeStruct((n_out, dim), src.dtype), mesh=mesh,
        scratch_types=[pltpu.VMEM((n_steps * k, window), jnp.int32), pltpu.VMEM((2, window, dim), src.dtype),
                       pltpu.SemaphoreType.DMA((2,)), pltpu.SemaphoreType.DMA((2,))],
        name="sc_scatter",
    )(src, idx3)


def _pad_cols(w, width):
    return jnp.pad(w, ((0, 0), (0, width - w.shape[1])))


def kernel(x, w_in, gmlp_ln_g, gmlp_ln_b, gmlp_ws, gmlp_bs, rwkv_mu, rwkv_w0, rwkv_w2, rwkv_a0, rwkv_a2, rwkv_g2, rwkv_k_k, rwkv_k_a, rwkv_r_k, rwkv_ln_g, rwkv_ln_b, mlstm_conv_w, mlstm_conv_b, mlstm_gate_b, mlstm_ln_g, w_out, ln1_g, ln1_b, router_w, router_b, exp_w1, exp_b1, exp_w2, exp_b2, ln2_g, ln2_b):
    batch, seq, dm = x.shape
    depth = w_in.shape[0]
    sb = batch // N_STREAMS if batch % N_STREAMS == 0 else batch
    t = sb * seq
    gw = gmlp_ln_g.shape[1]
    rw = rwkv_w0.shape[2]
    mw = mlstm_ln_g.shape[1]
    g_proj = 2 * gw
    r_proj = 3 * rw + W_LORA + A_LORA + G_LORA
    alpha = (2 * depth) ** 0.25
    n_blocks = -(-t * TOP_K // MOE_BLOCK) + N_EXPERTS
    streams = [x[i * sb:(i + 1) * sb].reshape(t, dm) for i in range(batch // sb)]
    for l in range(depth):
        mixed = []
        for xf in streams:
            y_g, pr, pm = _proj(xf, w_in, l, g_proj, r_proj, gmlp_ln_g[l], gmlp_ln_b[l], gmlp_ws[l], gmlp_bs[l])
            r, v, a, kd, b, lw, bonus, rgate = _rwkv_prep(
                pr, seq, rwkv_mu[l], rwkv_w0[l], rwkv_w2[l], rwkv_a0[l], rwkv_a2[l], rwkv_g2[l],
                rwkv_k_k[l], rwkv_k_a[l], rwkv_r_k[l].reshape(-1))
            ro = _rwkv_scan(r, v, a, kd, b, lw, sb, seq)
            q, k, gates = _mlstm_prep(pm, seq, mlstm_conv_w[l], mlstm_conv_b[l], mlstm_gate_b[l], mw)
            mh = _mlstm_scan(q, k, pm, gates, sb, seq)
            mixed.append(_mix_out(xf, y_g, ro, bonus, rgate, rwkv_ln_g[l], rwkv_ln_b[l], mh, pm, mlstm_ln_g[l],
                                  w_out, l, ln1_g[l], ln1_b[l], router_w[l], router_b[l], alpha))
        streams = []
        for x1, x1p, topi, gate in mixed:
            dest, block_e, valid_end, n_used = _moe_plan(topi, t, n_blocks)
            xs = _sc_scatter_rows(x1p, dest, n_blocks * MOE_BLOCK, window=64)
            ys = _moe_experts(xs, block_e, n_used, valid_end, exp_w1, exp_b1, exp_w2, exp_b2, l)
            yg = _sc_gather_rows(ys, dest.reshape(-1), window=64)
            streams.append(_combine(x1, yg, gate, ln2_g[l], ln2_b[l], alpha))
    return jnp.concatenate(streams, axis=0).reshape(batch, seq, dm)
```

```python
import functools
import math

import jax
import jax.numpy as jnp
from jax import lax
from jax.experimental import pallas as pl
from jax.experimental.pallas import tpu as pltpu
from jax.experimental.pallas import tpu_sc as plsc

F32 = jnp.float32
BF16 = jnp.bfloat16
HI = lax.Precision.HIGHEST

HEAD_DIM = 64
GMLP_CHUNK = 128
MLSTM_CHUNK = 128
RWKV_CHUNK = 64
W_LORA = 64
A_LORA = 64
G_LORA = 128
N_EXPERTS = 32
TOP_K = 4
MOE_BLOCK = 512
SWIGLU_LIMIT = 7.0
SWIGLU_ALPHA = 1.702
LN_EPS = 1e-5
RWKV_GN_EPS = 64e-5
LANE = 128
SUBLANE = 8
VMEM_LIMIT = 48 * 1024 * 1024
NEG_BIG = -1e30


def _cparams(n_axes):
    return pltpu.CompilerParams(dimension_semantics=("arbitrary",) * n_axes,
                                vmem_limit_bytes=VMEM_LIMIT)


def _full(shape):
    return pl.BlockSpec(shape, lambda *_: (0,) * len(shape))


def _dot(a, b, precision=None):
    return jnp.dot(a, b, preferred_element_type=F32, precision=precision)


def _dot_nt(a, b, precision=None):
    return lax.dot_general(a, b, (((1,), (1,)), ((), ())), preferred_element_type=F32, precision=precision)


def _dot_tn(a, b, precision=None):
    return lax.dot_general(a, b, (((0,), (0,)), ((), ())), preferred_element_type=F32, precision=precision)


def _split(x):
    hi = x.astype(BF16)
    return hi, (x - hi.astype(F32)).astype(BF16)


def _split3(x):
    hi = x.astype(BF16)
    r1 = x - hi.astype(F32)
    mid = r1.astype(BF16)
    return hi, mid, (r1 - mid.astype(F32)).astype(BF16)


def _mm(a, b, mode, dot=_dot):
    if mode == "hi":
        return dot(a, b, HI)
    if mode == "b1":
        return dot(a.astype(BF16), b.astype(BF16))
    bh, bl = _split(b)
    if mode == "b2":
        ah = a.astype(BF16)
        return dot(ah, bh) + dot(ah, bl)
    ah, al = _split(a)
    return dot(ah, bh) + (dot(ah, bl) + dot(al, bh))


def _dot_exact_lhs(a_bf16, x):
    hi, mid, lo = _split3(x)
    return _dot(a_bf16, hi) + (_dot(a_bf16, mid) + _dot(a_bf16, lo))


def _dot_exact_rhs(x, b_bf16, terms=3):
    if terms == 2:
        hi, lo = _split(x)
        return _dot(hi, b_bf16) + _dot(lo, b_bf16)
    hi, mid, lo = _split3(x)
    return _dot(hi, b_bf16) + (_dot(mid, b_bf16) + _dot(lo, b_bf16))


def _pack_bf16_pairs(x):
    n = x.shape[1] // 2
    lo = pltpu.bitcast(x[:, :n].astype(BF16).astype(F32), jnp.uint32)
    hi = pltpu.bitcast(x[:, n:].astype(BF16).astype(F32), jnp.uint32)
    return hi | (lo >> 16)


def _unpack_bf16_pairs(w):
    lo = pltpu.bitcast(w << 16, F32)
    hi = pltpu.bitcast(w & jnp.uint32(0xFFFF0000), F32)
    return lo, hi


def _sigmoid(x):
    return 1.0 / (1.0 + jnp.exp(-x))


def _softplus(x):
    return jnp.maximum(x, 0.0) + jnp.log1p(jnp.exp(-jnp.abs(x)))


def _block_diag_ones(width):
    h = jnp.arange(width) // HEAD_DIM
    return (h[:, None] == h[None, :]).astype(F32)


CAST_ROWS = 128


def _cast_rows(src_ref, dst_ref):
    n_src, n_dst = src_ref.shape[1], dst_ref.shape[1]
    whole = n_src // LANE * LANE

    def step(r, carry):
        rows = pl.ds(pl.multiple_of(r * CAST_ROWS, CAST_ROWS), CAST_ROWS)
        dst_ref[rows, :whole] = src_ref[rows, :whole].astype(BF16)
        if n_dst > whole:
            tail = [src_ref[rows, whole:]] if n_src > whole else []
            tail.append(jnp.zeros((CAST_ROWS, n_dst - n_src), F32))
            dst_ref[rows, whole:] = jnp.concatenate(tail, axis=1).astype(BF16)
        return carry
    lax.fori_loop(0, src_ref.shape[0] // CAST_ROWS, step, 0)


def _gmlp_gate(p, lng_ref, lnb_ref, ws_ref, bst_ref, o_ref):
    gw = p.shape[1] // 2
    p = 0.5 * p * (1.0 + lax.erf(p * math.sqrt(0.5)))
    u, v = p[:, :gw], p[:, gw:]
    mu = jnp.mean(v, axis=-1, keepdims=True)
    vc = v - mu
    var = jnp.mean(vc * vc, axis=-1, keepdims=True)
    vn = vc * lax.rsqrt(var + LN_EPS) * lng_ref[...] + lnb_ref[...]
    for c in range(p.shape[0] // GMLP_CHUNK):
        rows = slice(c * GMLP_CHUNK, (c + 1) * GMLP_CHUNK)
        ys = []
        for h in range(gw // HEAD_DIM):
            cols = slice(h * HEAD_DIM, (h + 1) * HEAD_DIM)
            ys.append(_dot(ws_ref[h], vn[rows, cols].astype(BF16)) + bst_ref[:, h:h + 1])
        o_ref[rows, :] = u[rows, :] * jnp.concatenate(ys, axis=1)


def _proj_body(x_ref, w_ref, lng_ref, lnb_ref, ws_ref, bst_ref, yg_ref, pr_ref, pm_ref, wb_ref, *, ng, nr):
    @pl.when(pl.program_id(0) == 0)
    def _():
        _cast_rows(w_ref, wb_ref)

    xb = x_ref[...].astype(BF16)
    _gmlp_gate(_dot(xb, wb_ref[:, :ng]), lng_ref, lnb_ref, ws_ref, bst_ref, yg_ref)
    pr_ref[...] = _dot(xb, wb_ref[:, ng:ng + nr])
    pm_ref[...] = _dot(xb, wb_ref[:, ng + nr:])


def _proj(x, w_in, layer, ng, nr, ln_g, ln_b, ws, bs, tm=512):
    t, d = x.shape
    p_in = w_in.shape[2]
    p_pad = -(-p_in // LANE) * LANE
    nm = p_pad - ng - nr
    gw = ng // 2
    n_heads = gw // HEAD_DIM
    bst = jnp.zeros((GMLP_CHUNK, LANE), F32).at[:, :n_heads].set(bs.T)
    row = lambda n: pl.BlockSpec((tm, n), lambda i: (i, 0))
    return pl.pallas_call(
        functools.partial(_proj_body, ng=ng, nr=nr),
        grid=(t // tm,),
        in_specs=[row(d), pl.BlockSpec((None, d, p_in), lambda i: (layer, 0, 0), pipeline_mode=pl.Buffered(1)),
                  _full((1, gw)), _full((1, gw)), _full((n_heads, GMLP_CHUNK, GMLP_CHUNK)),
                  _full((GMLP_CHUNK, LANE))],
        out_specs=[row(gw), row(nr), row(nm)],
        out_shape=[jax.ShapeDtypeStruct((t, n), F32) for n in (gw, nr, nm)],
        scratch_shapes=[pltpu.VMEM((d, p_pad), BF16)],
        compiler_params=_cparams(1),
        name="in_proj",
    )(x, w_in, ln_g.reshape(1, gw), ln_b.reshape(1, gw), ws.astype(BF16), bst)


def _halo_specs(tm, width, n_rows):
    per8 = tm // SUBLANE
    last = n_rows // SUBLANE - 1
    prev = pl.BlockSpec((SUBLANE, width), lambda i: (jnp.maximum(i * per8 - 1, 0), 0))
    nxt = pl.BlockSpec((SUBLANE, width), lambda i: (jnp.minimum((i + 1) * per8, last), 0))
    return prev, nxt


def _neighbours(cur, prev_blk, next_blk, tiles_per_seq):
    tm = cur.shape[0]
    j = pl.program_id(0) % tiles_per_seq
    prev_row = jnp.where(j > 0, prev_blk[SUBLANE - 1:SUBLANE, :], 0.0)
    next_row = jnp.where(j < tiles_per_seq - 1, next_blk[0:1, :], 0.0)
    ridx = lax.broadcasted_iota(jnp.int32, cur.shape, 0)
    before = jnp.where(ridx == 0, prev_row, pltpu.roll(cur, 1, 0))
    after = jnp.where(ridx == tm - 1, next_row, pltpu.roll(cur, tm - 1, 0))
    return before, after


def _rwkv_prep_body(pr_ref, prev_ref, next_ref, mu_ref, w0_ref, w2_ref, a0_ref, a2_ref, g2_ref,
                    kk_ref, ka_ref, rk_ref, bd_ref,
                    r_out, v_out, a_out, kd_out, b_out, lw_out, bonus_out, gate_out, *, rw, tiles_per_seq):
    pf = pr_ref[...]
    before, after = _neighbours(pf, prev_ref[...], next_ref[...], tiles_per_seq)
    pf = pf + mu_ref[0:1, :] * (before - pf) + mu_ref[1:2, :] * (after - pf)
    o3 = 3 * rw
    r, k, v = pf[:, :rw], pf[:, rw:2 * rw], pf[:, 2 * rw:o3]
    wd = pf[:, o3:o3 + W_LORA]
    ad = pf[:, o3 + W_LORA:o3 + W_LORA + A_LORA]
    gd = pf[:, o3 + W_LORA + A_LORA:]
    bd = bd_ref[...]
    kk = k * kk_ref[...]
    ss = _dot((kk * kk).astype(BF16), bd)
    kk = kk / jnp.maximum(jnp.sqrt(ss), 1e-12)
    twd = jnp.tanh(wd)
    ksum = jnp.zeros_like(k)
    for d in range(2):
        w_log = -_softplus(-(w0_ref[d:d + 1, :] + _mm(twd, w2_ref[d], "b3"))) - 0.5
        lw_out[d] = -jnp.exp(w_log)
        iclr = _sigmoid(a0_ref[d:d + 1, :] + _mm(ad, a2_ref[d], "b3"))
        kd = k * (1.0 + (iclr - 1.0) * ka_ref[...])
        kd_out[d] = kd
        b_out[d] = kk * iclr
        ksum = ksum + kd
    r_out[...] = r
    v_out[...] = v
    a_out[...] = -kk
    bonus_out[...] = _dot((r * ksum * rk_ref[...]).astype(BF16), bd) * v
    gate_out[...] = _dot(_sigmoid(gd).astype(BF16), g2_ref[...])


def _rwkv_prep(pr, seq, mu, w0, w2, a0, a2, g2, k_k, k_a, r_k, tm=512):
    t, rproj = pr.shape
    rw = w0.shape[1]
    tiles_per_seq = seq // tm
    prev, nxt = _halo_specs(tm, rproj, t)
    row = pl.BlockSpec((tm, rw), lambda i: (i, 0))
    row2 = pl.BlockSpec((2, tm, rw), lambda i: (0, i, 0))
    one = jax.ShapeDtypeStruct((t, rw), F32)
    two = jax.ShapeDtypeStruct((2, t, rw), F32)
    return pl.pallas_call(
        functools.partial(_rwkv_prep_body, rw=rw, tiles_per_seq=tiles_per_seq),
        grid=(t // tm,),
        in_specs=[pl.BlockSpec((tm, rproj), lambda i: (i, 0)), prev, nxt,
                  _full((2, rproj)), _full((2, rw)), _full((2, W_LORA, rw)), _full((2, rw)),
                  _full((2, A_LORA, rw)), _full((G_LORA, rw)), _full((1, rw)), _full((1, rw)),
                  _full((1, rw)), _full((rw, rw))],
        out_specs=[row, row, row, row2, row2, row2, row, row],
        out_shape=[one, one, one, two, two, two, one, one],
        compiler_params=_cparams(1),
        name="rwkv_prep",
    )(pr, pr, pr, mu, w0, w2, a0, a2, g2.astype(BF16), k_k.reshape(1, rw), k_a.reshape(1, rw),
      r_k.reshape(1, rw), _block_diag_ones(rw).astype(BF16))


P_G, P_INV, P_APPLY, P_STATE, P_SEQ = "b1", "b1", "b1", "b1", "b2"


def _rwkv_intra_body(r_ref, v_ref, a_ref, kd_ref, b_ref, lw_ref, rq_out, o0_out, mtx_out, hc_out,
                     *, n_heads, chunks):
    L = RWKV_CHUNK
    d = pl.program_id(0)
    row = lax.broadcasted_iota(jnp.int32, (L, L), 0)
    col = lax.broadcasted_iota(jnp.int32, (L, L), 1)
    fwd = d == 0
    rel = (col - row) * (1 - 2 * d)
    incl = rel <= 0
    strict = rel < 0
    eye = (row == col).astype(F32)
    tri = incl.astype(BF16)
    pairs = []
    for c in range(chunks):
        rows = slice(c * L, (c + 1) * L)
        lw = lw_ref[rows, :]
        cum = _dot_exact_lhs(tri, lw)
        tot = jnp.where(fwd, cum[L - 1:L, :], cum[0:1, :])
        e_neg = jnp.exp(-cum)
        e_end = jnp.exp(tot - cum)
        e_tot = jnp.exp(tot)
        r, v, a, kd, b = r_ref[rows, :], v_ref[rows, :], a_ref[rows, :], kd_ref[rows, :], b_ref[rows, :]
        at, rt, bt, kt = a * jnp.exp(cum - lw), r * jnp.exp(cum), b * e_neg, kd * e_neg
        kend, bend = kd * e_end, b * e_end
        for h in range(n_heads):
            sl = slice(h * HEAD_DIM, (h + 1) * HEAD_DIM)
            pairs.append(dict(at=at[:, sl], rt=rt[:, sl], bt=bt[:, sl], kt=kt[:, sl], v=v[:, sl],
                              kend=kend[:, sl], bend=bend[:, sl], e_tot=e_tot[:, sl]))
    for p in pairs:
        p["g"] = _mm(jnp.concatenate([p["at"], p["rt"]], axis=0),
                     jnp.concatenate([p["bt"], p["kt"]], axis=0), P_G, _dot_nt)
    row2 = lax.broadcasted_iota(jnp.int32, (L, 2 * L), 0)
    col2 = lax.broadcasted_iota(jnp.int32, (L, 2 * L), 1) & (L - 1)
    rel2 = (col2 - row2) * (1 - 2 * d)
    incl2 = rel2 <= 0
    strict2 = rel2 < 0
    zeros = jnp.zeros((L, HEAD_DIM), F32)
    for p in pairs:
        g = p.pop("g")
        a_both = jnp.where(strict2, g[:L, :], 0.0)
        p["m_both"] = jnp.where(incl2, g[L:, :], 0.0)
        p["pw"] = a_both[:, :L]
        p["a_ak"] = a_both[:, L:]
        p["inv"] = eye + p["pw"]
    for _ in range(int(math.log2(L)) - 1):
        for p in pairs:
            p["pw"] = _mm(p["pw"], p["pw"], P_INV)
        for p in pairs:
            p["inv"] = p["inv"] + _mm(p["inv"], p["pw"], P_INV)
    for p in pairs:
        p["akv"] = _mm(p["a_ak"], p["v"], P_APPLY)
    for p in pairs:
        wu = _mm(p["inv"], jnp.concatenate([p["at"], p["akv"]], axis=1), P_APPLY)
        p["rhs"] = jnp.concatenate([wu, jnp.concatenate([zeros, p["v"]], axis=1)], axis=0)
    for p in pairs:
        p["rq_o0"] = _mm(p["m_both"], p["rhs"], P_APPLY)
    for p in pairs:
        p["m_hc"] = _mm(jnp.concatenate([p["bend"], p["kend"]], axis=0), p["rhs"], P_STATE, _dot_tn)
    for c in range(chunks):
        ps = pairs[c * n_heads:(c + 1) * n_heads]
        rows = slice(c * L, (c + 1) * L)
        krows = slice(c * HEAD_DIM, (c + 1) * HEAD_DIM)
        rq_out[rows, :] = jnp.concatenate([p["rt"] + p["rq_o0"][:, :HEAD_DIM] for p in ps], axis=1)
        o0_out[rows, :] = jnp.concatenate([p["rq_o0"][:, HEAD_DIM:] for p in ps], axis=1)
        mtx_out[krows, :] = jnp.concatenate([eye * p["e_tot"] + p["m_hc"][:, :HEAD_DIM] for p in ps], axis=1)
        hc_out[krows, :] = jnp.concatenate([p["m_hc"][:, HEAD_DIM:] for p in ps], axis=1)


def _rwkv_intra(r, v, a, kd, b, lw, chunks=8):
    t, rw = r.shape
    n_heads = rw // HEAD_DIM
    tm = chunks * RWKV_CHUNK
    tk = chunks * HEAD_DIM
    n_tiles = t // tm
    one = pl.BlockSpec((tm, rw), lambda d, i: (i, 0))
    two = pl.BlockSpec((None, tm, rw), lambda d, i: (d, i, 0))
    twok = pl.BlockSpec((None, tk, rw), lambda d, i: (d, i, 0))
    return pl.pallas_call(
        functools.partial(_rwkv_intra_body, n_heads=n_heads, chunks=chunks),
        grid=(2, n_tiles),
        in_specs=[one, one, one, two, two, two],
        out_specs=[two, two, twok, twok],
        out_shape=[jax.ShapeDtypeStruct((2, t, rw), F32), jax.ShapeDtypeStruct((2, t, rw), F32),
                   jax.ShapeDtypeStruct((2, n_tiles * tk, rw), F32),
                   jax.ShapeDtypeStruct((2, n_tiles * tk, rw), F32)],
        compiler_params=_cparams(2),
        name="rwkv_intra",
    )(r, v, a, kd, b, lw)


def _rwkv_seq_body(rq0, o00, mtx0, hc0, rq1, o01, mtx1, hc1, out0, out1, h_ref, *, n_heads, batch, per_step):
    @pl.when(pl.program_id(0) == 0)
    def _():
        h_ref[...] = jnp.zeros_like(h_ref)

    L = RWKV_CHUNK
    dirs = ((rq0, o00, mtx0, hc0, out0), (rq1, o01, mtx1, hc1, out1))
    states = {(d, bi): h_ref[d, bi] for d in range(2) for bi in range(batch)}
    for step in range(per_step):
        for d, (rq, o0, mtx, hc, out) in enumerate(dirs):
            sub = step if d == 0 else per_step - 1 - step
            rows = slice(sub * L, (sub + 1) * L)
            krows = slice(sub * HEAD_DIM, (sub + 1) * HEAD_DIM)
            for bi in range(batch):
                rq_t, mtx_t, state = rq[bi, rows, :], mtx[bi, krows, :], states[(d, bi)]
                outs, new = [], []
                for h in range(n_heads):
                    sl = slice(h * HEAD_DIM, (h + 1) * HEAD_DIM)
                    prod = _mm(jnp.concatenate([rq_t[:, sl], mtx_t[:, sl]], axis=0), state[:, sl], P_SEQ)
                    outs.append(prod[:L])
                    new.append(prod[L:])
                out[bi, rows, :] = jnp.concatenate(outs, axis=1) + o0[bi, rows, :]
                states[(d, bi)] = jnp.concatenate(new, axis=1) + hc[bi, krows, :]
    for (d, bi), state in states.items():
        h_ref[d, bi] = state


def _rwkv_seq(rq, o0, mtx, hc, batch, seq, per_step=2):
    _, t, rw = rq.shape
    n_heads = rw // HEAD_DIM
    L = RWKV_CHUNK * per_step
    lk = HEAD_DIM * per_step
    nc = seq // L
    as4 = lambda x: x.reshape(2, batch, x.shape[1] // batch, rw)
    rq, o0, mtx, hc = as4(rq), as4(o0), as4(mtx), as4(hc)
    fwd = lambda rows: pl.BlockSpec((None, batch, rows, rw), lambda c: (0, 0, c, 0))
    bwd = lambda rows: pl.BlockSpec((None, batch, rows, rw), lambda c: (1, 0, nc - 1 - c, 0))
    out0, out1 = pl.pallas_call(
        functools.partial(_rwkv_seq_body, n_heads=n_heads, batch=batch, per_step=per_step),
        grid=(nc,),
        in_specs=[fwd(L), fwd(L), fwd(lk), fwd(lk), bwd(L), bwd(L), bwd(lk), bwd(lk)],
        out_specs=[pl.BlockSpec((batch, L, rw), lambda c: (0, c, 0)),
                   pl.BlockSpec((batch, L, rw), lambda c: (0, nc - 1 - c, 0))],
        out_shape=[jax.ShapeDtypeStruct((batch, seq, rw), F32)] * 2,
        scratch_shapes=[pltpu.VMEM((2, batch, HEAD_DIM, rw), F32)],
        compiler_params=_cparams(1),
        name="rwkv_seq",
    )(rq, o0, mtx, hc, rq, o0, mtx, hc)
    return out0.reshape(t, rw), out1.reshape(t, rw)


def _rwkv_scan(r, v, a, kd, b, lw, batch, seq):
    rq, o0, mtx, hc = _rwkv_intra(r, v, a, kd, b, lw)
    return _rwkv_seq(rq, o0, mtx, hc, batch, seq)


def _mlstm_prep_body(qk_ref, prev_ref, next_ref, g_ref, cw_ref, cb_ref, gb_ref, q_out, k_out, gate_out,
                     *, mw, n_heads, tiles_per_seq):
    x = qk_ref[...]
    before, after = _neighbours(x, prev_ref[...], next_ref[...], tiles_per_seq)
    y = cb_ref[...] + before * cw_ref[0:1, :] + x * cw_ref[1:2, :] + after * cw_ref[2:3, :]
    y = y * _sigmoid(y)
    q_out[...] = y[:, :mw]
    k_out[...] = y[:, mw:] * (HEAD_DIM ** -0.5)
    g = g_ref[...] + gb_ref[...]
    lane = lax.broadcasted_iota(jnp.int32, g.shape, 1)
    for d in range(2):
        ig = g if d == 0 else pltpu.roll(g, LANE - n_heads, 1)
        fg = pltpu.roll(g, LANE - (1 + d) * n_heads, 1)
        lf = -_softplus(-fg)
        gate_out[d] = jnp.where(lane < n_heads, ig, jnp.where(lane < 2 * n_heads, lf, 0.0))


def _mlstm_prep(pm, seq, conv_w, conv_b, gate_b, mw, tm=512):
    t = pm.shape[0]
    n_heads = mw // HEAD_DIM
    tiles_per_seq = seq // tm
    w2 = 2 * mw
    prev, nxt = _halo_specs(tm, w2, t)
    gcol = (4 * mw) // LANE
    gb = jnp.zeros((1, LANE), F32).at[0, :4 * n_heads].set(gate_b)
    row = pl.BlockSpec((tm, mw), lambda i: (i, 0))
    return pl.pallas_call(
        functools.partial(_mlstm_prep_body, mw=mw, n_heads=n_heads, tiles_per_seq=tiles_per_seq),
        grid=(t // tm,),
        in_specs=[pl.BlockSpec((tm, w2), lambda i: (i, 0)), prev, nxt,
                  pl.BlockSpec((tm, LANE), lambda i: (i, gcol)),
                  _full((3, w2)), _full((1, w2)), _full((1, LANE))],
        out_specs=[row, row, pl.BlockSpec((2, tm, LANE), lambda i: (0, i, 0))],
        out_shape=[jax.ShapeDtypeStruct((t, mw), F32), jax.ShapeDtypeStruct((t, mw), F32),
                   jax.ShapeDtypeStruct((2, t, LANE), F32)],
        compiler_params=_cparams(1),
        name="mlstm_prep",
    )(pm, pm, pm, pm, conv_w, conv_b.reshape(1, w2), gb)


def _mlstm_scan_body(q0_ref, k0_ref, v0_ref, g0_ref, q1_ref, k1_ref, v1_ref, g1_ref, o0_ref, o1_ref,
                     c_ref, m_ref, *, n_heads, group):
    L = MLSTM_CHUNK
    H = n_heads

    @pl.when(pl.program_id(1) == 0)
    def _():
        c_ref[...] = jnp.zeros_like(c_ref)
        m_ref[...] = jnp.zeros_like(m_ref)

    row = lax.broadcasted_iota(jnp.int32, (L, L), 0)
    col = lax.broadcasted_iota(jnp.int32, (L, L), 1)
    trow = lax.broadcasted_iota(jnp.int32, (L, LANE), 0)
    low = lax.broadcasted_iota(jnp.int32, (L, LANE), 1) < HEAD_DIM
    xr = lax.broadcasted_iota(jnp.int32, (LANE, H * L), 0)
    xc = lax.broadcasted_iota(jnp.int32, (LANE, H * L), 1)
    spread = (xr - H == lax.shift_right_logical(xc, int(math.log2(L)))).astype(BF16)
    hs = []
    dirs = ((q0_ref, k0_ref, v0_ref, g0_ref), (q1_ref, k1_ref, v1_ref, g1_ref))
    for bi, d in [(bi, d) for bi in range(group) for d in range(2)]:
        q_ref, k_ref, v_ref, g_ref = dirs[d]
        incl = (col <= row) if d == 0 else (col >= row)
        last = L - 1 if d == 0 else 0
        g = g_ref[bi]
        bcum = _dot_exact_lhs(incl.astype(BF16), g)
        z = pltpu.roll(g, H, 1) - bcum
        cmax = z
        shift = 1
        while shift < L:
            if d == 0:
                moved = jnp.where(trow >= shift, pltpu.roll(cmax, shift, 0), -jnp.inf)
            else:
                moved = jnp.where(trow < L - shift, pltpu.roll(cmax, L - shift, 0), -jnp.inf)
            cmax = jnp.maximum(cmax, moved)
            shift *= 2
        m_prev = m_ref[bi, d, 0:1, :]
        top = jnp.maximum(cmax, m_prev)
        b_last = bcum[last:last + 1, :]
        lwc = b_last + z
        m_new = jnp.maximum(b_last + m_prev, jnp.max(lwc, axis=0, keepdims=True))
        m_ref[bi, d, 0:1, :] = m_new
        alpha_w = _dot_exact_rhs(-top, spread)
        floor_w = jnp.exp(-_dot_exact_rhs(bcum + top, spread, terms=2))
        wts_w = _dot(jnp.exp(lwc - m_new).astype(BF16), spread)
        rows_w = _dot_exact_rhs(jnp.concatenate(
            [jnp.broadcast_to(m_prev, (SUBLANE, LANE)),
             jnp.broadcast_to(jnp.exp(b_last + m_prev - m_new), (SUBLANE, LANE))], axis=0), spread)
        z_t = z.T
        q, k, v = q_ref[bi], k_ref[bi], v_ref[bi]
        for h in range(H):
            slab = slice(h // 2 * LANE, (h // 2 + 1) * LANE)
            cols = slice(h * L, (h + 1) * L)
            mine = low if h % 2 == 0 else jnp.logical_not(low)
            kh = jnp.where(mine, k[:, slab], 0.0)
            hs.append(dict(
                qh=jnp.where(mine, q[:, slab], 0.0).astype(BF16), kh=kh.astype(BF16),
                vext=jnp.where(mine, v[:, slab], 1.0).astype(BF16),
                decay=jnp.exp(jnp.where(incl, alpha_w[:, cols] + z_t[H + h:H + h + 1, :], -jnp.inf)),
                w_inter=jnp.exp(alpha_w[:, cols] + rows_w[0:1, cols]), floor=floor_w[:, cols],
                wk=(wts_w[:, cols] * kh).astype(BF16), dec=rows_w[SUBLANE:SUBLANE + 1, cols],
                cst=c_ref[bi, d, h]))
    for p in hs:
        p["sc"] = (_dot_nt(p["qh"], p["kh"]) * p["decay"]).astype(BF16)
    for p in hs:
        p["numext"] = _dot(p["sc"], p["vext"]) + p["w_inter"] * _dot(p["qh"], p["cst"].astype(BF16))
    for p in hs:
        p["upd"] = _dot_tn(p["wk"], p["vext"])
    for bi, d in [(bi, d) for bi in range(group) for d in range(2)]:
        o_ref = (o0_ref, o1_ref)[d]
        res = []
        for h in range(H):
            p = hs[(bi * 2 + d) * H + h]
            den = pltpu.roll(p["numext"], HEAD_DIM, 1)
            res.append(p["numext"] / jnp.maximum(jnp.abs(den), p["floor"]))
            c_ref[bi, d, h] = p["dec"] * p["cst"] + p["upd"]
        for pair in range(H // 2):
            o_ref[bi, :, pair * LANE:(pair + 1) * LANE] = jnp.where(low, res[2 * pair], res[2 * pair + 1])


def _mlstm_scan(q, k, pm, gates, batch, seq):
    t, mw = q.shape
    n_heads = mw // HEAD_DIM
    L = MLSTM_CHUNK
    nc = seq // L
    group = math.gcd(batch, 4)
    q3, k3, pm3 = (x.reshape(batch, seq, x.shape[1]) for x in (q, k, pm))
    g4 = gates.reshape(2, batch, seq, LANE)
    specs = []
    for d, blk in enumerate((lambda c: c, lambda c: nc - 1 - c)):
        specs += [pl.BlockSpec((group, L, mw), lambda gi, c, blk=blk: (gi, blk(c), 0)),
                  pl.BlockSpec((group, L, mw), lambda gi, c, blk=blk: (gi, blk(c), 0)),
                  pl.BlockSpec((group, L, mw), lambda gi, c, blk=blk: (gi, blk(c), 2)),
                  pl.BlockSpec((None, group, L, LANE), lambda gi, c, blk=blk, d=d: (d, gi, blk(c), 0))]
    out0, out1 = pl.pallas_call(
        functools.partial(_mlstm_scan_body, n_heads=n_heads, group=group),
        grid=(batch // group, nc),
        in_specs=specs,
        out_specs=[pl.BlockSpec((group, L, mw), lambda gi, c: (gi, c, 0)),
                   pl.BlockSpec((group, L, mw), lambda gi, c: (gi, nc - 1 - c, 0))],
        out_shape=[jax.ShapeDtypeStruct((batch, seq, mw), F32)] * 2,
        scratch_shapes=[pltpu.VMEM((group, 2, n_heads, LANE, LANE), F32),
                        pltpu.VMEM((group, 2, SUBLANE, LANE), F32)],
        compiler_params=_cparams(2),
        name="mlstm_scan",
    )(q3, k3, pm3, g4, q3, k3, pm3, g4)
    return out0.reshape(t, mw), out1.reshape(t, mw)


def _layer_norm(x, g, b):
    mu = jnp.mean(x, axis=-1, keepdims=True)
    xc = x - mu
    var = jnp.mean(xc * xc, axis=-1, keepdims=True)
    return xc * lax.rsqrt(var + LN_EPS) * g + b


def _head_norm(x, bd_mean, eps):
    mu = _dot_exact_rhs(x, bd_mean, terms=2)
    xc = x - mu
    var = _dot_exact_rhs(xc * xc, bd_mean, terms=2)
    return xc * lax.rsqrt(var + eps)


def _mix_out_body(x_ref, yg_ref, ro0_ref, ro1_ref, bonus_ref, rgate_ref, rlg_ref, rlb_ref, mh0_ref, mh1_ref, og_ref,
                  mlg_ref, w_ref, l1g_ref, l1b_ref, rw_ref, rb_ref, bdm_ref,
                  x1_out, x1p_out, topi_out, gate_out, wb_ref, *, alpha, gw, rw):
    @pl.when(pl.program_id(0) == 0)
    def _():
        _cast_rows(w_ref, wb_ref)

    bdm = bdm_ref[...]
    yr = _head_norm(ro0_ref[...] + ro1_ref[...], bdm, RWKV_GN_EPS) * rlg_ref[...] + rlb_ref[...]
    yr = (yr + bonus_ref[...]) * rgate_ref[...]
    ym = _sigmoid(og_ref[...]) * (_head_norm(mh0_ref[...] + mh1_ref[...], bdm, LN_EPS) * mlg_ref[...])
    mix = (_dot(yg_ref[...].astype(BF16), wb_ref[:gw, :]) + _dot(yr.astype(BF16), wb_ref[gw:gw + rw, :])
           + _dot(ym.astype(BF16), wb_ref[gw + rw:, :]))
    x1 = _layer_norm(alpha * x_ref[...] + mix, l1g_ref[...], l1b_ref[...])
    x1_out[...] = x1
    x1p_out[...] = _pack_bf16_pairs(x1)
    lg = _mm(x1, rw_ref[...], "b3") + rb_ref[...]
    lane = lax.broadcasted_iota(jnp.int32, lg.shape, 1)
    vals, topi = [], jnp.zeros(lg.shape, jnp.int32)
    for j in range(TOP_K):
        mx = jnp.max(lg, axis=1, keepdims=True)
        idx = jnp.min(jnp.where(lg == mx, lane, LANE), axis=1, keepdims=True)
        vals.append(mx)
        topi = jnp.where(lane == j, idx, topi)
        lg = jnp.where(lane == idx, -jnp.inf, lg)
    es = [jnp.exp(vj - vals[0]) for vj in vals]
    den = es[0] + es[1] + es[2] + es[3]
    gate = jnp.zeros(lg.shape, F32)
    for j in range(TOP_K):
        gate = jnp.where(lane == j, es[j] / den, gate)
    topi_out[...] = topi.T[:SUBLANE, :]
    gate_out[...] = gate


def _mix_out(x, yg, ro, bonus, rgate, rlg, rlb, mh, pm, mlg, w_out, layer, l1g, l1b, router_w, router_b, alpha,
             tm=512):
    t, dm = x.shape
    gw, rw, mw = yg.shape[1], bonus.shape[1], mh[0].shape[1]
    assert rw == mw
    rwp = jnp.zeros((dm, LANE), F32).at[:, :N_EXPERTS].set(router_w)
    rbp = jnp.full((1, LANE), NEG_BIG, F32).at[0, :N_EXPERTS].set(router_b)
    row = lambda n: pl.BlockSpec((tm, n), lambda i: (i, 0))
    vec = lambda n: _full((1, n))
    return pl.pallas_call(
        functools.partial(_mix_out_body, alpha=alpha, gw=gw, rw=rw),
        grid=(t // tm,),
        in_specs=[row(dm), row(gw), row(rw), row(rw), row(rw), row(rw), vec(rw), vec(rw), row(mw), row(mw),
                  pl.BlockSpec((tm, mw), lambda i: (i, 3)),
                  vec(mw),
                  pl.BlockSpec((None, dm, dm), lambda i: (layer, 0, 0), pipeline_mode=pl.Buffered(1)),
                  vec(dm), vec(dm), _full((dm, LANE)), vec(LANE), _full((rw, rw))],
        out_specs=[row(dm), row(dm // 2), pl.BlockSpec((SUBLANE, tm), lambda i: (0, i)), row(LANE)],
        out_shape=[jax.ShapeDtypeStruct((t, dm), F32), jax.ShapeDtypeStruct((t, dm // 2), jnp.uint32),
                   jax.ShapeDtypeStruct((SUBLANE, t), jnp.int32), jax.ShapeDtypeStruct((t, LANE), F32)],
        scratch_shapes=[pltpu.VMEM((dm, dm), BF16)],
        compiler_params=_cparams(1),
        name="mix_out",
    )(x, yg, ro[0], ro[1], bonus, rgate, rlg.reshape(1, rw), rlb.reshape(1, rw), mh[0], mh[1], pm,
      mlg.reshape(1, mw), w_out, l1g.reshape(1, dm), l1b.reshape(1, dm), rwp, rbp,
      (_block_diag_ones(rw) / HEAD_DIM).astype(BF16))


def _moe_body(be_ref, nu_ref, ve_ref, xs_ref, w1_ref, b1_ref, w2_ref, b2_ref, o_ref, *, dff):
    i = pl.program_id(0)
    active = i < nu_ref[0]

    @pl.when(active)
    def _():
        rowid = i * MOE_BLOCK + lax.broadcasted_iota(jnp.int32, (MOE_BLOCK, 1), 0)
        lo, hi = _unpack_bf16_pairs(jnp.where(rowid < ve_ref[i], xs_ref[...], jnp.uint32(0)))
        xs = jnp.concatenate([lo.astype(BF16), hi.astype(BF16)], axis=1)
        hdn = _dot(xs, w1_ref[...].astype(BF16)) + b1_ref[...]
        g_ = jnp.minimum(hdn[:, :dff], SWIGLU_LIMIT)
        u_ = jnp.clip(hdn[:, dff:], -SWIGLU_LIMIT, SWIGLU_LIMIT)
        act = (u_ + 1.0) * (g_ * _sigmoid(g_ * SWIGLU_ALPHA))
        o_ref[...] = _pack_bf16_pairs(_dot(act.astype(BF16), w2_ref[...].astype(BF16)) + b2_ref[...])

    @pl.when(jnp.logical_not(active))
    def _():
        o_ref[...] = jnp.zeros_like(o_ref)


def _moe_experts(xs, block_e, n_used, valid_end, w1, b1, w2, b2, layer):
    rows, half = xs.shape
    nb = rows // MOE_BLOCK
    depth, ne, dm, dff2 = w1.shape
    dff = dff2 // 2
    grid_spec = pltpu.PrefetchScalarGridSpec(
        num_scalar_prefetch=3,
        grid=(nb,),
        in_specs=[pl.BlockSpec((MOE_BLOCK, half), lambda i, be, nu, ve: (i, 0)),
                  pl.BlockSpec((None, None, dm, dff2), lambda i, be, nu, ve: (layer, be[i], 0, 0)),
                  pl.BlockSpec((None, None, 1, dff2), lambda i, be, nu, ve: (layer, be[i], 0, 0)),
                  pl.BlockSpec((None, None, dff, dm), lambda i, be, nu, ve: (layer, be[i], 0, 0)),
                  pl.BlockSpec((None, None, 1, dm), lambda i, be, nu, ve: (layer, be[i], 0, 0))],
        out_specs=pl.BlockSpec((MOE_BLOCK, half), lambda i, be, nu, ve: (i, 0)),
    )
    return pl.pallas_call(
        functools.partial(_moe_body, dff=dff),
        grid_spec=grid_spec,
        out_shape=jax.ShapeDtypeStruct((rows, half), jnp.uint32),
        compiler_params=_cparams(1),
        name="moe_experts",
    )(block_e, n_used, valid_end, xs, w1, b1.reshape(depth, ne, 1, dff2), w2, b2.reshape(depth, ne, 1, dm))


N_STREAMS = 1
PLAN_TILE = 512
MOE_BLOCK_SHIFT = MOE_BLOCK.bit_length() - 1
assert 1 << MOE_BLOCK_SHIFT == MOE_BLOCK


def _moe_plan_body(e_ref, dest_ref, meta_ref, rank_ref, *, n_tokens, meta_lanes):
    tiles_per_row = n_tokens // PLAN_TILE
    n_tiles = TOP_K * tiles_per_row
    expert = lax.broadcasted_iota(jnp.int32, (N_EXPERTS, PLAN_TILE), 0)
    r_i = lax.broadcasted_iota(jnp.int32, (PLAN_TILE, PLAN_TILE), 0)
    c_i = lax.broadcasted_iota(jnp.int32, (PLAN_TILE, PLAN_TILE), 1)
    earlier = (r_i < c_i).astype(BF16)

    def tile_hits(it):
        j = it // tiles_per_row
        lanes = pl.ds(pl.multiple_of((it % tiles_per_row) * PLAN_TILE, PLAN_TILE), PLAN_TILE)
        return j, lanes, e_ref[pl.ds(j, 1), lanes] == expert

    def rank_step(it, seen):
        j, lanes, hit = tile_hits(it)
        hitf = hit.astype(F32)
        prior = _dot(hit.astype(BF16), earlier) + seen
        rank_ref[pl.ds(j, 1), lanes] = jnp.sum(hitf * prior, axis=0, keepdims=True)
        return seen + jnp.sum(hitf, axis=1, keepdims=True)

    dest_ref[...] = jnp.zeros_like(dest_ref)
    rank_ref[...] = jnp.zeros_like(rank_ref)
    counts = lax.fori_loop(0, n_tiles, rank_step, jnp.zeros((N_EXPERTS, 1), F32))
    padded = ((counts.astype(jnp.int32) + (MOE_BLOCK - 1)) >> MOE_BLOCK_SHIFT) << MOE_BLOCK_SHIFT
    er = lax.broadcasted_iota(jnp.int32, (N_EXPERTS, N_EXPERTS), 0)
    ec = lax.broadcasted_iota(jnp.int32, (N_EXPERTS, N_EXPERTS), 1)
    seg_end = _dot_exact_lhs((ec <= er).astype(BF16),
                             jnp.broadcast_to(padded.astype(F32), (N_EXPERTS, LANE)))[:, 0:1]
    seg_start = seg_end - padded.astype(F32)

    def dest_step(it, carry):
        j, lanes, hit = tile_hits(it)
        base = jnp.sum(jnp.where(hit, seg_start, 0.0), axis=0, keepdims=True)
        dest_ref[pl.ds(j, 1), lanes] = (rank_ref[pl.ds(j, 1), lanes] + base).astype(jnp.int32)
        return carry

    lax.fori_loop(0, n_tiles, dest_step, 0)
    blk_start = (lax.broadcasted_iota(jnp.int32, (N_EXPERTS, meta_lanes), 1) * MOE_BLOCK).astype(F32)
    blk_expert = jnp.minimum(jnp.sum((seg_end <= blk_start).astype(F32), axis=0, keepdims=True), N_EXPERTS - 1.0)
    mine = lax.broadcasted_iota(jnp.int32, (N_EXPERTS, meta_lanes), 0).astype(F32) == blk_expert
    valid_end = jnp.sum(jnp.where(mine, seg_start + counts, 0.0), axis=0, keepdims=True)
    n_used = jnp.broadcast_to(seg_end[N_EXPERTS - 1:N_EXPERTS, :] * (1.0 / MOE_BLOCK), (1, meta_lanes))
    mrow = lax.broadcasted_iota(jnp.int32, (SUBLANE, meta_lanes), 0)
    meta = jnp.where(mrow == 0, blk_expert, jnp.where(mrow == 1, valid_end, jnp.where(mrow == 2, n_used, 0.0)))
    meta_ref[...] = meta.astype(jnp.int32)


def _moe_plan(e_t, n_tokens, n_blocks):
    meta_lanes = -(-n_blocks // LANE) * LANE
    dest, meta = pl.pallas_call(
        functools.partial(_moe_plan_body, n_tokens=n_tokens, meta_lanes=meta_lanes),
        grid=(1,),
        in_specs=[_full((SUBLANE, n_tokens))],
        out_specs=[_full((SUBLANE, n_tokens)), _full((SUBLANE, meta_lanes))],
        out_shape=[jax.ShapeDtypeStruct((SUBLANE, n_tokens), jnp.int32),
                   jax.ShapeDtypeStruct((SUBLANE, meta_lanes), jnp.int32)],
        scratch_shapes=[pltpu.VMEM((SUBLANE, n_tokens), F32)],
        compiler_params=_cparams(1),
        name="moe_plan",
    )(e_t)
    return dest[:TOP_K], meta[0, :n_blocks], meta[1, :n_blocks], meta[2, :1]


def _combine_body(x1_ref, y0_ref, y1_ref, y2_ref, y3_ref, gate_ref, g_ref, b_ref, o_ref, *, alpha):
    gate = gate_ref[...]
    lo, hi = _unpack_bf16_pairs(y0_ref[...])
    lo, hi = gate[:, 0:1] * lo, gate[:, 0:1] * hi
    for j, y_ref in enumerate((y1_ref, y2_ref, y3_ref), start=1):
        lo_j, hi_j = _unpack_bf16_pairs(y_ref[...])
        lo, hi = lo + gate[:, j:j + 1] * lo_j, hi + gate[:, j:j + 1] * hi_j
    ffn = jnp.concatenate([lo, hi], axis=1)
    o_ref[...] = _layer_norm(alpha * x1_ref[...] + ffn, g_ref[...], b_ref[...])


def _combine(x1, yg, gate, ln_g, ln_b, alpha, tm=512):
    t, dm = x1.shape
    n_tiles = t // tm
    expert_rows = lambda j: pl.BlockSpec((tm, dm // 2), lambda i: (i + j * n_tiles, 0))
    return pl.pallas_call(
        functools.partial(_combine_body, alpha=alpha),
        grid=(n_tiles,),
        in_specs=[pl.BlockSpec((tm, dm), lambda i: (i, 0))] + [expert_rows(j) for j in range(TOP_K)]
                 + [pl.BlockSpec((tm, LANE), lambda i: (i, 0)), _full((1, dm)), _full((1, dm))],
        out_specs=pl.BlockSpec((tm, dm), lambda i: (i, 0)),
        out_shape=jax.ShapeDtypeStruct((t, dm), F32),
        compiler_params=_cparams(1),
        name="combine_ln",
    )(x1, yg, yg, yg, yg, gate, ln_g.reshape(1, dm), ln_b.reshape(1, dm))


SC_CORES = 2
SC_SUBCORES = 16
SC_WORKERS = SC_CORES * SC_SUBCORES


def _sc_gather_rows(table, idx, window):
    n = idx.shape[0]
    dim = table.shape[1]
    n_steps = n // (SC_WORKERS * window)
    assert n_steps * window * SC_WORKERS == n and n_steps % 2 == 0 and window % SUBLANE == 0 and window <= LANE
    idx3 = idx.reshape(SC_WORKERS, n_steps, window)
    mesh = plsc.VectorSubcoreMesh(core_axis_name="c", subcore_axis_name="s",
                                  num_cores=SC_CORES, num_subcores=SC_SUBCORES)

    def body(table_hbm, idx_hbm, out_hbm, idx_v, rows_v, gsem, wsem):
        wid = lax.axis_index("s") * SC_CORES + lax.axis_index("c")
        pltpu.sync_copy(idx_hbm.at[wid], idx_v)

        def gather(j, buf):
            return pltpu.make_async_copy(table_hbm.at[idx_v.at[j]], rows_v.at[buf], gsem.at[buf])

        def write(j, buf):
            base = pl.multiple_of((wid * n_steps + j) * window, window)
            return pltpu.make_async_copy(rows_v.at[buf], out_hbm.at[pl.ds(base, window)], wsem.at[buf])

        gather(0, 0).start()

        @pl.loop(0, n_steps, step=2)
        def _(j0):
            for buf in range(2):
                j = j0 + buf
                gather(j, buf).wait()

                @pl.when(j >= 1)
                def _():
                    write(j - 1, 1 - buf).wait()

                @pl.when(j + 1 < n_steps)
                def _():
                    gather(j + 1, 1 - buf).start()

                write(j, buf).start()

        write(n_steps - 1, 1).wait()

    return pl.kernel(
        body, out_type=jax.ShapeDtypeStruct((n, dim), table.dtype), mesh=mesh,
        scratch_types=[pltpu.VMEM((n_steps, window), jnp.int32), pltpu.VMEM((2, window, dim), table.dtype),
                       pltpu.SemaphoreType.DMA((2,)), pltpu.SemaphoreType.DMA((2,))],
        name="sc_gather",
    )(table, idx3)


def _sc_scatter_rows(src, dest, n_out, window):
    t, dim = src.shape
    k = dest.shape[0]
    n_steps = t // (SC_WORKERS * window)
    assert n_steps * window * SC_WORKERS == t and n_steps % 2 == 0 and window % SUBLANE == 0 and window <= LANE
    idx3 = dest.reshape(k, SC_WORKERS, n_steps, window).transpose(1, 2, 0, 3).reshape(SC_WORKERS, n_steps * k, window)
    mesh = plsc.VectorSubcoreMesh(core_axis_name="c", subcore_axis_name="s",
                                  num_cores=SC_CORES, num_subcores=SC_SUBCORES)

    def body(src_hbm, idx_hbm, out_hbm, idx_v, rows_v, rsem, ssem):
        wid = lax.axis_index("s") * SC_CORES + lax.axis_index("c")
        pltpu.sync_copy(idx_hbm.at[wid], idx_v)

        def read(s, buf):
            base = pl.multiple_of((wid * n_steps + s) * window, window)
            return pltpu.make_async_copy(src_hbm.at[pl.ds(base, window)], rows_v.at[buf], rsem.at[buf])

        def scatter(s, j, buf):
            return pltpu.make_async_copy(rows_v.at[buf], out_hbm.at[idx_v.at[s * k + j]], ssem.at[buf])

        read(0, 0).start()

        @pl.loop(0, n_steps, step=2)
        def _(s0):
            for buf in range(2):
                s = s0 + buf
                read(s, buf).wait()

                @pl.when(s >= 1)
                def _():
                    for j in range(k):
                        scatter(s - 1, j, 1 - buf).wait()

                @pl.when(s + 1 < n_steps)
                def _():
                    read(s + 1, 1 - buf).start()

                for j in range(k):
                    scatter(s, j, buf).start()

        for j in range(k):
            scatter(n_steps - 1, j, 1).wait()

    return pl.kernel(
        body, out_type=jax.ShapeDtypeStruct((n_out, dim), src.dtype), mesh=mesh,
        scratch_types=[pltpu.VMEM((n_steps * k, window), jnp.int32), pltpu.VMEM((2, window, dim), src.dtype),
                       pltpu.SemaphoreType.DMA((2,)), pltpu.SemaphoreType.DMA((2,))],
        name="sc_scatter",
    )(src, idx3)


def _pad_cols(w, width):
    return jnp.pad(w, ((0, 0), (0, width - w.shape[1])))


def kernel(x, w_in, gmlp_ln_g, gmlp_ln_b, gmlp_ws, gmlp_bs, rwkv_mu, rwkv_w0, rwkv_w2, rwkv_a0, rwkv_a2, rwkv_g2, rwkv_k_k, rwkv_k_a, rwkv_r_k, rwkv_ln_g, rwkv_ln_b, mlstm_conv_w, mlstm_conv_b, mlstm_gate_b, mlstm_ln_g, w_out, ln1_g, ln1_b, router_w, router_b, exp_w1, exp_b1, exp_w2, exp_b2, ln2_g, ln2_b):
    batch, seq, dm = x.shape
    depth = w_in.shape[0]
    sb = batch // N_STREAMS if batch % N_STREAMS == 0 else batch
    t = sb * seq
    gw = gmlp_ln_g.shape[1]
    rw = rwkv_w0.shape[2]
    mw = mlstm_ln_g.shape[1]
    g_proj = 2 * gw
    r_proj = 3 * rw + W_LORA + A_LORA + G_LORA
    alpha = (2 * depth) ** 0.25
    n_blocks = -(-t * TOP_K // MOE_BLOCK) + N_EXPERTS
    streams = [x[i * sb:(i + 1) * sb].reshape(t, dm) for i in range(batch // sb)]
    for l in range(depth):
        mixed = []
        for xf in streams:
            y_g, pr, pm = _proj(xf, w_in, l, g_proj, r_proj, gmlp_ln_g[l], gmlp_ln_b[l], gmlp_ws[l], gmlp_bs[l])
            r, v, a, kd, b, lw, bonus, rgate = _rwkv_prep(
                pr, seq, rwkv_mu[l], rwkv_w0[l], rwkv_w2[l], rwkv_a0[l], rwkv_a2[l], rwkv_g2[l],
                rwkv_k_k[l], rwkv_k_a[l], rwkv_r_k[l].reshape(-1))
            ro = _rwkv_scan(r, v, a, kd, b, lw, sb, seq)
            q, k, gates = _mlstm_prep(pm, seq, mlstm_conv_w[l], mlstm_conv_b[l], mlstm_gate_b[l], mw)
            mh = _mlstm_scan(q, k, pm, gates, sb, seq)
            mixed.append(_mix_out(xf, y_g, ro, bonus, rgate, rwkv_ln_g[l], rwkv_ln_b[l], mh, pm, mlstm_ln_g[l],
                                  w_out, l, ln1_g[l], ln1_b[l], router_w[l], router_b[l], alpha))
        streams = []
        for x1, x1p, topi, gate in mixed:
            dest, block_e, valid_end, n_used = _moe_plan(topi, t, n_blocks)
            xs = _sc_scatter_rows(x1p, dest, n_blocks * MOE_BLOCK, window=64)
            ys = _moe_experts(xs, block_e, n_used, valid_end, exp_w1, exp_b1, exp_w2, exp_b2, l)
            yg = _sc_gather_rows(ys, dest.reshape(-1), window=64)
            streams.append(_combine(x1, yg, gate, ln2_g[l], ln2_b[l], alpha))
    return jnp.concatenate(streams, axis=0).reshape(batch, seq, dm)
```

```python
import functools
import math

import jax
import jax.numpy as jnp
from jax import lax
from jax.experimental import pallas as pl
from jax.experimental.pallas import tpu as pltpu
from jax.experimental.pallas import tpu_sc as plsc

F32 = jnp.float32
BF16 = jnp.bfloat16
HI = lax.Precision.HIGHEST

HEAD_DIM = 64
GMLP_CHUNK = 128
MLSTM_CHUNK = 128
RWKV_CHUNK = 64
W_LORA = 64
A_LORA = 64
G_LORA = 128
N_EXPERTS = 32
TOP_K = 4
MOE_BLOCK = 512
SWIGLU_LIMIT = 7.0
SWIGLU_ALPHA = 1.702
LN_EPS = 1e-5
RWKV_GN_EPS = 64e-5
LANE = 128
SUBLANE = 8
VMEM_LIMIT = 48 * 1024 * 1024
NEG_BIG = -1e30


def _cparams(n_axes):
    return pltpu.CompilerParams(dimension_semantics=("arbitrary",) * n_axes,
                                vmem_limit_bytes=VMEM_LIMIT)


def _full(shape):
    return pl.BlockSpec(shape, lambda *_: (0,) * len(shape))


def _dot(a, b, precision=None):
    return jnp.dot(a, b, preferred_element_type=F32, precision=precision)


def _dot_nt(a, b, precision=None):
    return lax.dot_general(a, b, (((1,), (1,)), ((), ())), preferred_element_type=F32, precision=precision)


def _dot_tn(a, b, precision=None):
    return lax.dot_general(a, b, (((0,), (0,)), ((), ())), preferred_element_type=F32, precision=precision)


def _split(x):
    hi = x.astype(BF16)
    return hi, (x - hi.astype(F32)).astype(BF16)


def _split3(x):
    hi = x.astype(BF16)
    r1 = x - hi.astype(F32)
    mid = r1.astype(BF16)
    return hi, mid, (r1 - mid.astype(F32)).astype(BF16)


def _mm(a, b, mode, dot=_dot):
    if mode == "hi":
        return dot(a, b, HI)
    if mode == "b1":
        return dot(a.astype(BF16), b.astype(BF16))
    bh, bl = _split(b)
    if mode == "b2":
        ah = a.astype(BF16)
        return dot(ah, bh) + dot(ah, bl)
    ah, al = _split(a)
    return dot(ah, bh) + (dot(ah, bl) + dot(al, bh))


def _dot_exact_lhs(a_bf16, x):
    hi, mid, lo = _split3(x)
    return _dot(a_bf16, hi) + (_dot(a_bf16, mid) + _dot(a_bf16, lo))


def _dot_exact_rhs(x, b_bf16, terms=3):
    if terms == 2:
        hi, lo = _split(x)
        return _dot(hi, b_bf16) + _dot(lo, b_bf16)
    hi, mid, lo = _split3(x)
    return _dot(hi, b_bf16) + (_dot(mid, b_bf16) + _dot(lo, b_bf16))


def _pack_bf16_pairs(x):
    n = x.shape[1] // 2
    lo = pltpu.bitcast(x[:, :n].astype(BF16).astype(F32), jnp.uint32)
    hi = pltpu.bitcast(x[:, n:].astype(BF16).astype(F32), jnp.uint32)
    return hi | (lo >> 16)


def _unpack_bf16_pairs(w):
    lo = pltpu.bitcast(w << 16, F32)
    hi = pltpu.bitcast(w & jnp.uint32(0xFFFF0000), F32)
    return lo, hi


def _sigmoid(x):
    return 1.0 / (1.0 + jnp.exp(-x))


def _softplus(x):
    return jnp.maximum(x, 0.0) + jnp.log1p(jnp.exp(-jnp.abs(x)))


def _block_diag_ones(width):
    h = jnp.arange(width) // HEAD_DIM
    return (h[:, None] == h[None, :]).astype(F32)


CAST_ROWS = 128


def _cast_rows(src_ref, dst_ref):
    n_src, n_dst = src_ref.shape[1], dst_ref.shape[1]
    whole = n_src // LANE * LANE

    def step(r, carry):
        rows = pl.ds(pl.multiple_of(r * CAST_ROWS, CAST_ROWS), CAST_ROWS)
        dst_ref[rows, :whole] = src_ref[rows, :whole].astype(BF16)
        if n_dst > whole:
            tail = [src_ref[rows, whole:]] if n_src > whole else []
            tail.append(jnp.zeros((CAST_ROWS, n_dst - n_src), F32))
            dst_ref[rows, whole:] = jnp.concatenate(tail, axis=1).astype(BF16)
        return carry
    lax.fori_loop(0, src_ref.shape[0] // CAST_ROWS, step, 0)


def _gmlp_gate(p, lng_ref, lnb_ref, ws_ref, bst_ref, o_ref):
    gw = p.shape[1] // 2
    p = 0.5 * p * (1.0 + lax.erf(p * math.sqrt(0.5)))
    u, v = p[:, :gw], p[:, gw:]
    mu = jnp.mean(v, axis=-1, keepdims=True)
    vc = v - mu
    var = jnp.mean(vc * vc, axis=-1, keepdims=True)
    vn = vc * lax.rsqrt(var + LN_EPS) * lng_ref[...] + lnb_ref[...]
    for c in range(p.shape[0] // GMLP_CHUNK):
        rows = slice(c * GMLP_CHUNK, (c + 1) * GMLP_CHUNK)
        ys = []
        for h in range(gw // HEAD_DIM):
            cols = slice(h * HEAD_DIM, (h + 1) * HEAD_DIM)
            ys.append(_dot(ws_ref[h], vn[rows, cols].astype(BF16)) + bst_ref[:, h:h + 1])
        o_ref[rows, :] = u[rows, :] * jnp.concatenate(ys, axis=1)


def _proj_body(x_ref, w_ref, lng_ref, lnb_ref, ws_ref, bst_ref, yg_ref, pr_ref, pm_ref, wb_ref, *, ng, nr):
    @pl.when(pl.program_id(0) == 0)
    def _():
        _cast_rows(w_ref, wb_ref)

    xb = x_ref[...].astype(BF16)
    _gmlp_gate(_dot(xb, wb_ref[:, :ng]), lng_ref, lnb_ref, ws_ref, bst_ref, yg_ref)
    pr_ref[...] = _dot(xb, wb_ref[:, ng:ng + nr])
    pm_ref[...] = _dot(xb, wb_ref[:, ng + nr:])


def _proj(x, w_in, layer, ng, nr, ln_g, ln_b, ws, bs, tm=512):
    t, d = x.shape
    p_in = w_in.shape[2]
    p_pad = -(-p_in // LANE) * LANE
    nm = p_pad - ng - nr
    gw = ng // 2
    n_heads = gw // HEAD_DIM
    bst = jnp.zeros((GMLP_CHUNK, LANE), F32).at[:, :n_heads].set(bs.T)
    row = lambda n: pl.BlockSpec((tm, n), lambda i: (i, 0))
    return pl.pallas_call(
        functools.partial(_proj_body, ng=ng, nr=nr),
        grid=(t // tm,),
        in_specs=[row(d), pl.BlockSpec((None, d, p_in), lambda i: (layer, 0, 0), pipeline_mode=pl.Buffered(1)),
                  _full((1, gw)), _full((1, gw)), _full((n_heads, GMLP_CHUNK, GMLP_CHUNK)),
                  _full((GMLP_CHUNK, LANE))],
        out_specs=[row(gw), row(nr), row(nm)],
        out_shape=[jax.ShapeDtypeStruct((t, n), F32) for n in (gw, nr, nm)],
        scratch_shapes=[pltpu.VMEM((d, p_pad), BF16)],
        compiler_params=_cparams(1),
        name="in_proj",
    )(x, w_in, ln_g.reshape(1, gw), ln_b.reshape(1, gw), ws.astype(BF16), bst)


def _halo_specs(tm, width, n_rows):
    per8 = tm // SUBLANE
    last = n_rows // SUBLANE - 1
    prev = pl.BlockSpec((SUBLANE, width), lambda i: (jnp.maximum(i * per8 - 1, 0), 0))
    nxt = pl.BlockSpec((SUBLANE, width), lambda i: (jnp.minimum((i + 1) * per8, last), 0))
    return prev, nxt


def _neighbours(cur, prev_blk, next_blk, tiles_per_seq):
    tm = cur.shape[0]
    j = pl.program_id(0) % tiles_per_seq
    prev_row = jnp.where(j > 0, prev_blk[SUBLANE - 1:SUBLANE, :], 0.0)
    next_row = jnp.where(j < tiles_per_seq - 1, next_blk[0:1, :], 0.0)
    ridx = lax.broadcasted_iota(jnp.int32, cur.shape, 0)
    before = jnp.where(ridx == 0, prev_row, pltpu.roll(cur, 1, 0))
    after = jnp.where(ridx == tm - 1, next_row, pltpu.roll(cur, tm - 1, 0))
    return before, after


def _rwkv_prep_body(pr_ref, prev_ref, next_ref, mu_ref, w0_ref, w2_ref, a0_ref, a2_ref, g2_ref,
                    kk_ref, ka_ref, rk_ref, bd_ref,
                    r_out, v_out, a_out, kd_out, b_out, lw_out, bonus_out, gate_out, *, rw, tiles_per_seq):
    pf = pr_ref[...]
    before, after = _neighbours(pf, prev_ref[...], next_ref[...], tiles_per_seq)
    pf = pf + mu_ref[0:1, :] * (before - pf) + mu_ref[1:2, :] * (after - pf)
    o3 = 3 * rw
    r, k, v = pf[:, :rw], pf[:, rw:2 * rw], pf[:, 2 * rw:o3]
    wd = pf[:, o3:o3 + W_LORA]
    ad = pf[:, o3 + W_LORA:o3 + W_LORA + A_LORA]
    gd = pf[:, o3 + W_LORA + A_LORA:]
    bd = bd_ref[...]
    kk = k * kk_ref[...]
    ss = _dot((kk * kk).astype(BF16), bd)
    kk = kk / jnp.maximum(jnp.sqrt(ss), 1e-12)
    twd = jnp.tanh(wd)
    ksum = jnp.zeros_like(k)
    for d in range(2):
        w_log = -_softplus(-(w0_ref[d:d + 1, :] + _mm(twd, w2_ref[d], "b3"))) - 0.5
        lw_out[d] = -jnp.exp(w_log)
        iclr = _sigmoid(a0_ref[d:d + 1, :] + _mm(ad, a2_ref[d], "b3"))
        kd = k * (1.0 + (iclr - 1.0) * ka_ref[...])
        kd_out[d] = kd
        b_out[d] = kk * iclr
        ksum = ksum + kd
    r_out[...] = r
    v_out[...] = v
    a_out[...] = -kk
    bonus_out[...] = _dot((r * ksum * rk_ref[...]).astype(BF16), bd) * v
    gate_out[...] = _dot(_sigmoid(gd).astype(BF16), g2_ref[...])


def _rwkv_prep(pr, seq, mu, w0, w2, a0, a2, g2, k_k, k_a, r_k, tm=512):
    t, rproj = pr.shape
    rw = w0.shape[1]
    tiles_per_seq = seq // tm
    prev, nxt = _halo_specs(tm, rproj, t)
    row = pl.BlockSpec((tm, rw), lambda i: (i, 0))
    row2 = pl.BlockSpec((2, tm, rw), lambda i: (0, i, 0))
    one = jax.ShapeDtypeStruct((t, rw), F32)
    two = jax.ShapeDtypeStruct((2, t, rw), F32)
    return pl.pallas_call(
        functools.partial(_rwkv_prep_body, rw=rw, tiles_per_seq=tiles_per_seq),
        grid=(t // tm,),
        in_specs=[pl.BlockSpec((tm, rproj), lambda i: (i, 0)), prev, nxt,
                  _full((2, rproj)), _full((2, rw)), _full((2, W_LORA, rw)), _full((2, rw)),
                  _full((2, A_LORA, rw)), _full((G_LORA, rw)), _full((1, rw)), _full((1, rw)),
                  _full((1, rw)), _full((rw, rw))],
        out_specs=[row, row, row, row2, row2, row2, row, row],
        out_shape=[one, one, one, two, two, two, one, one],
        compiler_params=_cparams(1),
        name="rwkv_prep",
    )(pr, pr, pr, mu, w0, w2, a0, a2, g2.astype(BF16), k_k.reshape(1, rw), k_a.reshape(1, rw),
      r_k.reshape(1, rw), _block_diag_ones(rw).astype(BF16))


P_G, P_INV, P_APPLY, P_STATE, P_SEQ = "b1", "b1", "b1", "b1", "b2"


def _rwkv_intra_body(r_ref, v_ref, a_ref, kd_ref, b_ref, lw_ref, rq_out, o0_out, mtx_out, hc_out,
                     *, n_heads, chunks):
    L = RWKV_CHUNK
    d = pl.program_id(0)
    row = lax.broadcasted_iota(jnp.int32, (L, L), 0)
    col = lax.broadcasted_iota(jnp.int32, (L, L), 1)
    fwd = d == 0
    rel = (col - row) * (1 - 2 * d)
    incl = rel <= 0
    strict = rel < 0
    eye = (row == col).astype(F32)
    tri = incl.astype(BF16)
    pairs = []
    for c in range(chunks):
        rows = slice(c * L, (c + 1) * L)
        lw = lw_ref[rows, :]
        cum = _dot_exact_lhs(tri, lw)
        tot = jnp.where(fwd, cum[L - 1:L, :], cum[0:1, :])
        e_neg = jnp.exp(-cum)
        e_end = jnp.exp(tot - cum)
        e_tot = jnp.exp(tot)
        r, v, a, kd, b = r_ref[rows, :], v_ref[rows, :], a_ref[rows, :], kd_ref[rows, :], b_ref[rows, :]
        at, rt, bt, kt = a * jnp.exp(cum - lw), r * jnp.exp(cum), b * e_neg, kd * e_neg
        kend, bend = kd * e_end, b * e_end
        for h in range(n_heads):
            sl = slice(h * HEAD_DIM, (h + 1) * HEAD_DIM)
            pairs.append(dict(at=at[:, sl], rt=rt[:, sl], bt=bt[:, sl], kt=kt[:, sl], v=v[:, sl],
                              kend=kend[:, sl], bend=bend[:, sl], e_tot=e_tot[:, sl]))
    for p in pairs:
        p["g"] = _mm(jnp.concatenate([p["at"], p["rt"]], axis=0),
                     jnp.concatenate([p["bt"], p["kt"]], axis=0), P_G, _dot_nt)
    row2 = lax.broadcasted_iota(jnp.int32, (L, 2 * L), 0)
    col2 = lax.broadcasted_iota(jnp.int32, (L, 2 * L), 1) & (L - 1)
    rel2 = (col2 - row2) * (1 - 2 * d)
    incl2 = rel2 <= 0
    strict2 = rel2 < 0
    zeros = jnp.zeros((L, HEAD_DIM), F32)
    for p in pairs:
        g = p.pop("g")
        a_both = jnp.where(strict2, g[:L, :], 0.0)
        p["m_both"] = jnp.where(incl2, g[L:, :], 0.0)
        p["pw"] = a_both[:, :L]
        p["a_ak"] = a_both[:, L:]
        p["inv"] = eye + p["pw"]
    for _ in range(int(math.log2(L)) - 1):
        for p in pairs:
            p["pw"] = _mm(p["pw"], p["pw"], P_INV)
        for p in pairs:
            p["inv"] = p["inv"] + _mm(p["inv"], p["pw"], P_INV)
    for p in pairs:
        p["akv"] = _mm(p["a_ak"], p["v"], P_APPLY)
    for p in pairs:
        wu = _mm(p["inv"], jnp.concatenate([p["at"], p["akv"]], axis=1), P_APPLY)
        p["rhs"] = jnp.concatenate([wu, jnp.concatenate([zeros, p["v"]], axis=1)], axis=0)
    for p in pairs:
        p["rq_o0"] = _mm(p["m_both"], p["rhs"], P_APPLY)
    for p in pairs:
        p["m_hc"] = _mm(jnp.concatenate([p["bend"], p["kend"]], axis=0), p["rhs"], P_STATE, _dot_tn)
    for c in range(chunks):
        ps = pairs[c * n_heads:(c + 1) * n_heads]
        rows = slice(c * L, (c + 1) * L)
        krows = slice(c * HEAD_DIM, (c + 1) * HEAD_DIM)
        rq_out[rows, :] = jnp.concatenate([p["rt"] + p["rq_o0"][:, :HEAD_DIM] for p in ps], axis=1)
        o0_out[rows, :] = jnp.concatenate([p["rq_o0"][:, HEAD_DIM:] for p in ps], axis=1)
        mtx_out[krows, :] = jnp.concatenate([eye * p["e_tot"] + p["m_hc"][:, :HEAD_DIM] for p in ps], axis=1)
        hc_out[krows, :] = jnp.concatenate([p["m_hc"][:, HEAD_DIM:] for p in ps], axis=1)


def _rwkv_intra(r, v, a, kd, b, lw, chunks=8):
    t, rw = r.shape
    n_heads = rw // HEAD_DIM
    tm = chunks * RWKV_CHUNK
    tk = chunks * HEAD_DIM
    n_tiles = t // tm
    one = pl.BlockSpec((tm, rw), lambda d, i: (i, 0))
    two = pl.BlockSpec((None, tm, rw), lambda d, i: (d, i, 0))
    twok = pl.BlockSpec((None, tk, rw), lambda d, i: (d, i, 0))
    return pl.pallas_call(
        functools.partial(_rwkv_intra_body, n_heads=n_heads, chunks=chunks),
        grid=(2, n_tiles),
        in_specs=[one, one, one, two, two, two],
        out_specs=[two, two, twok, twok],
        out_shape=[jax.ShapeDtypeStruct((2, t, rw), F32), jax.ShapeDtypeStruct((2, t, rw), F32),
                   jax.ShapeDtypeStruct((2, n_tiles * tk, rw), F32),
                   jax.ShapeDtypeStruct((2, n_tiles * tk, rw), F32)],
        compiler_params=_cparams(2),
        name="rwkv_intra",
    )(r, v, a, kd, b, lw)


def _rwkv_seq_body(rq0, o00, mtx0, hc0, rq1, o01, mtx1, hc1, out0, out1, h_ref, *, n_heads, batch, per_step):
    @pl.when(pl.program_id(0) == 0)
    def _():
        h_ref[...] = jnp.zeros_like(h_ref)

    L = RWKV_CHUNK
    dirs = ((rq0, o00, mtx0, hc0, out0), (rq1, o01, mtx1, hc1, out1))
    states = {(d, bi): h_ref[d, bi] for d in range(2) for bi in range(batch)}
    for step in range(per_step):
        for d, (rq, o0, mtx, hc, out) in enumerate(dirs):
            sub = step if d == 0 else per_step - 1 - step
            rows = slice(sub * L, (sub + 1) * L)
            krows = slice(sub * HEAD_DIM, (sub + 1) * HEAD_DIM)
            for bi in range(batch):
                rq_t, mtx_t, state = rq[bi, rows, :], mtx[bi, krows, :], states[(d, bi)]
                outs, new = [], []
                for h in range(n_heads):
                    sl = slice(h * HEAD_DIM, (h + 1) * HEAD_DIM)
                    prod = _mm(jnp.concatenate([rq_t[:, sl], mtx_t[:, sl]], axis=0), state[:, sl], P_SEQ)
                    outs.append(prod[:L])
                    new.append(prod[L:])
                out[bi, rows, :] = jnp.concatenate(outs, axis=1) + o0[bi, rows, :]
                states[(d, bi)] = jnp.concatenate(new, axis=1) + hc[bi, krows, :]
    for (d, bi), state in states.items():
        h_ref[d, bi] = state


def _rwkv_seq(rq, o0, mtx, hc, batch, seq, per_step=4):
    _, t, rw = rq.shape
    n_heads = rw // HEAD_DIM
    L = RWKV_CHUNK * per_step
    lk = HEAD_DIM * per_step
    nc = seq // L
    as4 = lambda x: x.reshape(2, batch, x.shape[1] // batch, rw)
    rq, o0, mtx, hc = as4(rq), as4(o0), as4(mtx), as4(hc)
    fwd = lambda rows: pl.BlockSpec((None, batch, rows, rw), lambda c: (0, 0, c, 0))
    bwd = lambda rows: pl.BlockSpec((None, batch, rows, rw), lambda c: (1, 0, nc - 1 - c, 0))
    out0, out1 = pl.pallas_call(
        functools.partial(_rwkv_seq_body, n_heads=n_heads, batch=batch, per_step=per_step),
        grid=(nc,),
        in_specs=[fwd(L), fwd(L), fwd(lk), fwd(lk), bwd(L), bwd(L), bwd(lk), bwd(lk)],
        out_specs=[pl.BlockSpec((batch, L, rw), lambda c: (0, c, 0)),
                   pl.BlockSpec((batch, L, rw), lambda c: (0, nc - 1 - c, 0))],
        out_shape=[jax.ShapeDtypeStruct((batch, seq, rw), F32)] * 2,
        scratch_shapes=[pltpu.VMEM((2, batch, HEAD_DIM, rw), F32)],
        compiler_params=_cparams(1),
        name="rwkv_seq",
    )(rq, o0, mtx, hc, rq, o0, mtx, hc)
    return out0.reshape(t, rw), out1.reshape(t, rw)


def _rwkv_scan(r, v, a, kd, b, lw, batch, seq):
    rq, o0, mtx, hc = _rwkv_intra(r, v, a, kd, b, lw)
    return _rwkv_seq(rq, o0, mtx, hc, batch, seq)


def _mlstm_prep_body(qk_ref, prev_ref, next_ref, g_ref, cw_ref, cb_ref, gb_ref, q_out, k_out, gate_out,
                     *, mw, n_heads, tiles_per_seq):
    x = qk_ref[...]
    before, after = _neighbours(x, prev_ref[...], next_ref[...], tiles_per_seq)
    y = cb_ref[...] + before * cw_ref[0:1, :] + x * cw_ref[1:2, :] + after * cw_ref[2:3, :]
    y = y * _sigmoid(y)
    q_out[...] = y[:, :mw]
    k_out[...] = y[:, mw:] * (HEAD_DIM ** -0.5)
    g = g_ref[...] + gb_ref[...]
    lane = lax.broadcasted_iota(jnp.int32, g.shape, 1)
    for d in range(2):
        ig = g if d == 0 else pltpu.roll(g, LANE - n_heads, 1)
        fg = pltpu.roll(g, LANE - (1 + d) * n_heads, 1)
        lf = -_softplus(-fg)
        gate_out[d] = jnp.where(lane < n_heads, ig, jnp.where(lane < 2 * n_heads, lf, 0.0))


def _mlstm_prep(pm, seq, conv_w, conv_b, gate_b, mw, tm=512):
    t = pm.shape[0]
    n_heads = mw // HEAD_DIM
    tiles_per_seq = seq // tm
    w2 = 2 * mw
    prev, nxt = _halo_specs(tm, w2, t)
    gcol = (4 * mw) // LANE
    gb = jnp.zeros((1, LANE), F32).at[0, :4 * n_heads].set(gate_b)
    row = pl.BlockSpec((tm, mw), lambda i: (i, 0))
    return pl.pallas_call(
        functools.partial(_mlstm_prep_body, mw=mw, n_heads=n_heads, tiles_per_seq=tiles_per_seq),
        grid=(t // tm,),
        in_specs=[pl.BlockSpec((tm, w2), lambda i: (i, 0)), prev, nxt,
                  pl.BlockSpec((tm, LANE), lambda i: (i, gcol)),
                  _full((3, w2)), _full((1, w2)), _full((1, LANE))],
        out_specs=[row, row, pl.BlockSpec((2, tm, LANE), lambda i: (0, i, 0))],
        out_shape=[jax.ShapeDtypeStruct((t, mw), F32), jax.ShapeDtypeStruct((t, mw), F32),
                   jax.ShapeDtypeStruct((2, t, LANE), F32)],
        compiler_params=_cparams(1),
        name="mlstm_prep",
    )(pm, pm, pm, pm, conv_w, conv_b.reshape(1, w2), gb)


def _mlstm_scan_body(q0_ref, k0_ref, v0_ref, g0_ref, q1_ref, k1_ref, v1_ref, g1_ref, o0_ref, o1_ref,
                     c_ref, m_ref, *, n_heads, group):
    L = MLSTM_CHUNK
    H = n_heads

    @pl.when(pl.program_id(1) == 0)
    def _():
        c_ref[...] = jnp.zeros_like(c_ref)
        m_ref[...] = jnp.zeros_like(m_ref)

    row = lax.broadcasted_iota(jnp.int32, (L, L), 0)
    col = lax.broadcasted_iota(jnp.int32, (L, L), 1)
    trow = lax.broadcasted_iota(jnp.int32, (L, LANE), 0)
    low = lax.broadcasted_iota(jnp.int32, (L, LANE), 1) < HEAD_DIM
    xr = lax.broadcasted_iota(jnp.int32, (LANE, H * L), 0)
    xc = lax.broadcasted_iota(jnp.int32, (LANE, H * L), 1)
    spread = (xr - H == lax.shift_right_logical(xc, int(math.log2(L)))).astype(BF16)
    hs = []
    dirs = ((q0_ref, k0_ref, v0_ref, g0_ref), (q1_ref, k1_ref, v1_ref, g1_ref))
    for bi, d in [(bi, d) for bi in range(group) for d in range(2)]:
        q_ref, k_ref, v_ref, g_ref = dirs[d]
        incl = (col <= row) if d == 0 else (col >= row)
        last = L - 1 if d == 0 else 0
        g = g_ref[bi]
        bcum = _dot_exact_lhs(incl.astype(BF16), g)
        z = pltpu.roll(g, H, 1) - bcum
        cmax = z
        shift = 1
        while shift < L:
            if d == 0:
                moved = jnp.where(trow >= shift, pltpu.roll(cmax, shift, 0), -jnp.inf)
            else:
                moved = jnp.where(trow < L - shift, pltpu.roll(cmax, L - shift, 0), -jnp.inf)
            cmax = jnp.maximum(cmax, moved)
            shift *= 2
        m_prev = m_ref[bi, d, 0:1, :]
        top = jnp.maximum(cmax, m_prev)
        b_last = bcum[last:last + 1, :]
        lwc = b_last + z
        m_new = jnp.maximum(b_last + m_prev, jnp.max(lwc, axis=0, keepdims=True))
        m_ref[bi, d, 0:1, :] = m_new
        alpha_w = _dot_exact_rhs(-top, spread)
        floor_w = jnp.exp(-_dot_exact_rhs(bcum + top, spread, terms=2))
        wts_w = _dot(jnp.exp(lwc - m_new).astype(BF16), spread)
        rows_w = _dot_exact_rhs(jnp.concatenate(
            [jnp.broadcast_to(m_prev, (SUBLANE, LANE)),
             jnp.broadcast_to(jnp.exp(b_last + m_prev - m_new), (SUBLANE, LANE))], axis=0), spread)
        z_t = z.T
        q, k, v = q_ref[bi], k_ref[bi], v_ref[bi]
        for h in range(H):
            slab = slice(h // 2 * LANE, (h // 2 + 1) * LANE)
            cols = slice(h * L, (h + 1) * L)
            mine = low if h % 2 == 0 else jnp.logical_not(low)
            kh = jnp.where(mine, k[:, slab], 0.0)
            hs.append(dict(
                qh=jnp.where(mine, q[:, slab], 0.0).astype(BF16), kh=kh.astype(BF16),
                vext=jnp.where(mine, v[:, slab], 1.0).astype(BF16),
                decay=jnp.exp(jnp.where(incl, alpha_w[:, cols] + z_t[H + h:H + h + 1, :], -jnp.inf)),
                w_inter=jnp.exp(alpha_w[:, cols] + rows_w[0:1, cols]), floor=floor_w[:, cols],
                wk=(wts_w[:, cols] * kh).astype(BF16), dec=rows_w[SUBLANE:SUBLANE + 1, cols],
                cst=c_ref[bi, d, h]))
    for p in hs:
        p["sc"] = (_dot_nt(p["qh"], p["kh"]) * p["decay"]).astype(BF16)
    for p in hs:
        p["numext"] = _dot(p["sc"], p["vext"]) + p["w_inter"] * _dot(p["qh"], p["cst"].astype(BF16))
    for p in hs:
        p["upd"] = _dot_tn(p["wk"], p["vext"])
    for bi, d in [(bi, d) for bi in range(group) for d in range(2)]:
        o_ref = (o0_ref, o1_ref)[d]
        res = []
        for h in range(H):
            p = hs[(bi * 2 + d) * H + h]
            den = pltpu.roll(p["numext"], HEAD_DIM, 1)
            res.append(p["numext"] / jnp.maximum(jnp.abs(den), p["floor"]))
            c_ref[bi, d, h] = p["dec"] * p["cst"] + p["upd"]
        for pair in range(H // 2):
            o_ref[bi, :, pair * LANE:(pair + 1) * LANE] = jnp.where(low, res[2 * pair], res[2 * pair + 1])


def _mlstm_scan(q, k, pm, gates, batch, seq):
    t, mw = q.shape
    n_heads = mw // HEAD_DIM
    L = MLSTM_CHUNK
    nc = seq // L
    group = math.gcd(batch, 4)
    q3, k3, pm3 = (x.reshape(batch, seq, x.shape[1]) for x in (q, k, pm))
    g4 = gates.reshape(2, batch, seq, LANE)
    specs = []
    for d, blk in enumerate((lambda c: c, lambda c: nc - 1 - c)):
        specs += [pl.BlockSpec((group, L, mw), lambda gi, c, blk=blk: (gi, blk(c), 0)),
                  pl.BlockSpec((group, L, mw), lambda gi, c, blk=blk: (gi, blk(c), 0)),
                  pl.BlockSpec((group, L, mw), lambda gi, c, blk=blk: (gi, blk(c), 2)),
                  pl.BlockSpec((None, group, L, LANE), lambda gi, c, blk=blk, d=d: (d, gi, blk(c), 0))]
    out0, out1 = pl.pallas_call(
        functools.partial(_mlstm_scan_body, n_heads=n_heads, group=group),
        grid=(batch // group, nc),
        in_specs=specs,
        out_specs=[pl.BlockSpec((group, L, mw), lambda gi, c: (gi, c, 0)),
                   pl.BlockSpec((group, L, mw), lambda gi, c: (gi, nc - 1 - c, 0))],
        out_shape=[jax.ShapeDtypeStruct((batch, seq, mw), F32)] * 2,
        scratch_shapes=[pltpu.VMEM((group, 2, n_heads, LANE, LANE), F32),
                        pltpu.VMEM((group, 2, SUBLANE, LANE), F32)],
        compiler_params=_cparams(2),
        name="mlstm_scan",
    )(q3, k3, pm3, g4, q3, k3, pm3, g4)
    return out0.reshape(t, mw), out1.reshape(t, mw)


def _layer_norm(x, g, b):
    mu = jnp.mean(x, axis=-1, keepdims=True)
    xc = x - mu
    var = jnp.mean(xc * xc, axis=-1, keepdims=True)
    return xc * lax.rsqrt(var + LN_EPS) * g + b


def _head_norm(x, bd_mean, eps):
    mu = _dot_exact_rhs(x, bd_mean, terms=2)
    xc = x - mu
    var = _dot((xc * xc).astype(BF16), bd_mean)
    return xc * lax.rsqrt(var + eps)


def _mix_out_body(x_ref, yg_ref, ro0_ref, ro1_ref, bonus_ref, rgate_ref, rlg_ref, rlb_ref, mh0_ref, mh1_ref, og_ref,
                  mlg_ref, w_ref, l1g_ref, l1b_ref, rw_ref, rb_ref, bdm_ref,
                  x1_out, x1p_out, topi_out, gate_out, wb_ref, *, alpha, gw, rw):
    @pl.when(pl.program_id(0) == 0)
    def _():
        _cast_rows(w_ref, wb_ref)

    bdm = bdm_ref[...]
    yr = _head_norm(ro0_ref[...] + ro1_ref[...], bdm, RWKV_GN_EPS) * rlg_ref[...] + rlb_ref[...]
    yr = (yr + bonus_ref[...]) * rgate_ref[...]
    ym = _sigmoid(og_ref[...]) * (_head_norm(mh0_ref[...] + mh1_ref[...], bdm, LN_EPS) * mlg_ref[...])
    mix = (_dot(yg_ref[...].astype(BF16), wb_ref[:gw, :]) + _dot(yr.astype(BF16), wb_ref[gw:gw + rw, :])
           + _dot(ym.astype(BF16), wb_ref[gw + rw:, :]))
    x1 = _layer_norm(alpha * x_ref[...] + mix, l1g_ref[...], l1b_ref[...])
    x1_out[...] = x1
    x1p_out[...] = _pack_bf16_pairs(x1)
    lg = _mm(x1, rw_ref[...], "b3") + rb_ref[...]
    lane = lax.broadcasted_iota(jnp.int32, lg.shape, 1)
    vals, topi = [], jnp.zeros(lg.shape, jnp.int32)
    for j in range(TOP_K):
        mx = jnp.max(lg, axis=1, keepdims=True)
        idx = jnp.min(jnp.where(lg == mx, lane, LANE), axis=1, keepdims=True)
        vals.append(mx)
        topi = jnp.where(lane == j, idx, topi)
        lg = jnp.where(lane == idx, -jnp.inf, lg)
    es = [jnp.exp(vj - vals[0]) for vj in vals]
    den = es[0] + es[1] + es[2] + es[3]
    gate = jnp.zeros(lg.shape, F32)
    for j in range(TOP_K):
        gate = jnp.where(lane == j, es[j] / den, gate)
    topi_out[...] = topi.T[:SUBLANE, :]
    gate_out[...] = gate


def _mix_out(x, yg, ro, bonus, rgate, rlg, rlb, mh, pm, mlg, w_out, layer, l1g, l1b, router_w, router_b, alpha,
             tm=512):
    t, dm = x.shape
    gw, rw, mw = yg.shape[1], bonus.shape[1], mh[0].shape[1]
    assert rw == mw
    rwp = jnp.zeros((dm, LANE), F32).at[:, :N_EXPERTS].set(router_w)
    rbp = jnp.full((1, LANE), NEG_BIG, F32).at[0, :N_EXPERTS].set(router_b)
    row = lambda n: pl.BlockSpec((tm, n), lambda i: (i, 0))
    vec = lambda n: _full((1, n))
    return pl.pallas_call(
        functools.partial(_mix_out_body, alpha=alpha, gw=gw, rw=rw),
        grid=(t // tm,),
        in_specs=[row(dm), row(gw), row(rw), row(rw), row(rw), row(rw), vec(rw), vec(rw), row(mw), row(mw),
                  pl.BlockSpec((tm, mw), lambda i: (i, 3)),
                  vec(mw),
                  pl.BlockSpec((None, dm, dm), lambda i: (layer, 0, 0), pipeline_mode=pl.Buffered(1)),
                  vec(dm), vec(dm), _full((dm, LANE)), vec(LANE), _full((rw, rw))],
        out_specs=[row(dm), row(dm // 2), pl.BlockSpec((SUBLANE, tm), lambda i: (0, i)), row(LANE)],
        out_shape=[jax.ShapeDtypeStruct((t, dm), F32), jax.ShapeDtypeStruct((t, dm // 2), jnp.uint32),
                   jax.ShapeDtypeStruct((SUBLANE, t), jnp.int32), jax.ShapeDtypeStruct((t, LANE), F32)],
        scratch_shapes=[pltpu.VMEM((dm, dm), BF16)],
        compiler_params=_cparams(1),
        name="mix_out",
    )(x, yg, ro[0], ro[1], bonus, rgate, rlg.reshape(1, rw), rlb.reshape(1, rw), mh[0], mh[1], pm,
      mlg.reshape(1, mw), w_out, l1g.reshape(1, dm), l1b.reshape(1, dm), rwp, rbp,
      (_block_diag_ones(rw) / HEAD_DIM).astype(BF16))


def _moe_body(be_ref, nu_ref, ve_ref, xs_ref, w1_ref, b1_ref, w2_ref, b2_ref, o_ref, *, dff):
    i = pl.program_id(0)
    active = i < nu_ref[0]

    @pl.when(active)
    def _():
        rowid = i * MOE_BLOCK + lax.broadcasted_iota(jnp.int32, (MOE_BLOCK, 1), 0)
        lo, hi = _unpack_bf16_pairs(jnp.where(rowid < ve_ref[i], xs_ref[...], jnp.uint32(0)))
        xs = jnp.concatenate([lo.astype(BF16), hi.astype(BF16)], axis=1)
        hdn = _dot(xs, w1_ref[...].astype(BF16)) + b1_ref[...]
        g_ = jnp.minimum(hdn[:, :dff], SWIGLU_LIMIT)
        u_ = jnp.clip(hdn[:, dff:], -SWIGLU_LIMIT, SWIGLU_LIMIT)
        act = (u_ + 1.0) * (g_ * _sigmoid(g_ * SWIGLU_ALPHA))
        o_ref[...] = _pack_bf16_pairs(_dot(act.astype(BF16), w2_ref[...].astype(BF16)) + b2_ref[...])

    @pl.when(jnp.logical_not(active))
    def _():
        o_ref[...] = jnp.zeros_like(o_ref)


def _moe_experts(xs, block_e, n_used, valid_end, w1, b1, w2, b2, layer):
    rows, half = xs.shape
    nb = rows // MOE_BLOCK
    depth, ne, dm, dff2 = w1.shape
    dff = dff2 // 2
    grid_spec = pltpu.PrefetchScalarGridSpec(
        num_scalar_prefetch=3,
        grid=(nb,),
        in_specs=[pl.BlockSpec((MOE_BLOCK, half), lambda i, be, nu, ve: (i, 0)),
                  pl.BlockSpec((None, None, dm, dff2), lambda i, be, nu, ve: (layer, be[i], 0, 0)),
                  pl.BlockSpec((None, None, 1, dff2), lambda i, be, nu, ve: (layer, be[i], 0, 0)),
                  pl.BlockSpec((None, None, dff, dm), lambda i, be, nu, ve: (layer, be[i], 0, 0)),
                  pl.BlockSpec((None, None, 1, dm), lambda i, be, nu, ve: (layer, be[i], 0, 0))],
        out_specs=pl.BlockSpec((MOE_BLOCK, half), lambda i, be, nu, ve: (i, 0)),
    )
    return pl.pallas_call(
        functools.partial(_moe_body, dff=dff),
        grid_spec=grid_spec,
        out_shape=jax.ShapeDtypeStruct((rows, half), jnp.uint32),
        compiler_params=_cparams(1),
        name="moe_experts",
    )(block_e, n_used, valid_end, xs, w1, b1.reshape(depth, ne, 1, dff2), w2, b2.reshape(depth, ne, 1, dm))


N_STREAMS = 1
PLAN_TILE = 512
MOE_BLOCK_SHIFT = MOE_BLOCK.bit_length() - 1
assert 1 << MOE_BLOCK_SHIFT == MOE_BLOCK


def _moe_plan_body(e_ref, dest_ref, meta_ref, rank_ref, *, n_tokens, meta_lanes):
    tiles_per_row = n_tokens // PLAN_TILE
    n_tiles = TOP_K * tiles_per_row
    expert = lax.broadcasted_iota(jnp.int32, (N_EXPERTS, PLAN_TILE), 0)
    r_i = lax.broadcasted_iota(jnp.int32, (PLAN_TILE, PLAN_TILE), 0)
    c_i = lax.broadcasted_iota(jnp.int32, (PLAN_TILE, PLAN_TILE), 1)
    earlier = (r_i < c_i).astype(BF16)

    def tile_hits(it):
        j = it // tiles_per_row
        lanes = pl.ds(pl.multiple_of((it % tiles_per_row) * PLAN_TILE, PLAN_TILE), PLAN_TILE)
        return j, lanes, e_ref[pl.ds(j, 1), lanes] == expert

    def rank_step(it, seen):
        j, lanes, hit = tile_hits(it)
        hitf = hit.astype(F32)
        prior = _dot(hit.astype(BF16), earlier) + seen
        rank_ref[pl.ds(j, 1), lanes] = jnp.sum(hitf * prior, axis=0, keepdims=True)
        return seen + jnp.sum(hitf, axis=1, keepdims=True)

    dest_ref[...] = jnp.zeros_like(dest_ref)
    rank_ref[...] = jnp.zeros_like(rank_ref)
    counts = lax.fori_loop(0, n_tiles, rank_step, jnp.zeros((N_EXPERTS, 1), F32))
    padded = ((counts.astype(jnp.int32) + (MOE_BLOCK - 1)) >> MOE_BLOCK_SHIFT) << MOE_BLOCK_SHIFT
    er = lax.broadcasted_iota(jnp.int32, (N_EXPERTS, N_EXPERTS), 0)
    ec = lax.broadcasted_iota(jnp.int32, (N_EXPERTS, N_EXPERTS), 1)
    seg_end = _dot_exact_lhs((ec <= er).astype(BF16),
                             jnp.broadcast_to(padded.astype(F32), (N_EXPERTS, LANE)))[:, 0:1]
    seg_start = seg_end - padded.astype(F32)

    def dest_step(it, carry):
        j, lanes, hit = tile_hits(it)
        base = jnp.sum(jnp.where(hit, seg_start, 0.0), axis=0, keepdims=True)
        dest_ref[pl.ds(j, 1), lanes] = (rank_ref[pl.ds(j, 1), lanes] + base).astype(jnp.int32)
        return carry

    lax.fori_loop(0, n_tiles, dest_step, 0)
    blk_start = (lax.broadcasted_iota(jnp.int32, (N_EXPERTS, meta_lanes), 1) * MOE_BLOCK).astype(F32)
    blk_expert = jnp.minimum(jnp.sum((seg_end <= blk_start).astype(F32), axis=0, keepdims=True), N_EXPERTS - 1.0)
    mine = lax.broadcasted_iota(jnp.int32, (N_EXPERTS, meta_lanes), 0).astype(F32) == blk_expert
    valid_end = jnp.sum(jnp.where(mine, seg_start + counts, 0.0), axis=0, keepdims=True)
    n_used = jnp.broadcast_to(seg_end[N_EXPERTS - 1:N_EXPERTS, :] * (1.0 / MOE_BLOCK), (1, meta_lanes))
    mrow = lax.broadcasted_iota(jnp.int32, (SUBLANE, meta_lanes), 0)
    meta = jnp.where(mrow == 0, blk_expert, jnp.where(mrow == 1, valid_end, jnp.where(mrow == 2, n_used, 0.0)))
    meta_ref[...] = meta.astype(jnp.int32)


def _moe_plan(e_t, n_tokens, n_blocks):
    meta_lanes = -(-n_blocks // LANE) * LANE
    dest, meta = pl.pallas_call(
        functools.partial(_moe_plan_body, n_tokens=n_tokens, meta_lanes=meta_lanes),
        grid=(1,),
        in_specs=[_full((SUBLANE, n_tokens))],
        out_specs=[_full((SUBLANE, n_tokens)), _full((SUBLANE, meta_lanes))],
        out_shape=[jax.ShapeDtypeStruct((SUBLANE, n_tokens), jnp.int32),
                   jax.ShapeDtypeStruct((SUBLANE, meta_lanes), jnp.int32)],
        scratch_shapes=[pltpu.VMEM((SUBLANE, n_tokens), F32)],
        compiler_params=_cparams(1),
        name="moe_plan",
    )(e_t)
    return dest[:TOP_K], meta[0, :n_blocks], meta[1, :n_blocks], meta[2, :1]


def _combine_body(x1_ref, y0_ref, y1_ref, y2_ref, y3_ref, gate_ref, g_ref, b_ref, o_ref, *, alpha):
    gate = gate_ref[...]
    lo, hi = _unpack_bf16_pairs(y0_ref[...])
    lo, hi = gate[:, 0:1] * lo, gate[:, 0:1] * hi
    for j, y_ref in enumerate((y1_ref, y2_ref, y3_ref), start=1):
        lo_j, hi_j = _unpack_bf16_pairs(y_ref[...])
        lo, hi = lo + gate[:, j:j + 1] * lo_j, hi + gate[:, j:j + 1] * hi_j
    ffn = jnp.concatenate([lo, hi], axis=1)
    o_ref[...] = _layer_norm(alpha * x1_ref[...] + ffn, g_ref[...], b_ref[...])


def _combine(x1, yg, gate, ln_g, ln_b, alpha, tm=512):
    t, dm = x1.shape
    n_tiles = t // tm
    expert_rows = lambda j: pl.BlockSpec((tm, dm // 2), lambda i: (i + j * n_tiles, 0))
    return pl.pallas_call(
        functools.partial(_combine_body, alpha=alpha),
        grid=(n_tiles,),
        in_specs=[pl.BlockSpec((tm, dm), lambda i: (i, 0))] + [expert_rows(j) for j in range(TOP_K)]
                 + [pl.BlockSpec((tm, LANE), lambda i: (i, 0)), _full((1, dm)), _full((1, dm))],
        out_specs=pl.BlockSpec((tm, dm), lambda i: (i, 0)),
        out_shape=jax.ShapeDtypeStruct((t, dm), F32),
        compiler_params=_cparams(1),
        name="combine_ln",
    )(x1, yg, yg, yg, yg, gate, ln_g.reshape(1, dm), ln_b.reshape(1, dm))


SC_CORES = 2
SC_SUBCORES = 16
SC_WORKERS = SC_CORES * SC_SUBCORES


def _sc_gather_rows(table, idx, window):
    n = idx.shape[0]
    dim = table.shape[1]
    n_steps = n // (SC_WORKERS * window)
    assert n_steps * window * SC_WORKERS == n and n_steps % 2 == 0 and window % SUBLANE == 0 and window <= LANE
    idx3 = idx.reshape(SC_WORKERS, n_steps, window)
    mesh = plsc.VectorSubcoreMesh(core_axis_name="c", subcore_axis_name="s",
                                  num_cores=SC_CORES, num_subcores=SC_SUBCORES)

    def body(table_hbm, idx_hbm, out_hbm, idx_v, rows_v, gsem, wsem):
        wid = lax.axis_index("s") * SC_CORES + lax.axis_index("c")
        pltpu.sync_copy(idx_hbm.at[wid], idx_v)

        def gather(j, buf):
            return pltpu.make_async_copy(table_hbm.at[idx_v.at[j]], rows_v.at[buf], gsem.at[buf])

        def write(j, buf):
            base = pl.multiple_of((wid * n_steps + j) * window, window)
            return pltpu.make_async_copy(rows_v.at[buf], out_hbm.at[pl.ds(base, window)], wsem.at[buf])

        gather(0, 0).start()

        @pl.loop(0, n_steps, step=2)
        def _(j0):
            for buf in range(2):
                j = j0 + buf
                gather(j, buf).wait()

                @pl.when(j >= 1)
                def _():
                    write(j - 1, 1 - buf).wait()

                @pl.when(j + 1 < n_steps)
                def _():
                    gather(j + 1, 1 - buf).start()

                write(j, buf).start()

        write(n_steps - 1, 1).wait()

    return pl.kernel(
        body, out_type=jax.ShapeDtypeStruct((n, dim), table.dtype), mesh=mesh,
        scratch_types=[pltpu.VMEM((n_steps, window), jnp.int32), pltpu.VMEM((2, window, dim), table.dtype),
                       pltpu.SemaphoreType.DMA((2,)), pltpu.SemaphoreType.DMA((2,))],
        name="sc_gather",
    )(table, idx3)


def _sc_scatter_rows(src, dest, n_out, window):
    t, dim = src.shape
    k = dest.shape[0]
    n_steps = t // (SC_WORKERS * window)
    assert n_steps * window * SC_WORKERS == t and n_steps % 2 == 0 and window % SUBLANE == 0 and window <= LANE
    idx3 = dest.reshape(k, SC_WORKERS, n_steps, window).transpose(1, 2, 0, 3).reshape(SC_WORKERS, n_steps * k, window)
    mesh = plsc.VectorSubcoreMesh(core_axis_name="c", subcore_axis_name="s",
                                  num_cores=SC_CORES, num_subcores=SC_SUBCORES)

    def body(src_hbm, idx_hbm, out_hbm, idx_v, rows_v, rsem, ssem):
        wid = lax.axis_index("s") * SC_CORES + lax.axis_index("c")
        pltpu.sync_copy(idx_hbm.at[wid], idx_v)

        def read(s, buf):
            base = pl.multiple_of((wid * n_steps + s) * window, window)
            return pltpu.make_async_copy(src_hbm.at[pl.ds(base, window)], rows_v.at[buf], rsem.at[buf])

        def scatter(s, j, buf):
            return pltpu.make_async_copy(rows_v.at[buf], out_hbm.at[idx_v.at[s * k + j]], ssem.at[buf])

        read(0, 0).start()

        @pl.loop(0, n_steps, step=2)
        def _(s0):
            for buf in range(2):
                s = s0 + buf
                read(s, buf).wait()

                @pl.when(s >= 1)
                def _():
                    for j in range(k):
                        scatter(s - 1, j, 1 - buf).wait()

                @pl.when(s + 1 < n_steps)
                def _():
                    read(s + 1, 1 - buf).start()

                for j in range(k):
                    scatter(s, j, buf).start()

        for j in range(k):
            scatter(n_steps - 1, j, 1).wait()

    return pl.kernel(
        body, out_type=jax.ShapeDtypeStruct((n_out, dim), src.dtype), mesh=mesh,
        scratch_types=[pltpu.VMEM((n_steps * k, window), jnp.int32), pltpu.VMEM((2, window, dim), src.dtype),
                       pltpu.SemaphoreType.DMA((2,)), pltpu.SemaphoreType.DMA((2,))],
        name="sc_scatter",
    )(src, idx3)


def _pad_cols(w, width):
    return jnp.pad(w, ((0, 0), (0, width - w.shape[1])))


def kernel(x, w_in, gmlp_ln_g, gmlp_ln_b, gmlp_ws, gmlp_bs, rwkv_mu, rwkv_w0, rwkv_w2, rwkv_a0, rwkv_a2, rwkv_g2, rwkv_k_k, rwkv_k_a, rwkv_r_k, rwkv_ln_g, rwkv_ln_b, mlstm_conv_w, mlstm_conv_b, mlstm_gate_b, mlstm_ln_g, w_out, ln1_g, ln1_b, router_w, router_b, exp_w1, exp_b1, exp_w2, exp_b2, ln2_g, ln2_b):
    batch, seq, dm = x.shape
    depth = w_in.shape[0]
    sb = batch // N_STREAMS if batch % N_STREAMS == 0 else batch
    t = sb * seq
    gw = gmlp_ln_g.shape[1]
    rw = rwkv_w0.shape[2]
    mw = mlstm_ln_g.shape[1]
    g_proj = 2 * gw
    r_proj = 3 * rw + W_LORA + A_LORA + G_LORA
    alpha = (2 * depth) ** 0.25
    n_blocks = -(-t * TOP_K // MOE_BLOCK) + N_EXPERTS
    streams = [x[i * sb:(i + 1) * sb].reshape(t, dm) for i in range(batch // sb)]
    for l in range(depth):
        mixed = []
        for xf in streams:
            y_g, pr, pm = _proj(xf, w_in, l, g_proj, r_proj, gmlp_ln_g[l], gmlp_ln_b[l], gmlp_ws[l], gmlp_bs[l])
            r, v, a, kd, b, lw, bonus, rgate = _rwkv_prep(
                pr, seq, rwkv_mu[l], rwkv_w0[l], rwkv_w2[l], rwkv_a0[l], rwkv_a2[l], rwkv_g2[l],
                rwkv_k_k[l], rwkv_k_a[l], rwkv_r_k[l].reshape(-1))
            ro = _rwkv_scan(r, v, a, kd, b, lw, sb, seq)
            q, k, gates = _mlstm_prep(pm, seq, mlstm_conv_w[l], mlstm_conv_b[l], mlstm_gate_b[l], mw)
            mh = _mlstm_scan(q, k, pm, gates, sb, seq)
            mixed.append(_mix_out(xf, y_g, ro, bonus, rgate, rwkv_ln_g[l], rwkv_ln_b[l], mh, pm, mlstm_ln_g[l],
                                  w_out, l, ln1_g[l], ln1_b[l], router_w[l], router_b[l], alpha))
        streams = []
        for x1, x1p, topi, gate in mixed:
            dest, block_e, valid_end, n_used = _moe_plan(topi, t, n_blocks)
            xs = _sc_scatter_rows(x1p, dest, n_blocks * MOE_BLOCK, window=64)
            ys = _moe_experts(xs, block_e, n_used, valid_end, exp_w1, exp_b1, exp_w2, exp_b2, l)
            yg = _sc_gather_rows(ys, dest.reshape(-1), window=64)
            streams.append(_combine(x1, yg, gate, ln2_g[l], ln2_b[l], alpha))
    return jnp.concatenate(streams, axis=0).reshape(batch, seq, dm)
```

```python
import functools
import math

import jax
import jax.numpy as jnp
from jax import lax
from jax.experimental import pallas as pl
from jax.experimental.pallas import tpu as pltpu
from jax.experimental.pallas import tpu_sc as plsc

F32 = jnp.float32
BF16 = jnp.bfloat16

HEAD_DIM = 64
GMLP_CHUNK = 128
MLSTM_CHUNK = 128
RWKV_CHUNK = 64
W_LORA = 64
A_LORA = 64
G_LORA = 128
N_EXPERTS = 32
TOP_K = 4
MOE_BLOCK = 512
SWIGLU_LIMIT = 7.0
SWIGLU_ALPHA = 1.702
LN_EPS = 1e-5
RWKV_GN_EPS = 64e-5
LANE = 128
SUBLANE = 8
VMEM_LIMIT = 48 * 1024 * 1024
NEG_BIG = -1e30


def _cparams(n_axes):
    return pltpu.CompilerParams(dimension_semantics=("arbitrary",) * n_axes,
                                vmem_limit_bytes=VMEM_LIMIT)


def _full(shape):
    return pl.BlockSpec(shape, lambda *_: (0,) * len(shape))


def _dot(a, b, precision=None):
    return jnp.dot(a, b, preferred_element_type=F32, precision=precision)


def _dot_nt(a, b, precision=None):
    return lax.dot_general(a, b, (((1,), (1,)), ((), ())), preferred_element_type=F32, precision=precision)


def _dot_tn(a, b, precision=None):
    return lax.dot_general(a, b, (((0,), (0,)), ((), ())), preferred_element_type=F32, precision=precision)


def _split(x):
    hi = x.astype(BF16)
    return hi, (x - hi.astype(F32)).astype(BF16)


def _split3(x):
    hi = x.astype(BF16)
    r1 = x - hi.astype(F32)
    mid = r1.astype(BF16)
    return hi, mid, (r1 - mid.astype(F32)).astype(BF16)


def _mm(a, b, mode, dot=_dot):
    if mode == "b1":
        return dot(a.astype(BF16), b.astype(BF16))
    bh, bl = _split(b)
    if mode == "b2":
        ah = a.astype(BF16)
        return dot(ah, bh) + dot(ah, bl)
    ah, al = _split(a)
    return dot(ah, bh) + (dot(ah, bl) + dot(al, bh))


def _dot_exact_lhs(a_bf16, x):
    hi, mid, lo = _split3(x)
    return _dot(a_bf16, hi) + (_dot(a_bf16, mid) + _dot(a_bf16, lo))


def _dot_exact_rhs(x, b_bf16, terms=3):
    if terms == 2:
        hi, lo = _split(x)
        return _dot(hi, b_bf16) + _dot(lo, b_bf16)
    hi, mid, lo = _split3(x)
    return _dot(hi, b_bf16) + (_dot(mid, b_bf16) + _dot(lo, b_bf16))


def _pack_bf16_pairs(x):
    n = x.shape[1] // 2
    lo = pltpu.bitcast(x[:, :n].astype(BF16).astype(F32), jnp.uint32)
    hi = pltpu.bitcast(x[:, n:].astype(BF16).astype(F32), jnp.uint32)
    return hi | (lo >> 16)


def _unpack_bf16_pairs(w):
    lo = pltpu.bitcast(w << 16, F32)
    hi = pltpu.bitcast(w & jnp.uint32(0xFFFF0000), F32)
    return lo, hi


def _sigmoid(x):
    return 1.0 / (1.0 + jnp.exp(-x))


def _softplus(x):
    return jnp.maximum(x, 0.0) + jnp.log1p(jnp.exp(-jnp.abs(x)))


def _block_diag_ones(width):
    h = jnp.arange(width) // HEAD_DIM
    return (h[:, None] == h[None, :]).astype(F32)


CAST_ROWS = 128


def _cast_rows(src_ref, dst_ref):
    n_src, n_dst = src_ref.shape[1], dst_ref.shape[1]
    whole = n_src // LANE * LANE

    def step(r, carry):
        rows = pl.ds(pl.multiple_of(r * CAST_ROWS, CAST_ROWS), CAST_ROWS)
        dst_ref[rows, :whole] = src_ref[rows, :whole].astype(BF16)
        if n_dst > whole:
            tail = [src_ref[rows, whole:]] if n_src > whole else []
            tail.append(jnp.zeros((CAST_ROWS, n_dst - n_src), F32))
            dst_ref[rows, whole:] = jnp.concatenate(tail, axis=1).astype(BF16)
        return carry
    lax.fori_loop(0, src_ref.shape[0] // CAST_ROWS, step, 0)


def _gmlp_gate(p, lng_ref, lnb_ref, ws_ref, bst_ref, o_ref):
    gw = p.shape[1] // 2
    p = 0.5 * p * (1.0 + lax.erf(p * math.sqrt(0.5)))
    u, v = p[:, :gw], p[:, gw:]
    mu = jnp.mean(v, axis=-1, keepdims=True)
    vc = v - mu
    var = jnp.mean(vc * vc, axis=-1, keepdims=True)
    vn = vc * lax.rsqrt(var + LN_EPS) * lng_ref[...] + lnb_ref[...]
    for c in range(p.shape[0] // GMLP_CHUNK):
        rows = slice(c * GMLP_CHUNK, (c + 1) * GMLP_CHUNK)
        ys = []
        for h in range(gw // HEAD_DIM):
            cols = slice(h * HEAD_DIM, (h + 1) * HEAD_DIM)
            ys.append(_dot(ws_ref[h], vn[rows, cols].astype(BF16)) + bst_ref[:, h:h + 1])
        o_ref[rows, :] = u[rows, :] * jnp.concatenate(ys, axis=1)


def _proj_body(x_ref, w_ref, lng_ref, lnb_ref, ws_ref, bst_ref, yg_ref, pr_ref, pm_ref, wb_ref, *, ng, nr):
    @pl.when(pl.program_id(0) == 0)
    def _():
        _cast_rows(w_ref, wb_ref)

    xb = x_ref[...].astype(BF16)
    _gmlp_gate(_dot(xb, wb_ref[:, :ng]), lng_ref, lnb_ref, ws_ref, bst_ref, yg_ref)
    pr_ref[...] = _dot(xb, wb_ref[:, ng:ng + nr])
    pm_ref[...] = _dot(xb, wb_ref[:, ng + nr:])


def _proj(x, w_in, layer, ng, nr, ln_g, ln_b, ws, bs, tm=512):
    t, d = x.shape
    p_in = w_in.shape[2]
    p_pad = -(-p_in // LANE) * LANE
    nm = p_pad - ng - nr
    gw = ng // 2
    n_heads = gw // HEAD_DIM
    bst = jnp.zeros((GMLP_CHUNK, LANE), F32).at[:, :n_heads].set(bs.T)
    row = lambda n: pl.BlockSpec((tm, n), lambda i: (i, 0))
    return pl.pallas_call(
        functools.partial(_proj_body, ng=ng, nr=nr),
        grid=(t // tm,),
        in_specs=[row(d), pl.BlockSpec((None, d, p_in), lambda i: (layer, 0, 0), pipeline_mode=pl.Buffered(1)),
                  _full((1, gw)), _full((1, gw)), _full((n_heads, GMLP_CHUNK, GMLP_CHUNK)),
                  _full((GMLP_CHUNK, LANE))],
        out_specs=[row(gw), row(nr), row(nm)],
        out_shape=[jax.ShapeDtypeStruct((t, n), F32) for n in (gw, nr, nm)],
        scratch_shapes=[pltpu.VMEM((d, p_pad), BF16)],
        compiler_params=_cparams(1),
        name="in_proj",
    )(x, w_in, ln_g.reshape(1, gw), ln_b.reshape(1, gw), ws.astype(BF16), bst)


def _halo_specs(tm, width, n_rows):
    per8 = tm // SUBLANE
    last = n_rows // SUBLANE - 1
    prev = pl.BlockSpec((SUBLANE, width), lambda i: (jnp.maximum(i * per8 - 1, 0), 0))
    nxt = pl.BlockSpec((SUBLANE, width), lambda i: (jnp.minimum((i + 1) * per8, last), 0))
    return prev, nxt


def _neighbours(cur, prev_blk, next_blk, tiles_per_seq):
    tm = cur.shape[0]
    j = pl.program_id(0) % tiles_per_seq
    prev_row = jnp.where(j > 0, prev_blk[SUBLANE - 1:SUBLANE, :], 0.0)
    next_row = jnp.where(j < tiles_per_seq - 1, next_blk[0:1, :], 0.0)
    ridx = lax.broadcasted_iota(jnp.int32, cur.shape, 0)
    before = jnp.where(ridx == 0, prev_row, pltpu.roll(cur, 1, 0))
    after = jnp.where(ridx == tm - 1, next_row, pltpu.roll(cur, tm - 1, 0))
    return before, after


def _rwkv_prep_body(pr_ref, prev_ref, next_ref, mu_ref, w0_ref, w2_ref, a0_ref, a2_ref, g2_ref,
                    kk_ref, ka_ref, rk_ref, bd_ref,
                    r_out, v_out, a_out, kd_out, b_out, lw_out, bonus_out, gate_out, *, rw, tiles_per_seq):
    pf = pr_ref[...]
    before, after = _neighbours(pf, prev_ref[...], next_ref[...], tiles_per_seq)
    pf = pf + mu_ref[0:1, :] * (before - pf) + mu_ref[1:2, :] * (after - pf)
    o3 = 3 * rw
    r, k, v = pf[:, :rw], pf[:, rw:2 * rw], pf[:, 2 * rw:o3]
    wd = pf[:, o3:o3 + W_LORA]
    ad = pf[:, o3 + W_LORA:o3 + W_LORA + A_LORA]
    gd = pf[:, o3 + W_LORA + A_LORA:]
    bd = bd_ref[...]
    kk = k * kk_ref[...]
    ss = _dot((kk * kk).astype(BF16), bd)
    kk = kk / jnp.maximum(jnp.sqrt(ss), 1e-12)
    twd = jnp.tanh(wd)
    ksum = jnp.zeros_like(k)
    for d in range(2):
        w_log = -_softplus(-(w0_ref[d:d + 1, :] + _mm(twd, w2_ref[d], "b3"))) - 0.5
        lw_out[d] = -jnp.exp(w_log)
        iclr = _sigmoid(a0_ref[d:d + 1, :] + _mm(ad, a2_ref[d], "b3"))
        kd = k * (1.0 + (iclr - 1.0) * ka_ref[...])
        kd_out[d] = kd
        b_out[d] = kk * iclr
        ksum = ksum + kd
    r_out[...] = r
    v_out[...] = v
    a_out[...] = -kk
    bonus_out[...] = _dot((r * ksum * rk_ref[...]).astype(BF16), bd) * v
    gate_out[...] = _dot(_sigmoid(gd).astype(BF16), g2_ref[...])


def _rwkv_prep(pr, seq, mu, w0, w2, a0, a2, g2, k_k, k_a, r_k, tm=512):
    t, rproj = pr.shape
    rw = w0.shape[1]
    tiles_per_seq = seq // tm
    prev, nxt = _halo_specs(tm, rproj, t)
    row = pl.BlockSpec((tm, rw), lambda i: (i, 0))
    row2 = pl.BlockSpec((2, tm, rw), lambda i: (0, i, 0))
    one = jax.ShapeDtypeStruct((t, rw), F32)
    two = jax.ShapeDtypeStruct((2, t, rw), F32)
    return pl.pallas_call(
        functools.partial(_rwkv_prep_body, rw=rw, tiles_per_seq=tiles_per_seq),
        grid=(t // tm,),
        in_specs=[pl.BlockSpec((tm, rproj), lambda i: (i, 0)), prev, nxt,
                  _full((2, rproj)), _full((2, rw)), _full((2, W_LORA, rw)), _full((2, rw)),
                  _full((2, A_LORA, rw)), _full((G_LORA, rw)), _full((1, rw)), _full((1, rw)),
                  _full((1, rw)), _full((rw, rw))],
        out_specs=[row, row, row, row2, row2, row2, row, row],
        out_shape=[one, one, one, two, two, two, one, one],
        compiler_params=_cparams(1),
        name="rwkv_prep",
    )(pr, pr, pr, mu, w0, w2, a0, a2, g2.astype(BF16), k_k.reshape(1, rw), k_a.reshape(1, rw),
      r_k.reshape(1, rw), _block_diag_ones(rw).astype(BF16))


P_G, P_INV, P_APPLY, P_STATE, P_SEQ = "b1", "b1", "b1", "b1", "b2"


def _rwkv_intra_body(r_ref, v_ref, a_ref, kd_ref, b_ref, lw_ref, rq_out, o0_out, mtx_out, hc_out,
                     *, n_heads, chunks):
    L = RWKV_CHUNK
    d = pl.program_id(0)
    row = lax.broadcasted_iota(jnp.int32, (L, L), 0)
    col = lax.broadcasted_iota(jnp.int32, (L, L), 1)
    fwd = d == 0
    rel = (col - row) * (1 - 2 * d)
    incl = rel <= 0
    strict = rel < 0
    eye = (row == col).astype(F32)
    tri = incl.astype(BF16)
    pairs = []
    for c in range(chunks):
        rows = slice(c * L, (c + 1) * L)
        lw = lw_ref[rows, :]
        cum = _dot_exact_lhs(tri, lw)
        tot = jnp.where(fwd, cum[L - 1:L, :], cum[0:1, :])
        e_neg = jnp.exp(-cum)
        e_end = jnp.exp(tot - cum)
        e_tot = jnp.exp(tot)
        r, v, a, kd, b = r_ref[rows, :], v_ref[rows, :], a_ref[rows, :], kd_ref[rows, :], b_ref[rows, :]
        at, rt, bt, kt = a * jnp.exp(cum - lw), r * jnp.exp(cum), b * e_neg, kd * e_neg
        kend, bend = kd * e_end, b * e_end
        for h in range(n_heads):
            sl = slice(h * HEAD_DIM, (h + 1) * HEAD_DIM)
            pairs.append(dict(at=at[:, sl], rt=rt[:, sl], bt=bt[:, sl], kt=kt[:, sl], v=v[:, sl],
                              kend=kend[:, sl], bend=bend[:, sl], e_tot=e_tot[:, sl]))
    for p in pairs:
        p["g"] = _mm(jnp.concatenate([p["at"], p["rt"]], axis=0),
                     jnp.concatenate([p["bt"], p["kt"]], axis=0), P_G, _dot_nt)
    row2 = lax.broadcasted_iota(jnp.int32, (L, 2 * L), 0)
    col2 = lax.broadcasted_iota(jnp.int32, (L, 2 * L), 1) & (L - 1)
    rel2 = (col2 - row2) * (1 - 2 * d)
    incl2 = rel2 <= 0
    strict2 = rel2 < 0
    zeros = jnp.zeros((L, HEAD_DIM), F32)
    for p in pairs:
        g = p.pop("g")
        a_both = jnp.where(strict2, g[:L, :], 0.0)
        p["m_both"] = jnp.where(incl2, g[L:, :], 0.0)
        p["pw"] = a_both[:, :L]
        p["a_ak"] = a_both[:, L:]
        p["inv"] = eye + p["pw"]
    for _ in range(int(math.log2(L)) - 1):
        for p in pairs:
            p["pw"] = _mm(p["pw"], p["pw"], P_INV)
        for p in pairs:
            p["inv"] = p["inv"] + _mm(p["inv"], p["pw"], P_INV)
    for p in pairs:
        p["akv"] = _mm(p["a_ak"], p["v"], P_APPLY)
    for p in pairs:
        wu = _mm(p["inv"], jnp.concatenate([p["at"], p["akv"]], axis=1), P_APPLY)
        p["rhs"] = jnp.concatenate([wu, jnp.concatenate([zeros, p["v"]], axis=1)], axis=0)
    for p in pairs:
        p["rq_o0"] = _mm(p["m_both"], p["rhs"], P_APPLY)
    for p in pairs:
        p["m_hc"] = _mm(jnp.concatenate([p["bend"], p["kend"]], axis=0), p["rhs"], P_STATE, _dot_tn)
    for c in range(chunks):
        ps = pairs[c * n_heads:(c + 1) * n_heads]
        rows = slice(c * L, (c + 1) * L)
        krows = slice(c * HEAD_DIM, (c + 1) * HEAD_DIM)
        rq_out[rows, :] = jnp.concatenate([p["rt"] + p["rq_o0"][:, :HEAD_DIM] for p in ps], axis=1)
        o0_out[rows, :] = jnp.concatenate([p["rq_o0"][:, HEAD_DIM:] for p in ps], axis=1)
        mtx_out[krows, :] = jnp.concatenate([eye * p["e_tot"] + p["m_hc"][:, :HEAD_DIM] for p in ps], axis=1)
        hc_out[krows, :] = jnp.concatenate([p["m_hc"][:, HEAD_DIM:] for p in ps], axis=1)


def _rwkv_intra(r, v, a, kd, b, lw, chunks=8):
    t, rw = r.shape
    n_heads = rw // HEAD_DIM
    tm = chunks * RWKV_CHUNK
    tk = chunks * HEAD_DIM
    n_tiles = t // tm
    one = pl.BlockSpec((tm, rw), lambda d, i: (i, 0))
    two = pl.BlockSpec((None, tm, rw), lambda d, i: (d, i, 0))
    twok = pl.BlockSpec((None, tk, rw), lambda d, i: (d, i, 0))
    return pl.pallas_call(
        functools.partial(_rwkv_intra_body, n_heads=n_heads, chunks=chunks),
        grid=(2, n_tiles),
        in_specs=[one, one, one, two, two, two],
        out_specs=[two, two, twok, twok],
        out_shape=[jax.ShapeDtypeStruct((2, t, rw), F32), jax.ShapeDtypeStruct((2, t, rw), F32),
                   jax.ShapeDtypeStruct((2, n_tiles * tk, rw), F32),
                   jax.ShapeDtypeStruct((2, n_tiles * tk, rw), F32)],
        compiler_params=_cparams(2),
        name="rwkv_intra",
    )(r, v, a, kd, b, lw)


def _rwkv_seq_body(rq0, o00, mtx0, hc0, rq1, o01, mtx1, hc1, out0, out1, h_ref, *, n_heads, batch, per_step):
    @pl.when(pl.program_id(0) == 0)
    def _():
        h_ref[...] = jnp.zeros_like(h_ref)

    L = RWKV_CHUNK
    dirs = ((rq0, o00, mtx0, hc0, out0), (rq1, o01, mtx1, hc1, out1))
    states = {(d, bi): h_ref[d, bi] for d in range(2) for bi in range(batch)}
    for step in range(per_step):
        for d, (rq, o0, mtx, hc, out) in enumerate(dirs):
            sub = step if d == 0 else per_step - 1 - step
            rows = slice(sub * L, (sub + 1) * L)
            krows = slice(sub * HEAD_DIM, (sub + 1) * HEAD_DIM)
            for bi in range(batch):
                rq_t, mtx_t, state = rq[bi, rows, :], mtx[bi, krows, :], states[(d, bi)]
                outs, new = [], []
                for h in range(n_heads):
                    sl = slice(h * HEAD_DIM, (h + 1) * HEAD_DIM)
                    prod = _mm(jnp.concatenate([rq_t[:, sl], mtx_t[:, sl]], axis=0), state[:, sl], P_SEQ)
                    outs.append(prod[:L])
                    new.append(prod[L:])
                out[bi, rows, :] = jnp.concatenate(outs, axis=1) + o0[bi, rows, :]
                states[(d, bi)] = jnp.concatenate(new, axis=1) + hc[bi, krows, :]
    for (d, bi), state in states.items():
        h_ref[d, bi] = state


def _rwkv_seq(rq, o0, mtx, hc, batch, seq, per_step=4):
    _, t, rw = rq.shape
    n_heads = rw // HEAD_DIM
    L = RWKV_CHUNK * per_step
    lk = HEAD_DIM * per_step
    nc = seq // L
    as4 = lambda x: x.reshape(2, batch, x.shape[1] // batch, rw)
    rq, o0, mtx, hc = as4(rq), as4(o0), as4(mtx), as4(hc)
    fwd = lambda rows: pl.BlockSpec((None, batch, rows, rw), lambda c: (0, 0, c, 0))
    bwd = lambda rows: pl.BlockSpec((None, batch, rows, rw), lambda c: (1, 0, nc - 1 - c, 0))
    out0, out1 = pl.pallas_call(
        functools.partial(_rwkv_seq_body, n_heads=n_heads, batch=batch, per_step=per_step),
        grid=(nc,),
        in_specs=[fwd(L), fwd(L), fwd(lk), fwd(lk), bwd(L), bwd(L), bwd(lk), bwd(lk)],
        out_specs=[pl.BlockSpec((batch, L, rw), lambda c: (0, c, 0)),
                   pl.BlockSpec((batch, L, rw), lambda c: (0, nc - 1 - c, 0))],
        out_shape=[jax.ShapeDtypeStruct((batch, seq, rw), F32)] * 2,
        scratch_shapes=[pltpu.VMEM((2, batch, HEAD_DIM, rw), F32)],
        compiler_params=_cparams(1),
        name="rwkv_seq",
    )(rq, o0, mtx, hc, rq, o0, mtx, hc)
    return out0.reshape(t, rw), out1.reshape(t, rw)


def _rwkv_scan(r, v, a, kd, b, lw, batch, seq):
    rq, o0, mtx, hc = _rwkv_intra(r, v, a, kd, b, lw)
    return _rwkv_seq(rq, o0, mtx, hc, batch, seq)


def _mlstm_prep_body(qk_ref, prev_ref, next_ref, g_ref, cw_ref, cb_ref, gb_ref, q_out, k_out, gate_out,
                     *, mw, n_heads, tiles_per_seq):
    x = qk_ref[...]
    before, after = _neighbours(x, prev_ref[...], next_ref[...], tiles_per_seq)
    y = cb_ref[...] + before * cw_ref[0:1, :] + x * cw_ref[1:2, :] + after * cw_ref[2:3, :]
    y = y * _sigmoid(y)
    q_out[...] = y[:, :mw]
    k_out[...] = y[:, mw:] * (HEAD_DIM ** -0.5)
    g = g_ref[...] + gb_ref[...]
    lane = lax.broadcasted_iota(jnp.int32, g.shape, 1)
    for d in range(2):
        ig = g if d == 0 else pltpu.roll(g, LANE - n_heads, 1)
        fg = pltpu.roll(g, LANE - (1 + d) * n_heads, 1)
        lf = -_softplus(-fg)
        gate_out[d] = jnp.where(lane < n_heads, ig, jnp.where(lane < 2 * n_heads, lf, 0.0))


def _mlstm_prep(pm, seq, conv_w, conv_b, gate_b, mw, tm=512):
    t = pm.shape[0]
    n_heads = mw // HEAD_DIM
    tiles_per_seq = seq // tm
    w2 = 2 * mw
    prev, nxt = _halo_specs(tm, w2, t)
    gcol = (4 * mw) // LANE
    gb = jnp.zeros((1, LANE), F32).at[0, :4 * n_heads].set(gate_b)
    row = pl.BlockSpec((tm, mw), lambda i: (i, 0))
    return pl.pallas_call(
        functools.partial(_mlstm_prep_body, mw=mw, n_heads=n_heads, tiles_per_seq=tiles_per_seq),
        grid=(t // tm,),
        in_specs=[pl.BlockSpec((tm, w2), lambda i: (i, 0)), prev, nxt,
                  pl.BlockSpec((tm, LANE), lambda i: (i, gcol)),
                  _full((3, w2)), _full((1, w2)), _full((1, LANE))],
        out_specs=[row, row, pl.BlockSpec((2, tm, LANE), lambda i: (0, i, 0))],
        out_shape=[jax.ShapeDtypeStruct((t, mw), F32), jax.ShapeDtypeStruct((t, mw), F32),
                   jax.ShapeDtypeStruct((2, t, LANE), F32)],
        compiler_params=_cparams(1),
        name="mlstm_prep",
    )(pm, pm, pm, pm, conv_w, conv_b.reshape(1, w2), gb)


def _mlstm_scan_body(q0_ref, k0_ref, v0_ref, g0_ref, q1_ref, k1_ref, v1_ref, g1_ref, o0_ref, o1_ref,
                     c_ref, m_ref, *, n_heads, group):
    L = MLSTM_CHUNK
    H = n_heads

    @pl.when(pl.program_id(1) == 0)
    def _():
        c_ref[...] = jnp.zeros_like(c_ref)
        m_ref[...] = jnp.zeros_like(m_ref)

    row = lax.broadcasted_iota(jnp.int32, (L, L), 0)
    col = lax.broadcasted_iota(jnp.int32, (L, L), 1)
    trow = lax.broadcasted_iota(jnp.int32, (L, LANE), 0)
    low = lax.broadcasted_iota(jnp.int32, (L, LANE), 1) < HEAD_DIM
    xr = lax.broadcasted_iota(jnp.int32, (LANE, H * L), 0)
    xc = lax.broadcasted_iota(jnp.int32, (LANE, H * L), 1)
    spread = (xr - H == lax.shift_right_logical(xc, int(math.log2(L)))).astype(BF16)
    hs = []
    dirs = ((q0_ref, k0_ref, v0_ref, g0_ref), (q1_ref, k1_ref, v1_ref, g1_ref))
    for bi, d in [(bi, d) for bi in range(group) for d in range(2)]:
        q_ref, k_ref, v_ref, g_ref = dirs[d]
        incl = (col <= row) if d == 0 else (col >= row)
        last = L - 1 if d == 0 else 0
        g = g_ref[bi]
        bcum = _dot_exact_lhs(incl.astype(BF16), g)
        z = pltpu.roll(g, H, 1) - bcum
        cmax = z
        shift = 1
        while shift < L:
            if d == 0:
                moved = jnp.where(trow >= shift, pltpu.roll(cmax, shift, 0), -jnp.inf)
            else:
                moved = jnp.where(trow < L - shift, pltpu.roll(cmax, L - shift, 0), -jnp.inf)
            cmax = jnp.maximum(cmax, moved)
            shift *= 2
        m_prev = m_ref[bi, d, 0:1, :]
        top = jnp.maximum(cmax, m_prev)
        b_last = bcum[last:last + 1, :]
        lwc = b_last + z
        m_new = jnp.maximum(b_last + m_prev, jnp.max(lwc, axis=0, keepdims=True))
        m_ref[bi, d, 0:1, :] = m_new
        alpha_w = _dot_exact_rhs(-top, spread)
        floor_w = jnp.exp(-_dot_exact_rhs(bcum + top, spread, terms=2))
        wts_w = _dot(jnp.exp(lwc - m_new).astype(BF16), spread)
        rows_w = _dot_exact_rhs(jnp.concatenate(
            [jnp.broadcast_to(m_prev, (SUBLANE, LANE)),
             jnp.broadcast_to(jnp.exp(b_last + m_prev - m_new), (SUBLANE, LANE))], axis=0), spread)
        z_t = z.T
        q, k, v = q_ref[bi], k_ref[bi], v_ref[bi]
        for h in range(H):
            slab = slice(h // 2 * LANE, (h // 2 + 1) * LANE)
            cols = slice(h * L, (h + 1) * L)
            mine = low if h % 2 == 0 else jnp.logical_not(low)
            kh = jnp.where(mine, k[:, slab], 0.0)
            hs.append(dict(
                qh=jnp.where(mine, q[:, slab], 0.0).astype(BF16), kh=kh.astype(BF16),
                vext=jnp.where(mine, v[:, slab], 1.0).astype(BF16),
                decay=jnp.exp(jnp.where(incl, alpha_w[:, cols] + z_t[H + h:H + h + 1, :], -jnp.inf)),
                w_inter=jnp.exp(alpha_w[:, cols] + rows_w[0:1, cols]), floor=floor_w[:, cols],
                wk=(wts_w[:, cols] * kh).astype(BF16), dec=rows_w[SUBLANE:SUBLANE + 1, cols],
                cst=c_ref[bi, d, h]))
    for p in hs:
        p["sc"] = (_dot_nt(p["qh"], p["kh"]) * p["decay"]).astype(BF16)
    for p in hs:
        p["numext"] = _dot(p["sc"], p["vext"]) + p["w_inter"] * _dot(p["qh"], p["cst"].astype(BF16))
    for p in hs:
        p["upd"] = _dot_tn(p["wk"], p["vext"])
    for bi, d in [(bi, d) for bi in range(group) for d in range(2)]:
        o_ref = (o0_ref, o1_ref)[d]
        res = []
        for h in range(H):
            p = hs[(bi * 2 + d) * H + h]
            den = pltpu.roll(p["numext"], HEAD_DIM, 1)
            res.append(p["numext"] / jnp.maximum(jnp.abs(den), p["floor"]))
            c_ref[bi, d, h] = p["dec"] * p["cst"] + p["upd"]
        for pair in range(H // 2):
            o_ref[bi, :, pair * LANE:(pair + 1) * LANE] = jnp.where(low, res[2 * pair], res[2 * pair + 1])


def _mlstm_scan(q, k, pm, gates, batch, seq):
    t, mw = q.shape
    n_heads = mw // HEAD_DIM
    L = MLSTM_CHUNK
    nc = seq // L
    group = math.gcd(batch, 4)
    q3, k3, pm3 = (x.reshape(batch, seq, x.shape[1]) for x in (q, k, pm))
    g4 = gates.reshape(2, batch, seq, LANE)
    specs = []
    for d, blk in enumerate((lambda c: c, lambda c: nc - 1 - c)):
        specs += [pl.BlockSpec((group, L, mw), lambda gi, c, blk=blk: (gi, blk(c), 0)),
                  pl.BlockSpec((group, L, mw), lambda gi, c, blk=blk: (gi, blk(c), 0)),
                  pl.BlockSpec((group, L, mw), lambda gi, c, blk=blk: (gi, blk(c), 2)),
                  pl.BlockSpec((None, group, L, LANE), lambda gi, c, blk=blk, d=d: (d, gi, blk(c), 0))]
    out0, out1 = pl.pallas_call(
        functools.partial(_mlstm_scan_body, n_heads=n_heads, group=group),
        grid=(batch // group, nc),
        in_specs=specs,
        out_specs=[pl.BlockSpec((group, L, mw), lambda gi, c: (gi, c, 0)),
                   pl.BlockSpec((group, L, mw), lambda gi, c: (gi, nc - 1 - c, 0))],
        out_shape=[jax.ShapeDtypeStruct((batch, seq, mw), F32)] * 2,
        scratch_shapes=[pltpu.VMEM((group, 2, n_heads, LANE, LANE), F32),
                        pltpu.VMEM((group, 2, SUBLANE, LANE), F32)],
        compiler_params=_cparams(2),
        name="mlstm_scan",
    )(q3, k3, pm3, g4, q3, k3, pm3, g4)
    return out0.reshape(t, mw), out1.reshape(t, mw)


def _layer_norm(x, g, b):
    mu = jnp.mean(x, axis=-1, keepdims=True)
    xc = x - mu
    var = jnp.mean(xc * xc, axis=-1, keepdims=True)
    return xc * lax.rsqrt(var + LN_EPS) * g + b


def _head_norm(x, bd_mean, eps):
    mu = _dot_exact_rhs(x, bd_mean, terms=2)
    xc = x - mu
    var = _dot((xc * xc).astype(BF16), bd_mean)
    return xc * lax.rsqrt(var + eps)


def _mix_out_body(x_ref, yg_ref, ro0_ref, ro1_ref, bonus_ref, rgate_ref, rlg_ref, rlb_ref, mh0_ref, mh1_ref, og_ref,
                  mlg_ref, w_ref, l1g_ref, l1b_ref, rw_ref, rb_ref, bdm_ref,
                  x1_out, x1p_out, topi_out, gate_out, wb_ref, *, alpha, gw, rw):
    @pl.when(pl.program_id(0) == 0)
    def _():
        _cast_rows(w_ref, wb_ref)

    bdm = bdm_ref[...]
    yr = _head_norm(ro0_ref[...] + ro1_ref[...], bdm, RWKV_GN_EPS) * rlg_ref[...] + rlb_ref[...]
    yr = (yr + bonus_ref[...]) * rgate_ref[...]
    ym = _sigmoid(og_ref[...]) * (_head_norm(mh0_ref[...] + mh1_ref[...], bdm, LN_EPS) * mlg_ref[...])
    mix = (_dot(yg_ref[...].astype(BF16), wb_ref[:gw, :]) + _dot(yr.astype(BF16), wb_ref[gw:gw + rw, :])
           + _dot(ym.astype(BF16), wb_ref[gw + rw:, :]))
    x1 = _layer_norm(alpha * x_ref[...] + mix, l1g_ref[...], l1b_ref[...])
    x1_out[...] = x1
    x1p_out[...] = _pack_bf16_pairs(x1)
    lg = _mm(x1, rw_ref[...], "b3") + rb_ref[...]
    lane = lax.broadcasted_iota(jnp.int32, lg.shape, 1)
    vals, topi = [], jnp.zeros(lg.shape, jnp.int32)
    for j in range(TOP_K):
        mx = jnp.max(lg, axis=1, keepdims=True)
        idx = jnp.min(jnp.where(lg == mx, lane, LANE), axis=1, keepdims=True)
        vals.append(mx)
        topi = jnp.where(lane == j, idx, topi)
        lg = jnp.where(lane == idx, -jnp.inf, lg)
    es = [jnp.exp(vj - vals[0]) for vj in vals]
    den = es[0] + es[1] + es[2] + es[3]
    gate = jnp.zeros(lg.shape, F32)
    for j in range(TOP_K):
        gate = jnp.where(lane == j, es[j] / den, gate)
    topi_out[...] = topi.T[:SUBLANE, :]
    gate_out[...] = gate


def _mix_out(x, yg, ro, bonus, rgate, rlg, rlb, mh, pm, mlg, w_out, layer, l1g, l1b, router_w, router_b, alpha,
             tm=512):
    t, dm = x.shape
    gw, rw, mw = yg.shape[1], bonus.shape[1], mh[0].shape[1]
    assert rw == mw
    rwp = jnp.zeros((dm, LANE), F32).at[:, :N_EXPERTS].set(router_w)
    rbp = jnp.full((1, LANE), NEG_BIG, F32).at[0, :N_EXPERTS].set(router_b)
    row = lambda n: pl.BlockSpec((tm, n), lambda i: (i, 0))
    vec = lambda n: _full((1, n))
    return pl.pallas_call(
        functools.partial(_mix_out_body, alpha=alpha, gw=gw, rw=rw),
        grid=(t // tm,),
        in_specs=[row(dm), row(gw), row(rw), row(rw), row(rw), row(rw), vec(rw), vec(rw), row(mw), row(mw),
                  pl.BlockSpec((tm, mw), lambda i: (i, 3)),
                  vec(mw),
                  pl.BlockSpec((None, dm, dm), lambda i: (layer, 0, 0), pipeline_mode=pl.Buffered(1)),
                  vec(dm), vec(dm), _full((dm, LANE)), vec(LANE), _full((rw, rw))],
        out_specs=[row(dm), row(dm // 2), pl.BlockSpec((SUBLANE, tm), lambda i: (0, i)), row(LANE)],
        out_shape=[jax.ShapeDtypeStruct((t, dm), F32), jax.ShapeDtypeStruct((t, dm // 2), jnp.uint32),
                   jax.ShapeDtypeStruct((SUBLANE, t), jnp.int32), jax.ShapeDtypeStruct((t, LANE), F32)],
        scratch_shapes=[pltpu.VMEM((dm, dm), BF16)],
        compiler_params=_cparams(1),
        name="mix_out",
    )(x, yg, ro[0], ro[1], bonus, rgate, rlg.reshape(1, rw), rlb.reshape(1, rw), mh[0], mh[1], pm,
      mlg.reshape(1, mw), w_out, l1g.reshape(1, dm), l1b.reshape(1, dm), rwp, rbp,
      (_block_diag_ones(rw) / HEAD_DIM).astype(BF16))


def _moe_body(be_ref, nu_ref, ve_ref, xs_ref, w1_ref, b1_ref, w2_ref, b2_ref, o_ref, *, dff):
    i = pl.program_id(0)
    active = i < nu_ref[0]

    @pl.when(active)
    def _():
        rowid = i * MOE_BLOCK + lax.broadcasted_iota(jnp.int32, (MOE_BLOCK, 1), 0)
        lo, hi = _unpack_bf16_pairs(jnp.where(rowid < ve_ref[i], xs_ref[...], jnp.uint32(0)))
        xs = jnp.concatenate([lo.astype(BF16), hi.astype(BF16)], axis=1)
        hdn = _dot(xs, w1_ref[...].astype(BF16)) + b1_ref[...]
        g_ = jnp.minimum(hdn[:, :dff], SWIGLU_LIMIT)
        u_ = jnp.clip(hdn[:, dff:], -SWIGLU_LIMIT, SWIGLU_LIMIT)
        act = (u_ + 1.0) * (g_ * _sigmoid(g_ * SWIGLU_ALPHA))
        o_ref[...] = _pack_bf16_pairs(_dot(act.astype(BF16), w2_ref[...].astype(BF16)) + b2_ref[...])

    @pl.when(jnp.logical_not(active))
    def _():
        o_ref[...] = jnp.zeros_like(o_ref)


def _moe_experts(xs, block_e, n_used, valid_end, w1, b1, w2, b2, layer):
    rows, half = xs.shape
    nb = rows // MOE_BLOCK
    depth, ne, dm, dff2 = w1.shape
    dff = dff2 // 2
    grid_spec = pltpu.PrefetchScalarGridSpec(
        num_scalar_prefetch=3,
        grid=(nb,),
        in_specs=[pl.BlockSpec((MOE_BLOCK, half), lambda i, be, nu, ve: (i, 0)),
                  pl.BlockSpec((None, None, dm, dff2), lambda i, be, nu, ve: (layer, be[i], 0, 0)),
                  pl.BlockSpec((None, None, 1, dff2), lambda i, be, nu, ve: (layer, be[i], 0, 0)),
                  pl.BlockSpec((None, None, dff, dm), lambda i, be, nu, ve: (layer, be[i], 0, 0)),
                  pl.BlockSpec((None, None, 1, dm), lambda i, be, nu, ve: (layer, be[i], 0, 0))],
        out_specs=pl.BlockSpec((MOE_BLOCK, half), lambda i, be, nu, ve: (i, 0)),
    )
    return pl.pallas_call(
        functools.partial(_moe_body, dff=dff),
        grid_spec=grid_spec,
        out_shape=jax.ShapeDtypeStruct((rows, half), jnp.uint32),
        compiler_params=_cparams(1),
        name="moe_experts",
    )(block_e, n_used, valid_end, xs, w1, b1.reshape(depth, ne, 1, dff2), w2, b2.reshape(depth, ne, 1, dm))


PLAN_TILE = 512
MOE_BLOCK_SHIFT = MOE_BLOCK.bit_length() - 1
assert 1 << MOE_BLOCK_SHIFT == MOE_BLOCK


def _moe_plan_body(e_ref, dest_ref, meta_ref, rank_ref, *, n_tokens, meta_lanes):
    tiles_per_row = n_tokens // PLAN_TILE
    n_tiles = TOP_K * tiles_per_row
    expert = lax.broadcasted_iota(jnp.int32, (N_EXPERTS, PLAN_TILE), 0)
    r_i = lax.broadcasted_iota(jnp.int32, (PLAN_TILE, PLAN_TILE), 0)
    c_i = lax.broadcasted_iota(jnp.int32, (PLAN_TILE, PLAN_TILE), 1)
    earlier = (r_i < c_i).astype(BF16)

    def tile_hits(it):
        j = it // tiles_per_row
        lanes = pl.ds(pl.multiple_of((it % tiles_per_row) * PLAN_TILE, PLAN_TILE), PLAN_TILE)
        return j, lanes, e_ref[pl.ds(j, 1), lanes] == expert

    def rank_step(it, seen):
        j, lanes, hit = tile_hits(it)
        hitf = hit.astype(F32)
        prior = _dot(hit.astype(BF16), earlier) + seen
        rank_ref[pl.ds(j, 1), lanes] = jnp.sum(hitf * prior, axis=0, keepdims=True)
        return seen + jnp.sum(hitf, axis=1, keepdims=True)

    dest_ref[...] = jnp.zeros_like(dest_ref)
    rank_ref[...] = jnp.zeros_like(rank_ref)
    counts = lax.fori_loop(0, n_tiles, rank_step, jnp.zeros((N_EXPERTS, 1), F32))
    padded = ((counts.astype(jnp.int32) + (MOE_BLOCK - 1)) >> MOE_BLOCK_SHIFT) << MOE_BLOCK_SHIFT
    er = lax.broadcasted_iota(jnp.int32, (N_EXPERTS, N_EXPERTS), 0)
    ec = lax.broadcasted_iota(jnp.int32, (N_EXPERTS, N_EXPERTS), 1)
    seg_end = _dot_exact_lhs((ec <= er).astype(BF16),
                             jnp.broadcast_to(padded.astype(F32), (N_EXPERTS, LANE)))[:, 0:1]
    seg_start = seg_end - padded.astype(F32)

    def dest_step(it, carry):
        j, lanes, hit = tile_hits(it)
        base = jnp.sum(jnp.where(hit, seg_start, 0.0), axis=0, keepdims=True)
        dest_ref[pl.ds(j, 1), lanes] = (rank_ref[pl.ds(j, 1), lanes] + base).astype(jnp.int32)
        return carry

    lax.fori_loop(0, n_tiles, dest_step, 0)
    blk_start = (lax.broadcasted_iota(jnp.int32, (N_EXPERTS, meta_lanes), 1) * MOE_BLOCK).astype(F32)
    blk_expert = jnp.minimum(jnp.sum((seg_end <= blk_start).astype(F32), axis=0, keepdims=True), N_EXPERTS - 1.0)
    mine = lax.broadcasted_iota(jnp.int32, (N_EXPERTS, meta_lanes), 0).astype(F32) == blk_expert
    valid_end = jnp.sum(jnp.where(mine, seg_start + counts, 0.0), axis=0, keepdims=True)
    n_used = jnp.broadcast_to(seg_end[N_EXPERTS - 1:N_EXPERTS, :] * (1.0 / MOE_BLOCK), (1, meta_lanes))
    mrow = lax.broadcasted_iota(jnp.int32, (SUBLANE, meta_lanes), 0)
    meta = jnp.where(mrow == 0, blk_expert, jnp.where(mrow == 1, valid_end, jnp.where(mrow == 2, n_used, 0.0)))
    meta_ref[...] = meta.astype(jnp.int32)


def _moe_plan(e_t, n_tokens, n_blocks):
    meta_lanes = -(-n_blocks // LANE) * LANE
    dest, meta = pl.pallas_call(
        functools.partial(_moe_plan_body, n_tokens=n_tokens, meta_lanes=meta_lanes),
        grid=(1,),
        in_specs=[_full((SUBLANE, n_tokens))],
        out_specs=[_full((SUBLANE, n_tokens)), _full((SUBLANE, meta_lanes))],
        out_shape=[jax.ShapeDtypeStruct((SUBLANE, n_tokens), jnp.int32),
                   jax.ShapeDtypeStruct((SUBLANE, meta_lanes), jnp.int32)],
        scratch_shapes=[pltpu.VMEM((SUBLANE, n_tokens), F32)],
        compiler_params=_cparams(1),
        name="moe_plan",
    )(e_t)
    return dest[:TOP_K], meta[0, :n_blocks], meta[1, :n_blocks], meta[2, :1]


def _combine_body(x1_ref, y0_ref, y1_ref, y2_ref, y3_ref, gate_ref, g_ref, b_ref, o_ref, *, alpha):
    gate = gate_ref[...]
    lo, hi = _unpack_bf16_pairs(y0_ref[...])
    lo, hi = gate[:, 0:1] * lo, gate[:, 0:1] * hi
    for j, y_ref in enumerate((y1_ref, y2_ref, y3_ref), start=1):
        lo_j, hi_j = _unpack_bf16_pairs(y_ref[...])
        lo, hi = lo + gate[:, j:j + 1] * lo_j, hi + gate[:, j:j + 1] * hi_j
    ffn = jnp.concatenate([lo, hi], axis=1)
    o_ref[...] = _layer_norm(alpha * x1_ref[...] + ffn, g_ref[...], b_ref[...])


def _combine(x1, yg, gate, ln_g, ln_b, alpha, tm=512):
    t, dm = x1.shape
    n_tiles = t // tm
    expert_rows = lambda j: pl.BlockSpec((tm, dm // 2), lambda i: (i + j * n_tiles, 0))
    return pl.pallas_call(
        functools.partial(_combine_body, alpha=alpha),
        grid=(n_tiles,),
        in_specs=[pl.BlockSpec((tm, dm), lambda i: (i, 0))] + [expert_rows(j) for j in range(TOP_K)]
                 + [pl.BlockSpec((tm, LANE), lambda i: (i, 0)), _full((1, dm)), _full((1, dm))],
        out_specs=pl.BlockSpec((tm, dm), lambda i: (i, 0)),
        out_shape=jax.ShapeDtypeStruct((t, dm), F32),
        compiler_params=_cparams(1),
        name="combine_ln",
    )(x1, yg, yg, yg, yg, gate, ln_g.reshape(1, dm), ln_b.reshape(1, dm))


SC_CORES = 2
SC_SUBCORES = 16
SC_WORKERS = SC_CORES * SC_SUBCORES


def _sc_gather_rows(table, idx, window):
    n = idx.shape[0]
    dim = table.shape[1]
    n_steps = n // (SC_WORKERS * window)
    assert n_steps * window * SC_WORKERS == n and n_steps % 2 == 0 and window % SUBLANE == 0 and window <= LANE
    idx3 = idx.reshape(SC_WORKERS, n_steps, window)
    mesh = plsc.VectorSubcoreMesh(core_axis_name="c", subcore_axis_name="s",
                                  num_cores=SC_CORES, num_subcores=SC_SUBCORES)

    def body(table_hbm, idx_hbm, out_hbm, idx_v, rows_v, gsem, wsem):
        wid = lax.axis_index("s") * SC_CORES + lax.axis_index("c")
        pltpu.sync_copy(idx_hbm.at[wid], idx_v)

        def gather(j, buf):
            return pltpu.make_async_copy(table_hbm.at[idx_v.at[j]], rows_v.at[buf], gsem.at[buf])

        def write(j, buf):
            base = pl.multiple_of((wid * n_steps + j) * window, window)
            return pltpu.make_async_copy(rows_v.at[buf], out_hbm.at[pl.ds(base, window)], wsem.at[buf])

        gather(0, 0).start()

        @pl.loop(0, n_steps, step=2)
        def _(j0):
            for buf in range(2):
                j = j0 + buf
                gather(j, buf).wait()

                @pl.when(j >= 1)
                def _():
                    write(j - 1, 1 - buf).wait()

                @pl.when(j + 1 < n_steps)
                def _():
                    gather(j + 1, 1 - buf).start()

                write(j, buf).start()

        write(n_steps - 1, 1).wait()

    return pl.kernel(
        body, out_type=jax.ShapeDtypeStruct((n, dim), table.dtype), mesh=mesh,
        scratch_types=[pltpu.VMEM((n_steps, window), jnp.int32), pltpu.VMEM((2, window, dim), table.dtype),
                       pltpu.SemaphoreType.DMA((2,)), pltpu.SemaphoreType.DMA((2,))],
        name="sc_gather",
    )(table, idx3)


def _sc_scatter_rows(src, dest, n_out, window):
    t, dim = src.shape
    k = dest.shape[0]
    n_steps = t // (SC_WORKERS * window)
    assert n_steps * window * SC_WORKERS == t and n_steps % 2 == 0 and window % SUBLANE == 0 and window <= LANE
    idx3 = dest.reshape(k, SC_WORKERS, n_steps, window).transpose(1, 2, 0, 3).reshape(SC_WORKERS, n_steps * k, window)
    mesh = plsc.VectorSubcoreMesh(core_axis_name="c", subcore_axis_name="s",
                                  num_cores=SC_CORES, num_subcores=SC_SUBCORES)

    def body(src_hbm, idx_hbm, out_hbm, idx_v, rows_v, rsem, ssem):
        wid = lax.axis_index("s") * SC_CORES + lax.axis_index("c")
        pltpu.sync_copy(idx_hbm.at[wid], idx_v)

        def read(s, buf):
            base = pl.multiple_of((wid * n_steps + s) * window, window)
            return pltpu.make_async_copy(src_hbm.at[pl.ds(base, window)], rows_v.at[buf], rsem.at[buf])

        def scatter(s, j, buf):
            return pltpu.make_async_copy(rows_v.at[buf], out_hbm.at[idx_v.at[s * k + j]], ssem.at[buf])

        read(0, 0).start()

        @pl.loop(0, n_steps, step=2)
        def _(s0):
            for buf in range(2):
                s = s0 + buf
                read(s, buf).wait()

                @pl.when(s >= 1)
                def _():
                    for j in range(k):
                        scatter(s - 1, j, 1 - buf).wait()

                @pl.when(s + 1 < n_steps)
                def _():
                    read(s + 1, 1 - buf).start()

                for j in range(k):
                    scatter(s, j, buf).start()

        for j in range(k):
            scatter(n_steps - 1, j, 1).wait()

    return pl.kernel(
        body, out_type=jax.ShapeDtypeStruct((n_out, dim), src.dtype), mesh=mesh,
        scratch_types=[pltpu.VMEM((n_steps * k, window), jnp.int32), pltpu.VMEM((2, window, dim), src.dtype),
                       pltpu.SemaphoreType.DMA((2,)), pltpu.SemaphoreType.DMA((2,))],
        name="sc_scatter",
    )(src, idx3)


def kernel(x, w_in, gmlp_ln_g, gmlp_ln_b, gmlp_ws, gmlp_bs, rwkv_mu, rwkv_w0, rwkv_w2, rwkv_a0, rwkv_a2, rwkv_g2, rwkv_k_k, rwkv_k_a, rwkv_r_k, rwkv_ln_g, rwkv_ln_b, mlstm_conv_w, mlstm_conv_b, mlstm_gate_b, mlstm_ln_g, w_out, ln1_g, ln1_b, router_w, router_b, exp_w1, exp_b1, exp_w2, exp_b2, ln2_g, ln2_b):
    batch, seq, dm = x.shape
    depth = w_in.shape[0]
    t = batch * seq
    gw = gmlp_ln_g.shape[1]
    rw = rwkv_w0.shape[2]
    mw = mlstm_ln_g.shape[1]
    g_proj = 2 * gw
    r_proj = 3 * rw + W_LORA + A_LORA + G_LORA
    alpha = (2 * depth) ** 0.25
    n_blocks = -(-t * TOP_K // MOE_BLOCK) + N_EXPERTS
    xf = x.reshape(t, dm)
    for l in range(depth):
        y_g, pr, pm = _proj(xf, w_in, l, g_proj, r_proj, gmlp_ln_g[l], gmlp_ln_b[l], gmlp_ws[l], gmlp_bs[l])
        r, v, a, kd, b, lw, bonus, rgate = _rwkv_prep(
            pr, seq, rwkv_mu[l], rwkv_w0[l], rwkv_w2[l], rwkv_a0[l], rwkv_a2[l], rwkv_g2[l],
            rwkv_k_k[l], rwkv_k_a[l], rwkv_r_k[l].reshape(-1))
        ro = _rwkv_scan(r, v, a, kd, b, lw, batch, seq)
        q, k, gates = _mlstm_prep(pm, seq, mlstm_conv_w[l], mlstm_conv_b[l], mlstm_gate_b[l], mw)
        mh = _mlstm_scan(q, k, pm, gates, batch, seq)
        x1, x1p, topi, gate = _mix_out(xf, y_g, ro, bonus, rgate, rwkv_ln_g[l], rwkv_ln_b[l], mh, pm, mlstm_ln_g[l],
                                       w_out, l, ln1_g[l], ln1_b[l], router_w[l], router_b[l], alpha)
        dest, block_e, valid_end, n_used = _moe_plan(topi, t, n_blocks)
        xs = _sc_scatter_rows(x1p, dest, n_blocks * MOE_BLOCK, window=64)
        ys = _moe_experts(xs, block_e, n_used, valid_end, exp_w1, exp_b1, exp_w2, exp_b2, l)
        yg = _sc_gather_rows(ys, dest.reshape(-1), window=64)
        xf = _combine(x1, yg, gate, ln2_g[l], ln2_b[l], alpha)
    return xf.reshape(batch, seq, dm)
```

```python
import functools
import math

import jax
import jax.numpy as jnp
from jax import lax
from jax.experimental import pallas as pl
from jax.experimental.pallas import tpu as pltpu
from jax.experimental.pallas import tpu_sc as plsc

F32 = jnp.float32
BF16 = jnp.bfloat16

HEAD_DIM = 64
GMLP_CHUNK = 128
MLSTM_CHUNK = 128
RWKV_CHUNK = 64
W_LORA = 64
A_LORA = 64
G_LORA = 128
N_EXPERTS = 32
TOP_K = 4
MOE_BLOCK = 512
SWIGLU_LIMIT = 7.0
SWIGLU_ALPHA = 1.702
LN_EPS = 1e-5
RWKV_GN_EPS = 64e-5
LANE = 128
SUBLANE = 8
VMEM_LIMIT = 48 * 1024 * 1024
NEG_BIG = -1e30


def _cparams(n_axes):
    return pltpu.CompilerParams(dimension_semantics=("arbitrary",) * n_axes,
                                vmem_limit_bytes=VMEM_LIMIT)


def _full(shape):
    return pl.BlockSpec(shape, lambda *_: (0,) * len(shape))


def _dot(a, b, precision=None):
    return jnp.dot(a, b, preferred_element_type=F32, precision=precision)


def _dot_nt(a, b, precision=None):
    return lax.dot_general(a, b, (((1,), (1,)), ((), ())), preferred_element_type=F32, precision=precision)


def _dot_tn(a, b, precision=None):
    return lax.dot_general(a, b, (((0,), (0,)), ((), ())), preferred_element_type=F32, precision=precision)


def _split(x):
    hi = x.astype(BF16)
    return hi, (x - hi.astype(F32)).astype(BF16)


def _split3(x):
    hi = x.astype(BF16)
    r1 = x - hi.astype(F32)
    mid = r1.astype(BF16)
    return hi, mid, (r1 - mid.astype(F32)).astype(BF16)


def _mm(a, b, mode, dot=_dot):
    if mode == "b1":
        return dot(a.astype(BF16), b.astype(BF16))
    bh, bl = _split(b)
    if mode == "b2":
        ah = a.astype(BF16)
        return dot(ah, bh) + dot(ah, bl)
    ah, al = _split(a)
    return dot(ah, bh) + (dot(ah, bl) + dot(al, bh))


def _dot_exact_lhs(a_bf16, x):
    hi, mid, lo = _split3(x)
    return _dot(a_bf16, hi) + (_dot(a_bf16, mid) + _dot(a_bf16, lo))


def _dot_exact_rhs(x, b_bf16, terms=3):
    if terms == 2:
        hi, lo = _split(x)
        return _dot(hi, b_bf16) + _dot(lo, b_bf16)
    hi, mid, lo = _split3(x)
    return _dot(hi, b_bf16) + (_dot(mid, b_bf16) + _dot(lo, b_bf16))


def _pack_bf16_pairs(x):
    n = x.shape[1] // 2
    lo = pltpu.bitcast(x[:, :n].astype(BF16).astype(F32), jnp.uint32)
    hi = pltpu.bitcast(x[:, n:].astype(BF16).astype(F32), jnp.uint32)
    return hi | (lo >> 16)


def _unpack_bf16_pairs(w):
    lo = pltpu.bitcast(w << 16, F32)
    hi = pltpu.bitcast(w & jnp.uint32(0xFFFF0000), F32)
    return lo, hi


def _sigmoid(x):
    return 1.0 / (1.0 + jnp.exp(-x))


def _softplus(x):
    return jnp.maximum(x, 0.0) + jnp.log1p(jnp.exp(-jnp.abs(x)))


def _block_diag_ones(width):
    h = jnp.arange(width) // HEAD_DIM
    return (h[:, None] == h[None, :]).astype(F32)


CAST_ROWS = 128


def _cast_rows(src_ref, dst_ref):
    n_src, n_dst = src_ref.shape[1], dst_ref.shape[1]
    whole = n_src // LANE * LANE

    def step(r, carry):
        rows = pl.ds(pl.multiple_of(r * CAST_ROWS, CAST_ROWS), CAST_ROWS)
        dst_ref[rows, :whole] = src_ref[rows, :whole].astype(BF16)
        if n_dst > whole:
            tail = [src_ref[rows, whole:]] if n_src > whole else []
            tail.append(jnp.zeros((CAST_ROWS, n_dst - n_src), F32))
            dst_ref[rows, whole:] = jnp.concatenate(tail, axis=1).astype(BF16)
        return carry
    lax.fori_loop(0, src_ref.shape[0] // CAST_ROWS, step, 0)


def _gmlp_gate(p, lng_ref, lnb_ref, ws_ref, bst_ref, o_ref):
    gw = p.shape[1] // 2
    p = 0.5 * p * (1.0 + lax.erf(p * math.sqrt(0.5)))
    u, v = p[:, :gw], p[:, gw:]
    mu = jnp.mean(v, axis=-1, keepdims=True)
    vc = v - mu
    var = jnp.mean(vc * vc, axis=-1, keepdims=True)
    vn = vc * lax.rsqrt(var + LN_EPS) * lng_ref[...] + lnb_ref[...]
    for c in range(p.shape[0] // GMLP_CHUNK):
        rows = slice(c * GMLP_CHUNK, (c + 1) * GMLP_CHUNK)
        ys = []
        for h in range(gw // HEAD_DIM):
            cols = slice(h * HEAD_DIM, (h + 1) * HEAD_DIM)
            ys.append(_dot(ws_ref[h], vn[rows, cols].astype(BF16)) + bst_ref[:, h:h + 1])
        o_ref[rows, :] = u[rows, :] * jnp.concatenate(ys, axis=1)


def _proj_body(x_ref, w_ref, lng_ref, lnb_ref, ws_ref, bst_ref, yg_ref, pr_ref, pm_ref, wb_ref, *, ng, nr):
    @pl.when(pl.program_id(0) == 0)
    def _():
        _cast_rows(w_ref, wb_ref)

    xb = x_ref[...].astype(BF16)
    _gmlp_gate(_dot(xb, wb_ref[:, :ng]), lng_ref, lnb_ref, ws_ref, bst_ref, yg_ref)
    pr_ref[...] = _dot(xb, wb_ref[:, ng:ng + nr])
    pm_ref[...] = _dot(xb, wb_ref[:, ng + nr:])


def _proj(x, w_in, layer, ng, nr, ln_g, ln_b, ws, bs, tm=512):
    t, d = x.shape
    p_in = w_in.shape[2]
    p_pad = -(-p_in // LANE) * LANE
    nm = p_pad - ng - nr
    gw = ng // 2
    n_heads = gw // HEAD_DIM
    bst = jnp.zeros((GMLP_CHUNK, LANE), F32).at[:, :n_heads].set(bs.T)
    row = lambda n: pl.BlockSpec((tm, n), lambda i: (i, 0))
    return pl.pallas_call(
        functools.partial(_proj_body, ng=ng, nr=nr),
        grid=(t // tm,),
        in_specs=[row(d), pl.BlockSpec((None, d, p_in), lambda i: (layer, 0, 0), pipeline_mode=pl.Buffered(1)),
                  _full((1, gw)), _full((1, gw)), _full((n_heads, GMLP_CHUNK, GMLP_CHUNK)),
                  _full((GMLP_CHUNK, LANE))],
        out_specs=[row(gw), row(nr), row(nm)],
        out_shape=[jax.ShapeDtypeStruct((t, n), F32) for n in (gw, nr, nm)],
        scratch_shapes=[pltpu.VMEM((d, p_pad), BF16)],
        compiler_params=_cparams(1),
        name="in_proj",
    )(x, w_in, ln_g.reshape(1, gw), ln_b.reshape(1, gw), ws.astype(BF16), bst)


def _halo_specs(tm, width, n_rows):
    per8 = tm // SUBLANE
    last = n_rows // SUBLANE - 1
    prev = pl.BlockSpec((SUBLANE, width), lambda i: (jnp.maximum(i * per8 - 1, 0), 0))
    nxt = pl.BlockSpec((SUBLANE, width), lambda i: (jnp.minimum((i + 1) * per8, last), 0))
    return prev, nxt


def _neighbours(cur, prev_blk, next_blk, tiles_per_seq):
    tm = cur.shape[0]
    j = pl.program_id(0) % tiles_per_seq
    prev_row = jnp.where(j > 0, prev_blk[SUBLANE - 1:SUBLANE, :], 0.0)
    next_row = jnp.where(j < tiles_per_seq - 1, next_blk[0:1, :], 0.0)
    ridx = lax.broadcasted_iota(jnp.int32, cur.shape, 0)
    before = jnp.where(ridx == 0, prev_row, pltpu.roll(cur, 1, 0))
    after = jnp.where(ridx == tm - 1, next_row, pltpu.roll(cur, tm - 1, 0))
    return before, after


def _rwkv_prep_body(pr_ref, prev_ref, next_ref, mu_ref, w0_ref, w2_ref, a0_ref, a2_ref, g2_ref,
                    kk_ref, ka_ref, rk_ref, bd_ref,
                    r_out, v_out, a_out, kd_out, b_out, lw_out, bonus_out, gate_out, *, rw, tiles_per_seq):
    pf = pr_ref[...]
    before, after = _neighbours(pf, prev_ref[...], next_ref[...], tiles_per_seq)
    pf = pf + mu_ref[0:1, :] * (before - pf) + mu_ref[1:2, :] * (after - pf)
    o3 = 3 * rw
    r, k, v = pf[:, :rw], pf[:, rw:2 * rw], pf[:, 2 * rw:o3]
    wd = pf[:, o3:o3 + W_LORA]
    ad = pf[:, o3 + W_LORA:o3 + W_LORA + A_LORA]
    gd = pf[:, o3 + W_LORA + A_LORA:]
    bd = bd_ref[...]
    kk = k * kk_ref[...]
    ss = _dot((kk * kk).astype(BF16), bd)
    kk = kk / jnp.maximum(jnp.sqrt(ss), 1e-12)
    twd = jnp.tanh(wd)
    ksum = jnp.zeros_like(k)
    for d in range(2):
        w_log = -_softplus(-(w0_ref[d:d + 1, :] + _mm(twd, w2_ref[d], "b3"))) - 0.5
        lw_out[d] = -jnp.exp(w_log)
        iclr = _sigmoid(a0_ref[d:d + 1, :] + _mm(ad, a2_ref[d], "b3"))
        kd = k * (1.0 + (iclr - 1.0) * ka_ref[...])
        kd_out[d] = kd
        b_out[d] = kk * iclr
        ksum = ksum + kd
    r_out[...] = r
    v_out[...] = v
    a_out[...] = -kk
    bonus_out[...] = _dot((r * ksum * rk_ref[...]).astype(BF16), bd) * v
    gate_out[...] = _dot(_sigmoid(gd).astype(BF16), g2_ref[...])


def _rwkv_prep(pr, seq, mu, w0, w2, a0, a2, g2, k_k, k_a, r_k, tm=512):
    t, rproj = pr.shape
    rw = w0.shape[1]
    tiles_per_seq = seq // tm
    prev, nxt = _halo_specs(tm, rproj, t)
    row = pl.BlockSpec((tm, rw), lambda i: (i, 0))
    row2 = pl.BlockSpec((2, tm, rw), lambda i: (0, i, 0))
    one = jax.ShapeDtypeStruct((t, rw), F32)
    two = jax.ShapeDtypeStruct((2, t, rw), F32)
    return pl.pallas_call(
        functools.partial(_rwkv_prep_body, rw=rw, tiles_per_seq=tiles_per_seq),
        grid=(t // tm,),
        in_specs=[pl.BlockSpec((tm, rproj), lambda i: (i, 0)), prev, nxt,
                  _full((2, rproj)), _full((2, rw)), _full((2, W_LORA, rw)), _full((2, rw)),
                  _full((2, A_LORA, rw)), _full((G_LORA, rw)), _full((1, rw)), _full((1, rw)),
                  _full((1, rw)), _full((rw, rw))],
        out_specs=[row, row, row, row2, row2, row2, row, row],
        out_shape=[one, one, one, two, two, two, one, one],
        compiler_params=_cparams(1),
        name="rwkv_prep",
    )(pr, pr, pr, mu, w0, w2, a0, a2, g2.astype(BF16), k_k.reshape(1, rw), k_a.reshape(1, rw),
      r_k.reshape(1, rw), _block_diag_ones(rw).astype(BF16))


P_G, P_INV, P_APPLY, P_STATE, P_SEQ = "b1", "b1", "b1", "b1", "b1"


def _rwkv_intra_body(r_ref, v_ref, a_ref, kd_ref, b_ref, lw_ref, rq_out, o0_out, mtx_out, hc_out,
                     *, n_heads, chunks):
    L = RWKV_CHUNK
    d = pl.program_id(0)
    row = lax.broadcasted_iota(jnp.int32, (L, L), 0)
    col = lax.broadcasted_iota(jnp.int32, (L, L), 1)
    fwd = d == 0
    rel = (col - row) * (1 - 2 * d)
    incl = rel <= 0
    strict = rel < 0
    eye = (row == col).astype(F32)
    tri = incl.astype(BF16)
    pairs = []
    for c in range(chunks):
        rows = slice(c * L, (c + 1) * L)
        lw = lw_ref[rows, :]
        cum = _dot_exact_lhs(tri, lw)
        tot = jnp.where(fwd, cum[L - 1:L, :], cum[0:1, :])
        e_neg = jnp.exp(-cum)
        e_end = jnp.exp(tot - cum)
        e_tot = jnp.exp(tot)
        r, v, a, kd, b = r_ref[rows, :], v_ref[rows, :], a_ref[rows, :], kd_ref[rows, :], b_ref[rows, :]
        at, rt, bt, kt = a * jnp.exp(cum - lw), r * jnp.exp(cum), b * e_neg, kd * e_neg
        kend, bend = kd * e_end, b * e_end
        for h in range(n_heads):
            sl = slice(h * HEAD_DIM, (h + 1) * HEAD_DIM)
            pairs.append(dict(at=at[:, sl], rt=rt[:, sl], bt=bt[:, sl], kt=kt[:, sl], v=v[:, sl],
                              kend=kend[:, sl], bend=bend[:, sl], e_tot=e_tot[:, sl]))
    for p in pairs:
        p["g"] = _mm(jnp.concatenate([p["at"], p["rt"]], axis=0),
                     jnp.concatenate([p["bt"], p["kt"]], axis=0), P_G, _dot_nt)
    row2 = lax.broadcasted_iota(jnp.int32, (L, 2 * L), 0)
    col2 = lax.broadcasted_iota(jnp.int32, (L, 2 * L), 1) & (L - 1)
    rel2 = (col2 - row2) * (1 - 2 * d)
    incl2 = rel2 <= 0
    strict2 = rel2 < 0
    zeros = jnp.zeros((L, HEAD_DIM), F32)
    for p in pairs:
        g = p.pop("g")
        a_both = jnp.where(strict2, g[:L, :], 0.0)
        p["m_both"] = jnp.where(incl2, g[L:, :], 0.0)
        p["pw"] = a_both[:, :L]
        p["a_ak"] = a_both[:, L:]
        p["inv"] = eye + p["pw"]
    for _ in range(int(math.log2(L)) - 1):
        for p in pairs:
            p["pw"] = _mm(p["pw"], p["pw"], P_INV)
        for p in pairs:
            p["inv"] = p["inv"] + _mm(p["inv"], p["pw"], P_INV)
    for p in pairs:
        p["akv"] = _mm(p["a_ak"], p["v"], P_APPLY)
    for p in pairs:
        wu = _mm(p["inv"], jnp.concatenate([p["at"], p["akv"]], axis=1), P_APPLY)
        p["rhs"] = jnp.concatenate([wu, jnp.concatenate([zeros, p["v"]], axis=1)], axis=0)
    for p in pairs:
        p["rq_o0"] = _mm(p["m_both"], p["rhs"], P_APPLY)
    for p in pairs:
        p["m_hc"] = _mm(jnp.concatenate([p["bend"], p["kend"]], axis=0), p["rhs"], P_STATE, _dot_tn)
    for c in range(chunks):
        ps = pairs[c * n_heads:(c + 1) * n_heads]
        rows = slice(c * L, (c + 1) * L)
        krows = slice(c * HEAD_DIM, (c + 1) * HEAD_DIM)
        rq_out[rows, :] = jnp.concatenate([p["rt"] + p["rq_o0"][:, :HEAD_DIM] for p in ps], axis=1)
        o0_out[rows, :] = jnp.concatenate([p["rq_o0"][:, HEAD_DIM:] for p in ps], axis=1)
        mtx_out[krows, :] = jnp.concatenate([eye * p["e_tot"] + p["m_hc"][:, :HEAD_DIM] for p in ps], axis=1)
        hc_out[krows, :] = jnp.concatenate([p["m_hc"][:, HEAD_DIM:] for p in ps], axis=1)


def _rwkv_intra(r, v, a, kd, b, lw, chunks=8):
    t, rw = r.shape
    n_heads = rw // HEAD_DIM
    tm = chunks * RWKV_CHUNK
    tk = chunks * HEAD_DIM
    n_tiles = t // tm
    one = pl.BlockSpec((tm, rw), lambda d, i: (i, 0))
    two = pl.BlockSpec((None, tm, rw), lambda d, i: (d, i, 0))
    twok = pl.BlockSpec((None, tk, rw), lambda d, i: (d, i, 0))
    return pl.pallas_call(
        functools.partial(_rwkv_intra_body, n_heads=n_heads, chunks=chunks),
        grid=(2, n_tiles),
        in_specs=[one, one, one, two, two, two],
        out_specs=[two, two, twok, twok],
        out_shape=[jax.ShapeDtypeStruct((2, t, rw), F32), jax.ShapeDtypeStruct((2, t, rw), F32),
                   jax.ShapeDtypeStruct((2, n_tiles * tk, rw), F32),
                   jax.ShapeDtypeStruct((2, n_tiles * tk, rw), F32)],
        compiler_params=_cparams(2),
        name="rwkv_intra",
    )(r, v, a, kd, b, lw)


def _rwkv_seq_body(rq0, o00, mtx0, hc0, rq1, o01, mtx1, hc1, out0, out1, h_ref, *, n_heads, batch, per_step):
    @pl.when(pl.program_id(0) == 0)
    def _():
        h_ref[...] = jnp.zeros_like(h_ref)

    L = RWKV_CHUNK
    dirs = ((rq0, o00, mtx0, hc0, out0), (rq1, o01, mtx1, hc1, out1))
    states = {(d, bi): h_ref[d, bi] for d in range(2) for bi in range(batch)}
    for step in range(per_step):
        for d, (rq, o0, mtx, hc, out) in enumerate(dirs):
            sub = step if d == 0 else per_step - 1 - step
            rows = slice(sub * L, (sub + 1) * L)
            krows = slice(sub * HEAD_DIM, (sub + 1) * HEAD_DIM)
            for bi in range(batch):
                rq_t, mtx_t, state = rq[bi, rows, :], mtx[bi, krows, :], states[(d, bi)]
                outs, new = [], []
                for h in range(n_heads):
                    sl = slice(h * HEAD_DIM, (h + 1) * HEAD_DIM)
                    prod = _mm(jnp.concatenate([rq_t[:, sl], mtx_t[:, sl]], axis=0), state[:, sl], P_SEQ)
                    outs.append(prod[:L])
                    new.append(prod[L:])
                out[bi, rows, :] = jnp.concatenate(outs, axis=1) + o0[bi, rows, :]
                states[(d, bi)] = jnp.concatenate(new, axis=1) + hc[bi, krows, :]
    for (d, bi), state in states.items():
        h_ref[d, bi] = state


def _rwkv_seq(rq, o0, mtx, hc, batch, seq, per_step=4):
    _, t, rw = rq.shape
    n_heads = rw // HEAD_DIM
    L = RWKV_CHUNK * per_step
    lk = HEAD_DIM * per_step
    nc = seq // L
    as4 = lambda x: x.reshape(2, batch, x.shape[1] // batch, rw)
    rq, o0, mtx, hc = as4(rq), as4(o0), as4(mtx), as4(hc)
    fwd = lambda rows: pl.BlockSpec((None, batch, rows, rw), lambda c: (0, 0, c, 0))
    bwd = lambda rows: pl.BlockSpec((None, batch, rows, rw), lambda c: (1, 0, nc - 1 - c, 0))
    out0, out1 = pl.pallas_call(
        functools.partial(_rwkv_seq_body, n_heads=n_heads, batch=batch, per_step=per_step),
        grid=(nc,),
        in_specs=[fwd(L), fwd(L), fwd(lk), fwd(lk), bwd(L), bwd(L), bwd(lk), bwd(lk)],
        out_specs=[pl.BlockSpec((batch, L, rw), lambda c: (0, c, 0)),
                   pl.BlockSpec((batch, L, rw), lambda c: (0, nc - 1 - c, 0))],
        out_shape=[jax.ShapeDtypeStruct((batch, seq, rw), F32)] * 2,
        scratch_shapes=[pltpu.VMEM((2, batch, HEAD_DIM, rw), F32)],
        compiler_params=_cparams(1),
        name="rwkv_seq",
    )(rq, o0, mtx, hc, rq, o0, mtx, hc)
    return out0.reshape(t, rw), out1.reshape(t, rw)


def _rwkv_scan(r, v, a, kd, b, lw, batch, seq):
    rq, o0, mtx, hc = _rwkv_intra(r, v, a, kd, b, lw)
    return _rwkv_seq(rq, o0, mtx, hc, batch, seq)


def _mlstm_prep_body(qk_ref, prev_ref, next_ref, g_ref, cw_ref, cb_ref, gb_ref, q_out, k_out, gate_out,
                     *, mw, n_heads, tiles_per_seq):
    x = qk_ref[...]
    before, after = _neighbours(x, prev_ref[...], next_ref[...], tiles_per_seq)
    y = cb_ref[...] + before * cw_ref[0:1, :] + x * cw_ref[1:2, :] + after * cw_ref[2:3, :]
    y = y * _sigmoid(y)
    q_out[...] = y[:, :mw]
    k_out[...] = y[:, mw:] * (HEAD_DIM ** -0.5)
    g = g_ref[...] + gb_ref[...]
    lane = lax.broadcasted_iota(jnp.int32, g.shape, 1)
    for d in range(2):
        ig = g if d == 0 else pltpu.roll(g, LANE - n_heads, 1)
        fg = pltpu.roll(g, LANE - (1 + d) * n_heads, 1)
        lf = -_softplus(-fg)
        gate_out[d] = jnp.where(lane < n_heads, ig, jnp.where(lane < 2 * n_heads, lf, 0.0))


def _mlstm_prep(pm, seq, conv_w, conv_b, gate_b, mw, tm=512):
    t = pm.shape[0]
    n_heads = mw // HEAD_DIM
    tiles_per_seq = seq // tm
    w2 = 2 * mw
    prev, nxt = _halo_specs(tm, w2, t)
    gcol = (4 * mw) // LANE
    gb = jnp.zeros((1, LANE), F32).at[0, :4 * n_heads].set(gate_b)
    row = pl.BlockSpec((tm, mw), lambda i: (i, 0))
    return pl.pallas_call(
        functools.partial(_mlstm_prep_body, mw=mw, n_heads=n_heads, tiles_per_seq=tiles_per_seq),
        grid=(t // tm,),
        in_specs=[pl.BlockSpec((tm, w2), lambda i: (i, 0)), prev, nxt,
                  pl.BlockSpec((tm, LANE), lambda i: (i, gcol)),
                  _full((3, w2)), _full((1, w2)), _full((1, LANE))],
        out_specs=[row, row, pl.BlockSpec((2, tm, LANE), lambda i: (0, i, 0))],
        out_shape=[jax.ShapeDtypeStruct((t, mw), F32), jax.ShapeDtypeStruct((t, mw), F32),
                   jax.ShapeDtypeStruct((2, t, LANE), F32)],
        compiler_params=_cparams(1),
        name="mlstm_prep",
    )(pm, pm, pm, pm, conv_w, conv_b.reshape(1, w2), gb)


def _mlstm_scan_body(q0_ref, k0_ref, v0_ref, g0_ref, q1_ref, k1_ref, v1_ref, g1_ref, o0_ref, o1_ref,
                     c_ref, m_ref, *, n_heads, group):
    L = MLSTM_CHUNK
    H = n_heads

    @pl.when(pl.program_id(1) == 0)
    def _():
        c_ref[...] = jnp.zeros_like(c_ref)
        m_ref[...] = jnp.zeros_like(m_ref)

    row = lax.broadcasted_iota(jnp.int32, (L, L), 0)
    col = lax.broadcasted_iota(jnp.int32, (L, L), 1)
    trow = lax.broadcasted_iota(jnp.int32, (L, LANE), 0)
    low = lax.broadcasted_iota(jnp.int32, (L, LANE), 1) < HEAD_DIM
    xr = lax.broadcasted_iota(jnp.int32, (LANE, H * L), 0)
    xc = lax.broadcasted_iota(jnp.int32, (LANE, H * L), 1)
    spread = (xr - H == lax.shift_right_logical(xc, int(math.log2(L)))).astype(BF16)
    hs = []
    dirs = ((q0_ref, k0_ref, v0_ref, g0_ref), (q1_ref, k1_ref, v1_ref, g1_ref))
    for bi, d in [(bi, d) for bi in range(group) for d in range(2)]:
        q_ref, k_ref, v_ref, g_ref = dirs[d]
        incl = (col <= row) if d == 0 else (col >= row)
        last = L - 1 if d == 0 else 0
        g = g_ref[bi]
        bcum = _dot_exact_lhs(incl.astype(BF16), g)
        z = pltpu.roll(g, H, 1) - bcum
        cmax = z
        shift = 1
        while shift < L:
            if d == 0:
                moved = jnp.where(trow >= shift, pltpu.roll(cmax, shift, 0), -jnp.inf)
            else:
                moved = jnp.where(trow < L - shift, pltpu.roll(cmax, L - shift, 0), -jnp.inf)
            cmax = jnp.maximum(cmax, moved)
            shift *= 2
        m_prev = m_ref[bi, d, 0:1, :]
        top = jnp.maximum(cmax, m_prev)
        b_last = bcum[last:last + 1, :]
        lwc = b_last + z
        m_new = jnp.maximum(b_last + m_prev, jnp.max(lwc, axis=0, keepdims=True))
        m_ref[bi, d, 0:1, :] = m_new
        alpha_w = _dot_exact_rhs(-top, spread, terms=2)
        floor_w = jnp.exp(-_dot_exact_rhs(bcum + top, spread, terms=2))
        wts_w = _dot(jnp.exp(lwc - m_new).astype(BF16), spread)
        rows_w = _dot_exact_rhs(jnp.concatenate(
            [jnp.broadcast_to(m_prev, (SUBLANE, LANE)),
             jnp.broadcast_to(jnp.exp(b_last + m_prev - m_new), (SUBLANE, LANE))], axis=0), spread)
        z_t = z.T
        q, k, v = q_ref[bi], k_ref[bi], v_ref[bi]
        for h in range(H):
            slab = slice(h // 2 * LANE, (h // 2 + 1) * LANE)
            cols = slice(h * L, (h + 1) * L)
            mine = low if h % 2 == 0 else jnp.logical_not(low)
            kh = jnp.where(mine, k[:, slab], 0.0)
            hs.append(dict(
                qh=jnp.where(mine, q[:, slab], 0.0).astype(BF16), kh=kh.astype(BF16),
                vext=jnp.where(mine, v[:, slab], 1.0).astype(BF16),
                decay=jnp.exp(jnp.where(incl, alpha_w[:, cols] + z_t[H + h:H + h + 1, :], -jnp.inf)),
                w_inter=jnp.exp(alpha_w[:, cols] + rows_w[0:1, cols]), floor=floor_w[:, cols],
                wk=(wts_w[:, cols] * kh).astype(BF16), dec=rows_w[SUBLANE:SUBLANE + 1, cols],
                cst=c_ref[bi, d, h]))
    for p in hs:
        p["sc"] = (_dot_nt(p["qh"], p["kh"]) * p["decay"]).astype(BF16)
    for p in hs:
        p["numext"] = _dot(p["sc"], p["vext"]) + p["w_inter"] * _dot(p["qh"], p["cst"].astype(BF16))
    for p in hs:
        p["upd"] = _dot_tn(p["wk"], p["vext"])
    for bi, d in [(bi, d) for bi in range(group) for d in range(2)]:
        o_ref = (o0_ref, o1_ref)[d]
        res = []
        for h in range(H):
            p = hs[(bi * 2 + d) * H + h]
            den = pltpu.roll(p["numext"], HEAD_DIM, 1)
            res.append(p["numext"] / jnp.maximum(jnp.abs(den), p["floor"]))
            c_ref[bi, d, h] = p["dec"] * p["cst"] + p["upd"]
        for pair in range(H // 2):
            o_ref[bi, :, pair * LANE:(pair + 1) * LANE] = jnp.where(low, res[2 * pair], res[2 * pair + 1])


def _mlstm_scan(q, k, pm, gates, batch, seq):
    t, mw = q.shape
    n_heads = mw // HEAD_DIM
    L = MLSTM_CHUNK
    nc = seq // L
    group = math.gcd(batch, 4)
    q3, k3, pm3 = (x.reshape(batch, seq, x.shape[1]) for x in (q, k, pm))
    g4 = gates.reshape(2, batch, seq, LANE)
    specs = []
    for d, blk in enumerate((lambda c: c, lambda c: nc - 1 - c)):
        specs += [pl.BlockSpec((group, L, mw), lambda gi, c, blk=blk: (gi, blk(c), 0)),
                  pl.BlockSpec((group, L, mw), lambda gi, c, blk=blk: (gi, blk(c), 0)),
                  pl.BlockSpec((group, L, mw), lambda gi, c, blk=blk: (gi, blk(c), 2)),
                  pl.BlockSpec((None, group, L, LANE), lambda gi, c, blk=blk, d=d: (d, gi, blk(c), 0))]
    out0, out1 = pl.pallas_call(
        functools.partial(_mlstm_scan_body, n_heads=n_heads, group=group),
        grid=(batch // group, nc),
        in_specs=specs,
        out_specs=[pl.BlockSpec((group, L, mw), lambda gi, c: (gi, c, 0)),
                   pl.BlockSpec((group, L, mw), lambda gi, c: (gi, nc - 1 - c, 0))],
        out_shape=[jax.ShapeDtypeStruct((batch, seq, mw), F32)] * 2,
        scratch_shapes=[pltpu.VMEM((group, 2, n_heads, LANE, LANE), F32),
                        pltpu.VMEM((group, 2, SUBLANE, LANE), F32)],
        compiler_params=_cparams(2),
        name="mlstm_scan",
    )(q3, k3, pm3, g4, q3, k3, pm3, g4)
    return out0.reshape(t, mw), out1.reshape(t, mw)


def _layer_norm(x, g, b):
    mu = jnp.mean(x, axis=-1, keepdims=True)
    xc = x - mu
    var = jnp.mean(xc * xc, axis=-1, keepdims=True)
    return xc * lax.rsqrt(var + LN_EPS) * g + b


def _head_norm(x, bd_mean, eps):
    mu = _dot_exact_rhs(x, bd_mean, terms=2)
    xc = x - mu
    var = _dot((xc * xc).astype(BF16), bd_mean)
    return xc * lax.rsqrt(var + eps)


def _mix_out_body(x_ref, yg_ref, ro0_ref, ro1_ref, bonus_ref, rgate_ref, rlg_ref, rlb_ref, mh0_ref, mh1_ref, og_ref,
                  mlg_ref, w_ref, l1g_ref, l1b_ref, rw_ref, rb_ref, bdm_ref,
                  x1_out, x1p_out, topi_out, gate_out, wb_ref, *, alpha, gw, rw):
    @pl.when(pl.program_id(0) == 0)
    def _():
        _cast_rows(w_ref, wb_ref)

    bdm = bdm_ref[...]
    yr = _head_norm(ro0_ref[...] + ro1_ref[...], bdm, RWKV_GN_EPS) * rlg_ref[...] + rlb_ref[...]
    yr = (yr + bonus_ref[...]) * rgate_ref[...]
    ym = _sigmoid(og_ref[...]) * (_head_norm(mh0_ref[...] + mh1_ref[...], bdm, LN_EPS) * mlg_ref[...])
    mix = (_dot(yg_ref[...].astype(BF16), wb_ref[:gw, :]) + _dot(yr.astype(BF16), wb_ref[gw:gw + rw, :])
           + _dot(ym.astype(BF16), wb_ref[gw + rw:, :]))
    x1 = _layer_norm(alpha * x_ref[...] + mix, l1g_ref[...], l1b_ref[...])
    x1_out[...] = x1
    x1p_out[...] = _pack_bf16_pairs(x1)
    lg = _mm(x1, rw_ref[...], "b3") + rb_ref[...]
    lane = lax.broadcasted_iota(jnp.int32, lg.shape, 1)
    vals, topi = [], jnp.zeros(lg.shape, jnp.int32)
    for j in range(TOP_K):
        mx = jnp.max(lg, axis=1, keepdims=True)
        idx = jnp.min(jnp.where(lg == mx, lane, LANE), axis=1, keepdims=True)
        vals.append(mx)
        topi = jnp.where(lane == j, idx, topi)
        lg = jnp.where(lane == idx, -jnp.inf, lg)
    es = [jnp.exp(vj - vals[0]) for vj in vals]
    den = es[0] + es[1] + es[2] + es[3]
    gate = jnp.zeros(lg.shape, F32)
    for j in range(TOP_K):
        gate = jnp.where(lane == j, es[j] / den, gate)
    topi_out[...] = topi.T[:SUBLANE, :]
    gate_out[...] = gate


def _mix_out(x, yg, ro, bonus, rgate, rlg, rlb, mh, pm, mlg, w_out, layer, l1g, l1b, router_w, router_b, alpha,
             tm=512):
    t, dm = x.shape
    gw, rw, mw = yg.shape[1], bonus.shape[1], mh[0].shape[1]
    assert rw == mw
    rwp = jnp.zeros((dm, LANE), F32).at[:, :N_EXPERTS].set(router_w)
    rbp = jnp.full((1, LANE), NEG_BIG, F32).at[0, :N_EXPERTS].set(router_b)
    row = lambda n: pl.BlockSpec((tm, n), lambda i: (i, 0))
    vec = lambda n: _full((1, n))
    return pl.pallas_call(
        functools.partial(_mix_out_body, alpha=alpha, gw=gw, rw=rw),
        grid=(t // tm,),
        in_specs=[row(dm), row(gw), row(rw), row(rw), row(rw), row(rw), vec(rw), vec(rw), row(mw), row(mw),
                  pl.BlockSpec((tm, mw), lambda i: (i, 3)),
                  vec(mw),
                  pl.BlockSpec((None, dm, dm), lambda i: (layer, 0, 0), pipeline_mode=pl.Buffered(1)),
                  vec(dm), vec(dm), _full((dm, LANE)), vec(LANE), _full((rw, rw))],
        out_specs=[row(dm), row(dm // 2), pl.BlockSpec((SUBLANE, tm), lambda i: (0, i)), row(LANE)],
        out_shape=[jax.ShapeDtypeStruct((t, dm), F32), jax.ShapeDtypeStruct((t, dm // 2), jnp.uint32),
                   jax.ShapeDtypeStruct((SUBLANE, t), jnp.int32), jax.ShapeDtypeStruct((t, LANE), F32)],
        scratch_shapes=[pltpu.VMEM((dm, dm), BF16)],
        compiler_params=_cparams(1),
        name="mix_out",
    )(x, yg, ro[0], ro[1], bonus, rgate, rlg.reshape(1, rw), rlb.reshape(1, rw), mh[0], mh[1], pm,
      mlg.reshape(1, mw), w_out, l1g.reshape(1, dm), l1b.reshape(1, dm), rwp, rbp,
      (_block_diag_ones(rw) / HEAD_DIM).astype(BF16))


def _moe_body(be_ref, nu_ref, ve_ref, xs_ref, w1_ref, b1_ref, w2_ref, b2_ref, o_ref, *, dff):
    i = pl.program_id(0)
    active = i < nu_ref[0]

    @pl.when(active)
    def _():
        rowid = i * MOE_BLOCK + lax.broadcasted_iota(jnp.int32, (MOE_BLOCK, 1), 0)
        lo, hi = _unpack_bf16_pairs(jnp.where(rowid < ve_ref[i], xs_ref[...], jnp.uint32(0)))
        xs = jnp.concatenate([lo.astype(BF16), hi.astype(BF16)], axis=1)
        hdn = _dot(xs, w1_ref[...].astype(BF16)) + b1_ref[...]
        g_ = jnp.minimum(hdn[:, :dff], SWIGLU_LIMIT)
        u_ = jnp.clip(hdn[:, dff:], -SWIGLU_LIMIT, SWIGLU_LIMIT)
        act = (u_ + 1.0) * (g_ * _sigmoid(g_ * SWIGLU_ALPHA))
        o_ref[...] = _pack_bf16_pairs(_dot(act.astype(BF16), w2_ref[...].astype(BF16)) + b2_ref[...])

    @pl.when(jnp.logical_not(active))
    def _():
        o_ref[...] = jnp.zeros_like(o_ref)


def _moe_experts(xs, block_e, n_used, valid_end, w1, b1, w2, b2, layer):
    rows, half = xs.shape
    nb = rows // MOE_BLOCK
    depth, ne, dm, dff2 = w1.shape
    dff = dff2 // 2
    grid_spec = pltpu.PrefetchScalarGridSpec(
        num_scalar_prefetch=3,
        grid=(nb,),
        in_specs=[pl.BlockSpec((MOE_BLOCK, half), lambda i, be, nu, ve: (i, 0)),
                  pl.BlockSpec((None, None, dm, dff2), lambda i, be, nu, ve: (layer, be[i], 0, 0)),
                  pl.BlockSpec((None, None, 1, dff2), lambda i, be, nu, ve: (layer, be[i], 0, 0)),
                  pl.BlockSpec((None, None, dff, dm), lambda i, be, nu, ve: (layer, be[i], 0, 0)),
                  pl.BlockSpec((None, None, 1, dm), lambda i, be, nu, ve: (layer, be[i], 0, 0))],
        out_specs=pl.BlockSpec((MOE_BLOCK, half), lambda i, be, nu, ve: (i, 0)),
    )
    return pl.pallas_call(
        functools.partial(_moe_body, dff=dff),
        grid_spec=grid_spec,
        out_shape=jax.ShapeDtypeStruct((rows, half), jnp.uint32),
        compiler_params=_cparams(1),
        name="moe_experts",
    )(block_e, n_used, valid_end, xs, w1, b1.reshape(depth, ne, 1, dff2), w2, b2.reshape(depth, ne, 1, dm))


PLAN_TILE = 512
MOE_BLOCK_SHIFT = MOE_BLOCK.bit_length() - 1
assert 1 << MOE_BLOCK_SHIFT == MOE_BLOCK


def _moe_plan_body(e_ref, dest_ref, meta_ref, rank_ref, *, n_tokens, meta_lanes):
    tiles_per_row = n_tokens // PLAN_TILE
    n_tiles = TOP_K * tiles_per_row
    expert = lax.broadcasted_iota(jnp.int32, (N_EXPERTS, PLAN_TILE), 0)
    r_i = lax.broadcasted_iota(jnp.int32, (PLAN_TILE, PLAN_TILE), 0)
    c_i = lax.broadcasted_iota(jnp.int32, (PLAN_TILE, PLAN_TILE), 1)
    earlier = (r_i < c_i).astype(BF16)

    def tile_hits(it):
        j = it // tiles_per_row
        lanes = pl.ds(pl.multiple_of((it % tiles_per_row) * PLAN_TILE, PLAN_TILE), PLAN_TILE)
        return j, lanes, e_ref[pl.ds(j, 1), lanes] == expert

    def rank_step(it, seen):
        j, lanes, hit = tile_hits(it)
        hitf = hit.astype(F32)
        prior = _dot(hit.astype(BF16), earlier) + seen
        rank_ref[pl.ds(j, 1), lanes] = jnp.sum(hitf * prior, axis=0, keepdims=True)
        return seen + jnp.sum(hitf, axis=1, keepdims=True)

    dest_ref[...] = jnp.zeros_like(dest_ref)
    rank_ref[...] = jnp.zeros_like(rank_ref)
    counts = lax.fori_loop(0, n_tiles, rank_step, jnp.zeros((N_EXPERTS, 1), F32))
    padded = ((counts.astype(jnp.int32) + (MOE_BLOCK - 1)) >> MOE_BLOCK_SHIFT) << MOE_BLOCK_SHIFT
    er = lax.broadcasted_iota(jnp.int32, (N_EXPERTS, N_EXPERTS), 0)
    ec = lax.broadcasted_iota(jnp.int32, (N_EXPERTS, N_EXPERTS), 1)
    seg_end = _dot_exact_lhs((ec <= er).astype(BF16),
                             jnp.broadcast_to(padded.astype(F32), (N_EXPERTS, LANE)))[:, 0:1]
    seg_start = seg_end - padded.astype(F32)

    def dest_step(it, carry):
        j, lanes, hit = tile_hits(it)
        base = jnp.sum(jnp.where(hit, seg_start, 0.0), axis=0, keepdims=True)
        dest_ref[pl.ds(j, 1), lanes] = (rank_ref[pl.ds(j, 1), lanes] + base).astype(jnp.int32)
        return carry

    lax.fori_loop(0, n_tiles, dest_step, 0)
    blk_start = (lax.broadcasted_iota(jnp.int32, (N_EXPERTS, meta_lanes), 1) * MOE_BLOCK).astype(F32)
    blk_expert = jnp.minimum(jnp.sum((seg_end <= blk_start).astype(F32), axis=0, keepdims=True), N_EXPERTS - 1.0)
    mine = lax.broadcasted_iota(jnp.int32, (N_EXPERTS, meta_lanes), 0).astype(F32) == blk_expert
    valid_end = jnp.sum(jnp.where(mine, seg_start + counts, 0.0), axis=0, keepdims=True)
    n_used = jnp.broadcast_to(seg_end[N_EXPERTS - 1:N_EXPERTS, :] * (1.0 / MOE_BLOCK), (1, meta_lanes))
    mrow = lax.broadcasted_iota(jnp.int32, (SUBLANE, meta_lanes), 0)
    meta = jnp.where(mrow == 0, blk_expert, jnp.where(mrow == 1, valid_end, jnp.where(mrow == 2, n_used, 0.0)))
    meta_ref[...] = meta.astype(jnp.int32)


def _moe_plan(e_t, n_tokens, n_blocks):
    meta_lanes = -(-n_blocks // LANE) * LANE
    dest, meta = pl.pallas_call(
        functools.partial(_moe_plan_body, n_tokens=n_tokens, meta_lanes=meta_lanes),
        grid=(1,),
        in_specs=[_full((SUBLANE, n_tokens))],
        out_specs=[_full((SUBLANE, n_tokens)), _full((SUBLANE, meta_lanes))],
        out_shape=[jax.ShapeDtypeStruct((SUBLANE, n_tokens), jnp.int32),
                   jax.ShapeDtypeStruct((SUBLANE, meta_lanes), jnp.int32)],
        scratch_shapes=[pltpu.VMEM((SUBLANE, n_tokens), F32)],
        compiler_params=_cparams(1),
        name="moe_plan",
    )(e_t)
    return dest[:TOP_K], meta[0, :n_blocks], meta[1, :n_blocks], meta[2, :1]


def _combine_body(x1_ref, y0_ref, y1_ref, y2_ref, y3_ref, gate_ref, g_ref, b_ref, o_ref, *, alpha):
    gate = gate_ref[...]
    lo, hi = _unpack_bf16_pairs(y0_ref[...])
    lo, hi = gate[:, 0:1] * lo, gate[:, 0:1] * hi
    for j, y_ref in enumerate((y1_ref, y2_ref, y3_ref), start=1):
        lo_j, hi_j = _unpack_bf16_pairs(y_ref[...])
        lo, hi = lo + gate[:, j:j + 1] * lo_j, hi + gate[:, j:j + 1] * hi_j
    ffn = jnp.concatenate([lo, hi], axis=1)
    o_ref[...] = _layer_norm(alpha * x1_ref[...] + ffn, g_ref[...], b_ref[...])


def _combine(x1, yg, gate, ln_g, ln_b, alpha, tm=512):
    t, dm = x1.shape
    n_tiles = t // tm
    expert_rows = lambda j: pl.BlockSpec((tm, dm // 2), lambda i: (i + j * n_tiles, 0))
    return pl.pallas_call(
        functools.partial(_combine_body, alpha=alpha),
        grid=(n_tiles,),
        in_specs=[pl.BlockSpec((tm, dm), lambda i: (i, 0))] + [expert_rows(j) for j in range(TOP_K)]
                 + [pl.BlockSpec((tm, LANE), lambda i: (i, 0)), _full((1, dm)), _full((1, dm))],
        out_specs=pl.BlockSpec((tm, dm), lambda i: (i, 0)),
        out_shape=jax.ShapeDtypeStruct((t, dm), F32),
        compiler_params=_cparams(1),
        name="combine_ln",
    )(x1, yg, yg, yg, yg, gate, ln_g.reshape(1, dm), ln_b.reshape(1, dm))


SC_CORES = 2
SC_SUBCORES = 16
SC_WORKERS = SC_CORES * SC_SUBCORES


def _sc_gather_rows(table, idx, window):
    n = idx.shape[0]
    dim = table.shape[1]
    n_steps = n // (SC_WORKERS * window)
    assert n_steps * window * SC_WORKERS == n and n_steps % 2 == 0 and window % SUBLANE == 0 and window <= LANE
    idx3 = idx.reshape(SC_WORKERS, n_steps, window)
    mesh = plsc.VectorSubcoreMesh(core_axis_name="c", subcore_axis_name="s",
                                  num_cores=SC_CORES, num_subcores=SC_SUBCORES)

    def body(table_hbm, idx_hbm, out_hbm, idx_v, rows_v, gsem, wsem):
        wid = lax.axis_index("s") * SC_CORES + lax.axis_index("c")
        pltpu.sync_copy(idx_hbm.at[wid], idx_v)

        def gather(j, buf):
            return pltpu.make_async_copy(table_hbm.at[idx_v.at[j]], rows_v.at[buf], gsem.at[buf])

        def write(j, buf):
            base = pl.multiple_of((wid * n_steps + j) * window, window)
            return pltpu.make_async_copy(rows_v.at[buf], out_hbm.at[pl.ds(base, window)], wsem.at[buf])

        gather(0, 0).start()

        @pl.loop(0, n_steps, step=2)
        def _(j0):
            for buf in range(2):
                j = j0 + buf
                gather(j, buf).wait()

                @pl.when(j >= 1)
                def _():
                    write(j - 1, 1 - buf).wait()

                @pl.when(j + 1 < n_steps)
                def _():
                    gather(j + 1, 1 - buf).start()

                write(j, buf).start()

        write(n_steps - 1, 1).wait()

    return pl.kernel(
        body, out_type=jax.ShapeDtypeStruct((n, dim), table.dtype), mesh=mesh,
        scratch_types=[pltpu.VMEM((n_steps, window), jnp.int32), pltpu.VMEM((2, window, dim), table.dtype),
                       pltpu.SemaphoreType.DMA((2,)), pltpu.SemaphoreType.DMA((2,))],
        name="sc_gather",
    )(table, idx3)


def _sc_scatter_rows(src, dest, n_out, window):
    t, dim = src.shape
    k = dest.shape[0]
    n_steps = t // (SC_WORKERS * window)
    assert n_steps * window * SC_WORKERS == t and n_steps % 2 == 0 and window % SUBLANE == 0 and window <= LANE
    idx3 = dest.reshape(k, SC_WORKERS, n_steps, window).transpose(1, 2, 0, 3).reshape(SC_WORKERS, n_steps * k, window)
    mesh = plsc.VectorSubcoreMesh(core_axis_name="c", subcore_axis_name="s",
                                  num_cores=SC_CORES, num_subcores=SC_SUBCORES)

    def body(src_hbm, idx_hbm, out_hbm, idx_v, rows_v, rsem, ssem):
        wid = lax.axis_index("s") * SC_CORES + lax.axis_index("c")
        pltpu.sync_copy(idx_hbm.at[wid], idx_v)

        def read(s, buf):
            base = pl.multiple_of((wid * n_steps + s) * window, window)
            return pltpu.make_async_copy(src_hbm.at[pl.ds(base, window)], rows_v.at[buf], rsem.at[buf])

        def scatter(s, j, buf):
            return pltpu.make_async_copy(rows_v.at[buf], out_hbm.at[idx_v.at[s * k + j]], ssem.at[buf])

        read(0, 0).start()

        @pl.loop(0, n_steps, step=2)
        def _(s0):
            for buf in range(2):
                s = s0 + buf
                read(s, buf).wait()

                @pl.when(s >= 1)
                def _():
                    for j in range(k):
                        scatter(s - 1, j, 1 - buf).wait()

                @pl.when(s + 1 < n_steps)
                def _():
                    read(s + 1, 1 - buf).start()

                for j in range(k):
                    scatter(s, j, buf).start()

        for j in range(k):
            scatter(n_steps - 1, j, 1).wait()

    return pl.kernel(
        body, out_type=jax.ShapeDtypeStruct((n_out, dim), src.dtype), mesh=mesh,
        scratch_types=[pltpu.VMEM((n_steps * k, window), jnp.int32), pltpu.VMEM((2, window, dim), src.dtype),
                       pltpu.SemaphoreType.DMA((2,)), pltpu.SemaphoreType.DMA((2,))],
        name="sc_scatter",
    )(src, idx3)


def kernel(x, w_in, gmlp_ln_g, gmlp_ln_b, gmlp_ws, gmlp_bs, rwkv_mu, rwkv_w0, rwkv_w2, rwkv_a0, rwkv_a2, rwkv_g2, rwkv_k_k, rwkv_k_a, rwkv_r_k, rwkv_ln_g, rwkv_ln_b, mlstm_conv_w, mlstm_conv_b, mlstm_gate_b, mlstm_ln_g, w_out, ln1_g, ln1_b, router_w, router_b, exp_w1, exp_b1, exp_w2, exp_b2, ln2_g, ln2_b):
    batch, seq, dm = x.shape
    depth = w_in.shape[0]
    t = batch * seq
    gw = gmlp_ln_g.shape[1]
    rw = rwkv_w0.shape[2]
    mw = mlstm_ln_g.shape[1]
    g_proj = 2 * gw
    r_proj = 3 * rw + W_LORA + A_LORA + G_LORA
    alpha = (2 * depth) ** 0.25
    n_blocks = -(-t * TOP_K // MOE_BLOCK) + N_EXPERTS
    xf = x.reshape(t, dm)
    for l in range(depth):
        y_g, pr, pm = _proj(xf, w_in, l, g_proj, r_proj, gmlp_ln_g[l], gmlp_ln_b[l], gmlp_ws[l], gmlp_bs[l])
        r, v, a, kd, b, lw, bonus, rgate = _rwkv_prep(
            pr, seq, rwkv_mu[l], rwkv_w0[l], rwkv_w2[l], rwkv_a0[l], rwkv_a2[l], rwkv_g2[l],
            rwkv_k_k[l], rwkv_k_a[l], rwkv_r_k[l].reshape(-1))
        ro = _rwkv_scan(r, v, a, kd, b, lw, batch, seq)
        q, k, gates = _mlstm_prep(pm, seq, mlstm_conv_w[l], mlstm_conv_b[l], mlstm_gate_b[l], mw)
        mh = _mlstm_scan(q, k, pm, gates, batch, seq)
        x1, x1p, topi, gate = _mix_out(xf, y_g, ro, bonus, rgate, rwkv_ln_g[l], rwkv_ln_b[l], mh, pm, mlstm_ln_g[l],
                                       w_out, l, ln1_g[l], ln1_b[l], router_w[l], router_b[l], alpha)
        dest, block_e, valid_end, n_used = _moe_plan(topi, t, n_blocks)
        xs = _sc_scatter_rows(x1p, dest, n_blocks * MOE_BLOCK, window=64)
        ys = _moe_experts(xs, block_e, n_used, valid_end, exp_w1, exp_b1, exp_w2, exp_b2, l)
        yg = _sc_gather_rows(ys, dest.reshape(-1), window=64)
        xf = _combine(x1, yg, gate, ln2_g[l], ln2_b[l], alpha)
    return xf.reshape(batch, seq, dm)
```

```python
import functools
import math

import jax
import jax.numpy as jnp
from jax import lax
from jax.experimental import pallas as pl
from jax.experimental.pallas import tpu as pltpu
from jax.experimental.pallas import tpu_sc as plsc

F32 = jnp.float32
BF16 = jnp.bfloat16

HEAD_DIM = 64
GMLP_CHUNK = 128
MLSTM_CHUNK = 128
RWKV_CHUNK = 64
W_LORA = 64
A_LORA = 64
G_LORA = 128
N_EXPERTS = 32
TOP_K = 4
MOE_BLOCK = 512
SWIGLU_LIMIT = 7.0
SWIGLU_ALPHA = 1.702
LN_EPS = 1e-5
RWKV_GN_EPS = 64e-5
LANE = 128
SUBLANE = 8
VMEM_LIMIT = 48 * 1024 * 1024
NEG_BIG = -1e30


def _cparams(n_axes):
    return pltpu.CompilerParams(dimension_semantics=("arbitrary",) * n_axes,
                                vmem_limit_bytes=VMEM_LIMIT)


def _full(shape):
    return pl.BlockSpec(shape, lambda *_: (0,) * len(shape))


def _dot(a, b, precision=None):
    return jnp.dot(a, b, preferred_element_type=F32, precision=precision)


def _dot_nt(a, b, precision=None):
    return lax.dot_general(a, b, (((1,), (1,)), ((), ())), preferred_element_type=F32, precision=precision)


def _dot_tn(a, b, precision=None):
    return lax.dot_general(a, b, (((0,), (0,)), ((), ())), preferred_element_type=F32, precision=precision)


def _split(x):
    hi = x.astype(BF16)
    return hi, (x - hi.astype(F32)).astype(BF16)


def _split3(x):
    hi = x.astype(BF16)
    r1 = x - hi.astype(F32)
    mid = r1.astype(BF16)
    return hi, mid, (r1 - mid.astype(F32)).astype(BF16)


def _mm(a, b, mode, dot=_dot):
    if mode == "b1":
        return dot(a.astype(BF16), b.astype(BF16))
    bh, bl = _split(b)
    if mode == "b2":
        ah = a.astype(BF16)
        return dot(ah, bh) + dot(ah, bl)
    ah, al = _split(a)
    return dot(ah, bh) + (dot(ah, bl) + dot(al, bh))


def _dot_exact_lhs(a_bf16, x):
    hi, mid, lo = _split3(x)
    return _dot(a_bf16, hi) + (_dot(a_bf16, mid) + _dot(a_bf16, lo))


def _dot_exact_rhs(x, b_bf16, terms=3):
    if terms == 2:
        hi, lo = _split(x)
        return _dot(hi, b_bf16) + _dot(lo, b_bf16)
    hi, mid, lo = _split3(x)
    return _dot(hi, b_bf16) + (_dot(mid, b_bf16) + _dot(lo, b_bf16))


def _pack_bf16_pairs(x):
    n = x.shape[1] // 2
    lo = pltpu.bitcast(x[:, :n].astype(BF16).astype(F32), jnp.uint32)
    hi = pltpu.bitcast(x[:, n:].astype(BF16).astype(F32), jnp.uint32)
    return hi | (lo >> 16)


def _unpack_bf16_pairs(w):
    lo = pltpu.bitcast(w << 16, F32)
    hi = pltpu.bitcast(w & jnp.uint32(0xFFFF0000), F32)
    return lo, hi


def _sigmoid(x):
    return 1.0 / (1.0 + jnp.exp(-x))


def _softplus(x):
    return jnp.maximum(x, 0.0) + jnp.log1p(jnp.exp(-jnp.abs(x)))


def _block_diag_ones(width):
    h = jnp.arange(width) // HEAD_DIM
    return (h[:, None] == h[None, :]).astype(F32)


CAST_ROWS = 128


def _cast_rows(src_ref, dst_ref):
    n_src, n_dst = src_ref.shape[1], dst_ref.shape[1]
    whole = n_src // LANE * LANE

    def step(r, carry):
        rows = pl.ds(pl.multiple_of(r * CAST_ROWS, CAST_ROWS), CAST_ROWS)
        dst_ref[rows, :whole] = src_ref[rows, :whole].astype(BF16)
        if n_dst > whole:
            tail = [src_ref[rows, whole:]] if n_src > whole else []
            tail.append(jnp.zeros((CAST_ROWS, n_dst - n_src), F32))
            dst_ref[rows, whole:] = jnp.concatenate(tail, axis=1).astype(BF16)
        return carry
    lax.fori_loop(0, src_ref.shape[0] // CAST_ROWS, step, 0)


def _gmlp_gate(p, lng_ref, lnb_ref, ws_ref, bst_ref, o_ref):
    gw = p.shape[1] // 2
    p = 0.5 * p * (1.0 + lax.erf(p * math.sqrt(0.5)))
    u, v = p[:, :gw], p[:, gw:]
    mu = jnp.mean(v, axis=-1, keepdims=True)
    vc = v - mu
    var = jnp.mean(vc * vc, axis=-1, keepdims=True)
    vn = vc * lax.rsqrt(var + LN_EPS) * lng_ref[...] + lnb_ref[...]
    for c in range(p.shape[0] // GMLP_CHUNK):
        rows = slice(c * GMLP_CHUNK, (c + 1) * GMLP_CHUNK)
        ys = []
        for h in range(gw // HEAD_DIM):
            cols = slice(h * HEAD_DIM, (h + 1) * HEAD_DIM)
            ys.append(_dot(ws_ref[h], vn[rows, cols].astype(BF16)) + bst_ref[:, h:h + 1])
        o_ref[rows, :] = u[rows, :] * jnp.concatenate(ys, axis=1)


def _proj_body(x_ref, w_ref, lng_ref, lnb_ref, ws_ref, bst_ref, yg_ref, pr_ref, pm_ref, wb_ref, *, ng, nr):
    @pl.when(pl.program_id(0) == 0)
    def _():
        _cast_rows(w_ref, wb_ref)

    xb = x_ref[...].astype(BF16)
    _gmlp_gate(_dot(xb, wb_ref[:, :ng]), lng_ref, lnb_ref, ws_ref, bst_ref, yg_ref)
    pr_ref[...] = _dot(xb, wb_ref[:, ng:ng + nr])
    pm_ref[...] = _dot(xb, wb_ref[:, ng + nr:])


def _proj(x, w_in, layer, ng, nr, ln_g, ln_b, ws, bs, tm=512):
    t, d = x.shape
    p_in = w_in.shape[2]
    p_pad = -(-p_in // LANE) * LANE
    nm = p_pad - ng - nr
    gw = ng // 2
    n_heads = gw // HEAD_DIM
    bst = jnp.zeros((GMLP_CHUNK, LANE), F32).at[:, :n_heads].set(bs.T)
    row = lambda n: pl.BlockSpec((tm, n), lambda i: (i, 0))
    return pl.pallas_call(
        functools.partial(_proj_body, ng=ng, nr=nr),
        grid=(t // tm,),
        in_specs=[row(d), pl.BlockSpec((None, d, p_in), lambda i: (layer, 0, 0), pipeline_mode=pl.Buffered(1)),
                  _full((1, gw)), _full((1, gw)), _full((n_heads, GMLP_CHUNK, GMLP_CHUNK)),
                  _full((GMLP_CHUNK, LANE))],
        out_specs=[row(gw), row(nr), row(nm)],
        out_shape=[jax.ShapeDtypeStruct((t, n), F32) for n in (gw, nr, nm)],
        scratch_shapes=[pltpu.VMEM((d, p_pad), BF16)],
        compiler_params=_cparams(1),
        name="in_proj",
    )(x, w_in, ln_g.reshape(1, gw), ln_b.reshape(1, gw), ws.astype(BF16), bst)


def _halo_specs(tm, width, n_rows):
    per8 = tm // SUBLANE
    last = n_rows // SUBLANE - 1
    prev = pl.BlockSpec((SUBLANE, width), lambda i: (jnp.maximum(i * per8 - 1, 0), 0))
    nxt = pl.BlockSpec((SUBLANE, width), lambda i: (jnp.minimum((i + 1) * per8, last), 0))
    return prev, nxt


def _neighbours(cur, prev_blk, next_blk, tiles_per_seq):
    tm = cur.shape[0]
    j = pl.program_id(0) % tiles_per_seq
    prev_row = jnp.where(j > 0, prev_blk[SUBLANE - 1:SUBLANE, :], 0.0)
    next_row = jnp.where(j < tiles_per_seq - 1, next_blk[0:1, :], 0.0)
    ridx = lax.broadcasted_iota(jnp.int32, cur.shape, 0)
    before = jnp.where(ridx == 0, prev_row, pltpu.roll(cur, 1, 0))
    after = jnp.where(ridx == tm - 1, next_row, pltpu.roll(cur, tm - 1, 0))
    return before, after


def _rwkv_prep_body(pr_ref, prev_ref, next_ref, mu_ref, w0_ref, w2_ref, a0_ref, a2_ref, g2_ref,
                    kk_ref, ka_ref, rk_ref, bd_ref,
                    r_out, v_out, a_out, kd_out, b_out, lw_out, bonus_out, gate_out, *, rw, tiles_per_seq):
    pf = pr_ref[...]
    before, after = _neighbours(pf, prev_ref[...], next_ref[...], tiles_per_seq)
    pf = pf + mu_ref[0:1, :] * (before - pf) + mu_ref[1:2, :] * (after - pf)
    o3 = 3 * rw
    r, k, v = pf[:, :rw], pf[:, rw:2 * rw], pf[:, 2 * rw:o3]
    wd = pf[:, o3:o3 + W_LORA]
    ad = pf[:, o3 + W_LORA:o3 + W_LORA + A_LORA]
    gd = pf[:, o3 + W_LORA + A_LORA:]
    bd = bd_ref[...]
    kk = k * kk_ref[...]
    ss = _dot((kk * kk).astype(BF16), bd)
    kk = kk / jnp.maximum(jnp.sqrt(ss), 1e-12)
    twd = jnp.tanh(wd)
    ksum = jnp.zeros_like(k)
    for d in range(2):
        w_log = -_softplus(-(w0_ref[d:d + 1, :] + _mm(twd, w2_ref[d], "b3"))) - 0.5
        lw_out[d] = -jnp.exp(w_log)
        iclr = _sigmoid(a0_ref[d:d + 1, :] + _mm(ad, a2_ref[d], "b3"))
        kd = k * (1.0 + (iclr - 1.0) * ka_ref[...])
        kd_out[d] = kd
        b_out[d] = kk * iclr
        ksum = ksum + kd
    r_out[...] = r
    v_out[...] = v
    a_out[...] = -kk
    bonus_out[...] = _dot((r * ksum * rk_ref[...]).astype(BF16), bd) * v
    gate_out[...] = _dot(_sigmoid(gd).astype(BF16), g2_ref[...])


def _rwkv_prep(pr, seq, mu, w0, w2, a0, a2, g2, k_k, k_a, r_k, tm=512):
    t, rproj = pr.shape
    rw = w0.shape[1]
    tiles_per_seq = seq // tm
    prev, nxt = _halo_specs(tm, rproj, t)
    row = pl.BlockSpec((tm, rw), lambda i: (i, 0))
    row2 = pl.BlockSpec((2, tm, rw), lambda i: (0, i, 0))
    one = jax.ShapeDtypeStruct((t, rw), F32)
    two = jax.ShapeDtypeStruct((2, t, rw), F32)
    return pl.pallas_call(
        functools.partial(_rwkv_prep_body, rw=rw, tiles_per_seq=tiles_per_seq),
        grid=(t // tm,),
        in_specs=[pl.BlockSpec((tm, rproj), lambda i: (i, 0)), prev, nxt,
                  _full((2, rproj)), _full((2, rw)), _full((2, W_LORA, rw)), _full((2, rw)),
                  _full((2, A_LORA, rw)), _full((G_LORA, rw)), _full((1, rw)), _full((1, rw)),
                  _full((1, rw)), _full((rw, rw))],
        out_specs=[row, row, row, row2, row2, row2, row, row],
        out_shape=[one, one, one, two, two, two, one, one],
        compiler_params=_cparams(1),
        name="rwkv_prep",
    )(pr, pr, pr, mu, w0, w2, a0, a2, g2.astype(BF16), k_k.reshape(1, rw), k_a.reshape(1, rw),
      r_k.reshape(1, rw), _block_diag_ones(rw).astype(BF16))


P_G, P_INV, P_APPLY, P_STATE, P_SEQ = "b1", "b1", "b1", "b1", "b2"


def _rwkv_intra_body(r_ref, v_ref, a_ref, kd_ref, b_ref, lw_ref, rq_out, o0_out, mtx_out, hc_out,
                     *, n_heads, chunks):
    L = RWKV_CHUNK
    d = pl.program_id(0)
    row = lax.broadcasted_iota(jnp.int32, (L, L), 0)
    col = lax.broadcasted_iota(jnp.int32, (L, L), 1)
    fwd = d == 0
    rel = (col - row) * (1 - 2 * d)
    incl = rel <= 0
    strict = rel < 0
    eye = (row == col).astype(F32)
    tri = incl.astype(BF16)
    pairs = []
    for c in range(chunks):
        rows = slice(c * L, (c + 1) * L)
        lw = lw_ref[rows, :]
        cum = _dot_exact_lhs(tri, lw)
        tot = jnp.where(fwd, cum[L - 1:L, :], cum[0:1, :])
        e_neg = jnp.exp(-cum)
        e_end = jnp.exp(tot - cum)
        e_tot = jnp.exp(tot)
        r, v, a, kd, b = r_ref[rows, :], v_ref[rows, :], a_ref[rows, :], kd_ref[rows, :], b_ref[rows, :]
        at, rt, bt, kt = a * jnp.exp(cum - lw), r * jnp.exp(cum), b * e_neg, kd * e_neg
        kend, bend = kd * e_end, b * e_end
        for h in range(n_heads):
            sl = slice(h * HEAD_DIM, (h + 1) * HEAD_DIM)
            pairs.append(dict(at=at[:, sl], rt=rt[:, sl], bt=bt[:, sl], kt=kt[:, sl], v=v[:, sl],
                              kend=kend[:, sl], bend=bend[:, sl], e_tot=e_tot[:, sl]))
    for p in pairs:
        p["g"] = _mm(jnp.concatenate([p["at"], p["rt"]], axis=0),
                     jnp.concatenate([p["bt"], p["kt"]], axis=0), P_G, _dot_nt)
    row2 = lax.broadcasted_iota(jnp.int32, (L, 2 * L), 0)
    col2 = lax.broadcasted_iota(jnp.int32, (L, 2 * L), 1) & (L - 1)
    rel2 = (col2 - row2) * (1 - 2 * d)
    incl2 = rel2 <= 0
    strict2 = rel2 < 0
    zeros = jnp.zeros((L, HEAD_DIM), F32)
    for p in pairs:
        g = p.pop("g")
        a_both = jnp.where(strict2, g[:L, :], 0.0)
        p["m_both"] = jnp.where(incl2, g[L:, :], 0.0)
        p["pw"] = a_both[:, :L]
        p["a_ak"] = a_both[:, L:]
        p["inv"] = eye + p["pw"]
    for _ in range(int(math.log2(L)) - 1):
        for p in pairs:
            p["pw"] = _mm(p["pw"], p["pw"], P_INV)
        for p in pairs:
            p["inv"] = p["inv"] + _mm(p["inv"], p["pw"], P_INV)
    for p in pairs:
        p["akv"] = _mm(p["a_ak"], p["v"], P_APPLY)
    for p in pairs:
        wu = _mm(p["inv"], jnp.concatenate([p["at"], p["akv"]], axis=1), P_APPLY)
        p["rhs"] = jnp.concatenate([wu, jnp.concatenate([zeros, p["v"]], axis=1)], axis=0)
    for p in pairs:
        p["rq_o0"] = _mm(p["m_both"], p["rhs"], P_APPLY)
    for p in pairs:
        p["m_hc"] = _mm(jnp.concatenate([p["bend"], p["kend"]], axis=0), p["rhs"], P_STATE, _dot_tn)
    for c in range(chunks):
        ps = pairs[c * n_heads:(c + 1) * n_heads]
        rows = slice(c * L, (c + 1) * L)
        krows = slice(c * HEAD_DIM, (c + 1) * HEAD_DIM)
        rq_out[rows, :] = jnp.concatenate([p["rt"] + p["rq_o0"][:, :HEAD_DIM] for p in ps], axis=1)
        o0_out[rows, :] = jnp.concatenate([p["rq_o0"][:, HEAD_DIM:] for p in ps], axis=1)
        mtx_out[krows, :] = jnp.concatenate([eye * p["e_tot"] + p["m_hc"][:, :HEAD_DIM] for p in ps], axis=1)
        hc_out[krows, :] = jnp.concatenate([p["m_hc"][:, HEAD_DIM:] for p in ps], axis=1)


def _rwkv_intra(r, v, a, kd, b, lw, chunks=8):
    t, rw = r.shape
    n_heads = rw // HEAD_DIM
    tm = chunks * RWKV_CHUNK
    tk = chunks * HEAD_DIM
    n_tiles = t // tm
    one = pl.BlockSpec((tm, rw), lambda d, i: (i, 0))
    two = pl.BlockSpec((None, tm, rw), lambda d, i: (d, i, 0))
    twok = pl.BlockSpec((None, tk, rw), lambda d, i: (d, i, 0))
    return pl.pallas_call(
        functools.partial(_rwkv_intra_body, n_heads=n_heads, chunks=chunks),
        grid=(2, n_tiles),
        in_specs=[one, one, one, two, two, two],
        out_specs=[two, two, twok, twok],
        out_shape=[jax.ShapeDtypeStruct((2, t, rw), F32), jax.ShapeDtypeStruct((2, t, rw), F32),
                   jax.ShapeDtypeStruct((2, n_tiles * tk, rw), F32),
                   jax.ShapeDtypeStruct((2, n_tiles * tk, rw), F32)],
        compiler_params=_cparams(2),
        name="rwkv_intra",
    )(r, v, a, kd, b, lw)


def _rwkv_seq_body(rq0, o00, mtx0, hc0, rq1, o01, mtx1, hc1, out0, out1, h_ref, *, n_heads, batch, per_step):
    @pl.when(pl.program_id(0) == 0)
    def _():
        h_ref[...] = jnp.zeros_like(h_ref)

    L = RWKV_CHUNK
    dirs = ((rq0, o00, mtx0, hc0, out0), (rq1, o01, mtx1, hc1, out1))
    states = {(d, bi): h_ref[d, bi] for d in range(2) for bi in range(batch)}
    for step in range(per_step):
        for d, (rq, o0, mtx, hc, out) in enumerate(dirs):
            sub = step if d == 0 else per_step - 1 - step
            rows = slice(sub * L, (sub + 1) * L)
            krows = slice(sub * HEAD_DIM, (sub + 1) * HEAD_DIM)
            for bi in range(batch):
                rq_t, mtx_t, state = rq[bi, rows, :], mtx[bi, krows, :], states[(d, bi)]
                outs, new = [], []
                for h in range(n_heads):
                    sl = slice(h * HEAD_DIM, (h + 1) * HEAD_DIM)
                    prod = _mm(jnp.concatenate([rq_t[:, sl], mtx_t[:, sl]], axis=0), state[:, sl], P_SEQ)
                    outs.append(prod[:L])
                    new.append(prod[L:])
                out[bi, rows, :] = jnp.concatenate(outs, axis=1) + o0[bi, rows, :]
                states[(d, bi)] = jnp.concatenate(new, axis=1) + hc[bi, krows, :]
    for (d, bi), state in states.items():
        h_ref[d, bi] = state


def _rwkv_seq(rq, o0, mtx, hc, batch, seq, per_step=4):
    _, t, rw = rq.shape
    n_heads = rw // HEAD_DIM
    L = RWKV_CHUNK * per_step
    lk = HEAD_DIM * per_step
    nc = seq // L
    as4 = lambda x: x.reshape(2, batch, x.shape[1] // batch, rw)
    rq, o0, mtx, hc = as4(rq), as4(o0), as4(mtx), as4(hc)
    fwd = lambda rows: pl.BlockSpec((None, batch, rows, rw), lambda c: (0, 0, c, 0))
    bwd = lambda rows: pl.BlockSpec((None, batch, rows, rw), lambda c: (1, 0, nc - 1 - c, 0))
    out0, out1 = pl.pallas_call(
        functools.partial(_rwkv_seq_body, n_heads=n_heads, batch=batch, per_step=per_step),
        grid=(nc,),
        in_specs=[fwd(L), fwd(L), fwd(lk), fwd(lk), bwd(L), bwd(L), bwd(lk), bwd(lk)],
        out_specs=[pl.BlockSpec((batch, L, rw), lambda c: (0, c, 0)),
                   pl.BlockSpec((batch, L, rw), lambda c: (0, nc - 1 - c, 0))],
        out_shape=[jax.ShapeDtypeStruct((batch, seq, rw), F32)] * 2,
        scratch_shapes=[pltpu.VMEM((2, batch, HEAD_DIM, rw), F32)],
        compiler_params=_cparams(1),
        name="rwkv_seq",
    )(rq, o0, mtx, hc, rq, o0, mtx, hc)
    return out0.reshape(t, rw), out1.reshape(t, rw)


def _rwkv_scan(r, v, a, kd, b, lw, batch, seq):
    rq, o0, mtx, hc = _rwkv_intra(r, v, a, kd, b, lw)
    return _rwkv_seq(rq, o0, mtx, hc, batch, seq)


def _mlstm_prep_body(qk_ref, prev_ref, next_ref, g_ref, cw_ref, cb_ref, gb_ref, q_out, k_out, gate_out,
                     *, mw, n_heads, tiles_per_seq):
    x = qk_ref[...]
    before, after = _neighbours(x, prev_ref[...], next_ref[...], tiles_per_seq)
    y = cb_ref[...] + before * cw_ref[0:1, :] + x * cw_ref[1:2, :] + after * cw_ref[2:3, :]
    y = y * _sigmoid(y)
    q_out[...] = y[:, :mw]
    k_out[...] = y[:, mw:] * (HEAD_DIM ** -0.5)
    g = g_ref[...] + gb_ref[...]
    lane = lax.broadcasted_iota(jnp.int32, g.shape, 1)
    for d in range(2):
        ig = g if d == 0 else pltpu.roll(g, LANE - n_heads, 1)
        fg = pltpu.roll(g, LANE - (1 + d) * n_heads, 1)
        lf = -_softplus(-fg)
        gate_out[d] = jnp.where(lane < n_heads, ig, jnp.where(lane < 2 * n_heads, lf, 0.0))


def _mlstm_prep(pm, seq, conv_w, conv_b, gate_b, mw, tm=512):
    t = pm.shape[0]
    n_heads = mw // HEAD_DIM
    tiles_per_seq = seq // tm
    w2 = 2 * mw
    prev, nxt = _halo_specs(tm, w2, t)
    gcol = (4 * mw) // LANE
    gb = jnp.zeros((1, LANE), F32).at[0, :4 * n_heads].set(gate_b)
    row = pl.BlockSpec((tm, mw), lambda i: (i, 0))
    return pl.pallas_call(
        functools.partial(_mlstm_prep_body, mw=mw, n_heads=n_heads, tiles_per_seq=tiles_per_seq),
        grid=(t // tm,),
        in_specs=[pl.BlockSpec((tm, w2), lambda i: (i, 0)), prev, nxt,
                  pl.BlockSpec((tm, LANE), lambda i: (i, gcol)),
                  _full((3, w2)), _full((1, w2)), _full((1, LANE))],
        out_specs=[row, row, pl.BlockSpec((2, tm, LANE), lambda i: (0, i, 0))],
        out_shape=[jax.ShapeDtypeStruct((t, mw), F32), jax.ShapeDtypeStruct((t, mw), F32),
                   jax.ShapeDtypeStruct((2, t, LANE), F32)],
        compiler_params=_cparams(1),
        name="mlstm_prep",
    )(pm, pm, pm, pm, conv_w, conv_b.reshape(1, w2), gb)


def _mlstm_scan_body(q0_ref, k0_ref, v0_ref, g0_ref, q1_ref, k1_ref, v1_ref, g1_ref, o0_ref, o1_ref,
                     c_ref, m_ref, *, n_heads, group):
    L = MLSTM_CHUNK
    H = n_heads

    @pl.when(pl.program_id(1) == 0)
    def _():
        c_ref[...] = jnp.zeros_like(c_ref)
        m_ref[...] = jnp.zeros_like(m_ref)

    row = lax.broadcasted_iota(jnp.int32, (L, L), 0)
    col = lax.broadcasted_iota(jnp.int32, (L, L), 1)
    trow = lax.broadcasted_iota(jnp.int32, (L, LANE), 0)
    low = lax.broadcasted_iota(jnp.int32, (L, LANE), 1) < HEAD_DIM
    xr = lax.broadcasted_iota(jnp.int32, (LANE, H * L), 0)
    xc = lax.broadcasted_iota(jnp.int32, (LANE, H * L), 1)
    spread = (xr - H == lax.shift_right_logical(xc, int(math.log2(L)))).astype(BF16)
    hs = []
    dirs = ((q0_ref, k0_ref, v0_ref, g0_ref), (q1_ref, k1_ref, v1_ref, g1_ref))
    for bi, d in [(bi, d) for bi in range(group) for d in range(2)]:
        q_ref, k_ref, v_ref, g_ref = dirs[d]
        incl = (col <= row) if d == 0 else (col >= row)
        last = L - 1 if d == 0 else 0
        g = g_ref[bi]
        bcum = _dot_exact_lhs(incl.astype(BF16), g)
        z = pltpu.roll(g, H, 1) - bcum
        cmax = z
        shift = 1
        while shift < L:
            if d == 0:
                moved = jnp.where(trow >= shift, pltpu.roll(cmax, shift, 0), -jnp.inf)
            else:
                moved = jnp.where(trow < L - shift, pltpu.roll(cmax, L - shift, 0), -jnp.inf)
            cmax = jnp.maximum(cmax, moved)
            shift *= 2
        m_prev = m_ref[bi, d, 0:1, :]
        top = jnp.maximum(cmax, m_prev)
        b_last = bcum[last:last + 1, :]
        lwc = b_last + z
        m_new = jnp.maximum(b_last + m_prev, jnp.max(lwc, axis=0, keepdims=True))
        m_ref[bi, d, 0:1, :] = m_new
        alpha_w = _dot_exact_rhs(-top, spread, terms=2)
        floor_w = jnp.exp(-_dot_exact_rhs(bcum + top, spread, terms=2))
        wts_w = _dot(jnp.exp(lwc - m_new).astype(BF16), spread)
        rows_w = _dot_exact_rhs(jnp.concatenate(
            [jnp.broadcast_to(m_prev, (SUBLANE, LANE)),
             jnp.broadcast_to(jnp.exp(b_last + m_prev - m_new), (SUBLANE, LANE))], axis=0), spread)
        z_t = z.T
        q, k, v = q_ref[bi], k_ref[bi], v_ref[bi]
        for h in range(H):
            slab = slice(h // 2 * LANE, (h // 2 + 1) * LANE)
            cols = slice(h * L, (h + 1) * L)
            mine = low if h % 2 == 0 else jnp.logical_not(low)
            kh = jnp.where(mine, k[:, slab], 0.0)
            hs.append(dict(
                qh=jnp.where(mine, q[:, slab], 0.0).astype(BF16), kh=kh.astype(BF16),
                vext=jnp.where(mine, v[:, slab], 1.0).astype(BF16),
                decay=jnp.exp(jnp.where(incl, alpha_w[:, cols] + z_t[H + h:H + h + 1, :], -jnp.inf)),
                w_inter=jnp.exp(alpha_w[:, cols] + rows_w[0:1, cols]), floor=floor_w[:, cols],
                wk=(wts_w[:, cols] * kh).astype(BF16), dec=rows_w[SUBLANE:SUBLANE + 1, cols],
                cst=c_ref[bi, d, h]))
    for p in hs:
        p["sc"] = (_dot_nt(p["qh"], p["kh"]) * p["decay"]).astype(BF16)
    for p in hs:
        p["numext"] = _dot(p["sc"], p["vext"]) + p["w_inter"] * _dot(p["qh"], p["cst"].astype(BF16))
    for p in hs:
        p["upd"] = _dot_tn(p["wk"], p["vext"])
    for bi, d in [(bi, d) for bi in range(group) for d in range(2)]:
        o_ref = (o0_ref, o1_ref)[d]
        res = []
        for h in range(H):
            p = hs[(bi * 2 + d) * H + h]
            den = pltpu.roll(p["numext"], HEAD_DIM, 1)
            res.append(p["numext"] / jnp.maximum(jnp.abs(den), p["floor"]))
            c_ref[bi, d, h] = p["dec"] * p["cst"] + p["upd"]
        for pair in range(H // 2):
            o_ref[bi, :, pair * LANE:(pair + 1) * LANE] = jnp.where(low, res[2 * pair], res[2 * pair + 1])


def _mlstm_scan(q, k, pm, gates, batch, seq):
    t, mw = q.shape
    n_heads = mw // HEAD_DIM
    L = MLSTM_CHUNK
    nc = seq // L
    group = math.gcd(batch, 4)
    q3, k3, pm3 = (x.reshape(batch, seq, x.shape[1]) for x in (q, k, pm))
    g4 = gates.reshape(2, batch, seq, LANE)
    specs = []
    for d, blk in enumerate((lambda c: c, lambda c: nc - 1 - c)):
        specs += [pl.BlockSpec((group, L, mw), lambda gi, c, blk=blk: (gi, blk(c), 0)),
                  pl.BlockSpec((group, L, mw), lambda gi, c, blk=blk: (gi, blk(c), 0)),
                  pl.BlockSpec((group, L, mw), lambda gi, c, blk=blk: (gi, blk(c), 2)),
                  pl.BlockSpec((None, group, L, LANE), lambda gi, c, blk=blk, d=d: (d, gi, blk(c), 0))]
    out0, out1 = pl.pallas_call(
        functools.partial(_mlstm_scan_body, n_heads=n_heads, group=group),
        grid=(batch // group, nc),
        in_specs=specs,
        out_specs=[pl.BlockSpec((group, L, mw), lambda gi, c: (gi, c, 0)),
                   pl.BlockSpec((group, L, mw), lambda gi, c: (gi, nc - 1 - c, 0))],
        out_shape=[jax.ShapeDtypeStruct((batch, seq, mw), F32)] * 2,
        scratch_shapes=[pltpu.VMEM((group, 2, n_heads, LANE, LANE), F32),
                        pltpu.VMEM((group, 2, SUBLANE, LANE), F32)],
        compiler_params=_cparams(2),
        name="mlstm_scan",
    )(q3, k3, pm3, g4, q3, k3, pm3, g4)
    return out0.reshape(t, mw), out1.reshape(t, mw)


def _layer_norm(x, g, b):
    mu = jnp.mean(x, axis=-1, keepdims=True)
    xc = x - mu
    var = jnp.mean(xc * xc, axis=-1, keepdims=True)
    return xc * lax.rsqrt(var + LN_EPS) * g + b


def _head_norm(x, bd_mean, eps):
    mu = _dot_exact_rhs(x, bd_mean, terms=2)
    xc = x - mu
    var = _dot((xc * xc).astype(BF16), bd_mean)
    return xc * lax.rsqrt(var + eps)


def _mix_out_body(x_ref, yg_ref, ro0_ref, ro1_ref, bonus_ref, rgate_ref, rlg_ref, rlb_ref, mh0_ref, mh1_ref, og_ref,
                  mlg_ref, w_ref, l1g_ref, l1b_ref, rw_ref, rb_ref, bdm_ref,
                  x1_out, x1p_out, topi_out, gate_out, wb_ref, *, alpha, gw, rw):
    @pl.when(pl.program_id(0) == 0)
    def _():
        _cast_rows(w_ref, wb_ref)

    bdm = bdm_ref[...]
    yr = _head_norm(ro0_ref[...] + ro1_ref[...], bdm, RWKV_GN_EPS) * rlg_ref[...] + rlb_ref[...]
    yr = (yr + bonus_ref[...]) * rgate_ref[...]
    ym = _sigmoid(og_ref[...]) * (_head_norm(mh0_ref[...] + mh1_ref[...], bdm, LN_EPS) * mlg_ref[...])
    mix = (_dot(yg_ref[...].astype(BF16), wb_ref[:gw, :]) + _dot(yr.astype(BF16), wb_ref[gw:gw + rw, :])
           + _dot(ym.astype(BF16), wb_ref[gw + rw:, :]))
    x1 = _layer_norm(alpha * x_ref[...] + mix, l1g_ref[...], l1b_ref[...])
    x1_out[...] = x1
    x1p_out[...] = _pack_bf16_pairs(x1)
    lg = _mm(x1, rw_ref[...], "b3") + rb_ref[...]
    lane = lax.broadcasted_iota(jnp.int32, lg.shape, 1)
    vals, topi = [], jnp.zeros(lg.shape, jnp.int32)
    for j in range(TOP_K):
        mx = jnp.max(lg, axis=1, keepdims=True)
        idx = jnp.min(jnp.where(lg == mx, lane, LANE), axis=1, keepdims=True)
        vals.append(mx)
        topi = jnp.where(lane == j, idx, topi)
        lg = jnp.where(lane == idx, -jnp.inf, lg)
    es = [jnp.exp(vj - vals[0]) for vj in vals]
    den = es[0] + es[1] + es[2] + es[3]
    gate = jnp.zeros(lg.shape, F32)
    for j in range(TOP_K):
        gate = jnp.where(lane == j, es[j] / den, gate)
    topi_out[...] = topi.T[:SUBLANE, :]
    gate_out[...] = gate


def _mix_out(x, yg, ro, bonus, rgate, rlg, rlb, mh, pm, mlg, w_out, layer, l1g, l1b, router_w, router_b, alpha,
             tm=512):
    t, dm = x.shape
    gw, rw, mw = yg.shape[1], bonus.shape[1], mh[0].shape[1]
    assert rw == mw
    rwp = jnp.zeros((dm, LANE), F32).at[:, :N_EXPERTS].set(router_w)
    rbp = jnp.full((1, LANE), NEG_BIG, F32).at[0, :N_EXPERTS].set(router_b)
    row = lambda n: pl.BlockSpec((tm, n), lambda i: (i, 0))
    vec = lambda n: _full((1, n))
    return pl.pallas_call(
        functools.partial(_mix_out_body, alpha=alpha, gw=gw, rw=rw),
        grid=(t // tm,),
        in_specs=[row(dm), row(gw), row(rw), row(rw), row(rw), row(rw), vec(rw), vec(rw), row(mw), row(mw),
                  pl.BlockSpec((tm, mw), lambda i: (i, 3)),
                  vec(mw),
                  pl.BlockSpec((None, dm, dm), lambda i: (layer, 0, 0), pipeline_mode=pl.Buffered(1)),
                  vec(dm), vec(dm), _full((dm, LANE)), vec(LANE), _full((rw, rw))],
        out_specs=[row(dm), row(dm // 2), pl.BlockSpec((SUBLANE, tm), lambda i: (0, i)), row(LANE)],
        out_shape=[jax.ShapeDtypeStruct((t, dm), F32), jax.ShapeDtypeStruct((t, dm // 2), jnp.uint32),
                   jax.ShapeDtypeStruct((SUBLANE, t), jnp.int32), jax.ShapeDtypeStruct((t, LANE), F32)],
        scratch_shapes=[pltpu.VMEM((dm, dm), BF16)],
        compiler_params=_cparams(1),
        name="mix_out",
    )(x, yg, ro[0], ro[1], bonus, rgate, rlg.reshape(1, rw), rlb.reshape(1, rw), mh[0], mh[1], pm,
      mlg.reshape(1, mw), w_out, l1g.reshape(1, dm), l1b.reshape(1, dm), rwp, rbp,
      (_block_diag_ones(rw) / HEAD_DIM).astype(BF16))


def _moe_body(be_ref, nu_ref, ve_ref, xs_ref, w1_ref, b1_ref, w2_ref, b2_ref, o_ref, *, dff):
    i = pl.program_id(0)
    active = i < nu_ref[0]

    @pl.when(active)
    def _():
        rowid = i * MOE_BLOCK + lax.broadcasted_iota(jnp.int32, (MOE_BLOCK, 1), 0)
        lo, hi = _unpack_bf16_pairs(jnp.where(rowid < ve_ref[i], xs_ref[...], jnp.uint32(0)))
        xs = jnp.concatenate([lo.astype(BF16), hi.astype(BF16)], axis=1)
        hdn = _dot(xs, w1_ref[...].astype(BF16)) + b1_ref[...]
        g_ = jnp.minimum(hdn[:, :dff], SWIGLU_LIMIT)
        u_ = jnp.clip(hdn[:, dff:], -SWIGLU_LIMIT, SWIGLU_LIMIT)
        act = (u_ + 1.0) * (g_ * _sigmoid(g_ * SWIGLU_ALPHA))
        o_ref[...] = _pack_bf16_pairs(_dot(act.astype(BF16), w2_ref[...].astype(BF16)) + b2_ref[...])

    @pl.when(jnp.logical_not(active))
    def _():
        o_ref[...] = jnp.zeros_like(o_ref)


def _moe_experts(xs, block_e, n_used, valid_end, w1, b1, w2, b2, layer):
    rows, half = xs.shape
    nb = rows // MOE_BLOCK
    depth, ne, dm, dff2 = w1.shape
    dff = dff2 // 2
    grid_spec = pltpu.PrefetchScalarGridSpec(
        num_scalar_prefetch=3,
        grid=(nb,),
        in_specs=[pl.BlockSpec((MOE_BLOCK, half), lambda i, be, nu, ve: (i, 0)),
                  pl.BlockSpec((None, None, dm, dff2), lambda i, be, nu, ve: (layer, be[i], 0, 0)),
                  pl.BlockSpec((None, None, 1, dff2), lambda i, be, nu, ve: (layer, be[i], 0, 0)),
                  pl.BlockSpec((None, None, dff, dm), lambda i, be, nu, ve: (layer, be[i], 0, 0)),
                  pl.BlockSpec((None, None, 1, dm), lambda i, be, nu, ve: (layer, be[i], 0, 0))],
        out_specs=pl.BlockSpec((MOE_BLOCK, half), lambda i, be, nu, ve: (i, 0)),
    )
    return pl.pallas_call(
        functools.partial(_moe_body, dff=dff),
        grid_spec=grid_spec,
        out_shape=jax.ShapeDtypeStruct((rows, half), jnp.uint32),
        compiler_params=_cparams(1),
        name="moe_experts",
    )(block_e, n_used, valid_end, xs, w1, b1.reshape(depth, ne, 1, dff2), w2, b2.reshape(depth, ne, 1, dm))


PLAN_TILE = 512
MOE_BLOCK_SHIFT = MOE_BLOCK.bit_length() - 1
assert 1 << MOE_BLOCK_SHIFT == MOE_BLOCK


def _moe_plan_body(e_ref, dest_ref, meta_ref, rank_ref, *, n_tokens, meta_lanes):
    tiles_per_row = n_tokens // PLAN_TILE
    n_tiles = TOP_K * tiles_per_row
    expert = lax.broadcasted_iota(jnp.int32, (N_EXPERTS, PLAN_TILE), 0)
    r_i = lax.broadcasted_iota(jnp.int32, (PLAN_TILE, PLAN_TILE), 0)
    c_i = lax.broadcasted_iota(jnp.int32, (PLAN_TILE, PLAN_TILE), 1)
    earlier = (r_i < c_i).astype(BF16)

    def tile_hits(it):
        j = it // tiles_per_row
        lanes = pl.ds(pl.multiple_of((it % tiles_per_row) * PLAN_TILE, PLAN_TILE), PLAN_TILE)
        return j, lanes, e_ref[pl.ds(j, 1), lanes] == expert

    def rank_step(it, seen):
        j, lanes, hit = tile_hits(it)
        hitf = hit.astype(F32)
        prior = _dot(hit.astype(BF16), earlier) + seen
        rank_ref[pl.ds(j, 1), lanes] = jnp.sum(hitf * prior, axis=0, keepdims=True)
        return seen + jnp.sum(hitf, axis=1, keepdims=True)

    dest_ref[...] = jnp.zeros_like(dest_ref)
    rank_ref[...] = jnp.zeros_like(rank_ref)
    counts = lax.fori_loop(0, n_tiles, rank_step, jnp.zeros((N_EXPERTS, 1), F32))
    padded = ((counts.astype(jnp.int32) + (MOE_BLOCK - 1)) >> MOE_BLOCK_SHIFT) << MOE_BLOCK_SHIFT
    er = lax.broadcasted_iota(jnp.int32, (N_EXPERTS, N_EXPERTS), 0)
    ec = lax.broadcasted_iota(jnp.int32, (N_EXPERTS, N_EXPERTS), 1)
    seg_end = _dot_exact_lhs((ec <= er).astype(BF16),
                             jnp.broadcast_to(padded.astype(F32), (N_EXPERTS, LANE)))[:, 0:1]
    seg_start = seg_end - padded.astype(F32)

    def dest_step(it, carry):
        j, lanes, hit = tile_hits(it)
        base = jnp.sum(jnp.where(hit, seg_start, 0.0), axis=0, keepdims=True)
        dest_ref[pl.ds(j, 1), lanes] = (rank_ref[pl.ds(j, 1), lanes] + base).astype(jnp.int32)
        return carry

    lax.fori_loop(0, n_tiles, dest_step, 0)
    blk_start = (lax.broadcasted_iota(jnp.int32, (N_EXPERTS, meta_lanes), 1) * MOE_BLOCK).astype(F32)
    blk_expert = jnp.minimum(jnp.sum((seg_end <= blk_start).astype(F32), axis=0, keepdims=True), N_EXPERTS - 1.0)
    mine = lax.broadcasted_iota(jnp.int32, (N_EXPERTS, meta_lanes), 0).astype(F32) == blk_expert
    valid_end = jnp.sum(jnp.where(mine, seg_start + counts, 0.0), axis=0, keepdims=True)
    n_used = jnp.broadcast_to(seg_end[N_EXPERTS - 1:N_EXPERTS, :] * (1.0 / MOE_BLOCK), (1, meta_lanes))
    mrow = lax.broadcasted_iota(jnp.int32, (SUBLANE, meta_lanes), 0)
    meta = jnp.where(mrow == 0, blk_expert, jnp.where(mrow == 1, valid_end, jnp.where(mrow == 2, n_used, 0.0)))
    meta_ref[...] = meta.astype(jnp.int32)


def _moe_plan(e_t, n_tokens, n_blocks):
    meta_lanes = -(-n_blocks // LANE) * LANE
    dest, meta = pl.pallas_call(
        functools.partial(_moe_plan_body, n_tokens=n_tokens, meta_lanes=meta_lanes),
        grid=(1,),
        in_specs=[_full((SUBLANE, n_tokens))],
        out_specs=[_full((SUBLANE, n_tokens)), _full((SUBLANE, meta_lanes))],
        out_shape=[jax.ShapeDtypeStruct((SUBLANE, n_tokens), jnp.int32),
                   jax.ShapeDtypeStruct((SUBLANE, meta_lanes), jnp.int32)],
        scratch_shapes=[pltpu.VMEM((SUBLANE, n_tokens), F32)],
        compiler_params=_cparams(1),
        name="moe_plan",
    )(e_t)
    return dest[:TOP_K], meta[0, :n_blocks], meta[1, :n_blocks], meta[2, :1]


def _combine_body(x1_ref, y0_ref, y1_ref, y2_ref, y3_ref, gate_ref, g_ref, b_ref, o_ref, *, alpha):
    gate = gate_ref[...]
    lo, hi = _unpack_bf16_pairs(y0_ref[...])
    lo, hi = gate[:, 0:1] * lo, gate[:, 0:1] * hi
    for j, y_ref in enumerate((y1_ref, y2_ref, y3_ref), start=1):
        lo_j, hi_j = _unpack_bf16_pairs(y_ref[...])
        lo, hi = lo + gate[:, j:j + 1] * lo_j, hi + gate[:, j:j + 1] * hi_j
    ffn = jnp.concatenate([lo, hi], axis=1)
    o_ref[...] = _layer_norm(alpha * x1_ref[...] + ffn, g_ref[...], b_ref[...])


def _combine(x1, yg, gate, ln_g, ln_b, alpha, tm=512):
    t, dm = x1.shape
    n_tiles = t // tm
    expert_rows = lambda j: pl.BlockSpec((tm, dm // 2), lambda i: (i + j * n_tiles, 0))
    return pl.pallas_call(
        functools.partial(_combine_body, alpha=alpha),
        grid=(n_tiles,),
        in_specs=[pl.BlockSpec((tm, dm), lambda i: (i, 0))] + [expert_rows(j) for j in range(TOP_K)]
                 + [pl.BlockSpec((tm, LANE), lambda i: (i, 0)), _full((1, dm)), _full((1, dm))],
        out_specs=pl.BlockSpec((tm, dm), lambda i: (i, 0)),
        out_shape=jax.ShapeDtypeStruct((t, dm), F32),
        compiler_params=_cparams(1),
        name="combine_ln",
    )(x1, yg, yg, yg, yg, gate, ln_g.reshape(1, dm), ln_b.reshape(1, dm))


SC_CORES = 2
SC_SUBCORES = 16
SC_WORKERS = SC_CORES * SC_SUBCORES


def _sc_gather_rows(table, idx, window):
    n = idx.shape[0]
    dim = table.shape[1]
    n_steps = n // (SC_WORKERS * window)
    assert n_steps * window * SC_WORKERS == n and n_steps % 2 == 0 and window % SUBLANE == 0 and window <= LANE
    idx3 = idx.reshape(SC_WORKERS, n_steps, window)
    mesh = plsc.VectorSubcoreMesh(core_axis_name="c", subcore_axis_name="s",
                                  num_cores=SC_CORES, num_subcores=SC_SUBCORES)

    def body(table_hbm, idx_hbm, out_hbm, idx_v, rows_v, gsem, wsem):
        wid = lax.axis_index("s") * SC_CORES + lax.axis_index("c")
        pltpu.sync_copy(idx_hbm.at[wid], idx_v)

        def gather(j, buf):
            return pltpu.make_async_copy(table_hbm.at[idx_v.at[j]], rows_v.at[buf], gsem.at[buf])

        def write(j, buf):
            base = pl.multiple_of((wid * n_steps + j) * window, window)
            return pltpu.make_async_copy(rows_v.at[buf], out_hbm.at[pl.ds(base, window)], wsem.at[buf])

        gather(0, 0).start()

        @pl.loop(0, n_steps, step=2)
        def _(j0):
            for buf in range(2):
                j = j0 + buf
                gather(j, buf).wait()

                @pl.when(j >= 1)
                def _():
                    write(j - 1, 1 - buf).wait()

                @pl.when(j + 1 < n_steps)
                def _():
                    gather(j + 1, 1 - buf).start()

                write(j, buf).start()

        write(n_steps - 1, 1).wait()

    return pl.kernel(
        body, out_type=jax.ShapeDtypeStruct((n, dim), table.dtype), mesh=mesh,
        scratch_types=[pltpu.VMEM((n_steps, window), jnp.int32), pltpu.VMEM((2, window, dim), table.dtype),
                       pltpu.SemaphoreType.DMA((2,)), pltpu.SemaphoreType.DMA((2,))],
        name="sc_gather",
    )(table, idx3)


def _sc_scatter_rows(src, dest, n_out, window):
    t, dim = src.shape
    k = dest.shape[0]
    n_steps = t // (SC_WORKERS * window)
    assert n_steps * window * SC_WORKERS == t and n_steps % 2 == 0 and window % SUBLANE == 0 and window <= LANE
    idx3 = dest.reshape(k, SC_WORKERS, n_steps, window).transpose(1, 2, 0, 3).reshape(SC_WORKERS, n_steps * k, window)
    mesh = plsc.VectorSubcoreMesh(core_axis_name="c", subcore_axis_name="s",
                                  num_cores=SC_CORES, num_subcores=SC_SUBCORES)

    def body(src_hbm, idx_hbm, out_hbm, idx_v, rows_v, rsem, ssem):
        wid = lax.axis_index("s") * SC_CORES + lax.axis_index("c")
        pltpu.sync_copy(idx_hbm.at[wid], idx_v)

        def read(s, buf):
            base = pl.multiple_of((wid * n_steps + s) * window, window)
            return pltpu.make_async_copy(src_hbm.at[pl.ds(base, window)], rows_v.at[buf], rsem.at[buf])

        def scatter(s, j, buf):
            return pltpu.make_async_copy(rows_v.at[buf], out_hbm.at[idx_v.at[s * k + j]], ssem.at[buf])

        read(0, 0).start()

        @pl.loop(0, n_steps, step=2)
        def _(s0):
            for buf in range(2):
                s = s0 + buf
                read(s, buf).wait()

                @pl.when(s >= 1)
                def _():
                    for j in range(k):
                        scatter(s - 1, j, 1 - buf).wait()

                @pl.when(s + 1 < n_steps)
                def _():
                    read(s + 1, 1 - buf).start()

                for j in range(k):
                    scatter(s, j, buf).start()

        for j in range(k):
            scatter(n_steps - 1, j, 1).wait()

    return pl.kernel(
        body, out_type=jax.ShapeDtypeStruct((n_out, dim), src.dtype), mesh=mesh,
        scratch_types=[pltpu.VMEM((n_steps * k, window), jnp.int32), pltpu.VMEM((2, window, dim), src.dtype),
                       pltpu.SemaphoreType.DMA((2,)), pltpu.SemaphoreType.DMA((2,))],
        name="sc_scatter",
    )(src, idx3)


def kernel(x, w_in, gmlp_ln_g, gmlp_ln_b, gmlp_ws, gmlp_bs, rwkv_mu, rwkv_w0, rwkv_w2, rwkv_a0, rwkv_a2, rwkv_g2, rwkv_k_k, rwkv_k_a, rwkv_r_k, rwkv_ln_g, rwkv_ln_b, mlstm_conv_w, mlstm_conv_b, mlstm_gate_b, mlstm_ln_g, w_out, ln1_g, ln1_b, router_w, router_b, exp_w1, exp_b1, exp_w2, exp_b2, ln2_g, ln2_b):
    batch, seq, dm = x.shape
    depth = w_in.shape[0]
    t = batch * seq
    gw = gmlp_ln_g.shape[1]
    rw = rwkv_w0.shape[2]
    mw = mlstm_ln_g.shape[1]
    g_proj = 2 * gw
    r_proj = 3 * rw + W_LORA + A_LORA + G_LORA
    alpha = (2 * depth) ** 0.25
    n_blocks = -(-t * TOP_K // MOE_BLOCK) + N_EXPERTS
    xf = x.reshape(t, dm)
    for l in range(depth):
        y_g, pr, pm = _proj(xf, w_in, l, g_proj, r_proj, gmlp_ln_g[l], gmlp_ln_b[l], gmlp_ws[l], gmlp_bs[l])
        r, v, a, kd, b, lw, bonus, rgate = _rwkv_prep(
            pr, seq, rwkv_mu[l], rwkv_w0[l], rwkv_w2[l], rwkv_a0[l], rwkv_a2[l], rwkv_g2[l],
            rwkv_k_k[l], rwkv_k_a[l], rwkv_r_k[l].reshape(-1))
        ro = _rwkv_scan(r, v, a, kd, b, lw, batch, seq)
        q, k, gates = _mlstm_prep(pm, seq, mlstm_conv_w[l], mlstm_conv_b[l], mlstm_gate_b[l], mw)
        mh = _mlstm_scan(q, k, pm, gates, batch, seq)
        x1, x1p, topi, gate = _mix_out(xf, y_g, ro, bonus, rgate, rwkv_ln_g[l], rwkv_ln_b[l], mh, pm, mlstm_ln_g[l],
                                       w_out, l, ln1_g[l], ln1_b[l], router_w[l], router_b[l], alpha)
        dest, block_e, valid_end, n_used = _moe_plan(topi, t, n_blocks)
        xs = _sc_scatter_rows(x1p, dest, n_blocks * MOE_BLOCK, window=64)
        ys = _moe_experts(xs, block_e, n_used, valid_end, exp_w1, exp_b1, exp_w2, exp_b2, l)
        yg = _sc_gather_rows(ys, dest.reshape(-1), window=64)
        xf = _combine(x1, yg, gate, ln2_g[l], ln2_b[l], alpha)
    return xf.reshape(batch, seq, dm)
```
